```python
import math
import jax, jax.numpy as jnp
from jax import lax
import numpy as np

D_MODEL = 1024
BATCH = 2
SEQ = 8192
DEPTH = 1

GRID_W = 64
CTX_LEN = 256
EPS = 1e-6
M_HEADS = 4
M_HEAD_DIM = 256
M_WIDTH = M_HEADS * M_HEAD_DIM
M_CHUNK = 64
H_WIDTH = 1024
H_POS_BANDS = 16
H_POS_DIM = 1 + 2 * H_POS_BANDS
H_FILTER_HIDDEN = 64
H_FAST_DECAY_PCT = 0.3
H_SLOW_DECAY_PCT = 1.5
H_DECAY_TARGET = 1e-2
SHORT_CONV = 3
N_GROUPS = 8
EXPERTS_PER_GROUP = 8
N_EXPERTS = N_GROUPS * EXPERTS_PER_GROUP
TOP_K_IN_GROUP = 2
D_EXPERT = 512
MOE_BLOCK = 128
Q0 = 0
K0 = Q0 + M_WIDTH
V0 = K0 + M_WIDTH
O0 = V0 + M_WIDTH
IG0 = O0 + M_WIDTH
FG0 = IG0 + 2 * M_HEADS
M_COLS = FG0 + 2 * M_HEADS
HY0 = M_COLS
GA0 = HY0 + 3 * H_WIDTH
GB0 = GA0 + D_MODEL
IN_COLS = GB0 + D_MODEL

kernel_name = 'hybrid_mlstm_hyena_hmoe_dit_block'


def rmsnorm(x, g):
    xf = x.astype(jnp.float32)
    y = xf * lax.rsqrt(jnp.mean(xf * xf, axis=-1, keepdims=True) + EPS)
    return (y * g.astype(jnp.float32)).astype(x.dtype)


def adaln(cond, w_mod, b_mod):
    m = jax.nn.silu(cond) @ w_mod + b_mod
    return m.reshape(*cond.shape[:-1], 6, D_MODEL)


def modulate(x, g, mod, i):
    return rmsnorm(x, g) * (1 + mod[..., i + 1, :]) + mod[..., i, :]


def short_conv(u, w, b, rows):
    bsz, L, ch = u.shape
    u4 = u.reshape(bsz, rows, L // rows, ch)
    up = jnp.pad(u4, ((0, 0), (0, 0), (1, 1), (0, 0)))
    y = up[:, :, :-2] * w[0] + up[:, :, 1:-1] * w[1] + up[:, :, 2:] * w[2] + b
    return y.reshape(bsz, L, ch)


def mlstm_chunkwise(q, k, v, ig, lf, c0, n0, m0):
    bsz, grp, L, dh = q.shape
    nc = L // M_CHUNK

    def chunks(t):
        return jnp.moveaxis(t.reshape(bsz, grp, nc, M_CHUNK, *t.shape[3:]), 2, 0)

    tril = jnp.tril(jnp.ones((M_CHUNK, M_CHUNK), bool))

    def step(carry, inp):
        cmat, nvec, m = carry
        qc, kc, vc, ic, fc = inp
        b = jnp.cumsum(fc, axis=-1)
        dmat = jnp.where(tril, b[..., :, None] - b[..., None, :] + ic[..., None, :], -jnp.inf)
        inter = b + m[..., None]
        m_t = jnp.maximum(inter, jnp.max(dmat, axis=-1))
        s = jnp.einsum('bgtd,bgsd->bgts', qc, kc) * jnp.exp(dmat - m_t[..., None])
        carry_scale = jnp.exp(inter - m_t)
        num = (jnp.einsum('bgts,bgse->bgte', s, vc)
               + carry_scale[..., None] * jnp.einsum('bged,bgtd->bgte', cmat, qc))
        den = jnp.sum(s, axis=-1) + carry_scale * jnp.einsum('bgd,bgtd->bgt', nvec, qc)
        h = num / jnp.maximum(jnp.abs(den), jnp.exp(-m_t))[..., None]
        b_last = b[..., -1]
        g = b_last[..., None] - b + ic
        m_new = jnp.maximum(b_last + m, jnp.max(g, axis=-1))
        wgt = jnp.exp(g - m_new[..., None])
        decay = jnp.exp(b_last + m - m_new)
        c_new = decay[..., None, None] * cmat + jnp.einsum('bgs,bgse,bgsd->bged', wgt, vc, kc)
        n_new = decay[..., None] * nvec + jnp.einsum('bgs,bgsd->bgd', wgt, kc)
        return (c_new, n_new, m_new), h

    (cf, nf, mf), hs = lax.scan(step, (c0, n0, m0),
                                (chunks(q), chunks(k), chunks(v), chunks(ig), chunks(lf)))
    h = jnp.moveaxis(hs, 0, 2).reshape(bsz, grp, L, dh)
    return h, (cf, nf, mf)


def mlstm_branch(z, w_conv, b_conv, rows, state):
    bsz, L, _ = z.shape
    f32 = jnp.float32
    qk = jax.nn.silu(short_conv(z[..., Q0:V0], w_conv, b_conv, rows))
    q = qk[..., :M_WIDTH]
    k = qk[..., M_WIDTH:] * (M_HEAD_DIM ** -0.5)
    v = z[..., V0:O0]
    o = jax.nn.sigmoid(z[..., O0:IG0])
    ig = z[..., IG0:FG0]
    lf = jax.nn.log_sigmoid(z[..., FG0:M_COLS].astype(f32))

    def heads_dirs(t):
        t = t.astype(f32).reshape(bsz, L, M_HEADS, M_HEAD_DIM)
        t = jnp.stack([t, t[:, ::-1]], axis=1)
        return t.transpose(0, 1, 3, 2, 4).reshape(bsz, 2 * M_HEADS, L, M_HEAD_DIM)

    def gates_dirs(g):
        g = g.astype(f32).reshape(bsz, L, 2, M_HEADS)
        g = jnp.stack([g[:, :, 0], g[:, ::-1, 1]], axis=1)
        return g.transpose(0, 1, 3, 2).reshape(bsz, 2 * M_HEADS, L)

    h, state_out = mlstm_chunkwise(heads_dirs(q), heads_dirs(k), heads_dirs(v),
                                   gates_dirs(ig), gates_dirs(lf), *state)
    h = h.reshape(bsz, 2, M_HEADS, L, M_HEAD_DIM)
    h = h[:, 0] + h[:, 1, :, ::-1]
    h = h.transpose(0, 2, 1, 3).reshape(bsz, L, M_WIDTH)
    return o * h.astype(z.dtype), state_out


def hyena_filter(L, w1, b1, w2, b2, w3, freq):
    f32 = jnp.float32
    pos = jnp.arange(L, dtype=f32)
    t = (pos / max(L - 1, 1))[:, None]
    bands = jnp.linspace(1e-4, H_POS_BANDS - 1, H_POS_BANDS, dtype=f32)
    ang = (2 * math.pi / L) * pos[:, None] * bands[None]
    feats = jnp.concatenate([t, jnp.cos(ang), -jnp.sin(ang)], axis=-1)
    freq = freq.astype(f32)
    hid = jnp.sin(freq * (feats @ w1.astype(f32) + b1.astype(f32)))
    hid = jnp.sin(freq * (hid @ w2.astype(f32) + b2.astype(f32)))
    filt = hid @ w3.astype(f32)
    max_decay = math.log(H_DECAY_TARGET) / H_FAST_DECAY_PCT
    min_decay = math.log(H_DECAY_TARGET) / H_SLOW_DECAY_PCT
    deltas = jnp.linspace(min_decay, max_decay, H_WIDTH, dtype=f32)
    window = jnp.exp(-t * jnp.abs(deltas)[None])
    fwd = filt[:, :H_WIDTH] * window
    bwd = filt[:, H_WIDTH:] * window
    kern = jnp.concatenate([fwd, jnp.zeros((1, H_WIDTH), f32), bwd[:0:-1]], axis=0)
    return kern * lax.rsqrt(jnp.sum(kern * kern, axis=0, keepdims=True) + EPS)


def hyena_branch(z, w_conv, b_conv, f_w1, f_b1, f_w2, f_b2, f_w3, f_freq, h_bias, rows):
    bsz, L, _ = z.shape
    u = short_conv(z[..., HY0:GA0], w_conv, b_conv, rows)
    x0 = u[..., :H_WIDTH]
    x1 = u[..., H_WIDTH:2 * H_WIDTH]
    v = u[..., 2 * H_WIDTH:]
    s = (x1 * v).astype(jnp.float32)
    kern = hyena_filter(L, f_w1, f_b1, f_w2, f_b2, f_w3, f_freq)
    n_fft = 2 * L
    y = jnp.fft.irfft(jnp.fft.rfft(s, n=n_fft, axis=1) * jnp.fft.rfft(kern, n=n_fft, axis=0)[None],
                      n=n_fft, axis=1)[:, :L]
    y = y + h_bias.astype(jnp.float32) * s
    return x0 * y.astype(z.dtype)


def merge_branches(z, a, hy, w_a, w_b, w_out):
    ga = jax.nn.sigmoid(z[..., GA0:GB0])
    gb = jax.nn.sigmoid(z[..., GB0:IN_COLS])
    return (ga * (a @ w_a) + gb * (hy @ w_b)) @ w_out


def hier_moe(h, w_group, b_group, w_router, b_router, w1_e, w3_e, w2_e):
    bsz, L, d = h.shape
    n_tok = bsz * L
    xt = h.reshape(n_tok, d)
    f32 = jnp.float32
    glog = (xt @ w_group).astype(f32) + b_group.astype(f32)
    gprob = jax.nn.softmax(glog, axis=-1)
    gsel = jnp.argmax(glog, axis=-1)
    gw = jnp.take_along_axis(gprob, gsel[:, None], axis=1)
    elog = ((xt @ w_router).astype(f32) + b_router.astype(f32)).reshape(n_tok, N_GROUPS, EXPERTS_PER_GROUP)
    elog = jnp.take_along_axis(elog, gsel[:, None, None], axis=1)[:, 0]
    eprob = jax.nn.softmax(elog, axis=-1)
    topv, topi = lax.top_k(eprob, TOP_K_IN_GROUP)
    weights = gw * topv / jnp.sum(topv, axis=-1, keepdims=True)
    experts = gsel[:, None].astype(jnp.int32) * EXPERTS_PER_GROUP + topi.astype(jnp.int32)

    n_assign = n_tok * TOP_K_IN_GROUP
    e_flat = experts.reshape(n_assign)
    w_flat = weights.reshape(n_assign)
    tok = jnp.repeat(jnp.arange(n_tok, dtype=jnp.int32), TOP_K_IN_GROUP)
    order = jnp.argsort(e_flat)
    e_s, tok_s, w_s = e_flat[order], tok[order], w_flat[order]
    counts = jax.ops.segment_sum(jnp.ones_like(e_flat), e_flat, num_segments=N_EXPERTS)
    starts = jnp.cumsum(counts) - counts
    padded = ((counts + MOE_BLOCK - 1) // MOE_BLOCK) * MOE_BLOCK
    pend = jnp.cumsum(padded)
    pstart = pend - padded
    dest = pstart[e_s] + jnp.arange(n_assign, dtype=jnp.int32) - starts[e_s]
    n_blocks = -(-n_assign // MOE_BLOCK) + N_EXPERTS
    n_slots = n_blocks * MOE_BLOCK
    slot_tok = jnp.zeros((n_slots,), jnp.int32).at[dest].set(tok_s)
    slot_w = jnp.zeros((n_slots,), f32).at[dest].set(w_s)
    block_e = jnp.clip(jnp.searchsorted(pend, jnp.arange(n_blocks, dtype=jnp.int32) * MOE_BLOCK,
                                        side='right'), 0, N_EXPERTS - 1)
    xs = xt[slot_tok].reshape(n_blocks, MOE_BLOCK, d)

    def expert_block(args):
        xb, e = args
        return (jax.nn.silu(xb @ w1_e[e]) * (xb @ w3_e[e])) @ w2_e[e]

    ys = lax.map(expert_block, (xs, block_e)).reshape(n_slots, d)
    y = jnp.zeros((n_tok, d), f32).at[slot_tok].add(ys.astype(f32) * slot_w[:, None])
    return y.astype(h.dtype).reshape(bsz, L, d)


def trunk_layer(x, xc, rows, c, c_ctx, w_mod, b_mod, g_norm1, g_norm2, w_in, b_in, w_qk_conv, b_qk_conv,
                w_h_conv, b_h_conv, hf_w1, hf_b1, hf_w2, hf_b2, hf_w3, hf_freq, h_bias, w_a, w_b, w_out,
                w_group, b_group, w_router, b_router, w1_e, w3_e, w2_e, last):
    mod = adaln(c, w_mod, b_mod)[:, None]
    modc = adaln(c_ctx, w_mod, b_mod)[None, None]
    bsz = xc.shape[0]
    f32 = jnp.float32
    hc = modulate(xc, g_norm1, modc, 0)
    cols = M_COLS if last else IN_COLS
    zc = hc @ w_in[:, :cols] + b_in[:cols]
    zero_state = (jnp.zeros((bsz, 2 * M_HEADS, M_HEAD_DIM, M_HEAD_DIM), f32),
                  jnp.zeros((bsz, 2 * M_HEADS, M_HEAD_DIM), f32),
                  jnp.zeros((bsz, 2 * M_HEADS), f32))
    ac, ctx_state = mlstm_branch(zc, w_qk_conv, b_qk_conv, 1, zero_state)
    h = modulate(x, g_norm1, mod, 0)
    z = h @ w_in + b_in
    a, _ = mlstm_branch(z, w_qk_conv, b_qk_conv, rows, ctx_state)
    hy = hyena_branch(z, w_h_conv, b_h_conv, hf_w1, hf_b1, hf_w2, hf_b2, hf_w3, hf_freq, h_bias, rows)
    x = x + mod[..., 2, :] * merge_branches(z, a, hy, w_a, w_b, w_out)
    h2 = modulate(x, g_norm2, mod, 3)
    x = x + mod[..., 5, :] * hier_moe(h2, w_group, b_group, w_router, b_router, w1_e, w3_e, w2_e)
    if not last:
        hyc = hyena_branch(zc, w_h_conv, b_h_conv, hf_w1, hf_b1, hf_w2, hf_b2, hf_w3, hf_freq, h_bias, 1)
        xc = xc + modc[..., 2, :] * merge_branches(zc, ac, hyc, w_a, w_b, w_out)
        hc2 = modulate(xc, g_norm2, modc, 3)
        xc = xc + modc[..., 5, :] * hier_moe(hc2, w_group, b_group, w_router, b_router, w1_e, w3_e, w2_e)
    return x, xc


def setup_inputs(seed: int = 0) -> dict:
    key = jax.random.key(seed)
    ks = jax.random.split(key, 40)

    def nrm(k, shape, s):
        return jax.random.normal(k, shape, jnp.float32) * s

    forget_bias = jnp.tile(jnp.linspace(3.0, 6.0, M_HEADS, dtype=jnp.float32), 2)
    return {
        'x': nrm(ks[0], (BATCH, SEQ, D_MODEL), 1.0),
        'c': nrm(ks[1], (BATCH, D_MODEL), 1.0),
        'ctx': nrm(ks[2], (BATCH, CTX_LEN, D_MODEL), 1.0),
        'c_ctx': nrm(ks[3], (D_MODEL,), 1.0),
        'w_mod': nrm(ks[4], (DEPTH, D_MODEL, 6 * D_MODEL), 0.5 * D_MODEL ** -0.5),
        'b_mod': nrm(ks[5], (DEPTH, 6 * D_MODEL), 0.02),
        'g_norm1': 1.0 + nrm(ks[6], (DEPTH, D_MODEL), 0.02),
        'g_norm2': 1.0 + nrm(ks[7], (DEPTH, D_MODEL), 0.02),
        'w_in': nrm(ks[8], (DEPTH, D_MODEL, IN_COLS), D_MODEL ** -0.5),
        'b_in': nrm(ks[9], (DEPTH, IN_COLS), 0.02).at[:, FG0:M_COLS].add(forget_bias),
        'w_qk_conv': nrm(ks[10], (DEPTH, SHORT_CONV, 2 * M_WIDTH), SHORT_CONV ** -0.5),
        'b_qk_conv': nrm(ks[11], (DEPTH, 2 * M_WIDTH), 0.02),
        'w_h_conv': nrm(ks[12], (DEPTH, SHORT_CONV, 3 * H_WIDTH), SHORT_CONV ** -0.5),
        'b_h_conv': nrm(ks[13], (DEPTH, 3 * H_WIDTH), 0.02),
        'hf_w1': nrm(ks[14], (DEPTH, H_POS_DIM, H_FILTER_HIDDEN), H_POS_DIM ** -0.5),
        'hf_b1': nrm(ks[15], (DEPTH, H_FILTER_HIDDEN), 0.1),
        'hf_w2': nrm(ks[16], (DEPTH, H_FILTER_HIDDEN, H_FILTER_HIDDEN), H_FILTER_HIDDEN ** -0.5),
        'hf_b2': nrm(ks[17], (DEPTH, H_FILTER_HIDDEN), 0.1),
        'hf_w3': nrm(ks[18], (DEPTH, H_FILTER_HIDDEN, 2 * H_WIDTH), H_FILTER_HIDDEN ** -0.5),
        'hf_freq': 1.0 + nrm(ks[19], (DEPTH, H_FILTER_HIDDEN), 0.02),
        'h_bias': nrm(ks[20], (DEPTH, H_WIDTH), 1.0),
        'w_a': nrm(ks[21], (DEPTH, M_WIDTH, D_MODEL), M_WIDTH ** -0.5),
        'w_b': nrm(ks[22], (DEPTH, H_WIDTH, D_MODEL), H_WIDTH ** -0.5),
        'w_out': nrm(ks[23], (DEPTH, D_MODEL, D_MODEL), D_MODEL ** -0.5),
        'w_group': nrm(ks[24], (DEPTH, D_MODEL, N_GROUPS), D_MODEL ** -0.5),
        'b_group': nrm(ks[25], (DEPTH, N_GROUPS), 0.01),
        'w_router': nrm(ks[26], (DEPTH, D_MODEL, N_EXPERTS), D_MODEL ** -0.5),
        'b_router': nrm(ks[27], (DEPTH, N_EXPERTS), 0.01),
        'w1_e': nrm(ks[28], (DEPTH, N_EXPERTS, D_MODEL, D_EXPERT), D_MODEL ** -0.5),
        'w3_e': nrm(ks[29], (DEPTH, N_EXPERTS, D_MODEL, D_EXPERT), D_MODEL ** -0.5),
        'w2_e': nrm(ks[30], (DEPTH, N_EXPERTS, D_EXPERT, D_MODEL), D_EXPERT ** -0.5),
        'g_final': 1.0 + nrm(ks[31], (D_MODEL,), 0.02),
    }


def reference(x, c, ctx, c_ctx, w_mod, b_mod, g_norm1, g_norm2, w_in, b_in, w_qk_conv, b_qk_conv,
              w_h_conv, b_h_conv, hf_w1, hf_b1, hf_w2, hf_b2, hf_w3, hf_freq, h_bias, w_a, w_b, w_out,
              w_group, b_group, w_router, b_router, w1_e, w3_e, w2_e, g_final):
    rows = x.shape[1] // GRID_W
    xc = ctx
    for l in range(DEPTH):
        x, xc = trunk_layer(x, xc, rows, c, c_ctx, w_mod[l], b_mod[l], g_norm1[l], g_norm2[l], w_in[l],
                            b_in[l], w_qk_conv[l], b_qk_conv[l], w_h_conv[l], b_h_conv[l], hf_w1[l],
                            hf_b1[l], hf_w2[l], hf_b2[l], hf_w3[l], hf_freq[l], h_bias[l], w_a[l], w_b[l],
                            w_out[l], w_group[l], b_group[l], w_router[l], b_router[l], w1_e[l], w3_e[l],
                            w2_e[l], l == DEPTH - 1)
    return rmsnorm(x, g_final)
```

```python
import math
import jax, jax.numpy as jnp
from jax import lax
import numpy as np

D_MODEL = 1024
BATCH = 2
SEQ = 8192
DEPTH = 1

GRID_W = 64
CTX_LEN = 256
EPS = 1e-6
M_HEADS = 4
M_HEAD_DIM = 256
M_WIDTH = M_HEADS * M_HEAD_DIM
M_CHUNK = 64
H_WIDTH = 1024
H_POS_BANDS = 16
H_POS_DIM = 1 + 2 * H_POS_BANDS
H_FILTER_HIDDEN = 64
H_FAST_DECAY_PCT = 0.3
H_SLOW_DECAY_PCT = 1.5
H_DECAY_TARGET = 1e-2
SHORT_CONV = 3
N_GROUPS = 8
EXPERTS_PER_GROUP = 8
N_EXPERTS = N_GROUPS * EXPERTS_PER_GROUP
TOP_K_IN_GROUP = 2
D_EXPERT = 512
MOE_BLOCK = 128
Q0 = 0
K0 = Q0 + M_WIDTH
V0 = K0 + M_WIDTH
O0 = V0 + M_WIDTH
IG0 = O0 + M_WIDTH
FG0 = IG0 + 2 * M_HEADS
M_COLS = FG0 + 2 * M_HEADS
HY0 = M_COLS
GA0 = HY0 + 3 * H_WIDTH
GB0 = GA0 + D_MODEL
IN_COLS = GB0 + D_MODEL


def rmsnorm(x, g):
    xf = x.astype(jnp.float32)
    y = xf * lax.rsqrt(jnp.mean(xf * xf, axis=-1, keepdims=True) + EPS)
    return (y * g.astype(jnp.float32)).astype(x.dtype)


def adaln(cond, w_mod, b_mod):
    m = jax.nn.silu(cond) @ w_mod + b_mod
    return m.reshape(*cond.shape[:-1], 6, D_MODEL)


def modulate(x, g, mod, i):
    return rmsnorm(x, g) * (1 + mod[..., i + 1, :]) + mod[..., i, :]


def short_conv(u, w, b, rows):
    bsz, L, ch = u.shape
    u4 = u.reshape(bsz, rows, L // rows, ch)
    up = jnp.pad(u4, ((0, 0), (0, 0), (1, 1), (0, 0)))
    y = up[:, :, :-2] * w[0] + up[:, :, 1:-1] * w[1] + up[:, :, 2:] * w[2] + b
    return y.reshape(bsz, L, ch)


def mlstm_chunkwise(q, k, v, ig, lf, c0, n0, m0):
    bsz, grp, L, dh = q.shape
    nc = L // M_CHUNK

    def chunks(t):
        return jnp.moveaxis(t.reshape(bsz, grp, nc, M_CHUNK, *t.shape[3:]), 2, 0)

    tril = jnp.tril(jnp.ones((M_CHUNK, M_CHUNK), bool))

    def step(carry, inp):
        cmat, nvec, m = carry
        qc, kc, vc, ic, fc = inp
        b = jnp.cumsum(fc, axis=-1)
        dmat = jnp.where(tril, b[..., :, None] - b[..., None, :] + ic[..., None, :], -jnp.inf)
        inter = b + m[..., None]
        m_t = jnp.maximum(inter, jnp.max(dmat, axis=-1))
        s = jnp.einsum('bgtd,bgsd->bgts', qc, kc) * jnp.exp(dmat - m_t[..., None])
        carry_scale = jnp.exp(inter - m_t)
        num = (jnp.einsum('bgts,bgse->bgte', s, vc)
               + carry_scale[..., None] * jnp.einsum('bged,bgtd->bgte', cmat, qc))
        den = jnp.sum(s, axis=-1) + carry_scale * jnp.einsum('bgd,bgtd->bgt', nvec, qc)
        h = num / jnp.maximum(jnp.abs(den), jnp.exp(-m_t))[..., None]
        b_last = b[..., -1]
        g = b_last[..., None] - b + ic
        m_new = jnp.maximum(b_last + m, jnp.max(g, axis=-1))
        wgt = jnp.exp(g - m_new[..., None])
        decay = jnp.exp(b_last + m - m_new)
        c_new = decay[..., None, None] * cmat + jnp.einsum('bgs,bgse,bgsd->bged', wgt, vc, kc)
        n_new = decay[..., None] * nvec + jnp.einsum('bgs,bgsd->bgd', wgt, kc)
        return (c_new, n_new, m_new), h

    (cf, nf, mf), hs = lax.scan(step, (c0, n0, m0),
                                (chunks(q), chunks(k), chunks(v), chunks(ig), chunks(lf)))
    h = jnp.moveaxis(hs, 0, 2).reshape(bsz, grp, L, dh)
    return h, (cf, nf, mf)


def mlstm_branch(z, w_conv, b_conv, rows, state):
    bsz, L, _ = z.shape
    f32 = jnp.float32
    qk = jax.nn.silu(short_conv(z[..., Q0:V0], w_conv, b_conv, rows))
    q = qk[..., :M_WIDTH]
    k = qk[..., M_WIDTH:] * (M_HEAD_DIM ** -0.5)
    v = z[..., V0:O0]
    o = jax.nn.sigmoid(z[..., O0:IG0])
    ig = z[..., IG0:FG0]
    lf = jax.nn.log_sigmoid(z[..., FG0:M_COLS].astype(f32))

    def heads_dirs(t):
        t = t.astype(f32).reshape(bsz, L, M_HEADS, M_HEAD_DIM)
        t = jnp.stack([t, t[:, ::-1]], axis=1)
        return t.transpose(0, 1, 3, 2, 4).reshape(bsz, 2 * M_HEADS, L, M_HEAD_DIM)

    def gates_dirs(g):
        g = g.astype(f32).reshape(bsz, L, 2, M_HEADS)
        g = jnp.stack([g[:, :, 0], g[:, ::-1, 1]], axis=1)
        return g.transpose(0, 1, 3, 2).reshape(bsz, 2 * M_HEADS, L)

    h, state_out = mlstm_chunkwise(heads_dirs(q), heads_dirs(k), heads_dirs(v),
                                   gates_dirs(ig), gates_dirs(lf), *state)
    h = h.reshape(bsz, 2, M_HEADS, L, M_HEAD_DIM)
    h = h[:, 0] + h[:, 1, :, ::-1]
    h = h.transpose(0, 2, 1, 3).reshape(bsz, L, M_WIDTH)
    return o * h.astype(z.dtype), state_out


def hyena_filter(L, w1, b1, w2, b2, w3, freq):
    f32 = jnp.float32
    pos = jnp.arange(L, dtype=f32)
    t = (pos / max(L - 1, 1))[:, None]
    bands = jnp.linspace(1e-4, H_POS_BANDS - 1, H_POS_BANDS, dtype=f32)
    ang = (2 * math.pi / L) * pos[:, None] * bands[None]
    feats = jnp.concatenate([t, jnp.cos(ang), -jnp.sin(ang)], axis=-1)
    freq = freq.astype(f32)
    hid = jnp.sin(freq * (feats @ w1.astype(f32) + b1.astype(f32)))
    hid = jnp.sin(freq * (hid @ w2.astype(f32) + b2.astype(f32)))
    filt = hid @ w3.astype(f32)
    max_decay = math.log(H_DECAY_TARGET) / H_FAST_DECAY_PCT
    min_decay = math.log(H_DECAY_TARGET) / H_SLOW_DECAY_PCT
    deltas = jnp.linspace(min_decay, max_decay, H_WIDTH, dtype=f32)
    window = jnp.exp(-t * jnp.abs(deltas)[None])
    fwd = filt[:, :H_WIDTH] * window
    bwd = filt[:, H_WIDTH:] * window
    kern = jnp.concatenate([fwd, jnp.zeros((1, H_WIDTH), f32), bwd[:0:-1]], axis=0)
    return kern * lax.rsqrt(jnp.sum(kern * kern, axis=0, keepdims=True) + EPS)


def hyena_branch(z, w_conv, b_conv, f_w1, f_b1, f_w2, f_b2, f_w3, f_freq, h_bias, rows):
    bsz, L, _ = z.shape
    u = short_conv(z[..., HY0:GA0], w_conv, b_conv, rows)
    x0 = u[..., :H_WIDTH]
    x1 = u[..., H_WIDTH:2 * H_WIDTH]
    v = u[..., 2 * H_WIDTH:]
    s = (x1 * v).astype(jnp.float32)
    kern = hyena_filter(L, f_w1, f_b1, f_w2, f_b2, f_w3, f_freq)
    n_fft = 2 * L
    y = jnp.fft.irfft(jnp.fft.rfft(s, n=n_fft, axis=1) * jnp.fft.rfft(kern, n=n_fft, axis=0)[None],
                      n=n_fft, axis=1)[:, :L]
    y = y + h_bias.astype(jnp.float32) * s
    return x0 * y.astype(z.dtype)


def merge_branches(z, a, hy, w_a, w_b, w_out):
    ga = jax.nn.sigmoid(z[..., GA0:GB0])
    gb = jax.nn.sigmoid(z[..., GB0:IN_COLS])
    shp = a.shape
    pa = pallas_matmul(a.reshape(-1, shp[-1]), w_a).reshape(shp[:-1] + (D_MODEL,))
    pb = pallas_matmul(hy.reshape(-1, shp[-1]), w_b).reshape(shp[:-1] + (D_MODEL,))
    mix = ga * pa + gb * pb
    return pallas_matmul(mix.reshape(-1, D_MODEL), w_out).reshape(mix.shape)


def hier_moe(h, w_group, b_group, w_router, b_router, w1_e, w3_e, w2_e):
    bsz, L, d = h.shape
    n_tok = bsz * L
    xt = h.reshape(n_tok, d)
    f32 = jnp.float32
    glog = (xt @ w_group).astype(f32) + b_group.astype(f32)
    gprob = jax.nn.softmax(glog, axis=-1)
    gsel = jnp.argmax(glog, axis=-1)
    gw = jnp.take_along_axis(gprob, gsel[:, None], axis=1)
    elog = ((xt @ w_router).astype(f32) + b_router.astype(f32)).reshape(n_tok, N_GROUPS, EXPERTS_PER_GROUP)
    elog = jnp.take_along_axis(elog, gsel[:, None, None], axis=1)[:, 0]
    eprob = jax.nn.softmax(elog, axis=-1)
    topv, topi = lax.top_k(eprob, TOP_K_IN_GROUP)
    weights = gw * topv / jnp.sum(topv, axis=-1, keepdims=True)
    experts = gsel[:, None].astype(jnp.int32) * EXPERTS_PER_GROUP + topi.astype(jnp.int32)

    n_assign = n_tok * TOP_K_IN_GROUP
    e_flat = experts.reshape(n_assign)
    w_flat = weights.reshape(n_assign)
    tok = jnp.repeat(jnp.arange(n_tok, dtype=jnp.int32), TOP_K_IN_GROUP)
    order = jnp.argsort(e_flat)
    e_s, tok_s, w_s = e_flat[order], tok[order], w_flat[order]
    counts = jax.ops.segment_sum(jnp.ones_like(e_flat), e_flat, num_segments=N_EXPERTS)
    starts = jnp.cumsum(counts) - counts
    padded = ((counts + MOE_BLOCK - 1) // MOE_BLOCK) * MOE_BLOCK
    pend = jnp.cumsum(padded)
    pstart = pend - padded
    dest = pstart[e_s] + jnp.arange(n_assign, dtype=jnp.int32) - starts[e_s]
    n_blocks = -(-n_assign // MOE_BLOCK) + N_EXPERTS
    n_slots = n_blocks * MOE_BLOCK
    slot_tok = jnp.zeros((n_slots,), jnp.int32).at[dest].set(tok_s)
    slot_w = jnp.zeros((n_slots,), f32).at[dest].set(w_s)
    block_e = jnp.clip(jnp.searchsorted(pend, jnp.arange(n_blocks, dtype=jnp.int32) * MOE_BLOCK,
                                        side='right'), 0, N_EXPERTS - 1)
    xs = xt[slot_tok].reshape(n_blocks, MOE_BLOCK, d)

    def expert_block(args):
        xb, e = args
        return (jax.nn.silu(xb @ w1_e[e]) * (xb @ w3_e[e])) @ w2_e[e]

    ys = lax.map(expert_block, (xs, block_e)).reshape(n_slots, d)
    y = jnp.zeros((n_tok, d), f32).at[slot_tok].add(ys.astype(f32) * slot_w[:, None])
    return y.astype(h.dtype).reshape(bsz, L, d)


def trunk_layer(x, xc, rows, c, c_ctx, w_mod, b_mod, g_norm1, g_norm2, w_in, b_in, w_qk_conv, b_qk_conv,
                w_h_conv, b_h_conv, hf_w1, hf_b1, hf_w2, hf_b2, hf_w3, hf_freq, h_bias, w_a, w_b, w_out,
                w_group, b_group, w_router, b_router, w1_e, w3_e, w2_e, last):
    mod = adaln(c, w_mod, b_mod)[:, None]
    modc = adaln(c_ctx, w_mod, b_mod)[None, None]
    bsz = xc.shape[0]
    f32 = jnp.float32
    hc = modulate(xc, g_norm1, modc, 0)
    cols = M_COLS if last else IN_COLS
    zc = hc @ w_in[:, :cols] + b_in[:cols]
    zero_state = (jnp.zeros((bsz, 2 * M_HEADS, M_HEAD_DIM, M_HEAD_DIM), f32),
                  jnp.zeros((bsz, 2 * M_HEADS, M_HEAD_DIM), f32),
                  jnp.zeros((bsz, 2 * M_HEADS), f32))
    ac, ctx_state = mlstm_branch(zc, w_qk_conv, b_qk_conv, 1, zero_state)
    h = modulate(x, g_norm1, mod, 0)
    z = h @ w_in + b_in
    a, _ = mlstm_branch(z, w_qk_conv, b_qk_conv, rows, ctx_state)
    hy = hyena_branch(z, w_h_conv, b_h_conv, hf_w1, hf_b1, hf_w2, hf_b2, hf_w3, hf_freq, h_bias, rows)
    x = x + mod[..., 2, :] * merge_branches(z, a, hy, w_a, w_b, w_out)
    h2 = modulate(x, g_norm2, mod, 3)
    x = x + mod[..., 5, :] * hier_moe(h2, w_group, b_group, w_router, b_router, w1_e, w3_e, w2_e)
    if not last:
        hyc = hyena_branch(zc, w_h_conv, b_h_conv, hf_w1, hf_b1, hf_w2, hf_b2, hf_w3, hf_freq, h_bias, 1)
        xc = xc + modc[..., 2, :] * merge_branches(zc, ac, hyc, w_a, w_b, w_out)
        hc2 = modulate(xc, g_norm2, modc, 3)
        xc = xc + modc[..., 5, :] * hier_moe(hc2, w_group, b_group, w_router, b_router, w1_e, w3_e, w2_e)
    return x, xc


from jax.experimental import pallas as pl
from jax.experimental.pallas import tpu as pltpu


def _mm_kernel(x_ref, w_ref, o_ref):
    o_ref[...] = jnp.dot(x_ref[...].astype(jnp.bfloat16), w_ref[...].astype(jnp.bfloat16),
                         preferred_element_type=jnp.float32)


def pallas_matmul(x, w, tm=512, tn=512):
    m, k = x.shape
    _, n = w.shape
    return pl.pallas_call(
        _mm_kernel,
        grid=(m // tm, n // tn),
        in_specs=[pl.BlockSpec((tm, k), lambda i, j: (i, 0)), pl.BlockSpec((k, tn), lambda i, j: (0, j))],
        out_specs=pl.BlockSpec((tm, tn), lambda i, j: (i, j)),
        out_shape=jax.ShapeDtypeStruct((m, n), jnp.float32),
    )(x, w)

def kernel(x, c, ctx, c_ctx, w_mod, b_mod, g_norm1, g_norm2, w_in, b_in, w_qk_conv, b_qk_conv,
              w_h_conv, b_h_conv, hf_w1, hf_b1, hf_w2, hf_b2, hf_w3, hf_freq, h_bias, w_a, w_b, w_out,
              w_group, b_group, w_router, b_router, w1_e, w3_e, w2_e, g_final):
    rows = x.shape[1] // GRID_W
    xc = ctx
    for l in range(DEPTH):
        x, xc = trunk_layer(x, xc, rows, c, c_ctx, w_mod[l], b_mod[l], g_norm1[l], g_norm2[l], w_in[l],
                            b_in[l], w_qk_conv[l], b_qk_conv[l], w_h_conv[l], b_h_conv[l], hf_w1[l],
                            hf_b1[l], hf_w2[l], hf_b2[l], hf_w3[l], hf_freq[l], h_bias[l], w_a[l], w_b[l],
                            w_out[l], w_group[l], b_group[l], w_router[l], b_router[l], w1_e[l], w3_e[l],
                            w2_e[l], l == DEPTH - 1)
    return rmsnorm(x, g_final)
```

```python
import functools
import math

import jax
import jax.numpy as jnp
import numpy as np
from jax import lax
from jax.experimental import pallas as pl
from jax.experimental.pallas import tpu as pltpu

F32 = jnp.float32
BF16 = jnp.bfloat16

D_MODEL = 1024
GRID_W = 64
EPS = 1e-6
M_HEADS = 4
M_HEAD_DIM = 256
M_WIDTH = M_HEADS * M_HEAD_DIM
H_WIDTH = 1024
H_POS_BANDS = 16
H_FILTER_HIDDEN = 64
H_FAST_DECAY_PCT = 0.3
H_SLOW_DECAY_PCT = 1.5
H_DECAY_TARGET = 1e-2
N_GROUPS = 8
EXPERTS_PER_GROUP = 8
N_EXPERTS = N_GROUPS * EXPERTS_PER_GROUP
D_EXPERT = 512
Q0 = 0
K0 = Q0 + M_WIDTH
V0 = K0 + M_WIDTH
O0 = V0 + M_WIDTH
IG0 = O0 + M_WIDTH
FG0 = IG0 + 2 * M_HEADS
M_COLS = FG0 + 2 * M_HEADS
HY0 = M_COLS
GA0 = HY0 + 3 * H_WIDTH
GB0 = GA0 + D_MODEL
IN_COLS = GB0 + D_MODEL

LANES = 128
MLSTM_CHUNK = 256
NEG_BIG = -1e30
VMEM_LIMIT = 48 * 1024 * 1024


def _cparams(*sem):
    return pltpu.CompilerParams(dimension_semantics=sem, vmem_limit_bytes=VMEM_LIMIT)


def _adaln_kernel(c_ref, w_ref, b_ref, o_ref):
    s = c_ref[...]
    s = s * jax.nn.sigmoid(s)
    o_ref[...] = jnp.dot(s.astype(BF16), w_ref[...].astype(BF16), preferred_element_type=F32) + b_ref[...]


def _adaln(cond, w_mod, b_mod):
    n = w_mod.shape[1]
    tn = 1536
    return pl.pallas_call(
        _adaln_kernel,
        grid=(n // tn,),
        in_specs=[pl.BlockSpec((8, D_MODEL), lambda j: (0, 0)),
                  pl.BlockSpec((D_MODEL, tn), lambda j: (0, j)),
                  pl.BlockSpec((1, tn), lambda j: (0, j))],
        out_specs=pl.BlockSpec((8, tn), lambda j: (0, j)),
        out_shape=jax.ShapeDtypeStruct((8, n), F32),
        compiler_params=_cparams("arbitrary"),
        name="adaln",
    )(cond, w_mod, b_mod.reshape(1, n))


def _norm_mod_kernel(x_ref, g_ref, sh_ref, sc_ref, o_ref):
    x = x_ref[...]
    y = x * lax.rsqrt(jnp.mean(x * x, axis=-1, keepdims=True) + EPS)
    y = y * g_ref[...]
    o_ref[...] = (y * (1.0 + sc_ref[...]) + sh_ref[...]).astype(o_ref.dtype)


def _norm_mod(x, g, shift, scale, tm):
    bsz, L, d = x.shape
    return pl.pallas_call(
        _norm_mod_kernel,
        grid=(bsz, L // tm),
        in_specs=[pl.BlockSpec((None, tm, d), lambda b, i: (b, i, 0)),
                  pl.BlockSpec((1, d), lambda b, i: (0, 0)),
                  pl.BlockSpec((None, 1, d), lambda b, i: (b, 0, 0)),
                  pl.BlockSpec((None, 1, d), lambda b, i: (b, 0, 0))],
        out_specs=pl.BlockSpec((None, tm, d), lambda b, i: (b, i, 0)),
        out_shape=jax.ShapeDtypeStruct((bsz, L, d), BF16),
        compiler_params=_cparams("parallel", "parallel"),
        name="norm_mod",
    )(x, g.reshape(1, d), shift, scale)


def _conv3(z, wc, bc, seg):
    tm = z.shape[0]
    pos = lax.broadcasted_iota(jnp.int32, z.shape, 0) & (seg - 1)
    zp = jnp.where(pos == 0, 0.0, pltpu.roll(z, 1, 0))
    zn = jnp.where(pos == seg - 1, 0.0, pltpu.roll(z, tm - 1, 0))
    return zp * wc[0:1, :] + z * wc[1:2, :] + zn * wc[2:3, :] + bc


def _proj_act_kernel(h_ref, w_ref, b_ref, o_ref, *, act):
    z = jnp.dot(h_ref[...], w_ref[...], preferred_element_type=F32) + b_ref[...]
    if act == "sigmoid":
        z = jax.nn.sigmoid(z)
    o_ref[...] = z.astype(o_ref.dtype)


def _proj_act(h, w, b, act, out_dtype, tm, tn=512):
    bsz, L, d = h.shape
    n = w.shape[1]
    return pl.pallas_call(
        functools.partial(_proj_act_kernel, act=act),
        grid=(bsz, L // tm, n // tn),
        in_specs=[pl.BlockSpec((None, tm, d), lambda b_, i, j: (b_, i, 0)),
                  pl.BlockSpec((d, tn), lambda b_, i, j: (0, j)),
                  pl.BlockSpec((1, tn), lambda b_, i, j: (0, j))],
        out_specs=pl.BlockSpec((None, tm, tn), lambda b_, i, j: (b_, i, j)),
        out_shape=jax.ShapeDtypeStruct((bsz, L, n), out_dtype),
        compiler_params=_cparams("parallel", "parallel", "arbitrary"),
        name="proj_" + act,
    )(h, w, b.reshape(1, n))


def _proj_conv_silu_kernel(h_ref, w_ref, b_ref, wc_ref, bc_ref, cs_ref, o_ref, *, seg):
    z = jnp.dot(h_ref[...], w_ref[...], preferred_element_type=F32) + b_ref[...]
    y = _conv3(z, wc_ref[...], bc_ref[...], seg)
    y = y * jax.nn.sigmoid(y)
    o_ref[...] = (y * cs_ref[...]).astype(o_ref.dtype)


def _proj_conv_silu(h, w, b, wc, bc, colscale, seg, tm, tn=512):
    bsz, L, d = h.shape
    n = w.shape[1]
    col = lambda b_, i, j: (0, j)
    return pl.pallas_call(
        functools.partial(_proj_conv_silu_kernel, seg=seg),
        grid=(bsz, L // tm, n // tn),
        in_specs=[pl.BlockSpec((None, tm, d), lambda b_, i, j: (b_, i, 0)),
                  pl.BlockSpec((d, tn), col),
                  pl.BlockSpec((1, tn), col),
                  pl.BlockSpec((3, tn), col),
                  pl.BlockSpec((1, tn), col),
                  pl.BlockSpec((1, tn), col)],
        out_specs=pl.BlockSpec((None, tm, tn), lambda b_, i, j: (b_, i, j)),
        out_shape=jax.ShapeDtypeStruct((bsz, L, n), BF16),
        compiler_params=_cparams("parallel", "parallel", "arbitrary"),
        name="proj_conv_silu",
    )(h, w, b.reshape(1, n), wc, bc.reshape(1, n), colscale.reshape(1, n))


def _proj_hyena_kernel(h_ref, w0_ref, w1_ref, w2_ref, b_ref, wc_ref, bc_ref, x0_ref, s_ref, *, seg):
    h = h_ref[...]
    us = []
    for g, w_ref in enumerate((w0_ref, w1_ref, w2_ref)):
        z = jnp.dot(h, w_ref[...], preferred_element_type=F32) + b_ref[g]
        us.append(_conv3(z, wc_ref[g], bc_ref[g], seg))
    x0_ref[...] = us[0].astype(x0_ref.dtype)
    s_ref[...] = us[1] * us[2]


def _proj_hyena(h, w, b, wc, bc, seg, tm, tn=512):
    bsz, L, d = h.shape
    nblk = H_WIDTH // tn
    b3 = b.reshape(3, 1, H_WIDTH)
    wc3 = wc.reshape(3, 3, H_WIDTH).transpose(1, 0, 2)
    bc3 = bc.reshape(3, 1, H_WIDTH)
    out_spec = pl.BlockSpec((None, tm, tn), lambda b_, i, j: (b_, i, j))
    return pl.pallas_call(
        functools.partial(_proj_hyena_kernel, seg=seg),
        grid=(bsz, L // tm, nblk),
        in_specs=[pl.BlockSpec((None, tm, d), lambda b_, i, j: (b_, i, 0)),
                  pl.BlockSpec((d, tn), lambda b_, i, j: (0, j)),
                  pl.BlockSpec((d, tn), lambda b_, i, j: (0, nblk + j)),
                  pl.BlockSpec((d, tn), lambda b_, i, j: (0, 2 * nblk + j)),
                  pl.BlockSpec((3, 1, tn), lambda b_, i, j: (0, 0, j)),
                  pl.BlockSpec((3, 3, tn), lambda b_, i, j: (0, 0, j)),
                  pl.BlockSpec((3, 1, tn), lambda b_, i, j: (0, 0, j))],
        out_specs=[out_spec, out_spec],
        out_shape=[jax.ShapeDtypeStruct((bsz, L, H_WIDTH), BF16),
                   jax.ShapeDtypeStruct((bsz, L, H_WIDTH), F32)],
        compiler_params=_cparams("parallel", "parallel", "arbitrary"),
        name="proj_hyena",
    )(h, w, w, w, b3, wc3, bc3)


N_GATES = 4 * M_HEADS


def _split3(x):
    hi = x.astype(BF16)
    r1 = x - hi.astype(F32)
    mid = r1.astype(BF16)
    lo = (r1 - mid.astype(F32)).astype(BF16)
    return hi, mid, lo


def _log_sigmoid(x):
    return jnp.minimum(x, 0.0) - jnp.log1p(jnp.exp(-jnp.abs(x)))


def _gates_kernel(h_ref, w_ref, wt_ref, b_ref, bt_ref, bc_ref, ac_ref, ar_ref):
    h = h_ref[...]
    t = h.shape[0]
    z = jnp.dot(h, w_ref[...], preferred_element_type=F32) + b_ref[...]
    zt = lax.dot_general(wt_ref[...], h, (((1,), (1,)), ((), ())),
                         preferred_element_type=F32) + bt_ref[...]
    r = lax.broadcasted_iota(jnp.int32, (t, t), 0)
    c = lax.broadcasted_iota(jnp.int32, (t, t), 1)
    lower = (r >= c).astype(BF16)
    upper = (r <= c).astype(BF16)

    lf = _log_sigmoid(z)
    parts = _split3(lf)
    cf = sum(jnp.dot(lower, p, preferred_element_type=F32) for p in parts)
    cb = sum(jnp.dot(upper, p, preferred_element_type=F32) for p in parts)
    lane = lax.broadcasted_iota(jnp.int32, z.shape, 1)
    bc = jnp.where(lane < FG_LANE0 + M_HEADS, cf, cb)
    bc = pltpu.roll(bc, LANES - FG_LANE0, 1)
    bc_ref[...] = bc
    ac_ref[...] = z - bc

    lft = _log_sigmoid(zt[FG_LANE0:, :])
    tparts = _split3(lft)
    cft = sum(jnp.dot(p, upper, preferred_element_type=F32) for p in tparts)
    cbt = sum(jnp.dot(p, lower, preferred_element_type=F32) for p in tparts)
    row = lax.broadcasted_iota(jnp.int32, cft.shape, 0)
    ar_ref[...] = zt[:FG_LANE0, :] - jnp.where(row < M_HEADS, cft, cbt)


FG_LANE0 = 2 * M_HEADS


def _gates(h, w_g, b_g, chunk):
    bsz, L, d = h.shape
    w_pad = jnp.zeros((d, LANES), F32).at[:, :N_GATES].set(w_g).astype(BF16)
    b_pad = jnp.zeros((1, LANES), F32).at[0, :N_GATES].set(b_g)
    wt = w_g.T.astype(BF16)
    bt = b_g.reshape(N_GATES, 1)
    tok = pl.BlockSpec((None, chunk, LANES), lambda b_, i: (b_, i, 0))
    return pl.pallas_call(
        _gates_kernel,
        grid=(bsz, L // chunk),
        in_specs=[pl.BlockSpec((None, chunk, d), lambda b_, i: (b_, i, 0)),
                  pl.BlockSpec((d, LANES), lambda b_, i: (0, 0)),
                  pl.BlockSpec((N_GATES, d), lambda b_, i: (0, 0)),
                  pl.BlockSpec((1, LANES), lambda b_, i: (0, 0)),
                  pl.BlockSpec((N_GATES, 1), lambda b_, i: (0, 0))],
        out_specs=[tok, tok, pl.BlockSpec((None, FG_LANE0, chunk), lambda b_, i: (b_, 0, i))],
        out_shape=[jax.ShapeDtypeStruct((bsz, L, LANES), F32),
                   jax.ShapeDtypeStruct((bsz, L, LANES), F32),
                   jax.ShapeDtypeStruct((bsz, FG_LANE0, L), F32)],
        compiler_params=_cparams("parallel", "parallel"),
        name="mlstm_gates",
    )(h, w_pad, wt, b_pad, bt)


def _mlstm_kernel(*refs, emit_h, n_chunks):
    if emit_h:
        (q_ref, k_ref, v_ref, bc_ref, ac_ref, ar_ref, c0_ref, n0_ref, m0_ref,
         h_ref, cf_ref, nf_ref, mf_ref, c_sc, n_sc, m_sc) = refs
    else:
        (k_ref, v_ref, bc_ref, ac_ref, ar_ref, c0_ref, n0_ref, m0_ref,
         cf_ref, nf_ref, mf_ref, c_sc, n_sc, m_sc) = refs
    d = pl.program_id(1)
    j = pl.program_id(2)
    fwd = d == 0
    t = k_ref.shape[0]
    dh = M_HEAD_DIM

    @pl.when(j == 0)
    def _():
        c_sc[...] = c0_ref[...]
        n_sc[...] = n0_ref[...]
        m_sc[...] = m0_ref[...]

    r = lax.broadcasted_iota(jnp.int32, (t, t), 0)
    c = lax.broadcasted_iota(jnp.int32, (t, t), 1)
    causal = jnp.where(fwd, r - c, c - r) >= 0
    bc_all = bc_ref[...]
    ac_all = ac_ref[...]
    ar_all = ar_ref[...]
    for hd in range(M_HEADS):
        sl = slice(hd * dh, (hd + 1) * dh)
        bc = jnp.where(fwd, bc_all[:, hd:hd + 1], bc_all[:, M_HEADS + hd:M_HEADS + hd + 1])
        ac = jnp.where(fwd, ac_all[:, hd:hd + 1], ac_all[:, M_HEADS + hd:M_HEADS + hd + 1])
        ar = jnp.where(fwd, ar_all[hd:hd + 1, :], ar_all[M_HEADS + hd:M_HEADS + hd + 1, :])
        b_tot = jnp.where(fwd, bc[t - 1:t, :], bc[0:1, :])
        m_prev = m_sc[hd][:, 0:1]
        k_h = k_ref[:, sl]
        v_h = v_ref[:, sl]
        if emit_h:
            q_h = q_ref[:, sl]
            dm = jnp.where(causal, bc + ar, NEG_BIG)
            inter = bc + m_prev
            m_t = jnp.maximum(inter, jnp.max(dm, axis=1, keepdims=True))
            qk = lax.dot_general(q_h, k_h, (((1,), (1,)), ((), ())), preferred_element_type=F32)
            s = qk * jnp.exp(dm - m_t)
            carry = jnp.exp(inter - m_t)
            num = (jnp.dot(s.astype(BF16), v_h, preferred_element_type=F32)
                   + carry * jnp.dot(q_h, c_sc[hd].astype(BF16), preferred_element_type=F32))
            den = (jnp.sum(s, axis=1, keepdims=True)
                   + carry * jnp.sum(q_h.astype(F32) * n_sc[hd], axis=1, keepdims=True))
            h_ref[:, sl] = num / jnp.maximum(jnp.abs(den), jnp.exp(-m_t))
        g = b_tot + ac
        m_new = jnp.maximum(b_tot + m_prev, jnp.max(g, axis=0, keepdims=True))
        wgt = jnp.exp(g - m_new)
        decay = jnp.exp(b_tot + m_prev - m_new)
        kw = k_h.astype(F32) * wgt
        c_sc[hd] = decay * c_sc[hd] + lax.dot_general(kw.astype(BF16), v_h, (((0,), (0,)), ((), ())),
                                                      preferred_element_type=F32)
        n_sc[hd] = decay * n_sc[hd] + jnp.sum(kw, axis=0, keepdims=True)
        m_sc[hd] = jnp.broadcast_to(m_new, (1, LANES))

    @pl.when(j == n_chunks - 1)
    def _():
        cf_ref[...] = c_sc[...]
        nf_ref[...] = n_sc[...]
        mf_ref[...] = m_sc[...]


def _mlstm(q, k, v, bc, ac, ar, state, emit_h):
    bsz, L, _ = k[0].shape
    t = MLSTM_CHUNK
    nc = L // t
    seq = lambda b_, d, j: (b_, j + d * (nc - 1 - 2 * j), 0)
    st = lambda b_, d, j: (b_, d, 0, 0, 0)

    def tok(col):
        return pl.BlockSpec((None, t, M_WIDTH), lambda b_, d, j: (b_, j + d * (nc - 1 - 2 * j), col))

    gate_spec = pl.BlockSpec((None, t, LANES), seq)
    ar_spec = pl.BlockSpec((None, FG_LANE0, t), lambda b_, d, j: (b_, 0, j + d * (nc - 1 - 2 * j)))
    c_spec = pl.BlockSpec((None, None, M_HEADS, M_HEAD_DIM, M_HEAD_DIM), st)
    n_spec = pl.BlockSpec((None, None, M_HEADS, 1, M_HEAD_DIM), st)
    m_spec = pl.BlockSpec((None, None, M_HEADS, 1, LANES), st)
    state_shapes = [jax.ShapeDtypeStruct((bsz, 2, M_HEADS, M_HEAD_DIM, M_HEAD_DIM), F32),
                    jax.ShapeDtypeStruct((bsz, 2, M_HEADS, 1, M_HEAD_DIM), F32),
                    jax.ShapeDtypeStruct((bsz, 2, M_HEADS, 1, LANES), F32)]
    in_specs = [tok(k[1]), tok(v[1]), gate_spec, gate_spec, ar_spec, c_spec, n_spec, m_spec]
    args = [k[0], v[0], bc, ac, ar, *state]
    out_specs = [c_spec, n_spec, m_spec]
    out_shape = list(state_shapes)
    if emit_h:
        in_specs = [tok(q[1])] + in_specs
        args = [q[0]] + args
        out_specs = [pl.BlockSpec((None, None, t, M_WIDTH),
                                  lambda b_, d, j: (d, b_, j + d * (nc - 1 - 2 * j), 0))] + out_specs
        out_shape = [jax.ShapeDtypeStruct((2, bsz, L, M_WIDTH), F32)] + out_shape
    outs = pl.pallas_call(
        functools.partial(_mlstm_kernel, emit_h=emit_h, n_chunks=nc),
        grid=(bsz, 2, nc),
        in_specs=in_specs,
        out_specs=out_specs,
        out_shape=out_shape,
        scratch_shapes=[pltpu.VMEM((M_HEADS, M_HEAD_DIM, M_HEAD_DIM), F32),
                        pltpu.VMEM((M_HEADS, 1, M_HEAD_DIM), F32),
                        pltpu.VMEM((M_HEADS, 1, LANES), F32)],
        compiler_params=_cparams("parallel", "parallel", "arbitrary"),
        name="mlstm" if emit_h else "mlstm_state",
    )(*args)
    if emit_h:
        return outs[0], tuple(outs[1:])
    return None, tuple(outs)


FILT_ROWS = 512


def _hyena_filter_kernel(bands_ref, w1_ref, b1_ref, w2_ref, b2_ref, w3_ref, fr_ref, dl_ref, k_ref, ss_ref, *, L):
    i = pl.program_id(0)
    tr = k_ref.shape[0]
    idx = i * tr + lax.broadcasted_iota(jnp.int32, (tr, 1), 0)
    p = jnp.where(idx < L, idx, 2 * L - idx).astype(F32)
    t = p / float(max(L - 1, 1))
    ang = ((2 * math.pi / L) * p) * bands_ref[...]
    lane = lax.broadcasted_iota(jnp.int32, ang.shape, 1)
    feats = jnp.where(lane == 0, t,
                      jnp.where(lane <= H_POS_BANDS, jnp.cos(ang),
                                jnp.where(lane <= 2 * H_POS_BANDS, -jnp.sin(ang), 0.0)))
    fr = fr_ref[...]
    hid = jnp.sin(fr * (jnp.dot(feats.astype(BF16), w1_ref[...], preferred_element_type=F32) + b1_ref[...]))
    hid = jnp.sin(fr * (jnp.dot(hid.astype(BF16), w2_ref[...], preferred_element_type=F32) + b2_ref[...]))
    filt = jnp.dot(hid.astype(BF16), w3_ref[...], preferred_element_type=F32)
    kern = filt * jnp.exp(-t * jnp.abs(dl_ref[...]))
    kern = jnp.where(idx == L, 0.0, kern)
    k_ref[...] = kern

    @pl.when(i == 0)
    def _():
        ss_ref[...] = jnp.zeros_like(ss_ref)

    ss_ref[...] += jnp.sum(kern * kern, axis=0, keepdims=True)


def _hyena_filter(L, w1, b1, w2, b2, w3, freq):
    hid = H_FILTER_HIDDEN
    bands = jnp.linspace(1e-4, H_POS_BANDS - 1, H_POS_BANDS, dtype=F32)
    bands_row = jnp.zeros((1, LANES), F32).at[0, 1:1 + H_POS_BANDS].set(bands).at[
        0, 1 + H_POS_BANDS:1 + 2 * H_POS_BANDS].set(bands)
    w1p = jnp.zeros((LANES, LANES), F32).at[:w1.shape[0], :hid].set(w1).astype(BF16)
    b1p = jnp.zeros((1, LANES), F32).at[0, :hid].set(b1)
    w2p = jnp.zeros((LANES, LANES), F32).at[:hid, :hid].set(w2).astype(BF16)
    b2p = jnp.zeros((1, LANES), F32).at[0, :hid].set(b2)
    w3p = jnp.zeros((LANES, 2 * H_WIDTH), F32).at[:hid].set(w3).astype(BF16)
    frp = jnp.ones((1, LANES), F32).at[0, :hid].set(freq)
    max_decay = math.log(H_DECAY_TARGET) / H_FAST_DECAY_PCT
    min_decay = math.log(H_DECAY_TARGET) / H_SLOW_DECAY_PCT
    deltas = jnp.linspace(min_decay, max_decay, H_WIDTH, dtype=F32).reshape(1, H_WIDTH)
    nt = L // FILT_ROWS
    small = lambda i: (0, 0)
    return pl.pallas_call(
        functools.partial(_hyena_filter_kernel, L=L),
        grid=(2 * nt,),
        in_specs=[pl.BlockSpec((1, LANES), small),
                  pl.BlockSpec((LANES, LANES), small), pl.BlockSpec((1, LANES), small),
                  pl.BlockSpec((LANES, LANES), small), pl.BlockSpec((1, LANES), small),
                  pl.BlockSpec((LANES, H_WIDTH), lambda i: (0, i // nt)),
                  pl.BlockSpec((1, LANES), small),
                  pl.BlockSpec((1, H_WIDTH), small)],
        out_specs=[pl.BlockSpec((FILT_ROWS, H_WIDTH), lambda i: (i, 0)),
                   pl.BlockSpec((1, H_WIDTH), small)],
        out_shape=[jax.ShapeDtypeStruct((2 * L, H_WIDTH), F32),
                   jax.ShapeDtypeStruct((1, H_WIDTH), F32)],
        compiler_params=_cparams("arbitrary"),
        name="hyena_filter",
    )(bands_row, w1p, b1p, w2p, b2p, w3p, frp, deltas)


def _dft_factors(n):
    lg = int(round(math.log2(n)))
    n1 = 1 << ((lg + 1) // 2)
    return n1, n // n1


def _dft_outer_matrices(n1):
    k = np.arange(n1)[:, None]
    n = np.arange(n1)[None, :]
    ang = 2.0 * np.pi * ((k * n) % n1) / n1
    cr, ci = np.cos(ang), -np.sin(ang)
    h = n1 // 2
    fwd_c = np.block([[cr[:, :h], -ci[:, :h]], [ci[:, :h], cr[:, :h]]])
    fwd_r = np.concatenate([cr, ci], axis=0)
    ir, ii = cr[:h, :], -ci[:h, :]
    inv = np.block([[ir, -ii], [ii, ir]])
    return (jnp.asarray(fwd_c, F32).astype(BF16), jnp.asarray(fwd_r, F32).astype(BF16),
            jnp.asarray(inv, F32).astype(BF16))


def _dft_inner_matrices(n1, n2):
    n = n1 * n2
    k1 = jnp.arange(n1, dtype=jnp.int32)[:, None, None]
    k2 = jnp.arange(n2, dtype=jnp.int32)[None, :, None]
    m = jnp.arange(n2, dtype=jnp.int32)[None, None, :]
    ang = ((m * (k1 + n1 * k2)) % n).astype(F32) * (2.0 * math.pi / n)
    gr, gi = jnp.cos(ang), -jnp.sin(ang)
    g = jnp.concatenate([jnp.concatenate([gr, -gi], axis=2), jnp.concatenate([gi, gr], axis=2)], axis=1)
    return g.astype(BF16), jnp.swapaxes(g, 1, 2).astype(BF16)


def _lmat_kernel(*refs):
    l_ref, x_refs, o_ref = refs[0], refs[1:-1], refs[-1]
    xs = [r[...].astype(BF16) for r in x_refs]
    x = xs[0] if len(xs) == 1 else jnp.concatenate(xs, axis=0)
    o_ref[...] = jnp.dot(l_ref[...], x, preferred_element_type=F32)


def _lmat(lmat, xs, tn=4096):
    m, k = lmat.shape
    cols = xs[0].shape[1]
    return pl.pallas_call(
        _lmat_kernel,
        grid=(cols // tn,),
        in_specs=[pl.BlockSpec((m, k), lambda j: (0, 0))]
        + [pl.BlockSpec((x.shape[0], tn), lambda j: (0, j)) for x in xs],
        out_specs=pl.BlockSpec((m, tn), lambda j: (0, j)),
        out_shape=jax.ShapeDtypeStruct((m, cols), F32),
        compiler_params=_cparams("parallel"),
        name="dft_outer",
    )(lmat, *xs)


def _inner_fwd_kernel(g_ref, ar_ref, ai_ref, or_ref, oi_ref):
    n2 = ar_ref.shape[0]
    a = jnp.concatenate([ar_ref[...].astype(BF16), ai_ref[...].astype(BF16)], axis=0)
    x = jnp.dot(g_ref[...], a, preferred_element_type=F32)
    or_ref[...] = x[:n2]
    oi_ref[...] = x[n2:]


def _inner_fwd(g, a):
    _, n1, n2, c = a.shape
    re = pl.BlockSpec((None, None, n2, c), lambda k: (0, k, 0, 0))
    im = pl.BlockSpec((None, None, n2, c), lambda k: (1, k, 0, 0))
    outs = pl.pallas_call(
        _inner_fwd_kernel,
        grid=(n1,),
        in_specs=[pl.BlockSpec((None, 2 * n2, 2 * n2), lambda k: (k, 0, 0)), re, im],
        out_specs=[pl.BlockSpec((None, n2, c), lambda k: (k, 0, 0))] * 2,
        out_shape=[jax.ShapeDtypeStruct((n1, n2, c), F32)] * 2,
        compiler_params=_cparams("parallel"),
        name="dft_inner_filter",
    )(g, a, a)
    return outs


def _inner_conv_kernel(g_ref, gt_ref, ar_ref, ai_ref, kr_ref, ki_ref, or_ref, oi_ref):
    n2 = ar_ref.shape[0]
    a = jnp.concatenate([ar_ref[...].astype(BF16), ai_ref[...].astype(BF16)], axis=0)
    x = jnp.dot(g_ref[...], a, preferred_element_type=F32)
    xr, xi = x[:n2], x[n2:]
    kr, ki = kr_ref[...], ki_ref[...]
    yr = xr * kr - xi * ki
    yi = xr * ki + xi * kr
    y = jnp.concatenate([yr.astype(BF16), yi.astype(BF16)], axis=0)
    b = jnp.dot(gt_ref[...], y, preferred_element_type=F32)
    or_ref[...] = b[:n2]
    oi_ref[...] = b[n2:]


def _inner_conv(g, gt, a, kr, ki):
    _, n1, n2, c = a.shape
    re = pl.BlockSpec((None, None, n2, c), lambda k: (0, k, 0, 0))
    im = pl.BlockSpec((None, None, n2, c), lambda k: (1, k, 0, 0))
    mat = pl.BlockSpec((None, 2 * n2, 2 * n2), lambda k: (k, 0, 0))
    spec = pl.BlockSpec((None, n2, c), lambda k: (k, 0, 0))
    br, bi = pl.pallas_call(
        _inner_conv_kernel,
        grid=(n1,),
        in_specs=[mat, mat, re, im, spec, spec],
        out_specs=[spec, spec],
        out_shape=[jax.ShapeDtypeStruct((n1, n2, c), F32)] * 2,
        compiler_params=_cparams("parallel"),
        name="dft_inner_conv",
    )(g, gt, a, a, kr, ki)
    return br, bi


def _hyena_long_conv(s, kern):
    bsz, L, c = s.shape
    assert bsz == 2
    n = 2 * L
    n1, n2 = _dft_factors(n)
    fwd_c, fwd_r, inv = _dft_outer_matrices(n1)
    g, gt = _dft_inner_matrices(n1, n2)
    kf = _lmat(fwd_r, [kern.reshape(n1, n2 * c)])
    kr, ki = _inner_fwd(g, kf.reshape(2, n1, n2, c))
    s2 = s.reshape(2, n1 // 2, n2 * c)
    a = _lmat(fwd_c, [s2[0], s2[1]])
    br, bi = _inner_conv(g, gt, a.reshape(2, n1, n2, c), kr, ki)
    y = _lmat(inv, [br.reshape(n1, n2 * c), bi.reshape(n1, n2 * c)])
    return y.reshape(2, L, c)


def _merge_kernel(hf_ref, hb_ref, o_ref, x0_ref, s_ref, y_ref, ga_ref, gb_ref, x_ref,
                  ysc_ref, hbias_ref, gate_ref, g2_ref, sh_ref, sc_ref,
                  wa_ref, wb_ref, wo_ref, x1_ref, h2_ref):
    a = o_ref[...].astype(F32) * (hf_ref[...] + hb_ref[...])
    s = s_ref[...]
    hy = x0_ref[...].astype(F32) * (y_ref[...] * ysc_ref[...] + hbias_ref[...] * s)
    pa = jnp.dot(a.astype(BF16), wa_ref[...], preferred_element_type=F32)
    pb = jnp.dot(hy.astype(BF16), wb_ref[...], preferred_element_type=F32)
    mix = ga_ref[...].astype(F32) * pa + gb_ref[...].astype(F32) * pb
    out = jnp.dot(mix.astype(BF16), wo_ref[...], preferred_element_type=F32)
    x1 = x_ref[...] + gate_ref[...] * out
    x1_ref[...] = x1
    y = x1 * lax.rsqrt(jnp.mean(x1 * x1, axis=-1, keepdims=True) + EPS) * g2_ref[...]
    h2_ref[...] = y * (1.0 + sc_ref[...]) + sh_ref[...]


def _merge(hdirs, o, x0, s, y, gab, x, yscale, h_bias, gate1, g2, shift2, scale2, w_a, w_b, w_out, tm=256):
    bsz, L, d = x.shape
    tok = pl.BlockSpec((None, tm, d), lambda b, i: (b, i, 0))
    tok1 = pl.BlockSpec((None, tm, d), lambda b, i: (b, i, 1))
    vec = pl.BlockSpec((1, d), lambda b, i: (0, 0))
    bvec = pl.BlockSpec((None, 1, d), lambda b, i: (b, 0, 0))
    wsp = pl.BlockSpec((d, d), lambda b, i: (0, 0))
    return pl.pallas_call(
        _merge_kernel,
        grid=(bsz, L // tm),
        in_specs=[pl.BlockSpec((None, None, tm, d), lambda b, i: (0, b, i, 0)),
                  pl.BlockSpec((None, None, tm, d), lambda b, i: (1, b, i, 0)),
                  tok, tok, tok, tok, tok, tok1, tok,
                  vec, vec, bvec, vec, bvec, bvec, wsp, wsp, wsp],
        out_specs=[tok, tok],
        out_shape=[jax.ShapeDtypeStruct((bsz, L, d), F32), jax.ShapeDtypeStruct((bsz, L, d), F32)],
        compiler_params=_cparams("parallel", "parallel"),
        name="merge",
    )(hdirs, hdirs, o, x0, s, y, gab, gab, x, yscale, h_bias.reshape(1, d), gate1, g2.reshape(1, d),
      shift2, scale2, w_a, w_b, w_out)


MOE_BLOCK = 256
ROUTE_E1, ROUTE_E2, ROUTE_W1, ROUTE_W2 = 0, 1, 2, 3
EXP_LANE0 = N_GROUPS


def _first_lane_of_max(val, valid, lane):
    masked = jnp.where(valid, val, NEG_BIG)
    mx = jnp.max(masked, axis=1, keepdims=True)
    idx = jnp.min(jnp.where(valid & (masked == mx), lane, LANES), axis=1, keepdims=True)
    return mx, idx


def _router_kernel(h_ref, w_ref, b_ref, r_ref):
    logits = jnp.dot(h_ref[...].astype(BF16), w_ref[...], preferred_element_type=F32) + b_ref[...]
    lane = lax.broadcasted_iota(jnp.int32, logits.shape, 1)
    is_g = lane < N_GROUPS
    gmax, gsel = _first_lane_of_max(logits, is_g, lane)
    gsum = jnp.sum(jnp.where(is_g, jnp.exp(logits - gmax), 0.0), axis=1, keepdims=True)
    gw = 1.0 / gsum
    lo = EXP_LANE0 + gsel * EXPERTS_PER_GROUP
    in_grp = (lane >= lo) & (lane < lo + EXPERTS_PER_GROUP)
    emax, l1 = _first_lane_of_max(logits, in_grp, lane)
    esum = jnp.sum(jnp.where(in_grp, jnp.exp(logits - emax), 0.0), axis=1, keepdims=True)
    e2max, l2 = _first_lane_of_max(logits, in_grp & (lane != l1), lane)
    v1 = 1.0 / esum
    v2 = jnp.exp(e2max - emax) / esum
    vs = v1 + v2
    w1 = gw * v1 / vs
    w2 = gw * v2 / vs
    e1 = (l1 - EXP_LANE0).astype(F32)
    e2 = (l2 - EXP_LANE0).astype(F32)
    r_ref[...] = jnp.where(lane == ROUTE_E1, e1,
                           jnp.where(lane == ROUTE_E2, e2,
                                     jnp.where(lane == ROUTE_W1, w1,
                                               jnp.where(lane == ROUTE_W2, w2, 0.0))))


def _router(h2, w_group, b_group, w_router, b_router, tm=1024):
    n, d = h2.shape
    w = jnp.zeros((d, LANES), F32).at[:, :N_GROUPS].set(w_group).at[
        :, EXP_LANE0:EXP_LANE0 + N_EXPERTS].set(w_router).astype(BF16)
    b = jnp.zeros((1, LANES), F32).at[0, :N_GROUPS].set(b_group).at[
        0, EXP_LANE0:EXP_LANE0 + N_EXPERTS].set(b_router)
    return pl.pallas_call(
        _router_kernel,
        grid=(n // tm,),
        in_specs=[pl.BlockSpec((tm, d), lambda i: (i, 0)),
                  pl.BlockSpec((d, LANES), lambda i: (0, 0)),
                  pl.BlockSpec((1, LANES), lambda i: (0, 0))],
        out_specs=pl.BlockSpec((tm, LANES), lambda i: (i, 0)),
        out_shape=jax.ShapeDtypeStruct((n, LANES), F32),
        compiler_params=_cparams("parallel"),
        name="moe_router",
    )(h2, w, b)


def _slots_kernel(r_ref, dest_ref, cnt_ref, run_sc, start_sc):
    ph = pl.program_id(0)
    i = pl.program_id(1)
    rec = r_ref[...]
    tm = rec.shape[0]
    lane = lax.broadcasted_iota(jnp.int32, rec.shape, 1)
    e1 = rec[:, ROUTE_E1:ROUTE_E1 + 1].astype(jnp.int32)
    e2 = rec[:, ROUTE_E2:ROUTE_E2 + 1].astype(jnp.int32)
    oh1 = lane == e1
    oh2 = lane == e2
    oh = (oh1 | oh2).astype(F32)

    @pl.when((ph == 0) & (i == 0))
    def _():
        run_sc[...] = jnp.zeros_like(run_sc)

    @pl.when(ph == 0)
    def _():
        run_sc[...] += jnp.sum(oh, axis=0, keepdims=True)

    @pl.when((ph == 1) & (i == 0))
    def _():
        counts = run_sc[...]
        cnt_ref[...] = counts
        nblk = jnp.floor((counts + (MOE_BLOCK - 1)) * (1.0 / MOE_BLOCK))
        rr = lax.broadcasted_iota(jnp.int32, (LANES, LANES), 0)
        cc = lax.broadcasted_iota(jnp.int32, (LANES, LANES), 1)
        before = (rr < cc).astype(BF16)
        first = jnp.dot(nblk.astype(BF16), before, preferred_element_type=F32)
        start_sc[...] = first * float(MOE_BLOCK)
        run_sc[...] = jnp.zeros_like(run_sc)

    @pl.when(ph == 1)
    def _():
        r = lax.broadcasted_iota(jnp.int32, (tm, tm), 0)
        c = lax.broadcasted_iota(jnp.int32, (tm, tm), 1)
        earlier = (r > c).astype(BF16)
        rank = jnp.dot(earlier, oh.astype(BF16), preferred_element_type=F32) + run_sc[...] + start_sc[...]
        d1 = jnp.sum(jnp.where(oh1, rank, 0.0), axis=1, keepdims=True)
        d2 = jnp.sum(jnp.where(oh2, rank, 0.0), axis=1, keepdims=True)
        dest_ref[...] = jnp.where(lane == 0, d1, jnp.where(lane == 1, d2, 0.0)).astype(jnp.int32)
        run_sc[...] += jnp.sum(oh, axis=0, keepdims=True)


def _slots(route, tm=512):
    n = route.shape[0]
    return pl.pallas_call(
        _slots_kernel,
        grid=(2, n // tm),
        in_specs=[pl.BlockSpec((tm, LANES), lambda p, i: (i, 0))],
        out_specs=[pl.BlockSpec((tm, LANES), lambda p, i: (i * p, 0)),
                   pl.BlockSpec((1, LANES), lambda p, i: (0, 0))],
        out_shape=[jax.ShapeDtypeStruct((n, LANES), jnp.int32), jax.ShapeDtypeStruct((1, LANES), F32)],
        scratch_shapes=[pltpu.VMEM((1, LANES), F32), pltpu.VMEM((1, LANES), F32)],
        compiler_params=_cparams("arbitrary", "arbitrary"),
        name="moe_slots",
    )(route)


def _row_copy(src_ref, dst_ref, sem, src_row, dst_row):
    return pltpu.make_async_copy(src_ref.at[pl.ds(src_row, 1)], dst_ref.at[pl.ds(dst_row, 1)], sem)


def _dispatch_kernel(dest_ref, h_ref, xs_in_ref, xs_ref, sem):
    del xs_in_ref
    tm = h_ref.shape[0]

    def start(r, carry):
        _row_copy(h_ref, xs_ref, sem, r, dest_ref[0, 2 * r]).start()
        _row_copy(h_ref, xs_ref, sem, r, dest_ref[0, 2 * r + 1]).start()
        return carry

    lax.fori_loop(0, tm, start, 0)

    def wait(r, carry):
        _row_copy(h_ref, xs_ref, sem, 0, 0).wait()
        _row_copy(h_ref, xs_ref, sem, 0, 0).wait()
        return carry

    lax.fori_loop(0, tm, wait, 0)


def _dispatch(h2, dest, n_slots, tm=256):
    n, d = h2.shape
    dest3 = dest.reshape(n // tm, 1, 2 * tm)
    zeros = jnp.zeros((n_slots, d), F32)
    return pl.pallas_call(
        _dispatch_kernel,
        grid=(n // tm,),
        in_specs=[pl.BlockSpec((None, 1, 2 * tm), lambda i: (i, 0, 0), memory_space=pltpu.SMEM),
                  pl.BlockSpec((tm, d), lambda i: (i, 0)),
                  pl.BlockSpec(memory_space=pl.ANY)],
        out_specs=pl.BlockSpec(memory_space=pl.ANY),
        out_shape=jax.ShapeDtypeStruct((n_slots, d), F32),
        scratch_shapes=[pltpu.SemaphoreType.DMA(())],
        input_output_aliases={2: 0},
        compiler_params=_cparams("arbitrary"),
        name="moe_dispatch",
    )(dest3, h2, zeros)


def _experts_kernel(be_ref, nu_ref, x_ref, w1_ref, w3_ref, w2_ref, o_ref):
    i = pl.program_id(0)

    @pl.when(i < nu_ref[0])
    def _():
        x = x_ref[...].astype(BF16)
        a = jnp.dot(x, w1_ref[...].astype(BF16), preferred_element_type=F32)
        b = jnp.dot(x, w3_ref[...].astype(BF16), preferred_element_type=F32)
        hmid = (a * jax.nn.sigmoid(a)) * b
        o_ref[...] = jnp.dot(hmid.astype(BF16), w2_ref[...].astype(BF16), preferred_element_type=F32)

    @pl.when(i >= nu_ref[0])
    def _():
        o_ref[...] = jnp.zeros_like(o_ref)


def _experts(xs, block_e, n_used, w1_e, w3_e, w2_e):
    n_slots, d = xs.shape
    nb = n_slots // MOE_BLOCK
    de = w1_e.shape[2]
    grid_spec = pltpu.PrefetchScalarGridSpec(
        num_scalar_prefetch=2,
        grid=(nb,),
        in_specs=[pl.BlockSpec((MOE_BLOCK, d), lambda i, be, nu: (i, 0)),
                  pl.BlockSpec((None, d, de), lambda i, be, nu: (be[i], 0, 0)),
                  pl.BlockSpec((None, d, de), lambda i, be, nu: (be[i], 0, 0)),
                  pl.BlockSpec((None, de, d), lambda i, be, nu: (be[i], 0, 0))],
        out_specs=pl.BlockSpec((MOE_BLOCK, d), lambda i, be, nu: (i, 0)),
    )
    return pl.pallas_call(
        _experts_kernel,
        grid_spec=grid_spec,
        out_shape=jax.ShapeDtypeStruct((n_slots, d), F32),
        compiler_params=_cparams("arbitrary"),
        name="moe_experts",
    )(block_e, n_used, xs, w1_e, w3_e, w2_e)


def _combine_kernel(dest_ref, r_ref, x_ref, gate_ref, gf_ref, ys_ref, o_ref, buf1, buf2, sem):
    tm = x_ref.shape[0]

    def start(r, carry):
        _row_copy(ys_ref, buf1, sem, dest_ref[0, 2 * r], r).start()
        _row_copy(ys_ref, buf2, sem, dest_ref[0, 2 * r + 1], r).start()
        return carry

    lax.fori_loop(0, tm, start, 0)

    def wait(r, carry):
        _row_copy(ys_ref, buf1, sem, 0, 0).wait()
        _row_copy(ys_ref, buf2, sem, 0, 0).wait()
        return carry

    lax.fori_loop(0, tm, wait, 0)
    rec = r_ref[...]
    y = buf1[...] * rec[:, ROUTE_W1:ROUTE_W1 + 1] + buf2[...] * rec[:, ROUTE_W2:ROUTE_W2 + 1]
    x2 = x_ref[...] + gate_ref[...] * y
    o_ref[...] = x2 * lax.rsqrt(jnp.mean(x2 * x2, axis=-1, keepdims=True) + EPS) * gf_ref[...]


def _combine(ys, dest, route, x1, gate2, g_final, tm=256):
    bsz, L, d = x1.shape
    n = bsz * L
    tpb = L // tm
    dest3 = dest.reshape(n // tm, 1, 2 * tm)
    return pl.pallas_call(
        _combine_kernel,
        grid=(bsz, tpb),
        in_specs=[pl.BlockSpec((None, 1, 2 * tm), lambda b, i: (b * tpb + i, 0, 0), memory_space=pltpu.SMEM),
                  pl.BlockSpec((tm, LANES), lambda b, i: (b * tpb + i, 0)),
                  pl.BlockSpec((None, tm, d), lambda b, i: (b, i, 0)),
                  pl.BlockSpec((None, 1, d), lambda b, i: (b, 0, 0)),
                  pl.BlockSpec((1, d), lambda b, i: (0, 0)),
                  pl.BlockSpec(memory_space=pl.ANY)],
        out_specs=pl.BlockSpec((None, tm, d), lambda b, i: (b, i, 0)),
        out_shape=jax.ShapeDtypeStruct((bsz, L, d), F32),
        scratch_shapes=[pltpu.VMEM((tm, d), F32), pltpu.VMEM((tm, d), F32), pltpu.SemaphoreType.DMA(())],
        compiler_params=_cparams("arbitrary", "arbitrary"),
        name="moe_combine",
    )(dest3, route, x1, gate2, g_final.reshape(1, d), ys)


def _moe(h2, x1, gate2, g_final, w_group, b_group, w_router, b_router, w1_e, w3_e, w2_e):
    bsz, L, d = x1.shape
    n = bsz * L
    h2f = h2.reshape(n, d)
    route = _router(h2f, w_group, b_group, w_router, b_router)
    dest_rec, counts = _slots(route)
    dest = dest_rec[:, :2].reshape(2 * n)
    nb = (2 * n) // MOE_BLOCK + N_EXPERTS
    cnt = counts[0, :N_EXPERTS].astype(jnp.int32)
    blocks_per_e = (cnt + MOE_BLOCK - 1) // MOE_BLOCK
    ends = jnp.cumsum(blocks_per_e)
    block_e = jnp.clip(jnp.searchsorted(ends, jnp.arange(nb, dtype=jnp.int32), side='right'),
                       0, N_EXPERTS - 1).astype(jnp.int32)
    n_used = ends[-1:].astype(jnp.int32)
    xs = _dispatch(h2f, dest, nb * MOE_BLOCK)
    ys = _experts(xs, block_e, n_used, w1_e, w3_e, w2_e)
    return _combine(ys, dest, route, x1, gate2, g_final)


def kernel(x, c, ctx, c_ctx, w_mod, b_mod, g_norm1, g_norm2, w_in, b_in, w_qk_conv, b_qk_conv,
           w_h_conv, b_h_conv, hf_w1, hf_b1, hf_w2, hf_b2, hf_w3, hf_freq, h_bias, w_a, w_b, w_out,
           w_group, b_group, w_router, b_router, w1_e, w3_e, w2_e, g_final):
    assert w_mod.shape[0] == 1, "single-layer block"
    (w_mod, b_mod, g_norm1, g_norm2, w_in, b_in, w_qk_conv, b_qk_conv, w_h_conv, b_h_conv, hf_w1, hf_b1, hf_w2,
     hf_b2, hf_w3, hf_freq, h_bias, w_a, w_b, w_out, w_group, b_group, w_router, b_router, w1_e, w3_e, w2_e) = (
        t[0] for t in (w_mod, b_mod, g_norm1, g_norm2, w_in, b_in, w_qk_conv, b_qk_conv, w_h_conv, b_h_conv,
                       hf_w1, hf_b1, hf_w2, hf_b2, hf_w3, hf_freq, h_bias, w_a, w_b, w_out, w_group, b_group,
                       w_router, b_router, w1_e, w3_e, w2_e))
    bsz, L, d = x.shape
    lc = ctx.shape[1]
    seg = L // (L // GRID_W)
    assert bsz + 1 <= 8 and lc == MLSTM_CHUNK and L % MLSTM_CHUNK == 0

    cond = jnp.zeros((8, d), F32).at[:bsz].set(c).at[bsz].set(c_ctx)
    mod = _adaln(cond, w_mod, b_mod).reshape(8, 6, d)
    modx = mod[:bsz]
    shift1, scale1, gate1, shift2, scale2, gate2 = (modx[:, i:i + 1] for i in range(6))
    shift1c = jnp.broadcast_to(mod[bsz, 0].reshape(1, 1, d), (bsz, 1, d))
    scale1c = jnp.broadcast_to(mod[bsz, 1].reshape(1, 1, d), (bsz, 1, d))

    w_in16 = w_in.astype(BF16)
    k_scale = jnp.full((M_WIDTH,), M_HEAD_DIM ** -0.5, F32)
    qk_scale = jnp.concatenate([jnp.ones((M_WIDTH,), F32), k_scale])
    w_gates, b_gates = w_in[:, IG0:M_COLS], b_in[IG0:M_COLS]

    hc = _norm_mod(ctx, g_norm1, shift1c, scale1c, lc)
    kc = _proj_conv_silu(hc, w_in16[:, K0:V0], b_in[K0:V0], w_qk_conv[:, M_WIDTH:], b_qk_conv[M_WIDTH:],
                         k_scale, lc, lc)
    vc = _proj_act(hc, w_in16[:, V0:O0], b_in[V0:O0], "none", BF16, lc)
    bcc, acc, arc = _gates(hc, w_gates, b_gates, MLSTM_CHUNK)
    zero_state = (jnp.zeros((bsz, 2, M_HEADS, M_HEAD_DIM, M_HEAD_DIM), F32),
                  jnp.zeros((bsz, 2, M_HEADS, 1, M_HEAD_DIM), F32),
                  jnp.zeros((bsz, 2, M_HEADS, 1, LANES), F32))
    _, ctx_state = _mlstm(None, (kc, 0), (vc, 0), bcc, acc, arc, zero_state, False)

    tm = 512
    h = _norm_mod(x, g_norm1, shift1, scale1, tm)
    qk = _proj_conv_silu(h, w_in16[:, Q0:V0], b_in[Q0:V0], w_qk_conv, b_qk_conv, qk_scale, seg, tm)
    v = _proj_act(h, w_in16[:, V0:O0], b_in[V0:O0], "none", BF16, tm)
    o = _proj_act(h, w_in16[:, O0:IG0], b_in[O0:IG0], "sigmoid", BF16, tm)
    gab = _proj_act(h, w_in16[:, GA0:IN_COLS], b_in[GA0:IN_COLS], "sigmoid", BF16, tm)
    bc, ac, ar = _gates(h, w_gates, b_gates, MLSTM_CHUNK)
    hdirs, _ = _mlstm((qk, 0), (qk, 1), (v, 0), bc, ac, ar, ctx_state, True)

    x0, s = _proj_hyena(h, w_in16[:, HY0:GA0], b_in[HY0:GA0], w_h_conv, b_h_conv, seg, tm)
    kern, sumsq = _hyena_filter(L, hf_w1, hf_b1, hf_w2, hf_b2, hf_w3, hf_freq)
    y = _hyena_long_conv(s, kern)
    yscale = lax.rsqrt(sumsq + EPS) * (1.0 / (2 * L))

    x1, h2 = _merge(hdirs, o, x0, s, y, gab, x, yscale, h_bias, gate1, g_norm2, shift2, scale2,
                    w_a.astype(BF16), w_b.astype(BF16), w_out.astype(BF16))
    return _moe(h2, x1, gate2, g_final, w_group, b_group, w_router, b_router, w1_e, w3_e, w2_e)
```

```python
import functools
import math

import jax
import jax.numpy as jnp
import numpy as np
from jax import lax
from jax.experimental import pallas as pl
from jax.experimental.pallas import tpu as pltpu

F32 = jnp.float32
BF16 = jnp.bfloat16

D_MODEL = 1024
GRID_W = 64
EPS = 1e-6
M_HEADS = 4
M_HEAD_DIM = 256
M_WIDTH = M_HEADS * M_HEAD_DIM
H_WIDTH = 1024
H_POS_BANDS = 16
H_FILTER_HIDDEN = 64
H_FAST_DECAY_PCT = 0.3
H_SLOW_DECAY_PCT = 1.5
H_DECAY_TARGET = 1e-2
N_GROUPS = 8
EXPERTS_PER_GROUP = 8
N_EXPERTS = N_GROUPS * EXPERTS_PER_GROUP
D_EXPERT = 512
Q0 = 0
K0 = Q0 + M_WIDTH
V0 = K0 + M_WIDTH
O0 = V0 + M_WIDTH
IG0 = O0 + M_WIDTH
FG0 = IG0 + 2 * M_HEADS
M_COLS = FG0 + 2 * M_HEADS
HY0 = M_COLS
GA0 = HY0 + 3 * H_WIDTH
GB0 = GA0 + D_MODEL
IN_COLS = GB0 + D_MODEL

LANES = 128
MLSTM_CHUNK = 256
NEG_BIG = -1e30
VMEM_LIMIT = 48 * 1024 * 1024


def _cparams(*sem):
    return pltpu.CompilerParams(dimension_semantics=sem, vmem_limit_bytes=VMEM_LIMIT)


def _adaln_kernel(c_ref, w_ref, b_ref, o_ref):
    s = c_ref[...]
    s = s * jax.nn.sigmoid(s)
    o_ref[...] = jnp.dot(s.astype(BF16), w_ref[...].astype(BF16), preferred_element_type=F32) + b_ref[...]


def _adaln(cond, w_mod, b_mod):
    n = w_mod.shape[1]
    tn = 1536
    return pl.pallas_call(
        _adaln_kernel,
        grid=(n // tn,),
        in_specs=[pl.BlockSpec((8, D_MODEL), lambda j: (0, 0)),
                  pl.BlockSpec((D_MODEL, tn), lambda j: (0, j)),
                  pl.BlockSpec((1, tn), lambda j: (0, j))],
        out_specs=pl.BlockSpec((8, tn), lambda j: (0, j)),
        out_shape=jax.ShapeDtypeStruct((8, n), F32),
        compiler_params=_cparams("arbitrary"),
        name="adaln",
    )(cond, w_mod, b_mod.reshape(1, n))


def _norm_mod_kernel(x_ref, g_ref, sh_ref, sc_ref, o_ref):
    x = x_ref[...]
    y = x * lax.rsqrt(jnp.mean(x * x, axis=-1, keepdims=True) + EPS)
    y = y * g_ref[...]
    o_ref[...] = (y * (1.0 + sc_ref[...]) + sh_ref[...]).astype(o_ref.dtype)


def _norm_mod(x, g, shift, scale, tm):
    bsz, L, d = x.shape
    return pl.pallas_call(
        _norm_mod_kernel,
        grid=(bsz, L // tm),
        in_specs=[pl.BlockSpec((None, tm, d), lambda b, i: (b, i, 0)),
                  pl.BlockSpec((1, d), lambda b, i: (0, 0)),
                  pl.BlockSpec((None, 1, d), lambda b, i: (b, 0, 0)),
                  pl.BlockSpec((None, 1, d), lambda b, i: (b, 0, 0))],
        out_specs=pl.BlockSpec((None, tm, d), lambda b, i: (b, i, 0)),
        out_shape=jax.ShapeDtypeStruct((bsz, L, d), BF16),
        compiler_params=_cparams("parallel", "parallel"),
        name="norm_mod",
    )(x, g.reshape(1, d), shift, scale)


def _conv3(z, wc, bc, seg):
    tm = z.shape[0]
    pos = lax.broadcasted_iota(jnp.int32, z.shape, 0) & (seg - 1)
    zp = jnp.where(pos == 0, 0.0, pltpu.roll(z, 1, 0))
    zn = jnp.where(pos == seg - 1, 0.0, pltpu.roll(z, tm - 1, 0))
    return zp * wc[0:1, :] + z * wc[1:2, :] + zn * wc[2:3, :] + bc


def _proj_act_kernel(h_ref, w_ref, b_ref, o_ref, *, act):
    z = jnp.dot(h_ref[...], w_ref[...], preferred_element_type=F32) + b_ref[...]
    if act == "sigmoid":
        z = jax.nn.sigmoid(z)
    o_ref[...] = z.astype(o_ref.dtype)


def _proj_act(h, w, b, act, out_dtype, tm, tn=512):
    bsz, L, d = h.shape
    n = w.shape[1]
    return pl.pallas_call(
        functools.partial(_proj_act_kernel, act=act),
        grid=(bsz, L // tm, n // tn),
        in_specs=[pl.BlockSpec((None, tm, d), lambda b_, i, j: (b_, i, 0)),
                  pl.BlockSpec((d, tn), lambda b_, i, j: (0, j)),
                  pl.BlockSpec((1, tn), lambda b_, i, j: (0, j))],
        out_specs=pl.BlockSpec((None, tm, tn), lambda b_, i, j: (b_, i, j)),
        out_shape=jax.ShapeDtypeStruct((bsz, L, n), out_dtype),
        compiler_params=_cparams("parallel", "parallel", "arbitrary"),
        name="proj_" + act,
    )(h, w, b.reshape(1, n))


def _proj_conv_silu_kernel(h_ref, w_ref, b_ref, wc_ref, bc_ref, cs_ref, o_ref, *, seg):
    z = jnp.dot(h_ref[...], w_ref[...], preferred_element_type=F32) + b_ref[...]
    y = _conv3(z, wc_ref[...], bc_ref[...], seg)
    y = y * jax.nn.sigmoid(y)
    o_ref[...] = (y * cs_ref[...]).astype(o_ref.dtype)


def _proj_conv_silu(h, w, b, wc, bc, colscale, seg, tm, tn=512):
    bsz, L, d = h.shape
    n = w.shape[1]
    col = lambda b_, i, j: (0, j)
    return pl.pallas_call(
        functools.partial(_proj_conv_silu_kernel, seg=seg),
        grid=(bsz, L // tm, n // tn),
        in_specs=[pl.BlockSpec((None, tm, d), lambda b_, i, j: (b_, i, 0)),
                  pl.BlockSpec((d, tn), col),
                  pl.BlockSpec((1, tn), col),
                  pl.BlockSpec((3, tn), col),
                  pl.BlockSpec((1, tn), col),
                  pl.BlockSpec((1, tn), col)],
        out_specs=pl.BlockSpec((None, tm, tn), lambda b_, i, j: (b_, i, j)),
        out_shape=jax.ShapeDtypeStruct((bsz, L, n), BF16),
        compiler_params=_cparams("parallel", "parallel", "arbitrary"),
        name="proj_conv_silu",
    )(h, w, b.reshape(1, n), wc, bc.reshape(1, n), colscale.reshape(1, n))


def _proj_hyena_kernel(h_ref, w0_ref, w1_ref, w2_ref, b_ref, wc_ref, bc_ref, x0_ref, s_ref, *, seg):
    h = h_ref[...]
    us = []
    for g, w_ref in enumerate((w0_ref, w1_ref, w2_ref)):
        z = jnp.dot(h, w_ref[...], preferred_element_type=F32) + b_ref[g]
        us.append(_conv3(z, wc_ref[g], bc_ref[g], seg))
    x0_ref[...] = us[0].astype(x0_ref.dtype)
    s_ref[...] = us[1] * us[2]


def _proj_hyena(h, w, b, wc, bc, seg, tm, tn=512):
    bsz, L, d = h.shape
    nblk = H_WIDTH // tn
    b3 = b.reshape(3, 1, H_WIDTH)
    wc3 = wc.reshape(3, 3, H_WIDTH).transpose(1, 0, 2)
    bc3 = bc.reshape(3, 1, H_WIDTH)
    out_spec = pl.BlockSpec((None, tm, tn), lambda b_, i, j: (b_, i, j))
    return pl.pallas_call(
        functools.partial(_proj_hyena_kernel, seg=seg),
        grid=(bsz, L // tm, nblk),
        in_specs=[pl.BlockSpec((None, tm, d), lambda b_, i, j: (b_, i, 0)),
                  pl.BlockSpec((d, tn), lambda b_, i, j: (0, j)),
                  pl.BlockSpec((d, tn), lambda b_, i, j: (0, nblk + j)),
                  pl.BlockSpec((d, tn), lambda b_, i, j: (0, 2 * nblk + j)),
                  pl.BlockSpec((3, 1, tn), lambda b_, i, j: (0, 0, j)),
                  pl.BlockSpec((3, 3, tn), lambda b_, i, j: (0, 0, j)),
                  pl.BlockSpec((3, 1, tn), lambda b_, i, j: (0, 0, j))],
        out_specs=[out_spec, out_spec],
        out_shape=[jax.ShapeDtypeStruct((bsz, L, H_WIDTH), BF16),
                   jax.ShapeDtypeStruct((bsz, L, H_WIDTH), F32)],
        compiler_params=_cparams("parallel", "parallel", "arbitrary"),
        name="proj_hyena",
    )(h, w, w, w, b3, wc3, bc3)


N_GATES = 4 * M_HEADS


def _split3(x):
    hi = x.astype(BF16)
    r1 = x - hi.astype(F32)
    mid = r1.astype(BF16)
    lo = (r1 - mid.astype(F32)).astype(BF16)
    return hi, mid, lo


def _log_sigmoid(x):
    return jnp.minimum(x, 0.0) - jnp.log1p(jnp.exp(-jnp.abs(x)))


def _gates_kernel(h_ref, w_ref, wt_ref, b_ref, bt_ref, bc_ref, ac_ref, ar_ref):
    h = h_ref[...]
    t = h.shape[0]
    z = jnp.dot(h, w_ref[...], preferred_element_type=F32) + b_ref[...]
    zt = lax.dot_general(wt_ref[...], h, (((1,), (1,)), ((), ())),
                         preferred_element_type=F32) + bt_ref[...]
    r = lax.broadcasted_iota(jnp.int32, (t, t), 0)
    c = lax.broadcasted_iota(jnp.int32, (t, t), 1)
    lower = (r >= c).astype(BF16)
    upper = (r <= c).astype(BF16)

    lf = _log_sigmoid(z)
    parts = _split3(lf)
    cf = sum(jnp.dot(lower, p, preferred_element_type=F32) for p in parts)
    cb = sum(jnp.dot(upper, p, preferred_element_type=F32) for p in parts)
    lane = lax.broadcasted_iota(jnp.int32, z.shape, 1)
    bc = jnp.where(lane < FG_LANE0 + M_HEADS, cf, cb)
    bc = pltpu.roll(bc, LANES - FG_LANE0, 1)
    bc_ref[...] = bc
    ac_ref[...] = z - bc

    lft = _log_sigmoid(zt[FG_LANE0:, :])
    tparts = _split3(lft)
    cft = sum(jnp.dot(p, upper, preferred_element_type=F32) for p in tparts)
    cbt = sum(jnp.dot(p, lower, preferred_element_type=F32) for p in tparts)
    row = lax.broadcasted_iota(jnp.int32, cft.shape, 0)
    ar_ref[...] = zt[:FG_LANE0, :] - jnp.where(row < M_HEADS, cft, cbt)


FG_LANE0 = 2 * M_HEADS


def _gates(h, w_g, b_g, chunk):
    bsz, L, d = h.shape
    w_pad = jnp.zeros((d, LANES), F32).at[:, :N_GATES].set(w_g).astype(BF16)
    b_pad = jnp.zeros((1, LANES), F32).at[0, :N_GATES].set(b_g)
    wt = w_g.T.astype(BF16)
    bt = b_g.reshape(N_GATES, 1)
    tok = pl.BlockSpec((None, chunk, LANES), lambda b_, i: (b_, i, 0))
    return pl.pallas_call(
        _gates_kernel,
        grid=(bsz, L // chunk),
        in_specs=[pl.BlockSpec((None, chunk, d), lambda b_, i: (b_, i, 0)),
                  pl.BlockSpec((d, LANES), lambda b_, i: (0, 0)),
                  pl.BlockSpec((N_GATES, d), lambda b_, i: (0, 0)),
                  pl.BlockSpec((1, LANES), lambda b_, i: (0, 0)),
                  pl.BlockSpec((N_GATES, 1), lambda b_, i: (0, 0))],
        out_specs=[tok, tok, pl.BlockSpec((None, FG_LANE0, chunk), lambda b_, i: (b_, 0, i))],
        out_shape=[jax.ShapeDtypeStruct((bsz, L, LANES), F32),
                   jax.ShapeDtypeStruct((bsz, L, LANES), F32),
                   jax.ShapeDtypeStruct((bsz, FG_LANE0, L), F32)],
        compiler_params=_cparams("parallel", "parallel"),
        name="mlstm_gates",
    )(h, w_pad, wt, b_pad, bt)


def _mlstm_kernel(*refs, emit_h, n_chunks):
    if emit_h:
        (q_ref, k_ref, v_ref, bc_ref, ac_ref, ar_ref, c0_ref, n0_ref, m0_ref,
         h_ref, cf_ref, nf_ref, mf_ref, c_sc, n_sc, m_sc) = refs
    else:
        (k_ref, v_ref, bc_ref, ac_ref, ar_ref, c0_ref, n0_ref, m0_ref,
         cf_ref, nf_ref, mf_ref, c_sc, n_sc, m_sc) = refs
    d = pl.program_id(1)
    j = pl.program_id(2)
    fwd = d == 0
    t = k_ref.shape[0]
    dh = M_HEAD_DIM

    @pl.when(j == 0)
    def _():
        c_sc[...] = c0_ref[...]
        n_sc[...] = n0_ref[...]
        m_sc[...] = m0_ref[...]

    r = lax.broadcasted_iota(jnp.int32, (t, t), 0)
    c = lax.broadcasted_iota(jnp.int32, (t, t), 1)
    causal = jnp.where(fwd, r - c, c - r) >= 0
    bc_all = bc_ref[...]
    ac_all = ac_ref[...]
    ar_all = ar_ref[...]
    for hd in range(M_HEADS):
        sl = slice(hd * dh, (hd + 1) * dh)
        bc = jnp.where(fwd, bc_all[:, hd:hd + 1], bc_all[:, M_HEADS + hd:M_HEADS + hd + 1])
        ac = jnp.where(fwd, ac_all[:, hd:hd + 1], ac_all[:, M_HEADS + hd:M_HEADS + hd + 1])
        ar = jnp.where(fwd, ar_all[hd:hd + 1, :], ar_all[M_HEADS + hd:M_HEADS + hd + 1, :])
        b_tot = jnp.where(fwd, bc[t - 1:t, :], bc[0:1, :])
        m_prev = m_sc[hd][:, 0:1]
        k_h = k_ref[:, sl]
        v_h = v_ref[:, sl]
        if emit_h:
            q_h = q_ref[:, sl]
            dm = jnp.where(causal, bc + ar, NEG_BIG)
            inter = bc + m_prev
            m_t = jnp.maximum(inter, jnp.max(dm, axis=1, keepdims=True))
            qk = lax.dot_general(q_h, k_h, (((1,), (1,)), ((), ())), preferred_element_type=F32)
            s = qk * jnp.exp(dm - m_t)
            carry = jnp.exp(inter - m_t)
            num = (jnp.dot(s.astype(BF16), v_h, preferred_element_type=F32)
                   + carry * jnp.dot(q_h, c_sc[hd].astype(BF16), preferred_element_type=F32))
            den = (jnp.sum(s, axis=1, keepdims=True)
                   + carry * jnp.sum(q_h.astype(F32) * n_sc[hd], axis=1, keepdims=True))
            h_ref[:, sl] = num / jnp.maximum(jnp.abs(den), jnp.exp(-m_t))
        g = b_tot + ac
        m_new = jnp.maximum(b_tot + m_prev, jnp.max(g, axis=0, keepdims=True))
        wgt = jnp.exp(g - m_new)
        decay = jnp.exp(b_tot + m_prev - m_new)
        kw = k_h.astype(F32) * wgt
        c_sc[hd] = decay * c_sc[hd] + lax.dot_general(kw.astype(BF16), v_h, (((0,), (0,)), ((), ())),
                                                      preferred_element_type=F32)
        n_sc[hd] = decay * n_sc[hd] + jnp.sum(kw, axis=0, keepdims=True)
        m_sc[hd] = jnp.broadcast_to(m_new, (1, LANES))

    @pl.when(j == n_chunks - 1)
    def _():
        cf_ref[...] = c_sc[...]
        nf_ref[...] = n_sc[...]
        mf_ref[...] = m_sc[...]


def _mlstm(q, k, v, bc, ac, ar, state, emit_h):
    bsz, L, _ = k[0].shape
    t = MLSTM_CHUNK
    nc = L // t
    seq = lambda b_, d, j: (b_, j + d * (nc - 1 - 2 * j), 0)
    st = lambda b_, d, j: (b_, d, 0, 0, 0)

    def tok(col):
        return pl.BlockSpec((None, t, M_WIDTH), lambda b_, d, j: (b_, j + d * (nc - 1 - 2 * j), col))

    gate_spec = pl.BlockSpec((None, t, LANES), seq)
    ar_spec = pl.BlockSpec((None, FG_LANE0, t), lambda b_, d, j: (b_, 0, j + d * (nc - 1 - 2 * j)))
    c_spec = pl.BlockSpec((None, None, M_HEADS, M_HEAD_DIM, M_HEAD_DIM), st)
    n_spec = pl.BlockSpec((None, None, M_HEADS, 1, M_HEAD_DIM), st)
    m_spec = pl.BlockSpec((None, None, M_HEADS, 1, LANES), st)
    state_shapes = [jax.ShapeDtypeStruct((bsz, 2, M_HEADS, M_HEAD_DIM, M_HEAD_DIM), F32),
                    jax.ShapeDtypeStruct((bsz, 2, M_HEADS, 1, M_HEAD_DIM), F32),
                    jax.ShapeDtypeStruct((bsz, 2, M_HEADS, 1, LANES), F32)]
    in_specs = [tok(k[1]), tok(v[1]), gate_spec, gate_spec, ar_spec, c_spec, n_spec, m_spec]
    args = [k[0], v[0], bc, ac, ar, *state]
    out_specs = [c_spec, n_spec, m_spec]
    out_shape = list(state_shapes)
    if emit_h:
        in_specs = [tok(q[1])] + in_specs
        args = [q[0]] + args
        out_specs = [pl.BlockSpec((None, None, t, M_WIDTH),
                                  lambda b_, d, j: (d, b_, j + d * (nc - 1 - 2 * j), 0))] + out_specs
        out_shape = [jax.ShapeDtypeStruct((2, bsz, L, M_WIDTH), F32)] + out_shape
    outs = pl.pallas_call(
        functools.partial(_mlstm_kernel, emit_h=emit_h, n_chunks=nc),
        grid=(bsz, 2, nc),
        in_specs=in_specs,
        out_specs=out_specs,
        out_shape=out_shape,
        scratch_shapes=[pltpu.VMEM((M_HEADS, M_HEAD_DIM, M_HEAD_DIM), F32),
                        pltpu.VMEM((M_HEADS, 1, M_HEAD_DIM), F32),
                        pltpu.VMEM((M_HEADS, 1, LANES), F32)],
        compiler_params=_cparams("parallel", "parallel", "arbitrary"),
        name="mlstm" if emit_h else "mlstm_state",
    )(*args)
    if emit_h:
        return outs[0], tuple(outs[1:])
    return None, tuple(outs)


FILT_ROWS = 512


def _hyena_filter_kernel(bands_ref, w1_ref, b1_ref, w2_ref, b2_ref, w3_ref, fr_ref, dl_ref, k_ref, ss_ref, *, L):
    i = pl.program_id(0)
    tr = k_ref.shape[0]
    idx = i * tr + lax.broadcasted_iota(jnp.int32, (tr, 1), 0)
    p = jnp.where(idx < L, idx, 2 * L - idx).astype(F32)
    t = p / float(max(L - 1, 1))
    ang = ((2 * math.pi / L) * p) * bands_ref[...]
    lane = lax.broadcasted_iota(jnp.int32, ang.shape, 1)
    feats = jnp.where(lane == 0, t,
                      jnp.where(lane <= H_POS_BANDS, jnp.cos(ang),
                                jnp.where(lane <= 2 * H_POS_BANDS, -jnp.sin(ang), 0.0)))
    fr = fr_ref[...]
    hid = jnp.sin(fr * (jnp.dot(feats.astype(BF16), w1_ref[...], preferred_element_type=F32) + b1_ref[...]))
    hid = jnp.sin(fr * (jnp.dot(hid.astype(BF16), w2_ref[...], preferred_element_type=F32) + b2_ref[...]))
    filt = jnp.dot(hid.astype(BF16), w3_ref[...], preferred_element_type=F32)
    kern = filt * jnp.exp(-t * jnp.abs(dl_ref[...]))
    kern = jnp.where(idx == L, 0.0, kern)
    k_ref[...] = kern

    @pl.when(i == 0)
    def _():
        ss_ref[...] = jnp.zeros_like(ss_ref)

    ss_ref[...] += jnp.sum(kern * kern, axis=0, keepdims=True)


def _hyena_filter(L, w1, b1, w2, b2, w3, freq):
    hid = H_FILTER_HIDDEN
    bands = jnp.linspace(1e-4, H_POS_BANDS - 1, H_POS_BANDS, dtype=F32)
    bands_row = jnp.zeros((1, LANES), F32).at[0, 1:1 + H_POS_BANDS].set(bands).at[
        0, 1 + H_POS_BANDS:1 + 2 * H_POS_BANDS].set(bands)
    w1p = jnp.zeros((LANES, LANES), F32).at[:w1.shape[0], :hid].set(w1).astype(BF16)
    b1p = jnp.zeros((1, LANES), F32).at[0, :hid].set(b1)
    w2p = jnp.zeros((LANES, LANES), F32).at[:hid, :hid].set(w2).astype(BF16)
    b2p = jnp.zeros((1, LANES), F32).at[0, :hid].set(b2)
    w3p = jnp.zeros((LANES, 2 * H_WIDTH), F32).at[:hid].set(w3).astype(BF16)
    frp = jnp.ones((1, LANES), F32).at[0, :hid].set(freq)
    max_decay = math.log(H_DECAY_TARGET) / H_FAST_DECAY_PCT
    min_decay = math.log(H_DECAY_TARGET) / H_SLOW_DECAY_PCT
    deltas = jnp.linspace(min_decay, max_decay, H_WIDTH, dtype=F32).reshape(1, H_WIDTH)
    nt = L // FILT_ROWS
    small = lambda i: (0, 0)
    return pl.pallas_call(
        functools.partial(_hyena_filter_kernel, L=L),
        grid=(2 * nt,),
        in_specs=[pl.BlockSpec((1, LANES), small),
                  pl.BlockSpec((LANES, LANES), small), pl.BlockSpec((1, LANES), small),
                  pl.BlockSpec((LANES, LANES), small), pl.BlockSpec((1, LANES), small),
                  pl.BlockSpec((LANES, H_WIDTH), lambda i: (0, i // nt)),
                  pl.BlockSpec((1, LANES), small),
                  pl.BlockSpec((1, H_WIDTH), small)],
        out_specs=[pl.BlockSpec((FILT_ROWS, H_WIDTH), lambda i: (i, 0)),
                   pl.BlockSpec((1, H_WIDTH), small)],
        out_shape=[jax.ShapeDtypeStruct((2 * L, H_WIDTH), F32),
                   jax.ShapeDtypeStruct((1, H_WIDTH), F32)],
        compiler_params=_cparams("arbitrary"),
        name="hyena_filter",
    )(bands_row, w1p, b1p, w2p, b2p, w3p, frp, deltas)


def _dft_factors(n):
    lg = int(round(math.log2(n)))
    n1 = 1 << ((lg + 1) // 2)
    return n1, n // n1


def _dft_outer_matrices(n1):
    k = np.arange(n1)[:, None]
    n = np.arange(n1)[None, :]
    ang = 2.0 * np.pi * ((k * n) % n1) / n1
    cr, ci = np.cos(ang), -np.sin(ang)
    h = n1 // 2
    fwd_c = np.block([[cr[:, :h], -ci[:, :h]], [ci[:, :h], cr[:, :h]]])
    fwd_r = np.concatenate([cr, ci], axis=0)
    ir, ii = cr[:h, :], -ci[:h, :]
    inv = np.block([[ir, -ii], [ii, ir]])
    return (jnp.asarray(fwd_c, F32).astype(BF16), jnp.asarray(fwd_r, F32).astype(BF16),
            jnp.asarray(inv, F32).astype(BF16))


def _dft_inner_matrices(n1, n2):
    n = n1 * n2
    k1 = jnp.arange(n1, dtype=jnp.int32)[:, None, None]
    k2 = jnp.arange(n2, dtype=jnp.int32)[None, :, None]
    m = jnp.arange(n2, dtype=jnp.int32)[None, None, :]
    ang = ((m * (k1 + n1 * k2)) % n).astype(F32) * (2.0 * math.pi / n)
    gr, gi = jnp.cos(ang), -jnp.sin(ang)
    g = jnp.concatenate([jnp.concatenate([gr, -gi], axis=2), jnp.concatenate([gi, gr], axis=2)], axis=1)
    return g.astype(BF16), jnp.swapaxes(g, 1, 2).astype(BF16)


DFT_M_TILE = 8
DFT_C_TILE = 512


def _outer_dft_kernel(l_ref, x_ref, o_ref):
    p_in, p_out = x_ref.shape[0], o_ref.shape[0]
    r_out = o_ref.shape[1]
    for mm in range(x_ref.shape[2]):
        parts = [x_ref[p, :, mm, :].astype(BF16) for p in range(p_in)]
        x = parts[0] if p_in == 1 else jnp.concatenate(parts, axis=0)
        out = jnp.dot(l_ref[...], x, preferred_element_type=F32)
        for p in range(p_out):
            o_ref[p, :, mm, :] = out[p * r_out:(p + 1) * r_out]


def _outer_dft(lmat, x4, p_out):
    p_in, r_in, n2, c = x4.shape
    r_out = lmat.shape[0] // p_out
    tc = min(DFT_C_TILE, c)
    return pl.pallas_call(
        _outer_dft_kernel,
        grid=(n2 // DFT_M_TILE, c // tc),
        in_specs=[pl.BlockSpec(lmat.shape, lambda m, j: (0, 0)),
                  pl.BlockSpec((p_in, r_in, DFT_M_TILE, tc), lambda m, j: (0, 0, m, j))],
        out_specs=pl.BlockSpec((p_out, r_out, DFT_M_TILE, tc), lambda m, j: (0, 0, m, j)),
        out_shape=jax.ShapeDtypeStruct((p_out, r_out, n2, c), F32),
        compiler_params=_cparams("parallel", "parallel"),
        name="dft_outer",
    )(lmat, x4)


def _inner_fwd_kernel(g_ref, a_ref, o_ref):
    n2 = a_ref.shape[1]
    a = jnp.concatenate([a_ref[0].astype(BF16), a_ref[1].astype(BF16)], axis=0)
    x = jnp.dot(g_ref[...], a, preferred_element_type=F32)
    o_ref[0] = x[:n2].astype(o_ref.dtype)
    o_ref[1] = x[n2:].astype(o_ref.dtype)


def _inner_fwd(g, a):
    _, n1, n2, c = a.shape
    blk = pl.BlockSpec((2, None, n2, c), lambda k: (0, k, 0, 0))
    return pl.pallas_call(
        _inner_fwd_kernel,
        grid=(n1,),
        in_specs=[pl.BlockSpec((None, 2 * n2, 2 * n2), lambda k: (k, 0, 0)), blk],
        out_specs=blk,
        out_shape=jax.ShapeDtypeStruct((2, n1, n2, c), BF16),
        compiler_params=_cparams("parallel"),
        name="dft_inner_filter",
    )(g, a)


def _inner_conv_kernel(g_ref, gt_ref, a_ref, k_ref, o_ref):
    n2 = a_ref.shape[1]
    a = jnp.concatenate([a_ref[0].astype(BF16), a_ref[1].astype(BF16)], axis=0)
    x = jnp.dot(g_ref[...], a, preferred_element_type=F32)
    xr, xi = x[:n2], x[n2:]
    kr, ki = k_ref[0].astype(F32), k_ref[1].astype(F32)
    yr = xr * kr - xi * ki
    yi = xr * ki + xi * kr
    y = jnp.concatenate([yr.astype(BF16), yi.astype(BF16)], axis=0)
    b = jnp.dot(gt_ref[...], y, preferred_element_type=F32)
    o_ref[0] = b[:n2]
    o_ref[1] = b[n2:]


def _inner_conv(g, gt, a, kf):
    _, n1, n2, c = a.shape
    blk = pl.BlockSpec((2, None, n2, c), lambda k: (0, k, 0, 0))
    mat = pl.BlockSpec((None, 2 * n2, 2 * n2), lambda k: (k, 0, 0))
    return pl.pallas_call(
        _inner_conv_kernel,
        grid=(n1,),
        in_specs=[mat, mat, blk, blk],
        out_specs=blk,
        out_shape=jax.ShapeDtypeStruct((2, n1, n2, c), F32),
        compiler_params=_cparams("parallel"),
        name="dft_inner_conv",
    )(g, gt, a, kf)


def _hyena_long_conv(s, kern):
    bsz, L, c = s.shape
    assert bsz == 2
    n = 2 * L
    n1, n2 = _dft_factors(n)
    fwd_c, fwd_r, inv = _dft_outer_matrices(n1)
    g, gt = _dft_inner_matrices(n1, n2)
    kf = _inner_fwd(g, _outer_dft(fwd_r, kern.reshape(1, n1, n2, c), 2))
    a = _outer_dft(fwd_c, s.reshape(2, n1 // 2, n2, c), 2)
    b = _inner_conv(g, gt, a, kf)
    y = _outer_dft(inv, b, 2)
    return y.reshape(2, L, c)


def _merge_kernel(hf_ref, hb_ref, o_ref, x0_ref, s_ref, y_ref, ga_ref, gb_ref, x_ref,
                  ysc_ref, hbias_ref, gate_ref, g2_ref, sh_ref, sc_ref,
                  wa_ref, wb_ref, wo_ref, x1_ref, h2_ref):
    a = o_ref[...].astype(F32) * (hf_ref[...] + hb_ref[...])
    s = s_ref[...]
    hy = x0_ref[...].astype(F32) * (y_ref[...] * ysc_ref[...] + hbias_ref[...] * s)
    pa = jnp.dot(a.astype(BF16), wa_ref[...], preferred_element_type=F32)
    pb = jnp.dot(hy.astype(BF16), wb_ref[...], preferred_element_type=F32)
    mix = ga_ref[...].astype(F32) * pa + gb_ref[...].astype(F32) * pb
    out = jnp.dot(mix.astype(BF16), wo_ref[...], preferred_element_type=F32)
    x1 = x_ref[...] + gate_ref[...] * out
    x1_ref[...] = x1
    y = x1 * lax.rsqrt(jnp.mean(x1 * x1, axis=-1, keepdims=True) + EPS) * g2_ref[...]
    h2_ref[...] = y * (1.0 + sc_ref[...]) + sh_ref[...]


def _merge(hdirs, o, x0, s, y, gab, x, yscale, h_bias, gate1, g2, shift2, scale2, w_a, w_b, w_out, tm=256):
    bsz, L, d = x.shape
    tok = pl.BlockSpec((None, tm, d), lambda b, i: (b, i, 0))
    tok1 = pl.BlockSpec((None, tm, d), lambda b, i: (b, i, 1))
    vec = pl.BlockSpec((1, d), lambda b, i: (0, 0))
    bvec = pl.BlockSpec((None, 1, d), lambda b, i: (b, 0, 0))
    wsp = pl.BlockSpec((d, d), lambda b, i: (0, 0))
    return pl.pallas_call(
        _merge_kernel,
        grid=(bsz, L // tm),
        in_specs=[pl.BlockSpec((None, None, tm, d), lambda b, i: (0, b, i, 0)),
                  pl.BlockSpec((None, None, tm, d), lambda b, i: (1, b, i, 0)),
                  tok, tok, tok, tok, tok, tok1, tok,
                  vec, vec, bvec, vec, bvec, bvec, wsp, wsp, wsp],
        out_specs=[tok, tok],
        out_shape=[jax.ShapeDtypeStruct((bsz, L, d), F32), jax.ShapeDtypeStruct((bsz, L, d), F32)],
        compiler_params=_cparams("parallel", "parallel"),
        name="merge",
    )(hdirs, hdirs, o, x0, s, y, gab, gab, x, yscale, h_bias.reshape(1, d), gate1, g2.reshape(1, d),
      shift2, scale2, w_a, w_b, w_out)


MOE_BLOCK = 256
ROUTE_E1, ROUTE_E2, ROUTE_W1, ROUTE_W2 = 0, 1, 2, 3
EXP_LANE0 = N_GROUPS


def _first_lane_of_max(val, valid, lane):
    masked = jnp.where(valid, val, NEG_BIG)
    mx = jnp.max(masked, axis=1, keepdims=True)
    idx = jnp.min(jnp.where(valid & (masked == mx), lane, LANES), axis=1, keepdims=True)
    return mx, idx


def _router_kernel(h_ref, w_ref, b_ref, r_ref):
    logits = jnp.dot(h_ref[...].astype(BF16), w_ref[...], preferred_element_type=F32) + b_ref[...]
    lane = lax.broadcasted_iota(jnp.int32, logits.shape, 1)
    is_g = lane < N_GROUPS
    gmax, gsel = _first_lane_of_max(logits, is_g, lane)
    gsum = jnp.sum(jnp.where(is_g, jnp.exp(logits - gmax), 0.0), axis=1, keepdims=True)
    gw = 1.0 / gsum
    lo = EXP_LANE0 + gsel * EXPERTS_PER_GROUP
    in_grp = (lane >= lo) & (lane < lo + EXPERTS_PER_GROUP)
    emax, l1 = _first_lane_of_max(logits, in_grp, lane)
    esum = jnp.sum(jnp.where(in_grp, jnp.exp(logits - emax), 0.0), axis=1, keepdims=True)
    e2max, l2 = _first_lane_of_max(logits, in_grp & (lane != l1), lane)
    v1 = 1.0 / esum
    v2 = jnp.exp(e2max - emax) / esum
    vs = v1 + v2
    w1 = gw * v1 / vs
    w2 = gw * v2 / vs
    e1 = (l1 - EXP_LANE0).astype(F32)
    e2 = (l2 - EXP_LANE0).astype(F32)
    r_ref[...] = jnp.where(lane == ROUTE_E1, e1,
                           jnp.where(lane == ROUTE_E2, e2,
                                     jnp.where(lane == ROUTE_W1, w1,
                                               jnp.where(lane == ROUTE_W2, w2, 0.0))))


def _router(h2, w_group, b_group, w_router, b_router, tm=1024):
    n, d = h2.shape
    w = jnp.zeros((d, LANES), F32).at[:, :N_GROUPS].set(w_group).at[
        :, EXP_LANE0:EXP_LANE0 + N_EXPERTS].set(w_router).astype(BF16)
    b = jnp.zeros((1, LANES), F32).at[0, :N_GROUPS].set(b_group).at[
        0, EXP_LANE0:EXP_LANE0 + N_EXPERTS].set(b_router)
    return pl.pallas_call(
        _router_kernel,
        grid=(n // tm,),
        in_specs=[pl.BlockSpec((tm, d), lambda i: (i, 0)),
                  pl.BlockSpec((d, LANES), lambda i: (0, 0)),
                  pl.BlockSpec((1, LANES), lambda i: (0, 0))],
        out_specs=pl.BlockSpec((tm, LANES), lambda i: (i, 0)),
        out_shape=jax.ShapeDtypeStruct((n, LANES), F32),
        compiler_params=_cparams("parallel"),
        name="moe_router",
    )(h2, w, b)


def _slots_kernel(r_ref, dest_ref, cnt_ref, run_sc, start_sc):
    ph = pl.program_id(0)
    i = pl.program_id(1)
    rec = r_ref[...]
    tm = rec.shape[0]
    lane = lax.broadcasted_iota(jnp.int32, rec.shape, 1)
    e1 = rec[:, ROUTE_E1:ROUTE_E1 + 1].astype(jnp.int32)
    e2 = rec[:, ROUTE_E2:ROUTE_E2 + 1].astype(jnp.int32)
    oh1 = lane == e1
    oh2 = lane == e2
    oh = (oh1 | oh2).astype(F32)

    @pl.when((ph == 0) & (i == 0))
    def _():
        run_sc[...] = jnp.zeros_like(run_sc)

    @pl.when(ph == 0)
    def _():
        run_sc[...] += jnp.sum(oh, axis=0, keepdims=True)

    @pl.when((ph == 1) & (i == 0))
    def _():
        counts = run_sc[...]
        cnt_ref[...] = counts
        nblk = jnp.floor((counts + (MOE_BLOCK - 1)) * (1.0 / MOE_BLOCK))
        rr = lax.broadcasted_iota(jnp.int32, (LANES, LANES), 0)
        cc = lax.broadcasted_iota(jnp.int32, (LANES, LANES), 1)
        before = (rr < cc).astype(BF16)
        first = jnp.dot(nblk.astype(BF16), before, preferred_element_type=F32)
        start_sc[...] = first * float(MOE_BLOCK)
        run_sc[...] = jnp.zeros_like(run_sc)

    @pl.when(ph == 1)
    def _():
        r = lax.broadcasted_iota(jnp.int32, (tm, tm), 0)
        c = lax.broadcasted_iota(jnp.int32, (tm, tm), 1)
        earlier = (r > c).astype(BF16)
        rank = jnp.dot(earlier, oh.astype(BF16), preferred_element_type=F32) + run_sc[...] + start_sc[...]
        d1 = jnp.sum(jnp.where(oh1, rank, 0.0), axis=1, keepdims=True)
        d2 = jnp.sum(jnp.where(oh2, rank, 0.0), axis=1, keepdims=True)
        dest_ref[...] = jnp.where(lane == 0, d1, jnp.where(lane == 1, d2, 0.0)).astype(jnp.int32)
        run_sc[...] += jnp.sum(oh, axis=0, keepdims=True)


def _slots(route, tm=512):
    n = route.shape[0]
    return pl.pallas_call(
        _slots_kernel,
        grid=(2, n // tm),
        in_specs=[pl.BlockSpec((tm, LANES), lambda p, i: (i, 0))],
        out_specs=[pl.BlockSpec((tm, LANES), lambda p, i: (i * p, 0)),
                   pl.BlockSpec((1, LANES), lambda p, i: (0, 0))],
        out_shape=[jax.ShapeDtypeStruct((n, LANES), jnp.int32), jax.ShapeDtypeStruct((1, LANES), F32)],
        scratch_shapes=[pltpu.VMEM((1, LANES), F32), pltpu.VMEM((1, LANES), F32)],
        compiler_params=_cparams("arbitrary", "arbitrary"),
        name="moe_slots",
    )(route)


DMA_UNROLL = 8


def _row_copy(src_ref, dst_ref, sem, src_row, dst_row):
    return pltpu.make_async_copy(src_ref.at[pl.ds(src_row, 1)], dst_ref.at[pl.ds(dst_row, 1)], sem)


def _dispatch_kernel(dest_ref, h_ref, xs_in_ref, xs_ref, sem):
    del xs_in_ref
    tm = h_ref.shape[0]

    def start(r, carry):
        _row_copy(h_ref, xs_ref, sem, r, dest_ref[0, 2 * r]).start(priority=0)
        _row_copy(h_ref, xs_ref, sem, r, dest_ref[0, 2 * r + 1]).start(priority=1)
        return carry

    lax.fori_loop(0, tm, start, 0, unroll=DMA_UNROLL)

    def wait(r, carry):
        _row_copy(h_ref, xs_ref, sem, 0, 0).wait()
        _row_copy(h_ref, xs_ref, sem, 0, 0).wait()
        return carry

    lax.fori_loop(0, tm, wait, 0, unroll=DMA_UNROLL)


def _dispatch(h2, dest, n_slots, tm=256):
    n, d = h2.shape
    dest3 = dest.reshape(n // tm, 1, 2 * tm)
    zeros = jnp.zeros((n_slots, d), F32)
    return pl.pallas_call(
        _dispatch_kernel,
        grid=(n // tm,),
        in_specs=[pl.BlockSpec((None, 1, 2 * tm), lambda i: (i, 0, 0), memory_space=pltpu.SMEM),
                  pl.BlockSpec((tm, d), lambda i: (i, 0)),
                  pl.BlockSpec(memory_space=pl.ANY)],
        out_specs=pl.BlockSpec(memory_space=pl.ANY),
        out_shape=jax.ShapeDtypeStruct((n_slots, d), F32),
        scratch_shapes=[pltpu.SemaphoreType.DMA(())],
        input_output_aliases={2: 0},
        compiler_params=_cparams("arbitrary"),
        name="moe_dispatch",
    )(dest3, h2, zeros)


def _experts_kernel(be_ref, nu_ref, x_ref, w1_ref, w3_ref, w2_ref, o_ref):
    i = pl.program_id(0)

    @pl.when(i < nu_ref[0])
    def _():
        x = x_ref[...].astype(BF16)
        a = jnp.dot(x, w1_ref[...].astype(BF16), preferred_element_type=F32)
        b = jnp.dot(x, w3_ref[...].astype(BF16), preferred_element_type=F32)
        hmid = (a * jax.nn.sigmoid(a)) * b
        o_ref[...] = jnp.dot(hmid.astype(BF16), w2_ref[...].astype(BF16), preferred_element_type=F32)

    @pl.when(i >= nu_ref[0])
    def _():
        o_ref[...] = jnp.zeros_like(o_ref)


def _experts(xs, block_e, n_used, w1_e, w3_e, w2_e):
    n_slots, d = xs.shape
    nb = n_slots // MOE_BLOCK
    de = w1_e.shape[2]
    grid_spec = pltpu.PrefetchScalarGridSpec(
        num_scalar_prefetch=2,
        grid=(nb,),
        in_specs=[pl.BlockSpec((MOE_BLOCK, d), lambda i, be, nu: (i, 0)),
                  pl.BlockSpec((None, d, de), lambda i, be, nu: (be[i], 0, 0)),
                  pl.BlockSpec((None, d, de), lambda i, be, nu: (be[i], 0, 0)),
                  pl.BlockSpec((None, de, d), lambda i, be, nu: (be[i], 0, 0))],
        out_specs=pl.BlockSpec((MOE_BLOCK, d), lambda i, be, nu: (i, 0)),
    )
    return pl.pallas_call(
        _experts_kernel,
        grid_spec=grid_spec,
        out_shape=jax.ShapeDtypeStruct((n_slots, d), F32),
        compiler_params=_cparams("arbitrary"),
        name="moe_experts",
    )(block_e, n_used, xs, w1_e, w3_e, w2_e)


def _combine_kernel(dest_ref, r_ref, x_ref, gate_ref, gf_ref, ys_ref, o_ref, buf1, buf2, sem):
    tm = x_ref.shape[0]

    def start(r, carry):
        _row_copy(ys_ref, buf1, sem, dest_ref[0, 2 * r], r).start(priority=0)
        _row_copy(ys_ref, buf2, sem, dest_ref[0, 2 * r + 1], r).start(priority=1)
        return carry

    lax.fori_loop(0, tm, start, 0, unroll=DMA_UNROLL)

    def wait(r, carry):
        _row_copy(ys_ref, buf1, sem, 0, 0).wait()
        _row_copy(ys_ref, buf2, sem, 0, 0).wait()
        return carry

    lax.fori_loop(0, tm, wait, 0, unroll=DMA_UNROLL)
    rec = r_ref[...]
    y = buf1[...] * rec[:, ROUTE_W1:ROUTE_W1 + 1] + buf2[...] * rec[:, ROUTE_W2:ROUTE_W2 + 1]
    x2 = x_ref[...] + gate_ref[...] * y
    o_ref[...] = x2 * lax.rsqrt(jnp.mean(x2 * x2, axis=-1, keepdims=True) + EPS) * gf_ref[...]


def _combine(ys, dest, route, x1, gate2, g_final, tm=256):
    bsz, L, d = x1.shape
    n = bsz * L
    tpb = L // tm
    dest3 = dest.reshape(n // tm, 1, 2 * tm)
    return pl.pallas_call(
        _combine_kernel,
        grid=(bsz, tpb),
        in_specs=[pl.BlockSpec((None, 1, 2 * tm), lambda b, i: (b * tpb + i, 0, 0), memory_space=pltpu.SMEM),
                  pl.BlockSpec((tm, LANES), lambda b, i: (b * tpb + i, 0)),
                  pl.BlockSpec((None, tm, d), lambda b, i: (b, i, 0)),
                  pl.BlockSpec((None, 1, d), lambda b, i: (b, 0, 0)),
                  pl.BlockSpec((1, d), lambda b, i: (0, 0)),
                  pl.BlockSpec(memory_space=pl.ANY)],
        out_specs=pl.BlockSpec((None, tm, d), lambda b, i: (b, i, 0)),
        out_shape=jax.ShapeDtypeStruct((bsz, L, d), F32),
        scratch_shapes=[pltpu.VMEM((tm, d), F32), pltpu.VMEM((tm, d), F32), pltpu.SemaphoreType.DMA(())],
        compiler_params=_cparams("arbitrary", "arbitrary"),
        name="moe_combine",
    )(dest3, route, x1, gate2, g_final.reshape(1, d), ys)


def _moe(h2, x1, gate2, g_final, w_group, b_group, w_router, b_router, w1_e, w3_e, w2_e):
    bsz, L, d = x1.shape
    n = bsz * L
    h2f = h2.reshape(n, d)
    route = _router(h2f, w_group, b_group, w_router, b_router)
    dest_rec, counts = _slots(route)
    dest = dest_rec[:, :2].reshape(2 * n)
    nb = (2 * n) // MOE_BLOCK + N_EXPERTS
    cnt = counts[0, :N_EXPERTS].astype(jnp.int32)
    blocks_per_e = (cnt + MOE_BLOCK - 1) // MOE_BLOCK
    ends = jnp.cumsum(blocks_per_e)
    block_e = jnp.clip(jnp.searchsorted(ends, jnp.arange(nb, dtype=jnp.int32), side='right'),
                       0, N_EXPERTS - 1).astype(jnp.int32)
    n_used = ends[-1:].astype(jnp.int32)
    xs = _dispatch(h2f, dest, nb * MOE_BLOCK)
    ys = _experts(xs, block_e, n_used, w1_e, w3_e, w2_e)
    return _combine(ys, dest, route, x1, gate2, g_final)


def kernel(x, c, ctx, c_ctx, w_mod, b_mod, g_norm1, g_norm2, w_in, b_in, w_qk_conv, b_qk_conv,
           w_h_conv, b_h_conv, hf_w1, hf_b1, hf_w2, hf_b2, hf_w3, hf_freq, h_bias, w_a, w_b, w_out,
           w_group, b_group, w_router, b_router, w1_e, w3_e, w2_e, g_final):
    assert w_mod.shape[0] == 1, "single-layer block"
    (w_mod, b_mod, g_norm1, g_norm2, w_in, b_in, w_qk_conv, b_qk_conv, w_h_conv, b_h_conv, hf_w1, hf_b1, hf_w2,
     hf_b2, hf_w3, hf_freq, h_bias, w_a, w_b, w_out, w_group, b_group, w_router, b_router, w1_e, w3_e, w2_e) = (
        t[0] for t in (w_mod, b_mod, g_norm1, g_norm2, w_in, b_in, w_qk_conv, b_qk_conv, w_h_conv, b_h_conv,
                       hf_w1, hf_b1, hf_w2, hf_b2, hf_w3, hf_freq, h_bias, w_a, w_b, w_out, w_group, b_group,
                       w_router, b_router, w1_e, w3_e, w2_e))
    bsz, L, d = x.shape
    lc = ctx.shape[1]
    seg = L // (L // GRID_W)
    assert bsz + 1 <= 8 and lc == MLSTM_CHUNK and L % MLSTM_CHUNK == 0

    cond = jnp.zeros((8, d), F32).at[:bsz].set(c).at[bsz].set(c_ctx)
    mod = _adaln(cond, w_mod, b_mod).reshape(8, 6, d)
    modx = mod[:bsz]
    shift1, scale1, gate1, shift2, scale2, gate2 = (modx[:, i:i + 1] for i in range(6))
    shift1c = jnp.broadcast_to(mod[bsz, 0].reshape(1, 1, d), (bsz, 1, d))
    scale1c = jnp.broadcast_to(mod[bsz, 1].reshape(1, 1, d), (bsz, 1, d))

    w_in16 = w_in.astype(BF16)
    k_scale = jnp.full((M_WIDTH,), M_HEAD_DIM ** -0.5, F32)
    qk_scale = jnp.concatenate([jnp.ones((M_WIDTH,), F32), k_scale])
    w_gates, b_gates = w_in[:, IG0:M_COLS], b_in[IG0:M_COLS]

    hc = _norm_mod(ctx, g_norm1, shift1c, scale1c, lc)
    kc = _proj_conv_silu(hc, w_in16[:, K0:V0], b_in[K0:V0], w_qk_conv[:, M_WIDTH:], b_qk_conv[M_WIDTH:],
                         k_scale, lc, lc)
    vc = _proj_act(hc, w_in16[:, V0:O0], b_in[V0:O0], "none", BF16, lc)
    bcc, acc, arc = _gates(hc, w_gates, b_gates, MLSTM_CHUNK)
    zero_state = (jnp.zeros((bsz, 2, M_HEADS, M_HEAD_DIM, M_HEAD_DIM), F32),
                  jnp.zeros((bsz, 2, M_HEADS, 1, M_HEAD_DIM), F32),
                  jnp.zeros((bsz, 2, M_HEADS, 1, LANES), F32))
    _, ctx_state = _mlstm(None, (kc, 0), (vc, 0), bcc, acc, arc, zero_state, False)

    tm = 1024
    h = _norm_mod(x, g_norm1, shift1, scale1, tm)
    qk = _proj_conv_silu(h, w_in16[:, Q0:V0], b_in[Q0:V0], w_qk_conv, b_qk_conv, qk_scale, seg, tm)
    v = _proj_act(h, w_in16[:, V0:O0], b_in[V0:O0], "none", BF16, tm)
    o = _proj_act(h, w_in16[:, O0:IG0], b_in[O0:IG0], "sigmoid", BF16, tm)
    gab = _proj_act(h, w_in16[:, GA0:IN_COLS], b_in[GA0:IN_COLS], "sigmoid", BF16, tm)
    bc, ac, ar = _gates(h, w_gates, b_gates, MLSTM_CHUNK)
    hdirs, _ = _mlstm((qk, 0), (qk, 1), (v, 0), bc, ac, ar, ctx_state, True)

    x0, s = _proj_hyena(h, w_in16[:, HY0:GA0], b_in[HY0:GA0], w_h_conv, b_h_conv, seg, tm)
    kern, sumsq = _hyena_filter(L, hf_w1, hf_b1, hf_w2, hf_b2, hf_w3, hf_freq)
    y = _hyena_long_conv(s, kern)
    yscale = lax.rsqrt(sumsq + EPS) * (1.0 / (2 * L))

    x1, h2 = _merge(hdirs, o, x0, s, y, gab, x, yscale, h_bias, gate1, g_norm2, shift2, scale2,
                    w_a.astype(BF16), w_b.astype(BF16), w_out.astype(BF16))
    return _moe(h2, x1, gate2, g_final, w_group, b_group, w_router, b_router, w1_e, w3_e, w2_e)
```

```python
import functools
import math

import jax
import jax.numpy as jnp
import numpy as np
from jax import lax
from jax.experimental import pallas as pl
from jax.experimental.pallas import tpu as pltpu

F32 = jnp.float32
BF16 = jnp.bfloat16

D_MODEL = 1024
GRID_W = 64
EPS = 1e-6
M_HEADS = 4
M_HEAD_DIM = 256
M_WIDTH = M_HEADS * M_HEAD_DIM
H_WIDTH = 1024
H_POS_BANDS = 16
H_FILTER_HIDDEN = 64
H_FAST_DECAY_PCT = 0.3
H_SLOW_DECAY_PCT = 1.5
H_DECAY_TARGET = 1e-2
N_GROUPS = 8
EXPERTS_PER_GROUP = 8
N_EXPERTS = N_GROUPS * EXPERTS_PER_GROUP
D_EXPERT = 512
Q0 = 0
K0 = Q0 + M_WIDTH
V0 = K0 + M_WIDTH
O0 = V0 + M_WIDTH
IG0 = O0 + M_WIDTH
FG0 = IG0 + 2 * M_HEADS
M_COLS = FG0 + 2 * M_HEADS
HY0 = M_COLS
GA0 = HY0 + 3 * H_WIDTH
GB0 = GA0 + D_MODEL
IN_COLS = GB0 + D_MODEL

LANES = 128
MLSTM_CHUNK = 256
NEG_BIG = -1e30
VMEM_LIMIT = 48 * 1024 * 1024


def _cparams(*sem):
    return pltpu.CompilerParams(dimension_semantics=sem, vmem_limit_bytes=VMEM_LIMIT)


def _adaln_kernel(c_ref, w_ref, b_ref, o_ref):
    s = c_ref[...]
    s = s * jax.nn.sigmoid(s)
    o_ref[...] = jnp.dot(s.astype(BF16), w_ref[...].astype(BF16), preferred_element_type=F32) + b_ref[...]


def _adaln(cond, w_mod, b_mod):
    n = w_mod.shape[1]
    tn = 1536
    return pl.pallas_call(
        _adaln_kernel,
        grid=(n // tn,),
        in_specs=[pl.BlockSpec((8, D_MODEL), lambda j: (0, 0)),
                  pl.BlockSpec((D_MODEL, tn), lambda j: (0, j)),
                  pl.BlockSpec((1, tn), lambda j: (0, j))],
        out_specs=pl.BlockSpec((8, tn), lambda j: (0, j)),
        out_shape=jax.ShapeDtypeStruct((8, n), F32),
        compiler_params=_cparams("arbitrary"),
        name="adaln",
    )(cond, w_mod, b_mod.reshape(1, n))


def _norm_mod_kernel(x_ref, g_ref, sh_ref, sc_ref, o_ref):
    x = x_ref[...]
    y = x * lax.rsqrt(jnp.mean(x * x, axis=-1, keepdims=True) + EPS)
    y = y * g_ref[...]
    o_ref[...] = (y * (1.0 + sc_ref[...]) + sh_ref[...]).astype(o_ref.dtype)


def _norm_mod(x, g, shift, scale, tm):
    bsz, L, d = x.shape
    return pl.pallas_call(
        _norm_mod_kernel,
        grid=(bsz, L // tm),
        in_specs=[pl.BlockSpec((None, tm, d), lambda b, i: (b, i, 0)),
                  pl.BlockSpec((1, d), lambda b, i: (0, 0)),
                  pl.BlockSpec((None, 1, d), lambda b, i: (b, 0, 0)),
                  pl.BlockSpec((None, 1, d), lambda b, i: (b, 0, 0))],
        out_specs=pl.BlockSpec((None, tm, d), lambda b, i: (b, i, 0)),
        out_shape=jax.ShapeDtypeStruct((bsz, L, d), BF16),
        compiler_params=_cparams("parallel", "parallel"),
        name="norm_mod",
    )(x, g.reshape(1, d), shift, scale)


def _conv3(z, wc, bc, seg):
    tm = z.shape[0]
    pos = lax.broadcasted_iota(jnp.int32, z.shape, 0) & (seg - 1)
    zp = jnp.where(pos == 0, 0.0, pltpu.roll(z, 1, 0))
    zn = jnp.where(pos == seg - 1, 0.0, pltpu.roll(z, tm - 1, 0))
    return zp * wc[0:1, :] + z * wc[1:2, :] + zn * wc[2:3, :] + bc


def _proj_act_kernel(h_ref, w_ref, b_ref, o_ref, *, act):
    z = jnp.dot(h_ref[...], w_ref[...], preferred_element_type=F32) + b_ref[...]
    if act == "sigmoid":
        z = jax.nn.sigmoid(z)
    o_ref[...] = z.astype(o_ref.dtype)


def _proj_act(h, w, b, act, out_dtype, tm, tn=512):
    bsz, L, d = h.shape
    n = w.shape[1]
    return pl.pallas_call(
        functools.partial(_proj_act_kernel, act=act),
        grid=(bsz, L // tm, n // tn),
        in_specs=[pl.BlockSpec((None, tm, d), lambda b_, i, j: (b_, i, 0)),
                  pl.BlockSpec((d, tn), lambda b_, i, j: (0, j)),
                  pl.BlockSpec((1, tn), lambda b_, i, j: (0, j))],
        out_specs=pl.BlockSpec((None, tm, tn), lambda b_, i, j: (b_, i, j)),
        out_shape=jax.ShapeDtypeStruct((bsz, L, n), out_dtype),
        compiler_params=_cparams("parallel", "parallel", "arbitrary"),
        name="proj_" + act,
    )(h, w, b.reshape(1, n))


def _proj_conv_silu_kernel(h_ref, w_ref, b_ref, wc_ref, bc_ref, cs_ref, o_ref, *, seg):
    z = jnp.dot(h_ref[...], w_ref[...], preferred_element_type=F32) + b_ref[...]
    y = _conv3(z, wc_ref[...], bc_ref[...], seg)
    y = y * jax.nn.sigmoid(y)
    o_ref[...] = (y * cs_ref[...]).astype(o_ref.dtype)


def _proj_conv_silu(h, w, b, wc, bc, colscale, seg, tm, tn=512):
    bsz, L, d = h.shape
    n = w.shape[1]
    col = lambda b_, i, j: (0, j)
    return pl.pallas_call(
        functools.partial(_proj_conv_silu_kernel, seg=seg),
        grid=(bsz, L // tm, n // tn),
        in_specs=[pl.BlockSpec((None, tm, d), lambda b_, i, j: (b_, i, 0)),
                  pl.BlockSpec((d, tn), col),
                  pl.BlockSpec((1, tn), col),
                  pl.BlockSpec((3, tn), col),
                  pl.BlockSpec((1, tn), col),
                  pl.BlockSpec((1, tn), col)],
        out_specs=pl.BlockSpec((None, tm, tn), lambda b_, i, j: (b_, i, j)),
        out_shape=jax.ShapeDtypeStruct((bsz, L, n), BF16),
        compiler_params=_cparams("parallel", "parallel", "arbitrary"),
        name="proj_conv_silu",
    )(h, w, b.reshape(1, n), wc, bc.reshape(1, n), colscale.reshape(1, n))


PROJ_TN = 1024
PROJ_SUB = 512
PM_Q, PM_K, PM_V, PM_O, PM_GA, PM_GB = range(6)


def _proj_main_kernel(h_ref, w_ref, b_ref, wc_ref, bc_ref, cs_ref, o_ref, *, seg):
    j = pl.program_id(2)

    def run(epilogue):
        for c in range(PROJ_TN // PROJ_SUB):
            sl = slice(c * PROJ_SUB, (c + 1) * PROJ_SUB)
            z = jnp.dot(h_ref[...], w_ref[:, sl], preferred_element_type=F32) + b_ref[:, sl]
            o_ref[:, sl] = epilogue(z, sl).astype(o_ref.dtype)

    def conv_silu(z, sl):
        y = _conv3(z, wc_ref[:, sl], bc_ref[:, sl], seg)
        return (y * jax.nn.sigmoid(y)) * cs_ref[:, sl]

    @pl.when(j <= PM_K)
    def _():
        run(conv_silu)

    @pl.when(j == PM_V)
    def _():
        run(lambda z, sl: z)

    @pl.when(j >= PM_O)
    def _():
        run(lambda z, sl: jax.nn.sigmoid(z))


def _proj_main(h, w, b, wc, bc, colscale, seg, tm):
    bsz, L, d = h.shape
    n = w.shape[1]
    qk = lambda b_, i, j: (0, jnp.minimum(j, PM_K))
    return pl.pallas_call(
        functools.partial(_proj_main_kernel, seg=seg),
        grid=(bsz, L // tm, n // PROJ_TN),
        in_specs=[pl.BlockSpec((None, tm, d), lambda b_, i, j: (b_, i, 0)),
                  pl.BlockSpec((d, PROJ_TN), lambda b_, i, j: (0, j)),
                  pl.BlockSpec((1, PROJ_TN), lambda b_, i, j: (0, j)),
                  pl.BlockSpec((3, PROJ_TN), qk),
                  pl.BlockSpec((1, PROJ_TN), qk),
                  pl.BlockSpec((1, PROJ_TN), qk)],
        out_specs=pl.BlockSpec((None, tm, PROJ_TN), lambda b_, i, j: (b_, i, j)),
        out_shape=jax.ShapeDtypeStruct((bsz, L, n), BF16),
        compiler_params=_cparams("parallel", "parallel", "arbitrary"),
        name="proj_main",
    )(h, w, b.reshape(1, n), wc, bc.reshape(1, -1), colscale.reshape(1, -1))


def _proj_hyena_kernel(h_ref, w0_ref, w1_ref, w2_ref, b_ref, wc_ref, bc_ref, x0_ref, s_ref, *, seg):
    h = h_ref[...]
    us = []
    for g, w_ref in enumerate((w0_ref, w1_ref, w2_ref)):
        z = jnp.dot(h, w_ref[...], preferred_element_type=F32) + b_ref[g]
        us.append(_conv3(z, wc_ref[g], bc_ref[g], seg))
    x0_ref[...] = us[0].astype(x0_ref.dtype)
    s_ref[...] = us[1] * us[2]


def _proj_hyena(h, w, b, wc, bc, seg, tm, tn=512):
    bsz, L, d = h.shape
    nblk = H_WIDTH // tn
    b3 = b.reshape(3, 1, H_WIDTH)
    wc3 = wc.reshape(3, 3, H_WIDTH).transpose(1, 0, 2)
    bc3 = bc.reshape(3, 1, H_WIDTH)
    out_spec = pl.BlockSpec((None, tm, tn), lambda b_, i, j: (b_, i, j))
    return pl.pallas_call(
        functools.partial(_proj_hyena_kernel, seg=seg),
        grid=(bsz, L // tm, nblk),
        in_specs=[pl.BlockSpec((None, tm, d), lambda b_, i, j: (b_, i, 0)),
                  pl.BlockSpec((d, tn), lambda b_, i, j: (0, j)),
                  pl.BlockSpec((d, tn), lambda b_, i, j: (0, nblk + j)),
                  pl.BlockSpec((d, tn), lambda b_, i, j: (0, 2 * nblk + j)),
                  pl.BlockSpec((3, 1, tn), lambda b_, i, j: (0, 0, j)),
                  pl.BlockSpec((3, 3, tn), lambda b_, i, j: (0, 0, j)),
                  pl.BlockSpec((3, 1, tn), lambda b_, i, j: (0, 0, j))],
        out_specs=[out_spec, out_spec],
        out_shape=[jax.ShapeDtypeStruct((bsz, L, H_WIDTH), BF16),
                   jax.ShapeDtypeStruct((bsz, L, H_WIDTH), F32)],
        compiler_params=_cparams("parallel", "parallel", "arbitrary"),
        name="proj_hyena",
    )(h, w, w, w, b3, wc3, bc3)


N_GATES = 4 * M_HEADS


def _split3(x):
    hi = x.astype(BF16)
    r1 = x - hi.astype(F32)
    mid = r1.astype(BF16)
    lo = (r1 - mid.astype(F32)).astype(BF16)
    return hi, mid, lo


def _log_sigmoid(x):
    return jnp.minimum(x, 0.0) - jnp.log1p(jnp.exp(-jnp.abs(x)))


def _gates_kernel(h_ref, w_ref, wt_ref, b_ref, bt_ref, bc_ref, ac_ref, ar_ref):
    h = h_ref[...]
    t = h.shape[0]
    z = jnp.dot(h, w_ref[...], preferred_element_type=F32) + b_ref[...]
    zt = lax.dot_general(wt_ref[...], h, (((1,), (1,)), ((), ())),
                         preferred_element_type=F32) + bt_ref[...]
    r = lax.broadcasted_iota(jnp.int32, (t, t), 0)
    c = lax.broadcasted_iota(jnp.int32, (t, t), 1)
    lower = (r >= c).astype(BF16)
    upper = (r <= c).astype(BF16)

    lf = _log_sigmoid(z)
    parts = _split3(lf)
    cf = sum(jnp.dot(lower, p, preferred_element_type=F32) for p in parts)
    cb = sum(jnp.dot(upper, p, preferred_element_type=F32) for p in parts)
    lane = lax.broadcasted_iota(jnp.int32, z.shape, 1)
    bc = jnp.where(lane < FG_LANE0 + M_HEADS, cf, cb)
    bc = pltpu.roll(bc, LANES - FG_LANE0, 1)
    bc_ref[...] = bc
    ac_ref[...] = z - bc

    lft = _log_sigmoid(zt[FG_LANE0:, :])
    tparts = _split3(lft)
    cft = sum(jnp.dot(p, upper, preferred_element_type=F32) for p in tparts)
    cbt = sum(jnp.dot(p, lower, preferred_element_type=F32) for p in tparts)
    row = lax.broadcasted_iota(jnp.int32, cft.shape, 0)
    ar_ref[...] = zt[:FG_LANE0, :] - jnp.where(row < M_HEADS, cft, cbt)


FG_LANE0 = 2 * M_HEADS


def _gates(h, w_g, b_g, chunk):
    bsz, L, d = h.shape
    w_pad = jnp.zeros((d, LANES), F32).at[:, :N_GATES].set(w_g).astype(BF16)
    b_pad = jnp.zeros((1, LANES), F32).at[0, :N_GATES].set(b_g)
    wt = w_g.T.astype(BF16)
    bt = b_g.reshape(N_GATES, 1)
    tok = pl.BlockSpec((None, chunk, LANES), lambda b_, i: (b_, i, 0))
    return pl.pallas_call(
        _gates_kernel,
        grid=(bsz, L // chunk),
        in_specs=[pl.BlockSpec((None, chunk, d), lambda b_, i: (b_, i, 0)),
                  pl.BlockSpec((d, LANES), lambda b_, i: (0, 0)),
                  pl.BlockSpec((N_GATES, d), lambda b_, i: (0, 0)),
                  pl.BlockSpec((1, LANES), lambda b_, i: (0, 0)),
                  pl.BlockSpec((N_GATES, 1), lambda b_, i: (0, 0))],
        out_specs=[tok, tok, pl.BlockSpec((None, FG_LANE0, chunk), lambda b_, i: (b_, 0, i))],
        out_shape=[jax.ShapeDtypeStruct((bsz, L, LANES), F32),
                   jax.ShapeDtypeStruct((bsz, L, LANES), F32),
                   jax.ShapeDtypeStruct((bsz, FG_LANE0, L), F32)],
        compiler_params=_cparams("parallel", "parallel"),
        name="mlstm_gates",
    )(h, w_pad, wt, b_pad, bt)


def _mlstm_kernel(*refs, emit_h, n_chunks):
    if emit_h:
        (q_ref, k_ref, v_ref, bc_ref, ac_ref, ar_ref, c0_ref, n0_ref, m0_ref,
         h_ref, cf_ref, nf_ref, mf_ref, c_sc, n_sc, m_sc) = refs
    else:
        (k_ref, v_ref, bc_ref, ac_ref, ar_ref, c0_ref, n0_ref, m0_ref,
         cf_ref, nf_ref, mf_ref, c_sc, n_sc, m_sc) = refs
    d = pl.program_id(1)
    j = pl.program_id(2)
    fwd = d == 0
    t = k_ref.shape[0]
    dh = M_HEAD_DIM

    @pl.when(j == 0)
    def _():
        c_sc[...] = c0_ref[...]
        n_sc[...] = n0_ref[...]
        m_sc[...] = m0_ref[...]

    r = lax.broadcasted_iota(jnp.int32, (t, t), 0)
    c = lax.broadcasted_iota(jnp.int32, (t, t), 1)
    causal = jnp.where(fwd, r - c, c - r) >= 0
    bc_all = bc_ref[...]
    ac_all = ac_ref[...]
    ar_all = ar_ref[...]
    for hd in range(M_HEADS):
        sl = slice(hd * dh, (hd + 1) * dh)
        bc = jnp.where(fwd, bc_all[:, hd:hd + 1], bc_all[:, M_HEADS + hd:M_HEADS + hd + 1])
        ac = jnp.where(fwd, ac_all[:, hd:hd + 1], ac_all[:, M_HEADS + hd:M_HEADS + hd + 1])
        ar = jnp.where(fwd, ar_all[hd:hd + 1, :], ar_all[M_HEADS + hd:M_HEADS + hd + 1, :])
        b_tot = jnp.where(fwd, bc[t - 1:t, :], bc[0:1, :])
        m_prev = m_sc[hd][:, 0:1]
        k_h = k_ref[:, sl]
        v_h = v_ref[:, sl]
        if emit_h:
            q_h = q_ref[:, sl]
            dm = jnp.where(causal, bc + ar, NEG_BIG)
            inter = bc + m_prev
            m_t = jnp.maximum(inter, jnp.max(dm, axis=1, keepdims=True))
            qk = lax.dot_general(q_h, k_h, (((1,), (1,)), ((), ())), preferred_element_type=F32)
            s = qk * jnp.exp(dm - m_t)
            carry = jnp.exp(inter - m_t)
            num = (jnp.dot(s.astype(BF16), v_h, preferred_element_type=F32)
                   + carry * jnp.dot(q_h, c_sc[hd].astype(BF16), preferred_element_type=F32))
            den = (jnp.sum(s, axis=1, keepdims=True)
                   + carry * jnp.sum(q_h.astype(F32) * n_sc[hd], axis=1, keepdims=True))
            h_ref[:, sl] = (num / jnp.maximum(jnp.abs(den), jnp.exp(-m_t))).astype(h_ref.dtype)
        g = b_tot + ac
        m_new = jnp.maximum(b_tot + m_prev, jnp.max(g, axis=0, keepdims=True))
        wgt = jnp.exp(g - m_new)
        decay = jnp.exp(b_tot + m_prev - m_new)
        kw = k_h.astype(F32) * wgt
        c_sc[hd] = decay * c_sc[hd] + lax.dot_general(kw.astype(BF16), v_h, (((0,), (0,)), ((), ())),
                                                      preferred_element_type=F32)
        n_sc[hd] = decay * n_sc[hd] + jnp.sum(kw, axis=0, keepdims=True)
        m_sc[hd] = jnp.broadcast_to(m_new, (1, LANES))

    @pl.when(j == n_chunks - 1)
    def _():
        cf_ref[...] = c_sc[...]
        nf_ref[...] = n_sc[...]
        mf_ref[...] = m_sc[...]


def _mlstm(q, k, v, bc, ac, ar, state, emit_h):
    bsz, L, _ = k[0].shape
    t = MLSTM_CHUNK
    nc = L // t
    seq = lambda b_, d, j: (b_, j + d * (nc - 1 - 2 * j), 0)
    st = lambda b_, d, j: (b_, d, 0, 0, 0)

    def tok(col):
        return pl.BlockSpec((None, t, M_WIDTH), lambda b_, d, j: (b_, j + d * (nc - 1 - 2 * j), col))

    gate_spec = pl.BlockSpec((None, t, LANES), seq)
    ar_spec = pl.BlockSpec((None, FG_LANE0, t), lambda b_, d, j: (b_, 0, j + d * (nc - 1 - 2 * j)))
    c_spec = pl.BlockSpec((None, None, M_HEADS, M_HEAD_DIM, M_HEAD_DIM), st)
    n_spec = pl.BlockSpec((None, None, M_HEADS, 1, M_HEAD_DIM), st)
    m_spec = pl.BlockSpec((None, None, M_HEADS, 1, LANES), st)
    state_shapes = [jax.ShapeDtypeStruct((bsz, 2, M_HEADS, M_HEAD_DIM, M_HEAD_DIM), F32),
                    jax.ShapeDtypeStruct((bsz, 2, M_HEADS, 1, M_HEAD_DIM), F32),
                    jax.ShapeDtypeStruct((bsz, 2, M_HEADS, 1, LANES), F32)]
    in_specs = [tok(k[1]), tok(v[1]), gate_spec, gate_spec, ar_spec, c_spec, n_spec, m_spec]
    args = [k[0], v[0], bc, ac, ar, *state]
    out_specs = [c_spec, n_spec, m_spec]
    out_shape = list(state_shapes)
    if emit_h:
        in_specs = [tok(q[1])] + in_specs
        args = [q[0]] + args
        out_specs = [pl.BlockSpec((None, None, t, M_WIDTH),
                                  lambda b_, d, j: (d, b_, j + d * (nc - 1 - 2 * j), 0))] + out_specs
        out_shape = [jax.ShapeDtypeStruct((2, bsz, L, M_WIDTH), BF16)] + out_shape
    outs = pl.pallas_call(
        functools.partial(_mlstm_kernel, emit_h=emit_h, n_chunks=nc),
        grid=(bsz, 2, nc),
        in_specs=in_specs,
        out_specs=out_specs,
        out_shape=out_shape,
        scratch_shapes=[pltpu.VMEM((M_HEADS, M_HEAD_DIM, M_HEAD_DIM), F32),
                        pltpu.VMEM((M_HEADS, 1, M_HEAD_DIM), F32),
                        pltpu.VMEM((M_HEADS, 1, LANES), F32)],
        compiler_params=_cparams("parallel", "parallel", "arbitrary"),
        name="mlstm" if emit_h else "mlstm_state",
    )(*args)
    if emit_h:
        return outs[0], tuple(outs[1:])
    return None, tuple(outs)


FILT_ROWS = 512


def _hyena_filter_kernel(bands_ref, w1_ref, b1_ref, w2_ref, b2_ref, w3_ref, fr_ref, dl_ref, k_ref, ss_ref, *, L):
    i = pl.program_id(0)
    tr = k_ref.shape[0]
    idx = i * tr + lax.broadcasted_iota(jnp.int32, (tr, 1), 0)
    p = jnp.where(idx < L, idx, 2 * L - idx).astype(F32)
    t = p / float(max(L - 1, 1))
    ang = ((2 * math.pi / L) * p) * bands_ref[...]
    lane = lax.broadcasted_iota(jnp.int32, ang.shape, 1)
    feats = jnp.where(lane == 0, t,
                      jnp.where(lane <= H_POS_BANDS, jnp.cos(ang),
                                jnp.where(lane <= 2 * H_POS_BANDS, -jnp.sin(ang), 0.0)))
    fr = fr_ref[...]
    hid = jnp.sin(fr * (jnp.dot(feats.astype(BF16), w1_ref[...], preferred_element_type=F32) + b1_ref[...]))
    hid = jnp.sin(fr * (jnp.dot(hid.astype(BF16), w2_ref[...], preferred_element_type=F32) + b2_ref[...]))
    filt = jnp.dot(hid.astype(BF16), w3_ref[...], preferred_element_type=F32)
    kern = filt * jnp.exp(-t * jnp.abs(dl_ref[...]))
    kern = jnp.where(idx == L, 0.0, kern)
    k_ref[...] = kern

    @pl.when(i == 0)
    def _():
        ss_ref[...] = jnp.zeros_like(ss_ref)

    ss_ref[...] += jnp.sum(kern * kern, axis=0, keepdims=True)


def _hyena_filter(L, w1, b1, w2, b2, w3, freq):
    hid = H_FILTER_HIDDEN
    bands = jnp.linspace(1e-4, H_POS_BANDS - 1, H_POS_BANDS, dtype=F32)
    bands_row = jnp.zeros((1, LANES), F32).at[0, 1:1 + H_POS_BANDS].set(bands).at[
        0, 1 + H_POS_BANDS:1 + 2 * H_POS_BANDS].set(bands)
    w1p = jnp.zeros((LANES, LANES), F32).at[:w1.shape[0], :hid].set(w1).astype(BF16)
    b1p = jnp.zeros((1, LANES), F32).at[0, :hid].set(b1)
    w2p = jnp.zeros((LANES, LANES), F32).at[:hid, :hid].set(w2).astype(BF16)
    b2p = jnp.zeros((1, LANES), F32).at[0, :hid].set(b2)
    w3p = jnp.zeros((LANES, 2 * H_WIDTH), F32).at[:hid].set(w3).astype(BF16)
    frp = jnp.ones((1, LANES), F32).at[0, :hid].set(freq)
    max_decay = math.log(H_DECAY_TARGET) / H_FAST_DECAY_PCT
    min_decay = math.log(H_DECAY_TARGET) / H_SLOW_DECAY_PCT
    deltas = jnp.linspace(min_decay, max_decay, H_WIDTH, dtype=F32).reshape(1, H_WIDTH)
    nt = L // FILT_ROWS
    small = lambda i: (0, 0)
    return pl.pallas_call(
        functools.partial(_hyena_filter_kernel, L=L),
        grid=(2 * nt,),
        in_specs=[pl.BlockSpec((1, LANES), small),
                  pl.BlockSpec((LANES, LANES), small), pl.BlockSpec((1, LANES), small),
                  pl.BlockSpec((LANES, LANES), small), pl.BlockSpec((1, LANES), small),
                  pl.BlockSpec((LANES, H_WIDTH), lambda i: (0, i // nt)),
                  pl.BlockSpec((1, LANES), small),
                  pl.BlockSpec((1, H_WIDTH), small)],
        out_specs=[pl.BlockSpec((FILT_ROWS, H_WIDTH), lambda i: (i, 0)),
                   pl.BlockSpec((1, H_WIDTH), small)],
        out_shape=[jax.ShapeDtypeStruct((2 * L, H_WIDTH), F32),
                   jax.ShapeDtypeStruct((1, H_WIDTH), F32)],
        compiler_params=_cparams("arbitrary"),
        name="hyena_filter",
    )(bands_row, w1p, b1p, w2p, b2p, w3p, frp, deltas)


def _dft_factors(n):
    lg = int(round(math.log2(n)))
    n1 = 1 << ((lg + 1) // 2)
    return n1, n // n1


def _dft_outer_matrices(n1):
    k = np.arange(n1)[:, None]
    n = np.arange(n1)[None, :]
    ang = 2.0 * np.pi * ((k * n) % n1) / n1
    cr, ci = np.cos(ang), -np.sin(ang)
    h = n1 // 2
    fwd_c = np.block([[cr[:, :h], -ci[:, :h]], [ci[:, :h], cr[:, :h]]])
    fwd_r = np.concatenate([cr, ci], axis=0)
    ir, ii = cr[:h, :], -ci[:h, :]
    inv = np.block([[ir, -ii], [ii, ir]])
    return (jnp.asarray(fwd_c, F32).astype(BF16), jnp.asarray(fwd_r, F32).astype(BF16),
            jnp.asarray(inv, F32).astype(BF16))


def _dft_inner_matrices(n1, n2):
    n = n1 * n2
    k1 = jnp.arange(n1, dtype=jnp.int32)[:, None, None]
    k2 = jnp.arange(n2, dtype=jnp.int32)[None, :, None]
    m = jnp.arange(n2, dtype=jnp.int32)[None, None, :]
    ang = ((m * (k1 + n1 * k2)) % n).astype(F32) * (2.0 * math.pi / n)
    gr, gi = jnp.cos(ang), -jnp.sin(ang)
    g = jnp.concatenate([jnp.concatenate([gr, -gi], axis=2), jnp.concatenate([gi, gr], axis=2)], axis=1)
    return g.astype(BF16), jnp.swapaxes(g, 1, 2).astype(BF16)


DFT_M_TILE = 8
DFT_C_TILE = 512


def _outer_dft_kernel(l_ref, x_ref, o_ref):
    p_in, p_out = x_ref.shape[0], o_ref.shape[0]
    r_out = o_ref.shape[1]
    for mm in range(x_ref.shape[2]):
        parts = [x_ref[p, :, mm, :] for p in range(p_in)]
        x = parts[0] if p_in == 1 else jnp.concatenate(parts, axis=0)
        out = jnp.dot(l_ref[...], x.astype(BF16), preferred_element_type=F32)
        for p in range(p_out):
            o_ref[p, :, mm, :] = out[p * r_out:(p + 1) * r_out]


def _outer_dft(lmat, x4, p_out):
    p_in, r_in, n2, c = x4.shape
    r_out = lmat.shape[0] // p_out
    tc = min(DFT_C_TILE, c)
    return pl.pallas_call(
        _outer_dft_kernel,
        grid=(n2 // DFT_M_TILE, c // tc),
        in_specs=[pl.BlockSpec(lmat.shape, lambda m, j: (0, 0)),
                  pl.BlockSpec((p_in, r_in, DFT_M_TILE, tc), lambda m, j: (0, 0, m, j))],
        out_specs=pl.BlockSpec((p_out, r_out, DFT_M_TILE, tc), lambda m, j: (0, 0, m, j)),
        out_shape=jax.ShapeDtypeStruct((p_out, r_out, n2, c), F32),
        compiler_params=_cparams("parallel", "parallel"),
        name="dft_outer",
    )(lmat, x4)


def _inner_fwd_kernel(g_ref, a_ref, o_ref):
    n2 = a_ref.shape[1]
    a = jnp.concatenate([a_ref[0].astype(BF16), a_ref[1].astype(BF16)], axis=0)
    x = jnp.dot(g_ref[...], a, preferred_element_type=F32)
    o_ref[0] = x[:n2].astype(o_ref.dtype)
    o_ref[1] = x[n2:].astype(o_ref.dtype)


def _inner_fwd(g, a):
    _, n1, n2, c = a.shape
    blk = pl.BlockSpec((2, None, n2, c), lambda k: (0, k, 0, 0))
    return pl.pallas_call(
        _inner_fwd_kernel,
        grid=(n1,),
        in_specs=[pl.BlockSpec((None, 2 * n2, 2 * n2), lambda k: (k, 0, 0)), blk],
        out_specs=blk,
        out_shape=jax.ShapeDtypeStruct((2, n1, n2, c), BF16),
        compiler_params=_cparams("parallel"),
        name="dft_inner_filter",
    )(g, a)


def _inner_conv_kernel(g_ref, gt_ref, a_ref, k_ref, o_ref):
    n2 = a_ref.shape[1]
    a = jnp.concatenate([a_ref[0].astype(BF16), a_ref[1].astype(BF16)], axis=0)
    x = jnp.dot(g_ref[...], a, preferred_element_type=F32)
    xr, xi = x[:n2], x[n2:]
    kr, ki = k_ref[0].astype(F32), k_ref[1].astype(F32)
    yr = xr * kr - xi * ki
    yi = xr * ki + xi * kr
    y = jnp.concatenate([yr.astype(BF16), yi.astype(BF16)], axis=0)
    b = jnp.dot(gt_ref[...], y, preferred_element_type=F32)
    o_ref[0] = b[:n2]
    o_ref[1] = b[n2:]


def _inner_conv(g, gt, a, kf):
    _, n1, n2, c = a.shape
    blk = pl.BlockSpec((2, None, n2, c), lambda k: (0, k, 0, 0))
    mat = pl.BlockSpec((None, 2 * n2, 2 * n2), lambda k: (k, 0, 0))
    return pl.pallas_call(
        _inner_conv_kernel,
        grid=(n1,),
        in_specs=[mat, mat, blk, blk],
        out_specs=blk,
        out_shape=jax.ShapeDtypeStruct((2, n1, n2, c), F32),
        compiler_params=_cparams("parallel"),
        name="dft_inner_conv",
    )(g, gt, a, kf)


def _hyena_long_conv(s, kern):
    bsz, L, c = s.shape
    assert bsz == 2
    n = 2 * L
    n1, n2 = _dft_factors(n)
    fwd_c, fwd_r, inv = _dft_outer_matrices(n1)
    g, gt = _dft_inner_matrices(n1, n2)
    kf = _inner_fwd(g, _outer_dft(fwd_r, kern.reshape(1, n1, n2, c), 2))
    a = _outer_dft(fwd_c, s.reshape(2, n1 // 2, n2, c), 2)
    b = _inner_conv(g, gt, a, kf)
    y = _outer_dft(inv, b, 2)
    return y.reshape(2, L, c)


def _merge_kernel(hf_ref, hb_ref, o_ref, x0_ref, s_ref, y_ref, ga_ref, gb_ref, x_ref,
                  ysc_ref, hbias_ref, gate_ref, g2_ref, sh_ref, sc_ref,
                  wa_ref, wb_ref, wo_ref, x1_ref, h2_ref):
    a = o_ref[...].astype(F32) * (hf_ref[...].astype(F32) + hb_ref[...].astype(F32))
    s = s_ref[...]
    hy = x0_ref[...].astype(F32) * (y_ref[...] * ysc_ref[...] + hbias_ref[...] * s)
    pa = jnp.dot(a.astype(BF16), wa_ref[...], preferred_element_type=F32)
    pb = jnp.dot(hy.astype(BF16), wb_ref[...], preferred_element_type=F32)
    mix = ga_ref[...].astype(F32) * pa + gb_ref[...].astype(F32) * pb
    out = jnp.dot(mix.astype(BF16), wo_ref[...], preferred_element_type=F32)
    x1 = x_ref[...] + gate_ref[...] * out
    x1_ref[...] = x1
    y = x1 * lax.rsqrt(jnp.mean(x1 * x1, axis=-1, keepdims=True) + EPS) * g2_ref[...]
    h2_ref[...] = y * (1.0 + sc_ref[...]) + sh_ref[...]


def _merge(hdirs, pm, x0, s, y, x, yscale, h_bias, gate1, g2, shift2, scale2, w_a, w_b, w_out, tm=256):
    bsz, L, d = x.shape
    tok = pl.BlockSpec((None, tm, d), lambda b, i: (b, i, 0))

    def pm_tile(col):
        return pl.BlockSpec((None, tm, d), lambda b, i: (b, i, col))

    vec = pl.BlockSpec((1, d), lambda b, i: (0, 0))
    bvec = pl.BlockSpec((None, 1, d), lambda b, i: (b, 0, 0))
    wsp = pl.BlockSpec((d, d), lambda b, i: (0, 0))
    return pl.pallas_call(
        _merge_kernel,
        grid=(bsz, L // tm),
        in_specs=[pl.BlockSpec((None, None, tm, d), lambda b, i: (0, b, i, 0)),
                  pl.BlockSpec((None, None, tm, d), lambda b, i: (1, b, i, 0)),
                  pm_tile(PM_O), tok, tok, tok, pm_tile(PM_GA), pm_tile(PM_GB), tok,
                  vec, vec, bvec, vec, bvec, bvec, wsp, wsp, wsp],
        out_specs=[tok, tok],
        out_shape=[jax.ShapeDtypeStruct((bsz, L, d), F32), jax.ShapeDtypeStruct((bsz, L, d), F32)],
        compiler_params=_cparams("parallel", "parallel"),
        name="merge",
    )(hdirs, hdirs, pm, x0, s, y, pm, pm, x, yscale, h_bias.reshape(1, d), gate1, g2.reshape(1, d),
      shift2, scale2, w_a, w_b, w_out)


MOE_BLOCK = 256
ROUTE_E1, ROUTE_E2, ROUTE_W1, ROUTE_W2 = 0, 1, 2, 3
EXP_LANE0 = N_GROUPS


def _first_lane_of_max(val, valid, lane):
    masked = jnp.where(valid, val, NEG_BIG)
    mx = jnp.max(masked, axis=1, keepdims=True)
    idx = jnp.min(jnp.where(valid & (masked == mx), lane, LANES), axis=1, keepdims=True)
    return mx, idx


def _router_kernel(h_ref, w_ref, b_ref, r_ref):
    logits = jnp.dot(h_ref[...].astype(BF16), w_ref[...], preferred_element_type=F32) + b_ref[...]
    lane = lax.broadcasted_iota(jnp.int32, logits.shape, 1)
    is_g = lane < N_GROUPS
    gmax, gsel = _first_lane_of_max(logits, is_g, lane)
    gsum = jnp.sum(jnp.where(is_g, jnp.exp(logits - gmax), 0.0), axis=1, keepdims=True)
    gw = 1.0 / gsum
    lo = EXP_LANE0 + gsel * EXPERTS_PER_GROUP
    in_grp = (lane >= lo) & (lane < lo + EXPERTS_PER_GROUP)
    emax, l1 = _first_lane_of_max(logits, in_grp, lane)
    esum = jnp.sum(jnp.where(in_grp, jnp.exp(logits - emax), 0.0), axis=1, keepdims=True)
    e2max, l2 = _first_lane_of_max(logits, in_grp & (lane != l1), lane)
    v1 = 1.0 / esum
    v2 = jnp.exp(e2max - emax) / esum
    vs = v1 + v2
    w1 = gw * v1 / vs
    w2 = gw * v2 / vs
    e1 = (l1 - EXP_LANE0).astype(F32)
    e2 = (l2 - EXP_LANE0).astype(F32)
    r_ref[...] = jnp.where(lane == ROUTE_E1, e1,
                           jnp.where(lane == ROUTE_E2, e2,
                                     jnp.where(lane == ROUTE_W1, w1,
                                               jnp.where(lane == ROUTE_W2, w2, 0.0))))


def _router(h2, w_group, b_group, w_router, b_router, tm=1024):
    n, d = h2.shape
    w = jnp.zeros((d, LANES), F32).at[:, :N_GROUPS].set(w_group).at[
        :, EXP_LANE0:EXP_LANE0 + N_EXPERTS].set(w_router).astype(BF16)
    b = jnp.zeros((1, LANES), F32).at[0, :N_GROUPS].set(b_group).at[
        0, EXP_LANE0:EXP_LANE0 + N_EXPERTS].set(b_router)
    return pl.pallas_call(
        _router_kernel,
        grid=(n // tm,),
        in_specs=[pl.BlockSpec((tm, d), lambda i: (i, 0)),
                  pl.BlockSpec((d, LANES), lambda i: (0, 0)),
                  pl.BlockSpec((1, LANES), lambda i: (0, 0))],
        out_specs=pl.BlockSpec((tm, LANES), lambda i: (i, 0)),
        out_shape=jax.ShapeDtypeStruct((n, LANES), F32),
        compiler_params=_cparams("parallel"),
        name="moe_router",
    )(h2, w, b)


def _slots_kernel(r_ref, dest_ref, cnt_ref, run_sc, start_sc):
    ph = pl.program_id(0)
    i = pl.program_id(1)
    rec = r_ref[...]
    tm = rec.shape[0]
    lane = lax.broadcasted_iota(jnp.int32, rec.shape, 1)
    e1 = rec[:, ROUTE_E1:ROUTE_E1 + 1].astype(jnp.int32)
    e2 = rec[:, ROUTE_E2:ROUTE_E2 + 1].astype(jnp.int32)
    oh1 = lane == e1
    oh2 = lane == e2
    oh = (oh1 | oh2).astype(F32)

    @pl.when((ph == 0) & (i == 0))
    def _():
        run_sc[...] = jnp.zeros_like(run_sc)

    @pl.when(ph == 0)
    def _():
        run_sc[...] += jnp.sum(oh, axis=0, keepdims=True)

    @pl.when((ph == 1) & (i == 0))
    def _():
        counts = run_sc[...]
        cnt_ref[...] = counts
        nblk = jnp.floor((counts + (MOE_BLOCK - 1)) * (1.0 / MOE_BLOCK))
        rr = lax.broadcasted_iota(jnp.int32, (LANES, LANES), 0)
        cc = lax.broadcasted_iota(jnp.int32, (LANES, LANES), 1)
        before = (rr < cc).astype(BF16)
        first = jnp.dot(nblk.astype(BF16), before, preferred_element_type=F32)
        start_sc[...] = first * float(MOE_BLOCK)
        run_sc[...] = jnp.zeros_like(run_sc)

    @pl.when(ph == 1)
    def _():
        r = lax.broadcasted_iota(jnp.int32, (tm, tm), 0)
        c = lax.broadcasted_iota(jnp.int32, (tm, tm), 1)
        earlier = (r > c).astype(BF16)
        rank = jnp.dot(earlier, oh.astype(BF16), preferred_element_type=F32) + run_sc[...] + start_sc[...]
        d1 = jnp.sum(jnp.where(oh1, rank, 0.0), axis=1, keepdims=True)
        d2 = jnp.sum(jnp.where(oh2, rank, 0.0), axis=1, keepdims=True)
        dest_ref[...] = jnp.where(lane == 0, d1, jnp.where(lane == 1, d2, 0.0)).astype(jnp.int32)
        run_sc[...] += jnp.sum(oh, axis=0, keepdims=True)


def _slots(route, tm=512):
    n = route.shape[0]
    return pl.pallas_call(
        _slots_kernel,
        grid=(2, n // tm),
        in_specs=[pl.BlockSpec((tm, LANES), lambda p, i: (i, 0))],
        out_specs=[pl.BlockSpec((tm, LANES), lambda p, i: (i * p, 0)),
                   pl.BlockSpec((1, LANES), lambda p, i: (0, 0))],
        out_shape=[jax.ShapeDtypeStruct((n, LANES), jnp.int32), jax.ShapeDtypeStruct((1, LANES), F32)],
        scratch_shapes=[pltpu.VMEM((1, LANES), F32), pltpu.VMEM((1, LANES), F32)],
        compiler_params=_cparams("arbitrary", "arbitrary"),
        name="moe_slots",
    )(route)


DMA_UNROLL = 8


def _row_copy(src_ref, dst_ref, sem, src_row, dst_row):
    return pltpu.make_async_copy(src_ref.at[pl.ds(src_row, 1)], dst_ref.at[pl.ds(dst_row, 1)], sem)


def _dispatch_kernel(dest_ref, h_ref, xs_in_ref, xs_ref, sem):
    del xs_in_ref
    tm = h_ref.shape[0]

    def start(r, carry):
        _row_copy(h_ref, xs_ref, sem, r, dest_ref[0, 2 * r]).start(priority=0)
        _row_copy(h_ref, xs_ref, sem, r, dest_ref[0, 2 * r + 1]).start(priority=1)
        return carry

    lax.fori_loop(0, tm, start, 0, unroll=DMA_UNROLL)

    def wait(r, carry):
        _row_copy(h_ref, xs_ref, sem, 0, 0).wait()
        _row_copy(h_ref, xs_ref, sem, 0, 0).wait()
        return carry

    lax.fori_loop(0, tm, wait, 0, unroll=DMA_UNROLL)


def _dispatch(h2, dest, n_slots, tm=256):
    n, d = h2.shape
    dest3 = dest.reshape(n // tm, 1, 2 * tm)
    zeros = jnp.zeros((n_slots, d), F32)
    return pl.pallas_call(
        _dispatch_kernel,
        grid=(n // tm,),
        in_specs=[pl.BlockSpec((None, 1, 2 * tm), lambda i: (i, 0, 0), memory_space=pltpu.SMEM),
                  pl.BlockSpec((tm, d), lambda i: (i, 0)),
                  pl.BlockSpec(memory_space=pl.ANY)],
        out_specs=pl.BlockSpec(memory_space=pl.ANY),
        out_shape=jax.ShapeDtypeStruct((n_slots, d), F32),
        scratch_shapes=[pltpu.SemaphoreType.DMA(())],
        input_output_aliases={2: 0},
        compiler_params=_cparams("arbitrary"),
        name="moe_dispatch",
    )(dest3, h2, zeros)


def _experts_kernel(be_ref, first_ref, nxt_ref, par_ref, nu_ref, x_ref, w1_hbm, w3_hbm, w2_hbm, o_ref,
                    w1f, w3f, w2f, w1b, w3b, w2b, sems):
    i = pl.program_id(0)

    def weight_copies(e, slot):
        return (pltpu.make_async_copy(w1_hbm.at[e], w1f.at[slot], sems.at[0, slot]),
                pltpu.make_async_copy(w3_hbm.at[e], w3f.at[slot], sems.at[1, slot]),
                pltpu.make_async_copy(w2_hbm.at[e], w2f.at[slot], sems.at[2, slot]))

    @pl.when(i == 0)
    def _():
        for cp in weight_copies(be_ref[0], 0):
            cp.start()

    @pl.when(first_ref[i] == 1)
    def _():
        slot = par_ref[i]

        @pl.when(nxt_ref[i] >= 0)
        def _():
            for cp in weight_copies(nxt_ref[i], 1 - slot):
                cp.start()

        for cp in weight_copies(be_ref[i], slot):
            cp.wait()
        w1b[...] = w1f[slot].astype(BF16)
        w3b[...] = w3f[slot].astype(BF16)
        w2b[...] = w2f[slot].astype(BF16)

    @pl.when(i < nu_ref[0])
    def _():
        x = x_ref[...].astype(BF16)
        a = jnp.dot(x, w1b[...], preferred_element_type=F32)
        b = jnp.dot(x, w3b[...], preferred_element_type=F32)
        hmid = (a * jax.nn.sigmoid(a)) * b
        o_ref[...] = jnp.dot(hmid.astype(BF16), w2b[...], preferred_element_type=F32)

    @pl.when(i >= nu_ref[0])
    def _():
        o_ref[...] = jnp.zeros_like(o_ref)


def _experts(xs, block_e, n_used, w1_e, w3_e, w2_e):
    n_slots, d = xs.shape
    nb = n_slots // MOE_BLOCK
    de = w1_e.shape[2]
    idx = jnp.arange(nb, dtype=jnp.int32)
    used = idx < n_used[0]
    first = used & ((idx == 0) | (block_e != jnp.roll(block_e, 1)))
    ordinal = jnp.cumsum(first.astype(jnp.int32)) - 1
    par = (ordinal % 2).astype(jnp.int32)
    first_pos = jnp.where(first, idx, nb)
    next_first = lax.cummin(jnp.concatenate([first_pos[1:], jnp.full((1,), nb, jnp.int32)]), reverse=True)
    nxt = jnp.where(next_first < nb, block_e[jnp.minimum(next_first, nb - 1)], -1).astype(jnp.int32)
    any_spec = pl.BlockSpec(memory_space=pl.ANY)
    grid_spec = pltpu.PrefetchScalarGridSpec(
        num_scalar_prefetch=5,
        grid=(nb,),
        in_specs=[pl.BlockSpec((MOE_BLOCK, d), lambda i, *_: (i, 0)), any_spec, any_spec, any_spec],
        out_specs=pl.BlockSpec((MOE_BLOCK, d), lambda i, *_: (i, 0)),
        scratch_shapes=[pltpu.VMEM((2, d, de), F32), pltpu.VMEM((2, d, de), F32), pltpu.VMEM((2, de, d), F32),
                        pltpu.VMEM((d, de), BF16), pltpu.VMEM((d, de), BF16), pltpu.VMEM((de, d), BF16),
                        pltpu.SemaphoreType.DMA((3, 2))],
    )
    return pl.pallas_call(
        _experts_kernel,
        grid_spec=grid_spec,
        out_shape=jax.ShapeDtypeStruct((n_slots, d), F32),
        compiler_params=_cparams("arbitrary"),
        name="moe_experts",
    )(block_e, first.astype(jnp.int32), nxt, par, n_used, xs, w1_e, w3_e, w2_e)


def _combine_kernel(dest_ref, r_ref, x_ref, gate_ref, gf_ref, ys_ref, o_ref, buf1, buf2, sem):
    tm = x_ref.shape[0]

    def start(r, carry):
        _row_copy(ys_ref, buf1, sem, dest_ref[0, 2 * r], r).start(priority=0)
        _row_copy(ys_ref, buf2, sem, dest_ref[0, 2 * r + 1], r).start(priority=1)
        return carry

    lax.fori_loop(0, tm, start, 0, unroll=DMA_UNROLL)

    def wait(r, carry):
        _row_copy(ys_ref, buf1, sem, 0, 0).wait()
        _row_copy(ys_ref, buf2, sem, 0, 0).wait()
        return carry

    lax.fori_loop(0, tm, wait, 0, unroll=DMA_UNROLL)
    rec = r_ref[...]
    y = buf1[...] * rec[:, ROUTE_W1:ROUTE_W1 + 1] + buf2[...] * rec[:, ROUTE_W2:ROUTE_W2 + 1]
    x2 = x_ref[...] + gate_ref[...] * y
    o_ref[...] = x2 * lax.rsqrt(jnp.mean(x2 * x2, axis=-1, keepdims=True) + EPS) * gf_ref[...]


def _combine(ys, dest, route, x1, gate2, g_final, tm=256):
    bsz, L, d = x1.shape
    n = bsz * L
    tpb = L // tm
    dest3 = dest.reshape(n // tm, 1, 2 * tm)
    return pl.pallas_call(
        _combine_kernel,
        grid=(bsz, tpb),
        in_specs=[pl.BlockSpec((None, 1, 2 * tm), lambda b, i: (b * tpb + i, 0, 0), memory_space=pltpu.SMEM),
                  pl.BlockSpec((tm, LANES), lambda b, i: (b * tpb + i, 0)),
                  pl.BlockSpec((None, tm, d), lambda b, i: (b, i, 0)),
                  pl.BlockSpec((None, 1, d), lambda b, i: (b, 0, 0)),
                  pl.BlockSpec((1, d), lambda b, i: (0, 0)),
                  pl.BlockSpec(memory_space=pl.ANY)],
        out_specs=pl.BlockSpec((None, tm, d), lambda b, i: (b, i, 0)),
        out_shape=jax.ShapeDtypeStruct((bsz, L, d), F32),
        scratch_shapes=[pltpu.VMEM((tm, d), F32), pltpu.VMEM((tm, d), F32), pltpu.SemaphoreType.DMA(())],
        compiler_params=_cparams("arbitrary", "arbitrary"),
        name="moe_combine",
    )(dest3, route, x1, gate2, g_final.reshape(1, d), ys)


def _moe(h2, x1, gate2, g_final, w_group, b_group, w_router, b_router, w1_e, w3_e, w2_e):
    bsz, L, d = x1.shape
    n = bsz * L
    h2f = h2.reshape(n, d)
    route = _router(h2f, w_group, b_group, w_router, b_router)
    dest_rec, counts = _slots(route)
    dest = dest_rec[:, :2].reshape(2 * n)
    nb = (2 * n) // MOE_BLOCK + N_EXPERTS
    cnt = counts[0, :N_EXPERTS].astype(jnp.int32)
    blocks_per_e = (cnt + MOE_BLOCK - 1) // MOE_BLOCK
    ends = jnp.cumsum(blocks_per_e)
    block_e = jnp.clip(jnp.searchsorted(ends, jnp.arange(nb, dtype=jnp.int32), side='right'),
                       0, N_EXPERTS - 1).astype(jnp.int32)
    n_used = ends[-1:].astype(jnp.int32)
    xs = _dispatch(h2f, dest, nb * MOE_BLOCK)
    ys = _experts(xs, block_e, n_used, w1_e, w3_e, w2_e)
    return _combine(ys, dest, route, x1, gate2, g_final)


def kernel(x, c, ctx, c_ctx, w_mod, b_mod, g_norm1, g_norm2, w_in, b_in, w_qk_conv, b_qk_conv,
           w_h_conv, b_h_conv, hf_w1, hf_b1, hf_w2, hf_b2, hf_w3, hf_freq, h_bias, w_a, w_b, w_out,
           w_group, b_group, w_router, b_router, w1_e, w3_e, w2_e, g_final):
    assert w_mod.shape[0] == 1, "single-layer block"
    (w_mod, b_mod, g_norm1, g_norm2, w_in, b_in, w_qk_conv, b_qk_conv, w_h_conv, b_h_conv, hf_w1, hf_b1, hf_w2,
     hf_b2, hf_w3, hf_freq, h_bias, w_a, w_b, w_out, w_group, b_group, w_router, b_router, w1_e, w3_e, w2_e) = (
        t[0] for t in (w_mod, b_mod, g_norm1, g_norm2, w_in, b_in, w_qk_conv, b_qk_conv, w_h_conv, b_h_conv,
                       hf_w1, hf_b1, hf_w2, hf_b2, hf_w3, hf_freq, h_bias, w_a, w_b, w_out, w_group, b_group,
                       w_router, b_router, w1_e, w3_e, w2_e))
    bsz, L, d = x.shape
    lc = ctx.shape[1]
    seg = L // (L // GRID_W)
    assert bsz + 1 <= 8 and lc == MLSTM_CHUNK and L % MLSTM_CHUNK == 0

    cond = jnp.zeros((8, d), F32).at[:bsz].set(c).at[bsz].set(c_ctx)
    mod = _adaln(cond, w_mod, b_mod).reshape(8, 6, d)
    modx = mod[:bsz]
    shift1, scale1, gate1, shift2, scale2, gate2 = (modx[:, i:i + 1] for i in range(6))
    shift1c = jnp.broadcast_to(mod[bsz, 0].reshape(1, 1, d), (bsz, 1, d))
    scale1c = jnp.broadcast_to(mod[bsz, 1].reshape(1, 1, d), (bsz, 1, d))

    w_in16 = w_in.astype(BF16)
    k_scale = jnp.full((M_WIDTH,), M_HEAD_DIM ** -0.5, F32)
    qk_scale = jnp.concatenate([jnp.ones((M_WIDTH,), F32), k_scale])
    w_gates, b_gates = w_in[:, IG0:M_COLS], b_in[IG0:M_COLS]

    hc = _norm_mod(ctx, g_norm1, shift1c, scale1c, lc)
    kc = _proj_conv_silu(hc, w_in16[:, K0:V0], b_in[K0:V0], w_qk_conv[:, M_WIDTH:], b_qk_conv[M_WIDTH:],
                         k_scale, lc, lc)
    vc = _proj_act(hc, w_in16[:, V0:O0], b_in[V0:O0], "none", BF16, lc)
    bcc, acc, arc = _gates(hc, w_gates, b_gates, MLSTM_CHUNK)
    zero_state = (jnp.zeros((bsz, 2, M_HEADS, M_HEAD_DIM, M_HEAD_DIM), F32),
                  jnp.zeros((bsz, 2, M_HEADS, 1, M_HEAD_DIM), F32),
                  jnp.zeros((bsz, 2, M_HEADS, 1, LANES), F32))
    _, ctx_state = _mlstm(None, (kc, 0), (vc, 0), bcc, acc, arc, zero_state, False)

    tm = 1024
    h = _norm_mod(x, g_norm1, shift1, scale1, tm)
    w_main = jnp.concatenate([w_in16[:, Q0:IG0], w_in16[:, GA0:IN_COLS]], axis=1)
    b_main = jnp.concatenate([b_in[Q0:IG0], b_in[GA0:IN_COLS]])
    pm = _proj_main(h, w_main, b_main, w_qk_conv, b_qk_conv, qk_scale, seg, tm)
    bc, ac, ar = _gates(h, w_gates, b_gates, MLSTM_CHUNK)
    hdirs, _ = _mlstm((pm, PM_Q), (pm, PM_K), (pm, PM_V), bc, ac, ar, ctx_state, True)

    x0, s = _proj_hyena(h, w_in16[:, HY0:GA0], b_in[HY0:GA0], w_h_conv, b_h_conv, seg, tm)
    kern, sumsq = _hyena_filter(L, hf_w1, hf_b1, hf_w2, hf_b2, hf_w3, hf_freq)
    y = _hyena_long_conv(s, kern)
    yscale = lax.rsqrt(sumsq + EPS) * (1.0 / (2 * L))

    x1, h2 = _merge(hdirs, pm, x0, s, y, x, yscale, h_bias, gate1, g_norm2, shift2, scale2,
                    w_a.astype(BF16), w_b.astype(BF16), w_out.astype(BF16))
    return _moe(h2, x1, gate2, g_final, w_group, b_group, w_router, b_router, w1_e, w3_e, w2_e)
```

```python
import functools
import math

import jax
import jax.numpy as jnp
import numpy as np
from jax import lax
from jax.experimental import pallas as pl
from jax.experimental.pallas import tpu as pltpu

F32 = jnp.float32
BF16 = jnp.bfloat16

D_MODEL = 1024
GRID_W = 64
EPS = 1e-6
M_HEADS = 4
M_HEAD_DIM = 256
M_WIDTH = M_HEADS * M_HEAD_DIM
H_WIDTH = 1024
H_POS_BANDS = 16
H_FILTER_HIDDEN = 64
H_FAST_DECAY_PCT = 0.3
H_SLOW_DECAY_PCT = 1.5
H_DECAY_TARGET = 1e-2
N_GROUPS = 8
EXPERTS_PER_GROUP = 8
N_EXPERTS = N_GROUPS * EXPERTS_PER_GROUP
D_EXPERT = 512
Q0 = 0
K0 = Q0 + M_WIDTH
V0 = K0 + M_WIDTH
O0 = V0 + M_WIDTH
IG0 = O0 + M_WIDTH
FG0 = IG0 + 2 * M_HEADS
M_COLS = FG0 + 2 * M_HEADS
HY0 = M_COLS
GA0 = HY0 + 3 * H_WIDTH
GB0 = GA0 + D_MODEL
IN_COLS = GB0 + D_MODEL

LANES = 128
MLSTM_CHUNK = 256
NEG_BIG = -1e30
VMEM_LIMIT = 48 * 1024 * 1024


def _cparams(*sem):
    return pltpu.CompilerParams(dimension_semantics=sem, vmem_limit_bytes=VMEM_LIMIT)


def _adaln_kernel(c_ref, w_ref, b_ref, o_ref):
    s = c_ref[...]
    s = s * jax.nn.sigmoid(s)
    o_ref[...] = jnp.dot(s.astype(BF16), w_ref[...].astype(BF16), preferred_element_type=F32) + b_ref[...]


def _adaln(cond, w_mod, b_mod):
    n = w_mod.shape[1]
    tn = 1536
    return pl.pallas_call(
        _adaln_kernel,
        grid=(n // tn,),
        in_specs=[pl.BlockSpec((8, D_MODEL), lambda j: (0, 0)),
                  pl.BlockSpec((D_MODEL, tn), lambda j: (0, j)),
                  pl.BlockSpec((1, tn), lambda j: (0, j))],
        out_specs=pl.BlockSpec((8, tn), lambda j: (0, j)),
        out_shape=jax.ShapeDtypeStruct((8, n), F32),
        compiler_params=_cparams("arbitrary"),
        name="adaln",
    )(cond, w_mod, b_mod.reshape(1, n))


def _norm_mod_kernel(x_ref, g_ref, sh_ref, sc_ref, o_ref):
    x = x_ref[...]
    y = x * lax.rsqrt(jnp.mean(x * x, axis=-1, keepdims=True) + EPS)
    y = y * g_ref[...]
    o_ref[...] = (y * (1.0 + sc_ref[...]) + sh_ref[...]).astype(o_ref.dtype)


def _norm_mod(x, g, shift, scale, tm):
    bsz, L, d = x.shape
    return pl.pallas_call(
        _norm_mod_kernel,
        grid=(bsz, L // tm),
        in_specs=[pl.BlockSpec((None, tm, d), lambda b, i: (b, i, 0)),
                  pl.BlockSpec((1, d), lambda b, i: (0, 0)),
                  pl.BlockSpec((None, 1, d), lambda b, i: (b, 0, 0)),
                  pl.BlockSpec((None, 1, d), lambda b, i: (b, 0, 0))],
        out_specs=pl.BlockSpec((None, tm, d), lambda b, i: (b, i, 0)),
        out_shape=jax.ShapeDtypeStruct((bsz, L, d), BF16),
        compiler_params=_cparams("parallel", "parallel"),
        name="norm_mod",
    )(x, g.reshape(1, d), shift, scale)


def _conv3(z, wc, bc, seg):
    tm = z.shape[0]
    pos = lax.broadcasted_iota(jnp.int32, z.shape, 0) & (seg - 1)
    zp = jnp.where(pos == 0, 0.0, pltpu.roll(z, 1, 0))
    zn = jnp.where(pos == seg - 1, 0.0, pltpu.roll(z, tm - 1, 0))
    return zp * wc[0:1, :] + z * wc[1:2, :] + zn * wc[2:3, :] + bc


def _proj_act_kernel(h_ref, w_ref, b_ref, o_ref, *, act):
    z = jnp.dot(h_ref[...], w_ref[...], preferred_element_type=F32) + b_ref[...]
    if act == "sigmoid":
        z = jax.nn.sigmoid(z)
    o_ref[...] = z.astype(o_ref.dtype)


def _proj_act(h, w, b, act, out_dtype, tm, tn=512):
    bsz, L, d = h.shape
    n = w.shape[1]
    return pl.pallas_call(
        functools.partial(_proj_act_kernel, act=act),
        grid=(bsz, L // tm, n // tn),
        in_specs=[pl.BlockSpec((None, tm, d), lambda b_, i, j: (b_, i, 0)),
                  pl.BlockSpec((d, tn), lambda b_, i, j: (0, j)),
                  pl.BlockSpec((1, tn), lambda b_, i, j: (0, j))],
        out_specs=pl.BlockSpec((None, tm, tn), lambda b_, i, j: (b_, i, j)),
        out_shape=jax.ShapeDtypeStruct((bsz, L, n), out_dtype),
        compiler_params=_cparams("parallel", "parallel", "arbitrary"),
        name="proj_" + act,
    )(h, w, b.reshape(1, n))


def _proj_conv_silu_kernel(h_ref, w_ref, b_ref, wc_ref, bc_ref, cs_ref, o_ref, *, seg):
    z = jnp.dot(h_ref[...], w_ref[...], preferred_element_type=F32) + b_ref[...]
    y = _conv3(z, wc_ref[...], bc_ref[...], seg)
    y = y * jax.nn.sigmoid(y)
    o_ref[...] = (y * cs_ref[...]).astype(o_ref.dtype)


def _proj_conv_silu(h, w, b, wc, bc, colscale, seg, tm, tn=512):
    bsz, L, d = h.shape
    n = w.shape[1]
    col = lambda b_, i, j: (0, j)
    return pl.pallas_call(
        functools.partial(_proj_conv_silu_kernel, seg=seg),
        grid=(bsz, L // tm, n // tn),
        in_specs=[pl.BlockSpec((None, tm, d), lambda b_, i, j: (b_, i, 0)),
                  pl.BlockSpec((d, tn), col),
                  pl.BlockSpec((1, tn), col),
                  pl.BlockSpec((3, tn), col),
                  pl.BlockSpec((1, tn), col),
                  pl.BlockSpec((1, tn), col)],
        out_specs=pl.BlockSpec((None, tm, tn), lambda b_, i, j: (b_, i, j)),
        out_shape=jax.ShapeDtypeStruct((bsz, L, n), BF16),
        compiler_params=_cparams("parallel", "parallel", "arbitrary"),
        name="proj_conv_silu",
    )(h, w, b.reshape(1, n), wc, bc.reshape(1, n), colscale.reshape(1, n))


PROJ_TN = 1024
PROJ_SUB = 512
PM_Q, PM_K, PM_V, PM_O, PM_GA, PM_GB = range(6)


def _proj_main_kernel(h_ref, w_ref, b_ref, wc_ref, bc_ref, cs_ref, o_ref, *, seg):
    j = pl.program_id(2)

    def run(epilogue):
        for c in range(PROJ_TN // PROJ_SUB):
            sl = slice(c * PROJ_SUB, (c + 1) * PROJ_SUB)
            z = jnp.dot(h_ref[...], w_ref[:, sl], preferred_element_type=F32) + b_ref[:, sl]
            o_ref[:, sl] = epilogue(z, sl).astype(o_ref.dtype)

    def conv_silu(z, sl):
        y = _conv3(z, wc_ref[:, sl], bc_ref[:, sl], seg)
        return (y * jax.nn.sigmoid(y)) * cs_ref[:, sl]

    @pl.when(j <= PM_K)
    def _():
        run(conv_silu)

    @pl.when(j == PM_V)
    def _():
        run(lambda z, sl: z)

    @pl.when(j >= PM_O)
    def _():
        run(lambda z, sl: jax.nn.sigmoid(z))


def _proj_main(h, w, b, wc, bc, colscale, seg, tm):
    bsz, L, d = h.shape
    n = w.shape[1]
    qk = lambda b_, i, j: (0, jnp.minimum(j, PM_K))
    return pl.pallas_call(
        functools.partial(_proj_main_kernel, seg=seg),
        grid=(bsz, L // tm, n // PROJ_TN),
        in_specs=[pl.BlockSpec((None, tm, d), lambda b_, i, j: (b_, i, 0)),
                  pl.BlockSpec((d, PROJ_TN), lambda b_, i, j: (0, j)),
                  pl.BlockSpec((1, PROJ_TN), lambda b_, i, j: (0, j)),
                  pl.BlockSpec((3, PROJ_TN), qk),
                  pl.BlockSpec((1, PROJ_TN), qk),
                  pl.BlockSpec((1, PROJ_TN), qk)],
        out_specs=pl.BlockSpec((None, tm, PROJ_TN), lambda b_, i, j: (b_, i, j)),
        out_shape=jax.ShapeDtypeStruct((bsz, L, n), BF16),
        compiler_params=_cparams("parallel", "parallel", "arbitrary"),
        name="proj_main",
    )(h, w, b.reshape(1, n), wc, bc.reshape(1, -1), colscale.reshape(1, -1))


def _proj_hyena_kernel(h_ref, w0_ref, w1_ref, w2_ref, b_ref, wc_ref, bc_ref, x0_ref, s_ref, *, seg):
    h = h_ref[...]
    us = []
    for g, w_ref in enumerate((w0_ref, w1_ref, w2_ref)):
        z = jnp.dot(h, w_ref[...], preferred_element_type=F32) + b_ref[g]
        us.append(_conv3(z, wc_ref[g], bc_ref[g], seg))
    x0_ref[...] = us[0].astype(x0_ref.dtype)
    s_ref[...] = us[1] * us[2]


def _proj_hyena(h, w, b, wc, bc, seg, tm, tn=512):
    bsz, L, d = h.shape
    nblk = H_WIDTH // tn
    b3 = b.reshape(3, 1, H_WIDTH)
    wc3 = wc.reshape(3, 3, H_WIDTH).transpose(1, 0, 2)
    bc3 = bc.reshape(3, 1, H_WIDTH)
    out_spec = pl.BlockSpec((None, tm, tn), lambda b_, i, j: (b_, i, j))
    return pl.pallas_call(
        functools.partial(_proj_hyena_kernel, seg=seg),
        grid=(bsz, L // tm, nblk),
        in_specs=[pl.BlockSpec((None, tm, d), lambda b_, i, j: (b_, i, 0)),
                  pl.BlockSpec((d, tn), lambda b_, i, j: (0, j)),
                  pl.BlockSpec((d, tn), lambda b_, i, j: (0, nblk + j)),
                  pl.BlockSpec((d, tn), lambda b_, i, j: (0, 2 * nblk + j)),
                  pl.BlockSpec((3, 1, tn), lambda b_, i, j: (0, 0, j)),
                  pl.BlockSpec((3, 3, tn), lambda b_, i, j: (0, 0, j)),
                  pl.BlockSpec((3, 1, tn), lambda b_, i, j: (0, 0, j))],
        out_specs=[out_spec, out_spec],
        out_shape=[jax.ShapeDtypeStruct((bsz, L, H_WIDTH), BF16),
                   jax.ShapeDtypeStruct((bsz, L, H_WIDTH), F32)],
        compiler_params=_cparams("parallel", "parallel", "arbitrary"),
        name="proj_hyena",
    )(h, w, w, w, b3, wc3, bc3)


N_GATES = 4 * M_HEADS


def _split3(x):
    hi = x.astype(BF16)
    r1 = x - hi.astype(F32)
    mid = r1.astype(BF16)
    lo = (r1 - mid.astype(F32)).astype(BF16)
    return hi, mid, lo


def _log_sigmoid(x):
    return jnp.minimum(x, 0.0) - jnp.log1p(jnp.exp(-jnp.abs(x)))


def _gates_kernel(h_ref, w_ref, wt_ref, b_ref, bt_ref, bc_ref, ac_ref, ar_ref):
    h = h_ref[...]
    t = h.shape[0]
    z = jnp.dot(h, w_ref[...], preferred_element_type=F32) + b_ref[...]
    zt = lax.dot_general(wt_ref[...], h, (((1,), (1,)), ((), ())),
                         preferred_element_type=F32) + bt_ref[...]
    r = lax.broadcasted_iota(jnp.int32, (t, t), 0)
    c = lax.broadcasted_iota(jnp.int32, (t, t), 1)
    lower = (r >= c).astype(BF16)
    upper = (r <= c).astype(BF16)

    lf = _log_sigmoid(z)
    parts = _split3(lf)
    cf = sum(jnp.dot(lower, p, preferred_element_type=F32) for p in parts)
    cb = sum(jnp.dot(upper, p, preferred_element_type=F32) for p in parts)
    lane = lax.broadcasted_iota(jnp.int32, z.shape, 1)
    bc = jnp.where(lane < FG_LANE0 + M_HEADS, cf, cb)
    bc = pltpu.roll(bc, LANES - FG_LANE0, 1)
    bc_ref[...] = bc
    ac_ref[...] = z - bc

    lft = _log_sigmoid(zt[FG_LANE0:, :])
    tparts = _split3(lft)
    cft = sum(jnp.dot(p, upper, preferred_element_type=F32) for p in tparts)
    cbt = sum(jnp.dot(p, lower, preferred_element_type=F32) for p in tparts)
    row = lax.broadcasted_iota(jnp.int32, cft.shape, 0)
    ar_ref[...] = zt[:FG_LANE0, :] - jnp.where(row < M_HEADS, cft, cbt)


FG_LANE0 = 2 * M_HEADS


def _gates(h, w_g, b_g, chunk):
    bsz, L, d = h.shape
    w_pad = jnp.zeros((d, LANES), F32).at[:, :N_GATES].set(w_g).astype(BF16)
    b_pad = jnp.zeros((1, LANES), F32).at[0, :N_GATES].set(b_g)
    wt = w_g.T.astype(BF16)
    bt = b_g.reshape(N_GATES, 1)
    tok = pl.BlockSpec((None, chunk, LANES), lambda b_, i: (b_, i, 0))
    return pl.pallas_call(
        _gates_kernel,
        grid=(bsz, L // chunk),
        in_specs=[pl.BlockSpec((None, chunk, d), lambda b_, i: (b_, i, 0)),
                  pl.BlockSpec((d, LANES), lambda b_, i: (0, 0)),
                  pl.BlockSpec((N_GATES, d), lambda b_, i: (0, 0)),
                  pl.BlockSpec((1, LANES), lambda b_, i: (0, 0)),
                  pl.BlockSpec((N_GATES, 1), lambda b_, i: (0, 0))],
        out_specs=[tok, tok, pl.BlockSpec((None, FG_LANE0, chunk), lambda b_, i: (b_, 0, i))],
        out_shape=[jax.ShapeDtypeStruct((bsz, L, LANES), F32),
                   jax.ShapeDtypeStruct((bsz, L, LANES), F32),
                   jax.ShapeDtypeStruct((bsz, FG_LANE0, L), F32)],
        compiler_params=_cparams("parallel", "parallel"),
        name="mlstm_gates",
    )(h, w_pad, wt, b_pad, bt)


def _mlstm_kernel(*refs, emit_h, n_chunks):
    if emit_h:
        (q_ref, k_ref, v_ref, bc_ref, ac_ref, ar_ref, c0_ref, n0_ref, m0_ref,
         h_ref, cf_ref, nf_ref, mf_ref, c_sc, n_sc, m_sc) = refs
    else:
        (k_ref, v_ref, bc_ref, ac_ref, ar_ref, c0_ref, n0_ref, m0_ref,
         cf_ref, nf_ref, mf_ref, c_sc, n_sc, m_sc) = refs
    d = pl.program_id(1)
    j = pl.program_id(2)
    fwd = d == 0
    t = k_ref.shape[0]
    dh = M_HEAD_DIM

    @pl.when(j == 0)
    def _():
        c_sc[...] = c0_ref[...]
        n_sc[...] = n0_ref[...]
        m_sc[...] = m0_ref[...]

    r = lax.broadcasted_iota(jnp.int32, (t, t), 0)
    c = lax.broadcasted_iota(jnp.int32, (t, t), 1)
    causal = jnp.where(fwd, r - c, c - r) >= 0
    bc_all = bc_ref[...]
    ac_all = ac_ref[...]
    ar_all = ar_ref[...]
    for hd in range(M_HEADS):
        sl = slice(hd * dh, (hd + 1) * dh)
        bc = jnp.where(fwd, bc_all[:, hd:hd + 1], bc_all[:, M_HEADS + hd:M_HEADS + hd + 1])
        ac = jnp.where(fwd, ac_all[:, hd:hd + 1], ac_all[:, M_HEADS + hd:M_HEADS + hd + 1])
        ar = jnp.where(fwd, ar_all[hd:hd + 1, :], ar_all[M_HEADS + hd:M_HEADS + hd + 1, :])
        b_tot = jnp.where(fwd, bc[t - 1:t, :], bc[0:1, :])
        m_prev = m_sc[hd][:, 0:1]
        k_h = k_ref[:, sl]
        v_h = v_ref[:, sl]
        if emit_h:
            q_h = q_ref[:, sl]
            dm = jnp.where(causal, bc + ar, NEG_BIG)
            inter = bc + m_prev
            m_t = jnp.maximum(inter, jnp.max(dm, axis=1, keepdims=True))
            qk = lax.dot_general(q_h, k_h, (((1,), (1,)), ((), ())), preferred_element_type=F32)
            s = qk * jnp.exp(dm - m_t)
            carry = jnp.exp(inter - m_t)
            num = (jnp.dot(s.astype(BF16), v_h, preferred_element_type=F32)
                   + carry * jnp.dot(q_h, c_sc[hd].astype(BF16), preferred_element_type=F32))
            den = (jnp.sum(s, axis=1, keepdims=True)
                   + carry * jnp.sum(q_h.astype(F32) * n_sc[hd], axis=1, keepdims=True))
            h_ref[:, sl] = (num / jnp.maximum(jnp.abs(den), jnp.exp(-m_t))).astype(h_ref.dtype)
        g = b_tot + ac
        m_new = jnp.maximum(b_tot + m_prev, jnp.max(g, axis=0, keepdims=True))
        wgt = jnp.exp(g - m_new)
        decay = jnp.exp(b_tot + m_prev - m_new)
        kw = k_h.astype(F32) * wgt
        c_sc[hd] = decay * c_sc[hd] + lax.dot_general(kw.astype(BF16), v_h, (((0,), (0,)), ((), ())),
                                                      preferred_element_type=F32)
        n_sc[hd] = decay * n_sc[hd] + jnp.sum(kw, axis=0, keepdims=True)
        m_sc[hd] = jnp.broadcast_to(m_new, (1, LANES))

    @pl.when(j == n_chunks - 1)
    def _():
        cf_ref[...] = c_sc[...]
        nf_ref[...] = n_sc[...]
        mf_ref[...] = m_sc[...]


def _mlstm(q, k, v, bc, ac, ar, state, emit_h):
    bsz, L, _ = k[0].shape
    t = MLSTM_CHUNK
    nc = L // t
    seq = lambda b_, d, j: (b_, j + d * (nc - 1 - 2 * j), 0)
    st = lambda b_, d, j: (b_, d, 0, 0, 0)

    def tok(col):
        return pl.BlockSpec((None, t, M_WIDTH), lambda b_, d, j: (b_, j + d * (nc - 1 - 2 * j), col))

    gate_spec = pl.BlockSpec((None, t, LANES), seq)
    ar_spec = pl.BlockSpec((None, FG_LANE0, t), lambda b_, d, j: (b_, 0, j + d * (nc - 1 - 2 * j)))
    c_spec = pl.BlockSpec((None, None, M_HEADS, M_HEAD_DIM, M_HEAD_DIM), st)
    n_spec = pl.BlockSpec((None, None, M_HEADS, 1, M_HEAD_DIM), st)
    m_spec = pl.BlockSpec((None, None, M_HEADS, 1, LANES), st)
    state_shapes = [jax.ShapeDtypeStruct((bsz, 2, M_HEADS, M_HEAD_DIM, M_HEAD_DIM), F32),
                    jax.ShapeDtypeStruct((bsz, 2, M_HEADS, 1, M_HEAD_DIM), F32),
                    jax.ShapeDtypeStruct((bsz, 2, M_HEADS, 1, LANES), F32)]
    in_specs = [tok(k[1]), tok(v[1]), gate_spec, gate_spec, ar_spec, c_spec, n_spec, m_spec]
    args = [k[0], v[0], bc, ac, ar, *state]
    out_specs = [c_spec, n_spec, m_spec]
    out_shape = list(state_shapes)
    if emit_h:
        in_specs = [tok(q[1])] + in_specs
        args = [q[0]] + args
        out_specs = [pl.BlockSpec((None, None, t, M_WIDTH),
                                  lambda b_, d, j: (d, b_, j + d * (nc - 1 - 2 * j), 0))] + out_specs
        out_shape = [jax.ShapeDtypeStruct((2, bsz, L, M_WIDTH), BF16)] + out_shape
    outs = pl.pallas_call(
        functools.partial(_mlstm_kernel, emit_h=emit_h, n_chunks=nc),
        grid=(bsz, 2, nc),
        in_specs=in_specs,
        out_specs=out_specs,
        out_shape=out_shape,
        scratch_shapes=[pltpu.VMEM((M_HEADS, M_HEAD_DIM, M_HEAD_DIM), F32),
                        pltpu.VMEM((M_HEADS, 1, M_HEAD_DIM), F32),
                        pltpu.VMEM((M_HEADS, 1, LANES), F32)],
        compiler_params=_cparams("parallel", "parallel", "arbitrary"),
        name="mlstm" if emit_h else "mlstm_state",
    )(*args)
    if emit_h:
        return outs[0], tuple(outs[1:])
    return None, tuple(outs)


DFT_M_TILE = 8
DFT_C_TILE = 512
FEAT_ROWS = 16


def _filter_outer_kernel(bands_ref, w1t_ref, b1_ref, w2t_ref, b2_ref, w3p_ref, w3f_ref, fr_ref, dl_ref, l_ref,
                         a_ref, ss_ref, *, L, n1, n2):
    i = pl.program_id(0)
    h = n1 // 2
    cols = DFT_M_TILE * h

    def positions(shape, axis, side):
        q = lax.broadcasted_iota(jnp.int32, shape, axis)
        mm, jj = q // h, q % h
        n = n2 * (jj + side * h) + i * DFT_M_TILE + mm
        return n, jnp.where(n < L, n, 2 * L - n).astype(F32)

    taps = []
    sumsq = jnp.zeros((1, a_ref.shape[-1]), F32)
    for side, w3_ref in ((0, w3p_ref), (1, w3f_ref)):
        _, p_row = positions((1, cols), 1, side)
        t_row = p_row / float(max(L - 1, 1))
        ang = ((2 * math.pi / L) * p_row) * bands_ref[...]
        row = lax.broadcasted_iota(jnp.int32, (FEAT_ROWS, cols), 0)
        feats = jnp.concatenate([jnp.where(row == 0, t_row, 0.0), jnp.cos(ang), -jnp.sin(ang)], axis=0)
        fr = fr_ref[...]
        hid = jnp.sin(fr * (jnp.dot(w1t_ref[...], feats.astype(BF16), preferred_element_type=F32) + b1_ref[...]))
        hid = jnp.sin(fr * (jnp.dot(w2t_ref[...], hid.astype(BF16), preferred_element_type=F32) + b2_ref[...]))
        filt = lax.dot_general(hid.astype(BF16), w3_ref[...], (((0,), (0,)), ((), ())),
                               preferred_element_type=F32)
        n_col, p_col = positions((cols, 1), 0, side)
        t_col = p_col / float(max(L - 1, 1))
        kern = filt * jnp.exp(-t_col * jnp.abs(dl_ref[...]))
        kern = jnp.where(n_col == L, 0.0, kern)
        sumsq = sumsq + jnp.sum(kern * kern, axis=0, keepdims=True)
        taps.append(kern)

    for mm in range(DFT_M_TILE):
        x = jnp.concatenate([taps[0][mm * h:(mm + 1) * h], taps[1][mm * h:(mm + 1) * h]], axis=0)
        out = jnp.dot(l_ref[...], x.astype(BF16), preferred_element_type=F32)
        a_ref[0, :, mm, :] = out[:n1]
        a_ref[1, :, mm, :] = out[n1:]

    @pl.when(i == 0)
    def _():
        ss_ref[...] = jnp.zeros_like(ss_ref)

    ss_ref[...] += sumsq


def _filter_outer(L, n1, n2, fwd_r, w1, b1, w2, b2, w3, freq):
    hid = H_FILTER_HIDDEN
    bands = jnp.linspace(1e-4, H_POS_BANDS - 1, H_POS_BANDS, dtype=F32).reshape(H_POS_BANDS, 1)
    w1t = jnp.zeros((hid, 3 * FEAT_ROWS), F32)
    w1t = w1t.at[:, 0].set(w1[0]).at[:, FEAT_ROWS:2 * FEAT_ROWS].set(w1[1:1 + H_POS_BANDS].T)
    w1t = w1t.at[:, 2 * FEAT_ROWS:].set(w1[1 + H_POS_BANDS:].T).astype(BF16)
    w3h = w3.astype(BF16)
    max_decay = math.log(H_DECAY_TARGET) / H_FAST_DECAY_PCT
    min_decay = math.log(H_DECAY_TARGET) / H_SLOW_DECAY_PCT
    deltas = jnp.linspace(min_decay, max_decay, H_WIDTH, dtype=F32).reshape(1, H_WIDTH)
    col = lambda v: v.reshape(hid, 1)
    full = lambda a: pl.BlockSpec(a.shape, lambda i: (0,) * a.ndim)
    args = [bands, w1t, col(b1), w2.T.astype(BF16), col(b2)]
    return pl.pallas_call(
        functools.partial(_filter_outer_kernel, L=L, n1=n1, n2=n2),
        grid=(n2 // DFT_M_TILE,),
        in_specs=[full(a) for a in args]
        + [pl.BlockSpec((hid, H_WIDTH), lambda i: (0, 0)), pl.BlockSpec((hid, H_WIDTH), lambda i: (0, 1)),
           full(col(freq)), full(deltas), full(fwd_r)],
        out_specs=[pl.BlockSpec((2, n1, DFT_M_TILE, H_WIDTH), lambda i: (0, 0, i, 0)),
                   pl.BlockSpec((1, H_WIDTH), lambda i: (0, 0))],
        out_shape=[jax.ShapeDtypeStruct((2, n1, n2, H_WIDTH), F32),
                   jax.ShapeDtypeStruct((1, H_WIDTH), F32)],
        compiler_params=_cparams("arbitrary"),
        name="hyena_filter_outer",
    )(*args, w3h, w3h, col(freq), deltas, fwd_r)


def _dft_factors(n):
    lg = int(round(math.log2(n)))
    n1 = 1 << ((lg + 1) // 2)
    return n1, n // n1


def _dft_outer_matrices(n1):
    k = np.arange(n1)[:, None]
    n = np.arange(n1)[None, :]
    ang = 2.0 * np.pi * ((k * n) % n1) / n1
    cr, ci = np.cos(ang), -np.sin(ang)
    h = n1 // 2
    fwd_c = np.block([[cr[:, :h], -ci[:, :h]], [ci[:, :h], cr[:, :h]]])
    fwd_r = np.concatenate([cr, ci], axis=0)
    ir, ii = cr[:h, :], -ci[:h, :]
    inv = np.block([[ir, -ii], [ii, ir]])
    return (jnp.asarray(fwd_c, F32).astype(BF16), jnp.asarray(fwd_r, F32).astype(BF16),
            jnp.asarray(inv, F32).astype(BF16))


def _dft_inner_matrices(n1, n2):
    n = n1 * n2
    k1 = jnp.arange(n1, dtype=jnp.int32)[:, None, None]
    k2 = jnp.arange(n2, dtype=jnp.int32)[None, :, None]
    m = jnp.arange(n2, dtype=jnp.int32)[None, None, :]
    ang = ((m * (k1 + n1 * k2)) % n).astype(F32) * (2.0 * math.pi / n)
    gr, gi = jnp.cos(ang), -jnp.sin(ang)
    g = jnp.concatenate([jnp.concatenate([gr, -gi], axis=2), jnp.concatenate([gi, gr], axis=2)], axis=1)
    return g.astype(BF16), jnp.swapaxes(g, 1, 2).astype(BF16)


def _outer_dft_kernel(l_ref, x_ref, o_ref):
    p_in, p_out = x_ref.shape[0], o_ref.shape[0]
    r_out = o_ref.shape[1]
    for mm in range(x_ref.shape[2]):
        parts = [x_ref[p, :, mm, :] for p in range(p_in)]
        x = parts[0] if p_in == 1 else jnp.concatenate(parts, axis=0)
        out = jnp.dot(l_ref[...], x.astype(BF16), preferred_element_type=F32)
        for p in range(p_out):
            o_ref[p, :, mm, :] = out[p * r_out:(p + 1) * r_out]


def _outer_dft(lmat, x4, p_out):
    p_in, r_in, n2, c = x4.shape
    r_out = lmat.shape[0] // p_out
    tc = min(DFT_C_TILE, c)
    return pl.pallas_call(
        _outer_dft_kernel,
        grid=(n2 // DFT_M_TILE, c // tc),
        in_specs=[pl.BlockSpec(lmat.shape, lambda m, j: (0, 0)),
                  pl.BlockSpec((p_in, r_in, DFT_M_TILE, tc), lambda m, j: (0, 0, m, j))],
        out_specs=pl.BlockSpec((p_out, r_out, DFT_M_TILE, tc), lambda m, j: (0, 0, m, j)),
        out_shape=jax.ShapeDtypeStruct((p_out, r_out, n2, c), F32),
        compiler_params=_cparams("parallel", "parallel"),
        name="dft_outer",
    )(lmat, x4)


DFT_K_TILE = 2


def _inner_fwd_kernel(g_ref, a_ref, o_ref):
    n2 = a_ref.shape[2]
    for kk in range(a_ref.shape[1]):
        a = jnp.concatenate([a_ref[0, kk].astype(BF16), a_ref[1, kk].astype(BF16)], axis=0)
        x = jnp.dot(g_ref[kk], a, preferred_element_type=F32)
        o_ref[0, kk] = x[:n2].astype(o_ref.dtype)
        o_ref[1, kk] = x[n2:].astype(o_ref.dtype)


def _inner_fwd(g, a):
    _, n1, n2, c = a.shape
    blk = pl.BlockSpec((2, DFT_K_TILE, n2, c), lambda k: (0, k, 0, 0))
    return pl.pallas_call(
        _inner_fwd_kernel,
        grid=(n1 // DFT_K_TILE,),
        in_specs=[pl.BlockSpec((DFT_K_TILE, 2 * n2, 2 * n2), lambda k: (k, 0, 0)), blk],
        out_specs=blk,
        out_shape=jax.ShapeDtypeStruct((2, n1, n2, c), BF16),
        compiler_params=_cparams("parallel"),
        name="dft_inner_filter",
    )(g, a)


def _inner_conv_kernel(g_ref, gt_ref, a_ref, k_ref, o_ref):
    n2 = a_ref.shape[2]
    for kk in range(a_ref.shape[1]):
        a = jnp.concatenate([a_ref[0, kk].astype(BF16), a_ref[1, kk].astype(BF16)], axis=0)
        x = jnp.dot(g_ref[kk], a, preferred_element_type=F32)
        xr, xi = x[:n2], x[n2:]
        kr, ki = k_ref[0, kk].astype(F32), k_ref[1, kk].astype(F32)
        yr = xr * kr - xi * ki
        yi = xr * ki + xi * kr
        y = jnp.concatenate([yr.astype(BF16), yi.astype(BF16)], axis=0)
        b = jnp.dot(gt_ref[kk], y, preferred_element_type=F32)
        o_ref[0, kk] = b[:n2]
        o_ref[1, kk] = b[n2:]


def _inner_conv(g, gt, a, kf):
    _, n1, n2, c = a.shape
    blk = pl.BlockSpec((2, DFT_K_TILE, n2, c), lambda k: (0, k, 0, 0))
    mat = pl.BlockSpec((DFT_K_TILE, 2 * n2, 2 * n2), lambda k: (k, 0, 0))
    return pl.pallas_call(
        _inner_conv_kernel,
        grid=(n1 // DFT_K_TILE,),
        in_specs=[mat, mat, blk, blk],
        out_specs=blk,
        out_shape=jax.ShapeDtypeStruct((2, n1, n2, c), F32),
        compiler_params=_cparams("parallel"),
        name="dft_inner_conv",
    )(g, gt, a, kf)


def _hyena_long_conv(s, w1, b1, w2, b2, w3, freq):
    bsz, L, c = s.shape
    assert bsz == 2
    n = 2 * L
    n1, n2 = _dft_factors(n)
    fwd_c, fwd_r, inv = _dft_outer_matrices(n1)
    g, gt = _dft_inner_matrices(n1, n2)
    af, sumsq = _filter_outer(L, n1, n2, fwd_r, w1, b1, w2, b2, w3, freq)
    kf = _inner_fwd(g, af)
    a = _outer_dft(fwd_c, s.reshape(2, n1 // 2, n2, c), 2)
    b = _inner_conv(g, gt, a, kf)
    y = _outer_dft(inv, b, 2)
    return y.reshape(2, L, c), sumsq


def _merge_kernel(hf_ref, hb_ref, o_ref, x0_ref, s_ref, y_ref, ga_ref, gb_ref, x_ref,
                  ysc_ref, hbias_ref, gate_ref, g2_ref, sh_ref, sc_ref,
                  wa_ref, wb_ref, wo_ref, x1_ref, h2_ref):
    a = o_ref[...].astype(F32) * (hf_ref[...].astype(F32) + hb_ref[...].astype(F32))
    s = s_ref[...]
    hy = x0_ref[...].astype(F32) * (y_ref[...] * ysc_ref[...] + hbias_ref[...] * s)
    pa = jnp.dot(a.astype(BF16), wa_ref[...], preferred_element_type=F32)
    pb = jnp.dot(hy.astype(BF16), wb_ref[...], preferred_element_type=F32)
    mix = ga_ref[...].astype(F32) * pa + gb_ref[...].astype(F32) * pb
    out = jnp.dot(mix.astype(BF16), wo_ref[...], preferred_element_type=F32)
    x1 = x_ref[...] + gate_ref[...] * out
    x1_ref[...] = x1
    y = x1 * lax.rsqrt(jnp.mean(x1 * x1, axis=-1, keepdims=True) + EPS) * g2_ref[...]
    h2_ref[...] = y * (1.0 + sc_ref[...]) + sh_ref[...]


def _merge(hdirs, pm, x0, s, y, x, yscale, h_bias, gate1, g2, shift2, scale2, w_a, w_b, w_out, tm=256):
    bsz, L, d = x.shape
    tok = pl.BlockSpec((None, tm, d), lambda b, i: (b, i, 0))

    def pm_tile(col):
        return pl.BlockSpec((None, tm, d), lambda b, i: (b, i, col))

    vec = pl.BlockSpec((1, d), lambda b, i: (0, 0))
    bvec = pl.BlockSpec((None, 1, d), lambda b, i: (b, 0, 0))
    wsp = pl.BlockSpec((d, d), lambda b, i: (0, 0))
    return pl.pallas_call(
        _merge_kernel,
        grid=(bsz, L // tm),
        in_specs=[pl.BlockSpec((None, None, tm, d), lambda b, i: (0, b, i, 0)),
                  pl.BlockSpec((None, None, tm, d), lambda b, i: (1, b, i, 0)),
                  pm_tile(PM_O), tok, tok, tok, pm_tile(PM_GA), pm_tile(PM_GB), tok,
                  vec, vec, bvec, vec, bvec, bvec, wsp, wsp, wsp],
        out_specs=[tok, tok],
        out_shape=[jax.ShapeDtypeStruct((bsz, L, d), F32), jax.ShapeDtypeStruct((bsz, L, d), F32)],
        compiler_params=_cparams("parallel", "parallel"),
        name="merge",
    )(hdirs, hdirs, pm, x0, s, y, pm, pm, x, yscale, h_bias.reshape(1, d), gate1, g2.reshape(1, d),
      shift2, scale2, w_a, w_b, w_out)


MOE_BLOCK = 256
ROUTE_E1, ROUTE_E2, ROUTE_W1, ROUTE_W2 = 0, 1, 2, 3
EXP_LANE0 = N_GROUPS


def _first_lane_of_max(val, valid, lane):
    masked = jnp.where(valid, val, NEG_BIG)
    mx = jnp.max(masked, axis=1, keepdims=True)
    idx = jnp.min(jnp.where(valid & (masked == mx), lane, LANES), axis=1, keepdims=True)
    return mx, idx


def _router_kernel(h_ref, w_ref, b_ref, r_ref):
    logits = jnp.dot(h_ref[...].astype(BF16), w_ref[...], preferred_element_type=F32) + b_ref[...]
    lane = lax.broadcasted_iota(jnp.int32, logits.shape, 1)
    is_g = lane < N_GROUPS
    gmax, gsel = _first_lane_of_max(logits, is_g, lane)
    gsum = jnp.sum(jnp.where(is_g, jnp.exp(logits - gmax), 0.0), axis=1, keepdims=True)
    gw = 1.0 / gsum
    lo = EXP_LANE0 + gsel * EXPERTS_PER_GROUP
    in_grp = (lane >= lo) & (lane < lo + EXPERTS_PER_GROUP)
    emax, l1 = _first_lane_of_max(logits, in_grp, lane)
    esum = jnp.sum(jnp.where(in_grp, jnp.exp(logits - emax), 0.0), axis=1, keepdims=True)
    e2max, l2 = _first_lane_of_max(logits, in_grp & (lane != l1), lane)
    v1 = 1.0 / esum
    v2 = jnp.exp(e2max - emax) / esum
    vs = v1 + v2
    w1 = gw * v1 / vs
    w2 = gw * v2 / vs
    e1 = (l1 - EXP_LANE0).astype(F32)
    e2 = (l2 - EXP_LANE0).astype(F32)
    r_ref[...] = jnp.where(lane == ROUTE_E1, e1,
                           jnp.where(lane == ROUTE_E2, e2,
                                     jnp.where(lane == ROUTE_W1, w1,
                                               jnp.where(lane == ROUTE_W2, w2, 0.0))))


def _router(h2, w_group, b_group, w_router, b_router, tm=1024):
    n, d = h2.shape
    w = jnp.zeros((d, LANES), F32).at[:, :N_GROUPS].set(w_group).at[
        :, EXP_LANE0:EXP_LANE0 + N_EXPERTS].set(w_router).astype(BF16)
    b = jnp.zeros((1, LANES), F32).at[0, :N_GROUPS].set(b_group).at[
        0, EXP_LANE0:EXP_LANE0 + N_EXPERTS].set(b_router)
    return pl.pallas_call(
        _router_kernel,
        grid=(n // tm,),
        in_specs=[pl.BlockSpec((tm, d), lambda i: (i, 0)),
                  pl.BlockSpec((d, LANES), lambda i: (0, 0)),
                  pl.BlockSpec((1, LANES), lambda i: (0, 0))],
        out_specs=pl.BlockSpec((tm, LANES), lambda i: (i, 0)),
        out_shape=jax.ShapeDtypeStruct((n, LANES), F32),
        compiler_params=_cparams("parallel"),
        name="moe_router",
    )(h2, w, b)


def _slots_kernel(r_ref, dest_ref, cnt_ref, run_sc, start_sc):
    ph = pl.program_id(0)
    i = pl.program_id(1)
    rec = r_ref[...]
    tm = rec.shape[0]
    lane = lax.broadcasted_iota(jnp.int32, rec.shape, 1)
    e1 = rec[:, ROUTE_E1:ROUTE_E1 + 1].astype(jnp.int32)
    e2 = rec[:, ROUTE_E2:ROUTE_E2 + 1].astype(jnp.int32)
    oh1 = lane == e1
    oh2 = lane == e2
    oh = (oh1 | oh2).astype(F32)

    @pl.when((ph == 0) & (i == 0))
    def _():
        run_sc[...] = jnp.zeros_like(run_sc)

    @pl.when(ph == 0)
    def _():
        run_sc[...] += jnp.sum(oh, axis=0, keepdims=True)

    @pl.when((ph == 1) & (i == 0))
    def _():
        counts = run_sc[...]
        cnt_ref[...] = counts
        nblk = jnp.floor((counts + (MOE_BLOCK - 1)) * (1.0 / MOE_BLOCK))
        rr = lax.broadcasted_iota(jnp.int32, (LANES, LANES), 0)
        cc = lax.broadcasted_iota(jnp.int32, (LANES, LANES), 1)
        before = (rr < cc).astype(BF16)
        first = jnp.dot(nblk.astype(BF16), before, preferred_element_type=F32)
        start_sc[...] = first * float(MOE_BLOCK)
        run_sc[...] = jnp.zeros_like(run_sc)

    @pl.when(ph == 1)
    def _():
        r = lax.broadcasted_iota(jnp.int32, (tm, tm), 0)
        c = lax.broadcasted_iota(jnp.int32, (tm, tm), 1)
        earlier = (r > c).astype(BF16)
        rank = jnp.dot(earlier, oh.astype(BF16), preferred_element_type=F32) + run_sc[...] + start_sc[...]
        d1 = jnp.sum(jnp.where(oh1, rank, 0.0), axis=1, keepdims=True)
        d2 = jnp.sum(jnp.where(oh2, rank, 0.0), axis=1, keepdims=True)
        dest_ref[...] = jnp.where(lane == 0, d1, jnp.where(lane == 1, d2, 0.0)).astype(jnp.int32)
        run_sc[...] += jnp.sum(oh, axis=0, keepdims=True)


def _slots(route, tm=512):
    n = route.shape[0]
    return pl.pallas_call(
        _slots_kernel,
        grid=(2, n // tm),
        in_specs=[pl.BlockSpec((tm, LANES), lambda p, i: (i, 0))],
        out_specs=[pl.BlockSpec((tm, LANES), lambda p, i: (i * p, 0)),
                   pl.BlockSpec((1, LANES), lambda p, i: (0, 0))],
        out_shape=[jax.ShapeDtypeStruct((n, LANES), jnp.int32), jax.ShapeDtypeStruct((1, LANES), F32)],
        scratch_shapes=[pltpu.VMEM((1, LANES), F32), pltpu.VMEM((1, LANES), F32)],
        compiler_params=_cparams("arbitrary", "arbitrary"),
        name="moe_slots",
    )(route)


DMA_UNROLL = 8


def _row_copy(src_ref, dst_ref, sem, src_row, dst_row):
    return pltpu.make_async_copy(src_ref.at[pl.ds(src_row, 1)], dst_ref.at[pl.ds(dst_row, 1)], sem)


def _dispatch_kernel(dest_ref, h_ref, xs_in_ref, xs_ref, sem):
    del xs_in_ref
    tm = h_ref.shape[0]

    def start(r, carry):
        _row_copy(h_ref, xs_ref, sem, r, dest_ref[0, 2 * r]).start(priority=0)
        _row_copy(h_ref, xs_ref, sem, r, dest_ref[0, 2 * r + 1]).start(priority=1)
        return carry

    lax.fori_loop(0, tm, start, 0, unroll=DMA_UNROLL)

    def wait(r, carry):
        _row_copy(h_ref, xs_ref, sem, 0, 0).wait()
        _row_copy(h_ref, xs_ref, sem, 0, 0).wait()
        return carry

    lax.fori_loop(0, tm, wait, 0, unroll=DMA_UNROLL)


def _dispatch(h2, dest, n_slots, tm=256):
    n, d = h2.shape
    dest3 = dest.reshape(n // tm, 1, 2 * tm)
    zeros = jnp.zeros((n_slots, d), F32)
    return pl.pallas_call(
        _dispatch_kernel,
        grid=(n // tm,),
        in_specs=[pl.BlockSpec((None, 1, 2 * tm), lambda i: (i, 0, 0), memory_space=pltpu.SMEM),
                  pl.BlockSpec((tm, d), lambda i: (i, 0)),
                  pl.BlockSpec(memory_space=pl.ANY)],
        out_specs=pl.BlockSpec(memory_space=pl.ANY),
        out_shape=jax.ShapeDtypeStruct((n_slots, d), F32),
        scratch_shapes=[pltpu.SemaphoreType.DMA(())],
        input_output_aliases={2: 0},
        compiler_params=_cparams("arbitrary"),
        name="moe_dispatch",
    )(dest3, h2, zeros)


def _experts_kernel(be_ref, first_ref, nxt_ref, par_ref, nu_ref, x_ref, w1_hbm, w3_hbm, w2_hbm, o_ref,
                    w1f, w3f, w2f, w1b, w3b, w2b, sems):
    i = pl.program_id(0)

    def weight_copies(e, slot):
        return (pltpu.make_async_copy(w1_hbm.at[e], w1f.at[slot], sems.at[0, slot]),
                pltpu.make_async_copy(w3_hbm.at[e], w3f.at[slot], sems.at[1, slot]),
                pltpu.make_async_copy(w2_hbm.at[e], w2f.at[slot], sems.at[2, slot]))

    @pl.when(i == 0)
    def _():
        for cp in weight_copies(be_ref[0], 0):
            cp.start()

    @pl.when(first_ref[i] == 1)
    def _():
        slot = par_ref[i]

        @pl.when(nxt_ref[i] >= 0)
        def _():
            for cp in weight_copies(nxt_ref[i], 1 - slot):
                cp.start()

        for cp in weight_copies(be_ref[i], slot):
            cp.wait()
        w1b[...] = w1f[slot].astype(BF16)
        w3b[...] = w3f[slot].astype(BF16)
        w2b[...] = w2f[slot].astype(BF16)

    @pl.when(i < nu_ref[0])
    def _():
        x = x_ref[...].astype(BF16)
        a = jnp.dot(x, w1b[...], preferred_element_type=F32)
        b = jnp.dot(x, w3b[...], preferred_element_type=F32)
        hmid = (a * jax.nn.sigmoid(a)) * b
        o_ref[...] = jnp.dot(hmid.astype(BF16), w2b[...], preferred_element_type=F32)

    @pl.when(i >= nu_ref[0])
    def _():
        o_ref[...] = jnp.zeros_like(o_ref)


def _experts(xs, block_e, n_used, w1_e, w3_e, w2_e):
    n_slots, d = xs.shape
    nb = n_slots // MOE_BLOCK
    de = w1_e.shape[2]
    idx = jnp.arange(nb, dtype=jnp.int32)
    used = idx < n_used[0]
    first = used & ((idx == 0) | (block_e != jnp.roll(block_e, 1)))
    ordinal = jnp.cumsum(first.astype(jnp.int32)) - 1
    par = (ordinal % 2).astype(jnp.int32)
    first_pos = jnp.where(first, idx, nb)
    next_first = lax.cummin(jnp.concatenate([first_pos[1:], jnp.full((1,), nb, jnp.int32)]), reverse=True)
    nxt = jnp.where(next_first < nb, block_e[jnp.minimum(next_first, nb - 1)], -1).astype(jnp.int32)
    any_spec = pl.BlockSpec(memory_space=pl.ANY)
    grid_spec = pltpu.PrefetchScalarGridSpec(
        num_scalar_prefetch=5,
        grid=(nb,),
        in_specs=[pl.BlockSpec((MOE_BLOCK, d), lambda i, *_: (i, 0)), any_spec, any_spec, any_spec],
        out_specs=pl.BlockSpec((MOE_BLOCK, d), lambda i, *_: (i, 0)),
        scratch_shapes=[pltpu.VMEM((2, d, de), F32), pltpu.VMEM((2, d, de), F32), pltpu.VMEM((2, de, d), F32),
                        pltpu.VMEM((d, de), BF16), pltpu.VMEM((d, de), BF16), pltpu.VMEM((de, d), BF16),
                        pltpu.SemaphoreType.DMA((3, 2))],
    )
    return pl.pallas_call(
        _experts_kernel,
        grid_spec=grid_spec,
        out_shape=jax.ShapeDtypeStruct((n_slots, d), F32),
        compiler_params=_cparams("arbitrary"),
        name="moe_experts",
    )(block_e, first.astype(jnp.int32), nxt, par, n_used, xs, w1_e, w3_e, w2_e)


def _combine_kernel(dest_ref, r_ref, x_ref, gate_ref, gf_ref, ys_ref, o_ref, buf1, buf2, sem):
    tm = x_ref.shape[0]

    def start(r, carry):
        _row_copy(ys_ref, buf1, sem, dest_ref[0, 2 * r], r).start(priority=0)
        _row_copy(ys_ref, buf2, sem, dest_ref[0, 2 * r + 1], r).start(priority=1)
        return carry

    lax.fori_loop(0, tm, start, 0, unroll=DMA_UNROLL)

    def wait(r, carry):
        _row_copy(ys_ref, buf1, sem, 0, 0).wait()
        _row_copy(ys_ref, buf2, sem, 0, 0).wait()
        return carry

    lax.fori_loop(0, tm, wait, 0, unroll=DMA_UNROLL)
    rec = r_ref[...]
    y = buf1[...] * rec[:, ROUTE_W1:ROUTE_W1 + 1] + buf2[...] * rec[:, ROUTE_W2:ROUTE_W2 + 1]
    x2 = x_ref[...] + gate_ref[...] * y
    o_ref[...] = x2 * lax.rsqrt(jnp.mean(x2 * x2, axis=-1, keepdims=True) + EPS) * gf_ref[...]


def _combine(ys, dest, route, x1, gate2, g_final, tm=256):
    bsz, L, d = x1.shape
    n = bsz * L
    tpb = L // tm
    dest3 = dest.reshape(n // tm, 1, 2 * tm)
    return pl.pallas_call(
        _combine_kernel,
        grid=(bsz, tpb),
        in_specs=[pl.BlockSpec((None, 1, 2 * tm), lambda b, i: (b * tpb + i, 0, 0), memory_space=pltpu.SMEM),
                  pl.BlockSpec((tm, LANES), lambda b, i: (b * tpb + i, 0)),
                  pl.BlockSpec((None, tm, d), lambda b, i: (b, i, 0)),
                  pl.BlockSpec((None, 1, d), lambda b, i: (b, 0, 0)),
                  pl.BlockSpec((1, d), lambda b, i: (0, 0)),
                  pl.BlockSpec(memory_space=pl.ANY)],
        out_specs=pl.BlockSpec((None, tm, d), lambda b, i: (b, i, 0)),
        out_shape=jax.ShapeDtypeStruct((bsz, L, d), F32),
        scratch_shapes=[pltpu.VMEM((tm, d), F32), pltpu.VMEM((tm, d), F32), pltpu.SemaphoreType.DMA(())],
        compiler_params=_cparams("arbitrary", "arbitrary"),
        name="moe_combine",
    )(dest3, route, x1, gate2, g_final.reshape(1, d), ys)


def _moe(h2, x1, gate2, g_final, w_group, b_group, w_router, b_router, w1_e, w3_e, w2_e):
    bsz, L, d = x1.shape
    n = bsz * L
    h2f = h2.reshape(n, d)
    route = _router(h2f, w_group, b_group, w_router, b_router)
    dest_rec, counts = _slots(route)
    dest = dest_rec[:, :2].reshape(2 * n)
    nb = (2 * n) // MOE_BLOCK + N_EXPERTS
    cnt = counts[0, :N_EXPERTS].astype(jnp.int32)
    blocks_per_e = (cnt + MOE_BLOCK - 1) // MOE_BLOCK
    ends = jnp.cumsum(blocks_per_e)
    block_e = jnp.clip(jnp.searchsorted(ends, jnp.arange(nb, dtype=jnp.int32), side='right'),
                       0, N_EXPERTS - 1).astype(jnp.int32)
    n_used = ends[-1:].astype(jnp.int32)
    xs = _dispatch(h2f, dest, nb * MOE_BLOCK)
    ys = _experts(xs, block_e, n_used, w1_e, w3_e, w2_e)
    return _combine(ys, dest, route, x1, gate2, g_final)


def kernel(x, c, ctx, c_ctx, w_mod, b_mod, g_norm1, g_norm2, w_in, b_in, w_qk_conv, b_qk_conv,
           w_h_conv, b_h_conv, hf_w1, hf_b1, hf_w2, hf_b2, hf_w3, hf_freq, h_bias, w_a, w_b, w_out,
           w_group, b_group, w_router, b_router, w1_e, w3_e, w2_e, g_final):
    assert w_mod.shape[0] == 1, "single-layer block"
    (w_mod, b_mod, g_norm1, g_norm2, w_in, b_in, w_qk_conv, b_qk_conv, w_h_conv, b_h_conv, hf_w1, hf_b1, hf_w2,
     hf_b2, hf_w3, hf_freq, h_bias, w_a, w_b, w_out, w_group, b_group, w_router, b_router, w1_e, w3_e, w2_e) = (
        t[0] for t in (w_mod, b_mod, g_norm1, g_norm2, w_in, b_in, w_qk_conv, b_qk_conv, w_h_conv, b_h_conv,
                       hf_w1, hf_b1, hf_w2, hf_b2, hf_w3, hf_freq, h_bias, w_a, w_b, w_out, w_group, b_group,
                       w_router, b_router, w1_e, w3_e, w2_e))
    bsz, L, d = x.shape
    lc = ctx.shape[1]
    seg = L // (L // GRID_W)
    assert bsz + 1 <= 8 and lc == MLSTM_CHUNK and L % MLSTM_CHUNK == 0

    cond = jnp.zeros((8, d), F32).at[:bsz].set(c).at[bsz].set(c_ctx)
    mod = _adaln(cond, w_mod, b_mod).reshape(8, 6, d)
    modx = mod[:bsz]
    shift1, scale1, gate1, shift2, scale2, gate2 = (modx[:, i:i + 1] for i in range(6))
    shift1c = jnp.broadcast_to(mod[bsz, 0].reshape(1, 1, d), (bsz, 1, d))
    scale1c = jnp.broadcast_to(mod[bsz, 1].reshape(1, 1, d), (bsz, 1, d))

    w_in16 = w_in.astype(BF16)
    k_scale = jnp.full((M_WIDTH,), M_HEAD_DIM ** -0.5, F32)
    qk_scale = jnp.concatenate([jnp.ones((M_WIDTH,), F32), k_scale])
    w_gates, b_gates = w_in[:, IG0:M_COLS], b_in[IG0:M_COLS]

    hc = _norm_mod(ctx, g_norm1, shift1c, scale1c, lc)
    kc = _proj_conv_silu(hc, w_in16[:, K0:V0], b_in[K0:V0], w_qk_conv[:, M_WIDTH:], b_qk_conv[M_WIDTH:],
                         k_scale, lc, lc)
    vc = _proj_act(hc, w_in16[:, V0:O0], b_in[V0:O0], "none", BF16, lc)
    bcc, acc, arc = _gates(hc, w_gates, b_gates, MLSTM_CHUNK)
    zero_state = (jnp.zeros((bsz, 2, M_HEADS, M_HEAD_DIM, M_HEAD_DIM), F32),
                  jnp.zeros((bsz, 2, M_HEADS, 1, M_HEAD_DIM), F32),
                  jnp.zeros((bsz, 2, M_HEADS, 1, LANES), F32))
    _, ctx_state = _mlstm(None, (kc, 0), (vc, 0), bcc, acc, arc, zero_state, False)

    tm = 1024
    h = _norm_mod(x, g_norm1, shift1, scale1, tm)
    w_main = jnp.concatenate([w_in16[:, Q0:IG0], w_in16[:, GA0:IN_COLS]], axis=1)
    b_main = jnp.concatenate([b_in[Q0:IG0], b_in[GA0:IN_COLS]])
    pm = _proj_main(h, w_main, b_main, w_qk_conv, b_qk_conv, qk_scale, seg, tm)
    bc, ac, ar = _gates(h, w_gates, b_gates, MLSTM_CHUNK)
    hdirs, _ = _mlstm((pm, PM_Q), (pm, PM_K), (pm, PM_V), bc, ac, ar, ctx_state, True)

    x0, s = _proj_hyena(h, w_in16[:, HY0:GA0], b_in[HY0:GA0], w_h_conv, b_h_conv, seg, tm)
    y, sumsq = _hyena_long_conv(s, hf_w1, hf_b1, hf_w2, hf_b2, hf_w3, hf_freq)
    yscale = lax.rsqrt(sumsq + EPS) * (1.0 / (2 * L))

    x1, h2 = _merge(hdirs, pm, x0, s, y, x, yscale, h_bias, gate1, g_norm2, shift2, scale2,
                    w_a.astype(BF16), w_b.astype(BF16), w_out.astype(BF16))
    return _moe(h2, x1, gate2, g_final, w_group, b_group, w_router, b_router, w1_e, w3_e, w2_e)
```

```python
import functools
import math

import jax
import jax.numpy as jnp
import numpy as np
from jax import lax
from jax.experimental import pallas as pl
from jax.experimental.pallas import tpu as pltpu

F32 = jnp.float32
BF16 = jnp.bfloat16

D_MODEL = 1024
GRID_W = 64
EPS = 1e-6
M_HEADS = 4
M_HEAD_DIM = 256
M_WIDTH = M_HEADS * M_HEAD_DIM
H_WIDTH = 1024
H_POS_BANDS = 16
H_FILTER_HIDDEN = 64
H_FAST_DECAY_PCT = 0.3
H_SLOW_DECAY_PCT = 1.5
H_DECAY_TARGET = 1e-2
N_GROUPS = 8
EXPERTS_PER_GROUP = 8
N_EXPERTS = N_GROUPS * EXPERTS_PER_GROUP
D_EXPERT = 512
Q0 = 0
K0 = Q0 + M_WIDTH
V0 = K0 + M_WIDTH
O0 = V0 + M_WIDTH
IG0 = O0 + M_WIDTH
FG0 = IG0 + 2 * M_HEADS
M_COLS = FG0 + 2 * M_HEADS
HY0 = M_COLS
GA0 = HY0 + 3 * H_WIDTH
GB0 = GA0 + D_MODEL
IN_COLS = GB0 + D_MODEL

LANES = 128
MLSTM_CHUNK = 256
NEG_BIG = -1e30
VMEM_LIMIT = 48 * 1024 * 1024


def _cparams(*sem):
    return pltpu.CompilerParams(dimension_semantics=sem, vmem_limit_bytes=VMEM_LIMIT)


def _adaln_kernel(c_ref, w_ref, b_ref, o_ref):
    s = c_ref[...]
    s = s * jax.nn.sigmoid(s)
    o_ref[...] = jnp.dot(s.astype(BF16), w_ref[...].astype(BF16), preferred_element_type=F32) + b_ref[...]


def _adaln(cond, w_mod, b_mod):
    n = w_mod.shape[1]
    tn = 1536
    return pl.pallas_call(
        _adaln_kernel,
        grid=(n // tn,),
        in_specs=[pl.BlockSpec((8, D_MODEL), lambda j: (0, 0)),
                  pl.BlockSpec((D_MODEL, tn), lambda j: (0, j)),
                  pl.BlockSpec((1, tn), lambda j: (0, j))],
        out_specs=pl.BlockSpec((8, tn), lambda j: (0, j)),
        out_shape=jax.ShapeDtypeStruct((8, n), F32),
        compiler_params=_cparams("arbitrary"),
        name="adaln",
    )(cond, w_mod, b_mod.reshape(1, n))


def _norm_mod_kernel(x_ref, g_ref, sh_ref, sc_ref, o_ref):
    x = x_ref[...]
    y = x * lax.rsqrt(jnp.mean(x * x, axis=-1, keepdims=True) + EPS)
    y = y * g_ref[...]
    o_ref[...] = (y * (1.0 + sc_ref[...]) + sh_ref[...]).astype(o_ref.dtype)


def _norm_mod(x, g, shift, scale, tm):
    bsz, L, d = x.shape
    return pl.pallas_call(
        _norm_mod_kernel,
        grid=(bsz, L // tm),
        in_specs=[pl.BlockSpec((None, tm, d), lambda b, i: (b, i, 0)),
                  pl.BlockSpec((1, d), lambda b, i: (0, 0)),
                  pl.BlockSpec((None, 1, d), lambda b, i: (b, 0, 0)),
                  pl.BlockSpec((None, 1, d), lambda b, i: (b, 0, 0))],
        out_specs=pl.BlockSpec((None, tm, d), lambda b, i: (b, i, 0)),
        out_shape=jax.ShapeDtypeStruct((bsz, L, d), BF16),
        compiler_params=_cparams("parallel", "parallel"),
        name="norm_mod",
    )(x, g.reshape(1, d), shift, scale)


def _conv3(z, wc, bc, seg):
    tm = z.shape[0]
    pos = lax.broadcasted_iota(jnp.int32, z.shape, 0) & (seg - 1)
    zp = jnp.where(pos == 0, 0.0, pltpu.roll(z, 1, 0))
    zn = jnp.where(pos == seg - 1, 0.0, pltpu.roll(z, tm - 1, 0))
    return zp * wc[0:1, :] + z * wc[1:2, :] + zn * wc[2:3, :] + bc


def _proj_act_kernel(h_ref, w_ref, b_ref, o_ref, *, act):
    z = jnp.dot(h_ref[...], w_ref[...], preferred_element_type=F32) + b_ref[...]
    if act == "sigmoid":
        z = jax.nn.sigmoid(z)
    o_ref[...] = z.astype(o_ref.dtype)


def _proj_act(h, w, b, act, out_dtype, tm, tn=512):
    bsz, L, d = h.shape
    n = w.shape[1]
    return pl.pallas_call(
        functools.partial(_proj_act_kernel, act=act),
        grid=(bsz, L // tm, n // tn),
        in_specs=[pl.BlockSpec((None, tm, d), lambda b_, i, j: (b_, i, 0)),
                  pl.BlockSpec((d, tn), lambda b_, i, j: (0, j)),
                  pl.BlockSpec((1, tn), lambda b_, i, j: (0, j))],
        out_specs=pl.BlockSpec((None, tm, tn), lambda b_, i, j: (b_, i, j)),
        out_shape=jax.ShapeDtypeStruct((bsz, L, n), out_dtype),
        compiler_params=_cparams("parallel", "parallel", "arbitrary"),
        name="proj_" + act,
    )(h, w, b.reshape(1, n))


def _proj_conv_silu_kernel(h_ref, w_ref, b_ref, wc_ref, bc_ref, cs_ref, o_ref, *, seg):
    z = jnp.dot(h_ref[...], w_ref[...], preferred_element_type=F32) + b_ref[...]
    y = _conv3(z, wc_ref[...], bc_ref[...], seg)
    y = y * jax.nn.sigmoid(y)
    o_ref[...] = (y * cs_ref[...]).astype(o_ref.dtype)


def _proj_conv_silu(h, w, b, wc, bc, colscale, seg, tm, tn=512):
    bsz, L, d = h.shape
    n = w.shape[1]
    col = lambda b_, i, j: (0, j)
    return pl.pallas_call(
        functools.partial(_proj_conv_silu_kernel, seg=seg),
        grid=(bsz, L // tm, n // tn),
        in_specs=[pl.BlockSpec((None, tm, d), lambda b_, i, j: (b_, i, 0)),
                  pl.BlockSpec((d, tn), col),
                  pl.BlockSpec((1, tn), col),
                  pl.BlockSpec((3, tn), col),
                  pl.BlockSpec((1, tn), col),
                  pl.BlockSpec((1, tn), col)],
        out_specs=pl.BlockSpec((None, tm, tn), lambda b_, i, j: (b_, i, j)),
        out_shape=jax.ShapeDtypeStruct((bsz, L, n), BF16),
        compiler_params=_cparams("parallel", "parallel", "arbitrary"),
        name="proj_conv_silu",
    )(h, w, b.reshape(1, n), wc, bc.reshape(1, n), colscale.reshape(1, n))


PROJ_TN = 1024
PROJ_SUB = 512
PM_Q, PM_K, PM_V, PM_O, PM_GA, PM_GB = range(6)


def _proj_main_kernel(h_ref, w_ref, b_ref, wc_ref, bc_ref, cs_ref, o_ref, *, seg):
    j = pl.program_id(2)

    def run(epilogue):
        for c in range(PROJ_TN // PROJ_SUB):
            sl = slice(c * PROJ_SUB, (c + 1) * PROJ_SUB)
            z = jnp.dot(h_ref[...], w_ref[:, sl], preferred_element_type=F32) + b_ref[:, sl]
            o_ref[:, sl] = epilogue(z, sl).astype(o_ref.dtype)

    def conv_silu(z, sl):
        y = _conv3(z, wc_ref[:, sl], bc_ref[:, sl], seg)
        return (y * jax.nn.sigmoid(y)) * cs_ref[:, sl]

    @pl.when(j <= PM_K)
    def _():
        run(conv_silu)

    @pl.when(j == PM_V)
    def _():
        run(lambda z, sl: z)

    @pl.when(j >= PM_O)
    def _():
        run(lambda z, sl: jax.nn.sigmoid(z))


def _proj_main(h, w, b, wc, bc, colscale, seg, tm):
    bsz, L, d = h.shape
    n = w.shape[1]
    qk = lambda b_, i, j: (0, jnp.minimum(j, PM_K))
    return pl.pallas_call(
        functools.partial(_proj_main_kernel, seg=seg),
        grid=(bsz, L // tm, n // PROJ_TN),
        in_specs=[pl.BlockSpec((None, tm, d), lambda b_, i, j: (b_, i, 0)),
                  pl.BlockSpec((d, PROJ_TN), lambda b_, i, j: (0, j)),
                  pl.BlockSpec((1, PROJ_TN), lambda b_, i, j: (0, j)),
                  pl.BlockSpec((3, PROJ_TN), qk),
                  pl.BlockSpec((1, PROJ_TN), qk),
                  pl.BlockSpec((1, PROJ_TN), qk)],
        out_specs=pl.BlockSpec((None, tm, PROJ_TN), lambda b_, i, j: (b_, i, j)),
        out_shape=jax.ShapeDtypeStruct((bsz, L, n), BF16),
        compiler_params=_cparams("parallel", "parallel", "arbitrary"),
        name="proj_main",
    )(h, w, b.reshape(1, n), wc, bc.reshape(1, -1), colscale.reshape(1, -1))


def _proj_hyena_kernel(h_ref, w0_ref, w1_ref, w2_ref, b_ref, wc_ref, bc_ref, x0_ref, s_ref, *, seg):
    h = h_ref[...]
    us = []
    for g, w_ref in enumerate((w0_ref, w1_ref, w2_ref)):
        z = jnp.dot(h, w_ref[...], preferred_element_type=F32) + b_ref[g]
        us.append(_conv3(z, wc_ref[g], bc_ref[g], seg))
    x0_ref[...] = us[0].astype(x0_ref.dtype)
    s_ref[...] = us[1] * us[2]


def _proj_hyena(h, w, b, wc, bc, seg, tm, tn=512):
    bsz, L, d = h.shape
    nblk = H_WIDTH // tn
    b3 = b.reshape(3, 1, H_WIDTH)
    wc3 = wc.reshape(3, 3, H_WIDTH).transpose(1, 0, 2)
    bc3 = bc.reshape(3, 1, H_WIDTH)
    out_spec = pl.BlockSpec((None, tm, tn), lambda b_, i, j: (b_, i, j))
    return pl.pallas_call(
        functools.partial(_proj_hyena_kernel, seg=seg),
        grid=(bsz, L // tm, nblk),
        in_specs=[pl.BlockSpec((None, tm, d), lambda b_, i, j: (b_, i, 0)),
                  pl.BlockSpec((d, tn), lambda b_, i, j: (0, j)),
                  pl.BlockSpec((d, tn), lambda b_, i, j: (0, nblk + j)),
                  pl.BlockSpec((d, tn), lambda b_, i, j: (0, 2 * nblk + j)),
                  pl.BlockSpec((3, 1, tn), lambda b_, i, j: (0, 0, j)),
                  pl.BlockSpec((3, 3, tn), lambda b_, i, j: (0, 0, j)),
                  pl.BlockSpec((3, 1, tn), lambda b_, i, j: (0, 0, j))],
        out_specs=[out_spec, out_spec],
        out_shape=[jax.ShapeDtypeStruct((bsz, L, H_WIDTH), BF16),
                   jax.ShapeDtypeStruct((bsz, L, H_WIDTH), F32)],
        compiler_params=_cparams("parallel", "parallel", "arbitrary"),
        name="proj_hyena",
    )(h, w, w, w, b3, wc3, bc3)


N_GATES = 4 * M_HEADS


def _split3(x):
    hi = x.astype(BF16)
    r1 = x - hi.astype(F32)
    mid = r1.astype(BF16)
    lo = (r1 - mid.astype(F32)).astype(BF16)
    return hi, mid, lo


def _log_sigmoid(x):
    return jnp.minimum(x, 0.0) - jnp.log1p(jnp.exp(-jnp.abs(x)))


def _gates_kernel(h_ref, w_ref, wt_ref, b_ref, bt_ref, bc_ref, ac_ref, ar_ref):
    h = h_ref[...]
    t = h.shape[0]
    z = jnp.dot(h, w_ref[...], preferred_element_type=F32) + b_ref[...]
    zt = lax.dot_general(wt_ref[...], h, (((1,), (1,)), ((), ())),
                         preferred_element_type=F32) + bt_ref[...]
    r = lax.broadcasted_iota(jnp.int32, (t, t), 0)
    c = lax.broadcasted_iota(jnp.int32, (t, t), 1)
    lower = (r >= c).astype(BF16)
    upper = (r <= c).astype(BF16)

    lf = _log_sigmoid(z)
    parts = _split3(lf)
    cf = sum(jnp.dot(lower, p, preferred_element_type=F32) for p in parts)
    cb = sum(jnp.dot(upper, p, preferred_element_type=F32) for p in parts)
    lane = lax.broadcasted_iota(jnp.int32, z.shape, 1)
    bc = jnp.where(lane < FG_LANE0 + M_HEADS, cf, cb)
    bc = pltpu.roll(bc, LANES - FG_LANE0, 1)
    bc_ref[...] = bc
    ac_ref[...] = z - bc

    lft = _log_sigmoid(zt[FG_LANE0:, :])
    tparts = _split3(lft)
    cft = sum(jnp.dot(p, upper, preferred_element_type=F32) for p in tparts)
    cbt = sum(jnp.dot(p, lower, preferred_element_type=F32) for p in tparts)
    row = lax.broadcasted_iota(jnp.int32, cft.shape, 0)
    ar_ref[...] = zt[:FG_LANE0, :] - jnp.where(row < M_HEADS, cft, cbt)


FG_LANE0 = 2 * M_HEADS


def _gates(h, w_g, b_g, chunk):
    bsz, L, d = h.shape
    w_pad = jnp.zeros((d, LANES), F32).at[:, :N_GATES].set(w_g).astype(BF16)
    b_pad = jnp.zeros((1, LANES), F32).at[0, :N_GATES].set(b_g)
    wt = w_g.T.astype(BF16)
    bt = b_g.reshape(N_GATES, 1)
    tok = pl.BlockSpec((None, chunk, LANES), lambda b_, i: (b_, i, 0))
    return pl.pallas_call(
        _gates_kernel,
        grid=(bsz, L // chunk),
        in_specs=[pl.BlockSpec((None, chunk, d), lambda b_, i: (b_, i, 0)),
                  pl.BlockSpec((d, LANES), lambda b_, i: (0, 0)),
                  pl.BlockSpec((N_GATES, d), lambda b_, i: (0, 0)),
                  pl.BlockSpec((1, LANES), lambda b_, i: (0, 0)),
                  pl.BlockSpec((N_GATES, 1), lambda b_, i: (0, 0))],
        out_specs=[tok, tok, pl.BlockSpec((None, FG_LANE0, chunk), lambda b_, i: (b_, 0, i))],
        out_shape=[jax.ShapeDtypeStruct((bsz, L, LANES), F32),
                   jax.ShapeDtypeStruct((bsz, L, LANES), F32),
                   jax.ShapeDtypeStruct((bsz, FG_LANE0, L), F32)],
        compiler_params=_cparams("parallel", "parallel"),
        name="mlstm_gates",
    )(h, w_pad, wt, b_pad, bt)


def _mlstm_kernel(*refs, emit_h, n_chunks):
    if emit_h:
        (q_ref, k_ref, v_ref, bc_ref, ac_ref, ar_ref, c0_ref, n0_ref, m0_ref,
         h_ref, cf_ref, nf_ref, mf_ref, c_sc, n_sc, m_sc) = refs
    else:
        (k_ref, v_ref, bc_ref, ac_ref, ar_ref, c0_ref, n0_ref, m0_ref,
         cf_ref, nf_ref, mf_ref, c_sc, n_sc, m_sc) = refs
    d = pl.program_id(1)
    j = pl.program_id(2)
    fwd = d == 0
    t = k_ref.shape[0]
    dh = M_HEAD_DIM

    @pl.when(j == 0)
    def _():
        c_sc[...] = c0_ref[...]
        n_sc[...] = n0_ref[...]
        m_sc[...] = m0_ref[...]

    r = lax.broadcasted_iota(jnp.int32, (t, t), 0)
    c = lax.broadcasted_iota(jnp.int32, (t, t), 1)
    causal = jnp.where(fwd, r - c, c - r) >= 0
    bc_all = bc_ref[...]
    ac_all = ac_ref[...]
    ar_all = ar_ref[...]
    for hd in range(M_HEADS):
        sl = slice(hd * dh, (hd + 1) * dh)
        bc = jnp.where(fwd, bc_all[:, hd:hd + 1], bc_all[:, M_HEADS + hd:M_HEADS + hd + 1])
        ac = jnp.where(fwd, ac_all[:, hd:hd + 1], ac_all[:, M_HEADS + hd:M_HEADS + hd + 1])
        ar = jnp.where(fwd, ar_all[hd:hd + 1, :], ar_all[M_HEADS + hd:M_HEADS + hd + 1, :])
        b_tot = jnp.where(fwd, bc[t - 1:t, :], bc[0:1, :])
        m_prev = m_sc[hd][:, 0:1]
        k_h = k_ref[:, sl]
        v_h = v_ref[:, sl]
        if emit_h:
            q_h = q_ref[:, sl]
            dm = jnp.where(causal, bc + ar, NEG_BIG)
            inter = bc + m_prev
            m_t = jnp.maximum(inter, jnp.max(dm, axis=1, keepdims=True))
            qk = lax.dot_general(q_h, k_h, (((1,), (1,)), ((), ())), preferred_element_type=F32)
            s = qk * jnp.exp(dm - m_t)
            carry = jnp.exp(inter - m_t)
            num = (jnp.dot(s.astype(BF16), v_h, preferred_element_type=F32)
                   + carry * jnp.dot(q_h, c_sc[hd].astype(BF16), preferred_element_type=F32))
            den = (jnp.sum(s, axis=1, keepdims=True)
                   + carry * jnp.sum(q_h.astype(F32) * n_sc[hd], axis=1, keepdims=True))
            h_ref[:, sl] = (num / jnp.maximum(jnp.abs(den), jnp.exp(-m_t))).astype(h_ref.dtype)
        g = b_tot + ac
        m_new = jnp.maximum(b_tot + m_prev, jnp.max(g, axis=0, keepdims=True))
        wgt = jnp.exp(g - m_new)
        decay = jnp.exp(b_tot + m_prev - m_new)
        kw = k_h.astype(F32) * wgt
        c_sc[hd] = decay * c_sc[hd] + lax.dot_general(kw.astype(BF16), v_h, (((0,), (0,)), ((), ())),
                                                      preferred_element_type=F32)
        n_sc[hd] = decay * n_sc[hd] + jnp.sum(kw, axis=0, keepdims=True)
        m_sc[hd] = jnp.broadcast_to(m_new, (1, LANES))

    @pl.when(j == n_chunks - 1)
    def _():
        cf_ref[...] = c_sc[...]
        nf_ref[...] = n_sc[...]
        mf_ref[...] = m_sc[...]


def _mlstm(q, k, v, bc, ac, ar, state, emit_h):
    bsz, L, _ = k[0].shape
    t = MLSTM_CHUNK
    nc = L // t
    seq = lambda b_, d, j: (b_, j + d * (nc - 1 - 2 * j), 0)
    st = lambda b_, d, j: (b_, d, 0, 0, 0)

    def tok(col):
        return pl.BlockSpec((None, t, M_WIDTH), lambda b_, d, j: (b_, j + d * (nc - 1 - 2 * j), col))

    gate_spec = pl.BlockSpec((None, t, LANES), seq)
    ar_spec = pl.BlockSpec((None, FG_LANE0, t), lambda b_, d, j: (b_, 0, j + d * (nc - 1 - 2 * j)))
    c_spec = pl.BlockSpec((None, None, M_HEADS, M_HEAD_DIM, M_HEAD_DIM), st)
    n_spec = pl.BlockSpec((None, None, M_HEADS, 1, M_HEAD_DIM), st)
    m_spec = pl.BlockSpec((None, None, M_HEADS, 1, LANES), st)
    state_shapes = [jax.ShapeDtypeStruct((bsz, 2, M_HEADS, M_HEAD_DIM, M_HEAD_DIM), F32),
                    jax.ShapeDtypeStruct((bsz, 2, M_HEADS, 1, M_HEAD_DIM), F32),
                    jax.ShapeDtypeStruct((bsz, 2, M_HEADS, 1, LANES), F32)]
    in_specs = [tok(k[1]), tok(v[1]), gate_spec, gate_spec, ar_spec, c_spec, n_spec, m_spec]
    args = [k[0], v[0], bc, ac, ar, *state]
    out_specs = [c_spec, n_spec, m_spec]
    out_shape = list(state_shapes)
    if emit_h:
        in_specs = [tok(q[1])] + in_specs
        args = [q[0]] + args
        out_specs = [pl.BlockSpec((None, None, t, M_WIDTH),
                                  lambda b_, d, j: (d, b_, j + d * (nc - 1 - 2 * j), 0))] + out_specs
        out_shape = [jax.ShapeDtypeStruct((2, bsz, L, M_WIDTH), BF16)] + out_shape
    outs = pl.pallas_call(
        functools.partial(_mlstm_kernel, emit_h=emit_h, n_chunks=nc),
        grid=(bsz, 2, nc),
        in_specs=in_specs,
        out_specs=out_specs,
        out_shape=out_shape,
        scratch_shapes=[pltpu.VMEM((M_HEADS, M_HEAD_DIM, M_HEAD_DIM), F32),
                        pltpu.VMEM((M_HEADS, 1, M_HEAD_DIM), F32),
                        pltpu.VMEM((M_HEADS, 1, LANES), F32)],
        compiler_params=_cparams("parallel", "parallel", "arbitrary"),
        name="mlstm" if emit_h else "mlstm_state",
    )(*args)
    if emit_h:
        return outs[0], tuple(outs[1:])
    return None, tuple(outs)


DFT_M_TILE = 8
DFT_C_TILE = 512
FEAT_ROWS = 16


def _filter_outer_kernel(bands_ref, w1t_ref, b1_ref, w2t_ref, b2_ref, w3p_ref, w3f_ref, fr_ref, dl_ref, l_ref,
                         a_ref, ss_ref, *, L, n1, n2):
    i = pl.program_id(0)
    h = n1 // 2
    cols = DFT_M_TILE * h

    def positions(shape, axis, side):
        q = lax.broadcasted_iota(jnp.int32, shape, axis)
        mm, jj = q // h, q % h
        n = n2 * (jj + side * h) + i * DFT_M_TILE + mm
        return n, jnp.where(n < L, n, 2 * L - n).astype(F32)

    taps = []
    sumsq = jnp.zeros((1, a_ref.shape[-1]), F32)
    for side, w3_ref in ((0, w3p_ref), (1, w3f_ref)):
        _, p_row = positions((1, cols), 1, side)
        t_row = p_row / float(max(L - 1, 1))
        ang = ((2 * math.pi / L) * p_row) * bands_ref[...]
        row = lax.broadcasted_iota(jnp.int32, (FEAT_ROWS, cols), 0)
        feats = jnp.concatenate([jnp.where(row == 0, t_row, 0.0), jnp.cos(ang), -jnp.sin(ang)], axis=0)
        fr = fr_ref[...]
        hid = jnp.sin(fr * (jnp.dot(w1t_ref[...], feats.astype(BF16), preferred_element_type=F32) + b1_ref[...]))
        hid = jnp.sin(fr * (jnp.dot(w2t_ref[...], hid.astype(BF16), preferred_element_type=F32) + b2_ref[...]))
        filt = lax.dot_general(hid.astype(BF16), w3_ref[...], (((0,), (0,)), ((), ())),
                               preferred_element_type=F32)
        n_col, p_col = positions((cols, 1), 0, side)
        t_col = p_col / float(max(L - 1, 1))
        kern = filt * jnp.exp(-t_col * jnp.abs(dl_ref[...]))
        kern = jnp.where(n_col == L, 0.0, kern)
        sumsq = sumsq + jnp.sum(kern * kern, axis=0, keepdims=True)
        taps.append(kern)

    for mm in range(DFT_M_TILE):
        x = jnp.concatenate([taps[0][mm * h:(mm + 1) * h], taps[1][mm * h:(mm + 1) * h]], axis=0)
        out = jnp.dot(l_ref[...], x.astype(BF16), preferred_element_type=F32)
        a_ref[0, :, mm, :] = out[:n1]
        a_ref[1, :, mm, :] = out[n1:]

    @pl.when(i == 0)
    def _():
        ss_ref[...] = jnp.zeros_like(ss_ref)

    ss_ref[...] += sumsq


def _filter_outer(L, n1, n2, fwd_r, w1, b1, w2, b2, w3, freq):
    hid = H_FILTER_HIDDEN
    bands = jnp.linspace(1e-4, H_POS_BANDS - 1, H_POS_BANDS, dtype=F32).reshape(H_POS_BANDS, 1)
    w1t = jnp.zeros((hid, 3 * FEAT_ROWS), F32)
    w1t = w1t.at[:, 0].set(w1[0]).at[:, FEAT_ROWS:2 * FEAT_ROWS].set(w1[1:1 + H_POS_BANDS].T)
    w1t = w1t.at[:, 2 * FEAT_ROWS:].set(w1[1 + H_POS_BANDS:].T).astype(BF16)
    w3h = w3.astype(BF16)
    max_decay = math.log(H_DECAY_TARGET) / H_FAST_DECAY_PCT
    min_decay = math.log(H_DECAY_TARGET) / H_SLOW_DECAY_PCT
    deltas = jnp.linspace(min_decay, max_decay, H_WIDTH, dtype=F32).reshape(1, H_WIDTH)
    col = lambda v: v.reshape(hid, 1)
    full = lambda a: pl.BlockSpec(a.shape, lambda i: (0,) * a.ndim)
    args = [bands, w1t, col(b1), w2.T.astype(BF16), col(b2)]
    return pl.pallas_call(
        functools.partial(_filter_outer_kernel, L=L, n1=n1, n2=n2),
        grid=(n2 // DFT_M_TILE,),
        in_specs=[full(a) for a in args]
        + [pl.BlockSpec((hid, H_WIDTH), lambda i: (0, 0)), pl.BlockSpec((hid, H_WIDTH), lambda i: (0, 1)),
           full(col(freq)), full(deltas), full(fwd_r)],
        out_specs=[pl.BlockSpec((2, n1, DFT_M_TILE, H_WIDTH), lambda i: (0, 0, i, 0)),
                   pl.BlockSpec((1, H_WIDTH), lambda i: (0, 0))],
        out_shape=[jax.ShapeDtypeStruct((2, n1, n2, H_WIDTH), F32),
                   jax.ShapeDtypeStruct((1, H_WIDTH), F32)],
        compiler_params=_cparams("arbitrary"),
        name="hyena_filter_outer",
    )(*args, w3h, w3h, col(freq), deltas, fwd_r)


def _dft_factors(n):
    lg = int(round(math.log2(n)))
    n1 = 1 << ((lg + 1) // 2)
    return n1, n // n1


def _dft_outer_matrices(n1):
    k = np.arange(n1)[:, None]
    n = np.arange(n1)[None, :]
    ang = 2.0 * np.pi * ((k * n) % n1) / n1
    cr, ci = np.cos(ang), -np.sin(ang)
    h = n1 // 2
    fwd_c = np.block([[cr[:, :h], -ci[:, :h]], [ci[:, :h], cr[:, :h]]])
    fwd_r = np.concatenate([cr, ci], axis=0)
    ir, ii = cr[:h, :], -ci[:h, :]
    inv = np.block([[ir, -ii], [ii, ir]])
    return (jnp.asarray(fwd_c, F32).astype(BF16), jnp.asarray(fwd_r, F32).astype(BF16),
            jnp.asarray(inv, F32).astype(BF16))


def _dft_inner_matrices(n1, n2):
    n = n1 * n2
    k1 = jnp.arange(n1, dtype=jnp.int32)[:, None, None]
    k2 = jnp.arange(n2, dtype=jnp.int32)[None, :, None]
    m = jnp.arange(n2, dtype=jnp.int32)[None, None, :]
    ang = ((m * (k1 + n1 * k2)) % n).astype(F32) * (2.0 * math.pi / n)
    gr, gi = jnp.cos(ang), -jnp.sin(ang)
    g = jnp.concatenate([jnp.concatenate([gr, -gi], axis=2), jnp.concatenate([gi, gr], axis=2)], axis=1)
    return g.astype(BF16), jnp.swapaxes(g, 1, 2).astype(BF16)


def _outer_dft_kernel(l_ref, x_ref, o_ref):
    p_in, p_out = x_ref.shape[0], o_ref.shape[0]
    r_out = o_ref.shape[1]
    for mm in range(x_ref.shape[2]):
        parts = [x_ref[p, :, mm, :] for p in range(p_in)]
        x = parts[0] if p_in == 1 else jnp.concatenate(parts, axis=0)
        out = jnp.dot(l_ref[...], x.astype(BF16), preferred_element_type=F32)
        for p in range(p_out):
            o_ref[p, :, mm, :] = out[p * r_out:(p + 1) * r_out]


def _outer_dft(lmat, x4, p_out):
    p_in, r_in, n2, c = x4.shape
    r_out = lmat.shape[0] // p_out
    tc = min(DFT_C_TILE, c)
    return pl.pallas_call(
        _outer_dft_kernel,
        grid=(n2 // DFT_M_TILE, c // tc),
        in_specs=[pl.BlockSpec(lmat.shape, lambda m, j: (0, 0)),
                  pl.BlockSpec((p_in, r_in, DFT_M_TILE, tc), lambda m, j: (0, 0, m, j))],
        out_specs=pl.BlockSpec((p_out, r_out, DFT_M_TILE, tc), lambda m, j: (0, 0, m, j)),
        out_shape=jax.ShapeDtypeStruct((p_out, r_out, n2, c), F32),
        compiler_params=_cparams("parallel", "parallel"),
        name="dft_outer",
    )(lmat, x4)


DFT_K_TILE = 2


def _inner_fwd_kernel(g_ref, a_ref, o_ref):
    n2 = a_ref.shape[2]
    for kk in range(a_ref.shape[1]):
        a = jnp.concatenate([a_ref[0, kk].astype(BF16), a_ref[1, kk].astype(BF16)], axis=0)
        x = jnp.dot(g_ref[kk], a, preferred_element_type=F32)
        o_ref[0, kk] = x[:n2].astype(o_ref.dtype)
        o_ref[1, kk] = x[n2:].astype(o_ref.dtype)


def _inner_fwd(g, a):
    _, n1, n2, c = a.shape
    blk = pl.BlockSpec((2, DFT_K_TILE, n2, c), lambda k: (0, k, 0, 0))
    return pl.pallas_call(
        _inner_fwd_kernel,
        grid=(n1 // DFT_K_TILE,),
        in_specs=[pl.BlockSpec((DFT_K_TILE, 2 * n2, 2 * n2), lambda k: (k, 0, 0)), blk],
        out_specs=blk,
        out_shape=jax.ShapeDtypeStruct((2, n1, n2, c), BF16),
        compiler_params=_cparams("parallel"),
        name="dft_inner_filter",
    )(g, a)


def _inner_conv_kernel(g_ref, gt_ref, a_ref, k_ref, o_ref):
    n2 = a_ref.shape[2]
    for kk in range(a_ref.shape[1]):
        a = jnp.concatenate([a_ref[0, kk].astype(BF16), a_ref[1, kk].astype(BF16)], axis=0)
        x = jnp.dot(g_ref[kk], a, preferred_element_type=F32)
        xr, xi = x[:n2], x[n2:]
        kr, ki = k_ref[0, kk].astype(F32), k_ref[1, kk].astype(F32)
        yr = xr * kr - xi * ki
        yi = xr * ki + xi * kr
        y = jnp.concatenate([yr.astype(BF16), yi.astype(BF16)], axis=0)
        b = jnp.dot(gt_ref[kk], y, preferred_element_type=F32)
        o_ref[0, kk] = b[:n2]
        o_ref[1, kk] = b[n2:]


def _inner_conv(g, gt, a, kf):
    _, n1, n2, c = a.shape
    blk = pl.BlockSpec((2, DFT_K_TILE, n2, c), lambda k: (0, k, 0, 0))
    mat = pl.BlockSpec((DFT_K_TILE, 2 * n2, 2 * n2), lambda k: (k, 0, 0))
    return pl.pallas_call(
        _inner_conv_kernel,
        grid=(n1 // DFT_K_TILE,),
        in_specs=[mat, mat, blk, blk],
        out_specs=blk,
        out_shape=jax.ShapeDtypeStruct((2, n1, n2, c), F32),
        compiler_params=_cparams("parallel"),
        name="dft_inner_conv",
    )(g, gt, a, kf)


def _hyena_long_conv(s, w1, b1, w2, b2, w3, freq):
    bsz, L, c = s.shape
    assert bsz == 2
    n = 2 * L
    n1, n2 = _dft_factors(n)
    fwd_c, fwd_r, inv = _dft_outer_matrices(n1)
    g, gt = _dft_inner_matrices(n1, n2)
    af, sumsq = _filter_outer(L, n1, n2, fwd_r, w1, b1, w2, b2, w3, freq)
    kf = _inner_fwd(g, af)
    a = _outer_dft(fwd_c, s.reshape(2, n1 // 2, n2, c), 2)
    b = _inner_conv(g, gt, a, kf)
    y = _outer_dft(inv, b, 2)
    return y.reshape(2, L, c), sumsq


def _merge_kernel(hf_ref, hb_ref, o_ref, x0_ref, s_ref, y_ref, ga_ref, gb_ref, x_ref,
                  ysc_ref, hbias_ref, gate_ref, g2_ref, sh_ref, sc_ref,
                  wa_ref, wb_ref, wo_ref, x1_ref, h2_ref):
    a = o_ref[...].astype(F32) * (hf_ref[...].astype(F32) + hb_ref[...].astype(F32))
    s = s_ref[...]
    hy = x0_ref[...].astype(F32) * (y_ref[...] * ysc_ref[...] + hbias_ref[...] * s)
    pa = jnp.dot(a.astype(BF16), wa_ref[...], preferred_element_type=F32)
    pb = jnp.dot(hy.astype(BF16), wb_ref[...], preferred_element_type=F32)
    mix = ga_ref[...].astype(F32) * pa + gb_ref[...].astype(F32) * pb
    out = jnp.dot(mix.astype(BF16), wo_ref[...], preferred_element_type=F32)
    x1 = x_ref[...] + gate_ref[...] * out
    x1_ref[...] = x1
    y = x1 * lax.rsqrt(jnp.mean(x1 * x1, axis=-1, keepdims=True) + EPS) * g2_ref[...]
    h2_ref[...] = y * (1.0 + sc_ref[...]) + sh_ref[...]


def _merge(hdirs, pm, x0, s, y, x, yscale, h_bias, gate1, g2, shift2, scale2, w_a, w_b, w_out, tm=256):
    bsz, L, d = x.shape
    tok = pl.BlockSpec((None, tm, d), lambda b, i: (b, i, 0))

    def pm_tile(col):
        return pl.BlockSpec((None, tm, d), lambda b, i: (b, i, col))

    vec = pl.BlockSpec((1, d), lambda b, i: (0, 0))
    bvec = pl.BlockSpec((None, 1, d), lambda b, i: (b, 0, 0))
    wsp = pl.BlockSpec((d, d), lambda b, i: (0, 0))
    return pl.pallas_call(
        _merge_kernel,
        grid=(bsz, L // tm),
        in_specs=[pl.BlockSpec((None, None, tm, d), lambda b, i: (0, b, i, 0)),
                  pl.BlockSpec((None, None, tm, d), lambda b, i: (1, b, i, 0)),
                  pm_tile(PM_O), tok, tok, tok, pm_tile(PM_GA), pm_tile(PM_GB), tok,
                  vec, vec, bvec, vec, bvec, bvec, wsp, wsp, wsp],
        out_specs=[tok, tok],
        out_shape=[jax.ShapeDtypeStruct((bsz, L, d), F32), jax.ShapeDtypeStruct((bsz, L, d), F32)],
        compiler_params=_cparams("parallel", "parallel"),
        name="merge",
    )(hdirs, hdirs, pm, x0, s, y, pm, pm, x, yscale, h_bias.reshape(1, d), gate1, g2.reshape(1, d),
      shift2, scale2, w_a, w_b, w_out)


MOE_BLOCK = 256
ROUTE_E1, ROUTE_E2, ROUTE_W1, ROUTE_W2 = 0, 1, 2, 3
EXP_LANE0 = N_GROUPS


def _first_lane_of_max(val, valid, lane):
    masked = jnp.where(valid, val, NEG_BIG)
    mx = jnp.max(masked, axis=1, keepdims=True)
    idx = jnp.min(jnp.where(valid & (masked == mx), lane, LANES), axis=1, keepdims=True)
    return mx, idx


def _router_kernel(h_ref, w_ref, b_ref, r_ref):
    logits = jnp.dot(h_ref[...].astype(BF16), w_ref[...], preferred_element_type=F32) + b_ref[...]
    lane = lax.broadcasted_iota(jnp.int32, logits.shape, 1)
    is_g = lane < N_GROUPS
    gmax, gsel = _first_lane_of_max(logits, is_g, lane)
    gsum = jnp.sum(jnp.where(is_g, jnp.exp(logits - gmax), 0.0), axis=1, keepdims=True)
    gw = 1.0 / gsum
    lo = EXP_LANE0 + gsel * EXPERTS_PER_GROUP
    in_grp = (lane >= lo) & (lane < lo + EXPERTS_PER_GROUP)
    emax, l1 = _first_lane_of_max(logits, in_grp, lane)
    esum = jnp.sum(jnp.where(in_grp, jnp.exp(logits - emax), 0.0), axis=1, keepdims=True)
    e2max, l2 = _first_lane_of_max(logits, in_grp & (lane != l1), lane)
    v1 = 1.0 / esum
    v2 = jnp.exp(e2max - emax) / esum
    vs = v1 + v2
    w1 = gw * v1 / vs
    w2 = gw * v2 / vs
    e1 = (l1 - EXP_LANE0).astype(F32)
    e2 = (l2 - EXP_LANE0).astype(F32)
    r_ref[...] = jnp.where(lane == ROUTE_E1, e1,
                           jnp.where(lane == ROUTE_E2, e2,
                                     jnp.where(lane == ROUTE_W1, w1,
                                               jnp.where(lane == ROUTE_W2, w2, 0.0))))


def _router(h2, w_group, b_group, w_router, b_router, tm=1024):
    n, d = h2.shape
    w = jnp.zeros((d, LANES), F32).at[:, :N_GROUPS].set(w_group).at[
        :, EXP_LANE0:EXP_LANE0 + N_EXPERTS].set(w_router).astype(BF16)
    b = jnp.zeros((1, LANES), F32).at[0, :N_GROUPS].set(b_group).at[
        0, EXP_LANE0:EXP_LANE0 + N_EXPERTS].set(b_router)
    return pl.pallas_call(
        _router_kernel,
        grid=(n // tm,),
        in_specs=[pl.BlockSpec((tm, d), lambda i: (i, 0)),
                  pl.BlockSpec((d, LANES), lambda i: (0, 0)),
                  pl.BlockSpec((1, LANES), lambda i: (0, 0))],
        out_specs=pl.BlockSpec((tm, LANES), lambda i: (i, 0)),
        out_shape=jax.ShapeDtypeStruct((n, LANES), F32),
        compiler_params=_cparams("parallel"),
        name="moe_router",
    )(h2, w, b)


def _slots_kernel(r_ref, dest_ref, cnt_ref, run_sc, start_sc):
    ph = pl.program_id(0)
    i = pl.program_id(1)
    rec = r_ref[...]
    tm = rec.shape[0]
    lane = lax.broadcasted_iota(jnp.int32, rec.shape, 1)
    e1 = rec[:, ROUTE_E1:ROUTE_E1 + 1].astype(jnp.int32)
    e2 = rec[:, ROUTE_E2:ROUTE_E2 + 1].astype(jnp.int32)
    oh1 = lane == e1
    oh2 = lane == e2
    oh = (oh1 | oh2).astype(F32)

    @pl.when((ph == 0) & (i == 0))
    def _():
        run_sc[...] = jnp.zeros_like(run_sc)

    @pl.when(ph == 0)
    def _():
        run_sc[...] += jnp.sum(oh, axis=0, keepdims=True)

    @pl.when((ph == 1) & (i == 0))
    def _():
        counts = run_sc[...]
        cnt_ref[...] = counts
        nblk = jnp.floor((counts + (MOE_BLOCK - 1)) * (1.0 / MOE_BLOCK))
        rr = lax.broadcasted_iota(jnp.int32, (LANES, LANES), 0)
        cc = lax.broadcasted_iota(jnp.int32, (LANES, LANES), 1)
        before = (rr < cc).astype(BF16)
        first = jnp.dot(nblk.astype(BF16), before, preferred_element_type=F32)
        start_sc[...] = first * float(MOE_BLOCK)
        run_sc[...] = jnp.zeros_like(run_sc)

    @pl.when(ph == 1)
    def _():
        r = lax.broadcasted_iota(jnp.int32, (tm, tm), 0)
        c = lax.broadcasted_iota(jnp.int32, (tm, tm), 1)
        earlier = (r > c).astype(BF16)
        rank = jnp.dot(earlier, oh.astype(BF16), preferred_element_type=F32) + run_sc[...] + start_sc[...]
        d1 = jnp.sum(jnp.where(oh1, rank, 0.0), axis=1, keepdims=True)
        d2 = jnp.sum(jnp.where(oh2, rank, 0.0), axis=1, keepdims=True)
        dest_ref[...] = jnp.where(lane == 0, d1, jnp.where(lane == 1, d2, 0.0)).astype(jnp.int32)
        run_sc[...] += jnp.sum(oh, axis=0, keepdims=True)


def _slots(route, tm=512):
    n = route.shape[0]
    return pl.pallas_call(
        _slots_kernel,
        grid=(2, n // tm),
        in_specs=[pl.BlockSpec((tm, LANES), lambda p, i: (i, 0))],
        out_specs=[pl.BlockSpec((tm, LANES), lambda p, i: (i * p, 0)),
                   pl.BlockSpec((1, LANES), lambda p, i: (0, 0))],
        out_shape=[jax.ShapeDtypeStruct((n, LANES), jnp.int32), jax.ShapeDtypeStruct((1, LANES), F32)],
        scratch_shapes=[pltpu.VMEM((1, LANES), F32), pltpu.VMEM((1, LANES), F32)],
        compiler_params=_cparams("arbitrary", "arbitrary"),
        name="moe_slots",
    )(route)


DMA_UNROLL = 8


def _row_copy(src_ref, dst_ref, sem, src_row, dst_row):
    return pltpu.make_async_copy(src_ref.at[pl.ds(src_row, 1)], dst_ref.at[pl.ds(dst_row, 1)], sem)


def _dispatch_kernel(dest_ref, h_ref, xs_in_ref, xs_ref, sem):
    del xs_in_ref
    tm = h_ref.shape[0]

    def start(r, carry):
        _row_copy(h_ref, xs_ref, sem, r, dest_ref[0, 2 * r]).start(priority=0)
        _row_copy(h_ref, xs_ref, sem, r, dest_ref[0, 2 * r + 1]).start(priority=1)
        return carry

    lax.fori_loop(0, tm, start, 0, unroll=DMA_UNROLL)

    def wait(r, carry):
        _row_copy(h_ref, xs_ref, sem, 0, 0).wait()
        _row_copy(h_ref, xs_ref, sem, 0, 0).wait()
        return carry

    lax.fori_loop(0, tm, wait, 0, unroll=DMA_UNROLL)


def _dispatch(h2, dest, n_slots, tm=256):
    n, d = h2.shape
    dest3 = dest.reshape(n // tm, 1, 2 * tm)
    zeros = jnp.zeros((n_slots, d), F32)
    return pl.pallas_call(
        _dispatch_kernel,
        grid=(n // tm,),
        in_specs=[pl.BlockSpec((None, 1, 2 * tm), lambda i: (i, 0, 0), memory_space=pltpu.SMEM),
                  pl.BlockSpec((tm, d), lambda i: (i, 0)),
                  pl.BlockSpec(memory_space=pl.ANY)],
        out_specs=pl.BlockSpec(memory_space=pl.ANY),
        out_shape=jax.ShapeDtypeStruct((n_slots, d), F32),
        scratch_shapes=[pltpu.SemaphoreType.DMA(())],
        input_output_aliases={2: 0},
        compiler_params=_cparams("arbitrary"),
        name="moe_dispatch",
    )(dest3, h2, zeros)


def _experts_kernel(be_ref, first_ref, nxt_ref, par_ref, nu_ref, x_ref, w1_hbm, w3_hbm, w2_hbm, o_ref,
                    w1f, w3f, w2f, w1b, w3b, w2b, sems):
    i = pl.program_id(0)

    def weight_copies(e, slot):
        return (pltpu.make_async_copy(w1_hbm.at[e], w1f.at[slot], sems.at[0, slot]),
                pltpu.make_async_copy(w3_hbm.at[e], w3f.at[slot], sems.at[1, slot]),
                pltpu.make_async_copy(w2_hbm.at[e], w2f.at[slot], sems.at[2, slot]))

    @pl.when(i == 0)
    def _():
        for cp in weight_copies(be_ref[0], 0):
            cp.start()

    @pl.when(first_ref[i] == 1)
    def _():
        slot = par_ref[i]

        @pl.when(nxt_ref[i] >= 0)
        def _():
            for cp in weight_copies(nxt_ref[i], 1 - slot):
                cp.start()

        for cp in weight_copies(be_ref[i], slot):
            cp.wait()
        w1b[...] = w1f[slot].astype(BF16)
        w3b[...] = w3f[slot].astype(BF16)
        w2b[...] = w2f[slot].astype(BF16)

    @pl.when(i < nu_ref[0])
    def _():
        x = x_ref[...].astype(BF16)
        a = jnp.dot(x, w1b[...], preferred_element_type=F32)
        b = jnp.dot(x, w3b[...], preferred_element_type=F32)
        hmid = (a * jax.nn.sigmoid(a)) * b
        o_ref[...] = jnp.dot(hmid.astype(BF16), w2b[...], preferred_element_type=F32)

    @pl.when(i >= nu_ref[0])
    def _():
        o_ref[...] = jnp.zeros_like(o_ref)


def _experts(xs, block_e, n_used, w1_e, w3_e, w2_e):
    n_slots, d = xs.shape
    nb = n_slots // MOE_BLOCK
    de = w1_e.shape[2]
    idx = jnp.arange(nb, dtype=jnp.int32)
    used = idx < n_used[0]
    first = used & ((idx == 0) | (block_e != jnp.roll(block_e, 1)))
    ordinal = jnp.cumsum(first.astype(jnp.int32)) - 1
    par = (ordinal % 2).astype(jnp.int32)
    first_pos = jnp.where(first, idx, nb)
    next_first = lax.cummin(jnp.concatenate([first_pos[1:], jnp.full((1,), nb, jnp.int32)]), reverse=True)
    nxt = jnp.where(next_first < nb, block_e[jnp.minimum(next_first, nb - 1)], -1).astype(jnp.int32)
    any_spec = pl.BlockSpec(memory_space=pl.ANY)
    grid_spec = pltpu.PrefetchScalarGridSpec(
        num_scalar_prefetch=5,
        grid=(nb,),
        in_specs=[pl.BlockSpec((MOE_BLOCK, d), lambda i, *_: (i, 0)), any_spec, any_spec, any_spec],
        out_specs=pl.BlockSpec((MOE_BLOCK, d), lambda i, *_: (i, 0)),
        scratch_shapes=[pltpu.VMEM((2, d, de), F32), pltpu.VMEM((2, d, de), F32), pltpu.VMEM((2, de, d), F32),
                        pltpu.VMEM((d, de), BF16), pltpu.VMEM((d, de), BF16), pltpu.VMEM((de, d), BF16),
                        pltpu.SemaphoreType.DMA((3, 2))],
    )
    return pl.pallas_call(
        _experts_kernel,
        grid_spec=grid_spec,
        out_shape=jax.ShapeDtypeStruct((n_slots, d), F32),
        compiler_params=_cparams("arbitrary"),
        name="moe_experts",
    )(block_e, first.astype(jnp.int32), nxt, par, n_used, xs, w1_e, w3_e, w2_e)


def _combine_kernel(dest_ref, r_ref, x_ref, gate_ref, gf_ref, ys_ref, o_ref, buf1, buf2, sem):
    tm = x_ref.shape[0]

    def start(r, carry):
        _row_copy(ys_ref, buf1, sem, dest_ref[0, 2 * r], r).start(priority=0)
        _row_copy(ys_ref, buf2, sem, dest_ref[0, 2 * r + 1], r).start(priority=1)
        return carry

    lax.fori_loop(0, tm, start, 0, unroll=DMA_UNROLL)

    def wait(r, carry):
        _row_copy(ys_ref, buf1, sem, 0, 0).wait()
        _row_copy(ys_ref, buf2, sem, 0, 0).wait()
        return carry

    lax.fori_loop(0, tm, wait, 0, unroll=DMA_UNROLL)
    rec = r_ref[...]
    y = buf1[...] * rec[:, ROUTE_W1:ROUTE_W1 + 1] + buf2[...] * rec[:, ROUTE_W2:ROUTE_W2 + 1]
    x2 = x_ref[...] + gate_ref[...] * y
    o_ref[...] = x2 * lax.rsqrt(jnp.mean(x2 * x2, axis=-1, keepdims=True) + EPS) * gf_ref[...]


def _combine(ys, dest, route, x1, gate2, g_final, tm=256):
    bsz, L, d = x1.shape
    n = bsz * L
    tpb = L // tm
    dest3 = dest.reshape(n // tm, 1, 2 * tm)
    return pl.pallas_call(
        _combine_kernel,
        grid=(bsz, tpb),
        in_specs=[pl.BlockSpec((None, 1, 2 * tm), lambda b, i: (b * tpb + i, 0, 0), memory_space=pltpu.SMEM),
                  pl.BlockSpec((tm, LANES), lambda b, i: (b * tpb + i, 0)),
                  pl.BlockSpec((None, tm, d), lambda b, i: (b, i, 0)),
                  pl.BlockSpec((None, 1, d), lambda b, i: (b, 0, 0)),
                  pl.BlockSpec((1, d), lambda b, i: (0, 0)),
                  pl.BlockSpec(memory_space=pl.ANY)],
        out_specs=pl.BlockSpec((None, tm, d), lambda b, i: (b, i, 0)),
        out_shape=jax.ShapeDtypeStruct((bsz, L, d), F32),
        scratch_shapes=[pltpu.VMEM((tm, d), F32), pltpu.VMEM((tm, d), F32), pltpu.SemaphoreType.DMA(())],
        compiler_params=_cparams("arbitrary", "arbitrary"),
        name="moe_combine",
    )(dest3, route, x1, gate2, g_final.reshape(1, d), ys)


SLOT_TM = 512


def _slot_table_kernel(dest_ref, init_ref, tbl_ref, sem, *, plane):
    i = pl.program_id(0)

    @pl.when(i == 0)
    def _():
        cp = pltpu.make_async_copy(init_ref, tbl_ref, sem)
        cp.start()
        cp.wait()

    def body(r, carry):
        tok = i * SLOT_TM + r
        tbl_ref[MOE_BLOCK + dest_ref[0, 2 * r]] = tok
        tbl_ref[MOE_BLOCK + dest_ref[0, 2 * r + 1]] = tok + plane
        return carry

    lax.fori_loop(0, SLOT_TM, body, 0, unroll=DMA_UNROLL)


def _slot_table(dest, n, nb):
    size = (nb + 1) * MOE_BLOCK
    init = n + (jnp.arange(size, dtype=jnp.int32) % MOE_BLOCK)
    tbl = pl.pallas_call(
        functools.partial(_slot_table_kernel, plane=n + MOE_BLOCK),
        grid=(n // SLOT_TM,),
        in_specs=[pl.BlockSpec((None, 1, 2 * SLOT_TM), lambda i: (i, 0, 0), memory_space=pltpu.SMEM),
                  pl.BlockSpec(memory_space=pl.ANY)],
        out_specs=pl.BlockSpec(memory_space=pltpu.SMEM),
        out_shape=jax.ShapeDtypeStruct((size,), jnp.int32),
        scratch_shapes=[pltpu.SemaphoreType.DMA(())],
        compiler_params=_cparams("arbitrary"),
        name="moe_slot_table",
    )(dest.reshape(n // SLOT_TM, 1, 2 * SLOT_TM), init)
    return tbl.reshape(nb + 1, 1, MOE_BLOCK)


def _moe_pair_kernel(be_ref, first_ref, nxt_ref, par_ref, nu_ref,
                     t_prev, t_b0, t_b1, t_next, h_hbm, w1_hbm, w3_hbm, w2_hbm, yt_hbm,
                     x0, x1, y0, y1, w1f, w3f, w2f, w1b, w3b, w2b, wsem, gsem, ssem, *, n_tok):
    p = pl.program_id(0)
    n_used = nu_ref[0]
    plane = n_tok + MOE_BLOCK
    yt_flat = yt_hbm

    def gather(tbl, xbuf, sem):
        for r in range(MOE_BLOCK):
            a = tbl[0, r]
            tok = jnp.minimum(jnp.where(a >= plane, a - plane, a), n_tok - 1)
            _row_copy(h_hbm, xbuf, sem, tok, r).start(priority=r % 2)

    def scatter(tbl, ybuf, sem):
        for r in range(MOE_BLOCK):
            _row_copy(ybuf, yt_flat, sem, r, tbl[0, r]).start(priority=r % 2)

    def wait_rows(src, dst, sem):
        for _ in range(MOE_BLOCK):
            _row_copy(src, dst, sem, 0, 0).wait()

    def weight_copies(e, slot):
        return (pltpu.make_async_copy(w1_hbm.at[e], w1f.at[slot], wsem.at[0, slot]),
                pltpu.make_async_copy(w3_hbm.at[e], w3f.at[slot], wsem.at[1, slot]),
                pltpu.make_async_copy(w2_hbm.at[e], w2f.at[slot], wsem.at[2, slot]))

    def maybe_new_weights(b):
        @pl.when(first_ref[b] == 1)
        def _():
            slot = par_ref[b]

            @pl.when(nxt_ref[b] >= 0)
            def _():
                for cp in weight_copies(nxt_ref[b], 1 - slot):
                    cp.start()

            for cp in weight_copies(be_ref[b], slot):
                cp.wait()
            w1b[...] = w1f[slot].astype(BF16)
            w3b[...] = w3f[slot].astype(BF16)
            w2b[...] = w2f[slot].astype(BF16)

    def expert_mlp(xbuf, ybuf):
        x = xbuf[...].astype(BF16)
        a = jnp.dot(x, w1b[...], preferred_element_type=F32)
        b = jnp.dot(x, w3b[...], preferred_element_type=F32)
        hmid = (a * jax.nn.sigmoid(a)) * b
        ybuf[...] = jnp.dot(hmid.astype(BF16), w2b[...], preferred_element_type=F32)

    @pl.when(p == 0)
    def _():
        for cp in weight_copies(be_ref[0], 0):
            cp.start()
        y0[...] = jnp.zeros_like(y0)
        y1[...] = jnp.zeros_like(y1)
        gather(t_b0, x0, gsem.at[0])
        for r in range(MOE_BLOCK):
            _row_copy(y0, yt_flat, ssem.at[0], r, plane + n_tok + r).start(priority=r % 2)

    @pl.when(2 * p < n_used)
    def _():
        maybe_new_weights(2 * p)
        wait_rows(h_hbm, x0, gsem.at[0])
        wait_rows(y0, yt_flat, ssem.at[0])
        gather(t_b1, x1, gsem.at[1])
        scatter(t_prev, y1, ssem.at[1])
        expert_mlp(x0, y0)
        maybe_new_weights(2 * p + 1)
        wait_rows(h_hbm, x1, gsem.at[1])
        wait_rows(y1, yt_flat, ssem.at[1])
        gather(t_next, x0, gsem.at[0])
        scatter(t_b0, y0, ssem.at[0])
        expert_mlp(x1, y1)

        @pl.when(2 * p + 2 >= n_used)
        def _():
            scatter(t_b1, y1, ssem.at[1])
            wait_rows(y1, yt_flat, ssem.at[1])
            wait_rows(y0, yt_flat, ssem.at[0])
            wait_rows(h_hbm, x0, gsem.at[0])


def _moe_pairs(h2f, table, block_e, n_used, w1_e, w3_e, w2_e):
    n, d = h2f.shape
    nb = table.shape[0] - 1
    de = w1_e.shape[2]
    idx = jnp.arange(nb, dtype=jnp.int32)
    used = idx < n_used[0]
    first = used & ((idx == 0) | (block_e != jnp.roll(block_e, 1)))
    ordinal = jnp.cumsum(first.astype(jnp.int32)) - 1
    par = (ordinal % 2).astype(jnp.int32)
    first_pos = jnp.where(first, idx, nb)
    next_first = lax.cummin(jnp.concatenate([first_pos[1:], jnp.full((1,), nb, jnp.int32)]), reverse=True)
    nxt = jnp.where(next_first < nb, block_e[jnp.minimum(next_first, nb - 1)], -1).astype(jnp.int32)
    any_spec = pl.BlockSpec(memory_space=pl.ANY)

    def tbl(fn):
        return pl.BlockSpec((None, 1, MOE_BLOCK), lambda p, *_: (fn(p), 0, 0), memory_space=pltpu.SMEM)

    grid_spec = pltpu.PrefetchScalarGridSpec(
        num_scalar_prefetch=5,
        grid=(nb // 2,),
        in_specs=[tbl(lambda p: 2 * p), tbl(lambda p: 2 * p + 1), tbl(lambda p: 2 * p + 2),
                  tbl(lambda p: jnp.minimum(2 * p + 3, nb)), any_spec, any_spec, any_spec, any_spec],
        out_specs=any_spec,
        scratch_shapes=[pltpu.VMEM((MOE_BLOCK, d), F32), pltpu.VMEM((MOE_BLOCK, d), F32),
                        pltpu.VMEM((MOE_BLOCK, d), F32), pltpu.VMEM((MOE_BLOCK, d), F32),
                        pltpu.VMEM((2, d, de), F32), pltpu.VMEM((2, d, de), F32), pltpu.VMEM((2, de, d), F32),
                        pltpu.VMEM((d, de), BF16), pltpu.VMEM((d, de), BF16), pltpu.VMEM((de, d), BF16),
                        pltpu.SemaphoreType.DMA((3, 2)), pltpu.SemaphoreType.DMA((2,)),
                        pltpu.SemaphoreType.DMA((2,))],
    )
    return pl.pallas_call(
        functools.partial(_moe_pair_kernel, n_tok=n),
        grid_spec=grid_spec,
        out_shape=jax.ShapeDtypeStruct((2 * (n + MOE_BLOCK), d), F32),
        compiler_params=_cparams("arbitrary"),
        name="moe_experts_fused",
    )(block_e, first.astype(jnp.int32), nxt, par, n_used, table, table, table, table, h2f, w1_e, w3_e, w2_e)


def _combine_planes_kernel(r_ref, ya_ref, yb_ref, x_ref, gate_ref, gf_ref, o_ref):
    rec = r_ref[...]
    y = ya_ref[...] * rec[:, ROUTE_W1:ROUTE_W1 + 1] + yb_ref[...] * rec[:, ROUTE_W2:ROUTE_W2 + 1]
    x2 = x_ref[...] + gate_ref[...] * y
    o_ref[...] = x2 * lax.rsqrt(jnp.mean(x2 * x2, axis=-1, keepdims=True) + EPS) * gf_ref[...]


def _combine_planes(yt, route, x1, gate2, g_final, tm=512):
    bsz, L, d = x1.shape
    n = bsz * L
    tpb = L // tm
    yt3 = yt.reshape(2, n + MOE_BLOCK, d)
    return pl.pallas_call(
        _combine_planes_kernel,
        grid=(bsz, tpb),
        in_specs=[pl.BlockSpec((tm, LANES), lambda b, i: (b * tpb + i, 0)),
                  pl.BlockSpec((None, tm, d), lambda b, i: (0, b * tpb + i, 0)),
                  pl.BlockSpec((None, tm, d), lambda b, i: (1, b * tpb + i, 0)),
                  pl.BlockSpec((None, tm, d), lambda b, i: (b, i, 0)),
                  pl.BlockSpec((None, 1, d), lambda b, i: (b, 0, 0)),
                  pl.BlockSpec((1, d), lambda b, i: (0, 0))],
        out_specs=pl.BlockSpec((None, tm, d), lambda b, i: (b, i, 0)),
        out_shape=jax.ShapeDtypeStruct((bsz, L, d), F32),
        compiler_params=_cparams("parallel", "parallel"),
        name="moe_combine",
    )(route, yt3, yt3, x1, gate2, g_final.reshape(1, d))


def _moe(h2, x1, gate2, g_final, w_group, b_group, w_router, b_router, w1_e, w3_e, w2_e):
    bsz, L, d = x1.shape
    n = bsz * L
    h2f = h2.reshape(n, d)
    route = _router(h2f, w_group, b_group, w_router, b_router)
    dest_rec, counts = _slots(route)
    dest = dest_rec[:, :2].reshape(2 * n)
    nb = (2 * n) // MOE_BLOCK + N_EXPERTS
    cnt = counts[0, :N_EXPERTS].astype(jnp.int32)
    blocks_per_e = (cnt + MOE_BLOCK - 1) // MOE_BLOCK
    ends = jnp.cumsum(blocks_per_e)
    block_e = jnp.clip(jnp.searchsorted(ends, jnp.arange(nb, dtype=jnp.int32), side='right'),
                       0, N_EXPERTS - 1).astype(jnp.int32)
    n_used = ends[-1:].astype(jnp.int32)
    table = _slot_table(dest, n, nb)
    yt = _moe_pairs(h2f, table, block_e, n_used, w1_e, w3_e, w2_e)
    return _combine_planes(yt, route, x1, gate2, g_final)


def kernel(x, c, ctx, c_ctx, w_mod, b_mod, g_norm1, g_norm2, w_in, b_in, w_qk_conv, b_qk_conv,
           w_h_conv, b_h_conv, hf_w1, hf_b1, hf_w2, hf_b2, hf_w3, hf_freq, h_bias, w_a, w_b, w_out,
           w_group, b_group, w_router, b_router, w1_e, w3_e, w2_e, g_final):
    assert w_mod.shape[0] == 1, "single-layer block"
    (w_mod, b_mod, g_norm1, g_norm2, w_in, b_in, w_qk_conv, b_qk_conv, w_h_conv, b_h_conv, hf_w1, hf_b1, hf_w2,
     hf_b2, hf_w3, hf_freq, h_bias, w_a, w_b, w_out, w_group, b_group, w_router, b_router, w1_e, w3_e, w2_e) = (
        t[0] for t in (w_mod, b_mod, g_norm1, g_norm2, w_in, b_in, w_qk_conv, b_qk_conv, w_h_conv, b_h_conv,
                       hf_w1, hf_b1, hf_w2, hf_b2, hf_w3, hf_freq, h_bias, w_a, w_b, w_out, w_group, b_group,
                       w_router, b_router, w1_e, w3_e, w2_e))
    bsz, L, d = x.shape
    lc = ctx.shape[1]
    seg = L // (L // GRID_W)
    assert bsz + 1 <= 8 and lc == MLSTM_CHUNK and L % MLSTM_CHUNK == 0

    cond = jnp.zeros((8, d), F32).at[:bsz].set(c).at[bsz].set(c_ctx)
    mod = _adaln(cond, w_mod, b_mod).reshape(8, 6, d)
    modx = mod[:bsz]
    shift1, scale1, gate1, shift2, scale2, gate2 = (modx[:, i:i + 1] for i in range(6))
    shift1c = jnp.broadcast_to(mod[bsz, 0].reshape(1, 1, d), (bsz, 1, d))
    scale1c = jnp.broadcast_to(mod[bsz, 1].reshape(1, 1, d), (bsz, 1, d))

    w_in16 = w_in.astype(BF16)
    k_scale = jnp.full((M_WIDTH,), M_HEAD_DIM ** -0.5, F32)
    qk_scale = jnp.concatenate([jnp.ones((M_WIDTH,), F32), k_scale])
    w_gates, b_gates = w_in[:, IG0:M_COLS], b_in[IG0:M_COLS]

    hc = _norm_mod(ctx, g_norm1, shift1c, scale1c, lc)
    kc = _proj_conv_silu(hc, w_in16[:, K0:V0], b_in[K0:V0], w_qk_conv[:, M_WIDTH:], b_qk_conv[M_WIDTH:],
                         k_scale, lc, lc)
    vc = _proj_act(hc, w_in16[:, V0:O0], b_in[V0:O0], "none", BF16, lc)
    bcc, acc, arc = _gates(hc, w_gates, b_gates, MLSTM_CHUNK)
    zero_state = (jnp.zeros((bsz, 2, M_HEADS, M_HEAD_DIM, M_HEAD_DIM), F32),
                  jnp.zeros((bsz, 2, M_HEADS, 1, M_HEAD_DIM), F32),
                  jnp.zeros((bsz, 2, M_HEADS, 1, LANES), F32))
    _, ctx_state = _mlstm(None, (kc, 0), (vc, 0), bcc, acc, arc, zero_state, False)

    tm = 1024
    h = _norm_mod(x, g_norm1, shift1, scale1, tm)
    w_main = jnp.concatenate([w_in16[:, Q0:IG0], w_in16[:, GA0:IN_COLS]], axis=1)
    b_main = jnp.concatenate([b_in[Q0:IG0], b_in[GA0:IN_COLS]])
    pm = _proj_main(h, w_main, b_main, w_qk_conv, b_qk_conv, qk_scale, seg, tm)
    bc, ac, ar = _gates(h, w_gates, b_gates, MLSTM_CHUNK)
    hdirs, _ = _mlstm((pm, PM_Q), (pm, PM_K), (pm, PM_V), bc, ac, ar, ctx_state, True)

    x0, s = _proj_hyena(h, w_in16[:, HY0:GA0], b_in[HY0:GA0], w_h_conv, b_h_conv, seg, tm)
    y, sumsq = _hyena_long_conv(s, hf_w1, hf_b1, hf_w2, hf_b2, hf_w3, hf_freq)
    yscale = lax.rsqrt(sumsq + EPS) * (1.0 / (2 * L))

    x1, h2 = _merge(hdirs, pm, x0, s, y, x, yscale, h_bias, gate1, g_norm2, shift2, scale2,
                    w_a.astype(BF16), w_b.astype(BF16), w_out.astype(BF16))
    return _moe(h2, x1, gate2, g_final, w_group, b_group, w_router, b_router, w1_e, w3_e, w2_e)
```

```python
import functools
import math

import jax
import jax.numpy as jnp
import numpy as np
from jax import lax
from jax.experimental import pallas as pl
from jax.experimental.pallas import tpu as pltpu

F32 = jnp.float32
BF16 = jnp.bfloat16

D_MODEL = 1024
GRID_W = 64
EPS = 1e-6
M_HEADS = 4
M_HEAD_DIM = 256
M_WIDTH = M_HEADS * M_HEAD_DIM
H_WIDTH = 1024
H_POS_BANDS = 16
H_FILTER_HIDDEN = 64
H_FAST_DECAY_PCT = 0.3
H_SLOW_DECAY_PCT = 1.5
H_DECAY_TARGET = 1e-2
N_GROUPS = 8
EXPERTS_PER_GROUP = 8
N_EXPERTS = N_GROUPS * EXPERTS_PER_GROUP
D_EXPERT = 512
Q0 = 0
K0 = Q0 + M_WIDTH
V0 = K0 + M_WIDTH
O0 = V0 + M_WIDTH
IG0 = O0 + M_WIDTH
FG0 = IG0 + 2 * M_HEADS
M_COLS = FG0 + 2 * M_HEADS
HY0 = M_COLS
GA0 = HY0 + 3 * H_WIDTH
GB0 = GA0 + D_MODEL
IN_COLS = GB0 + D_MODEL

LANES = 128
MLSTM_CHUNK = 512
NEG_BIG = -1e30
VMEM_LIMIT = 48 * 1024 * 1024


def _cparams(*sem):
    return pltpu.CompilerParams(dimension_semantics=sem, vmem_limit_bytes=VMEM_LIMIT)


def _adaln_kernel(c_ref, w_ref, b_ref, o_ref):
    s = c_ref[...]
    s = s * jax.nn.sigmoid(s)
    o_ref[...] = jnp.dot(s.astype(BF16), w_ref[...].astype(BF16), preferred_element_type=F32) + b_ref[...]


def _adaln(cond, w_mod, b_mod):
    n = w_mod.shape[1]
    tn = 1536
    return pl.pallas_call(
        _adaln_kernel,
        grid=(n // tn,),
        in_specs=[pl.BlockSpec((8, D_MODEL), lambda j: (0, 0)),
                  pl.BlockSpec((D_MODEL, tn), lambda j: (0, j)),
                  pl.BlockSpec((1, tn), lambda j: (0, j))],
        out_specs=pl.BlockSpec((8, tn), lambda j: (0, j)),
        out_shape=jax.ShapeDtypeStruct((8, n), F32),
        compiler_params=_cparams("arbitrary"),
        name="adaln",
    )(cond, w_mod, b_mod.reshape(1, n))


def _norm_mod_kernel(x_ref, g_ref, sh_ref, sc_ref, o_ref):
    x = x_ref[...]
    y = x * lax.rsqrt(jnp.mean(x * x, axis=-1, keepdims=True) + EPS)
    y = y * g_ref[...]
    o_ref[...] = (y * (1.0 + sc_ref[...]) + sh_ref[...]).astype(o_ref.dtype)


def _norm_mod(x, g, shift, scale, tm):
    bsz, L, d = x.shape
    return pl.pallas_call(
        _norm_mod_kernel,
        grid=(bsz, L // tm),
        in_specs=[pl.BlockSpec((None, tm, d), lambda b, i: (b, i, 0)),
                  pl.BlockSpec((1, d), lambda b, i: (0, 0)),
                  pl.BlockSpec((None, 1, d), lambda b, i: (b, 0, 0)),
                  pl.BlockSpec((None, 1, d), lambda b, i: (b, 0, 0))],
        out_specs=pl.BlockSpec((None, tm, d), lambda b, i: (b, i, 0)),
        out_shape=jax.ShapeDtypeStruct((bsz, L, d), BF16),
        compiler_params=_cparams("parallel", "parallel"),
        name="norm_mod",
    )(x, g.reshape(1, d), shift, scale)


def _conv3(z, wc, bc, seg):
    tm = z.shape[0]
    pos = lax.broadcasted_iota(jnp.int32, z.shape, 0) & (seg - 1)
    zp = jnp.where(pos == 0, 0.0, pltpu.roll(z, 1, 0))
    zn = jnp.where(pos == seg - 1, 0.0, pltpu.roll(z, tm - 1, 0))
    return zp * wc[0:1, :] + z * wc[1:2, :] + zn * wc[2:3, :] + bc


def _proj_act_kernel(h_ref, w_ref, b_ref, o_ref, *, act):
    z = jnp.dot(h_ref[...], w_ref[...], preferred_element_type=F32) + b_ref[...]
    if act == "sigmoid":
        z = jax.nn.sigmoid(z)
    o_ref[...] = z.astype(o_ref.dtype)


def _proj_act(h, w, b, act, out_dtype, tm, tn=512):
    bsz, L, d = h.shape
    n = w.shape[1]
    return pl.pallas_call(
        functools.partial(_proj_act_kernel, act=act),
        grid=(bsz, L // tm, n // tn),
        in_specs=[pl.BlockSpec((None, tm, d), lambda b_, i, j: (b_, i, 0)),
                  pl.BlockSpec((d, tn), lambda b_, i, j: (0, j)),
                  pl.BlockSpec((1, tn), lambda b_, i, j: (0, j))],
        out_specs=pl.BlockSpec((None, tm, tn), lambda b_, i, j: (b_, i, j)),
        out_shape=jax.ShapeDtypeStruct((bsz, L, n), out_dtype),
        compiler_params=_cparams("parallel", "parallel", "arbitrary"),
        name="proj_" + act,
    )(h, w, b.reshape(1, n))


def _proj_conv_silu_kernel(h_ref, w_ref, b_ref, wc_ref, bc_ref, cs_ref, o_ref, *, seg):
    z = jnp.dot(h_ref[...], w_ref[...], preferred_element_type=F32) + b_ref[...]
    y = _conv3(z, wc_ref[...], bc_ref[...], seg)
    y = y * jax.nn.sigmoid(y)
    o_ref[...] = (y * cs_ref[...]).astype(o_ref.dtype)


def _proj_conv_silu(h, w, b, wc, bc, colscale, seg, tm, tn=512):
    bsz, L, d = h.shape
    n = w.shape[1]
    col = lambda b_, i, j: (0, j)
    return pl.pallas_call(
        functools.partial(_proj_conv_silu_kernel, seg=seg),
        grid=(bsz, L // tm, n // tn),
        in_specs=[pl.BlockSpec((None, tm, d), lambda b_, i, j: (b_, i, 0)),
                  pl.BlockSpec((d, tn), col),
                  pl.BlockSpec((1, tn), col),
                  pl.BlockSpec((3, tn), col),
                  pl.BlockSpec((1, tn), col),
                  pl.BlockSpec((1, tn), col)],
        out_specs=pl.BlockSpec((None, tm, tn), lambda b_, i, j: (b_, i, j)),
        out_shape=jax.ShapeDtypeStruct((bsz, L, n), BF16),
        compiler_params=_cparams("parallel", "parallel", "arbitrary"),
        name="proj_conv_silu",
    )(h, w, b.reshape(1, n), wc, bc.reshape(1, n), colscale.reshape(1, n))


PROJ_TN = 1024
PROJ_SUB = 512
PM_Q, PM_K, PM_V, PM_O, PM_GA, PM_GB = range(6)


def _proj_main_kernel(h_ref, w_ref, b_ref, wc_ref, bc_ref, cs_ref, o_ref, *, seg):
    j = pl.program_id(2)

    def run(epilogue):
        for c in range(PROJ_TN // PROJ_SUB):
            sl = slice(c * PROJ_SUB, (c + 1) * PROJ_SUB)
            z = jnp.dot(h_ref[...], w_ref[:, sl], preferred_element_type=F32) + b_ref[:, sl]
            o_ref[:, sl] = epilogue(z, sl).astype(o_ref.dtype)

    def conv_silu(z, sl):
        y = _conv3(z, wc_ref[:, sl], bc_ref[:, sl], seg)
        return (y * jax.nn.sigmoid(y)) * cs_ref[:, sl]

    @pl.when(j <= PM_K)
    def _():
        run(conv_silu)

    @pl.when(j == PM_V)
    def _():
        run(lambda z, sl: z)

    @pl.when(j >= PM_O)
    def _():
        run(lambda z, sl: jax.nn.sigmoid(z))


def _proj_main(h, w, b, wc, bc, colscale, seg, tm):
    bsz, L, d = h.shape
    n = w.shape[1]
    qk = lambda b_, i, j: (0, jnp.minimum(j, PM_K))
    return pl.pallas_call(
        functools.partial(_proj_main_kernel, seg=seg),
        grid=(bsz, L // tm, n // PROJ_TN),
        in_specs=[pl.BlockSpec((None, tm, d), lambda b_, i, j: (b_, i, 0)),
                  pl.BlockSpec((d, PROJ_TN), lambda b_, i, j: (0, j)),
                  pl.BlockSpec((1, PROJ_TN), lambda b_, i, j: (0, j)),
                  pl.BlockSpec((3, PROJ_TN), qk),
                  pl.BlockSpec((1, PROJ_TN), qk),
                  pl.BlockSpec((1, PROJ_TN), qk)],
        out_specs=pl.BlockSpec((None, tm, PROJ_TN), lambda b_, i, j: (b_, i, j)),
        out_shape=jax.ShapeDtypeStruct((bsz, L, n), BF16),
        compiler_params=_cparams("parallel", "parallel", "arbitrary"),
        name="proj_main",
    )(h, w, b.reshape(1, n), wc, bc.reshape(1, -1), colscale.reshape(1, -1))


def _proj_hyena_kernel(h_ref, w0_ref, w1_ref, w2_ref, b_ref, wc_ref, bc_ref, x0_ref, s_ref, *, seg):
    h = h_ref[...]
    us = []
    for g, w_ref in enumerate((w0_ref, w1_ref, w2_ref)):
        z = jnp.dot(h, w_ref[...], preferred_element_type=F32) + b_ref[g]
        us.append(_conv3(z, wc_ref[g], bc_ref[g], seg))
    x0_ref[...] = us[0].astype(x0_ref.dtype)
    s_ref[...] = us[1] * us[2]


def _proj_hyena(h, w, b, wc, bc, seg, tm, tn=512):
    bsz, L, d = h.shape
    nblk = H_WIDTH // tn
    b3 = b.reshape(3, 1, H_WIDTH)
    wc3 = wc.reshape(3, 3, H_WIDTH).transpose(1, 0, 2)
    bc3 = bc.reshape(3, 1, H_WIDTH)
    out_spec = pl.BlockSpec((None, tm, tn), lambda b_, i, j: (b_, i, j))
    return pl.pallas_call(
        functools.partial(_proj_hyena_kernel, seg=seg),
        grid=(bsz, L // tm, nblk),
        in_specs=[pl.BlockSpec((None, tm, d), lambda b_, i, j: (b_, i, 0)),
                  pl.BlockSpec((d, tn), lambda b_, i, j: (0, j)),
                  pl.BlockSpec((d, tn), lambda b_, i, j: (0, nblk + j)),
                  pl.BlockSpec((d, tn), lambda b_, i, j: (0, 2 * nblk + j)),
                  pl.BlockSpec((3, 1, tn), lambda b_, i, j: (0, 0, j)),
                  pl.BlockSpec((3, 3, tn), lambda b_, i, j: (0, 0, j)),
                  pl.BlockSpec((3, 1, tn), lambda b_, i, j: (0, 0, j))],
        out_specs=[out_spec, out_spec],
        out_shape=[jax.ShapeDtypeStruct((bsz, L, H_WIDTH), BF16),
                   jax.ShapeDtypeStruct((bsz, L, H_WIDTH), F32)],
        compiler_params=_cparams("parallel", "parallel", "arbitrary"),
        name="proj_hyena",
    )(h, w, w, w, b3, wc3, bc3)


N_GATES = 4 * M_HEADS


def _split3(x):
    hi = x.astype(BF16)
    r1 = x - hi.astype(F32)
    mid = r1.astype(BF16)
    lo = (r1 - mid.astype(F32)).astype(BF16)
    return hi, mid, lo


def _log_sigmoid(x):
    return jnp.minimum(x, 0.0) - jnp.log1p(jnp.exp(-jnp.abs(x)))


def _gates_kernel(h_ref, w_ref, wt_ref, b_ref, bt_ref, bc_ref, ac_ref, ar_ref):
    h = h_ref[...]
    t = h.shape[0]
    z = jnp.dot(h, w_ref[...], preferred_element_type=F32) + b_ref[...]
    zt = lax.dot_general(wt_ref[...], h, (((1,), (1,)), ((), ())),
                         preferred_element_type=F32) + bt_ref[...]
    r = lax.broadcasted_iota(jnp.int32, (t, t), 0)
    c = lax.broadcasted_iota(jnp.int32, (t, t), 1)
    lower = (r >= c).astype(BF16)
    upper = (r <= c).astype(BF16)

    lf = _log_sigmoid(z)
    parts = _split3(lf)
    cf = sum(jnp.dot(lower, p, preferred_element_type=F32) for p in parts)
    cb = sum(jnp.dot(upper, p, preferred_element_type=F32) for p in parts)
    lane = lax.broadcasted_iota(jnp.int32, z.shape, 1)
    bc = jnp.where(lane < FG_LANE0 + M_HEADS, cf, cb)
    bc = pltpu.roll(bc, LANES - FG_LANE0, 1)
    bc_ref[...] = bc
    ac_ref[...] = z - bc

    lft = _log_sigmoid(zt[FG_LANE0:, :])
    tparts = _split3(lft)
    cft = sum(jnp.dot(p, upper, preferred_element_type=F32) for p in tparts)
    cbt = sum(jnp.dot(p, lower, preferred_element_type=F32) for p in tparts)
    row = lax.broadcasted_iota(jnp.int32, cft.shape, 0)
    ar_ref[...] = zt[:FG_LANE0, :] - jnp.where(row < M_HEADS, cft, cbt)


FG_LANE0 = 2 * M_HEADS


def _gates(h, w_g, b_g, chunk):
    bsz, L, d = h.shape
    w_pad = jnp.zeros((d, LANES), F32).at[:, :N_GATES].set(w_g).astype(BF16)
    b_pad = jnp.zeros((1, LANES), F32).at[0, :N_GATES].set(b_g)
    wt = w_g.T.astype(BF16)
    bt = b_g.reshape(N_GATES, 1)
    tok = pl.BlockSpec((None, chunk, LANES), lambda b_, i: (b_, i, 0))
    return pl.pallas_call(
        _gates_kernel,
        grid=(bsz, L // chunk),
        in_specs=[pl.BlockSpec((None, chunk, d), lambda b_, i: (b_, i, 0)),
                  pl.BlockSpec((d, LANES), lambda b_, i: (0, 0)),
                  pl.BlockSpec((N_GATES, d), lambda b_, i: (0, 0)),
                  pl.BlockSpec((1, LANES), lambda b_, i: (0, 0)),
                  pl.BlockSpec((N_GATES, 1), lambda b_, i: (0, 0))],
        out_specs=[tok, tok, pl.BlockSpec((None, FG_LANE0, chunk), lambda b_, i: (b_, 0, i))],
        out_shape=[jax.ShapeDtypeStruct((bsz, L, LANES), F32),
                   jax.ShapeDtypeStruct((bsz, L, LANES), F32),
                   jax.ShapeDtypeStruct((bsz, FG_LANE0, L), F32)],
        compiler_params=_cparams("parallel", "parallel"),
        name="mlstm_gates",
    )(h, w_pad, wt, b_pad, bt)


def _mlstm_kernel(*refs, emit_h, n_chunks):
    if emit_h:
        (q_ref, k_ref, v_ref, bc_ref, ac_ref, ar_ref, c0_ref, n0_ref, m0_ref,
         h_ref, cf_ref, nf_ref, mf_ref, c_sc, n_sc, m_sc) = refs
    else:
        (k_ref, v_ref, bc_ref, ac_ref, ar_ref, c0_ref, n0_ref, m0_ref,
         cf_ref, nf_ref, mf_ref, c_sc, n_sc, m_sc) = refs
    d = pl.program_id(1)
    j = pl.program_id(2)
    fwd = d == 0
    t = k_ref.shape[0]
    dh = M_HEAD_DIM

    @pl.when(j == 0)
    def _():
        c_sc[...] = c0_ref[...]
        n_sc[...] = n0_ref[...]
        m_sc[...] = m0_ref[...]

    r = lax.broadcasted_iota(jnp.int32, (t, t), 0)
    c = lax.broadcasted_iota(jnp.int32, (t, t), 1)
    causal = jnp.where(fwd, r - c, c - r) >= 0
    bc_all = bc_ref[...]
    ac_all = ac_ref[...]
    ar_all = ar_ref[...]
    for hd in range(M_HEADS):
        sl = slice(hd * dh, (hd + 1) * dh)
        bc = jnp.where(fwd, bc_all[:, hd:hd + 1], bc_all[:, M_HEADS + hd:M_HEADS + hd + 1])
        ac = jnp.where(fwd, ac_all[:, hd:hd + 1], ac_all[:, M_HEADS + hd:M_HEADS + hd + 1])
        ar = jnp.where(fwd, ar_all[hd:hd + 1, :], ar_all[M_HEADS + hd:M_HEADS + hd + 1, :])
        b_tot = jnp.where(fwd, bc[t - 1:t, :], bc[0:1, :])
        m_prev = m_sc[hd][:, 0:1]
        k_h = k_ref[:, sl]
        v_h = v_ref[:, sl]
        if emit_h:
            q_h = q_ref[:, sl]
            dm = jnp.where(causal, bc + ar, NEG_BIG)
            inter = bc + m_prev
            m_t = jnp.maximum(inter, jnp.max(dm, axis=1, keepdims=True))
            qk = lax.dot_general(q_h, k_h, (((1,), (1,)), ((), ())), preferred_element_type=F32)
            s = qk * jnp.exp(dm - m_t)
            carry = jnp.exp(inter - m_t)
            num = (jnp.dot(s.astype(BF16), v_h, preferred_element_type=F32)
                   + carry * jnp.dot(q_h, c_sc[hd].astype(BF16), preferred_element_type=F32))
            den = (jnp.sum(s, axis=1, keepdims=True)
                   + carry * jnp.sum(q_h.astype(F32) * n_sc[hd], axis=1, keepdims=True))
            h_ref[:, sl] = (num / jnp.maximum(jnp.abs(den), jnp.exp(-m_t))).astype(h_ref.dtype)
        g = b_tot + ac
        m_new = jnp.maximum(b_tot + m_prev, jnp.max(g, axis=0, keepdims=True))
        wgt = jnp.exp(g - m_new)
        decay = jnp.exp(b_tot + m_prev - m_new)
        kw = k_h.astype(F32) * wgt
        c_sc[hd] = decay * c_sc[hd] + lax.dot_general(kw.astype(BF16), v_h, (((0,), (0,)), ((), ())),
                                                      preferred_element_type=F32)
        n_sc[hd] = decay * n_sc[hd] + jnp.sum(kw, axis=0, keepdims=True)
        m_sc[hd] = jnp.broadcast_to(m_new, (1, LANES))

    @pl.when(j == n_chunks - 1)
    def _():
        cf_ref[...] = c_sc[...]
        nf_ref[...] = n_sc[...]
        mf_ref[...] = m_sc[...]


def _mlstm(q, k, v, bc, ac, ar, state, emit_h, t):
    bsz, L, _ = k[0].shape
    nc = L // t
    seq = lambda b_, d, j: (b_, j + d * (nc - 1 - 2 * j), 0)
    st = lambda b_, d, j: (b_, d, 0, 0, 0)

    def tok(col):
        return pl.BlockSpec((None, t, M_WIDTH), lambda b_, d, j: (b_, j + d * (nc - 1 - 2 * j), col))

    gate_spec = pl.BlockSpec((None, t, LANES), seq)
    ar_spec = pl.BlockSpec((None, FG_LANE0, t), lambda b_, d, j: (b_, 0, j + d * (nc - 1 - 2 * j)))
    c_spec = pl.BlockSpec((None, None, M_HEADS, M_HEAD_DIM, M_HEAD_DIM), st)
    n_spec = pl.BlockSpec((None, None, M_HEADS, 1, M_HEAD_DIM), st)
    m_spec = pl.BlockSpec((None, None, M_HEADS, 1, LANES), st)
    state_shapes = [jax.ShapeDtypeStruct((bsz, 2, M_HEADS, M_HEAD_DIM, M_HEAD_DIM), F32),
                    jax.ShapeDtypeStruct((bsz, 2, M_HEADS, 1, M_HEAD_DIM), F32),
                    jax.ShapeDtypeStruct((bsz, 2, M_HEADS, 1, LANES), F32)]
    in_specs = [tok(k[1]), tok(v[1]), gate_spec, gate_spec, ar_spec, c_spec, n_spec, m_spec]
    args = [k[0], v[0], bc, ac, ar, *state]
    out_specs = [c_spec, n_spec, m_spec]
    out_shape = list(state_shapes)
    if emit_h:
        in_specs = [tok(q[1])] + in_specs
        args = [q[0]] + args
        out_specs = [pl.BlockSpec((None, None, t, M_WIDTH),
                                  lambda b_, d, j: (d, b_, j + d * (nc - 1 - 2 * j), 0))] + out_specs
        out_shape = [jax.ShapeDtypeStruct((2, bsz, L, M_WIDTH), BF16)] + out_shape
    outs = pl.pallas_call(
        functools.partial(_mlstm_kernel, emit_h=emit_h, n_chunks=nc),
        grid=(bsz, 2, nc),
        in_specs=in_specs,
        out_specs=out_specs,
        out_shape=out_shape,
        scratch_shapes=[pltpu.VMEM((M_HEADS, M_HEAD_DIM, M_HEAD_DIM), F32),
                        pltpu.VMEM((M_HEADS, 1, M_HEAD_DIM), F32),
                        pltpu.VMEM((M_HEADS, 1, LANES), F32)],
        compiler_params=_cparams("parallel", "parallel", "arbitrary"),
        name="mlstm" if emit_h else "mlstm_state",
    )(*args)
    if emit_h:
        return outs[0], tuple(outs[1:])
    return None, tuple(outs)


DFT_M_TILE = 8
DFT_C_TILE = 512
FEAT_ROWS = 16


def _filter_outer_kernel(bands_ref, w1t_ref, b1_ref, w2t_ref, b2_ref, w3p_ref, w3f_ref, fr_ref, dl_ref, l_ref,
                         a_ref, ss_ref, *, L, n1, n2):
    i = pl.program_id(0)
    h = n1 // 2
    cols = DFT_M_TILE * h

    def positions(shape, axis, side):
        q = lax.broadcasted_iota(jnp.int32, shape, axis)
        mm, jj = q // h, q % h
        n = n2 * (jj + side * h) + i * DFT_M_TILE + mm
        return n, jnp.where(n < L, n, 2 * L - n).astype(F32)

    taps = []
    sumsq = jnp.zeros((1, a_ref.shape[-1]), F32)
    for side, w3_ref in ((0, w3p_ref), (1, w3f_ref)):
        _, p_row = positions((1, cols), 1, side)
        t_row = p_row / float(max(L - 1, 1))
        ang = ((2 * math.pi / L) * p_row) * bands_ref[...]
        row = lax.broadcasted_iota(jnp.int32, (FEAT_ROWS, cols), 0)
        feats = jnp.concatenate([jnp.where(row == 0, t_row, 0.0), jnp.cos(ang), -jnp.sin(ang)], axis=0)
        fr = fr_ref[...]
        hid = jnp.sin(fr * (jnp.dot(w1t_ref[...], feats.astype(BF16), preferred_element_type=F32) + b1_ref[...]))
        hid = jnp.sin(fr * (jnp.dot(w2t_ref[...], hid.astype(BF16), preferred_element_type=F32) + b2_ref[...]))
        filt = lax.dot_general(hid.astype(BF16), w3_ref[...], (((0,), (0,)), ((), ())),
                               preferred_element_type=F32)
        n_col, p_col = positions((cols, 1), 0, side)
        t_col = p_col / float(max(L - 1, 1))
        kern = filt * jnp.exp(-t_col * jnp.abs(dl_ref[...]))
        kern = jnp.where(n_col == L, 0.0, kern)
        sumsq = sumsq + jnp.sum(kern * kern, axis=0, keepdims=True)
        taps.append(kern)

    for mm in range(DFT_M_TILE):
        x = jnp.concatenate([taps[0][mm * h:(mm + 1) * h], taps[1][mm * h:(mm + 1) * h]], axis=0)
        out = jnp.dot(l_ref[...], x.astype(BF16), preferred_element_type=F32)
        a_ref[0, :, mm, :] = out[:n1]
        a_ref[1, :, mm, :] = out[n1:]

    @pl.when(i == 0)
    def _():
        ss_ref[...] = jnp.zeros_like(ss_ref)

    ss_ref[...] += sumsq


def _filter_outer(L, n1, n2, fwd_r, w1, b1, w2, b2, w3, freq):
    hid = H_FILTER_HIDDEN
    bands = jnp.linspace(1e-4, H_POS_BANDS - 1, H_POS_BANDS, dtype=F32).reshape(H_POS_BANDS, 1)
    w1t = jnp.zeros((hid, 3 * FEAT_ROWS), F32)
    w1t = w1t.at[:, 0].set(w1[0]).at[:, FEAT_ROWS:2 * FEAT_ROWS].set(w1[1:1 + H_POS_BANDS].T)
    w1t = w1t.at[:, 2 * FEAT_ROWS:].set(w1[1 + H_POS_BANDS:].T).astype(BF16)
    w3h = w3.astype(BF16)
    max_decay = math.log(H_DECAY_TARGET) / H_FAST_DECAY_PCT
    min_decay = math.log(H_DECAY_TARGET) / H_SLOW_DECAY_PCT
    deltas = jnp.linspace(min_decay, max_decay, H_WIDTH, dtype=F32).reshape(1, H_WIDTH)
    col = lambda v: v.reshape(hid, 1)
    full = lambda a: pl.BlockSpec(a.shape, lambda i: (0,) * a.ndim)
    args = [bands, w1t, col(b1), w2.T.astype(BF16), col(b2)]
    return pl.pallas_call(
        functools.partial(_filter_outer_kernel, L=L, n1=n1, n2=n2),
        grid=(n2 // DFT_M_TILE,),
        in_specs=[full(a) for a in args]
        + [pl.BlockSpec((hid, H_WIDTH), lambda i: (0, 0)), pl.BlockSpec((hid, H_WIDTH), lambda i: (0, 1)),
           full(col(freq)), full(deltas), full(fwd_r)],
        out_specs=[pl.BlockSpec((2, n1, DFT_M_TILE, H_WIDTH), lambda i: (0, 0, i, 0)),
                   pl.BlockSpec((1, H_WIDTH), lambda i: (0, 0))],
        out_shape=[jax.ShapeDtypeStruct((2, n1, n2, H_WIDTH), F32),
                   jax.ShapeDtypeStruct((1, H_WIDTH), F32)],
        compiler_params=_cparams("arbitrary"),
        name="hyena_filter_outer",
    )(*args, w3h, w3h, col(freq), deltas, fwd_r)


def _dft_factors(n):
    lg = int(round(math.log2(n)))
    n1 = 1 << ((lg + 1) // 2)
    return n1, n // n1


def _dft_outer_matrices(n1):
    k = np.arange(n1)[:, None]
    n = np.arange(n1)[None, :]
    ang = 2.0 * np.pi * ((k * n) % n1) / n1
    cr, ci = np.cos(ang), -np.sin(ang)
    h = n1 // 2
    fwd_c = np.block([[cr[:, :h], -ci[:, :h]], [ci[:, :h], cr[:, :h]]])
    fwd_r = np.concatenate([cr, ci], axis=0)
    ir, ii = cr[:h, :], -ci[:h, :]
    inv = np.block([[ir, -ii], [ii, ir]])
    return (jnp.asarray(fwd_c, F32).astype(BF16), jnp.asarray(fwd_r, F32).astype(BF16),
            jnp.asarray(inv, F32).astype(BF16))


def _dft_inner_matrices(n1, n2):
    n = n1 * n2
    k2 = np.arange(n2)[:, None]
    m = np.arange(n2)[None, :]
    ang = 2.0 * np.pi * ((k2 * m) % n2) / n2
    fr, fi = np.cos(ang), -np.sin(ang)
    f = np.block([[fr, -fi], [fi, fr]])
    k1 = jnp.arange(n1, dtype=jnp.int32)[:, None]
    tw_ang = ((jnp.arange(n2, dtype=jnp.int32)[None, :] * k1) % n).astype(F32) * (2.0 * math.pi / n)
    rep = lambda t: jnp.broadcast_to(t[:, :, None], (n1, n2, LANES))
    return (jnp.asarray(f, F32).astype(BF16), jnp.asarray(f.T, F32).astype(BF16),
            rep(jnp.cos(tw_ang)), rep(-jnp.sin(tw_ang)))


def _outer_dft_kernel(l_ref, x_ref, o_ref):
    p_in, p_out = x_ref.shape[0], o_ref.shape[0]
    r_out = o_ref.shape[1]
    for mm in range(x_ref.shape[2]):
        parts = [x_ref[p, :, mm, :] for p in range(p_in)]
        x = parts[0] if p_in == 1 else jnp.concatenate(parts, axis=0)
        out = jnp.dot(l_ref[...], x.astype(BF16), preferred_element_type=F32)
        for p in range(p_out):
            o_ref[p, :, mm, :] = out[p * r_out:(p + 1) * r_out]


def _outer_dft(lmat, x4, p_out):
    p_in, r_in, n2, c = x4.shape
    r_out = lmat.shape[0] // p_out
    tc = min(DFT_C_TILE, c)
    return pl.pallas_call(
        _outer_dft_kernel,
        grid=(n2 // DFT_M_TILE, c // tc),
        in_specs=[pl.BlockSpec(lmat.shape, lambda m, j: (0, 0)),
                  pl.BlockSpec((p_in, r_in, DFT_M_TILE, tc), lambda m, j: (0, 0, m, j))],
        out_specs=pl.BlockSpec((p_out, r_out, DFT_M_TILE, tc), lambda m, j: (0, 0, m, j)),
        out_shape=jax.ShapeDtypeStruct((p_out, r_out, n2, c), F32),
        compiler_params=_cparams("parallel", "parallel"),
        name="dft_outer",
    )(lmat, x4)


DFT_K_TILE = 2


def _twiddled_inner_dft(f_ref, twr_ref, twi_ref, a_ref, kk):
    n2, c = a_ref.shape[2], a_ref.shape[3]
    twr = jnp.tile(twr_ref[kk], (1, c // LANES))
    twi = jnp.tile(twi_ref[kk], (1, c // LANES))
    ar, ai = a_ref[0, kk], a_ref[1, kk]
    a = jnp.concatenate([(ar * twr - ai * twi).astype(BF16), (ar * twi + ai * twr).astype(BF16)], axis=0)
    x = jnp.dot(f_ref[...], a, preferred_element_type=F32)
    return x[:n2], x[n2:], twr, twi


def _inner_fwd_kernel(f_ref, twr_ref, twi_ref, a_ref, o_ref):
    for kk in range(a_ref.shape[1]):
        xr, xi, _, _ = _twiddled_inner_dft(f_ref, twr_ref, twi_ref, a_ref, kk)
        o_ref[0, kk] = xr.astype(o_ref.dtype)
        o_ref[1, kk] = xi.astype(o_ref.dtype)


def _inner_specs(n1, n2, c):
    blk = pl.BlockSpec((2, DFT_K_TILE, n2, c), lambda k: (0, k, 0, 0))
    mat = pl.BlockSpec((2 * n2, 2 * n2), lambda k: (0, 0))
    tw = pl.BlockSpec((DFT_K_TILE, n2, LANES), lambda k: (k, 0, 0))
    return blk, mat, tw


def _inner_fwd(f, twr, twi, a):
    _, n1, n2, c = a.shape
    blk, mat, tw = _inner_specs(n1, n2, c)
    return pl.pallas_call(
        _inner_fwd_kernel,
        grid=(n1 // DFT_K_TILE,),
        in_specs=[mat, tw, tw, blk],
        out_specs=blk,
        out_shape=jax.ShapeDtypeStruct((2, n1, n2, c), BF16),
        compiler_params=_cparams("parallel"),
        name="dft_inner_filter",
    )(f, twr, twi, a)


def _inner_conv_kernel(f_ref, ft_ref, twr_ref, twi_ref, a_ref, k_ref, o_ref):
    n2 = a_ref.shape[2]
    for kk in range(a_ref.shape[1]):
        xr, xi, twr, twi = _twiddled_inner_dft(f_ref, twr_ref, twi_ref, a_ref, kk)
        kr, ki = k_ref[0, kk].astype(F32), k_ref[1, kk].astype(F32)
        yr = xr * kr - xi * ki
        yi = xr * ki + xi * kr
        y = jnp.concatenate([yr.astype(BF16), yi.astype(BF16)], axis=0)
        b = jnp.dot(ft_ref[...], y, preferred_element_type=F32)
        br, bi = b[:n2], b[n2:]
        o_ref[0, kk] = br * twr + bi * twi
        o_ref[1, kk] = bi * twr - br * twi


def _inner_conv(f, ft, twr, twi, a, kf):
    _, n1, n2, c = a.shape
    blk, mat, tw = _inner_specs(n1, n2, c)
    return pl.pallas_call(
        _inner_conv_kernel,
        grid=(n1 // DFT_K_TILE,),
        in_specs=[mat, mat, tw, tw, blk, blk],
        out_specs=blk,
        out_shape=jax.ShapeDtypeStruct((2, n1, n2, c), F32),
        compiler_params=_cparams("parallel"),
        name="dft_inner_conv",
    )(f, ft, twr, twi, a, kf)


def _hyena_long_conv(s, w1, b1, w2, b2, w3, freq):
    bsz, L, c = s.shape
    assert bsz == 2
    n = 2 * L
    n1, n2 = _dft_factors(n)
    fwd_c, fwd_r, inv = _dft_outer_matrices(n1)
    f, ft, twr, twi = _dft_inner_matrices(n1, n2)
    af, sumsq = _filter_outer(L, n1, n2, fwd_r, w1, b1, w2, b2, w3, freq)
    kf = _inner_fwd(f, twr, twi, af)
    a = _outer_dft(fwd_c, s.reshape(2, n1 // 2, n2, c), 2)
    b = _inner_conv(f, ft, twr, twi, a, kf)
    y = _outer_dft(inv, b, 2)
    return y.reshape(2, L, c), sumsq


def _merge_kernel(hf_ref, hb_ref, o_ref, x0_ref, s_ref, y_ref, ga_ref, gb_ref, x_ref,
                  ysc_ref, hbias_ref, gate_ref, g2_ref, sh_ref, sc_ref,
                  wa_ref, wb_ref, wo_ref, x1_ref, h2_ref):
    a = o_ref[...].astype(F32) * (hf_ref[...].astype(F32) + hb_ref[...].astype(F32))
    s = s_ref[...]
    hy = x0_ref[...].astype(F32) * (y_ref[...] * ysc_ref[...] + hbias_ref[...] * s)
    pa = jnp.dot(a.astype(BF16), wa_ref[...], preferred_element_type=F32)
    pb = jnp.dot(hy.astype(BF16), wb_ref[...], preferred_element_type=F32)
    mix = ga_ref[...].astype(F32) * pa + gb_ref[...].astype(F32) * pb
    out = jnp.dot(mix.astype(BF16), wo_ref[...], preferred_element_type=F32)
    x1 = x_ref[...] + gate_ref[...] * out
    x1_ref[...] = x1
    y = x1 * lax.rsqrt(jnp.mean(x1 * x1, axis=-1, keepdims=True) + EPS) * g2_ref[...]
    h2_ref[...] = y * (1.0 + sc_ref[...]) + sh_ref[...]


def _merge(hdirs, pm, x0, s, y, x, yscale, h_bias, gate1, g2, shift2, scale2, w_a, w_b, w_out, tm=256):
    bsz, L, d = x.shape
    tok = pl.BlockSpec((None, tm, d), lambda b, i: (b, i, 0))

    def pm_tile(col):
        return pl.BlockSpec((None, tm, d), lambda b, i: (b, i, col))

    vec = pl.BlockSpec((1, d), lambda b, i: (0, 0))
    bvec = pl.BlockSpec((None, 1, d), lambda b, i: (b, 0, 0))
    wsp = pl.BlockSpec((d, d), lambda b, i: (0, 0))
    return pl.pallas_call(
        _merge_kernel,
        grid=(bsz, L // tm),
        in_specs=[pl.BlockSpec((None, None, tm, d), lambda b, i: (0, b, i, 0)),
                  pl.BlockSpec((None, None, tm, d), lambda b, i: (1, b, i, 0)),
                  pm_tile(PM_O), tok, tok, tok, pm_tile(PM_GA), pm_tile(PM_GB), tok,
                  vec, vec, bvec, vec, bvec, bvec, wsp, wsp, wsp],
        out_specs=[tok, tok],
        out_shape=[jax.ShapeDtypeStruct((bsz, L, d), F32), jax.ShapeDtypeStruct((bsz, L, d), F32)],
        compiler_params=_cparams("parallel", "parallel"),
        name="merge",
    )(hdirs, hdirs, pm, x0, s, y, pm, pm, x, yscale, h_bias.reshape(1, d), gate1, g2.reshape(1, d),
      shift2, scale2, w_a, w_b, w_out)


MOE_BLOCK = 256
ROUTE_E1, ROUTE_E2, ROUTE_W1, ROUTE_W2 = 0, 1, 2, 3
EXP_LANE0 = N_GROUPS


def _first_lane_of_max(val, valid, lane):
    masked = jnp.where(valid, val, NEG_BIG)
    mx = jnp.max(masked, axis=1, keepdims=True)
    idx = jnp.min(jnp.where(valid & (masked == mx), lane, LANES), axis=1, keepdims=True)
    return mx, idx


def _router_kernel(h_ref, w_ref, b_ref, r_ref):
    logits = jnp.dot(h_ref[...].astype(BF16), w_ref[...], preferred_element_type=F32) + b_ref[...]
    lane = lax.broadcasted_iota(jnp.int32, logits.shape, 1)
    is_g = lane < N_GROUPS
    gmax, gsel = _first_lane_of_max(logits, is_g, lane)
    gsum = jnp.sum(jnp.where(is_g, jnp.exp(logits - gmax), 0.0), axis=1, keepdims=True)
    gw = 1.0 / gsum
    lo = EXP_LANE0 + gsel * EXPERTS_PER_GROUP
    in_grp = (lane >= lo) & (lane < lo + EXPERTS_PER_GROUP)
    emax, l1 = _first_lane_of_max(logits, in_grp, lane)
    esum = jnp.sum(jnp.where(in_grp, jnp.exp(logits - emax), 0.0), axis=1, keepdims=True)
    e2max, l2 = _first_lane_of_max(logits, in_grp & (lane != l1), lane)
    v1 = 1.0 / esum
    v2 = jnp.exp(e2max - emax) / esum
    vs = v1 + v2
    w1 = gw * v1 / vs
    w2 = gw * v2 / vs
    e1 = (l1 - EXP_LANE0).astype(F32)
    e2 = (l2 - EXP_LANE0).astype(F32)
    r_ref[...] = jnp.where(lane == ROUTE_E1, e1,
                           jnp.where(lane == ROUTE_E2, e2,
                                     jnp.where(lane == ROUTE_W1, w1,
                                               jnp.where(lane == ROUTE_W2, w2, 0.0))))


def _router(h2, w_group, b_group, w_router, b_router, tm=1024):
    n, d = h2.shape
    w = jnp.zeros((d, LANES), F32).at[:, :N_GROUPS].set(w_group).at[
        :, EXP_LANE0:EXP_LANE0 + N_EXPERTS].set(w_router).astype(BF16)
    b = jnp.zeros((1, LANES), F32).at[0, :N_GROUPS].set(b_group).at[
        0, EXP_LANE0:EXP_LANE0 + N_EXPERTS].set(b_router)
    return pl.pallas_call(
        _router_kernel,
        grid=(n // tm,),
        in_specs=[pl.BlockSpec((tm, d), lambda i: (i, 0)),
                  pl.BlockSpec((d, LANES), lambda i: (0, 0)),
                  pl.BlockSpec((1, LANES), lambda i: (0, 0))],
        out_specs=pl.BlockSpec((tm, LANES), lambda i: (i, 0)),
        out_shape=jax.ShapeDtypeStruct((n, LANES), F32),
        compiler_params=_cparams("parallel"),
        name="moe_router",
    )(h2, w, b)


def _slots_kernel(r_ref, dest_ref, cnt_ref, run_sc, start_sc):
    ph = pl.program_id(0)
    i = pl.program_id(1)
    rec = r_ref[...]
    tm = rec.shape[0]
    lane = lax.broadcasted_iota(jnp.int32, rec.shape, 1)
    e1 = rec[:, ROUTE_E1:ROUTE_E1 + 1].astype(jnp.int32)
    e2 = rec[:, ROUTE_E2:ROUTE_E2 + 1].astype(jnp.int32)
    oh1 = lane == e1
    oh2 = lane == e2
    oh = (oh1 | oh2).astype(F32)

    @pl.when((ph == 0) & (i == 0))
    def _():
        run_sc[...] = jnp.zeros_like(run_sc)

    @pl.when(ph == 0)
    def _():
        run_sc[...] += jnp.sum(oh, axis=0, keepdims=True)

    @pl.when((ph == 1) & (i == 0))
    def _():
        counts = run_sc[...]
        cnt_ref[...] = counts
        nblk = jnp.floor((counts + (MOE_BLOCK - 1)) * (1.0 / MOE_BLOCK))
        rr = lax.broadcasted_iota(jnp.int32, (LANES, LANES), 0)
        cc = lax.broadcasted_iota(jnp.int32, (LANES, LANES), 1)
        before = (rr < cc).astype(BF16)
        first = jnp.dot(nblk.astype(BF16), before, preferred_element_type=F32)
        start_sc[...] = first * float(MOE_BLOCK)
        run_sc[...] = jnp.zeros_like(run_sc)

    @pl.when(ph == 1)
    def _():
        r = lax.broadcasted_iota(jnp.int32, (tm, tm), 0)
        c = lax.broadcasted_iota(jnp.int32, (tm, tm), 1)
        earlier = (r > c).astype(BF16)
        rank = jnp.dot(earlier, oh.astype(BF16), preferred_element_type=F32) + run_sc[...] + start_sc[...]
        d1 = jnp.sum(jnp.where(oh1, rank, 0.0), axis=1, keepdims=True)
        d2 = jnp.sum(jnp.where(oh2, rank, 0.0), axis=1, keepdims=True)
        dest_ref[...] = jnp.where(lane == 0, d1, jnp.where(lane == 1, d2, 0.0)).astype(jnp.int32)
        run_sc[...] += jnp.sum(oh, axis=0, keepdims=True)


def _slots(route, tm=512):
    n = route.shape[0]
    return pl.pallas_call(
        _slots_kernel,
        grid=(2, n // tm),
        in_specs=[pl.BlockSpec((tm, LANES), lambda p, i: (i, 0))],
        out_specs=[pl.BlockSpec((tm, LANES), lambda p, i: (i * p, 0)),
                   pl.BlockSpec((1, LANES), lambda p, i: (0, 0))],
        out_shape=[jax.ShapeDtypeStruct((n, LANES), jnp.int32), jax.ShapeDtypeStruct((1, LANES), F32)],
        scratch_shapes=[pltpu.VMEM((1, LANES), F32), pltpu.VMEM((1, LANES), F32)],
        compiler_params=_cparams("arbitrary", "arbitrary"),
        name="moe_slots",
    )(route)


DMA_UNROLL = 8


def _row_copy(src_ref, dst_ref, sem, src_row, dst_row):
    return pltpu.make_async_copy(src_ref.at[pl.ds(src_row, 1)], dst_ref.at[pl.ds(dst_row, 1)], sem)


def _dispatch_kernel(dest_ref, h_ref, xs_in_ref, xs_ref, sem):
    del xs_in_ref
    tm = h_ref.shape[0]

    def start(r, carry):
        _row_copy(h_ref, xs_ref, sem, r, dest_ref[0, 2 * r]).start(priority=0)
        _row_copy(h_ref, xs_ref, sem, r, dest_ref[0, 2 * r + 1]).start(priority=1)
        return carry

    lax.fori_loop(0, tm, start, 0, unroll=DMA_UNROLL)

    def wait(r, carry):
        _row_copy(h_ref, xs_ref, sem, 0, 0).wait()
        _row_copy(h_ref, xs_ref, sem, 0, 0).wait()
        return carry

    lax.fori_loop(0, tm, wait, 0, unroll=DMA_UNROLL)


def _dispatch(h2, dest, n_slots, tm=256):
    n, d = h2.shape
    dest3 = dest.reshape(n // tm, 1, 2 * tm)
    zeros = jnp.zeros((n_slots, d), F32)
    return pl.pallas_call(
        _dispatch_kernel,
        grid=(n // tm,),
        in_specs=[pl.BlockSpec((None, 1, 2 * tm), lambda i: (i, 0, 0), memory_space=pltpu.SMEM),
                  pl.BlockSpec((tm, d), lambda i: (i, 0)),
                  pl.BlockSpec(memory_space=pl.ANY)],
        out_specs=pl.BlockSpec(memory_space=pl.ANY),
        out_shape=jax.ShapeDtypeStruct((n_slots, d), F32),
        scratch_shapes=[pltpu.SemaphoreType.DMA(())],
        input_output_aliases={2: 0},
        compiler_params=_cparams("arbitrary"),
        name="moe_dispatch",
    )(dest3, h2, zeros)


def _experts_kernel(be_ref, first_ref, nxt_ref, par_ref, nu_ref, x_ref, w1_hbm, w3_hbm, w2_hbm, o_ref,
                    w1f, w3f, w2f, w1b, w3b, w2b, sems):
    i = pl.program_id(0)

    def weight_copies(e, slot):
        return (pltpu.make_async_copy(w1_hbm.at[e], w1f.at[slot], sems.at[0, slot]),
                pltpu.make_async_copy(w3_hbm.at[e], w3f.at[slot], sems.at[1, slot]),
                pltpu.make_async_copy(w2_hbm.at[e], w2f.at[slot], sems.at[2, slot]))

    @pl.when(i == 0)
    def _():
        for cp in weight_copies(be_ref[0], 0):
            cp.start()

    @pl.when(first_ref[i] == 1)
    def _():
        slot = par_ref[i]

        @pl.when(nxt_ref[i] >= 0)
        def _():
            for cp in weight_copies(nxt_ref[i], 1 - slot):
                cp.start()

        for cp in weight_copies(be_ref[i], slot):
            cp.wait()
        w1b[...] = w1f[slot].astype(BF16)
        w3b[...] = w3f[slot].astype(BF16)
        w2b[...] = w2f[slot].astype(BF16)

    @pl.when(i < nu_ref[0])
    def _():
        x = x_ref[...].astype(BF16)
        a = jnp.dot(x, w1b[...], preferred_element_type=F32)
        b = jnp.dot(x, w3b[...], preferred_element_type=F32)
        hmid = (a * jax.nn.sigmoid(a)) * b
        o_ref[...] = jnp.dot(hmid.astype(BF16), w2b[...], preferred_element_type=F32)

    @pl.when(i >= nu_ref[0])
    def _():
        o_ref[...] = jnp.zeros_like(o_ref)


def _experts(xs, block_e, n_used, w1_e, w3_e, w2_e):
    n_slots, d = xs.shape
    nb = n_slots // MOE_BLOCK
    de = w1_e.shape[2]
    idx = jnp.arange(nb, dtype=jnp.int32)
    used = idx < n_used[0]
    first = used & ((idx == 0) | (block_e != jnp.roll(block_e, 1)))
    ordinal = jnp.cumsum(first.astype(jnp.int32)) - 1
    par = (ordinal % 2).astype(jnp.int32)
    first_pos = jnp.where(first, idx, nb)
    next_first = lax.cummin(jnp.concatenate([first_pos[1:], jnp.full((1,), nb, jnp.int32)]), reverse=True)
    nxt = jnp.where(next_first < nb, block_e[jnp.minimum(next_first, nb - 1)], -1).astype(jnp.int32)
    any_spec = pl.BlockSpec(memory_space=pl.ANY)
    grid_spec = pltpu.PrefetchScalarGridSpec(
        num_scalar_prefetch=5,
        grid=(nb,),
        in_specs=[pl.BlockSpec((MOE_BLOCK, d), lambda i, *_: (i, 0)), any_spec, any_spec, any_spec],
        out_specs=pl.BlockSpec((MOE_BLOCK, d), lambda i, *_: (i, 0)),
        scratch_shapes=[pltpu.VMEM((2, d, de), F32), pltpu.VMEM((2, d, de), F32), pltpu.VMEM((2, de, d), F32),
                        pltpu.VMEM((d, de), BF16), pltpu.VMEM((d, de), BF16), pltpu.VMEM((de, d), BF16),
                        pltpu.SemaphoreType.DMA((3, 2))],
    )
    return pl.pallas_call(
        _experts_kernel,
        grid_spec=grid_spec,
        out_shape=jax.ShapeDtypeStruct((n_slots, d), F32),
        compiler_params=_cparams("arbitrary"),
        name="moe_experts",
    )(block_e, first.astype(jnp.int32), nxt, par, n_used, xs, w1_e, w3_e, w2_e)


def _combine_kernel(dest_ref, r_ref, x_ref, gate_ref, gf_ref, ys_ref, o_ref, buf1, buf2, sem):
    tm = x_ref.shape[0]

    def start(r, carry):
        _row_copy(ys_ref, buf1, sem, dest_ref[0, 2 * r], r).start(priority=0)
        _row_copy(ys_ref, buf2, sem, dest_ref[0, 2 * r + 1], r).start(priority=1)
        return carry

    lax.fori_loop(0, tm, start, 0, unroll=DMA_UNROLL)

    def wait(r, carry):
        _row_copy(ys_ref, buf1, sem, 0, 0).wait()
        _row_copy(ys_ref, buf2, sem, 0, 0).wait()
        return carry

    lax.fori_loop(0, tm, wait, 0, unroll=DMA_UNROLL)
    rec = r_ref[...]
    y = buf1[...] * rec[:, ROUTE_W1:ROUTE_W1 + 1] + buf2[...] * rec[:, ROUTE_W2:ROUTE_W2 + 1]
    x2 = x_ref[...] + gate_ref[...] * y
    o_ref[...] = x2 * lax.rsqrt(jnp.mean(x2 * x2, axis=-1, keepdims=True) + EPS) * gf_ref[...]


def _combine(ys, dest, route, x1, gate2, g_final, tm=256):
    bsz, L, d = x1.shape
    n = bsz * L
    tpb = L // tm
    dest3 = dest.reshape(n // tm, 1, 2 * tm)
    return pl.pallas_call(
        _combine_kernel,
        grid=(bsz, tpb),
        in_specs=[pl.BlockSpec((None, 1, 2 * tm), lambda b, i: (b * tpb + i, 0, 0), memory_space=pltpu.SMEM),
                  pl.BlockSpec((tm, LANES), lambda b, i: (b * tpb + i, 0)),
                  pl.BlockSpec((None, tm, d), lambda b, i: (b, i, 0)),
                  pl.BlockSpec((None, 1, d), lambda b, i: (b, 0, 0)),
                  pl.BlockSpec((1, d), lambda b, i: (0, 0)),
                  pl.BlockSpec(memory_space=pl.ANY)],
        out_specs=pl.BlockSpec((None, tm, d), lambda b, i: (b, i, 0)),
        out_shape=jax.ShapeDtypeStruct((bsz, L, d), F32),
        scratch_shapes=[pltpu.VMEM((tm, d), F32), pltpu.VMEM((tm, d), F32), pltpu.SemaphoreType.DMA(())],
        compiler_params=_cparams("arbitrary", "arbitrary"),
        name="moe_combine",
    )(dest3, route, x1, gate2, g_final.reshape(1, d), ys)


SLOT_TM = 512


def _slot_table_kernel(dest_ref, init_ref, tbl_ref, sem, *, plane):
    i = pl.program_id(0)

    @pl.when(i == 0)
    def _():
        cp = pltpu.make_async_copy(init_ref, tbl_ref, sem)
        cp.start()
        cp.wait()

    def body(r, carry):
        tok = i * SLOT_TM + r
        tbl_ref[MOE_BLOCK + dest_ref[0, 2 * r]] = tok
        tbl_ref[MOE_BLOCK + dest_ref[0, 2 * r + 1]] = tok + plane
        return carry

    lax.fori_loop(0, SLOT_TM, body, 0, unroll=DMA_UNROLL)


def _slot_table(dest, n, nb):
    size = (nb + 1) * MOE_BLOCK
    init = n + (jnp.arange(size, dtype=jnp.int32) % MOE_BLOCK)
    tbl = pl.pallas_call(
        functools.partial(_slot_table_kernel, plane=n + MOE_BLOCK),
        grid=(n // SLOT_TM,),
        in_specs=[pl.BlockSpec((None, 1, 2 * SLOT_TM), lambda i: (i, 0, 0), memory_space=pltpu.SMEM),
                  pl.BlockSpec(memory_space=pl.ANY)],
        out_specs=pl.BlockSpec(memory_space=pltpu.SMEM),
        out_shape=jax.ShapeDtypeStruct((size,), jnp.int32),
        scratch_shapes=[pltpu.SemaphoreType.DMA(())],
        compiler_params=_cparams("arbitrary"),
        name="moe_slot_table",
    )(dest.reshape(n // SLOT_TM, 1, 2 * SLOT_TM), init)
    return tbl.reshape(nb + 1, 1, MOE_BLOCK)


def _moe_pair_kernel(be_ref, first_ref, nxt_ref, par_ref, nu_ref,
                     t_prev, t_b0, t_b1, t_next, h_hbm, w1_hbm, w3_hbm, w2_hbm, yt_hbm,
                     x0, x1, y0, y1, w1f, w3f, w2f, w1b, w3b, w2b, wsem, gsem, ssem, *, n_tok):
    p = pl.program_id(0)
    n_used = nu_ref[0]
    plane = n_tok + MOE_BLOCK
    yt_flat = yt_hbm

    def gather(tbl, xbuf, sem):
        for r in range(MOE_BLOCK):
            a = tbl[0, r]
            tok = jnp.minimum(jnp.where(a >= plane, a - plane, a), n_tok - 1)
            _row_copy(h_hbm, xbuf, sem, tok, r).start(priority=r % 2)

    def scatter(tbl, ybuf, sem):
        for r in range(MOE_BLOCK):
            _row_copy(ybuf, yt_flat, sem, r, tbl[0, r]).start(priority=r % 2)

    def wait_rows(src, dst, sem):
        for _ in range(MOE_BLOCK):
            _row_copy(src, dst, sem, 0, 0).wait()

    def weight_copies(e, slot):
        return (pltpu.make_async_copy(w1_hbm.at[e], w1f.at[slot], wsem.at[0, slot]),
                pltpu.make_async_copy(w3_hbm.at[e], w3f.at[slot], wsem.at[1, slot]),
                pltpu.make_async_copy(w2_hbm.at[e], w2f.at[slot], wsem.at[2, slot]))

    def maybe_new_weights(b):
        @pl.when(first_ref[b] == 1)
        def _():
            slot = par_ref[b]

            @pl.when(nxt_ref[b] >= 0)
            def _():
                for cp in weight_copies(nxt_ref[b], 1 - slot):
                    cp.start()

            for cp in weight_copies(be_ref[b], slot):
                cp.wait()
            w1b[...] = w1f[slot].astype(BF16)
            w3b[...] = w3f[slot].astype(BF16)
            w2b[...] = w2f[slot].astype(BF16)

    def expert_mlp(xbuf, ybuf):
        x = xbuf[...].astype(BF16)
        a = jnp.dot(x, w1b[...], preferred_element_type=F32)
        b = jnp.dot(x, w3b[...], preferred_element_type=F32)
        hmid = (a * jax.nn.sigmoid(a)) * b
        ybuf[...] = jnp.dot(hmid.astype(BF16), w2b[...], preferred_element_type=F32)

    @pl.when(p == 0)
    def _():
        for cp in weight_copies(be_ref[0], 0):
            cp.start()
        y0[...] = jnp.zeros_like(y0)
        y1[...] = jnp.zeros_like(y1)
        gather(t_b0, x0, gsem.at[0])
        for r in range(MOE_BLOCK):
            _row_copy(y0, yt_flat, ssem.at[0], r, plane + n_tok + r).start(priority=r % 2)

    @pl.when(2 * p < n_used)
    def _():
        maybe_new_weights(2 * p)
        wait_rows(h_hbm, x0, gsem.at[0])
        wait_rows(y0, yt_flat, ssem.at[0])
        gather(t_b1, x1, gsem.at[1])
        scatter(t_prev, y1, ssem.at[1])
        expert_mlp(x0, y0)
        maybe_new_weights(2 * p + 1)
        wait_rows(h_hbm, x1, gsem.at[1])
        wait_rows(y1, yt_flat, ssem.at[1])
        gather(t_next, x0, gsem.at[0])
        scatter(t_b0, y0, ssem.at[0])
        expert_mlp(x1, y1)

        @pl.when(2 * p + 2 >= n_used)
        def _():
            scatter(t_b1, y1, ssem.at[1])
            wait_rows(y1, yt_flat, ssem.at[1])
            wait_rows(y0, yt_flat, ssem.at[0])
            wait_rows(h_hbm, x0, gsem.at[0])


def _moe_pairs(h2f, table, block_e, n_used, w1_e, w3_e, w2_e):
    n, d = h2f.shape
    nb = table.shape[0] - 1
    de = w1_e.shape[2]
    idx = jnp.arange(nb, dtype=jnp.int32)
    used = idx < n_used[0]
    first = used & ((idx == 0) | (block_e != jnp.roll(block_e, 1)))
    ordinal = jnp.cumsum(first.astype(jnp.int32)) - 1
    par = (ordinal % 2).astype(jnp.int32)
    first_pos = jnp.where(first, idx, nb)
    next_first = lax.cummin(jnp.concatenate([first_pos[1:], jnp.full((1,), nb, jnp.int32)]), reverse=True)
    nxt = jnp.where(next_first < nb, block_e[jnp.minimum(next_first, nb - 1)], -1).astype(jnp.int32)
    any_spec = pl.BlockSpec(memory_space=pl.ANY)

    def tbl(fn):
        return pl.BlockSpec((None, 1, MOE_BLOCK), lambda p, *_: (fn(p), 0, 0), memory_space=pltpu.SMEM)

    grid_spec = pltpu.PrefetchScalarGridSpec(
        num_scalar_prefetch=5,
        grid=(nb // 2,),
        in_specs=[tbl(lambda p: 2 * p), tbl(lambda p: 2 * p + 1), tbl(lambda p: 2 * p + 2),
                  tbl(lambda p: jnp.minimum(2 * p + 3, nb)), any_spec, any_spec, any_spec, any_spec],
        out_specs=any_spec,
        scratch_shapes=[pltpu.VMEM((MOE_BLOCK, d), F32), pltpu.VMEM((MOE_BLOCK, d), F32),
                        pltpu.VMEM((MOE_BLOCK, d), F32), pltpu.VMEM((MOE_BLOCK, d), F32),
                        pltpu.VMEM((2, d, de), F32), pltpu.VMEM((2, d, de), F32), pltpu.VMEM((2, de, d), F32),
                        pltpu.VMEM((d, de), BF16), pltpu.VMEM((d, de), BF16), pltpu.VMEM((de, d), BF16),
                        pltpu.SemaphoreType.DMA((3, 2)), pltpu.SemaphoreType.DMA((2,)),
                        pltpu.SemaphoreType.DMA((2,))],
    )
    return pl.pallas_call(
        functools.partial(_moe_pair_kernel, n_tok=n),
        grid_spec=grid_spec,
        out_shape=jax.ShapeDtypeStruct((2 * (n + MOE_BLOCK), d), F32),
        compiler_params=_cparams("arbitrary"),
        name="moe_experts_fused",
    )(block_e, first.astype(jnp.int32), nxt, par, n_used, table, table, table, table, h2f, w1_e, w3_e, w2_e)


def _combine_planes_kernel(r_ref, ya_ref, yb_ref, x_ref, gate_ref, gf_ref, o_ref):
    rec = r_ref[...]
    y = ya_ref[...] * rec[:, ROUTE_W1:ROUTE_W1 + 1] + yb_ref[...] * rec[:, ROUTE_W2:ROUTE_W2 + 1]
    x2 = x_ref[...] + gate_ref[...] * y
    o_ref[...] = x2 * lax.rsqrt(jnp.mean(x2 * x2, axis=-1, keepdims=True) + EPS) * gf_ref[...]


def _combine_planes(yt, route, x1, gate2, g_final, tm=512):
    bsz, L, d = x1.shape
    n = bsz * L
    tpb = L // tm
    yt3 = yt.reshape(2, n + MOE_BLOCK, d)
    return pl.pallas_call(
        _combine_planes_kernel,
        grid=(bsz, tpb),
        in_specs=[pl.BlockSpec((tm, LANES), lambda b, i: (b * tpb + i, 0)),
                  pl.BlockSpec((None, tm, d), lambda b, i: (0, b * tpb + i, 0)),
                  pl.BlockSpec((None, tm, d), lambda b, i: (1, b * tpb + i, 0)),
                  pl.BlockSpec((None, tm, d), lambda b, i: (b, i, 0)),
                  pl.BlockSpec((None, 1, d), lambda b, i: (b, 0, 0)),
                  pl.BlockSpec((1, d), lambda b, i: (0, 0))],
        out_specs=pl.BlockSpec((None, tm, d), lambda b, i: (b, i, 0)),
        out_shape=jax.ShapeDtypeStruct((bsz, L, d), F32),
        compiler_params=_cparams("parallel", "parallel"),
        name="moe_combine",
    )(route, yt3, yt3, x1, gate2, g_final.reshape(1, d))


def _moe(h2, x1, gate2, g_final, w_group, b_group, w_router, b_router, w1_e, w3_e, w2_e):
    bsz, L, d = x1.shape
    n = bsz * L
    h2f = h2.reshape(n, d)
    route = _router(h2f, w_group, b_group, w_router, b_router)
    dest_rec, counts = _slots(route)
    dest = dest_rec[:, :2].reshape(2 * n)
    nb = (2 * n) // MOE_BLOCK + N_EXPERTS
    cnt = counts[0, :N_EXPERTS].astype(jnp.int32)
    blocks_per_e = (cnt + MOE_BLOCK - 1) // MOE_BLOCK
    ends = jnp.cumsum(blocks_per_e)
    block_e = jnp.clip(jnp.searchsorted(ends, jnp.arange(nb, dtype=jnp.int32), side='right'),
                       0, N_EXPERTS - 1).astype(jnp.int32)
    n_used = ends[-1:].astype(jnp.int32)
    xs = _dispatch(h2f, dest, nb * MOE_BLOCK)
    ys = _experts(xs, block_e, n_used, w1_e, w3_e, w2_e)
    return _combine(ys, dest, route, x1, gate2, g_final)


def kernel(x, c, ctx, c_ctx, w_mod, b_mod, g_norm1, g_norm2, w_in, b_in, w_qk_conv, b_qk_conv,
           w_h_conv, b_h_conv, hf_w1, hf_b1, hf_w2, hf_b2, hf_w3, hf_freq, h_bias, w_a, w_b, w_out,
           w_group, b_group, w_router, b_router, w1_e, w3_e, w2_e, g_final):
    assert w_mod.shape[0] == 1, "single-layer block"
    (w_mod, b_mod, g_norm1, g_norm2, w_in, b_in, w_qk_conv, b_qk_conv, w_h_conv, b_h_conv, hf_w1, hf_b1, hf_w2,
     hf_b2, hf_w3, hf_freq, h_bias, w_a, w_b, w_out, w_group, b_group, w_router, b_router, w1_e, w3_e, w2_e) = (
        t[0] for t in (w_mod, b_mod, g_norm1, g_norm2, w_in, b_in, w_qk_conv, b_qk_conv, w_h_conv, b_h_conv,
                       hf_w1, hf_b1, hf_w2, hf_b2, hf_w3, hf_freq, h_bias, w_a, w_b, w_out, w_group, b_group,
                       w_router, b_router, w1_e, w3_e, w2_e))
    bsz, L, d = x.shape
    lc = ctx.shape[1]
    seg = L // (L // GRID_W)
    chunk_c = min(lc, MLSTM_CHUNK)
    assert bsz + 1 <= 8 and lc % chunk_c == 0 and L % MLSTM_CHUNK == 0

    cond = jnp.zeros((8, d), F32).at[:bsz].set(c).at[bsz].set(c_ctx)
    mod = _adaln(cond, w_mod, b_mod).reshape(8, 6, d)
    modx = mod[:bsz]
    shift1, scale1, gate1, shift2, scale2, gate2 = (modx[:, i:i + 1] for i in range(6))
    shift1c = jnp.broadcast_to(mod[bsz, 0].reshape(1, 1, d), (bsz, 1, d))
    scale1c = jnp.broadcast_to(mod[bsz, 1].reshape(1, 1, d), (bsz, 1, d))

    w_in16 = w_in.astype(BF16)
    k_scale = jnp.full((M_WIDTH,), M_HEAD_DIM ** -0.5, F32)
    qk_scale = jnp.concatenate([jnp.ones((M_WIDTH,), F32), k_scale])
    w_gates, b_gates = w_in[:, IG0:M_COLS], b_in[IG0:M_COLS]

    hc = _norm_mod(ctx, g_norm1, shift1c, scale1c, lc)
    kc = _proj_conv_silu(hc, w_in16[:, K0:V0], b_in[K0:V0], w_qk_conv[:, M_WIDTH:], b_qk_conv[M_WIDTH:],
                         k_scale, lc, lc)
    vc = _proj_act(hc, w_in16[:, V0:O0], b_in[V0:O0], "none", BF16, lc)
    bcc, acc, arc = _gates(hc, w_gates, b_gates, chunk_c)
    zero_state = (jnp.zeros((bsz, 2, M_HEADS, M_HEAD_DIM, M_HEAD_DIM), F32),
                  jnp.zeros((bsz, 2, M_HEADS, 1, M_HEAD_DIM), F32),
                  jnp.zeros((bsz, 2, M_HEADS, 1, LANES), F32))
    _, ctx_state = _mlstm(None, (kc, 0), (vc, 0), bcc, acc, arc, zero_state, False, chunk_c)

    tm = 1024
    h = _norm_mod(x, g_norm1, shift1, scale1, tm)
    w_main = jnp.concatenate([w_in16[:, Q0:IG0], w_in16[:, GA0:IN_COLS]], axis=1)
    b_main = jnp.concatenate([b_in[Q0:IG0], b_in[GA0:IN_COLS]])
    pm = _proj_main(h, w_main, b_main, w_qk_conv, b_qk_conv, qk_scale, seg, tm)
    bc, ac, ar = _gates(h, w_gates, b_gates, MLSTM_CHUNK)
    hdirs, _ = _mlstm((pm, PM_Q), (pm, PM_K), (pm, PM_V), bc, ac, ar, ctx_state, True, MLSTM_CHUNK)

    x0, s = _proj_hyena(h, w_in16[:, HY0:GA0], b_in[HY0:GA0], w_h_conv, b_h_conv, seg, tm)
    y, sumsq = _hyena_long_conv(s, hf_w1, hf_b1, hf_w2, hf_b2, hf_w3, hf_freq)
    yscale = lax.rsqrt(sumsq + EPS) * (1.0 / (2 * L))

    x1, h2 = _merge(hdirs, pm, x0, s, y, x, yscale, h_bias, gate1, g_norm2, shift2, scale2,
                    w_a.astype(BF16), w_b.astype(BF16), w_out.astype(BF16))
    return _moe(h2, x1, gate2, g_final, w_group, b_group, w_router, b_router, w1_e, w3_e, w2_e)
```

```python
import functools
import math

import jax
import jax.numpy as jnp
import numpy as np
from jax import lax
from jax.experimental import pallas as pl
from jax.experimental.pallas import tpu as pltpu

F32 = jnp.float32
BF16 = jnp.bfloat16

D_MODEL = 1024
GRID_W = 64
EPS = 1e-6
M_HEADS = 4
M_HEAD_DIM = 256
M_WIDTH = M_HEADS * M_HEAD_DIM
H_WIDTH = 1024
H_POS_BANDS = 16
H_FILTER_HIDDEN = 64
H_FAST_DECAY_PCT = 0.3
H_SLOW_DECAY_PCT = 1.5
H_DECAY_TARGET = 1e-2
N_GROUPS = 8
EXPERTS_PER_GROUP = 8
N_EXPERTS = N_GROUPS * EXPERTS_PER_GROUP
D_EXPERT = 512
Q0 = 0
K0 = Q0 + M_WIDTH
V0 = K0 + M_WIDTH
O0 = V0 + M_WIDTH
IG0 = O0 + M_WIDTH
FG0 = IG0 + 2 * M_HEADS
M_COLS = FG0 + 2 * M_HEADS
HY0 = M_COLS
GA0 = HY0 + 3 * H_WIDTH
GB0 = GA0 + D_MODEL
IN_COLS = GB0 + D_MODEL

LANES = 128
MLSTM_CHUNK = 512
NEG_BIG = -1e30
VMEM_LIMIT = 48 * 1024 * 1024


def _cparams(*sem):
    return pltpu.CompilerParams(dimension_semantics=sem, vmem_limit_bytes=VMEM_LIMIT)


def _adaln_kernel(c_ref, w_ref, b_ref, o_ref):
    s = c_ref[...]
    s = s * jax.nn.sigmoid(s)
    o_ref[...] = jnp.dot(s.astype(BF16), w_ref[...].astype(BF16), preferred_element_type=F32) + b_ref[...]


def _adaln(cond, w_mod, b_mod):
    n = w_mod.shape[1]
    tn = 1536
    return pl.pallas_call(
        _adaln_kernel,
        grid=(n // tn,),
        in_specs=[pl.BlockSpec((8, D_MODEL), lambda j: (0, 0)),
                  pl.BlockSpec((D_MODEL, tn), lambda j: (0, j)),
                  pl.BlockSpec((1, tn), lambda j: (0, j))],
        out_specs=pl.BlockSpec((8, tn), lambda j: (0, j)),
        out_shape=jax.ShapeDtypeStruct((8, n), F32),
        compiler_params=_cparams("arbitrary"),
        name="adaln",
    )(cond, w_mod, b_mod.reshape(1, n))


def _norm_mod_kernel(x_ref, g_ref, sh_ref, sc_ref, o_ref):
    x = x_ref[...]
    y = x * lax.rsqrt(jnp.mean(x * x, axis=-1, keepdims=True) + EPS)
    y = y * g_ref[...]
    o_ref[...] = (y * (1.0 + sc_ref[...]) + sh_ref[...]).astype(o_ref.dtype)


def _norm_mod(x, g, shift, scale, tm):
    bsz, L, d = x.shape
    return pl.pallas_call(
        _norm_mod_kernel,
        grid=(bsz, L // tm),
        in_specs=[pl.BlockSpec((None, tm, d), lambda b, i: (b, i, 0)),
                  pl.BlockSpec((1, d), lambda b, i: (0, 0)),
                  pl.BlockSpec((None, 1, d), lambda b, i: (b, 0, 0)),
                  pl.BlockSpec((None, 1, d), lambda b, i: (b, 0, 0))],
        out_specs=pl.BlockSpec((None, tm, d), lambda b, i: (b, i, 0)),
        out_shape=jax.ShapeDtypeStruct((bsz, L, d), BF16),
        compiler_params=_cparams("parallel", "parallel"),
        name="norm_mod",
    )(x, g.reshape(1, d), shift, scale)


def _conv3(z, wc, bc, seg):
    tm = z.shape[0]
    pos = lax.broadcasted_iota(jnp.int32, z.shape, 0) & (seg - 1)
    zp = jnp.where(pos == 0, 0.0, pltpu.roll(z, 1, 0))
    zn = jnp.where(pos == seg - 1, 0.0, pltpu.roll(z, tm - 1, 0))
    return zp * wc[0:1, :] + z * wc[1:2, :] + zn * wc[2:3, :] + bc


def _proj_act_kernel(h_ref, w_ref, b_ref, o_ref, *, act):
    z = jnp.dot(h_ref[...], w_ref[...], preferred_element_type=F32) + b_ref[...]
    if act == "sigmoid":
        z = jax.nn.sigmoid(z)
    o_ref[...] = z.astype(o_ref.dtype)


def _proj_act(h, w, b, act, out_dtype, tm, tn=512):
    bsz, L, d = h.shape
    n = w.shape[1]
    return pl.pallas_call(
        functools.partial(_proj_act_kernel, act=act),
        grid=(bsz, L // tm, n // tn),
        in_specs=[pl.BlockSpec((None, tm, d), lambda b_, i, j: (b_, i, 0)),
                  pl.BlockSpec((d, tn), lambda b_, i, j: (0, j)),
                  pl.BlockSpec((1, tn), lambda b_, i, j: (0, j))],
        out_specs=pl.BlockSpec((None, tm, tn), lambda b_, i, j: (b_, i, j)),
        out_shape=jax.ShapeDtypeStruct((bsz, L, n), out_dtype),
        compiler_params=_cparams("parallel", "parallel", "arbitrary"),
        name="proj_" + act,
    )(h, w, b.reshape(1, n))


def _proj_conv_silu_kernel(h_ref, w_ref, b_ref, wc_ref, bc_ref, cs_ref, o_ref, *, seg):
    z = jnp.dot(h_ref[...], w_ref[...], preferred_element_type=F32) + b_ref[...]
    y = _conv3(z, wc_ref[...], bc_ref[...], seg)
    y = y * jax.nn.sigmoid(y)
    o_ref[...] = (y * cs_ref[...]).astype(o_ref.dtype)


def _proj_conv_silu(h, w, b, wc, bc, colscale, seg, tm, tn=512):
    bsz, L, d = h.shape
    n = w.shape[1]
    col = lambda b_, i, j: (0, j)
    return pl.pallas_call(
        functools.partial(_proj_conv_silu_kernel, seg=seg),
        grid=(bsz, L // tm, n // tn),
        in_specs=[pl.BlockSpec((None, tm, d), lambda b_, i, j: (b_, i, 0)),
                  pl.BlockSpec((d, tn), col),
                  pl.BlockSpec((1, tn), col),
                  pl.BlockSpec((3, tn), col),
                  pl.BlockSpec((1, tn), col),
                  pl.BlockSpec((1, tn), col)],
        out_specs=pl.BlockSpec((None, tm, tn), lambda b_, i, j: (b_, i, j)),
        out_shape=jax.ShapeDtypeStruct((bsz, L, n), BF16),
        compiler_params=_cparams("parallel", "parallel", "arbitrary"),
        name="proj_conv_silu",
    )(h, w, b.reshape(1, n), wc, bc.reshape(1, n), colscale.reshape(1, n))


PROJ_TN = 1024
PROJ_SUB = 512
PM_Q, PM_K, PM_V, PM_O, PM_GA, PM_GB = range(6)


def _proj_main_kernel(h_ref, w_ref, b_ref, wc_ref, bc_ref, cs_ref, o_ref, *, seg):
    j = pl.program_id(2)

    def run(epilogue):
        for c in range(PROJ_TN // PROJ_SUB):
            sl = slice(c * PROJ_SUB, (c + 1) * PROJ_SUB)
            z = jnp.dot(h_ref[...], w_ref[:, sl], preferred_element_type=F32) + b_ref[:, sl]
            o_ref[:, sl] = epilogue(z, sl).astype(o_ref.dtype)

    def conv_silu(z, sl):
        y = _conv3(z, wc_ref[:, sl], bc_ref[:, sl], seg)
        return (y * jax.nn.sigmoid(y)) * cs_ref[:, sl]

    @pl.when(j <= PM_K)
    def _():
        run(conv_silu)

    @pl.when(j == PM_V)
    def _():
        run(lambda z, sl: z)

    @pl.when(j >= PM_O)
    def _():
        run(lambda z, sl: jax.nn.sigmoid(z))


def _proj_main(h, w, b, wc, bc, colscale, seg, tm):
    bsz, L, d = h.shape
    n = w.shape[1]
    qk = lambda b_, i, j: (0, jnp.minimum(j, PM_K))
    return pl.pallas_call(
        functools.partial(_proj_main_kernel, seg=seg),
        grid=(bsz, L // tm, n // PROJ_TN),
        in_specs=[pl.BlockSpec((None, tm, d), lambda b_, i, j: (b_, i, 0)),
                  pl.BlockSpec((d, PROJ_TN), lambda b_, i, j: (0, j)),
                  pl.BlockSpec((1, PROJ_TN), lambda b_, i, j: (0, j)),
                  pl.BlockSpec((3, PROJ_TN), qk),
                  pl.BlockSpec((1, PROJ_TN), qk),
                  pl.BlockSpec((1, PROJ_TN), qk)],
        out_specs=pl.BlockSpec((None, tm, PROJ_TN), lambda b_, i, j: (b_, i, j)),
        out_shape=jax.ShapeDtypeStruct((bsz, L, n), BF16),
        compiler_params=_cparams("parallel", "parallel", "arbitrary"),
        name="proj_main",
    )(h, w, b.reshape(1, n), wc, bc.reshape(1, -1), colscale.reshape(1, -1))


def _proj_hyena_kernel(h_ref, w0_ref, w1_ref, w2_ref, b_ref, wc_ref, bc_ref, x0_ref, s_ref, *, seg):
    h = h_ref[...]
    us = []
    for g, w_ref in enumerate((w0_ref, w1_ref, w2_ref)):
        z = jnp.dot(h, w_ref[...], preferred_element_type=F32) + b_ref[g]
        us.append(_conv3(z, wc_ref[g], bc_ref[g], seg))
    x0_ref[...] = us[0].astype(x0_ref.dtype)
    s_ref[...] = us[1] * us[2]


def _proj_hyena(h, w, b, wc, bc, seg, tm, tn=512):
    bsz, L, d = h.shape
    nblk = H_WIDTH // tn
    b3 = b.reshape(3, 1, H_WIDTH)
    wc3 = wc.reshape(3, 3, H_WIDTH).transpose(1, 0, 2)
    bc3 = bc.reshape(3, 1, H_WIDTH)
    out_spec = pl.BlockSpec((None, tm, tn), lambda b_, i, j: (b_, i, j))
    return pl.pallas_call(
        functools.partial(_proj_hyena_kernel, seg=seg),
        grid=(bsz, L // tm, nblk),
        in_specs=[pl.BlockSpec((None, tm, d), lambda b_, i, j: (b_, i, 0)),
                  pl.BlockSpec((d, tn), lambda b_, i, j: (0, j)),
                  pl.BlockSpec((d, tn), lambda b_, i, j: (0, nblk + j)),
                  pl.BlockSpec((d, tn), lambda b_, i, j: (0, 2 * nblk + j)),
                  pl.BlockSpec((3, 1, tn), lambda b_, i, j: (0, 0, j)),
                  pl.BlockSpec((3, 3, tn), lambda b_, i, j: (0, 0, j)),
                  pl.BlockSpec((3, 1, tn), lambda b_, i, j: (0, 0, j))],
        out_specs=[out_spec, out_spec],
        out_shape=[jax.ShapeDtypeStruct((bsz, L, H_WIDTH), BF16),
                   jax.ShapeDtypeStruct((bsz, L, H_WIDTH), F32)],
        compiler_params=_cparams("parallel", "parallel", "arbitrary"),
        name="proj_hyena",
    )(h, w, w, w, b3, wc3, bc3)


N_GATES = 4 * M_HEADS


def _split3(x):
    hi = x.astype(BF16)
    r1 = x - hi.astype(F32)
    mid = r1.astype(BF16)
    lo = (r1 - mid.astype(F32)).astype(BF16)
    return hi, mid, lo


def _log_sigmoid(x):
    return jnp.minimum(x, 0.0) - jnp.log1p(jnp.exp(-jnp.abs(x)))


def _gates_kernel(h_ref, w_ref, wt_ref, b_ref, bt_ref, bc_ref, ac_ref, ar_ref):
    h = h_ref[...]
    t = h.shape[0]
    z = jnp.dot(h, w_ref[...], preferred_element_type=F32) + b_ref[...]
    zt = lax.dot_general(wt_ref[...], h, (((1,), (1,)), ((), ())),
                         preferred_element_type=F32) + bt_ref[...]
    r = lax.broadcasted_iota(jnp.int32, (t, t), 0)
    c = lax.broadcasted_iota(jnp.int32, (t, t), 1)
    lower = (r >= c).astype(BF16)
    upper = (r <= c).astype(BF16)

    lf = _log_sigmoid(z)
    parts = _split3(lf)
    cf = sum(jnp.dot(lower, p, preferred_element_type=F32) for p in parts)
    cb = sum(jnp.dot(upper, p, preferred_element_type=F32) for p in parts)
    lane = lax.broadcasted_iota(jnp.int32, z.shape, 1)
    bc = jnp.where(lane < FG_LANE0 + M_HEADS, cf, cb)
    bc = pltpu.roll(bc, LANES - FG_LANE0, 1)
    bc_ref[...] = bc
    ac_ref[...] = z - bc

    lft = _log_sigmoid(zt[FG_LANE0:, :])
    tparts = _split3(lft)
    cft = sum(jnp.dot(p, upper, preferred_element_type=F32) for p in tparts)
    cbt = sum(jnp.dot(p, lower, preferred_element_type=F32) for p in tparts)
    row = lax.broadcasted_iota(jnp.int32, cft.shape, 0)
    ar_ref[...] = zt[:FG_LANE0, :] - jnp.where(row < M_HEADS, cft, cbt)


FG_LANE0 = 2 * M_HEADS


def _gates(h, w_g, b_g, chunk):
    bsz, L, d = h.shape
    w_pad = jnp.zeros((d, LANES), F32).at[:, :N_GATES].set(w_g).astype(BF16)
    b_pad = jnp.zeros((1, LANES), F32).at[0, :N_GATES].set(b_g)
    wt = w_g.T.astype(BF16)
    bt = b_g.reshape(N_GATES, 1)
    tok = pl.BlockSpec((None, chunk, LANES), lambda b_, i: (b_, i, 0))
    return pl.pallas_call(
        _gates_kernel,
        grid=(bsz, L // chunk),
        in_specs=[pl.BlockSpec((None, chunk, d), lambda b_, i: (b_, i, 0)),
                  pl.BlockSpec((d, LANES), lambda b_, i: (0, 0)),
                  pl.BlockSpec((N_GATES, d), lambda b_, i: (0, 0)),
                  pl.BlockSpec((1, LANES), lambda b_, i: (0, 0)),
                  pl.BlockSpec((N_GATES, 1), lambda b_, i: (0, 0))],
        out_specs=[tok, tok, pl.BlockSpec((None, FG_LANE0, chunk), lambda b_, i: (b_, 0, i))],
        out_shape=[jax.ShapeDtypeStruct((bsz, L, LANES), F32),
                   jax.ShapeDtypeStruct((bsz, L, LANES), F32),
                   jax.ShapeDtypeStruct((bsz, FG_LANE0, L), F32)],
        compiler_params=_cparams("parallel", "parallel"),
        name="mlstm_gates",
    )(h, w_pad, wt, b_pad, bt)


def _mlstm_kernel(*refs, emit_h, n_chunks):
    if emit_h:
        (q_ref, k_ref, v_ref, bc_ref, ac_ref, ar_ref, c0_ref, n0_ref, m0_ref,
         h_ref, cf_ref, nf_ref, mf_ref, c_sc, n_sc, m_sc) = refs
    else:
        (k_ref, v_ref, bc_ref, ac_ref, ar_ref, c0_ref, n0_ref, m0_ref,
         cf_ref, nf_ref, mf_ref, c_sc, n_sc, m_sc) = refs
    d = pl.program_id(1)
    j = pl.program_id(2)
    fwd = d == 0
    t = k_ref.shape[0]
    dh = M_HEAD_DIM

    @pl.when(j == 0)
    def _():
        c_sc[...] = c0_ref[...]
        n_sc[...] = n0_ref[...]
        m_sc[...] = m0_ref[...]

    r = lax.broadcasted_iota(jnp.int32, (t, t), 0)
    c = lax.broadcasted_iota(jnp.int32, (t, t), 1)
    causal = jnp.where(fwd, r - c, c - r) >= 0
    bc_all = bc_ref[...]
    ac_all = ac_ref[...]
    ar_all = ar_ref[...]
    for hd in range(M_HEADS):
        sl = slice(hd * dh, (hd + 1) * dh)
        bc = jnp.where(fwd, bc_all[:, hd:hd + 1], bc_all[:, M_HEADS + hd:M_HEADS + hd + 1])
        ac = jnp.where(fwd, ac_all[:, hd:hd + 1], ac_all[:, M_HEADS + hd:M_HEADS + hd + 1])
        ar = jnp.where(fwd, ar_all[hd:hd + 1, :], ar_all[M_HEADS + hd:M_HEADS + hd + 1, :])
        b_tot = jnp.where(fwd, bc[t - 1:t, :], bc[0:1, :])
        m_prev = m_sc[hd][:, 0:1]
        k_h = k_ref[:, sl]
        v_h = v_ref[:, sl]
        if emit_h:
            q_h = q_ref[:, sl]
            dm = jnp.where(causal, bc + ar, NEG_BIG)
            inter = bc + m_prev
            m_t = jnp.maximum(inter, jnp.max(dm, axis=1, keepdims=True))
            qk = lax.dot_general(q_h, k_h, (((1,), (1,)), ((), ())), preferred_element_type=F32)
            s = qk * jnp.exp(dm - m_t)
            carry = jnp.exp(inter - m_t)
            num = (jnp.dot(s.astype(BF16), v_h, preferred_element_type=F32)
                   + carry * jnp.dot(q_h, c_sc[hd].astype(BF16), preferred_element_type=F32))
            den = (jnp.sum(s, axis=1, keepdims=True)
                   + carry * jnp.sum(q_h.astype(F32) * n_sc[hd], axis=1, keepdims=True))
            h_ref[:, sl] = (num / jnp.maximum(jnp.abs(den), jnp.exp(-m_t))).astype(h_ref.dtype)
        g = b_tot + ac
        m_new = jnp.maximum(b_tot + m_prev, jnp.max(g, axis=0, keepdims=True))
        wgt = jnp.exp(g - m_new)
        decay = jnp.exp(b_tot + m_prev - m_new)
        kw = k_h.astype(F32) * wgt
        c_sc[hd] = decay * c_sc[hd] + lax.dot_general(kw.astype(BF16), v_h, (((0,), (0,)), ((), ())),
                                                      preferred_element_type=F32)
        n_sc[hd] = decay * n_sc[hd] + jnp.sum(kw, axis=0, keepdims=True)
        m_sc[hd] = jnp.broadcast_to(m_new, (1, LANES))

    @pl.when(j == n_chunks - 1)
    def _():
        cf_ref[...] = c_sc[...]
        nf_ref[...] = n_sc[...]
        mf_ref[...] = m_sc[...]


def _mlstm(q, k, v, bc, ac, ar, state, emit_h, t):
    bsz, L, _ = k[0].shape
    nc = L // t
    seq = lambda b_, d, j: (b_, j + d * (nc - 1 - 2 * j), 0)
    st = lambda b_, d, j: (b_, d, 0, 0, 0)

    def tok(col):
        return pl.BlockSpec((None, t, M_WIDTH), lambda b_, d, j: (b_, j + d * (nc - 1 - 2 * j), col))

    gate_spec = pl.BlockSpec((None, t, LANES), seq)
    ar_spec = pl.BlockSpec((None, FG_LANE0, t), lambda b_, d, j: (b_, 0, j + d * (nc - 1 - 2 * j)))
    c_spec = pl.BlockSpec((None, None, M_HEADS, M_HEAD_DIM, M_HEAD_DIM), st)
    n_spec = pl.BlockSpec((None, None, M_HEADS, 1, M_HEAD_DIM), st)
    m_spec = pl.BlockSpec((None, None, M_HEADS, 1, LANES), st)
    state_shapes = [jax.ShapeDtypeStruct((bsz, 2, M_HEADS, M_HEAD_DIM, M_HEAD_DIM), F32),
                    jax.ShapeDtypeStruct((bsz, 2, M_HEADS, 1, M_HEAD_DIM), F32),
                    jax.ShapeDtypeStruct((bsz, 2, M_HEADS, 1, LANES), F32)]
    in_specs = [tok(k[1]), tok(v[1]), gate_spec, gate_spec, ar_spec, c_spec, n_spec, m_spec]
    args = [k[0], v[0], bc, ac, ar, *state]
    out_specs = [c_spec, n_spec, m_spec]
    out_shape = list(state_shapes)
    if emit_h:
        in_specs = [tok(q[1])] + in_specs
        args = [q[0]] + args
        out_specs = [pl.BlockSpec((None, None, t, M_WIDTH),
                                  lambda b_, d, j: (d, b_, j + d * (nc - 1 - 2 * j), 0))] + out_specs
        out_shape = [jax.ShapeDtypeStruct((2, bsz, L, M_WIDTH), BF16)] + out_shape
    outs = pl.pallas_call(
        functools.partial(_mlstm_kernel, emit_h=emit_h, n_chunks=nc),
        grid=(bsz, 2, nc),
        in_specs=in_specs,
        out_specs=out_specs,
        out_shape=out_shape,
        scratch_shapes=[pltpu.VMEM((M_HEADS, M_HEAD_DIM, M_HEAD_DIM), F32),
                        pltpu.VMEM((M_HEADS, 1, M_HEAD_DIM), F32),
                        pltpu.VMEM((M_HEADS, 1, LANES), F32)],
        compiler_params=_cparams("parallel", "parallel", "arbitrary"),
        name="mlstm" if emit_h else "mlstm_state",
    )(*args)
    if emit_h:
        return outs[0], tuple(outs[1:])
    return None, tuple(outs)


DFT_M_TILE = 8
DFT_C_TILE = 512
FEAT_ROWS = 16


def _filter_outer_kernel(bands_ref, w1t_ref, b1_ref, w2t_ref, b2_ref, w3p_ref, w3f_ref, fr_ref, dl_ref, l_ref,
                         a_ref, ss_ref, *, L, n1, n2):
    i = pl.program_id(0)
    h = n1 // 2
    cols = DFT_M_TILE * h

    def positions(shape, axis, side):
        q = lax.broadcasted_iota(jnp.int32, shape, axis)
        mm, jj = q // h, q % h
        n = n2 * (jj + side * h) + i * DFT_M_TILE + mm
        return n, jnp.where(n < L, n, 2 * L - n).astype(F32)

    taps = []
    sumsq = jnp.zeros((1, a_ref.shape[-1]), F32)
    for side, w3_ref in ((0, w3p_ref), (1, w3f_ref)):
        _, p_row = positions((1, cols), 1, side)
        t_row = p_row / float(max(L - 1, 1))
        ang = ((2 * math.pi / L) * p_row) * bands_ref[...]
        row = lax.broadcasted_iota(jnp.int32, (FEAT_ROWS, cols), 0)
        feats = jnp.concatenate([jnp.where(row == 0, t_row, 0.0), jnp.cos(ang), -jnp.sin(ang)], axis=0)
        fr = fr_ref[...]
        hid = jnp.sin(fr * (jnp.dot(w1t_ref[...], feats.astype(BF16), preferred_element_type=F32) + b1_ref[...]))
        hid = jnp.sin(fr * (jnp.dot(w2t_ref[...], hid.astype(BF16), preferred_element_type=F32) + b2_ref[...]))
        filt = lax.dot_general(hid.astype(BF16), w3_ref[...], (((0,), (0,)), ((), ())),
                               preferred_element_type=F32)
        n_col, p_col = positions((cols, 1), 0, side)
        t_col = p_col / float(max(L - 1, 1))
        kern = filt * jnp.exp(-t_col * jnp.abs(dl_ref[...]))
        kern = jnp.where(n_col == L, 0.0, kern)
        sumsq = sumsq + jnp.sum(kern * kern, axis=0, keepdims=True)
        taps.append(kern)

    for mm in range(DFT_M_TILE):
        x = jnp.concatenate([taps[0][mm * h:(mm + 1) * h], taps[1][mm * h:(mm + 1) * h]], axis=0)
        out = jnp.dot(l_ref[...], x.astype(BF16), preferred_element_type=F32)
        a_ref[0, :, mm, :] = out[:n1]
        a_ref[1, :, mm, :] = out[n1:]

    @pl.when(i == 0)
    def _():
        ss_ref[...] = jnp.zeros_like(ss_ref)

    ss_ref[...] += sumsq


def _filter_outer(L, n1, n2, fwd_r, w1, b1, w2, b2, w3, freq):
    hid = H_FILTER_HIDDEN
    bands = jnp.linspace(1e-4, H_POS_BANDS - 1, H_POS_BANDS, dtype=F32).reshape(H_POS_BANDS, 1)
    w1t = jnp.zeros((hid, 3 * FEAT_ROWS), F32)
    w1t = w1t.at[:, 0].set(w1[0]).at[:, FEAT_ROWS:2 * FEAT_ROWS].set(w1[1:1 + H_POS_BANDS].T)
    w1t = w1t.at[:, 2 * FEAT_ROWS:].set(w1[1 + H_POS_BANDS:].T).astype(BF16)
    w3h = w3.astype(BF16)
    max_decay = math.log(H_DECAY_TARGET) / H_FAST_DECAY_PCT
    min_decay = math.log(H_DECAY_TARGET) / H_SLOW_DECAY_PCT
    deltas = jnp.linspace(min_decay, max_decay, H_WIDTH, dtype=F32).reshape(1, H_WIDTH)
    col = lambda v: v.reshape(hid, 1)
    full = lambda a: pl.BlockSpec(a.shape, lambda i: (0,) * a.ndim)
    args = [bands, w1t, col(b1), w2.T.astype(BF16), col(b2)]
    return pl.pallas_call(
        functools.partial(_filter_outer_kernel, L=L, n1=n1, n2=n2),
        grid=(n2 // DFT_M_TILE,),
        in_specs=[full(a) for a in args]
        + [pl.BlockSpec((hid, H_WIDTH), lambda i: (0, 0)), pl.BlockSpec((hid, H_WIDTH), lambda i: (0, 1)),
           full(col(freq)), full(deltas), full(fwd_r)],
        out_specs=[pl.BlockSpec((2, n1, DFT_M_TILE, H_WIDTH), lambda i: (0, 0, i, 0)),
                   pl.BlockSpec((1, H_WIDTH), lambda i: (0, 0))],
        out_shape=[jax.ShapeDtypeStruct((2, n1, n2, H_WIDTH), F32),
                   jax.ShapeDtypeStruct((1, H_WIDTH), F32)],
        compiler_params=_cparams("arbitrary"),
        name="hyena_filter_outer",
    )(*args, w3h, w3h, col(freq), deltas, fwd_r)


def _dft_factors(n):
    lg = int(round(math.log2(n)))
    n1 = 1 << ((lg + 1) // 2)
    return n1, n // n1


def _dft_outer_matrices(n1):
    k = np.arange(n1)[:, None]
    n = np.arange(n1)[None, :]
    ang = 2.0 * np.pi * ((k * n) % n1) / n1
    cr, ci = np.cos(ang), -np.sin(ang)
    h = n1 // 2
    fwd_c = np.block([[cr[:, :h], -ci[:, :h]], [ci[:, :h], cr[:, :h]]])
    fwd_r = np.concatenate([cr, ci], axis=0)
    ir, ii = cr[:h, :], -ci[:h, :]
    inv = np.block([[ir, -ii], [ii, ir]])
    return (jnp.asarray(fwd_c, F32).astype(BF16), jnp.asarray(fwd_r, F32).astype(BF16),
            jnp.asarray(inv, F32).astype(BF16))


def _dft_inner_matrices(n1, n2):
    n = n1 * n2
    k2 = np.arange(n2)[:, None]
    m = np.arange(n2)[None, :]
    ang = 2.0 * np.pi * ((k2 * m) % n2) / n2
    fr, fi = np.cos(ang), -np.sin(ang)
    f = np.block([[fr, -fi], [fi, fr]])
    k1 = jnp.arange(n1, dtype=jnp.int32)[:, None]
    tw_ang = ((jnp.arange(n2, dtype=jnp.int32)[None, :] * k1) % n).astype(F32) * (2.0 * math.pi / n)
    rep = lambda t: jnp.broadcast_to(t[:, :, None], (n1, n2, LANES))
    return (jnp.asarray(f, F32).astype(BF16), jnp.asarray(f.T, F32).astype(BF16),
            rep(jnp.cos(tw_ang)), rep(-jnp.sin(tw_ang)))


def _outer_dft_kernel(l_ref, x_ref, o_ref):
    p_in, p_out = x_ref.shape[0], o_ref.shape[0]
    r_out = o_ref.shape[1]
    for mm in range(x_ref.shape[2]):
        parts = [x_ref[p, :, mm, :] for p in range(p_in)]
        x = parts[0] if p_in == 1 else jnp.concatenate(parts, axis=0)
        out = jnp.dot(l_ref[...], x.astype(BF16), preferred_element_type=F32)
        for p in range(p_out):
            o_ref[p, :, mm, :] = out[p * r_out:(p + 1) * r_out]


def _outer_dft(lmat, x4, p_out):
    p_in, r_in, n2, c = x4.shape
    r_out = lmat.shape[0] // p_out
    tc = min(DFT_C_TILE, c)
    return pl.pallas_call(
        _outer_dft_kernel,
        grid=(n2 // DFT_M_TILE, c // tc),
        in_specs=[pl.BlockSpec(lmat.shape, lambda m, j: (0, 0)),
                  pl.BlockSpec((p_in, r_in, DFT_M_TILE, tc), lambda m, j: (0, 0, m, j))],
        out_specs=pl.BlockSpec((p_out, r_out, DFT_M_TILE, tc), lambda m, j: (0, 0, m, j)),
        out_shape=jax.ShapeDtypeStruct((p_out, r_out, n2, c), F32),
        compiler_params=_cparams("parallel", "parallel"),
        name="dft_outer",
    )(lmat, x4)


DFT_K_TILE = 2


def _twiddled_inner_dft(f_ref, twr_ref, twi_ref, a_ref, kk):
    n2, c = a_ref.shape[2], a_ref.shape[3]
    twr = jnp.tile(twr_ref[kk], (1, c // LANES))
    twi = jnp.tile(twi_ref[kk], (1, c // LANES))
    ar, ai = a_ref[0, kk], a_ref[1, kk]
    a = jnp.concatenate([(ar * twr - ai * twi).astype(BF16), (ar * twi + ai * twr).astype(BF16)], axis=0)
    x = jnp.dot(f_ref[...], a, preferred_element_type=F32)
    return x[:n2], x[n2:], twr, twi


def _inner_fwd_kernel(f_ref, twr_ref, twi_ref, a_ref, o_ref):
    for kk in range(a_ref.shape[1]):
        xr, xi, _, _ = _twiddled_inner_dft(f_ref, twr_ref, twi_ref, a_ref, kk)
        o_ref[0, kk] = xr.astype(o_ref.dtype)
        o_ref[1, kk] = xi.astype(o_ref.dtype)


def _inner_specs(n1, n2, c):
    blk = pl.BlockSpec((2, DFT_K_TILE, n2, c), lambda k: (0, k, 0, 0))
    mat = pl.BlockSpec((2 * n2, 2 * n2), lambda k: (0, 0))
    tw = pl.BlockSpec((DFT_K_TILE, n2, LANES), lambda k: (k, 0, 0))
    return blk, mat, tw


def _inner_fwd(f, twr, twi, a):
    _, n1, n2, c = a.shape
    blk, mat, tw = _inner_specs(n1, n2, c)
    return pl.pallas_call(
        _inner_fwd_kernel,
        grid=(n1 // DFT_K_TILE,),
        in_specs=[mat, tw, tw, blk],
        out_specs=blk,
        out_shape=jax.ShapeDtypeStruct((2, n1, n2, c), BF16),
        compiler_params=_cparams("parallel"),
        name="dft_inner_filter",
    )(f, twr, twi, a)


def _inner_conv_kernel(f_ref, ft_ref, twr_ref, twi_ref, a_ref, k_ref, o_ref):
    n2 = a_ref.shape[2]
    for kk in range(a_ref.shape[1]):
        xr, xi, twr, twi = _twiddled_inner_dft(f_ref, twr_ref, twi_ref, a_ref, kk)
        kr, ki = k_ref[0, kk].astype(F32), k_ref[1, kk].astype(F32)
        yr = xr * kr - xi * ki
        yi = xr * ki + xi * kr
        y = jnp.concatenate([yr.astype(BF16), yi.astype(BF16)], axis=0)
        b = jnp.dot(ft_ref[...], y, preferred_element_type=F32)
        br, bi = b[:n2], b[n2:]
        o_ref[0, kk] = br * twr + bi * twi
        o_ref[1, kk] = bi * twr - br * twi


def _inner_conv(f, ft, twr, twi, a, kf):
    _, n1, n2, c = a.shape
    blk, mat, tw = _inner_specs(n1, n2, c)
    return pl.pallas_call(
        _inner_conv_kernel,
        grid=(n1 // DFT_K_TILE,),
        in_specs=[mat, mat, tw, tw, blk, blk],
        out_specs=blk,
        out_shape=jax.ShapeDtypeStruct((2, n1, n2, c), F32),
        compiler_params=_cparams("parallel"),
        name="dft_inner_conv",
    )(f, ft, twr, twi, a, kf)


def _hyena_long_conv(s, w1, b1, w2, b2, w3, freq):
    bsz, L, c = s.shape
    assert bsz == 2
    n = 2 * L
    n1, n2 = _dft_factors(n)
    fwd_c, fwd_r, inv = _dft_outer_matrices(n1)
    f, ft, twr, twi = _dft_inner_matrices(n1, n2)
    af, sumsq = _filter_outer(L, n1, n2, fwd_r, w1, b1, w2, b2, w3, freq)
    kf = _inner_fwd(f, twr, twi, af)
    a = _outer_dft(fwd_c, s.reshape(2, n1 // 2, n2, c), 2)
    b = _inner_conv(f, ft, twr, twi, a, kf)
    y = _outer_dft(inv, b, 2)
    return y.reshape(2, L, c), sumsq


def _pack_bf16_pairs(x):
    half = x.shape[1] // 2
    lo = pltpu.bitcast(x[:, :half].astype(BF16).astype(F32), jnp.uint32) >> 16
    hi = pltpu.bitcast(x[:, half:].astype(BF16).astype(F32), jnp.uint32) & jnp.uint32(0xFFFF0000)
    return lo | hi


def _unpack_bf16_pairs(p):
    lo = pltpu.bitcast(p << 16, F32).astype(BF16)
    hi = pltpu.bitcast(p & jnp.uint32(0xFFFF0000), F32).astype(BF16)
    return jnp.concatenate([lo, hi], axis=1)


def _merge_kernel(hf_ref, hb_ref, o_ref, x0_ref, s_ref, y_ref, ga_ref, gb_ref, x_ref,
                  ysc_ref, hbias_ref, gate_ref, g2_ref, sh_ref, sc_ref,
                  wa_ref, wb_ref, wo_ref, x1_ref, h2_ref):
    a = o_ref[...].astype(F32) * (hf_ref[...].astype(F32) + hb_ref[...].astype(F32))
    s = s_ref[...]
    hy = x0_ref[...].astype(F32) * (y_ref[...] * ysc_ref[...] + hbias_ref[...] * s)
    pa = jnp.dot(a.astype(BF16), wa_ref[...], preferred_element_type=F32)
    pb = jnp.dot(hy.astype(BF16), wb_ref[...], preferred_element_type=F32)
    mix = ga_ref[...].astype(F32) * pa + gb_ref[...].astype(F32) * pb
    out = jnp.dot(mix.astype(BF16), wo_ref[...], preferred_element_type=F32)
    x1 = x_ref[...] + gate_ref[...] * out
    x1_ref[...] = x1
    y = x1 * lax.rsqrt(jnp.mean(x1 * x1, axis=-1, keepdims=True) + EPS) * g2_ref[...]
    h2_ref[...] = _pack_bf16_pairs(y * (1.0 + sc_ref[...]) + sh_ref[...])


def _merge(hdirs, pm, x0, s, y, x, yscale, h_bias, gate1, g2, shift2, scale2, w_a, w_b, w_out, tm=256):
    bsz, L, d = x.shape
    tok = pl.BlockSpec((None, tm, d), lambda b, i: (b, i, 0))

    def pm_tile(col):
        return pl.BlockSpec((None, tm, d), lambda b, i: (b, i, col))

    vec = pl.BlockSpec((1, d), lambda b, i: (0, 0))
    bvec = pl.BlockSpec((None, 1, d), lambda b, i: (b, 0, 0))
    wsp = pl.BlockSpec((d, d), lambda b, i: (0, 0))
    return pl.pallas_call(
        _merge_kernel,
        grid=(bsz, L // tm),
        in_specs=[pl.BlockSpec((None, None, tm, d), lambda b, i: (0, b, i, 0)),
                  pl.BlockSpec((None, None, tm, d), lambda b, i: (1, b, i, 0)),
                  pm_tile(PM_O), tok, tok, tok, pm_tile(PM_GA), pm_tile(PM_GB), tok,
                  vec, vec, bvec, vec, bvec, bvec, wsp, wsp, wsp],
        out_specs=[tok, pl.BlockSpec((None, tm, d // 2), lambda b, i: (b, i, 0))],
        out_shape=[jax.ShapeDtypeStruct((bsz, L, d), F32), jax.ShapeDtypeStruct((bsz, L, d // 2), jnp.uint32)],
        compiler_params=_cparams("parallel", "parallel"),
        name="merge",
    )(hdirs, hdirs, pm, x0, s, y, pm, pm, x, yscale, h_bias.reshape(1, d), gate1, g2.reshape(1, d),
      shift2, scale2, w_a, w_b, w_out)


MOE_BLOCK = 256
ROUTE_E1, ROUTE_E2, ROUTE_W1, ROUTE_W2 = 0, 1, 2, 3
EXP_LANE0 = N_GROUPS


def _first_lane_of_max(val, valid, lane):
    masked = jnp.where(valid, val, NEG_BIG)
    mx = jnp.max(masked, axis=1, keepdims=True)
    idx = jnp.min(jnp.where(valid & (masked == mx), lane, LANES), axis=1, keepdims=True)
    return mx, idx


def _router_kernel(h_ref, w_ref, b_ref, r_ref):
    logits = jnp.dot(_unpack_bf16_pairs(h_ref[...]), w_ref[...], preferred_element_type=F32) + b_ref[...]
    lane = lax.broadcasted_iota(jnp.int32, logits.shape, 1)
    is_g = lane < N_GROUPS
    gmax, gsel = _first_lane_of_max(logits, is_g, lane)
    gsum = jnp.sum(jnp.where(is_g, jnp.exp(logits - gmax), 0.0), axis=1, keepdims=True)
    gw = 1.0 / gsum
    lo = EXP_LANE0 + gsel * EXPERTS_PER_GROUP
    in_grp = (lane >= lo) & (lane < lo + EXPERTS_PER_GROUP)
    emax, l1 = _first_lane_of_max(logits, in_grp, lane)
    esum = jnp.sum(jnp.where(in_grp, jnp.exp(logits - emax), 0.0), axis=1, keepdims=True)
    e2max, l2 = _first_lane_of_max(logits, in_grp & (lane != l1), lane)
    v1 = 1.0 / esum
    v2 = jnp.exp(e2max - emax) / esum
    vs = v1 + v2
    w1 = gw * v1 / vs
    w2 = gw * v2 / vs
    e1 = (l1 - EXP_LANE0).astype(F32)
    e2 = (l2 - EXP_LANE0).astype(F32)
    r_ref[...] = jnp.where(lane == ROUTE_E1, e1,
                           jnp.where(lane == ROUTE_E2, e2,
                                     jnp.where(lane == ROUTE_W1, w1,
                                               jnp.where(lane == ROUTE_W2, w2, 0.0))))


def _router(h2, w_group, b_group, w_router, b_router, tm=1024):
    n, dp = h2.shape
    d = 2 * dp
    w = jnp.zeros((d, LANES), F32).at[:, :N_GROUPS].set(w_group).at[
        :, EXP_LANE0:EXP_LANE0 + N_EXPERTS].set(w_router).astype(BF16)
    b = jnp.zeros((1, LANES), F32).at[0, :N_GROUPS].set(b_group).at[
        0, EXP_LANE0:EXP_LANE0 + N_EXPERTS].set(b_router)
    return pl.pallas_call(
        _router_kernel,
        grid=(n // tm,),
        in_specs=[pl.BlockSpec((tm, dp), lambda i: (i, 0)),
                  pl.BlockSpec((d, LANES), lambda i: (0, 0)),
                  pl.BlockSpec((1, LANES), lambda i: (0, 0))],
        out_specs=pl.BlockSpec((tm, LANES), lambda i: (i, 0)),
        out_shape=jax.ShapeDtypeStruct((n, LANES), F32),
        compiler_params=_cparams("parallel"),
        name="moe_router",
    )(h2, w, b)


def _slots_kernel(r_ref, dest_ref, cnt_ref, run_sc, start_sc):
    ph = pl.program_id(0)
    i = pl.program_id(1)
    rec = r_ref[...]
    tm = rec.shape[0]
    lane = lax.broadcasted_iota(jnp.int32, rec.shape, 1)
    e1 = rec[:, ROUTE_E1:ROUTE_E1 + 1].astype(jnp.int32)
    e2 = rec[:, ROUTE_E2:ROUTE_E2 + 1].astype(jnp.int32)
    oh1 = lane == e1
    oh2 = lane == e2
    oh = (oh1 | oh2).astype(F32)

    @pl.when((ph == 0) & (i == 0))
    def _():
        run_sc[...] = jnp.zeros_like(run_sc)

    @pl.when(ph == 0)
    def _():
        run_sc[...] += jnp.sum(oh, axis=0, keepdims=True)

    @pl.when((ph == 1) & (i == 0))
    def _():
        counts = run_sc[...]
        cnt_ref[...] = counts
        nblk = jnp.floor((counts + (MOE_BLOCK - 1)) * (1.0 / MOE_BLOCK))
        rr = lax.broadcasted_iota(jnp.int32, (LANES, LANES), 0)
        cc = lax.broadcasted_iota(jnp.int32, (LANES, LANES), 1)
        before = (rr < cc).astype(BF16)
        first = jnp.dot(nblk.astype(BF16), before, preferred_element_type=F32)
        start_sc[...] = first * float(MOE_BLOCK)
        run_sc[...] = jnp.zeros_like(run_sc)

    @pl.when(ph == 1)
    def _():
        r = lax.broadcasted_iota(jnp.int32, (tm, tm), 0)
        c = lax.broadcasted_iota(jnp.int32, (tm, tm), 1)
        earlier = (r > c).astype(BF16)
        rank = jnp.dot(earlier, oh.astype(BF16), preferred_element_type=F32) + run_sc[...] + start_sc[...]
        d1 = jnp.sum(jnp.where(oh1, rank, 0.0), axis=1, keepdims=True)
        d2 = jnp.sum(jnp.where(oh2, rank, 0.0), axis=1, keepdims=True)
        dest_ref[...] = jnp.where(lane == 0, d1, jnp.where(lane == 1, d2, 0.0)).astype(jnp.int32)
        run_sc[...] += jnp.sum(oh, axis=0, keepdims=True)


def _slots(route, tm=512):
    n = route.shape[0]
    return pl.pallas_call(
        _slots_kernel,
        grid=(2, n // tm),
        in_specs=[pl.BlockSpec((tm, LANES), lambda p, i: (i, 0))],
        out_specs=[pl.BlockSpec((tm, LANES), lambda p, i: (i * p, 0)),
                   pl.BlockSpec((1, LANES), lambda p, i: (0, 0))],
        out_shape=[jax.ShapeDtypeStruct((n, LANES), jnp.int32), jax.ShapeDtypeStruct((1, LANES), F32)],
        scratch_shapes=[pltpu.VMEM((1, LANES), F32), pltpu.VMEM((1, LANES), F32)],
        compiler_params=_cparams("arbitrary", "arbitrary"),
        name="moe_slots",
    )(route)


DMA_UNROLL = 8


def _row_copy(src_ref, dst_ref, sem, src_row, dst_row):
    return pltpu.make_async_copy(src_ref.at[pl.ds(src_row, 1)], dst_ref.at[pl.ds(dst_row, 1)], sem)


def _dispatch_kernel(dest_ref, h_ref, xs_in_ref, xs_ref, sem):
    del xs_in_ref
    tm = h_ref.shape[0]

    def start(r, carry):
        _row_copy(h_ref, xs_ref, sem, r, dest_ref[0, 2 * r]).start(priority=0)
        _row_copy(h_ref, xs_ref, sem, r, dest_ref[0, 2 * r + 1]).start(priority=1)
        return carry

    lax.fori_loop(0, tm, start, 0, unroll=DMA_UNROLL)

    def wait(r, carry):
        _row_copy(h_ref, xs_ref, sem, 0, 0).wait()
        _row_copy(h_ref, xs_ref, sem, 0, 0).wait()
        return carry

    lax.fori_loop(0, tm, wait, 0, unroll=DMA_UNROLL)


def _dispatch(h2, dest, n_slots, tm=256):
    n, d = h2.shape
    dest3 = dest.reshape(n // tm, 1, 2 * tm)
    zeros = jnp.zeros((n_slots, d), h2.dtype)
    return pl.pallas_call(
        _dispatch_kernel,
        grid=(n // tm,),
        in_specs=[pl.BlockSpec((None, 1, 2 * tm), lambda i: (i, 0, 0), memory_space=pltpu.SMEM),
                  pl.BlockSpec((tm, d), lambda i: (i, 0)),
                  pl.BlockSpec(memory_space=pl.ANY)],
        out_specs=pl.BlockSpec(memory_space=pl.ANY),
        out_shape=jax.ShapeDtypeStruct((n_slots, d), h2.dtype),
        scratch_shapes=[pltpu.SemaphoreType.DMA(())],
        input_output_aliases={2: 0},
        compiler_params=_cparams("arbitrary"),
        name="moe_dispatch",
    )(dest3, h2, zeros)


def _experts_kernel(be_ref, first_ref, nxt_ref, par_ref, nu_ref, x_ref, w1_hbm, w3_hbm, w2_hbm, o_ref,
                    w1f, w3f, w2f, w1b, w3b, w2b, sems):
    i = pl.program_id(0)

    def weight_copies(e, slot):
        return (pltpu.make_async_copy(w1_hbm.at[e], w1f.at[slot], sems.at[0, slot]),
                pltpu.make_async_copy(w3_hbm.at[e], w3f.at[slot], sems.at[1, slot]),
                pltpu.make_async_copy(w2_hbm.at[e], w2f.at[slot], sems.at[2, slot]))

    @pl.when(i == 0)
    def _():
        for cp in weight_copies(be_ref[0], 0):
            cp.start()

    @pl.when(first_ref[i] == 1)
    def _():
        slot = par_ref[i]

        @pl.when(nxt_ref[i] >= 0)
        def _():
            for cp in weight_copies(nxt_ref[i], 1 - slot):
                cp.start()

        for cp in weight_copies(be_ref[i], slot):
            cp.wait()
        w1b[...] = w1f[slot].astype(BF16)
        w3b[...] = w3f[slot].astype(BF16)
        w2b[...] = w2f[slot].astype(BF16)

    @pl.when(i < nu_ref[0])
    def _():
        x = _unpack_bf16_pairs(x_ref[...])
        a = jnp.dot(x, w1b[...], preferred_element_type=F32)
        b = jnp.dot(x, w3b[...], preferred_element_type=F32)
        hmid = (a * jax.nn.sigmoid(a)) * b
        o_ref[...] = jnp.dot(hmid.astype(BF16), w2b[...], preferred_element_type=F32)

    @pl.when(i >= nu_ref[0])
    def _():
        o_ref[...] = jnp.zeros_like(o_ref)


def _experts(xs, block_e, n_used, w1_e, w3_e, w2_e):
    n_slots, dp = xs.shape
    nb = n_slots // MOE_BLOCK
    d, de = w1_e.shape[1], w1_e.shape[2]
    idx = jnp.arange(nb, dtype=jnp.int32)
    used = idx < n_used[0]
    first = used & ((idx == 0) | (block_e != jnp.roll(block_e, 1)))
    ordinal = jnp.cumsum(first.astype(jnp.int32)) - 1
    par = (ordinal % 2).astype(jnp.int32)
    first_pos = jnp.where(first, idx, nb)
    next_first = lax.cummin(jnp.concatenate([first_pos[1:], jnp.full((1,), nb, jnp.int32)]), reverse=True)
    nxt = jnp.where(next_first < nb, block_e[jnp.minimum(next_first, nb - 1)], -1).astype(jnp.int32)
    any_spec = pl.BlockSpec(memory_space=pl.ANY)
    grid_spec = pltpu.PrefetchScalarGridSpec(
        num_scalar_prefetch=5,
        grid=(nb,),
        in_specs=[pl.BlockSpec((MOE_BLOCK, dp), lambda i, *_: (i, 0)), any_spec, any_spec, any_spec],
        out_specs=pl.BlockSpec((MOE_BLOCK, d), lambda i, *_: (i, 0)),
        scratch_shapes=[pltpu.VMEM((2, d, de), F32), pltpu.VMEM((2, d, de), F32), pltpu.VMEM((2, de, d), F32),
                        pltpu.VMEM((d, de), BF16), pltpu.VMEM((d, de), BF16), pltpu.VMEM((de, d), BF16),
                        pltpu.SemaphoreType.DMA((3, 2))],
    )
    return pl.pallas_call(
        _experts_kernel,
        grid_spec=grid_spec,
        out_shape=jax.ShapeDtypeStruct((n_slots, d), F32),
        compiler_params=_cparams("arbitrary"),
        name="moe_experts",
    )(block_e, first.astype(jnp.int32), nxt, par, n_used, xs, w1_e, w3_e, w2_e)


def _combine_kernel(dest_ref, r_ref, x_ref, gate_ref, gf_ref, ys_ref, o_ref, buf1, buf2, sem):
    tm = x_ref.shape[0]

    def start(r, carry):
        _row_copy(ys_ref, buf1, sem, dest_ref[0, 2 * r], r).start(priority=0)
        _row_copy(ys_ref, buf2, sem, dest_ref[0, 2 * r + 1], r).start(priority=1)
        return carry

    lax.fori_loop(0, tm, start, 0, unroll=DMA_UNROLL)

    def wait(r, carry):
        _row_copy(ys_ref, buf1, sem, 0, 0).wait()
        _row_copy(ys_ref, buf2, sem, 0, 0).wait()
        return carry

    lax.fori_loop(0, tm, wait, 0, unroll=DMA_UNROLL)
    rec = r_ref[...]
    y = buf1[...] * rec[:, ROUTE_W1:ROUTE_W1 + 1] + buf2[...] * rec[:, ROUTE_W2:ROUTE_W2 + 1]
    x2 = x_ref[...] + gate_ref[...] * y
    o_ref[...] = x2 * lax.rsqrt(jnp.mean(x2 * x2, axis=-1, keepdims=True) + EPS) * gf_ref[...]


def _combine(ys, dest, route, x1, gate2, g_final, tm=256):
    bsz, L, d = x1.shape
    n = bsz * L
    tpb = L // tm
    dest3 = dest.reshape(n // tm, 1, 2 * tm)
    return pl.pallas_call(
        _combine_kernel,
        grid=(bsz, tpb),
        in_specs=[pl.BlockSpec((None, 1, 2 * tm), lambda b, i: (b * tpb + i, 0, 0), memory_space=pltpu.SMEM),
                  pl.BlockSpec((tm, LANES), lambda b, i: (b * tpb + i, 0)),
                  pl.BlockSpec((None, tm, d), lambda b, i: (b, i, 0)),
                  pl.BlockSpec((None, 1, d), lambda b, i: (b, 0, 0)),
                  pl.BlockSpec((1, d), lambda b, i: (0, 0)),
                  pl.BlockSpec(memory_space=pl.ANY)],
        out_specs=pl.BlockSpec((None, tm, d), lambda b, i: (b, i, 0)),
        out_shape=jax.ShapeDtypeStruct((bsz, L, d), F32),
        scratch_shapes=[pltpu.VMEM((tm, d), F32), pltpu.VMEM((tm, d), F32), pltpu.SemaphoreType.DMA(())],
        compiler_params=_cparams("arbitrary", "arbitrary"),
        name="moe_combine",
    )(dest3, route, x1, gate2, g_final.reshape(1, d), ys)


SLOT_TM = 512


def _slot_table_kernel(dest_ref, init_ref, tbl_ref, sem, *, plane):
    i = pl.program_id(0)

    @pl.when(i == 0)
    def _():
        cp = pltpu.make_async_copy(init_ref, tbl_ref, sem)
        cp.start()
        cp.wait()

    def body(r, carry):
        tok = i * SLOT_TM + r
        tbl_ref[MOE_BLOCK + dest_ref[0, 2 * r]] = tok
        tbl_ref[MOE_BLOCK + dest_ref[0, 2 * r + 1]] = tok + plane
        return carry

    lax.fori_loop(0, SLOT_TM, body, 0, unroll=DMA_UNROLL)


def _slot_table(dest, n, nb):
    size = (nb + 1) * MOE_BLOCK
    init = n + (jnp.arange(size, dtype=jnp.int32) % MOE_BLOCK)
    tbl = pl.pallas_call(
        functools.partial(_slot_table_kernel, plane=n + MOE_BLOCK),
        grid=(n // SLOT_TM,),
        in_specs=[pl.BlockSpec((None, 1, 2 * SLOT_TM), lambda i: (i, 0, 0), memory_space=pltpu.SMEM),
                  pl.BlockSpec(memory_space=pl.ANY)],
        out_specs=pl.BlockSpec(memory_space=pltpu.SMEM),
        out_shape=jax.ShapeDtypeStruct((size,), jnp.int32),
        scratch_shapes=[pltpu.SemaphoreType.DMA(())],
        compiler_params=_cparams("arbitrary"),
        name="moe_slot_table",
    )(dest.reshape(n // SLOT_TM, 1, 2 * SLOT_TM), init)
    return tbl.reshape(nb + 1, 1, MOE_BLOCK)


def _moe_pair_kernel(be_ref, first_ref, nxt_ref, par_ref, nu_ref,
                     t_prev, t_b0, t_b1, t_next, h_hbm, w1_hbm, w3_hbm, w2_hbm, yt_hbm,
                     x0, x1, y0, y1, w1f, w3f, w2f, w1b, w3b, w2b, wsem, gsem, ssem, *, n_tok):
    p = pl.program_id(0)
    n_used = nu_ref[0]
    plane = n_tok + MOE_BLOCK
    yt_flat = yt_hbm

    def gather(tbl, xbuf, sem):
        for r in range(MOE_BLOCK):
            a = tbl[0, r]
            tok = jnp.minimum(jnp.where(a >= plane, a - plane, a), n_tok - 1)
            _row_copy(h_hbm, xbuf, sem, tok, r).start(priority=r % 2)

    def scatter(tbl, ybuf, sem):
        for r in range(MOE_BLOCK):
            _row_copy(ybuf, yt_flat, sem, r, tbl[0, r]).start(priority=r % 2)

    def wait_rows(src, dst, sem):
        for _ in range(MOE_BLOCK):
            _row_copy(src, dst, sem, 0, 0).wait()

    def weight_copies(e, slot):
        return (pltpu.make_async_copy(w1_hbm.at[e], w1f.at[slot], wsem.at[0, slot]),
                pltpu.make_async_copy(w3_hbm.at[e], w3f.at[slot], wsem.at[1, slot]),
                pltpu.make_async_copy(w2_hbm.at[e], w2f.at[slot], wsem.at[2, slot]))

    def maybe_new_weights(b):
        @pl.when(first_ref[b] == 1)
        def _():
            slot = par_ref[b]

            @pl.when(nxt_ref[b] >= 0)
            def _():
                for cp in weight_copies(nxt_ref[b], 1 - slot):
                    cp.start()

            for cp in weight_copies(be_ref[b], slot):
                cp.wait()
            w1b[...] = w1f[slot].astype(BF16)
            w3b[...] = w3f[slot].astype(BF16)
            w2b[...] = w2f[slot].astype(BF16)

    def expert_mlp(xbuf, ybuf):
        x = xbuf[...].astype(BF16)
        a = jnp.dot(x, w1b[...], preferred_element_type=F32)
        b = jnp.dot(x, w3b[...], preferred_element_type=F32)
        hmid = (a * jax.nn.sigmoid(a)) * b
        ybuf[...] = jnp.dot(hmid.astype(BF16), w2b[...], preferred_element_type=F32)

    @pl.when(p == 0)
    def _():
        for cp in weight_copies(be_ref[0], 0):
            cp.start()
        y0[...] = jnp.zeros_like(y0)
        y1[...] = jnp.zeros_like(y1)
        gather(t_b0, x0, gsem.at[0])
        for r in range(MOE_BLOCK):
            _row_copy(y0, yt_flat, ssem.at[0], r, plane + n_tok + r).start(priority=r % 2)

    @pl.when(2 * p < n_used)
    def _():
        maybe_new_weights(2 * p)
        wait_rows(h_hbm, x0, gsem.at[0])
        wait_rows(y0, yt_flat, ssem.at[0])
        gather(t_b1, x1, gsem.at[1])
        scatter(t_prev, y1, ssem.at[1])
        expert_mlp(x0, y0)
        maybe_new_weights(2 * p + 1)
        wait_rows(h_hbm, x1, gsem.at[1])
        wait_rows(y1, yt_flat, ssem.at[1])
        gather(t_next, x0, gsem.at[0])
        scatter(t_b0, y0, ssem.at[0])
        expert_mlp(x1, y1)

        @pl.when(2 * p + 2 >= n_used)
        def _():
            scatter(t_b1, y1, ssem.at[1])
            wait_rows(y1, yt_flat, ssem.at[1])
            wait_rows(y0, yt_flat, ssem.at[0])
            wait_rows(h_hbm, x0, gsem.at[0])


def _moe_pairs(h2f, table, block_e, n_used, w1_e, w3_e, w2_e):
    n, d = h2f.shape
    nb = table.shape[0] - 1
    de = w1_e.shape[2]
    idx = jnp.arange(nb, dtype=jnp.int32)
    used = idx < n_used[0]
    first = used & ((idx == 0) | (block_e != jnp.roll(block_e, 1)))
    ordinal = jnp.cumsum(first.astype(jnp.int32)) - 1
    par = (ordinal % 2).astype(jnp.int32)
    first_pos = jnp.where(first, idx, nb)
    next_first = lax.cummin(jnp.concatenate([first_pos[1:], jnp.full((1,), nb, jnp.int32)]), reverse=True)
    nxt = jnp.where(next_first < nb, block_e[jnp.minimum(next_first, nb - 1)], -1).astype(jnp.int32)
    any_spec = pl.BlockSpec(memory_space=pl.ANY)

    def tbl(fn):
        return pl.BlockSpec((None, 1, MOE_BLOCK), lambda p, *_: (fn(p), 0, 0), memory_space=pltpu.SMEM)

    grid_spec = pltpu.PrefetchScalarGridSpec(
        num_scalar_prefetch=5,
        grid=(nb // 2,),
        in_specs=[tbl(lambda p: 2 * p), tbl(lambda p: 2 * p + 1), tbl(lambda p: 2 * p + 2),
                  tbl(lambda p: jnp.minimum(2 * p + 3, nb)), any_spec, any_spec, any_spec, any_spec],
        out_specs=any_spec,
        scratch_shapes=[pltpu.VMEM((MOE_BLOCK, d), F32), pltpu.VMEM((MOE_BLOCK, d), F32),
                        pltpu.VMEM((MOE_BLOCK, d), F32), pltpu.VMEM((MOE_BLOCK, d), F32),
                        pltpu.VMEM((2, d, de), F32), pltpu.VMEM((2, d, de), F32), pltpu.VMEM((2, de, d), F32),
                        pltpu.VMEM((d, de), BF16), pltpu.VMEM((d, de), BF16), pltpu.VMEM((de, d), BF16),
                        pltpu.SemaphoreType.DMA((3, 2)), pltpu.SemaphoreType.DMA((2,)),
                        pltpu.SemaphoreType.DMA((2,))],
    )
    return pl.pallas_call(
        functools.partial(_moe_pair_kernel, n_tok=n),
        grid_spec=grid_spec,
        out_shape=jax.ShapeDtypeStruct((2 * (n + MOE_BLOCK), d), F32),
        compiler_params=_cparams("arbitrary"),
        name="moe_experts_fused",
    )(block_e, first.astype(jnp.int32), nxt, par, n_used, table, table, table, table, h2f, w1_e, w3_e, w2_e)


def _combine_planes_kernel(r_ref, ya_ref, yb_ref, x_ref, gate_ref, gf_ref, o_ref):
    rec = r_ref[...]
    y = ya_ref[...] * rec[:, ROUTE_W1:ROUTE_W1 + 1] + yb_ref[...] * rec[:, ROUTE_W2:ROUTE_W2 + 1]
    x2 = x_ref[...] + gate_ref[...] * y
    o_ref[...] = x2 * lax.rsqrt(jnp.mean(x2 * x2, axis=-1, keepdims=True) + EPS) * gf_ref[...]


def _combine_planes(yt, route, x1, gate2, g_final, tm=512):
    bsz, L, d = x1.shape
    n = bsz * L
    tpb = L // tm
    yt3 = yt.reshape(2, n + MOE_BLOCK, d)
    return pl.pallas_call(
        _combine_planes_kernel,
        grid=(bsz, tpb),
        in_specs=[pl.BlockSpec((tm, LANES), lambda b, i: (b * tpb + i, 0)),
                  pl.BlockSpec((None, tm, d), lambda b, i: (0, b * tpb + i, 0)),
                  pl.BlockSpec((None, tm, d), lambda b, i: (1, b * tpb + i, 0)),
                  pl.BlockSpec((None, tm, d), lambda b, i: (b, i, 0)),
                  pl.BlockSpec((None, 1, d), lambda b, i: (b, 0, 0)),
                  pl.BlockSpec((1, d), lambda b, i: (0, 0))],
        out_specs=pl.BlockSpec((None, tm, d), lambda b, i: (b, i, 0)),
        out_shape=jax.ShapeDtypeStruct((bsz, L, d), F32),
        compiler_params=_cparams("parallel", "parallel"),
        name="moe_combine",
    )(route, yt3, yt3, x1, gate2, g_final.reshape(1, d))


def _moe(h2, x1, gate2, g_final, w_group, b_group, w_router, b_router, w1_e, w3_e, w2_e):
    bsz, L, d = x1.shape
    n = bsz * L
    h2f = h2.reshape(n, h2.shape[-1])
    route = _router(h2f, w_group, b_group, w_router, b_router)
    dest_rec, counts = _slots(route)
    dest = dest_rec[:, :2].reshape(2 * n)
    nb = (2 * n) // MOE_BLOCK + N_EXPERTS
    cnt = counts[0, :N_EXPERTS].astype(jnp.int32)
    blocks_per_e = (cnt + MOE_BLOCK - 1) // MOE_BLOCK
    ends = jnp.cumsum(blocks_per_e)
    block_e = jnp.clip(jnp.searchsorted(ends, jnp.arange(nb, dtype=jnp.int32), side='right'),
                       0, N_EXPERTS - 1).astype(jnp.int32)
    n_used = ends[-1:].astype(jnp.int32)
    xs = _dispatch(h2f, dest, nb * MOE_BLOCK)
    ys = _experts(xs, block_e, n_used, w1_e, w3_e, w2_e)
    return _combine(ys, dest, route, x1, gate2, g_final)


def kernel(x, c, ctx, c_ctx, w_mod, b_mod, g_norm1, g_norm2, w_in, b_in, w_qk_conv, b_qk_conv,
           w_h_conv, b_h_conv, hf_w1, hf_b1, hf_w2, hf_b2, hf_w3, hf_freq, h_bias, w_a, w_b, w_out,
           w_group, b_group, w_router, b_router, w1_e, w3_e, w2_e, g_final):
    assert w_mod.shape[0] == 1, "single-layer block"
    (w_mod, b_mod, g_norm1, g_norm2, w_in, b_in, w_qk_conv, b_qk_conv, w_h_conv, b_h_conv, hf_w1, hf_b1, hf_w2,
     hf_b2, hf_w3, hf_freq, h_bias, w_a, w_b, w_out, w_group, b_group, w_router, b_router, w1_e, w3_e, w2_e) = (
        t[0] for t in (w_mod, b_mod, g_norm1, g_norm2, w_in, b_in, w_qk_conv, b_qk_conv, w_h_conv, b_h_conv,
                       hf_w1, hf_b1, hf_w2, hf_b2, hf_w3, hf_freq, h_bias, w_a, w_b, w_out, w_group, b_group,
                       w_router, b_router, w1_e, w3_e, w2_e))
    bsz, L, d = x.shape
    lc = ctx.shape[1]
    seg = L // (L // GRID_W)
    chunk_c = min(lc, MLSTM_CHUNK)
    assert bsz + 1 <= 8 and lc % chunk_c == 0 and L % MLSTM_CHUNK == 0

    cond = jnp.zeros((8, d), F32).at[:bsz].set(c).at[bsz].set(c_ctx)
    mod = _adaln(cond, w_mod, b_mod).reshape(8, 6, d)
    modx = mod[:bsz]
    shift1, scale1, gate1, shift2, scale2, gate2 = (modx[:, i:i + 1] for i in range(6))
    shift1c = jnp.broadcast_to(mod[bsz, 0].reshape(1, 1, d), (bsz, 1, d))
    scale1c = jnp.broadcast_to(mod[bsz, 1].reshape(1, 1, d), (bsz, 1, d))

    w_in16 = w_in.astype(BF16)
    k_scale = jnp.full((M_WIDTH,), M_HEAD_DIM ** -0.5, F32)
    qk_scale = jnp.concatenate([jnp.ones((M_WIDTH,), F32), k_scale])
    w_gates, b_gates = w_in[:, IG0:M_COLS], b_in[IG0:M_COLS]

    hc = _norm_mod(ctx, g_norm1, shift1c, scale1c, lc)
    kc = _proj_conv_silu(hc, w_in16[:, K0:V0], b_in[K0:V0], w_qk_conv[:, M_WIDTH:], b_qk_conv[M_WIDTH:],
                         k_scale, lc, lc)
    vc = _proj_act(hc, w_in16[:, V0:O0], b_in[V0:O0], "none", BF16, lc)
    bcc, acc, arc = _gates(hc, w_gates, b_gates, chunk_c)
    zero_state = (jnp.zeros((bsz, 2, M_HEADS, M_HEAD_DIM, M_HEAD_DIM), F32),
                  jnp.zeros((bsz, 2, M_HEADS, 1, M_HEAD_DIM), F32),
                  jnp.zeros((bsz, 2, M_HEADS, 1, LANES), F32))
    _, ctx_state = _mlstm(None, (kc, 0), (vc, 0), bcc, acc, arc, zero_state, False, chunk_c)

    tm = 1024
    h = _norm_mod(x, g_norm1, shift1, scale1, tm)
    w_main = jnp.concatenate([w_in16[:, Q0:IG0], w_in16[:, GA0:IN_COLS]], axis=1)
    b_main = jnp.concatenate([b_in[Q0:IG0], b_in[GA0:IN_COLS]])
    pm = _proj_main(h, w_main, b_main, w_qk_conv, b_qk_conv, qk_scale, seg, tm)
    bc, ac, ar = _gates(h, w_gates, b_gates, MLSTM_CHUNK)
    hdirs, _ = _mlstm((pm, PM_Q), (pm, PM_K), (pm, PM_V), bc, ac, ar, ctx_state, True, MLSTM_CHUNK)

    x0, s = _proj_hyena(h, w_in16[:, HY0:GA0], b_in[HY0:GA0], w_h_conv, b_h_conv, seg, tm)
    y, sumsq = _hyena_long_conv(s, hf_w1, hf_b1, hf_w2, hf_b2, hf_w3, hf_freq)
    yscale = lax.rsqrt(sumsq + EPS) * (1.0 / (2 * L))

    x1, h2 = _merge(hdirs, pm, x0, s, y, x, yscale, h_bias, gate1, g_norm2, shift2, scale2,
                    w_a.astype(BF16), w_b.astype(BF16), w_out.astype(BF16))
    return _moe(h2, x1, gate2, g_final, w_group, b_group, w_router, b_router, w1_e, w3_e, w2_e)
```

```python
import functools
import math

import jax
import jax.numpy as jnp
import numpy as np
from jax import lax
from jax.experimental import pallas as pl
from jax.experimental.pallas import tpu as pltpu

F32 = jnp.float32
BF16 = jnp.bfloat16

D_MODEL = 1024
GRID_W = 64
EPS = 1e-6
M_HEADS = 4
M_HEAD_DIM = 256
M_WIDTH = M_HEADS * M_HEAD_DIM
H_WIDTH = 1024
H_POS_BANDS = 16
H_FILTER_HIDDEN = 64
H_FAST_DECAY_PCT = 0.3
H_SLOW_DECAY_PCT = 1.5
H_DECAY_TARGET = 1e-2
N_GROUPS = 8
EXPERTS_PER_GROUP = 8
N_EXPERTS = N_GROUPS * EXPERTS_PER_GROUP
D_EXPERT = 512
Q0 = 0
K0 = Q0 + M_WIDTH
V0 = K0 + M_WIDTH
O0 = V0 + M_WIDTH
IG0 = O0 + M_WIDTH
FG0 = IG0 + 2 * M_HEADS
M_COLS = FG0 + 2 * M_HEADS
HY0 = M_COLS
GA0 = HY0 + 3 * H_WIDTH
GB0 = GA0 + D_MODEL
IN_COLS = GB0 + D_MODEL

LANES = 128
MLSTM_CHUNK = 512
NEG_BIG = -1e30
VMEM_LIMIT = 48 * 1024 * 1024


def _cparams(*sem):
    return pltpu.CompilerParams(dimension_semantics=sem, vmem_limit_bytes=VMEM_LIMIT)


def _adaln_kernel(c_ref, w_ref, b_ref, o_ref):
    s = c_ref[...]
    s = s * jax.nn.sigmoid(s)
    o_ref[...] = jnp.dot(s.astype(BF16), w_ref[...].astype(BF16), preferred_element_type=F32) + b_ref[...]


def _adaln(cond, w_mod, b_mod):
    n = w_mod.shape[1]
    tn = 1536
    return pl.pallas_call(
        _adaln_kernel,
        grid=(n // tn,),
        in_specs=[pl.BlockSpec((8, D_MODEL), lambda j: (0, 0)),
                  pl.BlockSpec((D_MODEL, tn), lambda j: (0, j)),
                  pl.BlockSpec((1, tn), lambda j: (0, j))],
        out_specs=pl.BlockSpec((8, tn), lambda j: (0, j)),
        out_shape=jax.ShapeDtypeStruct((8, n), F32),
        compiler_params=_cparams("arbitrary"),
        name="adaln",
    )(cond, w_mod, b_mod.reshape(1, n))


def _norm_mod_kernel(x_ref, g_ref, sh_ref, sc_ref, o_ref):
    x = x_ref[...]
    y = x * lax.rsqrt(jnp.mean(x * x, axis=-1, keepdims=True) + EPS)
    y = y * g_ref[...]
    o_ref[...] = (y * (1.0 + sc_ref[...]) + sh_ref[...]).astype(o_ref.dtype)


def _norm_mod(x, g, shift, scale, tm):
    bsz, L, d = x.shape
    return pl.pallas_call(
        _norm_mod_kernel,
        grid=(bsz, L // tm),
        in_specs=[pl.BlockSpec((None, tm, d), lambda b, i: (b, i, 0)),
                  pl.BlockSpec((1, d), lambda b, i: (0, 0)),
                  pl.BlockSpec((None, 1, d), lambda b, i: (b, 0, 0)),
                  pl.BlockSpec((None, 1, d), lambda b, i: (b, 0, 0))],
        out_specs=pl.BlockSpec((None, tm, d), lambda b, i: (b, i, 0)),
        out_shape=jax.ShapeDtypeStruct((bsz, L, d), BF16),
        compiler_params=_cparams("parallel", "parallel"),
        name="norm_mod",
    )(x, g.reshape(1, d), shift, scale)


def _conv3(z, wc, bc, seg):
    tm = z.shape[0]
    pos = lax.broadcasted_iota(jnp.int32, z.shape, 0) & (seg - 1)
    zp = jnp.where(pos == 0, 0.0, pltpu.roll(z, 1, 0))
    zn = jnp.where(pos == seg - 1, 0.0, pltpu.roll(z, tm - 1, 0))
    return zp * wc[0:1, :] + z * wc[1:2, :] + zn * wc[2:3, :] + bc


def _proj_act_kernel(h_ref, w_ref, b_ref, o_ref, *, act):
    z = jnp.dot(h_ref[...], w_ref[...], preferred_element_type=F32) + b_ref[...]
    if act == "sigmoid":
        z = jax.nn.sigmoid(z)
    o_ref[...] = z.astype(o_ref.dtype)


def _proj_act(h, w, b, act, out_dtype, tm, tn=512):
    bsz, L, d = h.shape
    n = w.shape[1]
    return pl.pallas_call(
        functools.partial(_proj_act_kernel, act=act),
        grid=(bsz, L // tm, n // tn),
        in_specs=[pl.BlockSpec((None, tm, d), lambda b_, i, j: (b_, i, 0)),
                  pl.BlockSpec((d, tn), lambda b_, i, j: (0, j)),
                  pl.BlockSpec((1, tn), lambda b_, i, j: (0, j))],
        out_specs=pl.BlockSpec((None, tm, tn), lambda b_, i, j: (b_, i, j)),
        out_shape=jax.ShapeDtypeStruct((bsz, L, n), out_dtype),
        compiler_params=_cparams("parallel", "parallel", "arbitrary"),
        name="proj_" + act,
    )(h, w, b.reshape(1, n))


def _proj_conv_silu_kernel(h_ref, w_ref, b_ref, wc_ref, bc_ref, cs_ref, o_ref, *, seg):
    z = jnp.dot(h_ref[...], w_ref[...], preferred_element_type=F32) + b_ref[...]
    y = _conv3(z, wc_ref[...], bc_ref[...], seg)
    y = y * jax.nn.sigmoid(y)
    o_ref[...] = (y * cs_ref[...]).astype(o_ref.dtype)


def _proj_conv_silu(h, w, b, wc, bc, colscale, seg, tm, tn=512):
    bsz, L, d = h.shape
    n = w.shape[1]
    col = lambda b_, i, j: (0, j)
    return pl.pallas_call(
        functools.partial(_proj_conv_silu_kernel, seg=seg),
        grid=(bsz, L // tm, n // tn),
        in_specs=[pl.BlockSpec((None, tm, d), lambda b_, i, j: (b_, i, 0)),
                  pl.BlockSpec((d, tn), col),
                  pl.BlockSpec((1, tn), col),
                  pl.BlockSpec((3, tn), col),
                  pl.BlockSpec((1, tn), col),
                  pl.BlockSpec((1, tn), col)],
        out_specs=pl.BlockSpec((None, tm, tn), lambda b_, i, j: (b_, i, j)),
        out_shape=jax.ShapeDtypeStruct((bsz, L, n), BF16),
        compiler_params=_cparams("parallel", "parallel", "arbitrary"),
        name="proj_conv_silu",
    )(h, w, b.reshape(1, n), wc, bc.reshape(1, n), colscale.reshape(1, n))


PROJ_TN = 1024
PROJ_SUB = 512
PM_Q, PM_K, PM_V, PM_O, PM_GA, PM_GB = range(6)


def _proj_main_kernel(x_ref, g_ref, sh_ref, sc_ref, w_ref, b_ref, wc_ref, bc_ref, cs_ref, o_ref, h_ref, *, seg):
    j = pl.program_id(2)

    @pl.when(j == 0)
    def _():
        _norm_mod_kernel(x_ref, g_ref, sh_ref, sc_ref, h_ref)

    def run(epilogue):
        for c in range(PROJ_TN // PROJ_SUB):
            sl = slice(c * PROJ_SUB, (c + 1) * PROJ_SUB)
            z = jnp.dot(h_ref[...], w_ref[:, sl], preferred_element_type=F32) + b_ref[:, sl]
            o_ref[:, sl] = epilogue(z, sl).astype(o_ref.dtype)

    def conv_silu(z, sl):
        y = _conv3(z, wc_ref[:, sl], bc_ref[:, sl], seg)
        return (y * jax.nn.sigmoid(y)) * cs_ref[:, sl]

    @pl.when(j <= PM_K)
    def _():
        run(conv_silu)

    @pl.when(j == PM_V)
    def _():
        run(lambda z, sl: z)

    @pl.when(j >= PM_O)
    def _():
        run(lambda z, sl: jax.nn.sigmoid(z))


def _proj_main(x, g, shift, scale, w, b, wc, bc, colscale, seg, tm):
    bsz, L, d = x.shape
    n = w.shape[1]
    qk = lambda b_, i, j: (0, jnp.minimum(j, PM_K))
    row = pl.BlockSpec((None, tm, d), lambda b_, i, j: (b_, i, 0))
    bvec = pl.BlockSpec((None, 1, d), lambda b_, i, j: (b_, 0, 0))
    return pl.pallas_call(
        functools.partial(_proj_main_kernel, seg=seg),
        grid=(bsz, L // tm, n // PROJ_TN),
        in_specs=[row, pl.BlockSpec((1, d), lambda b_, i, j: (0, 0)), bvec, bvec,
                  pl.BlockSpec((d, PROJ_TN), lambda b_, i, j: (0, j)),
                  pl.BlockSpec((1, PROJ_TN), lambda b_, i, j: (0, j)),
                  pl.BlockSpec((3, PROJ_TN), qk),
                  pl.BlockSpec((1, PROJ_TN), qk),
                  pl.BlockSpec((1, PROJ_TN), qk)],
        out_specs=[pl.BlockSpec((None, tm, PROJ_TN), lambda b_, i, j: (b_, i, j)), row],
        out_shape=[jax.ShapeDtypeStruct((bsz, L, n), BF16), jax.ShapeDtypeStruct((bsz, L, d), BF16)],
        compiler_params=_cparams("parallel", "parallel", "arbitrary"),
        name="proj_main",
    )(x, g.reshape(1, d), shift, scale, w, b.reshape(1, n), wc, bc.reshape(1, -1), colscale.reshape(1, -1))


def _proj_hyena_kernel(h_ref, w0_ref, w1_ref, w2_ref, b_ref, wc_ref, bc_ref, x0_ref, s_ref, *, seg):
    h = h_ref[...]
    us = []
    for g, w_ref in enumerate((w0_ref, w1_ref, w2_ref)):
        z = jnp.dot(h, w_ref[...], preferred_element_type=F32) + b_ref[g]
        us.append(_conv3(z, wc_ref[g], bc_ref[g], seg))
    x0_ref[...] = us[0].astype(x0_ref.dtype)
    s_ref[...] = us[1] * us[2]


def _proj_hyena(h, w, b, wc, bc, seg, tm, tn=512):
    bsz, L, d = h.shape
    nblk = H_WIDTH // tn
    b3 = b.reshape(3, 1, H_WIDTH)
    wc3 = wc.reshape(3, 3, H_WIDTH).transpose(1, 0, 2)
    bc3 = bc.reshape(3, 1, H_WIDTH)
    out_spec = pl.BlockSpec((None, tm, tn), lambda b_, i, j: (b_, i, j))
    return pl.pallas_call(
        functools.partial(_proj_hyena_kernel, seg=seg),
        grid=(bsz, L // tm, nblk),
        in_specs=[pl.BlockSpec((None, tm, d), lambda b_, i, j: (b_, i, 0)),
                  pl.BlockSpec((d, tn), lambda b_, i, j: (0, j)),
                  pl.BlockSpec((d, tn), lambda b_, i, j: (0, nblk + j)),
                  pl.BlockSpec((d, tn), lambda b_, i, j: (0, 2 * nblk + j)),
                  pl.BlockSpec((3, 1, tn), lambda b_, i, j: (0, 0, j)),
                  pl.BlockSpec((3, 3, tn), lambda b_, i, j: (0, 0, j)),
                  pl.BlockSpec((3, 1, tn), lambda b_, i, j: (0, 0, j))],
        out_specs=[out_spec, out_spec],
        out_shape=[jax.ShapeDtypeStruct((bsz, L, H_WIDTH), BF16),
                   jax.ShapeDtypeStruct((bsz, L, H_WIDTH), F32)],
        compiler_params=_cparams("parallel", "parallel", "arbitrary"),
        name="proj_hyena",
    )(h, w, w, w, b3, wc3, bc3)


N_GATES = 4 * M_HEADS


def _split3(x):
    hi = x.astype(BF16)
    r1 = x - hi.astype(F32)
    mid = r1.astype(BF16)
    lo = (r1 - mid.astype(F32)).astype(BF16)
    return hi, mid, lo


def _log_sigmoid(x):
    return jnp.minimum(x, 0.0) - jnp.log1p(jnp.exp(-jnp.abs(x)))


def _gates_kernel(h_ref, w_ref, wt_ref, b_ref, bt_ref, bc_ref, ac_ref, ar_ref):
    h = h_ref[...]
    t = h.shape[0]
    z = jnp.dot(h, w_ref[...], preferred_element_type=F32) + b_ref[...]
    zt = lax.dot_general(wt_ref[...], h, (((1,), (1,)), ((), ())),
                         preferred_element_type=F32) + bt_ref[...]
    r = lax.broadcasted_iota(jnp.int32, (t, t), 0)
    c = lax.broadcasted_iota(jnp.int32, (t, t), 1)
    lower = (r >= c).astype(BF16)
    upper = (r <= c).astype(BF16)
    g8 = FG_LANE0

    lf = _log_sigmoid(z)
    lane = lax.broadcasted_iota(jnp.int32, z.shape, 1)
    is_fg = (lane >= g8) & (lane < 2 * g8)
    terms = [jnp.where(is_fg, p.astype(F32), 0.0) for p in _split3(lf)]
    packed = terms[0] + pltpu.roll(terms[1], 2 * g8, 1) + pltpu.roll(terms[2], 4 * g8, 1)
    cfp = jnp.dot(lower, packed.astype(BF16), preferred_element_type=F32)
    cf = cfp + pltpu.roll(cfp, LANES - 2 * g8, 1) + pltpu.roll(cfp, LANES - 4 * g8, 1)
    cb = cf[t - 1:t, :] - cf + lf
    bc = jnp.where(lane < g8 + M_HEADS, cf, cb)
    bc = pltpu.roll(bc, LANES - g8, 1)
    bc_ref[...] = bc
    ac_ref[...] = z - bc

    lft = _log_sigmoid(zt[g8:, :])
    stacked = jnp.concatenate([p.astype(F32) for p in _split3(lft)] + [jnp.zeros_like(lft)], axis=0)
    cft3 = jnp.dot(stacked.astype(BF16), upper, preferred_element_type=F32)
    cft = cft3[0:g8] + cft3[g8:2 * g8] + cft3[2 * g8:3 * g8]
    cbt = cft[:, t - 1:t] - cft + lft
    row = lax.broadcasted_iota(jnp.int32, cft.shape, 0)
    ar_ref[...] = zt[:g8, :] - jnp.where(row < M_HEADS, cft, cbt)


FG_LANE0 = 2 * M_HEADS


def _gates(h, w_g, b_g, chunk):
    bsz, L, d = h.shape
    w_pad = jnp.zeros((d, LANES), F32).at[:, :N_GATES].set(w_g).astype(BF16)
    b_pad = jnp.zeros((1, LANES), F32).at[0, :N_GATES].set(b_g)
    wt = w_g.T.astype(BF16)
    bt = b_g.reshape(N_GATES, 1)
    tok = pl.BlockSpec((None, chunk, LANES), lambda b_, i: (b_, i, 0))
    return pl.pallas_call(
        _gates_kernel,
        grid=(bsz, L // chunk),
        in_specs=[pl.BlockSpec((None, chunk, d), lambda b_, i: (b_, i, 0)),
                  pl.BlockSpec((d, LANES), lambda b_, i: (0, 0)),
                  pl.BlockSpec((N_GATES, d), lambda b_, i: (0, 0)),
                  pl.BlockSpec((1, LANES), lambda b_, i: (0, 0)),
                  pl.BlockSpec((N_GATES, 1), lambda b_, i: (0, 0))],
        out_specs=[tok, tok, pl.BlockSpec((None, FG_LANE0, chunk), lambda b_, i: (b_, 0, i))],
        out_shape=[jax.ShapeDtypeStruct((bsz, L, LANES), F32),
                   jax.ShapeDtypeStruct((bsz, L, LANES), F32),
                   jax.ShapeDtypeStruct((bsz, FG_LANE0, L), F32)],
        compiler_params=_cparams("parallel", "parallel"),
        name="mlstm_gates",
    )(h, w_pad, wt, b_pad, bt)


def _mlstm_kernel(*refs, emit_h, n_chunks):
    if emit_h:
        (q_ref, k_ref, v_ref, bc_ref, ac_ref, ar_ref, c0_ref, n0_ref, m0_ref,
         h_ref, cf_ref, nf_ref, mf_ref, c_sc, n_sc, m_sc) = refs
    else:
        (k_ref, v_ref, bc_ref, ac_ref, ar_ref, c0_ref, n0_ref, m0_ref,
         cf_ref, nf_ref, mf_ref, c_sc, n_sc, m_sc) = refs
    d = pl.program_id(1)
    j = pl.program_id(2)
    fwd = d == 0
    t = k_ref.shape[0]
    dh = M_HEAD_DIM

    @pl.when(j == 0)
    def _():
        c_sc[...] = c0_ref[...]
        n_sc[...] = n0_ref[...]
        m_sc[...] = m0_ref[...]

    r = lax.broadcasted_iota(jnp.int32, (t, t), 0)
    c = lax.broadcasted_iota(jnp.int32, (t, t), 1)
    causal = jnp.where(fwd, r - c, c - r) >= 0
    bc_all = bc_ref[...]
    ac_all = ac_ref[...]
    ar_all = ar_ref[...]
    for hd in range(M_HEADS):
        sl = slice(hd * dh, (hd + 1) * dh)
        bc = jnp.where(fwd, bc_all[:, hd:hd + 1], bc_all[:, M_HEADS + hd:M_HEADS + hd + 1])
        ac = jnp.where(fwd, ac_all[:, hd:hd + 1], ac_all[:, M_HEADS + hd:M_HEADS + hd + 1])
        ar = jnp.where(fwd, ar_all[hd:hd + 1, :], ar_all[M_HEADS + hd:M_HEADS + hd + 1, :])
        b_tot = jnp.where(fwd, bc[t - 1:t, :], bc[0:1, :])
        m_prev = m_sc[hd][:, 0:1]
        k_h = k_ref[:, sl]
        v_h = v_ref[:, sl]
        if emit_h:
            q_h = q_ref[:, sl]
            dm = jnp.where(causal, bc + ar, NEG_BIG)
            inter = bc + m_prev
            m_t = jnp.maximum(inter, jnp.max(dm, axis=1, keepdims=True))
            qk = lax.dot_general(q_h, k_h, (((1,), (1,)), ((), ())), preferred_element_type=F32)
            s = qk * jnp.exp(dm - m_t)
            carry = jnp.exp(inter - m_t)
            num = (jnp.dot(s.astype(BF16), v_h, preferred_element_type=F32)
                   + carry * jnp.dot(q_h, c_sc[hd].astype(BF16), preferred_element_type=F32))
            den = (jnp.sum(s, axis=1, keepdims=True)
                   + carry * jnp.sum(q_h.astype(F32) * n_sc[hd], axis=1, keepdims=True))
            h_ref[:, sl] = (num / jnp.maximum(jnp.abs(den), jnp.exp(-m_t))).astype(h_ref.dtype)
        g = b_tot + ac
        m_new = jnp.maximum(b_tot + m_prev, jnp.max(g, axis=0, keepdims=True))
        wgt = jnp.exp(g - m_new)
        decay = jnp.exp(b_tot + m_prev - m_new)
        kw = k_h.astype(F32) * wgt
        c_sc[hd] = decay * c_sc[hd] + lax.dot_general(kw.astype(BF16), v_h, (((0,), (0,)), ((), ())),
                                                      preferred_element_type=F32)
        n_sc[hd] = decay * n_sc[hd] + jnp.sum(kw, axis=0, keepdims=True)
        m_sc[hd] = jnp.broadcast_to(m_new, (1, LANES))

    @pl.when(j == n_chunks - 1)
    def _():
        cf_ref[...] = c_sc[...]
        nf_ref[...] = n_sc[...]
        mf_ref[...] = m_sc[...]


def _mlstm(q, k, v, bc, ac, ar, state, emit_h, t):
    bsz, L, _ = k[0].shape
    nc = L // t
    seq = lambda b_, d, j: (b_, j + d * (nc - 1 - 2 * j), 0)
    st = lambda b_, d, j: (b_, d, 0, 0, 0)

    def tok(col):
        return pl.BlockSpec((None, t, M_WIDTH), lambda b_, d, j: (b_, j + d * (nc - 1 - 2 * j), col))

    gate_spec = pl.BlockSpec((None, t, LANES), seq)
    ar_spec = pl.BlockSpec((None, FG_LANE0, t), lambda b_, d, j: (b_, 0, j + d * (nc - 1 - 2 * j)))
    c_spec = pl.BlockSpec((None, None, M_HEADS, M_HEAD_DIM, M_HEAD_DIM), st)
    n_spec = pl.BlockSpec((None, None, M_HEADS, 1, M_HEAD_DIM), st)
    m_spec = pl.BlockSpec((None, None, M_HEADS, 1, LANES), st)
    state_shapes = [jax.ShapeDtypeStruct((bsz, 2, M_HEADS, M_HEAD_DIM, M_HEAD_DIM), F32),
                    jax.ShapeDtypeStruct((bsz, 2, M_HEADS, 1, M_HEAD_DIM), F32),
                    jax.ShapeDtypeStruct((bsz, 2, M_HEADS, 1, LANES), F32)]
    in_specs = [tok(k[1]), tok(v[1]), gate_spec, gate_spec, ar_spec, c_spec, n_spec, m_spec]
    args = [k[0], v[0], bc, ac, ar, *state]
    out_specs = [c_spec, n_spec, m_spec]
    out_shape = list(state_shapes)
    if emit_h:
        in_specs = [tok(q[1])] + in_specs
        args = [q[0]] + args
        out_specs = [pl.BlockSpec((None, None, t, M_WIDTH),
                                  lambda b_, d, j: (d, b_, j + d * (nc - 1 - 2 * j), 0))] + out_specs
        out_shape = [jax.ShapeDtypeStruct((2, bsz, L, M_WIDTH), BF16)] + out_shape
    outs = pl.pallas_call(
        functools.partial(_mlstm_kernel, emit_h=emit_h, n_chunks=nc),
        grid=(bsz, 2, nc),
        in_specs=in_specs,
        out_specs=out_specs,
        out_shape=out_shape,
        scratch_shapes=[pltpu.VMEM((M_HEADS, M_HEAD_DIM, M_HEAD_DIM), F32),
                        pltpu.VMEM((M_HEADS, 1, M_HEAD_DIM), F32),
                        pltpu.VMEM((M_HEADS, 1, LANES), F32)],
        compiler_params=_cparams("parallel", "parallel", "arbitrary"),
        name="mlstm" if emit_h else "mlstm_state",
    )(*args)
    if emit_h:
        return outs[0], tuple(outs[1:])
    return None, tuple(outs)


DFT_M_TILE = 8
DFT_C_TILE = 512
FEAT_ROWS = 16


def _filter_outer_kernel(bands_ref, w1t_ref, b1_ref, w2t_ref, b2_ref, w3p_ref, w3f_ref, fr_ref, dl_ref, l_ref,
                         a_ref, ss_ref, *, L, n1, n2):
    i = pl.program_id(0)
    h = n1 // 2
    cols = DFT_M_TILE * h

    def positions(shape, axis, side):
        q = lax.broadcasted_iota(jnp.int32, shape, axis)
        mm, jj = q // h, q % h
        n = n2 * (jj + side * h) + i * DFT_M_TILE + mm
        return n, jnp.where(n < L, n, 2 * L - n).astype(F32)

    taps = []
    sumsq = jnp.zeros((1, a_ref.shape[-1]), F32)
    for side, w3_ref in ((0, w3p_ref), (1, w3f_ref)):
        _, p_row = positions((1, cols), 1, side)
        t_row = p_row / float(max(L - 1, 1))
        ang = ((2 * math.pi / L) * p_row) * bands_ref[...]
        row = lax.broadcasted_iota(jnp.int32, (FEAT_ROWS, cols), 0)
        feats = jnp.concatenate([jnp.where(row == 0, t_row, 0.0), jnp.cos(ang), -jnp.sin(ang)], axis=0)
        fr = fr_ref[...]
        hid = jnp.sin(fr * (jnp.dot(w1t_ref[...], feats.astype(BF16), preferred_element_type=F32) + b1_ref[...]))
        hid = jnp.sin(fr * (jnp.dot(w2t_ref[...], hid.astype(BF16), preferred_element_type=F32) + b2_ref[...]))
        filt = lax.dot_general(hid.astype(BF16), w3_ref[...], (((0,), (0,)), ((), ())),
                               preferred_element_type=F32)
        n_col, p_col = positions((cols, 1), 0, side)
        t_col = p_col / float(max(L - 1, 1))
        kern = filt * jnp.exp(-t_col * jnp.abs(dl_ref[...]))
        kern = jnp.where(n_col == L, 0.0, kern)
        sumsq = sumsq + jnp.sum(kern * kern, axis=0, keepdims=True)
        taps.append(kern)

    for mm in range(DFT_M_TILE):
        x = jnp.concatenate([taps[0][mm * h:(mm + 1) * h], taps[1][mm * h:(mm + 1) * h]], axis=0)
        out = jnp.dot(l_ref[...], x.astype(BF16), preferred_element_type=F32)
        a_ref[0, :, mm, :] = out[:n1]
        a_ref[1, :, mm, :] = out[n1:]

    @pl.when(i == 0)
    def _():
        ss_ref[...] = jnp.zeros_like(ss_ref)

    ss_ref[...] += sumsq


def _filter_outer(L, n1, n2, fwd_r, w1, b1, w2, b2, w3, freq):
    hid = H_FILTER_HIDDEN
    bands = jnp.linspace(1e-4, H_POS_BANDS - 1, H_POS_BANDS, dtype=F32).reshape(H_POS_BANDS, 1)
    w1t = jnp.zeros((hid, 3 * FEAT_ROWS), F32)
    w1t = w1t.at[:, 0].set(w1[0]).at[:, FEAT_ROWS:2 * FEAT_ROWS].set(w1[1:1 + H_POS_BANDS].T)
    w1t = w1t.at[:, 2 * FEAT_ROWS:].set(w1[1 + H_POS_BANDS:].T).astype(BF16)
    w3h = w3.astype(BF16)
    max_decay = math.log(H_DECAY_TARGET) / H_FAST_DECAY_PCT
    min_decay = math.log(H_DECAY_TARGET) / H_SLOW_DECAY_PCT
    deltas = jnp.linspace(min_decay, max_decay, H_WIDTH, dtype=F32).reshape(1, H_WIDTH)
    col = lambda v: v.reshape(hid, 1)
    full = lambda a: pl.BlockSpec(a.shape, lambda i: (0,) * a.ndim)
    args = [bands, w1t, col(b1), w2.T.astype(BF16), col(b2)]
    return pl.pallas_call(
        functools.partial(_filter_outer_kernel, L=L, n1=n1, n2=n2),
        grid=(n2 // DFT_M_TILE,),
        in_specs=[full(a) for a in args]
        + [pl.BlockSpec((hid, H_WIDTH), lambda i: (0, 0)), pl.BlockSpec((hid, H_WIDTH), lambda i: (0, 1)),
           full(col(freq)), full(deltas), full(fwd_r)],
        out_specs=[pl.BlockSpec((2, n1, DFT_M_TILE, H_WIDTH), lambda i: (0, 0, i, 0)),
                   pl.BlockSpec((1, H_WIDTH), lambda i: (0, 0))],
        out_shape=[jax.ShapeDtypeStruct((2, n1, n2, H_WIDTH), F32),
                   jax.ShapeDtypeStruct((1, H_WIDTH), F32)],
        compiler_params=_cparams("arbitrary"),
        name="hyena_filter_outer",
    )(*args, w3h, w3h, col(freq), deltas, fwd_r)


def _dft_factors(n):
    lg = int(round(math.log2(n)))
    n1 = 1 << ((lg + 1) // 2)
    return n1, n // n1


def _dft_outer_matrices(n1):
    k = np.arange(n1)[:, None]
    n = np.arange(n1)[None, :]
    ang = 2.0 * np.pi * ((k * n) % n1) / n1
    cr, ci = np.cos(ang), -np.sin(ang)
    h = n1 // 2
    fwd_c = np.block([[cr[:, :h], -ci[:, :h]], [ci[:, :h], cr[:, :h]]])
    fwd_r = np.concatenate([cr, ci], axis=0)
    ir, ii = cr[:h, :], -ci[:h, :]
    inv = np.block([[ir, -ii], [ii, ir]])
    return (jnp.asarray(fwd_c, F32).astype(BF16), jnp.asarray(fwd_r, F32).astype(BF16),
            jnp.asarray(inv, F32).astype(BF16))


def _dft_inner_matrices(n1, n2):
    n = n1 * n2
    k2 = np.arange(n2)[:, None]
    m = np.arange(n2)[None, :]
    ang = 2.0 * np.pi * ((k2 * m) % n2) / n2
    fr, fi = np.cos(ang), -np.sin(ang)
    f = np.block([[fr, -fi], [fi, fr]])
    k1 = jnp.arange(n1, dtype=jnp.int32)[:, None]
    tw_ang = ((jnp.arange(n2, dtype=jnp.int32)[None, :] * k1) % n).astype(F32) * (2.0 * math.pi / n)
    rep = lambda t: jnp.broadcast_to(t[:, :, None], (n1, n2, LANES))
    return (jnp.asarray(f, F32).astype(BF16), jnp.asarray(f.T, F32).astype(BF16),
            rep(jnp.cos(tw_ang)), rep(-jnp.sin(tw_ang)))


def _outer_dft_kernel(l_ref, x_ref, o_ref):
    p_in, p_out = x_ref.shape[0], o_ref.shape[0]
    r_out = o_ref.shape[1]
    for mm in range(x_ref.shape[2]):
        parts = [x_ref[p, :, mm, :] for p in range(p_in)]
        x = parts[0] if p_in == 1 else jnp.concatenate(parts, axis=0)
        out = jnp.dot(l_ref[...], x.astype(BF16), preferred_element_type=F32)
        for p in range(p_out):
            o_ref[p, :, mm, :] = out[p * r_out:(p + 1) * r_out]


def _outer_dft(lmat, x4, p_out):
    p_in, r_in, n2, c = x4.shape
    r_out = lmat.shape[0] // p_out
    tc = min(DFT_C_TILE, c)
    return pl.pallas_call(
        _outer_dft_kernel,
        grid=(n2 // DFT_M_TILE, c // tc),
        in_specs=[pl.BlockSpec(lmat.shape, lambda m, j: (0, 0)),
                  pl.BlockSpec((p_in, r_in, DFT_M_TILE, tc), lambda m, j: (0, 0, m, j))],
        out_specs=pl.BlockSpec((p_out, r_out, DFT_M_TILE, tc), lambda m, j: (0, 0, m, j)),
        out_shape=jax.ShapeDtypeStruct((p_out, r_out, n2, c), F32),
        compiler_params=_cparams("parallel", "parallel"),
        name="dft_outer",
    )(lmat, x4)


DFT_K_TILE = 2


def _twiddled_inner_dft(f_ref, twr_ref, twi_ref, a_ref, kk):
    n2, c = a_ref.shape[2], a_ref.shape[3]
    twr = jnp.tile(twr_ref[kk], (1, c // LANES))
    twi = jnp.tile(twi_ref[kk], (1, c // LANES))
    ar, ai = a_ref[0, kk], a_ref[1, kk]
    a = jnp.concatenate([(ar * twr - ai * twi).astype(BF16), (ar * twi + ai * twr).astype(BF16)], axis=0)
    x = jnp.dot(f_ref[...], a, preferred_element_type=F32)
    return x[:n2], x[n2:], twr, twi


def _inner_fwd_kernel(f_ref, twr_ref, twi_ref, a_ref, o_ref):
    for kk in range(a_ref.shape[1]):
        xr, xi, _, _ = _twiddled_inner_dft(f_ref, twr_ref, twi_ref, a_ref, kk)
        o_ref[0, kk] = xr.astype(o_ref.dtype)
        o_ref[1, kk] = xi.astype(o_ref.dtype)


def _inner_specs(n1, n2, c):
    blk = pl.BlockSpec((2, DFT_K_TILE, n2, c), lambda k: (0, k, 0, 0))
    mat = pl.BlockSpec((2 * n2, 2 * n2), lambda k: (0, 0))
    tw = pl.BlockSpec((DFT_K_TILE, n2, LANES), lambda k: (k, 0, 0))
    return blk, mat, tw


def _inner_fwd(f, twr, twi, a):
    _, n1, n2, c = a.shape
    blk, mat, tw = _inner_specs(n1, n2, c)
    return pl.pallas_call(
        _inner_fwd_kernel,
        grid=(n1 // DFT_K_TILE,),
        in_specs=[mat, tw, tw, blk],
        out_specs=blk,
        out_shape=jax.ShapeDtypeStruct((2, n1, n2, c), BF16),
        compiler_params=_cparams("parallel"),
        name="dft_inner_filter",
    )(f, twr, twi, a)


def _inner_conv_kernel(f_ref, ft_ref, twr_ref, twi_ref, a_ref, k_ref, o_ref):
    n2 = a_ref.shape[2]
    for kk in range(a_ref.shape[1]):
        xr, xi, twr, twi = _twiddled_inner_dft(f_ref, twr_ref, twi_ref, a_ref, kk)
        kr, ki = k_ref[0, kk].astype(F32), k_ref[1, kk].astype(F32)
        yr = xr * kr - xi * ki
        yi = xr * ki + xi * kr
        y = jnp.concatenate([yr.astype(BF16), yi.astype(BF16)], axis=0)
        b = jnp.dot(ft_ref[...], y, preferred_element_type=F32)
        br, bi = b[:n2], b[n2:]
        o_ref[0, kk] = br * twr + bi * twi
        o_ref[1, kk] = bi * twr - br * twi


def _inner_conv(f, ft, twr, twi, a, kf):
    _, n1, n2, c = a.shape
    blk, mat, tw = _inner_specs(n1, n2, c)
    return pl.pallas_call(
        _inner_conv_kernel,
        grid=(n1 // DFT_K_TILE,),
        in_specs=[mat, mat, tw, tw, blk, blk],
        out_specs=blk,
        out_shape=jax.ShapeDtypeStruct((2, n1, n2, c), F32),
        compiler_params=_cparams("parallel"),
        name="dft_inner_conv",
    )(f, ft, twr, twi, a, kf)


def _hyena_long_conv(s, w1, b1, w2, b2, w3, freq):
    bsz, L, c = s.shape
    assert bsz == 2
    n = 2 * L
    n1, n2 = _dft_factors(n)
    fwd_c, fwd_r, inv = _dft_outer_matrices(n1)
    f, ft, twr, twi = _dft_inner_matrices(n1, n2)
    af, sumsq = _filter_outer(L, n1, n2, fwd_r, w1, b1, w2, b2, w3, freq)
    kf = _inner_fwd(f, twr, twi, af)
    a = _outer_dft(fwd_c, s.reshape(2, n1 // 2, n2, c), 2)
    b = _inner_conv(f, ft, twr, twi, a, kf)
    y = _outer_dft(inv, b, 2)
    return y.reshape(2, L, c), sumsq


def _pack_bf16_pairs(x):
    half = x.shape[1] // 2
    lo = pltpu.bitcast(x[:, :half].astype(BF16).astype(F32), jnp.uint32) >> 16
    hi = pltpu.bitcast(x[:, half:].astype(BF16).astype(F32), jnp.uint32) & jnp.uint32(0xFFFF0000)
    return lo | hi


def _unpack_bf16_pairs(p):
    lo = pltpu.bitcast(p << 16, F32).astype(BF16)
    hi = pltpu.bitcast(p & jnp.uint32(0xFFFF0000), F32).astype(BF16)
    return jnp.concatenate([lo, hi], axis=1)


def _merge_kernel(hf_ref, hb_ref, o_ref, x0_ref, s_ref, y_ref, ga_ref, gb_ref, x_ref,
                  ysc_ref, hbias_ref, gate_ref, g2_ref, sh_ref, sc_ref,
                  wa_ref, wb_ref, wo_ref, x1_ref, h2_ref):
    a = o_ref[...].astype(F32) * (hf_ref[...].astype(F32) + hb_ref[...].astype(F32))
    s = s_ref[...]
    hy = x0_ref[...].astype(F32) * (y_ref[...] * ysc_ref[...] + hbias_ref[...] * s)
    pa = jnp.dot(a.astype(BF16), wa_ref[...], preferred_element_type=F32)
    pb = jnp.dot(hy.astype(BF16), wb_ref[...], preferred_element_type=F32)
    mix = ga_ref[...].astype(F32) * pa + gb_ref[...].astype(F32) * pb
    out = jnp.dot(mix.astype(BF16), wo_ref[...], preferred_element_type=F32)
    x1 = x_ref[...] + gate_ref[...] * out
    x1_ref[...] = x1
    y = x1 * lax.rsqrt(jnp.mean(x1 * x1, axis=-1, keepdims=True) + EPS) * g2_ref[...]
    h2_ref[...] = _pack_bf16_pairs(y * (1.0 + sc_ref[...]) + sh_ref[...])


def _merge(hdirs, pm, x0, s, y, x, yscale, h_bias, gate1, g2, shift2, scale2, w_a, w_b, w_out, tm=256):
    bsz, L, d = x.shape
    tok = pl.BlockSpec((None, tm, d), lambda b, i: (b, i, 0))

    def pm_tile(col):
        return pl.BlockSpec((None, tm, d), lambda b, i: (b, i, col))

    vec = pl.BlockSpec((1, d), lambda b, i: (0, 0))
    bvec = pl.BlockSpec((None, 1, d), lambda b, i: (b, 0, 0))
    wsp = pl.BlockSpec((d, d), lambda b, i: (0, 0))
    return pl.pallas_call(
        _merge_kernel,
        grid=(bsz, L // tm),
        in_specs=[pl.BlockSpec((None, None, tm, d), lambda b, i: (0, b, i, 0)),
                  pl.BlockSpec((None, None, tm, d), lambda b, i: (1, b, i, 0)),
                  pm_tile(PM_O), tok, tok, tok, pm_tile(PM_GA), pm_tile(PM_GB), tok,
                  vec, vec, bvec, vec, bvec, bvec, wsp, wsp, wsp],
        out_specs=[tok, pl.BlockSpec((None, tm, d // 2), lambda b, i: (b, i, 0))],
        out_shape=[jax.ShapeDtypeStruct((bsz, L, d), F32), jax.ShapeDtypeStruct((bsz, L, d // 2), jnp.uint32)],
        compiler_params=_cparams("parallel", "parallel"),
        name="merge",
    )(hdirs, hdirs, pm, x0, s, y, pm, pm, x, yscale, h_bias.reshape(1, d), gate1, g2.reshape(1, d),
      shift2, scale2, w_a, w_b, w_out)


MOE_BLOCK = 256
ROUTE_E1, ROUTE_E2, ROUTE_W1, ROUTE_W2 = 0, 1, 2, 3
EXP_LANE0 = N_GROUPS


def _first_lane_of_max(val, valid, lane):
    masked = jnp.where(valid, val, NEG_BIG)
    mx = jnp.max(masked, axis=1, keepdims=True)
    idx = jnp.min(jnp.where(valid & (masked == mx), lane, LANES), axis=1, keepdims=True)
    return mx, idx


def _router_kernel(h_ref, w_ref, b_ref, r_ref):
    logits = jnp.dot(_unpack_bf16_pairs(h_ref[...]), w_ref[...], preferred_element_type=F32) + b_ref[...]
    lane = lax.broadcasted_iota(jnp.int32, logits.shape, 1)
    is_g = lane < N_GROUPS
    gmax, gsel = _first_lane_of_max(logits, is_g, lane)
    gsum = jnp.sum(jnp.where(is_g, jnp.exp(logits - gmax), 0.0), axis=1, keepdims=True)
    gw = 1.0 / gsum
    lo = EXP_LANE0 + gsel * EXPERTS_PER_GROUP
    in_grp = (lane >= lo) & (lane < lo + EXPERTS_PER_GROUP)
    emax, l1 = _first_lane_of_max(logits, in_grp, lane)
    esum = jnp.sum(jnp.where(in_grp, jnp.exp(logits - emax), 0.0), axis=1, keepdims=True)
    e2max, l2 = _first_lane_of_max(logits, in_grp & (lane != l1), lane)
    v1 = 1.0 / esum
    v2 = jnp.exp(e2max - emax) / esum
    vs = v1 + v2
    w1 = gw * v1 / vs
    w2 = gw * v2 / vs
    e1 = (l1 - EXP_LANE0).astype(F32)
    e2 = (l2 - EXP_LANE0).astype(F32)
    r_ref[...] = jnp.where(lane == ROUTE_E1, e1,
                           jnp.where(lane == ROUTE_E2, e2,
                                     jnp.where(lane == ROUTE_W1, w1,
                                               jnp.where(lane == ROUTE_W2, w2, 0.0))))


def _router(h2, w_group, b_group, w_router, b_router, tm=1024):
    n, dp = h2.shape
    d = 2 * dp
    w = jnp.zeros((d, LANES), F32).at[:, :N_GROUPS].set(w_group).at[
        :, EXP_LANE0:EXP_LANE0 + N_EXPERTS].set(w_router).astype(BF16)
    b = jnp.zeros((1, LANES), F32).at[0, :N_GROUPS].set(b_group).at[
        0, EXP_LANE0:EXP_LANE0 + N_EXPERTS].set(b_router)
    return pl.pallas_call(
        _router_kernel,
        grid=(n // tm,),
        in_specs=[pl.BlockSpec((tm, dp), lambda i: (i, 0)),
                  pl.BlockSpec((d, LANES), lambda i: (0, 0)),
                  pl.BlockSpec((1, LANES), lambda i: (0, 0))],
        out_specs=pl.BlockSpec((tm, LANES), lambda i: (i, 0)),
        out_shape=jax.ShapeDtypeStruct((n, LANES), F32),
        compiler_params=_cparams("parallel"),
        name="moe_router",
    )(h2, w, b)


def _slots_kernel(r_ref, dest_ref, cnt_ref, run_sc, start_sc):
    ph = pl.program_id(0)
    i = pl.program_id(1)
    rec = r_ref[...]
    tm = rec.shape[0]
    lane = lax.broadcasted_iota(jnp.int32, rec.shape, 1)
    e1 = rec[:, ROUTE_E1:ROUTE_E1 + 1].astype(jnp.int32)
    e2 = rec[:, ROUTE_E2:ROUTE_E2 + 1].astype(jnp.int32)
    oh1 = lane == e1
    oh2 = lane == e2
    oh = (oh1 | oh2).astype(F32)

    @pl.when((ph == 0) & (i == 0))
    def _():
        run_sc[...] = jnp.zeros_like(run_sc)

    @pl.when(ph == 0)
    def _():
        run_sc[...] += jnp.sum(oh, axis=0, keepdims=True)

    @pl.when((ph == 1) & (i == 0))
    def _():
        counts = run_sc[...]
        cnt_ref[...] = counts
        nblk = jnp.floor((counts + (MOE_BLOCK - 1)) * (1.0 / MOE_BLOCK))
        rr = lax.broadcasted_iota(jnp.int32, (LANES, LANES), 0)
        cc = lax.broadcasted_iota(jnp.int32, (LANES, LANES), 1)
        before = (rr < cc).astype(BF16)
        first = jnp.dot(nblk.astype(BF16), before, preferred_element_type=F32)
        start_sc[...] = first * float(MOE_BLOCK)
        run_sc[...] = jnp.zeros_like(run_sc)

    @pl.when(ph == 1)
    def _():
        r = lax.broadcasted_iota(jnp.int32, (tm, tm), 0)
        c = lax.broadcasted_iota(jnp.int32, (tm, tm), 1)
        earlier = (r > c).astype(BF16)
        rank = jnp.dot(earlier, oh.astype(BF16), preferred_element_type=F32) + run_sc[...] + start_sc[...]
        d1 = jnp.sum(jnp.where(oh1, rank, 0.0), axis=1, keepdims=True)
        d2 = jnp.sum(jnp.where(oh2, rank, 0.0), axis=1, keepdims=True)
        dest_ref[...] = jnp.where(lane == 0, d1, jnp.where(lane == 1, d2, 0.0)).astype(jnp.int32)
        run_sc[...] += jnp.sum(oh, axis=0, keepdims=True)


def _slots(route, tm=512):
    n = route.shape[0]
    return pl.pallas_call(
        _slots_kernel,
        grid=(2, n // tm),
        in_specs=[pl.BlockSpec((tm, LANES), lambda p, i: (i, 0))],
        out_specs=[pl.BlockSpec((tm, LANES), lambda p, i: (i * p, 0)),
                   pl.BlockSpec((1, LANES), lambda p, i: (0, 0))],
        out_shape=[jax.ShapeDtypeStruct((n, LANES), jnp.int32), jax.ShapeDtypeStruct((1, LANES), F32)],
        scratch_shapes=[pltpu.VMEM((1, LANES), F32), pltpu.VMEM((1, LANES), F32)],
        compiler_params=_cparams("arbitrary", "arbitrary"),
        name="moe_slots",
    )(route)


DMA_UNROLL = 8


def _row_copy(src_ref, dst_ref, sem, src_row, dst_row):
    return pltpu.make_async_copy(src_ref.at[pl.ds(src_row, 1)], dst_ref.at[pl.ds(dst_row, 1)], sem)


def _dispatch_kernel(dest_ref, h_ref, xs_in_ref, xs_ref, sem):
    del xs_in_ref
    tm = h_ref.shape[0]

    def start(r, carry):
        _row_copy(h_ref, xs_ref, sem, r, dest_ref[0, 2 * r]).start(priority=0)
        _row_copy(h_ref, xs_ref, sem, r, dest_ref[0, 2 * r + 1]).start(priority=1)
        return carry

    lax.fori_loop(0, tm, start, 0, unroll=DMA_UNROLL)

    def wait(r, carry):
        _row_copy(h_ref, xs_ref, sem, 0, 0).wait()
        _row_copy(h_ref, xs_ref, sem, 0, 0).wait()
        return carry

    lax.fori_loop(0, tm, wait, 0, unroll=DMA_UNROLL)


def _dispatch(h2, dest, n_slots, tm=256):
    n, d = h2.shape
    dest3 = dest.reshape(n // tm, 1, 2 * tm)
    zeros = jnp.zeros((n_slots, d), h2.dtype)
    return pl.pallas_call(
        _dispatch_kernel,
        grid=(n // tm,),
        in_specs=[pl.BlockSpec((None, 1, 2 * tm), lambda i: (i, 0, 0), memory_space=pltpu.SMEM),
                  pl.BlockSpec((tm, d), lambda i: (i, 0)),
                  pl.BlockSpec(memory_space=pl.ANY)],
        out_specs=pl.BlockSpec(memory_space=pl.ANY),
        out_shape=jax.ShapeDtypeStruct((n_slots, d), h2.dtype),
        scratch_shapes=[pltpu.SemaphoreType.DMA(())],
        input_output_aliases={2: 0},
        compiler_params=_cparams("arbitrary"),
        name="moe_dispatch",
    )(dest3, h2, zeros)


def _experts_kernel(be_ref, first_ref, nxt_ref, par_ref, nu_ref, x_ref, w1_hbm, w3_hbm, w2_hbm, o_ref,
                    w1f, w3f, w2f, w1b, w3b, w2b, sems):
    i = pl.program_id(0)

    def weight_copies(e, slot):
        return (pltpu.make_async_copy(w1_hbm.at[e], w1f.at[slot], sems.at[0, slot]),
                pltpu.make_async_copy(w3_hbm.at[e], w3f.at[slot], sems.at[1, slot]),
                pltpu.make_async_copy(w2_hbm.at[e], w2f.at[slot], sems.at[2, slot]))

    @pl.when(i == 0)
    def _():
        for cp in weight_copies(be_ref[0], 0):
            cp.start()

    @pl.when(first_ref[i] == 1)
    def _():
        slot = par_ref[i]

        @pl.when(nxt_ref[i] >= 0)
        def _():
            for cp in weight_copies(nxt_ref[i], 1 - slot):
                cp.start()

        for cp in weight_copies(be_ref[i], slot):
            cp.wait()
        w1b[...] = w1f[slot].astype(BF16)
        w3b[...] = w3f[slot].astype(BF16)
        w2b[...] = w2f[slot].astype(BF16)

    @pl.when(i < nu_ref[0])
    def _():
        x = _unpack_bf16_pairs(x_ref[...])
        a = jnp.dot(x, w1b[...], preferred_element_type=F32)
        b = jnp.dot(x, w3b[...], preferred_element_type=F32)
        hmid = (a * jax.nn.sigmoid(a)) * b
        o_ref[...] = jnp.dot(hmid.astype(BF16), w2b[...], preferred_element_type=F32)

    @pl.when(i >= nu_ref[0])
    def _():
        o_ref[...] = jnp.zeros_like(o_ref)


def _experts(xs, block_e, n_used, w1_e, w3_e, w2_e):
    n_slots, dp = xs.shape
    nb = n_slots // MOE_BLOCK
    d, de = w1_e.shape[1], w1_e.shape[2]
    idx = jnp.arange(nb, dtype=jnp.int32)
    used = idx < n_used[0]
    first = used & ((idx == 0) | (block_e != jnp.roll(block_e, 1)))
    ordinal = jnp.cumsum(first.astype(jnp.int32)) - 1
    par = (ordinal % 2).astype(jnp.int32)
    first_pos = jnp.where(first, idx, nb)
    next_first = lax.cummin(jnp.concatenate([first_pos[1:], jnp.full((1,), nb, jnp.int32)]), reverse=True)
    nxt = jnp.where(next_first < nb, block_e[jnp.minimum(next_first, nb - 1)], -1).astype(jnp.int32)
    any_spec = pl.BlockSpec(memory_space=pl.ANY)
    grid_spec = pltpu.PrefetchScalarGridSpec(
        num_scalar_prefetch=5,
        grid=(nb,),
        in_specs=[pl.BlockSpec((MOE_BLOCK, dp), lambda i, *_: (i, 0)), any_spec, any_spec, any_spec],
        out_specs=pl.BlockSpec((MOE_BLOCK, d), lambda i, *_: (i, 0)),
        scratch_shapes=[pltpu.VMEM((2, d, de), F32), pltpu.VMEM((2, d, de), F32), pltpu.VMEM((2, de, d), F32),
                        pltpu.VMEM((d, de), BF16), pltpu.VMEM((d, de), BF16), pltpu.VMEM((de, d), BF16),
                        pltpu.SemaphoreType.DMA((3, 2))],
    )
    return pl.pallas_call(
        _experts_kernel,
        grid_spec=grid_spec,
        out_shape=jax.ShapeDtypeStruct((n_slots, d), F32),
        compiler_params=_cparams("arbitrary"),
        name="moe_experts",
    )(block_e, first.astype(jnp.int32), nxt, par, n_used, xs, w1_e, w3_e, w2_e)


def _combine_kernel(dest_ref, r_ref, x_ref, gate_ref, gf_ref, ys_ref, o_ref, buf1, buf2, sem):
    tm = x_ref.shape[0]

    def start(r, carry):
        _row_copy(ys_ref, buf1, sem, dest_ref[0, 2 * r], r).start(priority=0)
        _row_copy(ys_ref, buf2, sem, dest_ref[0, 2 * r + 1], r).start(priority=1)
        return carry

    lax.fori_loop(0, tm, start, 0, unroll=DMA_UNROLL)

    def wait(r, carry):
        _row_copy(ys_ref, buf1, sem, 0, 0).wait()
        _row_copy(ys_ref, buf2, sem, 0, 0).wait()
        return carry

    lax.fori_loop(0, tm, wait, 0, unroll=DMA_UNROLL)
    rec = r_ref[...]
    y = buf1[...] * rec[:, ROUTE_W1:ROUTE_W1 + 1] + buf2[...] * rec[:, ROUTE_W2:ROUTE_W2 + 1]
    x2 = x_ref[...] + gate_ref[...] * y
    o_ref[...] = x2 * lax.rsqrt(jnp.mean(x2 * x2, axis=-1, keepdims=True) + EPS) * gf_ref[...]


def _combine(ys, dest, route, x1, gate2, g_final, tm=256):
    bsz, L, d = x1.shape
    n = bsz * L
    tpb = L // tm
    dest3 = dest.reshape(n // tm, 1, 2 * tm)
    return pl.pallas_call(
        _combine_kernel,
        grid=(bsz, tpb),
        in_specs=[pl.BlockSpec((None, 1, 2 * tm), lambda b, i: (b * tpb + i, 0, 0), memory_space=pltpu.SMEM),
                  pl.BlockSpec((tm, LANES), lambda b, i: (b * tpb + i, 0)),
                  pl.BlockSpec((None, tm, d), lambda b, i: (b, i, 0)),
                  pl.BlockSpec((None, 1, d), lambda b, i: (b, 0, 0)),
                  pl.BlockSpec((1, d), lambda b, i: (0, 0)),
                  pl.BlockSpec(memory_space=pl.ANY)],
        out_specs=pl.BlockSpec((None, tm, d), lambda b, i: (b, i, 0)),
        out_shape=jax.ShapeDtypeStruct((bsz, L, d), F32),
        scratch_shapes=[pltpu.VMEM((tm, d), F32), pltpu.VMEM((tm, d), F32), pltpu.SemaphoreType.DMA(())],
        compiler_params=_cparams("arbitrary", "arbitrary"),
        name="moe_combine",
    )(dest3, route, x1, gate2, g_final.reshape(1, d), ys)


SLOT_TM = 512


def _slot_table_kernel(dest_ref, init_ref, tbl_ref, sem, *, plane):
    i = pl.program_id(0)

    @pl.when(i == 0)
    def _():
        cp = pltpu.make_async_copy(init_ref, tbl_ref, sem)
        cp.start()
        cp.wait()

    def body(r, carry):
        tok = i * SLOT_TM + r
        tbl_ref[MOE_BLOCK + dest_ref[0, 2 * r]] = tok
        tbl_ref[MOE_BLOCK + dest_ref[0, 2 * r + 1]] = tok + plane
        return carry

    lax.fori_loop(0, SLOT_TM, body, 0, unroll=DMA_UNROLL)


def _slot_table(dest, n, nb):
    size = (nb + 1) * MOE_BLOCK
    init = n + (jnp.arange(size, dtype=jnp.int32) % MOE_BLOCK)
    tbl = pl.pallas_call(
        functools.partial(_slot_table_kernel, plane=n + MOE_BLOCK),
        grid=(n // SLOT_TM,),
        in_specs=[pl.BlockSpec((None, 1, 2 * SLOT_TM), lambda i: (i, 0, 0), memory_space=pltpu.SMEM),
                  pl.BlockSpec(memory_space=pl.ANY)],
        out_specs=pl.BlockSpec(memory_space=pltpu.SMEM),
        out_shape=jax.ShapeDtypeStruct((size,), jnp.int32),
        scratch_shapes=[pltpu.SemaphoreType.DMA(())],
        compiler_params=_cparams("arbitrary"),
        name="moe_slot_table",
    )(dest.reshape(n // SLOT_TM, 1, 2 * SLOT_TM), init)
    return tbl.reshape(nb + 1, 1, MOE_BLOCK)


def _moe_pair_kernel(be_ref, first_ref, nxt_ref, par_ref, nu_ref,
                     t_prev, t_b0, t_b1, t_next, h_hbm, w1_hbm, w3_hbm, w2_hbm, yt_hbm,
                     x0, x1, y0, y1, w1f, w3f, w2f, w1b, w3b, w2b, wsem, gsem, ssem, *, n_tok):
    p = pl.program_id(0)
    n_used = nu_ref[0]
    plane = n_tok + MOE_BLOCK
    yt_flat = yt_hbm

    def gather(tbl, xbuf, sem):
        for r in range(MOE_BLOCK):
            a = tbl[0, r]
            tok = jnp.minimum(jnp.where(a >= plane, a - plane, a), n_tok - 1)
            _row_copy(h_hbm, xbuf, sem, tok, r).start(priority=r % 2)

    def scatter(tbl, ybuf, sem):
        for r in range(MOE_BLOCK):
            _row_copy(ybuf, yt_flat, sem, r, tbl[0, r]).start(priority=r % 2)

    def wait_rows(src, dst, sem):
        for _ in range(MOE_BLOCK):
            _row_copy(src, dst, sem, 0, 0).wait()

    def weight_copies(e, slot):
        return (pltpu.make_async_copy(w1_hbm.at[e], w1f.at[slot], wsem.at[0, slot]),
                pltpu.make_async_copy(w3_hbm.at[e], w3f.at[slot], wsem.at[1, slot]),
                pltpu.make_async_copy(w2_hbm.at[e], w2f.at[slot], wsem.at[2, slot]))

    def maybe_new_weights(b):
        @pl.when(first_ref[b] == 1)
        def _():
            slot = par_ref[b]

            @pl.when(nxt_ref[b] >= 0)
            def _():
                for cp in weight_copies(nxt_ref[b], 1 - slot):
                    cp.start()

            for cp in weight_copies(be_ref[b], slot):
                cp.wait()
            w1b[...] = w1f[slot].astype(BF16)
            w3b[...] = w3f[slot].astype(BF16)
            w2b[...] = w2f[slot].astype(BF16)

    def expert_mlp(xbuf, ybuf):
        x = xbuf[...].astype(BF16)
        a = jnp.dot(x, w1b[...], preferred_element_type=F32)
        b = jnp.dot(x, w3b[...], preferred_element_type=F32)
        hmid = (a * jax.nn.sigmoid(a)) * b
        ybuf[...] = jnp.dot(hmid.astype(BF16), w2b[...], preferred_element_type=F32)

    @pl.when(p == 0)
    def _():
        for cp in weight_copies(be_ref[0], 0):
            cp.start()
        y0[...] = jnp.zeros_like(y0)
        y1[...] = jnp.zeros_like(y1)
        gather(t_b0, x0, gsem.at[0])
        for r in range(MOE_BLOCK):
            _row_copy(y0, yt_flat, ssem.at[0], r, plane + n_tok + r).start(priority=r % 2)

    @pl.when(2 * p < n_used)
    def _():
        maybe_new_weights(2 * p)
        wait_rows(h_hbm, x0, gsem.at[0])
        wait_rows(y0, yt_flat, ssem.at[0])
        gather(t_b1, x1, gsem.at[1])
        scatter(t_prev, y1, ssem.at[1])
        expert_mlp(x0, y0)
        maybe_new_weights(2 * p + 1)
        wait_rows(h_hbm, x1, gsem.at[1])
        wait_rows(y1, yt_flat, ssem.at[1])
        gather(t_next, x0, gsem.at[0])
        scatter(t_b0, y0, ssem.at[0])
        expert_mlp(x1, y1)

        @pl.when(2 * p + 2 >= n_used)
        def _():
            scatter(t_b1, y1, ssem.at[1])
            wait_rows(y1, yt_flat, ssem.at[1])
            wait_rows(y0, yt_flat, ssem.at[0])
            wait_rows(h_hbm, x0, gsem.at[0])


def _moe_pairs(h2f, table, block_e, n_used, w1_e, w3_e, w2_e):
    n, d = h2f.shape
    nb = table.shape[0] - 1
    de = w1_e.shape[2]
    idx = jnp.arange(nb, dtype=jnp.int32)
    used = idx < n_used[0]
    first = used & ((idx == 0) | (block_e != jnp.roll(block_e, 1)))
    ordinal = jnp.cumsum(first.astype(jnp.int32)) - 1
    par = (ordinal % 2).astype(jnp.int32)
    first_pos = jnp.where(first, idx, nb)
    next_first = lax.cummin(jnp.concatenate([first_pos[1:], jnp.full((1,), nb, jnp.int32)]), reverse=True)
    nxt = jnp.where(next_first < nb, block_e[jnp.minimum(next_first, nb - 1)], -1).astype(jnp.int32)
    any_spec = pl.BlockSpec(memory_space=pl.ANY)

    def tbl(fn):
        return pl.BlockSpec((None, 1, MOE_BLOCK), lambda p, *_: (fn(p), 0, 0), memory_space=pltpu.SMEM)

    grid_spec = pltpu.PrefetchScalarGridSpec(
        num_scalar_prefetch=5,
        grid=(nb // 2,),
        in_specs=[tbl(lambda p: 2 * p), tbl(lambda p: 2 * p + 1), tbl(lambda p: 2 * p + 2),
                  tbl(lambda p: jnp.minimum(2 * p + 3, nb)), any_spec, any_spec, any_spec, any_spec],
        out_specs=any_spec,
        scratch_shapes=[pltpu.VMEM((MOE_BLOCK, d), F32), pltpu.VMEM((MOE_BLOCK, d), F32),
                        pltpu.VMEM((MOE_BLOCK, d), F32), pltpu.VMEM((MOE_BLOCK, d), F32),
                        pltpu.VMEM((2, d, de), F32), pltpu.VMEM((2, d, de), F32), pltpu.VMEM((2, de, d), F32),
                        pltpu.VMEM((d, de), BF16), pltpu.VMEM((d, de), BF16), pltpu.VMEM((de, d), BF16),
                        pltpu.SemaphoreType.DMA((3, 2)), pltpu.SemaphoreType.DMA((2,)),
                        pltpu.SemaphoreType.DMA((2,))],
    )
    return pl.pallas_call(
        functools.partial(_moe_pair_kernel, n_tok=n),
        grid_spec=grid_spec,
        out_shape=jax.ShapeDtypeStruct((2 * (n + MOE_BLOCK), d), F32),
        compiler_params=_cparams("arbitrary"),
        name="moe_experts_fused",
    )(block_e, first.astype(jnp.int32), nxt, par, n_used, table, table, table, table, h2f, w1_e, w3_e, w2_e)


def _combine_planes_kernel(r_ref, ya_ref, yb_ref, x_ref, gate_ref, gf_ref, o_ref):
    rec = r_ref[...]
    y = ya_ref[...] * rec[:, ROUTE_W1:ROUTE_W1 + 1] + yb_ref[...] * rec[:, ROUTE_W2:ROUTE_W2 + 1]
    x2 = x_ref[...] + gate_ref[...] * y
    o_ref[...] = x2 * lax.rsqrt(jnp.mean(x2 * x2, axis=-1, keepdims=True) + EPS) * gf_ref[...]


def _combine_planes(yt, route, x1, gate2, g_final, tm=512):
    bsz, L, d = x1.shape
    n = bsz * L
    tpb = L // tm
    yt3 = yt.reshape(2, n + MOE_BLOCK, d)
    return pl.pallas_call(
        _combine_planes_kernel,
        grid=(bsz, tpb),
        in_specs=[pl.BlockSpec((tm, LANES), lambda b, i: (b * tpb + i, 0)),
                  pl.BlockSpec((None, tm, d), lambda b, i: (0, b * tpb + i, 0)),
                  pl.BlockSpec((None, tm, d), lambda b, i: (1, b * tpb + i, 0)),
                  pl.BlockSpec((None, tm, d), lambda b, i: (b, i, 0)),
                  pl.BlockSpec((None, 1, d), lambda b, i: (b, 0, 0)),
                  pl.BlockSpec((1, d), lambda b, i: (0, 0))],
        out_specs=pl.BlockSpec((None, tm, d), lambda b, i: (b, i, 0)),
        out_shape=jax.ShapeDtypeStruct((bsz, L, d), F32),
        compiler_params=_cparams("parallel", "parallel"),
        name="moe_combine",
    )(route, yt3, yt3, x1, gate2, g_final.reshape(1, d))


def _moe(h2, x1, gate2, g_final, w_group, b_group, w_router, b_router, w1_e, w3_e, w2_e):
    bsz, L, d = x1.shape
    n = bsz * L
    h2f = h2.reshape(n, h2.shape[-1])
    route = _router(h2f, w_group, b_group, w_router, b_router)
    dest_rec, counts = _slots(route)
    dest = dest_rec[:, :2].reshape(2 * n)
    nb = (2 * n) // MOE_BLOCK + N_EXPERTS
    cnt = counts[0, :N_EXPERTS].astype(jnp.int32)
    blocks_per_e = (cnt + MOE_BLOCK - 1) // MOE_BLOCK
    ends = jnp.cumsum(blocks_per_e)
    block_e = jnp.clip(jnp.searchsorted(ends, jnp.arange(nb, dtype=jnp.int32), side='right'),
                       0, N_EXPERTS - 1).astype(jnp.int32)
    n_used = ends[-1:].astype(jnp.int32)
    xs = _dispatch(h2f, dest, nb * MOE_BLOCK)
    ys = _experts(xs, block_e, n_used, w1_e, w3_e, w2_e)
    return _combine(ys, dest, route, x1, gate2, g_final)


def kernel(x, c, ctx, c_ctx, w_mod, b_mod, g_norm1, g_norm2, w_in, b_in, w_qk_conv, b_qk_conv,
           w_h_conv, b_h_conv, hf_w1, hf_b1, hf_w2, hf_b2, hf_w3, hf_freq, h_bias, w_a, w_b, w_out,
           w_group, b_group, w_router, b_router, w1_e, w3_e, w2_e, g_final):
    assert w_mod.shape[0] == 1, "single-layer block"
    (w_mod, b_mod, g_norm1, g_norm2, w_in, b_in, w_qk_conv, b_qk_conv, w_h_conv, b_h_conv, hf_w1, hf_b1, hf_w2,
     hf_b2, hf_w3, hf_freq, h_bias, w_a, w_b, w_out, w_group, b_group, w_router, b_router, w1_e, w3_e, w2_e) = (
        t[0] for t in (w_mod, b_mod, g_norm1, g_norm2, w_in, b_in, w_qk_conv, b_qk_conv, w_h_conv, b_h_conv,
                       hf_w1, hf_b1, hf_w2, hf_b2, hf_w3, hf_freq, h_bias, w_a, w_b, w_out, w_group, b_group,
                       w_router, b_router, w1_e, w3_e, w2_e))
    bsz, L, d = x.shape
    lc = ctx.shape[1]
    seg = L // (L // GRID_W)
    chunk_c = min(lc, MLSTM_CHUNK)
    assert bsz + 1 <= 8 and lc % chunk_c == 0 and L % MLSTM_CHUNK == 0

    cond = jnp.zeros((8, d), F32).at[:bsz].set(c).at[bsz].set(c_ctx)
    mod = _adaln(cond, w_mod, b_mod).reshape(8, 6, d)
    modx = mod[:bsz]
    shift1, scale1, gate1, shift2, scale2, gate2 = (modx[:, i:i + 1] for i in range(6))
    shift1c = jnp.broadcast_to(mod[bsz, 0].reshape(1, 1, d), (bsz, 1, d))
    scale1c = jnp.broadcast_to(mod[bsz, 1].reshape(1, 1, d), (bsz, 1, d))

    w_in16 = w_in.astype(BF16)
    k_scale = jnp.full((M_WIDTH,), M_HEAD_DIM ** -0.5, F32)
    qk_scale = jnp.concatenate([jnp.ones((M_WIDTH,), F32), k_scale])
    w_gates, b_gates = w_in[:, IG0:M_COLS], b_in[IG0:M_COLS]

    hc = _norm_mod(ctx, g_norm1, shift1c, scale1c, lc)
    kc = _proj_conv_silu(hc, w_in16[:, K0:V0], b_in[K0:V0], w_qk_conv[:, M_WIDTH:], b_qk_conv[M_WIDTH:],
                         k_scale, lc, lc)
    vc = _proj_act(hc, w_in16[:, V0:O0], b_in[V0:O0], "none", BF16, lc)
    bcc, acc, arc = _gates(hc, w_gates, b_gates, chunk_c)
    zero_state = (jnp.zeros((bsz, 2, M_HEADS, M_HEAD_DIM, M_HEAD_DIM), F32),
                  jnp.zeros((bsz, 2, M_HEADS, 1, M_HEAD_DIM), F32),
                  jnp.zeros((bsz, 2, M_HEADS, 1, LANES), F32))
    _, ctx_state = _mlstm(None, (kc, 0), (vc, 0), bcc, acc, arc, zero_state, False, chunk_c)

    tm = 1024
    w_main = jnp.concatenate([w_in16[:, Q0:IG0], w_in16[:, GA0:IN_COLS]], axis=1)
    b_main = jnp.concatenate([b_in[Q0:IG0], b_in[GA0:IN_COLS]])
    pm, h = _proj_main(x, g_norm1, shift1, scale1, w_main, b_main, w_qk_conv, b_qk_conv, qk_scale, seg, tm)
    bc, ac, ar = _gates(h, w_gates, b_gates, MLSTM_CHUNK)
    hdirs, _ = _mlstm((pm, PM_Q), (pm, PM_K), (pm, PM_V), bc, ac, ar, ctx_state, True, MLSTM_CHUNK)

    x0, s = _proj_hyena(h, w_in16[:, HY0:GA0], b_in[HY0:GA0], w_h_conv, b_h_conv, seg, tm)
    y, sumsq = _hyena_long_conv(s, hf_w1, hf_b1, hf_w2, hf_b2, hf_w3, hf_freq)
    yscale = lax.rsqrt(sumsq + EPS) * (1.0 / (2 * L))

    x1, h2 = _merge(hdirs, pm, x0, s, y, x, yscale, h_bias, gate1, g_norm2, shift2, scale2,
                    w_a.astype(BF16), w_b.astype(BF16), w_out.astype(BF16))
    return _moe(h2, x1, gate2, g_final, w_group, b_group, w_router, b_router, w1_e, w3_e, w2_e)
```

```python
import functools
import math

import jax
import jax.numpy as jnp
import numpy as np
from jax import lax
from jax.experimental import pallas as pl
from jax.experimental.pallas import tpu as pltpu
from jax.experimental.pallas import tpu_sc as plsc

F32 = jnp.float32
BF16 = jnp.bfloat16

D_MODEL = 1024
GRID_W = 64
EPS = 1e-6
M_HEADS = 4
M_HEAD_DIM = 256
M_WIDTH = M_HEADS * M_HEAD_DIM
H_WIDTH = 1024
H_POS_BANDS = 16
H_FILTER_HIDDEN = 64
H_FAST_DECAY_PCT = 0.3
H_SLOW_DECAY_PCT = 1.5
H_DECAY_TARGET = 1e-2
N_GROUPS = 8
EXPERTS_PER_GROUP = 8
N_EXPERTS = N_GROUPS * EXPERTS_PER_GROUP
D_EXPERT = 512
Q0 = 0
K0 = Q0 + M_WIDTH
V0 = K0 + M_WIDTH
O0 = V0 + M_WIDTH
IG0 = O0 + M_WIDTH
FG0 = IG0 + 2 * M_HEADS
M_COLS = FG0 + 2 * M_HEADS
HY0 = M_COLS
GA0 = HY0 + 3 * H_WIDTH
GB0 = GA0 + D_MODEL
IN_COLS = GB0 + D_MODEL

LANES = 128
MLSTM_CHUNK = 512
NEG_BIG = -1e30
VMEM_LIMIT = 48 * 1024 * 1024


def _cparams(*sem):
    return pltpu.CompilerParams(dimension_semantics=sem, vmem_limit_bytes=VMEM_LIMIT)


def _adaln_kernel(c_ref, w_ref, b_ref, o_ref):
    s = c_ref[...]
    s = s * jax.nn.sigmoid(s)
    o_ref[...] = jnp.dot(s.astype(BF16), w_ref[...].astype(BF16), preferred_element_type=F32) + b_ref[...]


def _adaln(cond, w_mod, b_mod):
    n = w_mod.shape[1]
    tn = 1536
    return pl.pallas_call(
        _adaln_kernel,
        grid=(n // tn,),
        in_specs=[pl.BlockSpec((8, D_MODEL), lambda j: (0, 0)),
                  pl.BlockSpec((D_MODEL, tn), lambda j: (0, j)),
                  pl.BlockSpec((1, tn), lambda j: (0, j))],
        out_specs=pl.BlockSpec((8, tn), lambda j: (0, j)),
        out_shape=jax.ShapeDtypeStruct((8, n), F32),
        compiler_params=_cparams("arbitrary"),
        name="adaln",
    )(cond, w_mod, b_mod.reshape(1, n))


def _norm_mod_kernel(x_ref, g_ref, sh_ref, sc_ref, o_ref):
    x = x_ref[...]
    y = x * lax.rsqrt(jnp.mean(x * x, axis=-1, keepdims=True) + EPS)
    y = y * g_ref[...]
    o_ref[...] = (y * (1.0 + sc_ref[...]) + sh_ref[...]).astype(o_ref.dtype)


def _norm_mod(x, g, shift, scale, tm):
    bsz, L, d = x.shape
    return pl.pallas_call(
        _norm_mod_kernel,
        grid=(bsz, L // tm),
        in_specs=[pl.BlockSpec((None, tm, d), lambda b, i: (b, i, 0)),
                  pl.BlockSpec((1, d), lambda b, i: (0, 0)),
                  pl.BlockSpec((None, 1, d), lambda b, i: (b, 0, 0)),
                  pl.BlockSpec((None, 1, d), lambda b, i: (b, 0, 0))],
        out_specs=pl.BlockSpec((None, tm, d), lambda b, i: (b, i, 0)),
        out_shape=jax.ShapeDtypeStruct((bsz, L, d), BF16),
        compiler_params=_cparams("parallel", "parallel"),
        name="norm_mod",
    )(x, g.reshape(1, d), shift, scale)


def _conv3(z, wc, bc, seg):
    tm = z.shape[0]
    pos = lax.broadcasted_iota(jnp.int32, z.shape, 0) & (seg - 1)
    zp = jnp.where(pos == 0, 0.0, pltpu.roll(z, 1, 0))
    zn = jnp.where(pos == seg - 1, 0.0, pltpu.roll(z, tm - 1, 0))
    return zp * wc[0:1, :] + z * wc[1:2, :] + zn * wc[2:3, :] + bc


def _proj_act_kernel(h_ref, w_ref, b_ref, o_ref, *, act):
    z = jnp.dot(h_ref[...], w_ref[...], preferred_element_type=F32) + b_ref[...]
    if act == "sigmoid":
        z = jax.nn.sigmoid(z)
    o_ref[...] = z.astype(o_ref.dtype)


def _proj_act(h, w, b, act, out_dtype, tm, tn=512):
    bsz, L, d = h.shape
    n = w.shape[1]
    return pl.pallas_call(
        functools.partial(_proj_act_kernel, act=act),
        grid=(bsz, L // tm, n // tn),
        in_specs=[pl.BlockSpec((None, tm, d), lambda b_, i, j: (b_, i, 0)),
                  pl.BlockSpec((d, tn), lambda b_, i, j: (0, j)),
                  pl.BlockSpec((1, tn), lambda b_, i, j: (0, j))],
        out_specs=pl.BlockSpec((None, tm, tn), lambda b_, i, j: (b_, i, j)),
        out_shape=jax.ShapeDtypeStruct((bsz, L, n), out_dtype),
        compiler_params=_cparams("parallel", "parallel", "arbitrary"),
        name="proj_" + act,
    )(h, w, b.reshape(1, n))


def _proj_conv_silu_kernel(h_ref, w_ref, b_ref, wc_ref, bc_ref, cs_ref, o_ref, *, seg):
    z = jnp.dot(h_ref[...], w_ref[...], preferred_element_type=F32) + b_ref[...]
    y = _conv3(z, wc_ref[...], bc_ref[...], seg)
    y = y * jax.nn.sigmoid(y)
    o_ref[...] = (y * cs_ref[...]).astype(o_ref.dtype)


def _proj_conv_silu(h, w, b, wc, bc, colscale, seg, tm, tn=512):
    bsz, L, d = h.shape
    n = w.shape[1]
    col = lambda b_, i, j: (0, j)
    return pl.pallas_call(
        functools.partial(_proj_conv_silu_kernel, seg=seg),
        grid=(bsz, L // tm, n // tn),
        in_specs=[pl.BlockSpec((None, tm, d), lambda b_, i, j: (b_, i, 0)),
                  pl.BlockSpec((d, tn), col),
                  pl.BlockSpec((1, tn), col),
                  pl.BlockSpec((3, tn), col),
                  pl.BlockSpec((1, tn), col),
                  pl.BlockSpec((1, tn), col)],
        out_specs=pl.BlockSpec((None, tm, tn), lambda b_, i, j: (b_, i, j)),
        out_shape=jax.ShapeDtypeStruct((bsz, L, n), BF16),
        compiler_params=_cparams("parallel", "parallel", "arbitrary"),
        name="proj_conv_silu",
    )(h, w, b.reshape(1, n), wc, bc.reshape(1, n), colscale.reshape(1, n))


PROJ_TN = 1024
PROJ_SUB = 512
PM_Q, PM_K, PM_V, PM_O, PM_GA, PM_GB = range(6)


def _proj_main_kernel(x_ref, g_ref, sh_ref, sc_ref, w_ref, b_ref, wc_ref, bc_ref, cs_ref, o_ref, h_ref, *, seg):
    j = pl.program_id(2)

    @pl.when(j == 0)
    def _():
        _norm_mod_kernel(x_ref, g_ref, sh_ref, sc_ref, h_ref)

    def run(epilogue):
        for c in range(PROJ_TN // PROJ_SUB):
            sl = slice(c * PROJ_SUB, (c + 1) * PROJ_SUB)
            z = jnp.dot(h_ref[...], w_ref[:, sl], preferred_element_type=F32) + b_ref[:, sl]
            o_ref[:, sl] = epilogue(z, sl).astype(o_ref.dtype)

    def conv_silu(z, sl):
        y = _conv3(z, wc_ref[:, sl], bc_ref[:, sl], seg)
        return (y * jax.nn.sigmoid(y)) * cs_ref[:, sl]

    @pl.when(j <= PM_K)
    def _():
        run(conv_silu)

    @pl.when(j == PM_V)
    def _():
        run(lambda z, sl: z)

    @pl.when(j >= PM_O)
    def _():
        run(lambda z, sl: jax.nn.sigmoid(z))


def _proj_main(x, g, shift, scale, w, b, wc, bc, colscale, seg, tm):
    bsz, L, d = x.shape
    n = w.shape[1]
    qk = lambda b_, i, j: (0, jnp.minimum(j, PM_K))
    row = pl.BlockSpec((None, tm, d), lambda b_, i, j: (b_, i, 0))
    bvec = pl.BlockSpec((None, 1, d), lambda b_, i, j: (b_, 0, 0))
    return pl.pallas_call(
        functools.partial(_proj_main_kernel, seg=seg),
        grid=(bsz, L // tm, n // PROJ_TN),
        in_specs=[row, pl.BlockSpec((1, d), lambda b_, i, j: (0, 0)), bvec, bvec,
                  pl.BlockSpec((d, PROJ_TN), lambda b_, i, j: (0, j)),
                  pl.BlockSpec((1, PROJ_TN), lambda b_, i, j: (0, j)),
                  pl.BlockSpec((3, PROJ_TN), qk),
                  pl.BlockSpec((1, PROJ_TN), qk),
                  pl.BlockSpec((1, PROJ_TN), qk)],
        out_specs=[pl.BlockSpec((None, tm, PROJ_TN), lambda b_, i, j: (b_, i, j)), row],
        out_shape=[jax.ShapeDtypeStruct((bsz, L, n), BF16), jax.ShapeDtypeStruct((bsz, L, d), BF16)],
        compiler_params=_cparams("parallel", "parallel", "arbitrary"),
        name="proj_main",
    )(x, g.reshape(1, d), shift, scale, w, b.reshape(1, n), wc, bc.reshape(1, -1), colscale.reshape(1, -1))


def _proj_hyena_kernel(h_ref, w0_ref, w1_ref, w2_ref, b_ref, wc_ref, bc_ref, x0_ref, s_ref, *, seg):
    h = h_ref[...]
    us = []
    for g, w_ref in enumerate((w0_ref, w1_ref, w2_ref)):
        z = jnp.dot(h, w_ref[...], preferred_element_type=F32) + b_ref[g]
        us.append(_conv3(z, wc_ref[g], bc_ref[g], seg))
    x0_ref[...] = us[0].astype(x0_ref.dtype)
    s_ref[...] = us[1] * us[2]


def _proj_hyena(h, w, b, wc, bc, seg, tm, tn=512):
    bsz, L, d = h.shape
    nblk = H_WIDTH // tn
    b3 = b.reshape(3, 1, H_WIDTH)
    wc3 = wc.reshape(3, 3, H_WIDTH).transpose(1, 0, 2)
    bc3 = bc.reshape(3, 1, H_WIDTH)
    out_spec = pl.BlockSpec((None, tm, tn), lambda b_, i, j: (b_, i, j))
    return pl.pallas_call(
        functools.partial(_proj_hyena_kernel, seg=seg),
        grid=(bsz, L // tm, nblk),
        in_specs=[pl.BlockSpec((None, tm, d), lambda b_, i, j: (b_, i, 0)),
                  pl.BlockSpec((d, tn), lambda b_, i, j: (0, j)),
                  pl.BlockSpec((d, tn), lambda b_, i, j: (0, nblk + j)),
                  pl.BlockSpec((d, tn), lambda b_, i, j: (0, 2 * nblk + j)),
                  pl.BlockSpec((3, 1, tn), lambda b_, i, j: (0, 0, j)),
                  pl.BlockSpec((3, 3, tn), lambda b_, i, j: (0, 0, j)),
                  pl.BlockSpec((3, 1, tn), lambda b_, i, j: (0, 0, j))],
        out_specs=[out_spec, out_spec],
        out_shape=[jax.ShapeDtypeStruct((bsz, L, H_WIDTH), BF16),
                   jax.ShapeDtypeStruct((bsz, L, H_WIDTH), F32)],
        compiler_params=_cparams("parallel", "parallel", "arbitrary"),
        name="proj_hyena",
    )(h, w, w, w, b3, wc3, bc3)


N_GATES = 4 * M_HEADS


def _split3(x):
    hi = x.astype(BF16)
    r1 = x - hi.astype(F32)
    mid = r1.astype(BF16)
    lo = (r1 - mid.astype(F32)).astype(BF16)
    return hi, mid, lo


def _log_sigmoid(x):
    return jnp.minimum(x, 0.0) - jnp.log1p(jnp.exp(-jnp.abs(x)))


def _gates_kernel(h_ref, w_ref, wt_ref, b_ref, bt_ref, bc_ref, ac_ref, ar_ref):
    h = h_ref[...]
    t = h.shape[0]
    z = jnp.dot(h, w_ref[...], preferred_element_type=F32) + b_ref[...]
    zt = lax.dot_general(wt_ref[...], h, (((1,), (1,)), ((), ())),
                         preferred_element_type=F32) + bt_ref[...]
    r = lax.broadcasted_iota(jnp.int32, (t, t), 0)
    c = lax.broadcasted_iota(jnp.int32, (t, t), 1)
    lower = (r >= c).astype(BF16)
    upper = (r <= c).astype(BF16)
    g8 = FG_LANE0

    lf = _log_sigmoid(z)
    lane = lax.broadcasted_iota(jnp.int32, z.shape, 1)
    is_fg = (lane >= g8) & (lane < 2 * g8)
    terms = [jnp.where(is_fg, p.astype(F32), 0.0) for p in _split3(lf)]
    packed = terms[0] + pltpu.roll(terms[1], 2 * g8, 1) + pltpu.roll(terms[2], 4 * g8, 1)
    cfp = jnp.dot(lower, packed.astype(BF16), preferred_element_type=F32)
    cf = cfp + pltpu.roll(cfp, LANES - 2 * g8, 1) + pltpu.roll(cfp, LANES - 4 * g8, 1)
    cb = cf[t - 1:t, :] - cf + lf
    bc = jnp.where(lane < g8 + M_HEADS, cf, cb)
    bc = pltpu.roll(bc, LANES - g8, 1)
    bc_ref[...] = bc
    ac_ref[...] = z - bc

    lft = _log_sigmoid(zt[g8:, :])
    stacked = jnp.concatenate([p.astype(F32) for p in _split3(lft)] + [jnp.zeros_like(lft)], axis=0)
    cft3 = jnp.dot(stacked.astype(BF16), upper, preferred_element_type=F32)
    cft = cft3[0:g8] + cft3[g8:2 * g8] + cft3[2 * g8:3 * g8]
    cbt = cft[:, t - 1:t] - cft + lft
    row = lax.broadcasted_iota(jnp.int32, cft.shape, 0)
    ar_ref[...] = zt[:g8, :] - jnp.where(row < M_HEADS, cft, cbt)


FG_LANE0 = 2 * M_HEADS


def _gates(h, w_g, b_g, chunk):
    bsz, L, d = h.shape
    w_pad = jnp.zeros((d, LANES), F32).at[:, :N_GATES].set(w_g).astype(BF16)
    b_pad = jnp.zeros((1, LANES), F32).at[0, :N_GATES].set(b_g)
    wt = w_g.T.astype(BF16)
    bt = b_g.reshape(N_GATES, 1)
    tok = pl.BlockSpec((None, chunk, LANES), lambda b_, i: (b_, i, 0))
    return pl.pallas_call(
        _gates_kernel,
        grid=(bsz, L // chunk),
        in_specs=[pl.BlockSpec((None, chunk, d), lambda b_, i: (b_, i, 0)),
                  pl.BlockSpec((d, LANES), lambda b_, i: (0, 0)),
                  pl.BlockSpec((N_GATES, d), lambda b_, i: (0, 0)),
                  pl.BlockSpec((1, LANES), lambda b_, i: (0, 0)),
                  pl.BlockSpec((N_GATES, 1), lambda b_, i: (0, 0))],
        out_specs=[tok, tok, pl.BlockSpec((None, FG_LANE0, chunk), lambda b_, i: (b_, 0, i))],
        out_shape=[jax.ShapeDtypeStruct((bsz, L, LANES), F32),
                   jax.ShapeDtypeStruct((bsz, L, LANES), F32),
                   jax.ShapeDtypeStruct((bsz, FG_LANE0, L), F32)],
        compiler_params=_cparams("parallel", "parallel"),
        name="mlstm_gates",
    )(h, w_pad, wt, b_pad, bt)


def _mlstm_kernel(*refs, emit_h, n_chunks):
    if emit_h:
        (q_ref, k_ref, v_ref, bc_ref, ac_ref, ar_ref, c0_ref, n0_ref, m0_ref,
         h_ref, cf_ref, nf_ref, mf_ref, c_sc, n_sc, m_sc) = refs
    else:
        (k_ref, v_ref, bc_ref, ac_ref, ar_ref, c0_ref, n0_ref, m0_ref,
         cf_ref, nf_ref, mf_ref, c_sc, n_sc, m_sc) = refs
    d = pl.program_id(1)
    j = pl.program_id(2)
    fwd = d == 0
    t = k_ref.shape[0]
    dh = M_HEAD_DIM

    @pl.when(j == 0)
    def _():
        c_sc[...] = c0_ref[...]
        n_sc[...] = n0_ref[...]
        m_sc[...] = m0_ref[...]

    r = lax.broadcasted_iota(jnp.int32, (t, t), 0)
    c = lax.broadcasted_iota(jnp.int32, (t, t), 1)
    causal = jnp.where(fwd, r - c, c - r) >= 0
    bc_all = bc_ref[...]
    ac_all = ac_ref[...]
    ar_all = ar_ref[...]
    for hd in range(M_HEADS):
        sl = slice(hd * dh, (hd + 1) * dh)
        bc = jnp.where(fwd, bc_all[:, hd:hd + 1], bc_all[:, M_HEADS + hd:M_HEADS + hd + 1])
        ac = jnp.where(fwd, ac_all[:, hd:hd + 1], ac_all[:, M_HEADS + hd:M_HEADS + hd + 1])
        ar = jnp.where(fwd, ar_all[hd:hd + 1, :], ar_all[M_HEADS + hd:M_HEADS + hd + 1, :])
        b_tot = jnp.where(fwd, bc[t - 1:t, :], bc[0:1, :])
        m_prev = m_sc[hd][:, 0:1]
        k_h = k_ref[:, sl]
        v_h = v_ref[:, sl]
        if emit_h:
            q_h = q_ref[:, sl]
            dm = jnp.where(causal, bc + ar, NEG_BIG)
            inter = bc + m_prev
            m_t = jnp.maximum(inter, jnp.max(dm, axis=1, keepdims=True))
            qk = lax.dot_general(q_h, k_h, (((1,), (1,)), ((), ())), preferred_element_type=F32)
            s = qk * jnp.exp(dm - m_t)
            carry = jnp.exp(inter - m_t)
            num = (jnp.dot(s.astype(BF16), v_h, preferred_element_type=F32)
                   + carry * jnp.dot(q_h, c_sc[hd].astype(BF16), preferred_element_type=F32))
            den = (jnp.sum(s, axis=1, keepdims=True)
                   + carry * jnp.sum(q_h.astype(F32) * n_sc[hd], axis=1, keepdims=True))
            h_ref[:, sl] = (num / jnp.maximum(jnp.abs(den), jnp.exp(-m_t))).astype(h_ref.dtype)
        g = b_tot + ac
        m_new = jnp.maximum(b_tot + m_prev, jnp.max(g, axis=0, keepdims=True))
        wgt = jnp.exp(g - m_new)
        decay = jnp.exp(b_tot + m_prev - m_new)
        kw = k_h.astype(F32) * wgt
        c_sc[hd] = decay * c_sc[hd] + lax.dot_general(kw.astype(BF16), v_h, (((0,), (0,)), ((), ())),
                                                      preferred_element_type=F32)
        n_sc[hd] = decay * n_sc[hd] + jnp.sum(kw, axis=0, keepdims=True)
        m_sc[hd] = jnp.broadcast_to(m_new, (1, LANES))

    @pl.when(j == n_chunks - 1)
    def _():
        cf_ref[...] = c_sc[...]
        nf_ref[...] = n_sc[...]
        mf_ref[...] = m_sc[...]


def _mlstm(q, k, v, bc, ac, ar, state, emit_h, t):
    bsz, L, _ = k[0].shape
    nc = L // t
    seq = lambda b_, d, j: (b_, j + d * (nc - 1 - 2 * j), 0)
    st = lambda b_, d, j: (b_, d, 0, 0, 0)

    def tok(col):
        return pl.BlockSpec((None, t, M_WIDTH), lambda b_, d, j: (b_, j + d * (nc - 1 - 2 * j), col))

    gate_spec = pl.BlockSpec((None, t, LANES), seq)
    ar_spec = pl.BlockSpec((None, FG_LANE0, t), lambda b_, d, j: (b_, 0, j + d * (nc - 1 - 2 * j)))
    c_spec = pl.BlockSpec((None, None, M_HEADS, M_HEAD_DIM, M_HEAD_DIM), st)
    n_spec = pl.BlockSpec((None, None, M_HEADS, 1, M_HEAD_DIM), st)
    m_spec = pl.BlockSpec((None, None, M_HEADS, 1, LANES), st)
    state_shapes = [jax.ShapeDtypeStruct((bsz, 2, M_HEADS, M_HEAD_DIM, M_HEAD_DIM), F32),
                    jax.ShapeDtypeStruct((bsz, 2, M_HEADS, 1, M_HEAD_DIM), F32),
                    jax.ShapeDtypeStruct((bsz, 2, M_HEADS, 1, LANES), F32)]
    in_specs = [tok(k[1]), tok(v[1]), gate_spec, gate_spec, ar_spec, c_spec, n_spec, m_spec]
    args = [k[0], v[0], bc, ac, ar, *state]
    out_specs = [c_spec, n_spec, m_spec]
    out_shape = list(state_shapes)
    if emit_h:
        in_specs = [tok(q[1])] + in_specs
        args = [q[0]] + args
        out_specs = [pl.BlockSpec((None, None, t, M_WIDTH),
                                  lambda b_, d, j: (d, b_, j + d * (nc - 1 - 2 * j), 0))] + out_specs
        out_shape = [jax.ShapeDtypeStruct((2, bsz, L, M_WIDTH), BF16)] + out_shape
    outs = pl.pallas_call(
        functools.partial(_mlstm_kernel, emit_h=emit_h, n_chunks=nc),
        grid=(bsz, 2, nc),
        in_specs=in_specs,
        out_specs=out_specs,
        out_shape=out_shape,
        scratch_shapes=[pltpu.VMEM((M_HEADS, M_HEAD_DIM, M_HEAD_DIM), F32),
                        pltpu.VMEM((M_HEADS, 1, M_HEAD_DIM), F32),
                        pltpu.VMEM((M_HEADS, 1, LANES), F32)],
        compiler_params=_cparams("parallel", "parallel", "arbitrary"),
        name="mlstm" if emit_h else "mlstm_state",
    )(*args)
    if emit_h:
        return outs[0], tuple(outs[1:])
    return None, tuple(outs)


DFT_M_TILE = 8
DFT_C_TILE = 512
FEAT_ROWS = 16


def _filter_outer_kernel(bands_ref, w1t_ref, b1_ref, w2t_ref, b2_ref, w3p_ref, w3f_ref, fr_ref, dl_ref, l_ref,
                         a_ref, ss_ref, *, L, n1, n2):
    i = pl.program_id(0)
    h = n1 // 2
    cols = DFT_M_TILE * h

    def positions(shape, axis, side):
        q = lax.broadcasted_iota(jnp.int32, shape, axis)
        mm, jj = q // h, q % h
        n = n2 * (jj + side * h) + i * DFT_M_TILE + mm
        return n, jnp.where(n < L, n, 2 * L - n).astype(F32)

    taps = []
    sumsq = jnp.zeros((1, a_ref.shape[-1]), F32)
    for side, w3_ref in ((0, w3p_ref), (1, w3f_ref)):
        _, p_row = positions((1, cols), 1, side)
        t_row = p_row / float(max(L - 1, 1))
        ang = ((2 * math.pi / L) * p_row) * bands_ref[...]
        row = lax.broadcasted_iota(jnp.int32, (FEAT_ROWS, cols), 0)
        feats = jnp.concatenate([jnp.where(row == 0, t_row, 0.0), jnp.cos(ang), -jnp.sin(ang)], axis=0)
        fr = fr_ref[...]
        hid = jnp.sin(fr * (jnp.dot(w1t_ref[...], feats.astype(BF16), preferred_element_type=F32) + b1_ref[...]))
        hid = jnp.sin(fr * (jnp.dot(w2t_ref[...], hid.astype(BF16), preferred_element_type=F32) + b2_ref[...]))
        filt = lax.dot_general(hid.astype(BF16), w3_ref[...], (((0,), (0,)), ((), ())),
                               preferred_element_type=F32)
        n_col, p_col = positions((cols, 1), 0, side)
        t_col = p_col / float(max(L - 1, 1))
        kern = filt * jnp.exp(-t_col * jnp.abs(dl_ref[...]))
        kern = jnp.where(n_col == L, 0.0, kern)
        sumsq = sumsq + jnp.sum(kern * kern, axis=0, keepdims=True)
        taps.append(kern)

    for mm in range(DFT_M_TILE):
        x = jnp.concatenate([taps[0][mm * h:(mm + 1) * h], taps[1][mm * h:(mm + 1) * h]], axis=0)
        out = jnp.dot(l_ref[...], x.astype(BF16), preferred_element_type=F32)
        a_ref[0, :, mm, :] = out[:n1]
        a_ref[1, :, mm, :] = out[n1:]

    @pl.when(i == 0)
    def _():
        ss_ref[...] = jnp.zeros_like(ss_ref)

    ss_ref[...] += sumsq


def _filter_outer(L, n1, n2, fwd_r, w1, b1, w2, b2, w3, freq):
    hid = H_FILTER_HIDDEN
    bands = jnp.linspace(1e-4, H_POS_BANDS - 1, H_POS_BANDS, dtype=F32).reshape(H_POS_BANDS, 1)
    w1t = jnp.zeros((hid, 3 * FEAT_ROWS), F32)
    w1t = w1t.at[:, 0].set(w1[0]).at[:, FEAT_ROWS:2 * FEAT_ROWS].set(w1[1:1 + H_POS_BANDS].T)
    w1t = w1t.at[:, 2 * FEAT_ROWS:].set(w1[1 + H_POS_BANDS:].T).astype(BF16)
    w3h = w3.astype(BF16)
    max_decay = math.log(H_DECAY_TARGET) / H_FAST_DECAY_PCT
    min_decay = math.log(H_DECAY_TARGET) / H_SLOW_DECAY_PCT
    deltas = jnp.linspace(min_decay, max_decay, H_WIDTH, dtype=F32).reshape(1, H_WIDTH)
    col = lambda v: v.reshape(hid, 1)
    full = lambda a: pl.BlockSpec(a.shape, lambda i: (0,) * a.ndim)
    args = [bands, w1t, col(b1), w2.T.astype(BF16), col(b2)]
    return pl.pallas_call(
        functools.partial(_filter_outer_kernel, L=L, n1=n1, n2=n2),
        grid=(n2 // DFT_M_TILE,),
        in_specs=[full(a) for a in args]
        + [pl.BlockSpec((hid, H_WIDTH), lambda i: (0, 0)), pl.BlockSpec((hid, H_WIDTH), lambda i: (0, 1)),
           full(col(freq)), full(deltas), full(fwd_r)],
        out_specs=[pl.BlockSpec((2, n1, DFT_M_TILE, H_WIDTH), lambda i: (0, 0, i, 0)),
                   pl.BlockSpec((1, H_WIDTH), lambda i: (0, 0))],
        out_shape=[jax.ShapeDtypeStruct((2, n1, n2, H_WIDTH), F32),
                   jax.ShapeDtypeStruct((1, H_WIDTH), F32)],
        compiler_params=_cparams("arbitrary"),
        name="hyena_filter_outer",
    )(*args, w3h, w3h, col(freq), deltas, fwd_r)


def _dft_factors(n):
    lg = int(round(math.log2(n)))
    n1 = 1 << ((lg + 1) // 2)
    return n1, n // n1


def _dft_outer_matrices(n1):
    k = np.arange(n1)[:, None]
    n = np.arange(n1)[None, :]
    ang = 2.0 * np.pi * ((k * n) % n1) / n1
    cr, ci = np.cos(ang), -np.sin(ang)
    h = n1 // 2
    fwd_c = np.block([[cr[:, :h], -ci[:, :h]], [ci[:, :h], cr[:, :h]]])
    fwd_r = np.concatenate([cr, ci], axis=0)
    ir, ii = cr[:h, :], -ci[:h, :]
    inv = np.block([[ir, -ii], [ii, ir]])
    return (jnp.asarray(fwd_c, F32).astype(BF16), jnp.asarray(fwd_r, F32).astype(BF16),
            jnp.asarray(inv, F32).astype(BF16))


def _dft_inner_matrices(n1, n2):
    n = n1 * n2
    k2 = np.arange(n2)[:, None]
    m = np.arange(n2)[None, :]
    ang = 2.0 * np.pi * ((k2 * m) % n2) / n2
    fr, fi = np.cos(ang), -np.sin(ang)
    f = np.block([[fr, -fi], [fi, fr]])
    k1 = jnp.arange(n1, dtype=jnp.int32)[:, None]
    tw_ang = ((jnp.arange(n2, dtype=jnp.int32)[None, :] * k1) % n).astype(F32) * (2.0 * math.pi / n)
    rep = lambda t: jnp.broadcast_to(t[:, :, None], (n1, n2, LANES))
    return (jnp.asarray(f, F32).astype(BF16), jnp.asarray(f.T, F32).astype(BF16),
            rep(jnp.cos(tw_ang)), rep(-jnp.sin(tw_ang)))


def _outer_dft_kernel(l_ref, x_ref, o_ref):
    p_in, p_out = x_ref.shape[0], o_ref.shape[0]
    r_out = o_ref.shape[1]
    for mm in range(x_ref.shape[2]):
        parts = [x_ref[p, :, mm, :] for p in range(p_in)]
        x = parts[0] if p_in == 1 else jnp.concatenate(parts, axis=0)
        out = jnp.dot(l_ref[...], x.astype(BF16), preferred_element_type=F32)
        for p in range(p_out):
            o_ref[p, :, mm, :] = out[p * r_out:(p + 1) * r_out]


def _outer_dft(lmat, x4, p_out):
    p_in, r_in, n2, c = x4.shape
    r_out = lmat.shape[0] // p_out
    tc = min(DFT_C_TILE, c)
    return pl.pallas_call(
        _outer_dft_kernel,
        grid=(n2 // DFT_M_TILE, c // tc),
        in_specs=[pl.BlockSpec(lmat.shape, lambda m, j: (0, 0)),
                  pl.BlockSpec((p_in, r_in, DFT_M_TILE, tc), lambda m, j: (0, 0, m, j))],
        out_specs=pl.BlockSpec((p_out, r_out, DFT_M_TILE, tc), lambda m, j: (0, 0, m, j)),
        out_shape=jax.ShapeDtypeStruct((p_out, r_out, n2, c), F32),
        compiler_params=_cparams("parallel", "parallel"),
        name="dft_outer",
    )(lmat, x4)


DFT_K_TILE = 2


def _twiddled_inner_dft(f_ref, twr_ref, twi_ref, a_ref, kk):
    n2, c = a_ref.shape[2], a_ref.shape[3]
    twr = jnp.tile(twr_ref[kk], (1, c // LANES))
    twi = jnp.tile(twi_ref[kk], (1, c // LANES))
    ar, ai = a_ref[0, kk], a_ref[1, kk]
    a = jnp.concatenate([(ar * twr - ai * twi).astype(BF16), (ar * twi + ai * twr).astype(BF16)], axis=0)
    x = jnp.dot(f_ref[...], a, preferred_element_type=F32)
    return x[:n2], x[n2:], twr, twi


def _inner_fwd_kernel(f_ref, twr_ref, twi_ref, a_ref, o_ref):
    for kk in range(a_ref.shape[1]):
        xr, xi, _, _ = _twiddled_inner_dft(f_ref, twr_ref, twi_ref, a_ref, kk)
        o_ref[0, kk] = xr.astype(o_ref.dtype)
        o_ref[1, kk] = xi.astype(o_ref.dtype)


def _inner_specs(n1, n2, c):
    blk = pl.BlockSpec((2, DFT_K_TILE, n2, c), lambda k: (0, k, 0, 0))
    mat = pl.BlockSpec((2 * n2, 2 * n2), lambda k: (0, 0))
    tw = pl.BlockSpec((DFT_K_TILE, n2, LANES), lambda k: (k, 0, 0))
    return blk, mat, tw


def _inner_fwd(f, twr, twi, a):
    _, n1, n2, c = a.shape
    blk, mat, tw = _inner_specs(n1, n2, c)
    return pl.pallas_call(
        _inner_fwd_kernel,
        grid=(n1 // DFT_K_TILE,),
        in_specs=[mat, tw, tw, blk],
        out_specs=blk,
        out_shape=jax.ShapeDtypeStruct((2, n1, n2, c), BF16),
        compiler_params=_cparams("parallel"),
        name="dft_inner_filter",
    )(f, twr, twi, a)


def _inner_conv_kernel(f_ref, ft_ref, twr_ref, twi_ref, a_ref, k_ref, o_ref):
    n2 = a_ref.shape[2]
    for kk in range(a_ref.shape[1]):
        xr, xi, twr, twi = _twiddled_inner_dft(f_ref, twr_ref, twi_ref, a_ref, kk)
        kr, ki = k_ref[0, kk].astype(F32), k_ref[1, kk].astype(F32)
        yr = xr * kr - xi * ki
        yi = xr * ki + xi * kr
        y = jnp.concatenate([yr.astype(BF16), yi.astype(BF16)], axis=0)
        b = jnp.dot(ft_ref[...], y, preferred_element_type=F32)
        br, bi = b[:n2], b[n2:]
        o_ref[0, kk] = br * twr + bi * twi
        o_ref[1, kk] = bi * twr - br * twi


def _inner_conv(f, ft, twr, twi, a, kf):
    _, n1, n2, c = a.shape
    blk, mat, tw = _inner_specs(n1, n2, c)
    return pl.pallas_call(
        _inner_conv_kernel,
        grid=(n1 // DFT_K_TILE,),
        in_specs=[mat, mat, tw, tw, blk, blk],
        out_specs=blk,
        out_shape=jax.ShapeDtypeStruct((2, n1, n2, c), F32),
        compiler_params=_cparams("parallel"),
        name="dft_inner_conv",
    )(f, ft, twr, twi, a, kf)


def _hyena_long_conv(s, w1, b1, w2, b2, w3, freq):
    bsz, L, c = s.shape
    assert bsz == 2
    n = 2 * L
    n1, n2 = _dft_factors(n)
    fwd_c, fwd_r, inv = _dft_outer_matrices(n1)
    f, ft, twr, twi = _dft_inner_matrices(n1, n2)
    af, sumsq = _filter_outer(L, n1, n2, fwd_r, w1, b1, w2, b2, w3, freq)
    kf = _inner_fwd(f, twr, twi, af)
    a = _outer_dft(fwd_c, s.reshape(2, n1 // 2, n2, c), 2)
    b = _inner_conv(f, ft, twr, twi, a, kf)
    y = _outer_dft(inv, b, 2)
    return y.reshape(2, L, c), sumsq


def _pack_bf16_pairs(x):
    half = x.shape[1] // 2
    lo = pltpu.bitcast(x[:, :half].astype(BF16).astype(F32), jnp.uint32) >> 16
    hi = pltpu.bitcast(x[:, half:].astype(BF16).astype(F32), jnp.uint32) & jnp.uint32(0xFFFF0000)
    return lo | hi


def _unpack_bf16_pairs(p):
    lo = pltpu.bitcast(p << 16, F32).astype(BF16)
    hi = pltpu.bitcast(p & jnp.uint32(0xFFFF0000), F32).astype(BF16)
    return jnp.concatenate([lo, hi], axis=1)


def _merge_kernel(hf_ref, hb_ref, o_ref, x0_ref, s_ref, y_ref, ga_ref, gb_ref, x_ref,
                  ysc_ref, hbias_ref, gate_ref, g2_ref, sh_ref, sc_ref,
                  wa_ref, wb_ref, wo_ref, x1_ref, h2_ref):
    a = o_ref[...].astype(F32) * (hf_ref[...].astype(F32) + hb_ref[...].astype(F32))
    s = s_ref[...]
    hy = x0_ref[...].astype(F32) * (y_ref[...] * ysc_ref[...] + hbias_ref[...] * s)
    pa = jnp.dot(a.astype(BF16), wa_ref[...], preferred_element_type=F32)
    pb = jnp.dot(hy.astype(BF16), wb_ref[...], preferred_element_type=F32)
    mix = ga_ref[...].astype(F32) * pa + gb_ref[...].astype(F32) * pb
    out = jnp.dot(mix.astype(BF16), wo_ref[...], preferred_element_type=F32)
    x1 = x_ref[...] + gate_ref[...] * out
    x1_ref[...] = x1
    y = x1 * lax.rsqrt(jnp.mean(x1 * x1, axis=-1, keepdims=True) + EPS) * g2_ref[...]
    h2_ref[...] = _pack_bf16_pairs(y * (1.0 + sc_ref[...]) + sh_ref[...])


def _merge(hdirs, pm, x0, s, y, x, yscale, h_bias, gate1, g2, shift2, scale2, w_a, w_b, w_out, tm=256):
    bsz, L, d = x.shape
    tok = pl.BlockSpec((None, tm, d), lambda b, i: (b, i, 0))

    def pm_tile(col):
        return pl.BlockSpec((None, tm, d), lambda b, i: (b, i, col))

    vec = pl.BlockSpec((1, d), lambda b, i: (0, 0))
    bvec = pl.BlockSpec((None, 1, d), lambda b, i: (b, 0, 0))
    wsp = pl.BlockSpec((d, d), lambda b, i: (0, 0))
    return pl.pallas_call(
        _merge_kernel,
        grid=(bsz, L // tm),
        in_specs=[pl.BlockSpec((None, None, tm, d), lambda b, i: (0, b, i, 0)),
                  pl.BlockSpec((None, None, tm, d), lambda b, i: (1, b, i, 0)),
                  pm_tile(PM_O), tok, tok, tok, pm_tile(PM_GA), pm_tile(PM_GB), tok,
                  vec, vec, bvec, vec, bvec, bvec, wsp, wsp, wsp],
        out_specs=[tok, pl.BlockSpec((None, tm, d // 2), lambda b, i: (b, i, 0))],
        out_shape=[jax.ShapeDtypeStruct((bsz, L, d), F32), jax.ShapeDtypeStruct((bsz, L, d // 2), jnp.uint32)],
        compiler_params=_cparams("parallel", "parallel"),
        name="merge",
    )(hdirs, hdirs, pm, x0, s, y, pm, pm, x, yscale, h_bias.reshape(1, d), gate1, g2.reshape(1, d),
      shift2, scale2, w_a, w_b, w_out)


MOE_BLOCK = 256
ROUTE_E1, ROUTE_E2, ROUTE_W1, ROUTE_W2 = 0, 1, 2, 3
EXP_LANE0 = N_GROUPS


def _first_lane_of_max(val, valid, lane):
    masked = jnp.where(valid, val, NEG_BIG)
    mx = jnp.max(masked, axis=1, keepdims=True)
    idx = jnp.min(jnp.where(valid & (masked == mx), lane, LANES), axis=1, keepdims=True)
    return mx, idx


def _router_kernel(h_ref, w_ref, b_ref, r_ref):
    logits = jnp.dot(_unpack_bf16_pairs(h_ref[...]), w_ref[...], preferred_element_type=F32) + b_ref[...]
    lane = lax.broadcasted_iota(jnp.int32, logits.shape, 1)
    is_g = lane < N_GROUPS
    gmax, gsel = _first_lane_of_max(logits, is_g, lane)
    gsum = jnp.sum(jnp.where(is_g, jnp.exp(logits - gmax), 0.0), axis=1, keepdims=True)
    gw = 1.0 / gsum
    lo = EXP_LANE0 + gsel * EXPERTS_PER_GROUP
    in_grp = (lane >= lo) & (lane < lo + EXPERTS_PER_GROUP)
    emax, l1 = _first_lane_of_max(logits, in_grp, lane)
    esum = jnp.sum(jnp.where(in_grp, jnp.exp(logits - emax), 0.0), axis=1, keepdims=True)
    e2max, l2 = _first_lane_of_max(logits, in_grp & (lane != l1), lane)
    v1 = 1.0 / esum
    v2 = jnp.exp(e2max - emax) / esum
    vs = v1 + v2
    w1 = gw * v1 / vs
    w2 = gw * v2 / vs
    e1 = (l1 - EXP_LANE0).astype(F32)
    e2 = (l2 - EXP_LANE0).astype(F32)
    r_ref[...] = jnp.where(lane == ROUTE_E1, e1,
                           jnp.where(lane == ROUTE_E2, e2,
                                     jnp.where(lane == ROUTE_W1, w1,
                                               jnp.where(lane == ROUTE_W2, w2, 0.0))))


def _router(h2, w_group, b_group, w_router, b_router, tm=1024):
    n, dp = h2.shape
    d = 2 * dp
    w = jnp.zeros((d, LANES), F32).at[:, :N_GROUPS].set(w_group).at[
        :, EXP_LANE0:EXP_LANE0 + N_EXPERTS].set(w_router).astype(BF16)
    b = jnp.zeros((1, LANES), F32).at[0, :N_GROUPS].set(b_group).at[
        0, EXP_LANE0:EXP_LANE0 + N_EXPERTS].set(b_router)
    return pl.pallas_call(
        _router_kernel,
        grid=(n // tm,),
        in_specs=[pl.BlockSpec((tm, dp), lambda i: (i, 0)),
                  pl.BlockSpec((d, LANES), lambda i: (0, 0)),
                  pl.BlockSpec((1, LANES), lambda i: (0, 0))],
        out_specs=pl.BlockSpec((tm, LANES), lambda i: (i, 0)),
        out_shape=jax.ShapeDtypeStruct((n, LANES), F32),
        compiler_params=_cparams("parallel"),
        name="moe_router",
    )(h2, w, b)


def _slots_kernel(r_ref, dest_ref, cnt_ref, run_sc, start_sc):
    ph = pl.program_id(0)
    i = pl.program_id(1)
    rec = r_ref[...]
    tm = rec.shape[0]
    lane = lax.broadcasted_iota(jnp.int32, rec.shape, 1)
    e1 = rec[:, ROUTE_E1:ROUTE_E1 + 1].astype(jnp.int32)
    e2 = rec[:, ROUTE_E2:ROUTE_E2 + 1].astype(jnp.int32)
    oh1 = lane == e1
    oh2 = lane == e2
    oh = (oh1 | oh2).astype(F32)

    @pl.when((ph == 0) & (i == 0))
    def _():
        run_sc[...] = jnp.zeros_like(run_sc)

    @pl.when(ph == 0)
    def _():
        run_sc[...] += jnp.sum(oh, axis=0, keepdims=True)

    @pl.when((ph == 1) & (i == 0))
    def _():
        counts = run_sc[...]
        cnt_ref[...] = counts
        nblk = jnp.floor((counts + (MOE_BLOCK - 1)) * (1.0 / MOE_BLOCK))
        rr = lax.broadcasted_iota(jnp.int32, (LANES, LANES), 0)
        cc = lax.broadcasted_iota(jnp.int32, (LANES, LANES), 1)
        before = (rr < cc).astype(BF16)
        first = jnp.dot(nblk.astype(BF16), before, preferred_element_type=F32)
        start_sc[...] = first * float(MOE_BLOCK)
        run_sc[...] = jnp.zeros_like(run_sc)

    @pl.when(ph == 1)
    def _():
        r = lax.broadcasted_iota(jnp.int32, (tm, tm), 0)
        c = lax.broadcasted_iota(jnp.int32, (tm, tm), 1)
        earlier = (r > c).astype(BF16)
        rank = jnp.dot(earlier, oh.astype(BF16), preferred_element_type=F32) + run_sc[...] + start_sc[...]
        d1 = jnp.sum(jnp.where(oh1, rank, 0.0), axis=1, keepdims=True)
        d2 = jnp.sum(jnp.where(oh2, rank, 0.0), axis=1, keepdims=True)
        dest_ref[...] = jnp.where(lane == 0, d1, jnp.where(lane == 1, d2, 0.0)).astype(jnp.int32)
        run_sc[...] += jnp.sum(oh, axis=0, keepdims=True)


def _slots(route, tm=512):
    n = route.shape[0]
    return pl.pallas_call(
        _slots_kernel,
        grid=(2, n // tm),
        in_specs=[pl.BlockSpec((tm, LANES), lambda p, i: (i, 0))],
        out_specs=[pl.BlockSpec((tm, LANES), lambda p, i: (i * p, 0)),
                   pl.BlockSpec((1, LANES), lambda p, i: (0, 0))],
        out_shape=[jax.ShapeDtypeStruct((n, LANES), jnp.int32), jax.ShapeDtypeStruct((1, LANES), F32)],
        scratch_shapes=[pltpu.VMEM((1, LANES), F32), pltpu.VMEM((1, LANES), F32)],
        compiler_params=_cparams("arbitrary", "arbitrary"),
        name="moe_slots",
    )(route)


DMA_UNROLL = 8


def _row_copy(src_ref, dst_ref, sem, src_row, dst_row):
    return pltpu.make_async_copy(src_ref.at[pl.ds(src_row, 1)], dst_ref.at[pl.ds(dst_row, 1)], sem)


def _dispatch_kernel(dest_ref, h_ref, xs_in_ref, xs_ref, sem):
    del xs_in_ref
    tm = h_ref.shape[0]

    def start(r, carry):
        _row_copy(h_ref, xs_ref, sem, r, dest_ref[0, 2 * r]).start(priority=0)
        _row_copy(h_ref, xs_ref, sem, r, dest_ref[0, 2 * r + 1]).start(priority=1)
        return carry

    lax.fori_loop(0, tm, start, 0, unroll=DMA_UNROLL)

    def wait(r, carry):
        _row_copy(h_ref, xs_ref, sem, 0, 0).wait()
        _row_copy(h_ref, xs_ref, sem, 0, 0).wait()
        return carry

    lax.fori_loop(0, tm, wait, 0, unroll=DMA_UNROLL)


def _dispatch(h2, dest, n_slots, tm=256):
    n, d = h2.shape
    dest3 = dest.reshape(n // tm, 1, 2 * tm)
    zeros = jnp.zeros((n_slots, d), h2.dtype)
    return pl.pallas_call(
        _dispatch_kernel,
        grid=(n // tm,),
        in_specs=[pl.BlockSpec((None, 1, 2 * tm), lambda i: (i, 0, 0), memory_space=pltpu.SMEM),
                  pl.BlockSpec((tm, d), lambda i: (i, 0)),
                  pl.BlockSpec(memory_space=pl.ANY)],
        out_specs=pl.BlockSpec(memory_space=pl.ANY),
        out_shape=jax.ShapeDtypeStruct((n_slots, d), h2.dtype),
        scratch_shapes=[pltpu.SemaphoreType.DMA(())],
        input_output_aliases={2: 0},
        compiler_params=_cparams("arbitrary"),
        name="moe_dispatch",
    )(dest3, h2, zeros)


def _experts_kernel(be_ref, first_ref, nxt_ref, par_ref, nu_ref, x_ref, w1_hbm, w3_hbm, w2_hbm, o_ref,
                    w1f, w3f, w2f, w1b, w3b, w2b, sems):
    i = pl.program_id(0)

    def weight_copies(e, slot):
        return (pltpu.make_async_copy(w1_hbm.at[e], w1f.at[slot], sems.at[0, slot]),
                pltpu.make_async_copy(w3_hbm.at[e], w3f.at[slot], sems.at[1, slot]),
                pltpu.make_async_copy(w2_hbm.at[e], w2f.at[slot], sems.at[2, slot]))

    @pl.when(i == 0)
    def _():
        for cp in weight_copies(be_ref[0], 0):
            cp.start()

    @pl.when(first_ref[i] == 1)
    def _():
        slot = par_ref[i]

        @pl.when(nxt_ref[i] >= 0)
        def _():
            for cp in weight_copies(nxt_ref[i], 1 - slot):
                cp.start()

        for cp in weight_copies(be_ref[i], slot):
            cp.wait()
        w1b[...] = w1f[slot].astype(BF16)
        w3b[...] = w3f[slot].astype(BF16)
        w2b[...] = w2f[slot].astype(BF16)

    @pl.when(i < nu_ref[0])
    def _():
        x = _unpack_bf16_pairs(x_ref[...])
        a = jnp.dot(x, w1b[...], preferred_element_type=F32)
        b = jnp.dot(x, w3b[...], preferred_element_type=F32)
        hmid = (a * jax.nn.sigmoid(a)) * b
        o_ref[...] = _pack_bf16_pairs(jnp.dot(hmid.astype(BF16), w2b[...], preferred_element_type=F32))

    @pl.when(i >= nu_ref[0])
    def _():
        o_ref[...] = jnp.zeros_like(o_ref)


def _experts(xs, nb, block_e, n_used, w1_e, w3_e, w2_e):
    dp = xs.shape[1]
    d, de = w1_e.shape[1], w1_e.shape[2]
    idx = jnp.arange(nb, dtype=jnp.int32)
    used = idx < n_used[0]
    first = used & ((idx == 0) | (block_e != jnp.roll(block_e, 1)))
    ordinal = jnp.cumsum(first.astype(jnp.int32)) - 1
    par = (ordinal % 2).astype(jnp.int32)
    first_pos = jnp.where(first, idx, nb)
    next_first = lax.cummin(jnp.concatenate([first_pos[1:], jnp.full((1,), nb, jnp.int32)]), reverse=True)
    nxt = jnp.where(next_first < nb, block_e[jnp.minimum(next_first, nb - 1)], -1).astype(jnp.int32)
    any_spec = pl.BlockSpec(memory_space=pl.ANY)
    grid_spec = pltpu.PrefetchScalarGridSpec(
        num_scalar_prefetch=5,
        grid=(nb,),
        in_specs=[pl.BlockSpec((MOE_BLOCK, dp), lambda i, *_: (i, 0)), any_spec, any_spec, any_spec],
        out_specs=pl.BlockSpec((MOE_BLOCK, dp), lambda i, *_: (i, 0)),
        scratch_shapes=[pltpu.VMEM((2, d, de), F32), pltpu.VMEM((2, d, de), F32), pltpu.VMEM((2, de, d), F32),
                        pltpu.VMEM((d, de), BF16), pltpu.VMEM((d, de), BF16), pltpu.VMEM((de, d), BF16),
                        pltpu.SemaphoreType.DMA((3, 2))],
    )
    return pl.pallas_call(
        _experts_kernel,
        grid_spec=grid_spec,
        out_shape=jax.ShapeDtypeStruct((nb * MOE_BLOCK, dp), xs.dtype),
        compiler_params=_cparams("arbitrary"),
        name="moe_experts",
    )(block_e, first.astype(jnp.int32), nxt, par, n_used, xs, w1_e, w3_e, w2_e)


def _combine_kernel(dest_ref, r_ref, x_ref, gate_ref, gf_ref, ys_ref, o_ref, buf1, buf2, sem):
    tm = x_ref.shape[0]

    def start(r, carry):
        _row_copy(ys_ref, buf1, sem, dest_ref[0, 2 * r], r).start(priority=0)
        _row_copy(ys_ref, buf2, sem, dest_ref[0, 2 * r + 1], r).start(priority=1)
        return carry

    lax.fori_loop(0, tm, start, 0, unroll=DMA_UNROLL)

    def wait(r, carry):
        _row_copy(ys_ref, buf1, sem, 0, 0).wait()
        _row_copy(ys_ref, buf2, sem, 0, 0).wait()
        return carry

    lax.fori_loop(0, tm, wait, 0, unroll=DMA_UNROLL)
    rec = r_ref[...]
    y = buf1[...] * rec[:, ROUTE_W1:ROUTE_W1 + 1] + buf2[...] * rec[:, ROUTE_W2:ROUTE_W2 + 1]
    x2 = x_ref[...] + gate_ref[...] * y
    o_ref[...] = x2 * lax.rsqrt(jnp.mean(x2 * x2, axis=-1, keepdims=True) + EPS) * gf_ref[...]


def _combine(ys, dest, route, x1, gate2, g_final, tm=256):
    bsz, L, d = x1.shape
    n = bsz * L
    tpb = L // tm
    dest3 = dest.reshape(n // tm, 1, 2 * tm)
    return pl.pallas_call(
        _combine_kernel,
        grid=(bsz, tpb),
        in_specs=[pl.BlockSpec((None, 1, 2 * tm), lambda b, i: (b * tpb + i, 0, 0), memory_space=pltpu.SMEM),
                  pl.BlockSpec((tm, LANES), lambda b, i: (b * tpb + i, 0)),
                  pl.BlockSpec((None, tm, d), lambda b, i: (b, i, 0)),
                  pl.BlockSpec((None, 1, d), lambda b, i: (b, 0, 0)),
                  pl.BlockSpec((1, d), lambda b, i: (0, 0)),
                  pl.BlockSpec(memory_space=pl.ANY)],
        out_specs=pl.BlockSpec((None, tm, d), lambda b, i: (b, i, 0)),
        out_shape=jax.ShapeDtypeStruct((bsz, L, d), F32),
        scratch_shapes=[pltpu.VMEM((tm, d), F32), pltpu.VMEM((tm, d), F32), pltpu.SemaphoreType.DMA(())],
        compiler_params=_cparams("arbitrary", "arbitrary"),
        name="moe_combine",
    )(dest3, route, x1, gate2, g_final.reshape(1, d), ys)


SC_WINDOW = 128
SC_CORES, SC_SUBCORES = 2, 16
SC_WORKERS = SC_CORES * SC_SUBCORES


def _sc_worker_id():
    return lax.axis_index("c") * SC_SUBCORES + lax.axis_index("s")


def _sc_mesh():
    return plsc.VectorSubcoreMesh(core_axis_name="c", subcore_axis_name="s")


def _sc_dispatch(rows, dest0, dest1, pad_slots, n_rows):
    n, dv = rows.shape
    nwin, pwin = n // SC_WINDOW, pad_slots.shape[0] // SC_WINDOW
    assert n % (SC_WINDOW * SC_WORKERS) == 0 and pad_slots.shape[0] % (SC_WINDOW * SC_WORKERS) == 0
    zeros = jnp.zeros((SC_WINDOW, dv), rows.dtype)

    @pl.kernel(out_type=jax.ShapeDtypeStruct((n_rows, dv), rows.dtype), mesh=_sc_mesh(),
               scratch_types=[pltpu.VMEM((1, SC_WINDOW), jnp.int32), pltpu.VMEM((SC_WINDOW, dv), rows.dtype)],
               name="moe_dispatch_sc")
    def scatter(x_hbm, d0_hbm, d1_hbm, p_hbm, z_hbm, o_hbm, idx, buf):
        wid = _sc_worker_id()
        pltpu.sync_copy(z_hbm, buf)

        @pl.loop(0, pwin // SC_WORKERS)
        def _(t):
            w = t * SC_WORKERS + wid
            pltpu.sync_copy(p_hbm.at[pl.ds(w, 1)], idx)
            pltpu.sync_copy(buf, o_hbm.at[idx.at[0]])

        @pl.loop(0, nwin // SC_WORKERS)
        def _(t):
            w = t * SC_WORKERS + wid
            pltpu.sync_copy(x_hbm.at[pl.ds(w * SC_WINDOW, SC_WINDOW)], buf)
            for d_hbm in (d0_hbm, d1_hbm):
                pltpu.sync_copy(d_hbm.at[pl.ds(w, 1)], idx)
                pltpu.sync_copy(buf, o_hbm.at[idx.at[0]])

    return scatter(rows, dest0.reshape(nwin, SC_WINDOW), dest1.reshape(nwin, SC_WINDOW),
                   pad_slots.reshape(pwin, SC_WINDOW), zeros)


def _sc_gather(table, index):
    m = index.shape[0]
    dv = table.shape[1]
    nwin = m // SC_WINDOW
    assert m % (SC_WINDOW * SC_WORKERS) == 0

    @pl.kernel(out_type=jax.ShapeDtypeStruct((m, dv), table.dtype), mesh=_sc_mesh(),
               scratch_types=[pltpu.VMEM((1, SC_WINDOW), jnp.int32), pltpu.VMEM((SC_WINDOW, dv), table.dtype)],
               name="moe_gather_sc")
    def gather(x_hbm, i_hbm, o_hbm, idx, buf):
        wid = _sc_worker_id()

        @pl.loop(0, nwin // SC_WORKERS)
        def _(t):
            w = t * SC_WORKERS + wid
            pltpu.sync_copy(i_hbm.at[pl.ds(w, 1)], idx)
            pltpu.sync_copy(x_hbm.at[idx.at[0]], buf)
            pltpu.sync_copy(buf, o_hbm.at[pl.ds(w * SC_WINDOW, SC_WINDOW)])

    return gather(table, index.reshape(nwin, SC_WINDOW))


def _combine_planes_kernel(r_ref, ya_ref, yb_ref, x_ref, gate_ref, gf_ref, o_ref):
    rec = r_ref[...]
    y = (_unpack_bf16_pairs(ya_ref[...]).astype(F32) * rec[:, ROUTE_W1:ROUTE_W1 + 1]
         + _unpack_bf16_pairs(yb_ref[...]).astype(F32) * rec[:, ROUTE_W2:ROUTE_W2 + 1])
    x2 = x_ref[...] + gate_ref[...] * y
    o_ref[...] = x2 * lax.rsqrt(jnp.mean(x2 * x2, axis=-1, keepdims=True) + EPS) * gf_ref[...]


def _combine_planes(g, route, x1, gate2, g_final, tm=512):
    bsz, L, d = x1.shape
    tpb = L // tm
    dp = g.shape[-1]
    return pl.pallas_call(
        _combine_planes_kernel,
        grid=(bsz, tpb),
        in_specs=[pl.BlockSpec((tm, LANES), lambda b, i: (b * tpb + i, 0)),
                  pl.BlockSpec((None, tm, dp), lambda b, i: (0, b * tpb + i, 0)),
                  pl.BlockSpec((None, tm, dp), lambda b, i: (1, b * tpb + i, 0)),
                  pl.BlockSpec((None, tm, d), lambda b, i: (b, i, 0)),
                  pl.BlockSpec((None, 1, d), lambda b, i: (b, 0, 0)),
                  pl.BlockSpec((1, d), lambda b, i: (0, 0))],
        out_specs=pl.BlockSpec((None, tm, d), lambda b, i: (b, i, 0)),
        out_shape=jax.ShapeDtypeStruct((bsz, L, d), F32),
        compiler_params=_cparams("parallel", "parallel"),
        name="moe_combine",
    )(route, g, g, x1, gate2, g_final.reshape(1, d))


def _moe(h2, x1, gate2, g_final, w_group, b_group, w_router, b_router, w1_e, w3_e, w2_e):
    bsz, L, d = x1.shape
    n = bsz * L
    h2f = h2.reshape(n, h2.shape[-1])
    route = _router(h2f, w_group, b_group, w_router, b_router)
    dest_rec, counts = _slots(route)
    dest = dest_rec[:, :2].reshape(2 * n)
    nb = (2 * n) // MOE_BLOCK + N_EXPERTS
    cnt = counts[0, :N_EXPERTS].astype(jnp.int32)
    blocks_per_e = (cnt + MOE_BLOCK - 1) // MOE_BLOCK
    ends = jnp.cumsum(blocks_per_e)
    block_e = jnp.clip(jnp.searchsorted(ends, jnp.arange(nb, dtype=jnp.int32), side='right'),
                       0, N_EXPERTS - 1).astype(jnp.int32)
    n_used = ends[-1:].astype(jnp.int32)
    n_slots = nb * MOE_BLOCK
    pad_i = jnp.arange(N_EXPERTS * MOE_BLOCK, dtype=jnp.int32)
    pad_e, pad_j = pad_i // MOE_BLOCK, pad_i % MOE_BLOCK
    first_slot = (ends - blocks_per_e) * MOE_BLOCK
    is_pad = cnt[pad_e] + pad_j < blocks_per_e[pad_e] * MOE_BLOCK
    pad_slots = jnp.where(is_pad, first_slot[pad_e] + cnt[pad_e] + pad_j, n_slots + pad_i)
    xs = _sc_dispatch(h2f, dest_rec[:, 0], dest_rec[:, 1], pad_slots, n_slots + N_EXPERTS * MOE_BLOCK)
    ys = _experts(xs, nb, block_e, n_used, w1_e, w3_e, w2_e)
    g = _sc_gather(ys, jnp.concatenate([dest_rec[:, 0], dest_rec[:, 1]]))
    return _combine_planes(g.reshape(2, n, g.shape[-1]), route, x1, gate2, g_final)


def kernel(x, c, ctx, c_ctx, w_mod, b_mod, g_norm1, g_norm2, w_in, b_in, w_qk_conv, b_qk_conv,
           w_h_conv, b_h_conv, hf_w1, hf_b1, hf_w2, hf_b2, hf_w3, hf_freq, h_bias, w_a, w_b, w_out,
           w_group, b_group, w_router, b_router, w1_e, w3_e, w2_e, g_final):
    assert w_mod.shape[0] == 1, "single-layer block"
    (w_mod, b_mod, g_norm1, g_norm2, w_in, b_in, w_qk_conv, b_qk_conv, w_h_conv, b_h_conv, hf_w1, hf_b1, hf_w2,
     hf_b2, hf_w3, hf_freq, h_bias, w_a, w_b, w_out, w_group, b_group, w_router, b_router, w1_e, w3_e, w2_e) = (
        t[0] for t in (w_mod, b_mod, g_norm1, g_norm2, w_in, b_in, w_qk_conv, b_qk_conv, w_h_conv, b_h_conv,
                       hf_w1, hf_b1, hf_w2, hf_b2, hf_w3, hf_freq, h_bias, w_a, w_b, w_out, w_group, b_group,
                       w_router, b_router, w1_e, w3_e, w2_e))
    bsz, L, d = x.shape
    lc = ctx.shape[1]
    seg = L // (L // GRID_W)
    chunk_c = min(lc, MLSTM_CHUNK)
    assert bsz + 1 <= 8 and lc % chunk_c == 0 and L % MLSTM_CHUNK == 0

    cond = jnp.zeros((8, d), F32).at[:bsz].set(c).at[bsz].set(c_ctx)
    mod = _adaln(cond, w_mod, b_mod).reshape(8, 6, d)
    modx = mod[:bsz]
    shift1, scale1, gate1, shift2, scale2, gate2 = (modx[:, i:i + 1] for i in range(6))
    shift1c = jnp.broadcast_to(mod[bsz, 0].reshape(1, 1, d), (bsz, 1, d))
    scale1c = jnp.broadcast_to(mod[bsz, 1].reshape(1, 1, d), (bsz, 1, d))

    w_in16 = w_in.astype(BF16)
    k_scale = jnp.full((M_WIDTH,), M_HEAD_DIM ** -0.5, F32)
    qk_scale = jnp.concatenate([jnp.ones((M_WIDTH,), F32), k_scale])
    w_gates, b_gates = w_in[:, IG0:M_COLS], b_in[IG0:M_COLS]

    hc = _norm_mod(ctx, g_norm1, shift1c, scale1c, lc)
    kc = _proj_conv_silu(hc, w_in16[:, K0:V0], b_in[K0:V0], w_qk_conv[:, M_WIDTH:], b_qk_conv[M_WIDTH:],
                         k_scale, lc, lc)
    vc = _proj_act(hc, w_in16[:, V0:O0], b_in[V0:O0], "none", BF16, lc)
    bcc, acc, arc = _gates(hc, w_gates, b_gates, chunk_c)
    zero_state = (jnp.zeros((bsz, 2, M_HEADS, M_HEAD_DIM, M_HEAD_DIM), F32),
                  jnp.zeros((bsz, 2, M_HEADS, 1, M_HEAD_DIM), F32),
                  jnp.zeros((bsz, 2, M_HEADS, 1, LANES), F32))
    _, ctx_state = _mlstm(None, (kc, 0), (vc, 0), bcc, acc, arc, zero_state, False, chunk_c)

    tm = 1024
    w_main = jnp.concatenate([w_in16[:, Q0:IG0], w_in16[:, GA0:IN_COLS]], axis=1)
    b_main = jnp.concatenate([b_in[Q0:IG0], b_in[GA0:IN_COLS]])
    pm, h = _proj_main(x, g_norm1, shift1, scale1, w_main, b_main, w_qk_conv, b_qk_conv, qk_scale, seg, tm)
    bc, ac, ar = _gates(h, w_gates, b_gates, MLSTM_CHUNK)
    hdirs, _ = _mlstm((pm, PM_Q), (pm, PM_K), (pm, PM_V), bc, ac, ar, ctx_state, True, MLSTM_CHUNK)

    x0, s = _proj_hyena(h, w_in16[:, HY0:GA0], b_in[HY0:GA0], w_h_conv, b_h_conv, seg, tm)
    y, sumsq = _hyena_long_conv(s, hf_w1, hf_b1, hf_w2, hf_b2, hf_w3, hf_freq)
    yscale = lax.rsqrt(sumsq + EPS) * (1.0 / (2 * L))

    x1, h2 = _merge(hdirs, pm, x0, s, y, x, yscale, h_bias, gate1, g_norm2, shift2, scale2,
                    w_a.astype(BF16), w_b.astype(BF16), w_out.astype(BF16))
    return _moe(h2, x1, gate2, g_final, w_group, b_group, w_router, b_router, w1_e, w3_e, w2_e)
```

```python
import functools
import math

import jax
import jax.numpy as jnp
import numpy as np
from jax import lax
from jax.experimental import pallas as pl
from jax.experimental.pallas import tpu as pltpu
from jax.experimental.pallas import tpu_sc as plsc

F32 = jnp.float32
BF16 = jnp.bfloat16

D_MODEL = 1024
GRID_W = 64
EPS = 1e-6
M_HEADS = 4
M_HEAD_DIM = 256
M_WIDTH = M_HEADS * M_HEAD_DIM
H_WIDTH = 1024
H_POS_BANDS = 16
H_FILTER_HIDDEN = 64
H_FAST_DECAY_PCT = 0.3
H_SLOW_DECAY_PCT = 1.5
H_DECAY_TARGET = 1e-2
N_GROUPS = 8
EXPERTS_PER_GROUP = 8
N_EXPERTS = N_GROUPS * EXPERTS_PER_GROUP
D_EXPERT = 512
Q0 = 0
K0 = Q0 + M_WIDTH
V0 = K0 + M_WIDTH
O0 = V0 + M_WIDTH
IG0 = O0 + M_WIDTH
FG0 = IG0 + 2 * M_HEADS
M_COLS = FG0 + 2 * M_HEADS
HY0 = M_COLS
GA0 = HY0 + 3 * H_WIDTH
GB0 = GA0 + D_MODEL
IN_COLS = GB0 + D_MODEL

LANES = 128
MLSTM_CHUNK = 512
NEG_BIG = -1e30
VMEM_LIMIT = 48 * 1024 * 1024


def _cparams(*sem):
    return pltpu.CompilerParams(dimension_semantics=sem, vmem_limit_bytes=VMEM_LIMIT)


def _adaln_kernel(c_ref, w_ref, b_ref, o_ref):
    s = c_ref[...]
    s = s * jax.nn.sigmoid(s)
    o_ref[...] = jnp.dot(s.astype(BF16), w_ref[...].astype(BF16), preferred_element_type=F32) + b_ref[...]


def _adaln(cond, w_mod, b_mod):
    n = w_mod.shape[1]
    tn = 1536
    return pl.pallas_call(
        _adaln_kernel,
        grid=(n // tn,),
        in_specs=[pl.BlockSpec((8, D_MODEL), lambda j: (0, 0)),
                  pl.BlockSpec((D_MODEL, tn), lambda j: (0, j)),
                  pl.BlockSpec((1, tn), lambda j: (0, j))],
        out_specs=pl.BlockSpec((8, tn), lambda j: (0, j)),
        out_shape=jax.ShapeDtypeStruct((8, n), F32),
        compiler_params=_cparams("arbitrary"),
        name="adaln",
    )(cond, w_mod, b_mod.reshape(1, n))


def _norm_mod_kernel(x_ref, g_ref, sh_ref, sc_ref, o_ref):
    x = x_ref[...]
    y = x * lax.rsqrt(jnp.mean(x * x, axis=-1, keepdims=True) + EPS)
    y = y * g_ref[...]
    o_ref[...] = (y * (1.0 + sc_ref[...]) + sh_ref[...]).astype(o_ref.dtype)


def _norm_mod(x, g, shift, scale, tm):
    bsz, L, d = x.shape
    return pl.pallas_call(
        _norm_mod_kernel,
        grid=(bsz, L // tm),
        in_specs=[pl.BlockSpec((None, tm, d), lambda b, i: (b, i, 0)),
                  pl.BlockSpec((1, d), lambda b, i: (0, 0)),
                  pl.BlockSpec((None, 1, d), lambda b, i: (b, 0, 0)),
                  pl.BlockSpec((None, 1, d), lambda b, i: (b, 0, 0))],
        out_specs=pl.BlockSpec((None, tm, d), lambda b, i: (b, i, 0)),
        out_shape=jax.ShapeDtypeStruct((bsz, L, d), BF16),
        compiler_params=_cparams("parallel", "parallel"),
        name="norm_mod",
    )(x, g.reshape(1, d), shift, scale)


def _conv3(z, wc, bc, seg):
    tm = z.shape[0]
    pos = lax.broadcasted_iota(jnp.int32, z.shape, 0) & (seg - 1)
    zp = jnp.where(pos == 0, 0.0, pltpu.roll(z, 1, 0))
    zn = jnp.where(pos == seg - 1, 0.0, pltpu.roll(z, tm - 1, 0))
    return zp * wc[0:1, :] + z * wc[1:2, :] + zn * wc[2:3, :] + bc


def _proj_act_kernel(h_ref, w_ref, b_ref, o_ref, *, act):
    z = jnp.dot(h_ref[...], w_ref[...], preferred_element_type=F32) + b_ref[...]
    if act == "sigmoid":
        z = jax.nn.sigmoid(z)
    o_ref[...] = z.astype(o_ref.dtype)


def _proj_act(h, w, b, act, out_dtype, tm, tn=512):
    bsz, L, d = h.shape
    n = w.shape[1]
    return pl.pallas_call(
        functools.partial(_proj_act_kernel, act=act),
        grid=(bsz, L // tm, n // tn),
        in_specs=[pl.BlockSpec((None, tm, d), lambda b_, i, j: (b_, i, 0)),
                  pl.BlockSpec((d, tn), lambda b_, i, j: (0, j)),
                  pl.BlockSpec((1, tn), lambda b_, i, j: (0, j))],
        out_specs=pl.BlockSpec((None, tm, tn), lambda b_, i, j: (b_, i, j)),
        out_shape=jax.ShapeDtypeStruct((bsz, L, n), out_dtype),
        compiler_params=_cparams("parallel", "parallel", "arbitrary"),
        name="proj_" + act,
    )(h, w, b.reshape(1, n))


def _proj_conv_silu_kernel(h_ref, w_ref, b_ref, wc_ref, bc_ref, cs_ref, o_ref, *, seg):
    z = jnp.dot(h_ref[...], w_ref[...], preferred_element_type=F32) + b_ref[...]
    y = _conv3(z, wc_ref[...], bc_ref[...], seg)
    y = y * jax.nn.sigmoid(y)
    o_ref[...] = (y * cs_ref[...]).astype(o_ref.dtype)


def _proj_conv_silu(h, w, b, wc, bc, colscale, seg, tm, tn=512):
    bsz, L, d = h.shape
    n = w.shape[1]
    col = lambda b_, i, j: (0, j)
    return pl.pallas_call(
        functools.partial(_proj_conv_silu_kernel, seg=seg),
        grid=(bsz, L // tm, n // tn),
        in_specs=[pl.BlockSpec((None, tm, d), lambda b_, i, j: (b_, i, 0)),
                  pl.BlockSpec((d, tn), col),
                  pl.BlockSpec((1, tn), col),
                  pl.BlockSpec((3, tn), col),
                  pl.BlockSpec((1, tn), col),
                  pl.BlockSpec((1, tn), col)],
        out_specs=pl.BlockSpec((None, tm, tn), lambda b_, i, j: (b_, i, j)),
        out_shape=jax.ShapeDtypeStruct((bsz, L, n), BF16),
        compiler_params=_cparams("parallel", "parallel", "arbitrary"),
        name="proj_conv_silu",
    )(h, w, b.reshape(1, n), wc, bc.reshape(1, n), colscale.reshape(1, n))


PROJ_TN = 1024
PROJ_SUB = 512
PM_Q, PM_K, PM_V, PM_O, PM_GA, PM_GB = range(6)


def _proj_main_kernel(x_ref, g_ref, sh_ref, sc_ref, w_ref, b_ref, wc_ref, bc_ref, cs_ref, o_ref, h_ref, *, seg):
    j = pl.program_id(2)

    @pl.when(j == 0)
    def _():
        _norm_mod_kernel(x_ref, g_ref, sh_ref, sc_ref, h_ref)

    def run(epilogue):
        for c in range(PROJ_TN // PROJ_SUB):
            sl = slice(c * PROJ_SUB, (c + 1) * PROJ_SUB)
            z = jnp.dot(h_ref[...], w_ref[:, sl], preferred_element_type=F32) + b_ref[:, sl]
            o_ref[:, sl] = epilogue(z, sl).astype(o_ref.dtype)

    def conv_silu(z, sl):
        y = _conv3(z, wc_ref[:, sl], bc_ref[:, sl], seg)
        return (y * jax.nn.sigmoid(y)) * cs_ref[:, sl]

    @pl.when(j <= PM_K)
    def _():
        run(conv_silu)

    @pl.when(j == PM_V)
    def _():
        run(lambda z, sl: z)

    @pl.when(j >= PM_O)
    def _():
        run(lambda z, sl: jax.nn.sigmoid(z))


def _proj_main(x, g, shift, scale, w, b, wc, bc, colscale, seg, tm):
    bsz, L, d = x.shape
    n = w.shape[1]
    qk = lambda b_, i, j: (0, jnp.minimum(j, PM_K))
    row = pl.BlockSpec((None, tm, d), lambda b_, i, j: (b_, i, 0))
    bvec = pl.BlockSpec((None, 1, d), lambda b_, i, j: (b_, 0, 0))
    return pl.pallas_call(
        functools.partial(_proj_main_kernel, seg=seg),
        grid=(bsz, L // tm, n // PROJ_TN),
        in_specs=[row, pl.BlockSpec((1, d), lambda b_, i, j: (0, 0)), bvec, bvec,
                  pl.BlockSpec((d, PROJ_TN), lambda b_, i, j: (0, j)),
                  pl.BlockSpec((1, PROJ_TN), lambda b_, i, j: (0, j)),
                  pl.BlockSpec((3, PROJ_TN), qk),
                  pl.BlockSpec((1, PROJ_TN), qk),
                  pl.BlockSpec((1, PROJ_TN), qk)],
        out_specs=[pl.BlockSpec((None, tm, PROJ_TN), lambda b_, i, j: (b_, i, j)), row],
        out_shape=[jax.ShapeDtypeStruct((bsz, L, n), BF16), jax.ShapeDtypeStruct((bsz, L, d), BF16)],
        compiler_params=_cparams("parallel", "parallel", "arbitrary"),
        name="proj_main",
    )(x, g.reshape(1, d), shift, scale, w, b.reshape(1, n), wc, bc.reshape(1, -1), colscale.reshape(1, -1))


def _proj_hyena_kernel(h_ref, w0_ref, w1_ref, w2_ref, b_ref, wc_ref, bc_ref, x0_ref, s_ref, *, seg):
    h = h_ref[...]
    us = []
    for g, w_ref in enumerate((w0_ref, w1_ref, w2_ref)):
        z = jnp.dot(h, w_ref[...], preferred_element_type=F32) + b_ref[g]
        us.append(_conv3(z, wc_ref[g], bc_ref[g], seg))
    x0_ref[...] = us[0].astype(x0_ref.dtype)
    s_ref[...] = us[1] * us[2]


def _proj_hyena(h, w, b, wc, bc, seg, tm, tn=512):
    bsz, L, d = h.shape
    nblk = H_WIDTH // tn
    b3 = b.reshape(3, 1, H_WIDTH)
    wc3 = wc.reshape(3, 3, H_WIDTH).transpose(1, 0, 2)
    bc3 = bc.reshape(3, 1, H_WIDTH)
    out_spec = pl.BlockSpec((None, tm, tn), lambda b_, i, j: (b_, i, j))
    return pl.pallas_call(
        functools.partial(_proj_hyena_kernel, seg=seg),
        grid=(bsz, L // tm, nblk),
        in_specs=[pl.BlockSpec((None, tm, d), lambda b_, i, j: (b_, i, 0)),
                  pl.BlockSpec((d, tn), lambda b_, i, j: (0, j)),
                  pl.BlockSpec((d, tn), lambda b_, i, j: (0, nblk + j)),
                  pl.BlockSpec((d, tn), lambda b_, i, j: (0, 2 * nblk + j)),
                  pl.BlockSpec((3, 1, tn), lambda b_, i, j: (0, 0, j)),
                  pl.BlockSpec((3, 3, tn), lambda b_, i, j: (0, 0, j)),
                  pl.BlockSpec((3, 1, tn), lambda b_, i, j: (0, 0, j))],
        out_specs=[out_spec, out_spec],
        out_shape=[jax.ShapeDtypeStruct((bsz, L, H_WIDTH), BF16),
                   jax.ShapeDtypeStruct((bsz, L, H_WIDTH), F32)],
        compiler_params=_cparams("parallel", "parallel", "arbitrary"),
        name="proj_hyena",
    )(h, w, w, w, b3, wc3, bc3)


N_GATES = 4 * M_HEADS


def _split3(x):
    hi = x.astype(BF16)
    r1 = x - hi.astype(F32)
    mid = r1.astype(BF16)
    lo = (r1 - mid.astype(F32)).astype(BF16)
    return hi, mid, lo


def _log_sigmoid(x):
    return jnp.minimum(x, 0.0) - jnp.log1p(jnp.exp(-jnp.abs(x)))


def _gates_kernel(h_ref, w_ref, wt_ref, b_ref, bt_ref, bc_ref, ac_ref, ar_ref):
    h = h_ref[...]
    t = h.shape[0]
    z = jnp.dot(h, w_ref[...], preferred_element_type=F32) + b_ref[...]
    zt = lax.dot_general(wt_ref[...], h, (((1,), (1,)), ((), ())),
                         preferred_element_type=F32) + bt_ref[...]
    r = lax.broadcasted_iota(jnp.int32, (t, t), 0)
    c = lax.broadcasted_iota(jnp.int32, (t, t), 1)
    lower = (r >= c).astype(BF16)
    upper = (r <= c).astype(BF16)
    g8 = FG_LANE0

    lf = _log_sigmoid(z)
    lane = lax.broadcasted_iota(jnp.int32, z.shape, 1)
    is_fg = (lane >= g8) & (lane < 2 * g8)
    terms = [jnp.where(is_fg, p.astype(F32), 0.0) for p in _split3(lf)]
    packed = terms[0] + pltpu.roll(terms[1], 2 * g8, 1) + pltpu.roll(terms[2], 4 * g8, 1)
    cfp = jnp.dot(lower, packed.astype(BF16), preferred_element_type=F32)
    cf = cfp + pltpu.roll(cfp, LANES - 2 * g8, 1) + pltpu.roll(cfp, LANES - 4 * g8, 1)
    cb = cf[t - 1:t, :] - cf + lf
    bc = jnp.where(lane < g8 + M_HEADS, cf, cb)
    bc = pltpu.roll(bc, LANES - g8, 1)
    bc_ref[...] = bc
    ac_ref[...] = z - bc

    lft = _log_sigmoid(zt[g8:, :])
    stacked = jnp.concatenate([p.astype(F32) for p in _split3(lft)] + [jnp.zeros_like(lft)], axis=0)
    cft3 = jnp.dot(stacked.astype(BF16), upper, preferred_element_type=F32)
    cft = cft3[0:g8] + cft3[g8:2 * g8] + cft3[2 * g8:3 * g8]
    cbt = cft[:, t - 1:t] - cft + lft
    row = lax.broadcasted_iota(jnp.int32, cft.shape, 0)
    ar_ref[...] = zt[:g8, :] - jnp.where(row < M_HEADS, cft, cbt)


FG_LANE0 = 2 * M_HEADS


def _gates(h, w_g, b_g, chunk):
    bsz, L, d = h.shape
    w_pad = jnp.zeros((d, LANES), F32).at[:, :N_GATES].set(w_g).astype(BF16)
    b_pad = jnp.zeros((1, LANES), F32).at[0, :N_GATES].set(b_g)
    wt = w_g.T.astype(BF16)
    bt = b_g.reshape(N_GATES, 1)
    tok = pl.BlockSpec((None, chunk, LANES), lambda b_, i: (b_, i, 0))
    return pl.pallas_call(
        _gates_kernel,
        grid=(bsz, L // chunk),
        in_specs=[pl.BlockSpec((None, chunk, d), lambda b_, i: (b_, i, 0)),
                  pl.BlockSpec((d, LANES), lambda b_, i: (0, 0)),
                  pl.BlockSpec((N_GATES, d), lambda b_, i: (0, 0)),
                  pl.BlockSpec((1, LANES), lambda b_, i: (0, 0)),
                  pl.BlockSpec((N_GATES, 1), lambda b_, i: (0, 0))],
        out_specs=[tok, tok, pl.BlockSpec((None, FG_LANE0, chunk), lambda b_, i: (b_, 0, i))],
        out_shape=[jax.ShapeDtypeStruct((bsz, L, LANES), F32),
                   jax.ShapeDtypeStruct((bsz, L, LANES), F32),
                   jax.ShapeDtypeStruct((bsz, FG_LANE0, L), F32)],
        compiler_params=_cparams("parallel", "parallel"),
        name="mlstm_gates",
    )(h, w_pad, wt, b_pad, bt)


def _mlstm_kernel(*refs, emit_h, n_chunks):
    if emit_h:
        (q_ref, k_ref, v_ref, bc_ref, ac_ref, ar_ref, c0_ref, n0_ref, m0_ref,
         h_ref, cf_ref, nf_ref, mf_ref, c_sc, n_sc, m_sc) = refs
    else:
        (k_ref, v_ref, bc_ref, ac_ref, ar_ref, c0_ref, n0_ref, m0_ref,
         cf_ref, nf_ref, mf_ref, c_sc, n_sc, m_sc) = refs
    d = pl.program_id(1)
    j = pl.program_id(2)
    fwd = d == 0
    t = k_ref.shape[0]
    dh = M_HEAD_DIM

    @pl.when(j == 0)
    def _():
        c_sc[...] = c0_ref[...]
        n_sc[...] = n0_ref[...]
        m_sc[...] = m0_ref[...]

    r = lax.broadcasted_iota(jnp.int32, (t, t), 0)
    c = lax.broadcasted_iota(jnp.int32, (t, t), 1)
    causal = jnp.where(fwd, r - c, c - r) >= 0
    bc_all = bc_ref[...]
    ac_all = ac_ref[...]
    ar_all = ar_ref[...]
    for hd in range(M_HEADS):
        sl = slice(hd * dh, (hd + 1) * dh)
        bc = jnp.where(fwd, bc_all[:, hd:hd + 1], bc_all[:, M_HEADS + hd:M_HEADS + hd + 1])
        ac = jnp.where(fwd, ac_all[:, hd:hd + 1], ac_all[:, M_HEADS + hd:M_HEADS + hd + 1])
        ar = jnp.where(fwd, ar_all[hd:hd + 1, :], ar_all[M_HEADS + hd:M_HEADS + hd + 1, :])
        b_tot = jnp.where(fwd, bc[t - 1:t, :], bc[0:1, :])
        m_prev = m_sc[hd][:, 0:1]
        k_h = k_ref[:, sl]
        v_h = v_ref[:, sl]
        if emit_h:
            q_h = q_ref[:, sl]
            dm = jnp.where(causal, bc + ar, NEG_BIG)
            inter = bc + m_prev
            m_t = jnp.maximum(inter, jnp.max(dm, axis=1, keepdims=True))
            qk = lax.dot_general(q_h, k_h, (((1,), (1,)), ((), ())), preferred_element_type=F32)
            s = qk * jnp.exp(dm - m_t)
            carry = jnp.exp(inter - m_t)
            num = (jnp.dot(s.astype(BF16), v_h, preferred_element_type=F32)
                   + carry * jnp.dot(q_h, c_sc[hd].astype(BF16), preferred_element_type=F32))
            den = (jnp.sum(s, axis=1, keepdims=True)
                   + carry * jnp.sum(q_h.astype(F32) * n_sc[hd], axis=1, keepdims=True))
            h_ref[:, sl] = (num / jnp.maximum(jnp.abs(den), jnp.exp(-m_t))).astype(h_ref.dtype)
        g = b_tot + ac
        m_new = jnp.maximum(b_tot + m_prev, jnp.max(g, axis=0, keepdims=True))
        wgt = jnp.exp(g - m_new)
        decay = jnp.exp(b_tot + m_prev - m_new)
        kw = k_h.astype(F32) * wgt
        c_sc[hd] = decay * c_sc[hd] + lax.dot_general(kw.astype(BF16), v_h, (((0,), (0,)), ((), ())),
                                                      preferred_element_type=F32)
        n_sc[hd] = decay * n_sc[hd] + jnp.sum(kw, axis=0, keepdims=True)
        m_sc[hd] = jnp.broadcast_to(m_new, (1, LANES))

    @pl.when(j == n_chunks - 1)
    def _():
        cf_ref[...] = c_sc[...]
        nf_ref[...] = n_sc[...]
        mf_ref[...] = m_sc[...]


def _mlstm(q, k, v, bc, ac, ar, state, emit_h, t):
    bsz, L, _ = k[0].shape
    nc = L // t
    seq = lambda b_, d, j: (b_, j + d * (nc - 1 - 2 * j), 0)
    st = lambda b_, d, j: (b_, d, 0, 0, 0)

    def tok(col):
        return pl.BlockSpec((None, t, M_WIDTH), lambda b_, d, j: (b_, j + d * (nc - 1 - 2 * j), col))

    gate_spec = pl.BlockSpec((None, t, LANES), seq)
    ar_spec = pl.BlockSpec((None, FG_LANE0, t), lambda b_, d, j: (b_, 0, j + d * (nc - 1 - 2 * j)))
    c_spec = pl.BlockSpec((None, None, M_HEADS, M_HEAD_DIM, M_HEAD_DIM), st)
    n_spec = pl.BlockSpec((None, None, M_HEADS, 1, M_HEAD_DIM), st)
    m_spec = pl.BlockSpec((None, None, M_HEADS, 1, LANES), st)
    state_shapes = [jax.ShapeDtypeStruct((bsz, 2, M_HEADS, M_HEAD_DIM, M_HEAD_DIM), F32),
                    jax.ShapeDtypeStruct((bsz, 2, M_HEADS, 1, M_HEAD_DIM), F32),
                    jax.ShapeDtypeStruct((bsz, 2, M_HEADS, 1, LANES), F32)]
    in_specs = [tok(k[1]), tok(v[1]), gate_spec, gate_spec, ar_spec, c_spec, n_spec, m_spec]
    args = [k[0], v[0], bc, ac, ar, *state]
    out_specs = [c_spec, n_spec, m_spec]
    out_shape = list(state_shapes)
    if emit_h:
        in_specs = [tok(q[1])] + in_specs
        args = [q[0]] + args
        out_specs = [pl.BlockSpec((None, None, t, M_WIDTH),
                                  lambda b_, d, j: (d, b_, j + d * (nc - 1 - 2 * j), 0))] + out_specs
        out_shape = [jax.ShapeDtypeStruct((2, bsz, L, M_WIDTH), BF16)] + out_shape
    outs = pl.pallas_call(
        functools.partial(_mlstm_kernel, emit_h=emit_h, n_chunks=nc),
        grid=(bsz, 2, nc),
        in_specs=in_specs,
        out_specs=out_specs,
        out_shape=out_shape,
        scratch_shapes=[pltpu.VMEM((M_HEADS, M_HEAD_DIM, M_HEAD_DIM), F32),
                        pltpu.VMEM((M_HEADS, 1, M_HEAD_DIM), F32),
                        pltpu.VMEM((M_HEADS, 1, LANES), F32)],
        compiler_params=_cparams("parallel", "parallel", "arbitrary"),
        name="mlstm" if emit_h else "mlstm_state",
    )(*args)
    if emit_h:
        return outs[0], tuple(outs[1:])
    return None, tuple(outs)


DFT_M_TILE = 8
DFT_C_TILE = 512
FEAT_ROWS = 16


def _filter_outer_kernel(bands_ref, w1t_ref, b1_ref, w2t_ref, b2_ref, w3p_ref, w3f_ref, fr_ref, dl_ref, l_ref,
                         a_ref, ss_ref, *, L, n1, n2):
    i = pl.program_id(0)
    h = n1 // 2
    cols = DFT_M_TILE * h

    def positions(shape, axis, side):
        q = lax.broadcasted_iota(jnp.int32, shape, axis)
        mm, jj = q // h, q % h
        n = n2 * (jj + side * h) + i * DFT_M_TILE + mm
        return n, jnp.where(n < L, n, 2 * L - n).astype(F32)

    taps = []
    sumsq = jnp.zeros((1, a_ref.shape[-1]), F32)
    for side, w3_ref in ((0, w3p_ref), (1, w3f_ref)):
        _, p_row = positions((1, cols), 1, side)
        t_row = p_row / float(max(L - 1, 1))
        ang = ((2 * math.pi / L) * p_row) * bands_ref[...]
        row = lax.broadcasted_iota(jnp.int32, (FEAT_ROWS, cols), 0)
        feats = jnp.concatenate([jnp.where(row == 0, t_row, 0.0), jnp.cos(ang), -jnp.sin(ang)], axis=0)
        fr = fr_ref[...]
        hid = jnp.sin(fr * (jnp.dot(w1t_ref[...], feats.astype(BF16), preferred_element_type=F32) + b1_ref[...]))
        hid = jnp.sin(fr * (jnp.dot(w2t_ref[...], hid.astype(BF16), preferred_element_type=F32) + b2_ref[...]))
        filt = lax.dot_general(hid.astype(BF16), w3_ref[...], (((0,), (0,)), ((), ())),
                               preferred_element_type=F32)
        n_col, p_col = positions((cols, 1), 0, side)
        t_col = p_col / float(max(L - 1, 1))
        kern = filt * jnp.exp(-t_col * jnp.abs(dl_ref[...]))
        kern = jnp.where(n_col == L, 0.0, kern)
        sumsq = sumsq + jnp.sum(kern * kern, axis=0, keepdims=True)
        taps.append(kern)

    for mm in range(DFT_M_TILE):
        x = jnp.concatenate([taps[0][mm * h:(mm + 1) * h], taps[1][mm * h:(mm + 1) * h]], axis=0)
        out = jnp.dot(l_ref[...], x.astype(BF16), preferred_element_type=F32)
        a_ref[0, :, mm, :] = out[:n1]
        a_ref[1, :, mm, :] = out[n1:]

    @pl.when(i == 0)
    def _():
        ss_ref[...] = jnp.zeros_like(ss_ref)

    ss_ref[...] += sumsq


def _filter_outer(L, n1, n2, fwd_r, w1, b1, w2, b2, w3, freq):
    hid = H_FILTER_HIDDEN
    bands = jnp.linspace(1e-4, H_POS_BANDS - 1, H_POS_BANDS, dtype=F32).reshape(H_POS_BANDS, 1)
    w1t = jnp.zeros((hid, 3 * FEAT_ROWS), F32)
    w1t = w1t.at[:, 0].set(w1[0]).at[:, FEAT_ROWS:2 * FEAT_ROWS].set(w1[1:1 + H_POS_BANDS].T)
    w1t = w1t.at[:, 2 * FEAT_ROWS:].set(w1[1 + H_POS_BANDS:].T).astype(BF16)
    w3h = w3.astype(BF16)
    max_decay = math.log(H_DECAY_TARGET) / H_FAST_DECAY_PCT
    min_decay = math.log(H_DECAY_TARGET) / H_SLOW_DECAY_PCT
    deltas = jnp.linspace(min_decay, max_decay, H_WIDTH, dtype=F32).reshape(1, H_WIDTH)
    col = lambda v: v.reshape(hid, 1)
    full = lambda a: pl.BlockSpec(a.shape, lambda i: (0,) * a.ndim)
    args = [bands, w1t, col(b1), w2.T.astype(BF16), col(b2)]
    return pl.pallas_call(
        functools.partial(_filter_outer_kernel, L=L, n1=n1, n2=n2),
        grid=(n2 // DFT_M_TILE,),
        in_specs=[full(a) for a in args]
        + [pl.BlockSpec((hid, H_WIDTH), lambda i: (0, 0)), pl.BlockSpec((hid, H_WIDTH), lambda i: (0, 1)),
           full(col(freq)), full(deltas), full(fwd_r)],
        out_specs=[pl.BlockSpec((2, n1, DFT_M_TILE, H_WIDTH), lambda i: (0, 0, i, 0)),
                   pl.BlockSpec((1, H_WIDTH), lambda i: (0, 0))],
        out_shape=[jax.ShapeDtypeStruct((2, n1, n2, H_WIDTH), F32),
                   jax.ShapeDtypeStruct((1, H_WIDTH), F32)],
        compiler_params=_cparams("arbitrary"),
        name="hyena_filter_outer",
    )(*args, w3h, w3h, col(freq), deltas, fwd_r)


def _dft_factors(n):
    lg = int(round(math.log2(n)))
    n1 = 1 << ((lg + 1) // 2)
    return n1, n // n1


def _dft_outer_matrices(n1):
    k = np.arange(n1)[:, None]
    n = np.arange(n1)[None, :]
    ang = 2.0 * np.pi * ((k * n) % n1) / n1
    cr, ci = np.cos(ang), -np.sin(ang)
    h = n1 // 2
    fwd_c = np.block([[cr[:, :h], -ci[:, :h]], [ci[:, :h], cr[:, :h]]])
    fwd_r = np.concatenate([cr, ci], axis=0)
    ir, ii = cr[:h, :], -ci[:h, :]
    inv = np.block([[ir, -ii], [ii, ir]])
    return (jnp.asarray(fwd_c, F32).astype(BF16), jnp.asarray(fwd_r, F32).astype(BF16),
            jnp.asarray(inv, F32).astype(BF16))


def _dft_inner_matrices(n1, n2):
    n = n1 * n2
    k2 = np.arange(n2)[:, None]
    m = np.arange(n2)[None, :]
    ang = 2.0 * np.pi * ((k2 * m) % n2) / n2
    fr, fi = np.cos(ang), -np.sin(ang)
    f = np.block([[fr, -fi], [fi, fr]])
    k1 = jnp.arange(n1, dtype=jnp.int32)[:, None]
    tw_ang = ((jnp.arange(n2, dtype=jnp.int32)[None, :] * k1) % n).astype(F32) * (2.0 * math.pi / n)
    rep = lambda t: jnp.broadcast_to(t[:, :, None], (n1, n2, LANES))
    return (jnp.asarray(f, F32).astype(BF16), jnp.asarray(f.T, F32).astype(BF16),
            rep(jnp.cos(tw_ang)), rep(-jnp.sin(tw_ang)))


def _outer_dft_kernel(l_ref, x_ref, o_ref):
    p_in, p_out = x_ref.shape[0], o_ref.shape[0]
    r_out = o_ref.shape[1]
    for mm in range(x_ref.shape[2]):
        parts = [x_ref[p, :, mm, :] for p in range(p_in)]
        x = parts[0] if p_in == 1 else jnp.concatenate(parts, axis=0)
        out = jnp.dot(l_ref[...], x.astype(BF16), preferred_element_type=F32)
        for p in range(p_out):
            o_ref[p, :, mm, :] = out[p * r_out:(p + 1) * r_out]


def _outer_dft(lmat, x4, p_out):
    p_in, r_in, n2, c = x4.shape
    r_out = lmat.shape[0] // p_out
    tc = min(DFT_C_TILE, c)
    return pl.pallas_call(
        _outer_dft_kernel,
        grid=(n2 // DFT_M_TILE, c // tc),
        in_specs=[pl.BlockSpec(lmat.shape, lambda m, j: (0, 0)),
                  pl.BlockSpec((p_in, r_in, DFT_M_TILE, tc), lambda m, j: (0, 0, m, j))],
        out_specs=pl.BlockSpec((p_out, r_out, DFT_M_TILE, tc), lambda m, j: (0, 0, m, j)),
        out_shape=jax.ShapeDtypeStruct((p_out, r_out, n2, c), F32),
        compiler_params=_cparams("parallel", "parallel"),
        name="dft_outer",
    )(lmat, x4)


DFT_K_TILE = 2


def _twiddled_inner_dft(f_ref, twr_ref, twi_ref, a_ref, kk):
    n2, c = a_ref.shape[2], a_ref.shape[3]
    twr = jnp.tile(twr_ref[kk], (1, c // LANES))
    twi = jnp.tile(twi_ref[kk], (1, c // LANES))
    ar, ai = a_ref[0, kk], a_ref[1, kk]
    a = jnp.concatenate([(ar * twr - ai * twi).astype(BF16), (ar * twi + ai * twr).astype(BF16)], axis=0)
    x = jnp.dot(f_ref[...], a, preferred_element_type=F32)
    return x[:n2], x[n2:], twr, twi


def _inner_fwd_kernel(f_ref, twr_ref, twi_ref, a_ref, o_ref):
    for kk in range(a_ref.shape[1]):
        xr, xi, _, _ = _twiddled_inner_dft(f_ref, twr_ref, twi_ref, a_ref, kk)
        o_ref[0, kk] = xr.astype(o_ref.dtype)
        o_ref[1, kk] = xi.astype(o_ref.dtype)


def _inner_specs(n1, n2, c):
    blk = pl.BlockSpec((2, DFT_K_TILE, n2, c), lambda k: (0, k, 0, 0))
    mat = pl.BlockSpec((2 * n2, 2 * n2), lambda k: (0, 0))
    tw = pl.BlockSpec((DFT_K_TILE, n2, LANES), lambda k: (k, 0, 0))
    return blk, mat, tw


def _inner_fwd(f, twr, twi, a):
    _, n1, n2, c = a.shape
    blk, mat, tw = _inner_specs(n1, n2, c)
    return pl.pallas_call(
        _inner_fwd_kernel,
        grid=(n1 // DFT_K_TILE,),
        in_specs=[mat, tw, tw, blk],
        out_specs=blk,
        out_shape=jax.ShapeDtypeStruct((2, n1, n2, c), BF16),
        compiler_params=_cparams("parallel"),
        name="dft_inner_filter",
    )(f, twr, twi, a)


def _inner_conv_kernel(f_ref, ft_ref, twr_ref, twi_ref, a_ref, k_ref, o_ref):
    n2 = a_ref.shape[2]
    for kk in range(a_ref.shape[1]):
        xr, xi, twr, twi = _twiddled_inner_dft(f_ref, twr_ref, twi_ref, a_ref, kk)
        kr, ki = k_ref[0, kk].astype(F32), k_ref[1, kk].astype(F32)
        yr = xr * kr - xi * ki
        yi = xr * ki + xi * kr
        y = jnp.concatenate([yr.astype(BF16), yi.astype(BF16)], axis=0)
        b = jnp.dot(ft_ref[...], y, preferred_element_type=F32)
        br, bi = b[:n2], b[n2:]
        o_ref[0, kk] = br * twr + bi * twi
        o_ref[1, kk] = bi * twr - br * twi


def _inner_conv(f, ft, twr, twi, a, kf):
    _, n1, n2, c = a.shape
    blk, mat, tw = _inner_specs(n1, n2, c)
    return pl.pallas_call(
        _inner_conv_kernel,
        grid=(n1 // DFT_K_TILE,),
        in_specs=[mat, mat, tw, tw, blk, blk],
        out_specs=blk,
        out_shape=jax.ShapeDtypeStruct((2, n1, n2, c), F32),
        compiler_params=_cparams("parallel"),
        name="dft_inner_conv",
    )(f, ft, twr, twi, a, kf)


def _hyena_long_conv(s, w1, b1, w2, b2, w3, freq):
    bsz, L, c = s.shape
    assert bsz == 2
    n = 2 * L
    n1, n2 = _dft_factors(n)
    fwd_c, fwd_r, inv = _dft_outer_matrices(n1)
    f, ft, twr, twi = _dft_inner_matrices(n1, n2)
    af, sumsq = _filter_outer(L, n1, n2, fwd_r, w1, b1, w2, b2, w3, freq)
    kf = _inner_fwd(f, twr, twi, af)
    a = _outer_dft(fwd_c, s.reshape(2, n1 // 2, n2, c), 2)
    b = _inner_conv(f, ft, twr, twi, a, kf)
    y = _outer_dft(inv, b, 2)
    return y.reshape(2, L, c), sumsq


def _pack_bf16_pairs(x):
    half = x.shape[1] // 2
    lo = pltpu.bitcast(x[:, :half].astype(BF16).astype(F32), jnp.uint32) >> 16
    hi = pltpu.bitcast(x[:, half:].astype(BF16).astype(F32), jnp.uint32) & jnp.uint32(0xFFFF0000)
    return lo | hi


def _unpack_bf16_pairs(p):
    lo = pltpu.bitcast(p << 16, F32).astype(BF16)
    hi = pltpu.bitcast(p & jnp.uint32(0xFFFF0000), F32).astype(BF16)
    return jnp.concatenate([lo, hi], axis=1)


def _merge_kernel(hf_ref, hb_ref, o_ref, x0_ref, s_ref, y_ref, ga_ref, gb_ref, x_ref,
                  ysc_ref, hbias_ref, gate_ref, g2_ref, sh_ref, sc_ref,
                  wa_ref, wb_ref, wo_ref, x1_ref, h2_ref):
    a = o_ref[...].astype(F32) * (hf_ref[...].astype(F32) + hb_ref[...].astype(F32))
    s = s_ref[...]
    hy = x0_ref[...].astype(F32) * (y_ref[...] * ysc_ref[...] + hbias_ref[...] * s)
    pa = jnp.dot(a.astype(BF16), wa_ref[...], preferred_element_type=F32)
    pb = jnp.dot(hy.astype(BF16), wb_ref[...], preferred_element_type=F32)
    mix = ga_ref[...].astype(F32) * pa + gb_ref[...].astype(F32) * pb
    out = jnp.dot(mix.astype(BF16), wo_ref[...], preferred_element_type=F32)
    x1 = x_ref[...] + gate_ref[...] * out
    x1_ref[...] = x1
    y = x1 * lax.rsqrt(jnp.mean(x1 * x1, axis=-1, keepdims=True) + EPS) * g2_ref[...]
    h2_ref[...] = _pack_bf16_pairs(y * (1.0 + sc_ref[...]) + sh_ref[...])


def _merge(hdirs, pm, x0, s, y, x, yscale, h_bias, gate1, g2, shift2, scale2, w_a, w_b, w_out, tm=256):
    bsz, L, d = x.shape
    tok = pl.BlockSpec((None, tm, d), lambda b, i: (b, i, 0))

    def pm_tile(col):
        return pl.BlockSpec((None, tm, d), lambda b, i: (b, i, col))

    vec = pl.BlockSpec((1, d), lambda b, i: (0, 0))
    bvec = pl.BlockSpec((None, 1, d), lambda b, i: (b, 0, 0))
    wsp = pl.BlockSpec((d, d), lambda b, i: (0, 0))
    return pl.pallas_call(
        _merge_kernel,
        grid=(bsz, L // tm),
        in_specs=[pl.BlockSpec((None, None, tm, d), lambda b, i: (0, b, i, 0)),
                  pl.BlockSpec((None, None, tm, d), lambda b, i: (1, b, i, 0)),
                  pm_tile(PM_O), tok, tok, tok, pm_tile(PM_GA), pm_tile(PM_GB), tok,
                  vec, vec, bvec, vec, bvec, bvec, wsp, wsp, wsp],
        out_specs=[tok, pl.BlockSpec((None, tm, d // 2), lambda b, i: (b, i, 0))],
        out_shape=[jax.ShapeDtypeStruct((bsz, L, d), F32), jax.ShapeDtypeStruct((bsz, L, d // 2), jnp.uint32)],
        compiler_params=_cparams("parallel", "parallel"),
        name="merge",
    )(hdirs, hdirs, pm, x0, s, y, pm, pm, x, yscale, h_bias.reshape(1, d), gate1, g2.reshape(1, d),
      shift2, scale2, w_a, w_b, w_out)


MOE_BLOCK = 256
ROUTE_E1, ROUTE_E2, ROUTE_W1, ROUTE_W2 = 0, 1, 2, 3
EXP_LANE0 = N_GROUPS


def _first_lane_of_max(val, valid, lane):
    masked = jnp.where(valid, val, NEG_BIG)
    mx = jnp.max(masked, axis=1, keepdims=True)
    idx = jnp.min(jnp.where(valid & (masked == mx), lane, LANES), axis=1, keepdims=True)
    return mx, idx


def _router_kernel(h_ref, w_ref, b_ref, r_ref):
    logits = jnp.dot(_unpack_bf16_pairs(h_ref[...]), w_ref[...], preferred_element_type=F32) + b_ref[...]
    lane = lax.broadcasted_iota(jnp.int32, logits.shape, 1)
    is_g = lane < N_GROUPS
    gmax, gsel = _first_lane_of_max(logits, is_g, lane)
    gsum = jnp.sum(jnp.where(is_g, jnp.exp(logits - gmax), 0.0), axis=1, keepdims=True)
    gw = 1.0 / gsum
    lo = EXP_LANE0 + gsel * EXPERTS_PER_GROUP
    in_grp = (lane >= lo) & (lane < lo + EXPERTS_PER_GROUP)
    emax, l1 = _first_lane_of_max(logits, in_grp, lane)
    esum = jnp.sum(jnp.where(in_grp, jnp.exp(logits - emax), 0.0), axis=1, keepdims=True)
    e2max, l2 = _first_lane_of_max(logits, in_grp & (lane != l1), lane)
    v1 = 1.0 / esum
    v2 = jnp.exp(e2max - emax) / esum
    vs = v1 + v2
    w1 = gw * v1 / vs
    w2 = gw * v2 / vs
    e1 = (l1 - EXP_LANE0).astype(F32)
    e2 = (l2 - EXP_LANE0).astype(F32)
    r_ref[...] = jnp.where(lane == ROUTE_E1, e1,
                           jnp.where(lane == ROUTE_E2, e2,
                                     jnp.where(lane == ROUTE_W1, w1,
                                               jnp.where(lane == ROUTE_W2, w2, 0.0))))


def _router(h2, w_group, b_group, w_router, b_router, tm=1024):
    n, dp = h2.shape
    d = 2 * dp
    w = jnp.zeros((d, LANES), F32).at[:, :N_GROUPS].set(w_group).at[
        :, EXP_LANE0:EXP_LANE0 + N_EXPERTS].set(w_router).astype(BF16)
    b = jnp.zeros((1, LANES), F32).at[0, :N_GROUPS].set(b_group).at[
        0, EXP_LANE0:EXP_LANE0 + N_EXPERTS].set(b_router)
    return pl.pallas_call(
        _router_kernel,
        grid=(n // tm,),
        in_specs=[pl.BlockSpec((tm, dp), lambda i: (i, 0)),
                  pl.BlockSpec((d, LANES), lambda i: (0, 0)),
                  pl.BlockSpec((1, LANES), lambda i: (0, 0))],
        out_specs=pl.BlockSpec((tm, LANES), lambda i: (i, 0)),
        out_shape=jax.ShapeDtypeStruct((n, LANES), F32),
        compiler_params=_cparams("parallel"),
        name="moe_router",
    )(h2, w, b)


def _slots_kernel(r_ref, dest_ref, cnt_ref, run_sc, start_sc):
    ph = pl.program_id(0)
    i = pl.program_id(1)
    rec = r_ref[...]
    tm = rec.shape[0]
    lane = lax.broadcasted_iota(jnp.int32, rec.shape, 1)
    e1 = rec[:, ROUTE_E1:ROUTE_E1 + 1].astype(jnp.int32)
    e2 = rec[:, ROUTE_E2:ROUTE_E2 + 1].astype(jnp.int32)
    oh1 = lane == e1
    oh2 = lane == e2
    oh = (oh1 | oh2).astype(F32)

    @pl.when((ph == 0) & (i == 0))
    def _():
        run_sc[...] = jnp.zeros_like(run_sc)

    @pl.when(ph == 0)
    def _():
        run_sc[...] += jnp.sum(oh, axis=0, keepdims=True)

    @pl.when((ph == 1) & (i == 0))
    def _():
        counts = run_sc[...]
        cnt_ref[...] = counts
        nblk = jnp.floor((counts + (MOE_BLOCK - 1)) * (1.0 / MOE_BLOCK))
        rr = lax.broadcasted_iota(jnp.int32, (LANES, LANES), 0)
        cc = lax.broadcasted_iota(jnp.int32, (LANES, LANES), 1)
        before = (rr < cc).astype(BF16)
        first = jnp.dot(nblk.astype(BF16), before, preferred_element_type=F32)
        start_sc[...] = first * float(MOE_BLOCK)
        run_sc[...] = jnp.zeros_like(run_sc)

    @pl.when(ph == 1)
    def _():
        r = lax.broadcasted_iota(jnp.int32, (tm, tm), 0)
        c = lax.broadcasted_iota(jnp.int32, (tm, tm), 1)
        earlier = (r > c).astype(BF16)
        rank = jnp.dot(earlier, oh.astype(BF16), preferred_element_type=F32) + run_sc[...] + start_sc[...]
        d1 = jnp.sum(jnp.where(oh1, rank, 0.0), axis=1, keepdims=True)
        d2 = jnp.sum(jnp.where(oh2, rank, 0.0), axis=1, keepdims=True)
        dest_ref[...] = jnp.where(lane == 0, d1, jnp.where(lane == 1, d2, 0.0)).astype(jnp.int32)
        run_sc[...] += jnp.sum(oh, axis=0, keepdims=True)


def _slots(route, tm=512):
    n = route.shape[0]
    return pl.pallas_call(
        _slots_kernel,
        grid=(2, n // tm),
        in_specs=[pl.BlockSpec((tm, LANES), lambda p, i: (i, 0))],
        out_specs=[pl.BlockSpec((tm, LANES), lambda p, i: (i * p, 0)),
                   pl.BlockSpec((1, LANES), lambda p, i: (0, 0))],
        out_shape=[jax.ShapeDtypeStruct((n, LANES), jnp.int32), jax.ShapeDtypeStruct((1, LANES), F32)],
        scratch_shapes=[pltpu.VMEM((1, LANES), F32), pltpu.VMEM((1, LANES), F32)],
        compiler_params=_cparams("arbitrary", "arbitrary"),
        name="moe_slots",
    )(route)


def _experts_kernel(be_ref, first_ref, nxt_ref, par_ref, nu_ref, x_ref, w1_hbm, w3_hbm, w2_hbm, o_ref,
                    w1f, w3f, w2f, w1b, w3b, w2b, sems):
    i = pl.program_id(0)

    def weight_copies(e, slot):
        return (pltpu.make_async_copy(w1_hbm.at[e], w1f.at[slot], sems.at[0, slot]),
                pltpu.make_async_copy(w3_hbm.at[e], w3f.at[slot], sems.at[1, slot]),
                pltpu.make_async_copy(w2_hbm.at[e], w2f.at[slot], sems.at[2, slot]))

    @pl.when(i == 0)
    def _():
        for cp in weight_copies(be_ref[0], 0):
            cp.start()

    @pl.when(first_ref[i] == 1)
    def _():
        slot = par_ref[i]

        @pl.when(nxt_ref[i] >= 0)
        def _():
            for cp in weight_copies(nxt_ref[i], 1 - slot):
                cp.start()

        for cp in weight_copies(be_ref[i], slot):
            cp.wait()
        w1b[...] = w1f[slot].astype(BF16)
        w3b[...] = w3f[slot].astype(BF16)
        w2b[...] = w2f[slot].astype(BF16)

    @pl.when(i < nu_ref[0])
    def _():
        x = _unpack_bf16_pairs(x_ref[...])
        a = jnp.dot(x, w1b[...], preferred_element_type=F32)
        b = jnp.dot(x, w3b[...], preferred_element_type=F32)
        hmid = (a * jax.nn.sigmoid(a)) * b
        o_ref[...] = _pack_bf16_pairs(jnp.dot(hmid.astype(BF16), w2b[...], preferred_element_type=F32))

    @pl.when(i >= nu_ref[0])
    def _():
        o_ref[...] = jnp.zeros_like(o_ref)


def _experts(xs, nb, block_e, n_used, w1_e, w3_e, w2_e):
    dp = xs.shape[1]
    d, de = w1_e.shape[1], w1_e.shape[2]
    idx = jnp.arange(nb, dtype=jnp.int32)
    used = idx < n_used[0]
    first = used & ((idx == 0) | (block_e != jnp.roll(block_e, 1)))
    ordinal = jnp.cumsum(first.astype(jnp.int32)) - 1
    par = (ordinal % 2).astype(jnp.int32)
    first_pos = jnp.where(first, idx, nb)
    next_first = lax.cummin(jnp.concatenate([first_pos[1:], jnp.full((1,), nb, jnp.int32)]), reverse=True)
    nxt = jnp.where(next_first < nb, block_e[jnp.minimum(next_first, nb - 1)], -1).astype(jnp.int32)
    any_spec = pl.BlockSpec(memory_space=pl.ANY)
    grid_spec = pltpu.PrefetchScalarGridSpec(
        num_scalar_prefetch=5,
        grid=(nb,),
        in_specs=[pl.BlockSpec((MOE_BLOCK, dp), lambda i, *_: (i, 0)), any_spec, any_spec, any_spec],
        out_specs=pl.BlockSpec((MOE_BLOCK, dp), lambda i, *_: (i, 0)),
        scratch_shapes=[pltpu.VMEM((2, d, de), F32), pltpu.VMEM((2, d, de), F32), pltpu.VMEM((2, de, d), F32),
                        pltpu.VMEM((d, de), BF16), pltpu.VMEM((d, de), BF16), pltpu.VMEM((de, d), BF16),
                        pltpu.SemaphoreType.DMA((3, 2))],
    )
    return pl.pallas_call(
        _experts_kernel,
        grid_spec=grid_spec,
        out_shape=jax.ShapeDtypeStruct((nb * MOE_BLOCK, dp), xs.dtype),
        compiler_params=_cparams("arbitrary"),
        name="moe_experts",
    )(block_e, first.astype(jnp.int32), nxt, par, n_used, xs, w1_e, w3_e, w2_e)


SC_WINDOW = 128
SC_CORES, SC_SUBCORES = 2, 16
SC_WORKERS = SC_CORES * SC_SUBCORES


def _sc_worker_id():
    return lax.axis_index("c") * SC_SUBCORES + lax.axis_index("s")


def _sc_mesh():
    return plsc.VectorSubcoreMesh(core_axis_name="c", subcore_axis_name="s")


def _sc_dispatch(rows, dest0, dest1, pad_slots, n_rows):
    n, dv = rows.shape
    nwin, pwin = n // SC_WINDOW, pad_slots.shape[0] // SC_WINDOW
    assert n % (SC_WINDOW * SC_WORKERS) == 0 and pad_slots.shape[0] % (SC_WINDOW * SC_WORKERS) == 0
    zeros = jnp.zeros((SC_WINDOW, dv), rows.dtype)

    @pl.kernel(out_type=jax.ShapeDtypeStruct((n_rows, dv), rows.dtype), mesh=_sc_mesh(),
               scratch_types=[pltpu.VMEM((1, SC_WINDOW), jnp.int32), pltpu.VMEM((SC_WINDOW, dv), rows.dtype)],
               name="moe_dispatch_sc")
    def scatter(x_hbm, d0_hbm, d1_hbm, p_hbm, z_hbm, o_hbm, idx, buf):
        wid = _sc_worker_id()
        pltpu.sync_copy(z_hbm, buf)

        @pl.loop(0, pwin // SC_WORKERS)
        def _(t):
            w = t * SC_WORKERS + wid
            pltpu.sync_copy(p_hbm.at[pl.ds(w, 1)], idx)
            pltpu.sync_copy(buf, o_hbm.at[idx.at[0]])

        @pl.loop(0, nwin // SC_WORKERS)
        def _(t):
            w = t * SC_WORKERS + wid
            pltpu.sync_copy(x_hbm.at[pl.ds(w * SC_WINDOW, SC_WINDOW)], buf)
            for d_hbm in (d0_hbm, d1_hbm):
                pltpu.sync_copy(d_hbm.at[pl.ds(w, 1)], idx)
                pltpu.sync_copy(buf, o_hbm.at[idx.at[0]])

    return scatter(rows, dest0.reshape(nwin, SC_WINDOW), dest1.reshape(nwin, SC_WINDOW),
                   pad_slots.reshape(pwin, SC_WINDOW), zeros)


def _sc_gather(table, index):
    m = index.shape[0]
    dv = table.shape[1]
    nwin = m // SC_WINDOW
    assert m % (SC_WINDOW * SC_WORKERS) == 0

    @pl.kernel(out_type=jax.ShapeDtypeStruct((m, dv), table.dtype), mesh=_sc_mesh(),
               scratch_types=[pltpu.VMEM((1, SC_WINDOW), jnp.int32), pltpu.VMEM((SC_WINDOW, dv), table.dtype)],
               name="moe_gather_sc")
    def gather(x_hbm, i_hbm, o_hbm, idx, buf):
        wid = _sc_worker_id()

        @pl.loop(0, nwin // SC_WORKERS)
        def _(t):
            w = t * SC_WORKERS + wid
            pltpu.sync_copy(i_hbm.at[pl.ds(w, 1)], idx)
            pltpu.sync_copy(x_hbm.at[idx.at[0]], buf)
            pltpu.sync_copy(buf, o_hbm.at[pl.ds(w * SC_WINDOW, SC_WINDOW)])

    return gather(table, index.reshape(nwin, SC_WINDOW))


def _combine_planes_kernel(r_ref, ya_ref, yb_ref, x_ref, gate_ref, gf_ref, o_ref):
    rec = r_ref[...]
    y = (_unpack_bf16_pairs(ya_ref[...]).astype(F32) * rec[:, ROUTE_W1:ROUTE_W1 + 1]
         + _unpack_bf16_pairs(yb_ref[...]).astype(F32) * rec[:, ROUTE_W2:ROUTE_W2 + 1])
    x2 = x_ref[...] + gate_ref[...] * y
    o_ref[...] = x2 * lax.rsqrt(jnp.mean(x2 * x2, axis=-1, keepdims=True) + EPS) * gf_ref[...]


def _combine_planes(g, route, x1, gate2, g_final, tm=512):
    bsz, L, d = x1.shape
    tpb = L // tm
    dp = g.shape[-1]
    return pl.pallas_call(
        _combine_planes_kernel,
        grid=(bsz, tpb),
        in_specs=[pl.BlockSpec((tm, LANES), lambda b, i: (b * tpb + i, 0)),
                  pl.BlockSpec((None, tm, dp), lambda b, i: (0, b * tpb + i, 0)),
                  pl.BlockSpec((None, tm, dp), lambda b, i: (1, b * tpb + i, 0)),
                  pl.BlockSpec((None, tm, d), lambda b, i: (b, i, 0)),
                  pl.BlockSpec((None, 1, d), lambda b, i: (b, 0, 0)),
                  pl.BlockSpec((1, d), lambda b, i: (0, 0))],
        out_specs=pl.BlockSpec((None, tm, d), lambda b, i: (b, i, 0)),
        out_shape=jax.ShapeDtypeStruct((bsz, L, d), F32),
        compiler_params=_cparams("parallel", "parallel"),
        name="moe_combine",
    )(route, g, g, x1, gate2, g_final.reshape(1, d))


def _moe(h2, x1, gate2, g_final, w_group, b_group, w_router, b_router, w1_e, w3_e, w2_e):
    bsz, L, d = x1.shape
    n = bsz * L
    h2f = h2.reshape(n, h2.shape[-1])
    route = _router(h2f, w_group, b_group, w_router, b_router)
    dest_rec, counts = _slots(route)
    nb = (2 * n) // MOE_BLOCK + N_EXPERTS
    cnt = counts[0, :N_EXPERTS].astype(jnp.int32)
    blocks_per_e = (cnt + MOE_BLOCK - 1) // MOE_BLOCK
    ends = jnp.cumsum(blocks_per_e)
    block_e = jnp.minimum(jnp.sum(ends[None, :] <= jnp.arange(nb, dtype=jnp.int32)[:, None], axis=1),
                          N_EXPERTS - 1).astype(jnp.int32)
    n_used = ends[-1:].astype(jnp.int32)
    n_slots = nb * MOE_BLOCK
    pad_j = jnp.arange(MOE_BLOCK, dtype=jnp.int32)[None, :]
    spare = n_slots + jnp.arange(N_EXPERTS * MOE_BLOCK, dtype=jnp.int32).reshape(N_EXPERTS, MOE_BLOCK)
    first_slot = ((ends - blocks_per_e) * MOE_BLOCK)[:, None]
    is_pad = cnt[:, None] + pad_j < blocks_per_e[:, None] * MOE_BLOCK
    pad_slots = jnp.where(is_pad, first_slot + cnt[:, None] + pad_j, spare).reshape(-1)
    xs = _sc_dispatch(h2f, dest_rec[:, 0], dest_rec[:, 1], pad_slots, n_slots + N_EXPERTS * MOE_BLOCK)
    ys = _experts(xs, nb, block_e, n_used, w1_e, w3_e, w2_e)
    g = _sc_gather(ys, jnp.concatenate([dest_rec[:, 0], dest_rec[:, 1]]))
    return _combine_planes(g.reshape(2, n, g.shape[-1]), route, x1, gate2, g_final)


def kernel(x, c, ctx, c_ctx, w_mod, b_mod, g_norm1, g_norm2, w_in, b_in, w_qk_conv, b_qk_conv,
           w_h_conv, b_h_conv, hf_w1, hf_b1, hf_w2, hf_b2, hf_w3, hf_freq, h_bias, w_a, w_b, w_out,
           w_group, b_group, w_router, b_router, w1_e, w3_e, w2_e, g_final):
    assert w_mod.shape[0] == 1, "single-layer block"
    (w_mod, b_mod, g_norm1, g_norm2, w_in, b_in, w_qk_conv, b_qk_conv, w_h_conv, b_h_conv, hf_w1, hf_b1, hf_w2,
     hf_b2, hf_w3, hf_freq, h_bias, w_a, w_b, w_out, w_group, b_group, w_router, b_router, w1_e, w3_e, w2_e) = (
        t[0] for t in (w_mod, b_mod, g_norm1, g_norm2, w_in, b_in, w_qk_conv, b_qk_conv, w_h_conv, b_h_conv,
                       hf_w1, hf_b1, hf_w2, hf_b2, hf_w3, hf_freq, h_bias, w_a, w_b, w_out, w_group, b_group,
                       w_router, b_router, w1_e, w3_e, w2_e))
    bsz, L, d = x.shape
    lc = ctx.shape[1]
    seg = L // (L // GRID_W)
    chunk_c = min(lc, MLSTM_CHUNK)
    assert bsz + 1 <= 8 and lc % chunk_c == 0 and L % MLSTM_CHUNK == 0

    cond = jnp.zeros((8, d), F32).at[:bsz].set(c).at[bsz].set(c_ctx)
    mod = _adaln(cond, w_mod, b_mod).reshape(8, 6, d)
    modx = mod[:bsz]
    shift1, scale1, gate1, shift2, scale2, gate2 = (modx[:, i:i + 1] for i in range(6))
    shift1c = jnp.broadcast_to(mod[bsz, 0].reshape(1, 1, d), (bsz, 1, d))
    scale1c = jnp.broadcast_to(mod[bsz, 1].reshape(1, 1, d), (bsz, 1, d))

    w_in16 = w_in.astype(BF16)
    k_scale = jnp.full((M_WIDTH,), M_HEAD_DIM ** -0.5, F32)
    qk_scale = jnp.concatenate([jnp.ones((M_WIDTH,), F32), k_scale])
    w_gates, b_gates = w_in[:, IG0:M_COLS], b_in[IG0:M_COLS]

    hc = _norm_mod(ctx, g_norm1, shift1c, scale1c, lc)
    kc = _proj_conv_silu(hc, w_in16[:, K0:V0], b_in[K0:V0], w_qk_conv[:, M_WIDTH:], b_qk_conv[M_WIDTH:],
                         k_scale, lc, lc)
    vc = _proj_act(hc, w_in16[:, V0:O0], b_in[V0:O0], "none", BF16, lc)
    bcc, acc, arc = _gates(hc, w_gates, b_gates, chunk_c)
    zero_state = (jnp.zeros((bsz, 2, M_HEADS, M_HEAD_DIM, M_HEAD_DIM), F32),
                  jnp.zeros((bsz, 2, M_HEADS, 1, M_HEAD_DIM), F32),
                  jnp.zeros((bsz, 2, M_HEADS, 1, LANES), F32))
    _, ctx_state = _mlstm(None, (kc, 0), (vc, 0), bcc, acc, arc, zero_state, False, chunk_c)

    tm = 1024
    w_main = jnp.concatenate([w_in16[:, Q0:IG0], w_in16[:, GA0:IN_COLS]], axis=1)
    b_main = jnp.concatenate([b_in[Q0:IG0], b_in[GA0:IN_COLS]])
    pm, h = _proj_main(x, g_norm1, shift1, scale1, w_main, b_main, w_qk_conv, b_qk_conv, qk_scale, seg, tm)
    bc, ac, ar = _gates(h, w_gates, b_gates, MLSTM_CHUNK)
    hdirs, _ = _mlstm((pm, PM_Q), (pm, PM_K), (pm, PM_V), bc, ac, ar, ctx_state, True, MLSTM_CHUNK)

    x0, s = _proj_hyena(h, w_in16[:, HY0:GA0], b_in[HY0:GA0], w_h_conv, b_h_conv, seg, tm)
    y, sumsq = _hyena_long_conv(s, hf_w1, hf_b1, hf_w2, hf_b2, hf_w3, hf_freq)
    yscale = lax.rsqrt(sumsq + EPS) * (1.0 / (2 * L))

    x1, h2 = _merge(hdirs, pm, x0, s, y, x, yscale, h_bias, gate1, g_norm2, shift2, scale2,
                    w_a.astype(BF16), w_b.astype(BF16), w_out.astype(BF16))
    return _moe(h2, x1, gate2, g_final, w_group, b_group, w_router, b_router, w1_e, w3_e, w2_e)
```

```python
import functools
import math

import jax
import jax.numpy as jnp
import numpy as np
from jax import lax
from jax.experimental import pallas as pl
from jax.experimental.pallas import tpu as pltpu
from jax.experimental.pallas import tpu_sc as plsc

F32 = jnp.float32
BF16 = jnp.bfloat16

D_MODEL = 1024
GRID_W = 64
EPS = 1e-6
M_HEADS = 4
M_HEAD_DIM = 256
M_WIDTH = M_HEADS * M_HEAD_DIM
H_WIDTH = 1024
H_POS_BANDS = 16
H_FILTER_HIDDEN = 64
H_FAST_DECAY_PCT = 0.3
H_SLOW_DECAY_PCT = 1.5
H_DECAY_TARGET = 1e-2
N_GROUPS = 8
EXPERTS_PER_GROUP = 8
N_EXPERTS = N_GROUPS * EXPERTS_PER_GROUP
D_EXPERT = 512
Q0 = 0
K0 = Q0 + M_WIDTH
V0 = K0 + M_WIDTH
O0 = V0 + M_WIDTH
IG0 = O0 + M_WIDTH
FG0 = IG0 + 2 * M_HEADS
M_COLS = FG0 + 2 * M_HEADS
HY0 = M_COLS
GA0 = HY0 + 3 * H_WIDTH
GB0 = GA0 + D_MODEL
IN_COLS = GB0 + D_MODEL

LANES = 128
MLSTM_CHUNK = 512
NEG_BIG = -1e30
VMEM_LIMIT = 48 * 1024 * 1024


def _cparams(*sem):
    return pltpu.CompilerParams(dimension_semantics=sem, vmem_limit_bytes=VMEM_LIMIT)


def _adaln_kernel(c_ref, w_ref, b_ref, o_ref):
    s = c_ref[...]
    s = s * jax.nn.sigmoid(s)
    o_ref[...] = jnp.dot(s.astype(BF16), w_ref[...].astype(BF16), preferred_element_type=F32) + b_ref[...]


def _adaln(cond, w_mod, b_mod):
    n = w_mod.shape[1]
    tn = 1536
    return pl.pallas_call(
        _adaln_kernel,
        grid=(n // tn,),
        in_specs=[pl.BlockSpec((8, D_MODEL), lambda j: (0, 0)),
                  pl.BlockSpec((D_MODEL, tn), lambda j: (0, j)),
                  pl.BlockSpec((1, tn), lambda j: (0, j))],
        out_specs=pl.BlockSpec((8, tn), lambda j: (0, j)),
        out_shape=jax.ShapeDtypeStruct((8, n), F32),
        compiler_params=_cparams("arbitrary"),
        name="adaln",
    )(cond, w_mod, b_mod.reshape(1, n))


def _norm_mod_kernel(x_ref, g_ref, sh_ref, sc_ref, o_ref):
    x = x_ref[...]
    y = x * lax.rsqrt(jnp.mean(x * x, axis=-1, keepdims=True) + EPS)
    y = y * g_ref[...]
    o_ref[...] = (y * (1.0 + sc_ref[...]) + sh_ref[...]).astype(o_ref.dtype)


def _norm_mod(x, g, shift, scale, tm):
    bsz, L, d = x.shape
    return pl.pallas_call(
        _norm_mod_kernel,
        grid=(bsz, L // tm),
        in_specs=[pl.BlockSpec((None, tm, d), lambda b, i: (b, i, 0)),
                  pl.BlockSpec((1, d), lambda b, i: (0, 0)),
                  pl.BlockSpec((None, 1, d), lambda b, i: (b, 0, 0)),
                  pl.BlockSpec((None, 1, d), lambda b, i: (b, 0, 0))],
        out_specs=pl.BlockSpec((None, tm, d), lambda b, i: (b, i, 0)),
        out_shape=jax.ShapeDtypeStruct((bsz, L, d), BF16),
        compiler_params=_cparams("parallel", "parallel"),
        name="norm_mod",
    )(x, g.reshape(1, d), shift, scale)


def _conv3(z, wc, bc, seg):
    tm = z.shape[0]
    pos = lax.broadcasted_iota(jnp.int32, z.shape, 0) & (seg - 1)
    zp = jnp.where(pos == 0, 0.0, pltpu.roll(z, 1, 0))
    zn = jnp.where(pos == seg - 1, 0.0, pltpu.roll(z, tm - 1, 0))
    return zp * wc[0:1, :] + z * wc[1:2, :] + zn * wc[2:3, :] + bc


def _proj_act_kernel(h_ref, w_ref, b_ref, o_ref, *, act):
    z = jnp.dot(h_ref[...], w_ref[...], preferred_element_type=F32) + b_ref[...]
    if act == "sigmoid":
        z = jax.nn.sigmoid(z)
    o_ref[...] = z.astype(o_ref.dtype)


def _proj_act(h, w, b, act, out_dtype, tm, tn=512):
    bsz, L, d = h.shape
    n = w.shape[1]
    return pl.pallas_call(
        functools.partial(_proj_act_kernel, act=act),
        grid=(bsz, L // tm, n // tn),
        in_specs=[pl.BlockSpec((None, tm, d), lambda b_, i, j: (b_, i, 0)),
                  pl.BlockSpec((d, tn), lambda b_, i, j: (0, j)),
                  pl.BlockSpec((1, tn), lambda b_, i, j: (0, j))],
        out_specs=pl.BlockSpec((None, tm, tn), lambda b_, i, j: (b_, i, j)),
        out_shape=jax.ShapeDtypeStruct((bsz, L, n), out_dtype),
        compiler_params=_cparams("parallel", "parallel", "arbitrary"),
        name="proj_" + act,
    )(h, w, b.reshape(1, n))


def _proj_conv_silu_kernel(h_ref, w_ref, b_ref, wc_ref, bc_ref, cs_ref, o_ref, *, seg):
    z = jnp.dot(h_ref[...], w_ref[...], preferred_element_type=F32) + b_ref[...]
    y = _conv3(z, wc_ref[...], bc_ref[...], seg)
    y = y * jax.nn.sigmoid(y)
    o_ref[...] = (y * cs_ref[...]).astype(o_ref.dtype)


def _proj_conv_silu(h, w, b, wc, bc, colscale, seg, tm, tn=512):
    bsz, L, d = h.shape
    n = w.shape[1]
    col = lambda b_, i, j: (0, j)
    return pl.pallas_call(
        functools.partial(_proj_conv_silu_kernel, seg=seg),
        grid=(bsz, L // tm, n // tn),
        in_specs=[pl.BlockSpec((None, tm, d), lambda b_, i, j: (b_, i, 0)),
                  pl.BlockSpec((d, tn), col),
                  pl.BlockSpec((1, tn), col),
                  pl.BlockSpec((3, tn), col),
                  pl.BlockSpec((1, tn), col),
                  pl.BlockSpec((1, tn), col)],
        out_specs=pl.BlockSpec((None, tm, tn), lambda b_, i, j: (b_, i, j)),
        out_shape=jax.ShapeDtypeStruct((bsz, L, n), BF16),
        compiler_params=_cparams("parallel", "parallel", "arbitrary"),
        name="proj_conv_silu",
    )(h, w, b.reshape(1, n), wc, bc.reshape(1, n), colscale.reshape(1, n))


PROJ_TN = 1024
PROJ_SUB = 512
PM_Q, PM_K, PM_V, PM_O, PM_GA, PM_GB = range(6)


def _proj_main_kernel(x_ref, g_ref, sh_ref, sc_ref, w_ref, b_ref, wc_ref, bc_ref, cs_ref, o_ref, h_ref, *, seg):
    j = pl.program_id(2)

    @pl.when(j == 0)
    def _():
        _norm_mod_kernel(x_ref, g_ref, sh_ref, sc_ref, h_ref)

    def run(epilogue):
        for c in range(PROJ_TN // PROJ_SUB):
            sl = slice(c * PROJ_SUB, (c + 1) * PROJ_SUB)
            z = jnp.dot(h_ref[...], w_ref[:, sl], preferred_element_type=F32) + b_ref[:, sl]
            o_ref[:, sl] = epilogue(z, sl).astype(o_ref.dtype)

    def conv_silu(z, sl):
        y = _conv3(z, wc_ref[:, sl], bc_ref[:, sl], seg)
        return (y * jax.nn.sigmoid(y)) * cs_ref[:, sl]

    @pl.when(j <= PM_K)
    def _():
        run(conv_silu)

    @pl.when(j == PM_V)
    def _():
        run(lambda z, sl: z)

    @pl.when(j >= PM_O)
    def _():
        run(lambda z, sl: jax.nn.sigmoid(z))


def _proj_main(x, g, shift, scale, w, b, wc, bc, colscale, seg, tm):
    bsz, L, d = x.shape
    n = w.shape[1]
    qk = lambda b_, i, j: (0, jnp.minimum(j, PM_K))
    row = pl.BlockSpec((None, tm, d), lambda b_, i, j: (b_, i, 0))
    bvec = pl.BlockSpec((None, 1, d), lambda b_, i, j: (b_, 0, 0))
    return pl.pallas_call(
        functools.partial(_proj_main_kernel, seg=seg),
        grid=(bsz, L // tm, n // PROJ_TN),
        in_specs=[row, pl.BlockSpec((1, d), lambda b_, i, j: (0, 0)), bvec, bvec,
                  pl.BlockSpec((d, PROJ_TN), lambda b_, i, j: (0, j)),
                  pl.BlockSpec((1, PROJ_TN), lambda b_, i, j: (0, j)),
                  pl.BlockSpec((3, PROJ_TN), qk),
                  pl.BlockSpec((1, PROJ_TN), qk),
                  pl.BlockSpec((1, PROJ_TN), qk)],
        out_specs=[pl.BlockSpec((None, tm, PROJ_TN), lambda b_, i, j: (b_, i, j)), row],
        out_shape=[jax.ShapeDtypeStruct((bsz, L, n), BF16), jax.ShapeDtypeStruct((bsz, L, d), BF16)],
        compiler_params=_cparams("parallel", "parallel", "arbitrary"),
        name="proj_main",
    )(x, g.reshape(1, d), shift, scale, w, b.reshape(1, n), wc, bc.reshape(1, -1), colscale.reshape(1, -1))


def _proj_hyena_kernel(h_ref, w0_ref, w1_ref, w2_ref, b_ref, wc_ref, bc_ref, x0_ref, s_ref, *, seg):
    h = h_ref[...]
    us = []
    for g, w_ref in enumerate((w0_ref, w1_ref, w2_ref)):
        z = jnp.dot(h, w_ref[...], preferred_element_type=F32) + b_ref[g]
        us.append(_conv3(z, wc_ref[g], bc_ref[g], seg))
    x0 = _pack_bf16_pairs(us[0])
    s = us[1] * us[2]
    n2 = s_ref.shape[0]
    for jj in range(s_ref.shape[1]):
        x0_ref[:, jj, :] = x0[jj * n2:(jj + 1) * n2]
        s_ref[:, jj, :] = s[jj * n2:(jj + 1) * n2]


def _proj_hyena(h, w, b, wc, bc, seg, tm, n2):
    bsz, L, d = h.shape
    tn = DFT_C_TILE
    nblk = H_WIDTH // tn
    jt = tm // n2
    assert tm % n2 == 0 and (jt % 8 == 0 or jt == L // n2)
    b3 = b.reshape(3, 1, H_WIDTH)
    wc3 = wc.reshape(3, 3, H_WIDTH).transpose(1, 0, 2)
    bc3 = bc.reshape(3, 1, H_WIDTH)
    return pl.pallas_call(
        functools.partial(_proj_hyena_kernel, seg=seg),
        grid=(bsz, L // tm, nblk),
        in_specs=[pl.BlockSpec((None, tm, d), lambda b_, i, j: (b_, i, 0)),
                  pl.BlockSpec((d, tn), lambda b_, i, j: (0, j)),
                  pl.BlockSpec((d, tn), lambda b_, i, j: (0, nblk + j)),
                  pl.BlockSpec((d, tn), lambda b_, i, j: (0, 2 * nblk + j)),
                  pl.BlockSpec((3, 1, tn), lambda b_, i, j: (0, 0, j)),
                  pl.BlockSpec((3, 3, tn), lambda b_, i, j: (0, 0, j)),
                  pl.BlockSpec((3, 1, tn), lambda b_, i, j: (0, 0, j))],
        out_specs=[pl.BlockSpec((None, n2, jt, tn // 2), lambda b_, i, j: (b_, 0, i, j)),
                   pl.BlockSpec((None, n2, jt, tn), lambda b_, i, j: (b_, 0, i, j))],
        out_shape=[jax.ShapeDtypeStruct((bsz, n2, L // n2, H_WIDTH // 2), jnp.uint32),
                   jax.ShapeDtypeStruct((bsz, n2, L // n2, H_WIDTH), F32)],
        compiler_params=_cparams("parallel", "parallel", "arbitrary"),
        name="proj_hyena",
    )(h, w, w, w, b3, wc3, bc3)


N_GATES = 4 * M_HEADS


def _split3(x):
    hi = x.astype(BF16)
    r1 = x - hi.astype(F32)
    mid = r1.astype(BF16)
    lo = (r1 - mid.astype(F32)).astype(BF16)
    return hi, mid, lo


def _log_sigmoid(x):
    return jnp.minimum(x, 0.0) - jnp.log1p(jnp.exp(-jnp.abs(x)))


def _gates_kernel(h_ref, w_ref, wt_ref, b_ref, bt_ref, bc_ref, ac_ref, ar_ref):
    h = h_ref[...]
    t = h.shape[0]
    z = jnp.dot(h, w_ref[...], preferred_element_type=F32) + b_ref[...]
    zt = lax.dot_general(wt_ref[...], h, (((1,), (1,)), ((), ())),
                         preferred_element_type=F32) + bt_ref[...]
    r = lax.broadcasted_iota(jnp.int32, (t, t), 0)
    c = lax.broadcasted_iota(jnp.int32, (t, t), 1)
    lower = (r >= c).astype(BF16)
    upper = (r <= c).astype(BF16)
    g8 = FG_LANE0

    lf = _log_sigmoid(z)
    lane = lax.broadcasted_iota(jnp.int32, z.shape, 1)
    is_fg = (lane >= g8) & (lane < 2 * g8)
    terms = [jnp.where(is_fg, p.astype(F32), 0.0) for p in _split3(lf)]
    packed = terms[0] + pltpu.roll(terms[1], 2 * g8, 1) + pltpu.roll(terms[2], 4 * g8, 1)
    cfp = jnp.dot(lower, packed.astype(BF16), preferred_element_type=F32)
    cf = cfp + pltpu.roll(cfp, LANES - 2 * g8, 1) + pltpu.roll(cfp, LANES - 4 * g8, 1)
    cb = cf[t - 1:t, :] - cf + lf
    bc = jnp.where(lane < g8 + M_HEADS, cf, cb)
    bc = pltpu.roll(bc, LANES - g8, 1)
    bc_ref[...] = bc
    ac_ref[...] = z - bc

    lft = _log_sigmoid(zt[g8:, :])
    stacked = jnp.concatenate([p.astype(F32) for p in _split3(lft)] + [jnp.zeros_like(lft)], axis=0)
    cft3 = jnp.dot(stacked.astype(BF16), upper, preferred_element_type=F32)
    cft = cft3[0:g8] + cft3[g8:2 * g8] + cft3[2 * g8:3 * g8]
    cbt = cft[:, t - 1:t] - cft + lft
    row = lax.broadcasted_iota(jnp.int32, cft.shape, 0)
    ar_ref[...] = zt[:g8, :] - jnp.where(row < M_HEADS, cft, cbt)


FG_LANE0 = 2 * M_HEADS


def _gates(h, w_g, b_g, chunk):
    bsz, L, d = h.shape
    w_pad = jnp.zeros((d, LANES), F32).at[:, :N_GATES].set(w_g).astype(BF16)
    b_pad = jnp.zeros((1, LANES), F32).at[0, :N_GATES].set(b_g)
    wt = w_g.T.astype(BF16)
    bt = b_g.reshape(N_GATES, 1)
    tok = pl.BlockSpec((None, chunk, LANES), lambda b_, i: (b_, i, 0))
    return pl.pallas_call(
        _gates_kernel,
        grid=(bsz, L // chunk),
        in_specs=[pl.BlockSpec((None, chunk, d), lambda b_, i: (b_, i, 0)),
                  pl.BlockSpec((d, LANES), lambda b_, i: (0, 0)),
                  pl.BlockSpec((N_GATES, d), lambda b_, i: (0, 0)),
                  pl.BlockSpec((1, LANES), lambda b_, i: (0, 0)),
                  pl.BlockSpec((N_GATES, 1), lambda b_, i: (0, 0))],
        out_specs=[tok, tok, pl.BlockSpec((None, FG_LANE0, chunk), lambda b_, i: (b_, 0, i))],
        out_shape=[jax.ShapeDtypeStruct((bsz, L, LANES), F32),
                   jax.ShapeDtypeStruct((bsz, L, LANES), F32),
                   jax.ShapeDtypeStruct((bsz, FG_LANE0, L), F32)],
        compiler_params=_cparams("parallel", "parallel"),
        name="mlstm_gates",
    )(h, w_pad, wt, b_pad, bt)


def _mlstm_kernel(*refs, emit_h, n_chunks):
    if emit_h:
        (q_ref, k_ref, v_ref, bc_ref, ac_ref, ar_ref, c0_ref, n0_ref, m0_ref,
         h_ref, cf_ref, nf_ref, mf_ref, c_sc, n_sc, m_sc) = refs
    else:
        (k_ref, v_ref, bc_ref, ac_ref, ar_ref, c0_ref, n0_ref, m0_ref,
         cf_ref, nf_ref, mf_ref, c_sc, n_sc, m_sc) = refs
    d = pl.program_id(1)
    j = pl.program_id(2)
    fwd = d == 0
    t = k_ref.shape[0]
    dh = M_HEAD_DIM

    @pl.when(j == 0)
    def _():
        c_sc[...] = c0_ref[...]
        n_sc[...] = n0_ref[...]
        m_sc[...] = m0_ref[...]

    r = lax.broadcasted_iota(jnp.int32, (t, t), 0)
    c = lax.broadcasted_iota(jnp.int32, (t, t), 1)
    causal = jnp.where(fwd, r - c, c - r) >= 0
    bc_all = bc_ref[...]
    ac_all = ac_ref[...]
    ar_all = ar_ref[...]
    for hd in range(M_HEADS):
        sl = slice(hd * dh, (hd + 1) * dh)
        bc = jnp.where(fwd, bc_all[:, hd:hd + 1], bc_all[:, M_HEADS + hd:M_HEADS + hd + 1])
        ac = jnp.where(fwd, ac_all[:, hd:hd + 1], ac_all[:, M_HEADS + hd:M_HEADS + hd + 1])
        ar = jnp.where(fwd, ar_all[hd:hd + 1, :], ar_all[M_HEADS + hd:M_HEADS + hd + 1, :])
        b_tot = jnp.where(fwd, bc[t - 1:t, :], bc[0:1, :])
        m_prev = m_sc[hd][:, 0:1]
        k_h = k_ref[:, sl]
        v_h = v_ref[:, sl]
        if emit_h:
            q_h = q_ref[:, sl]
            dm = jnp.where(causal, bc + ar, NEG_BIG)
            inter = bc + m_prev
            m_t = jnp.maximum(inter, jnp.max(dm, axis=1, keepdims=True))
            qk = lax.dot_general(q_h, k_h, (((1,), (1,)), ((), ())), preferred_element_type=F32)
            s = qk * jnp.exp(dm - m_t)
            carry = jnp.exp(inter - m_t)
            num = (jnp.dot(s.astype(BF16), v_h, preferred_element_type=F32)
                   + carry * jnp.dot(q_h, c_sc[hd].astype(BF16), preferred_element_type=F32))
            den = (jnp.sum(s, axis=1, keepdims=True)
                   + carry * jnp.sum(q_h.astype(F32) * n_sc[hd], axis=1, keepdims=True))
            h_ref[:, sl] = (num / jnp.maximum(jnp.abs(den), jnp.exp(-m_t))).astype(h_ref.dtype)
        g = b_tot + ac
        m_new = jnp.maximum(b_tot + m_prev, jnp.max(g, axis=0, keepdims=True))
        wgt = jnp.exp(g - m_new)
        decay = jnp.exp(b_tot + m_prev - m_new)
        kw = k_h.astype(F32) * wgt
        c_sc[hd] = decay * c_sc[hd] + lax.dot_general(kw.astype(BF16), v_h, (((0,), (0,)), ((), ())),
                                                      preferred_element_type=F32)
        n_sc[hd] = decay * n_sc[hd] + jnp.sum(kw, axis=0, keepdims=True)
        m_sc[hd] = jnp.broadcast_to(m_new, (1, LANES))

    @pl.when(j == n_chunks - 1)
    def _():
        cf_ref[...] = c_sc[...]
        nf_ref[...] = n_sc[...]
        mf_ref[...] = m_sc[...]


def _mlstm(q, k, v, bc, ac, ar, state, emit_h, t):
    bsz, L, _ = k[0].shape
    nc = L // t
    seq = lambda b_, d, j: (b_, j + d * (nc - 1 - 2 * j), 0)
    st = lambda b_, d, j: (b_, d, 0, 0, 0)

    def tok(col):
        return pl.BlockSpec((None, t, M_WIDTH), lambda b_, d, j: (b_, j + d * (nc - 1 - 2 * j), col))

    gate_spec = pl.BlockSpec((None, t, LANES), seq)
    ar_spec = pl.BlockSpec((None, FG_LANE0, t), lambda b_, d, j: (b_, 0, j + d * (nc - 1 - 2 * j)))
    c_spec = pl.BlockSpec((None, None, M_HEADS, M_HEAD_DIM, M_HEAD_DIM), st)
    n_spec = pl.BlockSpec((None, None, M_HEADS, 1, M_HEAD_DIM), st)
    m_spec = pl.BlockSpec((None, None, M_HEADS, 1, LANES), st)
    state_shapes = [jax.ShapeDtypeStruct((bsz, 2, M_HEADS, M_HEAD_DIM, M_HEAD_DIM), F32),
                    jax.ShapeDtypeStruct((bsz, 2, M_HEADS, 1, M_HEAD_DIM), F32),
                    jax.ShapeDtypeStruct((bsz, 2, M_HEADS, 1, LANES), F32)]
    in_specs = [tok(k[1]), tok(v[1]), gate_spec, gate_spec, ar_spec, c_spec, n_spec, m_spec]
    args = [k[0], v[0], bc, ac, ar, *state]
    out_specs = [c_spec, n_spec, m_spec]
    out_shape = list(state_shapes)
    if emit_h:
        in_specs = [tok(q[1])] + in_specs
        args = [q[0]] + args
        out_specs = [pl.BlockSpec((None, None, t, M_WIDTH),
                                  lambda b_, d, j: (d, b_, j + d * (nc - 1 - 2 * j), 0))] + out_specs
        out_shape = [jax.ShapeDtypeStruct((2, bsz, L, M_WIDTH), BF16)] + out_shape
    outs = pl.pallas_call(
        functools.partial(_mlstm_kernel, emit_h=emit_h, n_chunks=nc),
        grid=(bsz, 2, nc),
        in_specs=in_specs,
        out_specs=out_specs,
        out_shape=out_shape,
        scratch_shapes=[pltpu.VMEM((M_HEADS, M_HEAD_DIM, M_HEAD_DIM), F32),
                        pltpu.VMEM((M_HEADS, 1, M_HEAD_DIM), F32),
                        pltpu.VMEM((M_HEADS, 1, LANES), F32)],
        compiler_params=_cparams("parallel", "parallel", "arbitrary"),
        name="mlstm" if emit_h else "mlstm_state",
    )(*args)
    if emit_h:
        return outs[0], tuple(outs[1:])
    return None, tuple(outs)


DFT_M_TILE = 8
DFT_C_TILE = 512
FEAT_ROWS = 16


def _filter_outer_kernel(bands_ref, w1t_ref, b1_ref, w2t_ref, b2_ref, w3p_ref, w3f_ref, fr_ref, dl_ref, l_ref,
                         a_ref, ss_ref, *, L, n1, n2):
    i = pl.program_id(0)
    h = n1 // 2
    cols = DFT_M_TILE * h

    def positions(shape, axis, side):
        q = lax.broadcasted_iota(jnp.int32, shape, axis)
        mm, jj = q // h, q % h
        n = n2 * (jj + side * h) + i * DFT_M_TILE + mm
        return n, jnp.where(n < L, n, 2 * L - n).astype(F32)

    taps = []
    sumsq = jnp.zeros((1, a_ref.shape[-1]), F32)
    for side, w3_ref in ((0, w3p_ref), (1, w3f_ref)):
        _, p_row = positions((1, cols), 1, side)
        t_row = p_row / float(max(L - 1, 1))
        ang = ((2 * math.pi / L) * p_row) * bands_ref[...]
        row = lax.broadcasted_iota(jnp.int32, (FEAT_ROWS, cols), 0)
        feats = jnp.concatenate([jnp.where(row == 0, t_row, 0.0), jnp.cos(ang), -jnp.sin(ang)], axis=0)
        fr = fr_ref[...]
        hid = jnp.sin(fr * (jnp.dot(w1t_ref[...], feats.astype(BF16), preferred_element_type=F32) + b1_ref[...]))
        hid = jnp.sin(fr * (jnp.dot(w2t_ref[...], hid.astype(BF16), preferred_element_type=F32) + b2_ref[...]))
        filt = lax.dot_general(hid.astype(BF16), w3_ref[...], (((0,), (0,)), ((), ())),
                               preferred_element_type=F32)
        n_col, p_col = positions((cols, 1), 0, side)
        t_col = p_col / float(max(L - 1, 1))
        kern = filt * jnp.exp(-t_col * jnp.abs(dl_ref[...]))
        kern = jnp.where(n_col == L, 0.0, kern)
        sumsq = sumsq + jnp.sum(kern * kern, axis=0, keepdims=True)
        taps.append(kern)

    for mm in range(DFT_M_TILE):
        x = jnp.concatenate([taps[0][mm * h:(mm + 1) * h], taps[1][mm * h:(mm + 1) * h]], axis=0)
        out = jnp.dot(l_ref[...], x.astype(BF16), preferred_element_type=F32)
        a_ref[0, :, mm, :] = out[:n1]
        a_ref[1, :, mm, :] = out[n1:]

    @pl.when(i == 0)
    def _():
        ss_ref[...] = jnp.zeros_like(ss_ref)

    ss_ref[...] += sumsq


def _filter_outer(L, n1, n2, fwd_r, w1, b1, w2, b2, w3, freq):
    hid = H_FILTER_HIDDEN
    bands = jnp.linspace(1e-4, H_POS_BANDS - 1, H_POS_BANDS, dtype=F32).reshape(H_POS_BANDS, 1)
    w1t = jnp.zeros((hid, 3 * FEAT_ROWS), F32)
    w1t = w1t.at[:, 0].set(w1[0]).at[:, FEAT_ROWS:2 * FEAT_ROWS].set(w1[1:1 + H_POS_BANDS].T)
    w1t = w1t.at[:, 2 * FEAT_ROWS:].set(w1[1 + H_POS_BANDS:].T).astype(BF16)
    w3h = w3.astype(BF16)
    max_decay = math.log(H_DECAY_TARGET) / H_FAST_DECAY_PCT
    min_decay = math.log(H_DECAY_TARGET) / H_SLOW_DECAY_PCT
    deltas = jnp.linspace(min_decay, max_decay, H_WIDTH, dtype=F32).reshape(1, H_WIDTH)
    col = lambda v: v.reshape(hid, 1)
    full = lambda a: pl.BlockSpec(a.shape, lambda i: (0,) * a.ndim)
    args = [bands, w1t, col(b1), w2.T.astype(BF16), col(b2)]
    return pl.pallas_call(
        functools.partial(_filter_outer_kernel, L=L, n1=n1, n2=n2),
        grid=(n2 // DFT_M_TILE,),
        in_specs=[full(a) for a in args]
        + [pl.BlockSpec((hid, H_WIDTH), lambda i: (0, 0)), pl.BlockSpec((hid, H_WIDTH), lambda i: (0, 1)),
           full(col(freq)), full(deltas), full(fwd_r)],
        out_specs=[pl.BlockSpec((2, n1, DFT_M_TILE, H_WIDTH), lambda i: (0, 0, i, 0)),
                   pl.BlockSpec((1, H_WIDTH), lambda i: (0, 0))],
        out_shape=[jax.ShapeDtypeStruct((2, n1, n2, H_WIDTH), F32),
                   jax.ShapeDtypeStruct((1, H_WIDTH), F32)],
        compiler_params=_cparams("arbitrary"),
        name="hyena_filter_outer",
    )(*args, w3h, w3h, col(freq), deltas, fwd_r)


def _dft_factors(n):
    lg = int(round(math.log2(n)))
    n1 = 1 << ((lg + 1) // 2)
    return n1, n // n1


def _dft_outer_matrices(n1):
    k = np.arange(n1)[:, None]
    n = np.arange(n1)[None, :]
    ang = 2.0 * np.pi * ((k * n) % n1) / n1
    cr, ci = np.cos(ang), -np.sin(ang)
    h = n1 // 2
    fwd_c = np.block([[cr[:, :h], -ci[:, :h]], [ci[:, :h], cr[:, :h]]])
    fwd_r = np.concatenate([cr, ci], axis=0)
    ir, ii = cr[:h, :], -ci[:h, :]
    inv = np.block([[ir, -ii], [ii, ir]])
    return (jnp.asarray(fwd_c, F32).astype(BF16), jnp.asarray(fwd_r, F32).astype(BF16),
            jnp.asarray(inv, F32).astype(BF16))


def _dft_inner_matrices(n1, n2):
    n = n1 * n2
    k2 = np.arange(n2)[:, None]
    m = np.arange(n2)[None, :]
    ang = 2.0 * np.pi * ((k2 * m) % n2) / n2
    fr, fi = np.cos(ang), -np.sin(ang)
    f = np.block([[fr, -fi], [fi, fr]])
    k1 = jnp.arange(n1, dtype=jnp.int32)[:, None]
    tw_ang = ((jnp.arange(n2, dtype=jnp.int32)[None, :] * k1) % n).astype(F32) * (2.0 * math.pi / n)
    rep = lambda t: jnp.broadcast_to(t[:, :, None], (n1, n2, LANES))
    return (jnp.asarray(f, F32).astype(BF16), jnp.asarray(f.T, F32).astype(BF16),
            rep(jnp.cos(tw_ang)), rep(-jnp.sin(tw_ang)))


def _outer_fwd_kernel(l_ref, s_ref, a_ref):
    n1 = a_ref.shape[1]
    for mm in range(s_ref.shape[1]):
        x = jnp.concatenate([s_ref[0, mm], s_ref[1, mm]], axis=0).astype(BF16)
        out = jnp.dot(l_ref[...], x, preferred_element_type=F32)
        a_ref[0, :, mm, :] = out[:n1]
        a_ref[1, :, mm, :] = out[n1:]


def _outer_fwd(lmat, s_t):
    _, n2, n1h, c = s_t.shape
    n1 = 2 * n1h
    tc = min(DFT_C_TILE, c)
    return pl.pallas_call(
        _outer_fwd_kernel,
        grid=(n2 // DFT_M_TILE, c // tc),
        in_specs=[pl.BlockSpec(lmat.shape, lambda m, j: (0, 0)),
                  pl.BlockSpec((2, DFT_M_TILE, n1h, tc), lambda m, j: (0, m, 0, j))],
        out_specs=pl.BlockSpec((2, n1, DFT_M_TILE, tc), lambda m, j: (0, 0, m, j)),
        out_shape=jax.ShapeDtypeStruct((2, n1, n2, c), F32),
        compiler_params=_cparams("parallel", "parallel"),
        name="dft_outer_fwd",
    )(lmat, s_t)


def _outer_inv_kernel(l_ref, b_ref, s_ref, x0_ref, ysc_ref, hb_ref, o_ref):
    n1h = s_ref.shape[2]
    for mm in range(b_ref.shape[1]):
        y = jnp.concatenate([b_ref[0, mm], b_ref[1, mm]], axis=0).astype(BF16)
        out = jnp.dot(l_ref[...], y, preferred_element_type=F32)
        for b in range(2):
            conv = out[b * n1h:(b + 1) * n1h]
            x0 = _unpack_bf16_pairs(x0_ref[b, mm]).astype(F32)
            hy = x0 * (conv * ysc_ref[...] + hb_ref[...] * s_ref[b, mm])
            o_ref[b, :, mm, :] = _pack_bf16_pairs(hy)


def _outer_inv(lmat, b_t, s_t, x0_t, yscale, h_bias):
    _, n2, n1, c = b_t.shape
    n1h = n1 // 2
    tc = min(DFT_C_TILE, c)
    vec = pl.BlockSpec((1, tc), lambda m, j: (0, j))
    hy = pl.pallas_call(
        _outer_inv_kernel,
        grid=(n2 // DFT_M_TILE, c // tc),
        in_specs=[pl.BlockSpec(lmat.shape, lambda m, j: (0, 0)),
                  pl.BlockSpec((2, DFT_M_TILE, n1, tc), lambda m, j: (0, m, 0, j)),
                  pl.BlockSpec((2, DFT_M_TILE, n1h, tc), lambda m, j: (0, m, 0, j)),
                  pl.BlockSpec((2, DFT_M_TILE, n1h, tc // 2), lambda m, j: (0, m, 0, j)),
                  vec, vec],
        out_specs=pl.BlockSpec((2, n1h, DFT_M_TILE, tc // 2), lambda m, j: (0, 0, m, j)),
        out_shape=jax.ShapeDtypeStruct((2, n1h, n2, c // 2), jnp.uint32),
        compiler_params=_cparams("parallel", "parallel"),
        name="dft_outer_inv",
    )(lmat, b_t, s_t, x0_t, yscale, h_bias.reshape(1, c))
    return hy.reshape(2, n1h * n2, c // 2)


DFT_K_TILE = 8


def _twiddled_inner_dft(f_ref, twr_ref, twi_ref, a_ref, kk):
    n2, c = a_ref.shape[2], a_ref.shape[3]
    twr = jnp.tile(twr_ref[kk], (1, c // LANES))
    twi = jnp.tile(twi_ref[kk], (1, c // LANES))
    ar, ai = a_ref[0, kk], a_ref[1, kk]
    a = jnp.concatenate([(ar * twr - ai * twi).astype(BF16), (ar * twi + ai * twr).astype(BF16)], axis=0)
    x = jnp.dot(f_ref[...], a, preferred_element_type=F32)
    return x[:n2], x[n2:], twr, twi


def _inner_fwd_kernel(f_ref, twr_ref, twi_ref, a_ref, o_ref):
    for kk in range(a_ref.shape[1]):
        xr, xi, _, _ = _twiddled_inner_dft(f_ref, twr_ref, twi_ref, a_ref, kk)
        o_ref[0, kk] = xr.astype(o_ref.dtype)
        o_ref[1, kk] = xi.astype(o_ref.dtype)


def _inner_specs(n1, n2, c):
    tc = min(DFT_C_TILE, c)
    kt = min(DFT_K_TILE, n1)
    blk = pl.BlockSpec((2, kt, n2, tc), lambda k, j: (0, k, 0, j))
    mat = pl.BlockSpec((2 * n2, 2 * n2), lambda k, j: (0, 0))
    tw = pl.BlockSpec((kt, n2, LANES), lambda k, j: (k, 0, 0))
    return blk, mat, tw, (n1 // kt, c // tc), kt, tc


def _inner_fwd(f, twr, twi, a):
    _, n1, n2, c = a.shape
    blk, mat, tw, grid, _, _ = _inner_specs(n1, n2, c)
    return pl.pallas_call(
        _inner_fwd_kernel,
        grid=grid,
        in_specs=[mat, tw, tw, blk],
        out_specs=blk,
        out_shape=jax.ShapeDtypeStruct((2, n1, n2, c), BF16),
        compiler_params=_cparams("parallel", "parallel"),
        name="dft_inner_filter",
    )(f, twr, twi, a)


def _inner_conv_kernel(f_ref, ft_ref, twr_ref, twi_ref, a_ref, k_ref, o_ref):
    n2 = a_ref.shape[2]
    for kk in range(a_ref.shape[1]):
        xr, xi, twr, twi = _twiddled_inner_dft(f_ref, twr_ref, twi_ref, a_ref, kk)
        kr, ki = k_ref[0, kk].astype(F32), k_ref[1, kk].astype(F32)
        yr = xr * kr - xi * ki
        yi = xr * ki + xi * kr
        y = jnp.concatenate([yr.astype(BF16), yi.astype(BF16)], axis=0)
        b = jnp.dot(ft_ref[...], y, preferred_element_type=F32)
        br, bi = b[:n2], b[n2:]
        o_ref[0, :, kk, :] = br * twr + bi * twi
        o_ref[1, :, kk, :] = bi * twr - br * twi


def _inner_conv(f, ft, twr, twi, a, kf):
    _, n1, n2, c = a.shape
    blk, mat, tw, grid, kt, tc = _inner_specs(n1, n2, c)
    return pl.pallas_call(
        _inner_conv_kernel,
        grid=grid,
        in_specs=[mat, mat, tw, tw, blk, blk],
        out_specs=pl.BlockSpec((2, n2, kt, tc), lambda k, j: (0, 0, k, j)),
        out_shape=jax.ShapeDtypeStruct((2, n2, n1, c), F32),
        compiler_params=_cparams("parallel", "parallel"),
        name="dft_inner_conv",
    )(f, ft, twr, twi, a, kf)


def _hyena_long_conv(s_t, x0_t, h_bias, w1, b1, w2, b2, w3, freq):
    bsz, n2, n1h, c = s_t.shape
    assert bsz == 2
    n1 = 2 * n1h
    L = n1h * n2
    fwd_c, fwd_r, inv = _dft_outer_matrices(n1)
    f, ft, twr, twi = _dft_inner_matrices(n1, n2)
    af, sumsq = _filter_outer(L, n1, n2, fwd_r, w1, b1, w2, b2, w3, freq)
    kf = _inner_fwd(f, twr, twi, af)
    a = _outer_fwd(fwd_c, s_t)
    b_t = _inner_conv(f, ft, twr, twi, a, kf)
    yscale = lax.rsqrt(sumsq + EPS) * (1.0 / (2 * L))
    return _outer_inv(inv, b_t, s_t, x0_t, yscale, h_bias)


def _pack_bf16_pairs(x):
    half = x.shape[1] // 2
    lo = pltpu.bitcast(x[:, :half].astype(BF16).astype(F32), jnp.uint32) >> 16
    hi = pltpu.bitcast(x[:, half:].astype(BF16).astype(F32), jnp.uint32) & jnp.uint32(0xFFFF0000)
    return lo | hi


def _unpack_bf16_pairs(p):
    lo = pltpu.bitcast(p << 16, F32).astype(BF16)
    hi = pltpu.bitcast(p & jnp.uint32(0xFFFF0000), F32).astype(BF16)
    return jnp.concatenate([lo, hi], axis=1)


def _merge_kernel(hf_ref, hb_ref, o_ref, hy_ref, ga_ref, gb_ref, x_ref,
                  gate_ref, g2_ref, sh_ref, sc_ref, wa_ref, wb_ref, wo_ref, x1_ref, h2_ref):
    a = o_ref[...].astype(F32) * (hf_ref[...].astype(F32) + hb_ref[...].astype(F32))
    half = DFT_C_TILE // 2
    hy = jnp.concatenate([_unpack_bf16_pairs(hy_ref[:, c * half:(c + 1) * half])
                          for c in range(hy_ref.shape[1] // half)], axis=1)
    pa = jnp.dot(a.astype(BF16), wa_ref[...], preferred_element_type=F32)
    pb = jnp.dot(hy, wb_ref[...], preferred_element_type=F32)
    mix = ga_ref[...].astype(F32) * pa + gb_ref[...].astype(F32) * pb
    out = jnp.dot(mix.astype(BF16), wo_ref[...], preferred_element_type=F32)
    x1 = x_ref[...] + gate_ref[...] * out
    x1_ref[...] = x1
    y = x1 * lax.rsqrt(jnp.mean(x1 * x1, axis=-1, keepdims=True) + EPS) * g2_ref[...]
    h2_ref[...] = _pack_bf16_pairs(y * (1.0 + sc_ref[...]) + sh_ref[...])


def _merge(hdirs, pm, hy, x, gate1, g2, shift2, scale2, w_a, w_b, w_out, tm=256):
    bsz, L, d = x.shape
    tok = pl.BlockSpec((None, tm, d), lambda b, i: (b, i, 0))

    def pm_tile(col):
        return pl.BlockSpec((None, tm, d), lambda b, i: (b, i, col))

    packed = pl.BlockSpec((None, tm, d // 2), lambda b, i: (b, i, 0))
    vec = pl.BlockSpec((1, d), lambda b, i: (0, 0))
    bvec = pl.BlockSpec((None, 1, d), lambda b, i: (b, 0, 0))
    wsp = pl.BlockSpec((d, d), lambda b, i: (0, 0))
    return pl.pallas_call(
        _merge_kernel,
        grid=(bsz, L // tm),
        in_specs=[pl.BlockSpec((None, None, tm, d), lambda b, i: (0, b, i, 0)),
                  pl.BlockSpec((None, None, tm, d), lambda b, i: (1, b, i, 0)),
                  pm_tile(PM_O), packed, pm_tile(PM_GA), pm_tile(PM_GB), tok,
                  bvec, vec, bvec, bvec, wsp, wsp, wsp],
        out_specs=[tok, packed],
        out_shape=[jax.ShapeDtypeStruct((bsz, L, d), F32), jax.ShapeDtypeStruct((bsz, L, d // 2), jnp.uint32)],
        compiler_params=_cparams("parallel", "parallel"),
        name="merge",
    )(hdirs, hdirs, pm, hy, pm, pm, x, gate1, g2.reshape(1, d), shift2, scale2, w_a, w_b, w_out)


MOE_BLOCK = 256
ROUTE_E1, ROUTE_E2, ROUTE_W1, ROUTE_W2 = 0, 1, 2, 3
EXP_LANE0 = N_GROUPS


def _first_lane_of_max(val, valid, lane):
    masked = jnp.where(valid, val, NEG_BIG)
    mx = jnp.max(masked, axis=1, keepdims=True)
    idx = jnp.min(jnp.where(valid & (masked == mx), lane, LANES), axis=1, keepdims=True)
    return mx, idx


def _router_kernel(h_ref, w_ref, b_ref, r_ref):
    logits = jnp.dot(_unpack_bf16_pairs(h_ref[...]), w_ref[...], preferred_element_type=F32) + b_ref[...]
    lane = lax.broadcasted_iota(jnp.int32, logits.shape, 1)
    is_g = lane < N_GROUPS
    gmax, gsel = _first_lane_of_max(logits, is_g, lane)
    gsum = jnp.sum(jnp.where(is_g, jnp.exp(logits - gmax), 0.0), axis=1, keepdims=True)
    gw = 1.0 / gsum
    lo = EXP_LANE0 + gsel * EXPERTS_PER_GROUP
    in_grp = (lane >= lo) & (lane < lo + EXPERTS_PER_GROUP)
    emax, l1 = _first_lane_of_max(logits, in_grp, lane)
    esum = jnp.sum(jnp.where(in_grp, jnp.exp(logits - emax), 0.0), axis=1, keepdims=True)
    e2max, l2 = _first_lane_of_max(logits, in_grp & (lane != l1), lane)
    v1 = 1.0 / esum
    v2 = jnp.exp(e2max - emax) / esum
    vs = v1 + v2
    w1 = gw * v1 / vs
    w2 = gw * v2 / vs
    e1 = (l1 - EXP_LANE0).astype(F32)
    e2 = (l2 - EXP_LANE0).astype(F32)
    r_ref[...] = jnp.where(lane == ROUTE_E1, e1,
                           jnp.where(lane == ROUTE_E2, e2,
                                     jnp.where(lane == ROUTE_W1, w1,
                                               jnp.where(lane == ROUTE_W2, w2, 0.0))))


def _router(h2, w_group, b_group, w_router, b_router, tm=1024):
    n, dp = h2.shape
    d = 2 * dp
    w = jnp.zeros((d, LANES), F32).at[:, :N_GROUPS].set(w_group).at[
        :, EXP_LANE0:EXP_LANE0 + N_EXPERTS].set(w_router).astype(BF16)
    b = jnp.zeros((1, LANES), F32).at[0, :N_GROUPS].set(b_group).at[
        0, EXP_LANE0:EXP_LANE0 + N_EXPERTS].set(b_router)
    return pl.pallas_call(
        _router_kernel,
        grid=(n // tm,),
        in_specs=[pl.BlockSpec((tm, dp), lambda i: (i, 0)),
                  pl.BlockSpec((d, LANES), lambda i: (0, 0)),
                  pl.BlockSpec((1, LANES), lambda i: (0, 0))],
        out_specs=pl.BlockSpec((tm, LANES), lambda i: (i, 0)),
        out_shape=jax.ShapeDtypeStruct((n, LANES), F32),
        compiler_params=_cparams("parallel"),
        name="moe_router",
    )(h2, w, b)


def _slots_kernel(r_ref, dest_ref, cnt_ref, run_sc, start_sc):
    ph = pl.program_id(0)
    i = pl.program_id(1)
    rec = r_ref[...]
    tm = rec.shape[0]
    lane = lax.broadcasted_iota(jnp.int32, rec.shape, 1)
    e1 = rec[:, ROUTE_E1:ROUTE_E1 + 1].astype(jnp.int32)
    e2 = rec[:, ROUTE_E2:ROUTE_E2 + 1].astype(jnp.int32)
    oh1 = lane == e1
    oh2 = lane == e2
    oh = (oh1 | oh2).astype(F32)

    @pl.when((ph == 0) & (i == 0))
    def _():
        run_sc[...] = jnp.zeros_like(run_sc)

    @pl.when(ph == 0)
    def _():
        run_sc[...] += jnp.sum(oh, axis=0, keepdims=True)

    @pl.when((ph == 1) & (i == 0))
    def _():
        counts = run_sc[...]
        cnt_ref[...] = counts
        nblk = jnp.floor((counts + (MOE_BLOCK - 1)) * (1.0 / MOE_BLOCK))
        rr = lax.broadcasted_iota(jnp.int32, (LANES, LANES), 0)
        cc = lax.broadcasted_iota(jnp.int32, (LANES, LANES), 1)
        before = (rr < cc).astype(BF16)
        first = jnp.dot(nblk.astype(BF16), before, preferred_element_type=F32)
        start_sc[...] = first * float(MOE_BLOCK)
        run_sc[...] = jnp.zeros_like(run_sc)

    @pl.when(ph == 1)
    def _():
        r = lax.broadcasted_iota(jnp.int32, (tm, tm), 0)
        c = lax.broadcasted_iota(jnp.int32, (tm, tm), 1)
        earlier = (r > c).astype(BF16)
        rank = jnp.dot(earlier, oh.astype(BF16), preferred_element_type=F32) + run_sc[...] + start_sc[...]
        d1 = jnp.sum(jnp.where(oh1, rank, 0.0), axis=1, keepdims=True)
        d2 = jnp.sum(jnp.where(oh2, rank, 0.0), axis=1, keepdims=True)
        dest_ref[...] = jnp.where(lane == 0, d1, jnp.where(lane == 1, d2, 0.0)).astype(jnp.int32)
        run_sc[...] += jnp.sum(oh, axis=0, keepdims=True)


def _slots(route, tm=512):
    n = route.shape[0]
    return pl.pallas_call(
        _slots_kernel,
        grid=(2, n // tm),
        in_specs=[pl.BlockSpec((tm, LANES), lambda p, i: (i, 0))],
        out_specs=[pl.BlockSpec((tm, LANES), lambda p, i: (i * p, 0)),
                   pl.BlockSpec((1, LANES), lambda p, i: (0, 0))],
        out_shape=[jax.ShapeDtypeStruct((n, LANES), jnp.int32), jax.ShapeDtypeStruct((1, LANES), F32)],
        scratch_shapes=[pltpu.VMEM((1, LANES), F32), pltpu.VMEM((1, LANES), F32)],
        compiler_params=_cparams("arbitrary", "arbitrary"),
        name="moe_slots",
    )(route)


def _experts_kernel(be_ref, first_ref, nxt_ref, par_ref, nu_ref, x_ref, w1_hbm, w3_hbm, w2_hbm, o_ref,
                    w1f, w3f, w2f, w1b, w3b, w2b, sems):
    i = pl.program_id(0)

    def weight_copies(e, slot):
        return (pltpu.make_async_copy(w1_hbm.at[e], w1f.at[slot], sems.at[0, slot]),
                pltpu.make_async_copy(w3_hbm.at[e], w3f.at[slot], sems.at[1, slot]),
                pltpu.make_async_copy(w2_hbm.at[e], w2f.at[slot], sems.at[2, slot]))

    @pl.when(i == 0)
    def _():
        for cp in weight_copies(be_ref[0], 0):
            cp.start()

    @pl.when(first_ref[i] == 1)
    def _():
        slot = par_ref[i]

        @pl.when(nxt_ref[i] >= 0)
        def _():
            for cp in weight_copies(nxt_ref[i], 1 - slot):
                cp.start()

        for cp in weight_copies(be_ref[i], slot):
            cp.wait()
        w1b[...] = w1f[slot].astype(BF16)
        w3b[...] = w3f[slot].astype(BF16)
        w2b[...] = w2f[slot].astype(BF16)

    @pl.when(i < nu_ref[0])
    def _():
        x = _unpack_bf16_pairs(x_ref[...])
        a = jnp.dot(x, w1b[...], preferred_element_type=F32)
        b = jnp.dot(x, w3b[...], preferred_element_type=F32)
        hmid = (a * jax.nn.sigmoid(a)) * b
        o_ref[...] = _pack_bf16_pairs(jnp.dot(hmid.astype(BF16), w2b[...], preferred_element_type=F32))

    @pl.when(i >= nu_ref[0])
    def _():
        o_ref[...] = jnp.zeros_like(o_ref)


def _experts(xs, nb, block_e, n_used, w1_e, w3_e, w2_e):
    dp = xs.shape[1]
    d, de = w1_e.shape[1], w1_e.shape[2]
    idx = jnp.arange(nb, dtype=jnp.int32)
    used = idx < n_used[0]
    first = used & ((idx == 0) | (block_e != jnp.roll(block_e, 1)))
    ordinal = jnp.cumsum(first.astype(jnp.int32)) - 1
    par = (ordinal % 2).astype(jnp.int32)
    first_pos = jnp.where(first, idx, nb)
    next_first = lax.cummin(jnp.concatenate([first_pos[1:], jnp.full((1,), nb, jnp.int32)]), reverse=True)
    nxt = jnp.where(next_first < nb, block_e[jnp.minimum(next_first, nb - 1)], -1).astype(jnp.int32)
    any_spec = pl.BlockSpec(memory_space=pl.ANY)
    grid_spec = pltpu.PrefetchScalarGridSpec(
        num_scalar_prefetch=5,
        grid=(nb,),
        in_specs=[pl.BlockSpec((MOE_BLOCK, dp), lambda i, *_: (i, 0)), any_spec, any_spec, any_spec],
        out_specs=pl.BlockSpec((MOE_BLOCK, dp), lambda i, *_: (i, 0)),
        scratch_shapes=[pltpu.VMEM((2, d, de), F32), pltpu.VMEM((2, d, de), F32), pltpu.VMEM((2, de, d), F32),
                        pltpu.VMEM((d, de), BF16), pltpu.VMEM((d, de), BF16), pltpu.VMEM((de, d), BF16),
                        pltpu.SemaphoreType.DMA((3, 2))],
    )
    return pl.pallas_call(
        _experts_kernel,
        grid_spec=grid_spec,
        out_shape=jax.ShapeDtypeStruct((nb * MOE_BLOCK, dp), xs.dtype),
        compiler_params=_cparams("arbitrary"),
        name="moe_experts",
    )(block_e, first.astype(jnp.int32), nxt, par, n_used, xs, w1_e, w3_e, w2_e)


SC_WINDOW = 128
SC_CORES, SC_SUBCORES = 2, 16
SC_WORKERS = SC_CORES * SC_SUBCORES


def _sc_worker_id():
    return lax.axis_index("c") * SC_SUBCORES + lax.axis_index("s")


def _sc_mesh():
    return plsc.VectorSubcoreMesh(core_axis_name="c", subcore_axis_name="s")


def _sc_dispatch(rows, dest0, dest1, pad_slots, n_rows):
    n, dv = rows.shape
    nwin, pwin = n // SC_WINDOW, pad_slots.shape[0] // SC_WINDOW
    assert n % (SC_WINDOW * SC_WORKERS) == 0 and pad_slots.shape[0] % (SC_WINDOW * SC_WORKERS) == 0
    zeros = jnp.zeros((SC_WINDOW, dv), rows.dtype)

    @pl.kernel(out_type=jax.ShapeDtypeStruct((n_rows, dv), rows.dtype), mesh=_sc_mesh(),
               scratch_types=[pltpu.VMEM((1, SC_WINDOW), jnp.int32), pltpu.VMEM((SC_WINDOW, dv), rows.dtype)],
               name="moe_dispatch_sc")
    def scatter(x_hbm, d0_hbm, d1_hbm, p_hbm, z_hbm, o_hbm, idx, buf):
        wid = _sc_worker_id()
        pltpu.sync_copy(z_hbm, buf)

        @pl.loop(0, pwin // SC_WORKERS)
        def _(t):
            w = t * SC_WORKERS + wid
            pltpu.sync_copy(p_hbm.at[pl.ds(w, 1)], idx)
            pltpu.sync_copy(buf, o_hbm.at[idx.at[0]])

        @pl.loop(0, nwin // SC_WORKERS)
        def _(t):
            w = t * SC_WORKERS + wid
            pltpu.sync_copy(x_hbm.at[pl.ds(w * SC_WINDOW, SC_WINDOW)], buf)
            for d_hbm in (d0_hbm, d1_hbm):
                pltpu.sync_copy(d_hbm.at[pl.ds(w, 1)], idx)
                pltpu.sync_copy(buf, o_hbm.at[idx.at[0]])

    return scatter(rows, dest0.reshape(nwin, SC_WINDOW), dest1.reshape(nwin, SC_WINDOW),
                   pad_slots.reshape(pwin, SC_WINDOW), zeros)


def _sc_gather(table, index):
    m = index.shape[0]
    dv = table.shape[1]
    nwin = m // SC_WINDOW
    assert m % (SC_WINDOW * SC_WORKERS) == 0

    @pl.kernel(out_type=jax.ShapeDtypeStruct((m, dv), table.dtype), mesh=_sc_mesh(),
               scratch_types=[pltpu.VMEM((1, SC_WINDOW), jnp.int32), pltpu.VMEM((SC_WINDOW, dv), table.dtype)],
               name="moe_gather_sc")
    def gather(x_hbm, i_hbm, o_hbm, idx, buf):
        wid = _sc_worker_id()

        @pl.loop(0, nwin // SC_WORKERS)
        def _(t):
            w = t * SC_WORKERS + wid
            pltpu.sync_copy(i_hbm.at[pl.ds(w, 1)], idx)
            pltpu.sync_copy(x_hbm.at[idx.at[0]], buf)
            pltpu.sync_copy(buf, o_hbm.at[pl.ds(w * SC_WINDOW, SC_WINDOW)])

    return gather(table, index.reshape(nwin, SC_WINDOW))


def _combine_planes_kernel(r_ref, ya_ref, yb_ref, x_ref, gate_ref, gf_ref, o_ref):
    rec = r_ref[...]
    y = (_unpack_bf16_pairs(ya_ref[...]).astype(F32) * rec[:, ROUTE_W1:ROUTE_W1 + 1]
         + _unpack_bf16_pairs(yb_ref[...]).astype(F32) * rec[:, ROUTE_W2:ROUTE_W2 + 1])
    x2 = x_ref[...] + gate_ref[...] * y
    o_ref[...] = x2 * lax.rsqrt(jnp.mean(x2 * x2, axis=-1, keepdims=True) + EPS) * gf_ref[...]


def _combine_planes(g, route, x1, gate2, g_final, tm=512):
    bsz, L, d = x1.shape
    tpb = L // tm
    dp = g.shape[-1]
    return pl.pallas_call(
        _combine_planes_kernel,
        grid=(bsz, tpb),
        in_specs=[pl.BlockSpec((tm, LANES), lambda b, i: (b * tpb + i, 0)),
                  pl.BlockSpec((None, tm, dp), lambda b, i: (0, b * tpb + i, 0)),
                  pl.BlockSpec((None, tm, dp), lambda b, i: (1, b * tpb + i, 0)),
                  pl.BlockSpec((None, tm, d), lambda b, i: (b, i, 0)),
                  pl.BlockSpec((None, 1, d), lambda b, i: (b, 0, 0)),
                  pl.BlockSpec((1, d), lambda b, i: (0, 0))],
        out_specs=pl.BlockSpec((None, tm, d), lambda b, i: (b, i, 0)),
        out_shape=jax.ShapeDtypeStruct((bsz, L, d), F32),
        compiler_params=_cparams("parallel", "parallel"),
        name="moe_combine",
    )(route, g, g, x1, gate2, g_final.reshape(1, d))


def _moe(h2, x1, gate2, g_final, w_group, b_group, w_router, b_router, w1_e, w3_e, w2_e):
    bsz, L, d = x1.shape
    n = bsz * L
    h2f = h2.reshape(n, h2.shape[-1])
    route = _router(h2f, w_group, b_group, w_router, b_router)
    dest_rec, counts = _slots(route)
    nb = (2 * n) // MOE_BLOCK + N_EXPERTS
    cnt = counts[0, :N_EXPERTS].astype(jnp.int32)
    blocks_per_e = (cnt + MOE_BLOCK - 1) // MOE_BLOCK
    ends = jnp.cumsum(blocks_per_e)
    block_e = jnp.minimum(jnp.sum(ends[None, :] <= jnp.arange(nb, dtype=jnp.int32)[:, None], axis=1),
                          N_EXPERTS - 1).astype(jnp.int32)
    n_used = ends[-1:].astype(jnp.int32)
    n_slots = nb * MOE_BLOCK
    pad_j = jnp.arange(MOE_BLOCK, dtype=jnp.int32)[None, :]
    spare = n_slots + jnp.arange(N_EXPERTS * MOE_BLOCK, dtype=jnp.int32).reshape(N_EXPERTS, MOE_BLOCK)
    first_slot = ((ends - blocks_per_e) * MOE_BLOCK)[:, None]
    is_pad = cnt[:, None] + pad_j < blocks_per_e[:, None] * MOE_BLOCK
    pad_slots = jnp.where(is_pad, first_slot + cnt[:, None] + pad_j, spare).reshape(-1)
    xs = _sc_dispatch(h2f, dest_rec[:, 0], dest_rec[:, 1], pad_slots, n_slots + N_EXPERTS * MOE_BLOCK)
    ys = _experts(xs, nb, block_e, n_used, w1_e, w3_e, w2_e)
    g = _sc_gather(ys, jnp.concatenate([dest_rec[:, 0], dest_rec[:, 1]]))
    return _combine_planes(g.reshape(2, n, g.shape[-1]), route, x1, gate2, g_final)


def kernel(x, c, ctx, c_ctx, w_mod, b_mod, g_norm1, g_norm2, w_in, b_in, w_qk_conv, b_qk_conv,
           w_h_conv, b_h_conv, hf_w1, hf_b1, hf_w2, hf_b2, hf_w3, hf_freq, h_bias, w_a, w_b, w_out,
           w_group, b_group, w_router, b_router, w1_e, w3_e, w2_e, g_final):
    assert w_mod.shape[0] == 1, "single-layer block"
    (w_mod, b_mod, g_norm1, g_norm2, w_in, b_in, w_qk_conv, b_qk_conv, w_h_conv, b_h_conv, hf_w1, hf_b1, hf_w2,
     hf_b2, hf_w3, hf_freq, h_bias, w_a, w_b, w_out, w_group, b_group, w_router, b_router, w1_e, w3_e, w2_e) = (
        t[0] for t in (w_mod, b_mod, g_norm1, g_norm2, w_in, b_in, w_qk_conv, b_qk_conv, w_h_conv, b_h_conv,
                       hf_w1, hf_b1, hf_w2, hf_b2, hf_w3, hf_freq, h_bias, w_a, w_b, w_out, w_group, b_group,
                       w_router, b_router, w1_e, w3_e, w2_e))
    bsz, L, d = x.shape
    lc = ctx.shape[1]
    seg = L // (L // GRID_W)
    chunk_c = min(lc, MLSTM_CHUNK)
    assert bsz + 1 <= 8 and lc % chunk_c == 0 and L % MLSTM_CHUNK == 0

    cond = jnp.zeros((8, d), F32).at[:bsz].set(c).at[bsz].set(c_ctx)
    mod = _adaln(cond, w_mod, b_mod).reshape(8, 6, d)
    modx = mod[:bsz]
    shift1, scale1, gate1, shift2, scale2, gate2 = (modx[:, i:i + 1] for i in range(6))
    shift1c = jnp.broadcast_to(mod[bsz, 0].reshape(1, 1, d), (bsz, 1, d))
    scale1c = jnp.broadcast_to(mod[bsz, 1].reshape(1, 1, d), (bsz, 1, d))

    w_in16 = w_in.astype(BF16)
    k_scale = jnp.full((M_WIDTH,), M_HEAD_DIM ** -0.5, F32)
    qk_scale = jnp.concatenate([jnp.ones((M_WIDTH,), F32), k_scale])
    w_gates, b_gates = w_in[:, IG0:M_COLS], b_in[IG0:M_COLS]

    hc = _norm_mod(ctx, g_norm1, shift1c, scale1c, lc)
    kc = _proj_conv_silu(hc, w_in16[:, K0:V0], b_in[K0:V0], w_qk_conv[:, M_WIDTH:], b_qk_conv[M_WIDTH:],
                         k_scale, lc, lc)
    vc = _proj_act(hc, w_in16[:, V0:O0], b_in[V0:O0], "none", BF16, lc)
    bcc, acc, arc = _gates(hc, w_gates, b_gates, chunk_c)
    zero_state = (jnp.zeros((bsz, 2, M_HEADS, M_HEAD_DIM, M_HEAD_DIM), F32),
                  jnp.zeros((bsz, 2, M_HEADS, 1, M_HEAD_DIM), F32),
                  jnp.zeros((bsz, 2, M_HEADS, 1, LANES), F32))
    _, ctx_state = _mlstm(None, (kc, 0), (vc, 0), bcc, acc, arc, zero_state, False, chunk_c)

    tm = 1024
    w_main = jnp.concatenate([w_in16[:, Q0:IG0], w_in16[:, GA0:IN_COLS]], axis=1)
    b_main = jnp.concatenate([b_in[Q0:IG0], b_in[GA0:IN_COLS]])
    pm, h = _proj_main(x, g_norm1, shift1, scale1, w_main, b_main, w_qk_conv, b_qk_conv, qk_scale, seg, tm)
    bc, ac, ar = _gates(h, w_gates, b_gates, MLSTM_CHUNK)
    hdirs, _ = _mlstm((pm, PM_Q), (pm, PM_K), (pm, PM_V), bc, ac, ar, ctx_state, True, MLSTM_CHUNK)

    _, dft_fast = _dft_factors(2 * L)
    x0_t, s_t = _proj_hyena(h, w_in16[:, HY0:GA0], b_in[HY0:GA0], w_h_conv, b_h_conv, seg, tm, dft_fast)
    hy = _hyena_long_conv(s_t, x0_t, h_bias, hf_w1, hf_b1, hf_w2, hf_b2, hf_w3, hf_freq)

    x1, h2 = _merge(hdirs, pm, hy, x, gate1, g_norm2, shift2, scale2,
                    w_a.astype(BF16), w_b.astype(BF16), w_out.astype(BF16))
    return _moe(h2, x1, gate2, g_final, w_group, b_group, w_router, b_router, w1_e, w3_e, w2_e)
```

```python
import functools
import math

import jax
import jax.numpy as jnp
import numpy as np
from jax import lax
from jax.experimental import pallas as pl
from jax.experimental.pallas import tpu as pltpu
from jax.experimental.pallas import tpu_sc as plsc

F32 = jnp.float32
BF16 = jnp.bfloat16

D_MODEL = 1024
GRID_W = 64
EPS = 1e-6
M_HEADS = 4
M_HEAD_DIM = 256
M_WIDTH = M_HEADS * M_HEAD_DIM
H_WIDTH = 1024
H_POS_BANDS = 16
H_FILTER_HIDDEN = 64
H_FAST_DECAY_PCT = 0.3
H_SLOW_DECAY_PCT = 1.5
H_DECAY_TARGET = 1e-2
N_GROUPS = 8
EXPERTS_PER_GROUP = 8
N_EXPERTS = N_GROUPS * EXPERTS_PER_GROUP
D_EXPERT = 512
Q0 = 0
K0 = Q0 + M_WIDTH
V0 = K0 + M_WIDTH
O0 = V0 + M_WIDTH
IG0 = O0 + M_WIDTH
FG0 = IG0 + 2 * M_HEADS
M_COLS = FG0 + 2 * M_HEADS
HY0 = M_COLS
GA0 = HY0 + 3 * H_WIDTH
GB0 = GA0 + D_MODEL
IN_COLS = GB0 + D_MODEL

LANES = 128
MLSTM_CHUNK = 512
NEG_BIG = -1e30
VMEM_LIMIT = 48 * 1024 * 1024


def _cparams(*sem):
    return pltpu.CompilerParams(dimension_semantics=sem, vmem_limit_bytes=VMEM_LIMIT)


def _adaln_kernel(c_ref, w_ref, b_ref, o_ref):
    s = c_ref[...]
    s = s * jax.nn.sigmoid(s)
    o_ref[...] = jnp.dot(s.astype(BF16), w_ref[...].astype(BF16), preferred_element_type=F32) + b_ref[...]


def _adaln(cond, w_mod, b_mod):
    n = w_mod.shape[1]
    tn = 1536
    return pl.pallas_call(
        _adaln_kernel,
        grid=(n // tn,),
        in_specs=[pl.BlockSpec((8, D_MODEL), lambda j: (0, 0)),
                  pl.BlockSpec((D_MODEL, tn), lambda j: (0, j)),
                  pl.BlockSpec((1, tn), lambda j: (0, j))],
        out_specs=pl.BlockSpec((8, tn), lambda j: (0, j)),
        out_shape=jax.ShapeDtypeStruct((8, n), F32),
        compiler_params=_cparams("arbitrary"),
        name="adaln",
    )(cond, w_mod, b_mod.reshape(1, n))


def _norm_mod_kernel(x_ref, g_ref, sh_ref, sc_ref, o_ref):
    x = x_ref[...]
    y = x * lax.rsqrt(jnp.mean(x * x, axis=-1, keepdims=True) + EPS)
    y = y * g_ref[...]
    o_ref[...] = (y * (1.0 + sc_ref[...]) + sh_ref[...]).astype(o_ref.dtype)


def _norm_mod(x, g, shift, scale, tm):
    bsz, L, d = x.shape
    return pl.pallas_call(
        _norm_mod_kernel,
        grid=(bsz, L // tm),
        in_specs=[pl.BlockSpec((None, tm, d), lambda b, i: (b, i, 0)),
                  pl.BlockSpec((1, d), lambda b, i: (0, 0)),
                  pl.BlockSpec((None, 1, d), lambda b, i: (b, 0, 0)),
                  pl.BlockSpec((None, 1, d), lambda b, i: (b, 0, 0))],
        out_specs=pl.BlockSpec((None, tm, d), lambda b, i: (b, i, 0)),
        out_shape=jax.ShapeDtypeStruct((bsz, L, d), BF16),
        compiler_params=_cparams("parallel", "parallel"),
        name="norm_mod",
    )(x, g.reshape(1, d), shift, scale)


def _conv3(z, wc, bc, seg):
    tm = z.shape[0]
    pos = lax.broadcasted_iota(jnp.int32, z.shape, 0) & (seg - 1)
    zp = jnp.where(pos == 0, 0.0, pltpu.roll(z, 1, 0))
    zn = jnp.where(pos == seg - 1, 0.0, pltpu.roll(z, tm - 1, 0))
    return zp * wc[0:1, :] + z * wc[1:2, :] + zn * wc[2:3, :] + bc


def _proj_act_kernel(h_ref, w_ref, b_ref, o_ref, *, act):
    z = jnp.dot(h_ref[...], w_ref[...], preferred_element_type=F32) + b_ref[...]
    if act == "sigmoid":
        z = jax.nn.sigmoid(z)
    o_ref[...] = z.astype(o_ref.dtype)


def _proj_act(h, w, b, act, out_dtype, tm, tn=512):
    bsz, L, d = h.shape
    n = w.shape[1]
    return pl.pallas_call(
        functools.partial(_proj_act_kernel, act=act),
        grid=(bsz, L // tm, n // tn),
        in_specs=[pl.BlockSpec((None, tm, d), lambda b_, i, j: (b_, i, 0)),
                  pl.BlockSpec((d, tn), lambda b_, i, j: (0, j)),
                  pl.BlockSpec((1, tn), lambda b_, i, j: (0, j))],
        out_specs=pl.BlockSpec((None, tm, tn), lambda b_, i, j: (b_, i, j)),
        out_shape=jax.ShapeDtypeStruct((bsz, L, n), out_dtype),
        compiler_params=_cparams("parallel", "parallel", "arbitrary"),
        name="proj_" + act,
    )(h, w, b.reshape(1, n))


def _proj_conv_silu_kernel(h_ref, w_ref, b_ref, wc_ref, bc_ref, cs_ref, o_ref, *, seg):
    z = jnp.dot(h_ref[...], w_ref[...], preferred_element_type=F32) + b_ref[...]
    y = _conv3(z, wc_ref[...], bc_ref[...], seg)
    y = y * jax.nn.sigmoid(y)
    o_ref[...] = (y * cs_ref[...]).astype(o_ref.dtype)


def _proj_conv_silu(h, w, b, wc, bc, colscale, seg, tm, tn=512):
    bsz, L, d = h.shape
    n = w.shape[1]
    col = lambda b_, i, j: (0, j)
    return pl.pallas_call(
        functools.partial(_proj_conv_silu_kernel, seg=seg),
        grid=(bsz, L // tm, n // tn),
        in_specs=[pl.BlockSpec((None, tm, d), lambda b_, i, j: (b_, i, 0)),
                  pl.BlockSpec((d, tn), col),
                  pl.BlockSpec((1, tn), col),
                  pl.BlockSpec((3, tn), col),
                  pl.BlockSpec((1, tn), col),
                  pl.BlockSpec((1, tn), col)],
        out_specs=pl.BlockSpec((None, tm, tn), lambda b_, i, j: (b_, i, j)),
        out_shape=jax.ShapeDtypeStruct((bsz, L, n), BF16),
        compiler_params=_cparams("parallel", "parallel", "arbitrary"),
        name="proj_conv_silu",
    )(h, w, b.reshape(1, n), wc, bc.reshape(1, n), colscale.reshape(1, n))


PROJ_TN = 1024
PROJ_SUB = 512
PM_Q, PM_K, PM_V, PM_O, PM_GA, PM_GB = range(6)


def _proj_main_kernel(x_ref, g_ref, sh_ref, sc_ref, w_ref, b_ref, wc_ref, bc_ref, cs_ref,
                      o_ref, h_ref, hi_hbm, hp_sc, sem, *, seg):
    b, i, j = pl.program_id(0), pl.program_id(1), pl.program_id(2)
    n2, jt = hi_hbm.shape[2], hi_hbm.shape[3]

    def interleave_copy(jj):
        return pltpu.make_async_copy(hp_sc.at[pl.ds(jj * n2, n2)], hi_hbm.at[b, i, :, jj, :], sem)

    @pl.when(j == 0)
    def _():
        x = x_ref[...]
        y = x * lax.rsqrt(jnp.mean(x * x, axis=-1, keepdims=True) + EPS) * g_ref[...]
        y = y * (1.0 + sc_ref[...]) + sh_ref[...]
        h_ref[...] = y.astype(h_ref.dtype)
        hp_sc[...] = _pack_bf16_pairs(y)
        for jj in range(jt):
            interleave_copy(jj).start()

    @pl.when(j == pl.num_programs(2) - 1)
    def _():
        for jj in range(jt):
            interleave_copy(jj).wait()

    def run(epilogue):
        for c in range(PROJ_TN // PROJ_SUB):
            sl = slice(c * PROJ_SUB, (c + 1) * PROJ_SUB)
            z = jnp.dot(h_ref[...], w_ref[:, sl], preferred_element_type=F32) + b_ref[:, sl]
            o_ref[:, sl] = epilogue(z, sl).astype(o_ref.dtype)

    def conv_silu(z, sl):
        y = _conv3(z, wc_ref[:, sl], bc_ref[:, sl], seg)
        return (y * jax.nn.sigmoid(y)) * cs_ref[:, sl]

    @pl.when(j <= PM_K)
    def _():
        run(conv_silu)

    @pl.when(j == PM_V)
    def _():
        run(lambda z, sl: z)

    @pl.when(j >= PM_O)
    def _():
        run(lambda z, sl: jax.nn.sigmoid(z))


def _proj_main(x, g, shift, scale, w, b, wc, bc, colscale, seg, tm, n2):
    bsz, L, d = x.shape
    n = w.shape[1]
    jt = tm // n2
    qk = lambda b_, i, j: (0, jnp.minimum(j, PM_K))
    row = pl.BlockSpec((None, tm, d), lambda b_, i, j: (b_, i, 0))
    bvec = pl.BlockSpec((None, 1, d), lambda b_, i, j: (b_, 0, 0))
    return pl.pallas_call(
        functools.partial(_proj_main_kernel, seg=seg),
        grid=(bsz, L // tm, n // PROJ_TN),
        in_specs=[row, pl.BlockSpec((1, d), lambda b_, i, j: (0, 0)), bvec, bvec,
                  pl.BlockSpec((d, PROJ_TN), lambda b_, i, j: (0, j)),
                  pl.BlockSpec((1, PROJ_TN), lambda b_, i, j: (0, j)),
                  pl.BlockSpec((3, PROJ_TN), qk),
                  pl.BlockSpec((1, PROJ_TN), qk),
                  pl.BlockSpec((1, PROJ_TN), qk)],
        out_specs=[pl.BlockSpec((None, tm, PROJ_TN), lambda b_, i, j: (b_, i, j)), row,
                   pl.BlockSpec(memory_space=pl.ANY)],
        out_shape=[jax.ShapeDtypeStruct((bsz, L, n), BF16), jax.ShapeDtypeStruct((bsz, L, d), BF16),
                   jax.ShapeDtypeStruct((bsz, L // tm, n2, jt, d // 2), jnp.uint32)],
        scratch_shapes=[pltpu.VMEM((tm, d // 2), jnp.uint32), pltpu.SemaphoreType.DMA(())],
        compiler_params=_cparams("parallel", "parallel", "arbitrary"),
        name="proj_main",
    )(x, g.reshape(1, d), shift, scale, w, b.reshape(1, n), wc, bc.reshape(1, -1), colscale.reshape(1, -1))


def _conv3_interleaved(z, wc, bc, seg, jt):
    grp = seg * jt
    pad = jnp.zeros((jt, z.shape[1]), z.dtype)
    prev, nxt = [], []
    for g0 in range(0, z.shape[0], grp):
        zg = z[g0:g0 + grp]
        prev += [pad, zg[:grp - jt]]
        nxt += [zg[jt:], pad]
    zp = jnp.concatenate(prev, axis=0)
    zn = jnp.concatenate(nxt, axis=0)
    return zp * wc[0:1, :] + z * wc[1:2, :] + zn * wc[2:3, :] + bc


def _proj_hyena_kernel(h_ref, w0_ref, w1_ref, w2_ref, b_ref, wc_ref, bc_ref, x0_ref, s_ref, *, seg):
    n2, jt = s_ref.shape[0], s_ref.shape[1]
    h = _unpack_bf16_pairs(h_ref[...].reshape(n2 * jt, h_ref.shape[2]))
    us = []
    for g, w_ref in enumerate((w0_ref, w1_ref, w2_ref)):
        z = jnp.dot(h, w_ref[...], preferred_element_type=F32) + b_ref[g]
        us.append(_conv3_interleaved(z, wc_ref[g], bc_ref[g], seg, jt))
    x0_ref[...] = _pack_bf16_pairs(us[0]).reshape(x0_ref.shape)
    s_ref[...] = (us[1] * us[2]).reshape(s_ref.shape)


def _proj_hyena(hi, w, b, wc, bc, seg):
    bsz, nt, n2, jt, dp = hi.shape
    d, tm = 2 * dp, n2 * jt
    L = nt * tm
    tn = DFT_C_TILE
    nblk = H_WIDTH // tn
    assert n2 % seg == 0 and (jt % 8 == 0 or nt == 1)
    b3 = b.reshape(3, 1, H_WIDTH)
    wc3 = wc.reshape(3, 3, H_WIDTH).transpose(1, 0, 2)
    bc3 = bc.reshape(3, 1, H_WIDTH)
    return pl.pallas_call(
        functools.partial(_proj_hyena_kernel, seg=seg),
        grid=(bsz, nt, nblk),
        in_specs=[pl.BlockSpec((None, None, n2, jt, dp), lambda b_, i, j: (b_, i, 0, 0, 0)),
                  pl.BlockSpec((d, tn), lambda b_, i, j: (0, j)),
                  pl.BlockSpec((d, tn), lambda b_, i, j: (0, nblk + j)),
                  pl.BlockSpec((d, tn), lambda b_, i, j: (0, 2 * nblk + j)),
                  pl.BlockSpec((3, 1, tn), lambda b_, i, j: (0, 0, j)),
                  pl.BlockSpec((3, 3, tn), lambda b_, i, j: (0, 0, j)),
                  pl.BlockSpec((3, 1, tn), lambda b_, i, j: (0, 0, j))],
        out_specs=[pl.BlockSpec((None, n2, jt, tn // 2), lambda b_, i, j: (b_, 0, i, j)),
                   pl.BlockSpec((None, n2, jt, tn), lambda b_, i, j: (b_, 0, i, j))],
        out_shape=[jax.ShapeDtypeStruct((bsz, n2, L // n2, H_WIDTH // 2), jnp.uint32),
                   jax.ShapeDtypeStruct((bsz, n2, L // n2, H_WIDTH), F32)],
        compiler_params=_cparams("parallel", "parallel", "arbitrary"),
        name="proj_hyena",
    )(hi, w, w, w, b3, wc3, bc3)


N_GATES = 4 * M_HEADS


def _split3(x):
    hi = x.astype(BF16)
    r1 = x - hi.astype(F32)
    mid = r1.astype(BF16)
    lo = (r1 - mid.astype(F32)).astype(BF16)
    return hi, mid, lo


def _log_sigmoid(x):
    return jnp.minimum(x, 0.0) - jnp.log1p(jnp.exp(-jnp.abs(x)))


def _gates_kernel(h_ref, w_ref, wt_ref, b_ref, bt_ref, bc_ref, ac_ref, ar_ref):
    h = h_ref[...]
    t = h.shape[0]
    z = jnp.dot(h, w_ref[...], preferred_element_type=F32) + b_ref[...]
    zt = lax.dot_general(wt_ref[...], h, (((1,), (1,)), ((), ())),
                         preferred_element_type=F32) + bt_ref[...]
    r = lax.broadcasted_iota(jnp.int32, (t, t), 0)
    c = lax.broadcasted_iota(jnp.int32, (t, t), 1)
    lower = (r >= c).astype(BF16)
    upper = (r <= c).astype(BF16)
    g8 = FG_LANE0

    lf = _log_sigmoid(z)
    lane = lax.broadcasted_iota(jnp.int32, z.shape, 1)
    is_fg = (lane >= g8) & (lane < 2 * g8)
    terms = [jnp.where(is_fg, p.astype(F32), 0.0) for p in _split3(lf)]
    packed = terms[0] + pltpu.roll(terms[1], 2 * g8, 1) + pltpu.roll(terms[2], 4 * g8, 1)
    cfp = jnp.dot(lower, packed.astype(BF16), preferred_element_type=F32)
    cf = cfp + pltpu.roll(cfp, LANES - 2 * g8, 1) + pltpu.roll(cfp, LANES - 4 * g8, 1)
    cb = cf[t - 1:t, :] - cf + lf
    bc = jnp.where(lane < g8 + M_HEADS, cf, cb)
    bc = pltpu.roll(bc, LANES - g8, 1)
    bc_ref[...] = bc
    ac_ref[...] = z - bc

    lft = _log_sigmoid(zt[g8:, :])
    stacked = jnp.concatenate([p.astype(F32) for p in _split3(lft)] + [jnp.zeros_like(lft)], axis=0)
    cft3 = jnp.dot(stacked.astype(BF16), upper, preferred_element_type=F32)
    cft = cft3[0:g8] + cft3[g8:2 * g8] + cft3[2 * g8:3 * g8]
    cbt = cft[:, t - 1:t] - cft + lft
    row = lax.broadcasted_iota(jnp.int32, cft.shape, 0)
    ar_ref[...] = zt[:g8, :] - jnp.where(row < M_HEADS, cft, cbt)


FG_LANE0 = 2 * M_HEADS


def _gates(h, w_g, b_g, chunk):
    bsz, L, d = h.shape
    w_pad = jnp.zeros((d, LANES), F32).at[:, :N_GATES].set(w_g).astype(BF16)
    b_pad = jnp.zeros((1, LANES), F32).at[0, :N_GATES].set(b_g)
    wt = w_g.T.astype(BF16)
    bt = b_g.reshape(N_GATES, 1)
    tok = pl.BlockSpec((None, chunk, LANES), lambda b_, i: (b_, i, 0))
    return pl.pallas_call(
        _gates_kernel,
        grid=(bsz, L // chunk),
        in_specs=[pl.BlockSpec((None, chunk, d), lambda b_, i: (b_, i, 0)),
                  pl.BlockSpec((d, LANES), lambda b_, i: (0, 0)),
                  pl.BlockSpec((N_GATES, d), lambda b_, i: (0, 0)),
                  pl.BlockSpec((1, LANES), lambda b_, i: (0, 0)),
                  pl.BlockSpec((N_GATES, 1), lambda b_, i: (0, 0))],
        out_specs=[tok, tok, pl.BlockSpec((None, FG_LANE0, chunk), lambda b_, i: (b_, 0, i))],
        out_shape=[jax.ShapeDtypeStruct((bsz, L, LANES), F32),
                   jax.ShapeDtypeStruct((bsz, L, LANES), F32),
                   jax.ShapeDtypeStruct((bsz, FG_LANE0, L), F32)],
        compiler_params=_cparams("parallel", "parallel"),
        name="mlstm_gates",
    )(h, w_pad, wt, b_pad, bt)


def _mlstm_kernel(*refs, emit_h, n_chunks):
    if emit_h:
        (q_ref, k_ref, v_ref, bc_ref, ac_ref, ar_ref, c0_ref, n0_ref, m0_ref,
         h_ref, cf_ref, nf_ref, mf_ref, c_sc, n_sc, m_sc) = refs
    else:
        (k_ref, v_ref, bc_ref, ac_ref, ar_ref, c0_ref, n0_ref, m0_ref,
         cf_ref, nf_ref, mf_ref, c_sc, n_sc, m_sc) = refs
    d = pl.program_id(1)
    j = pl.program_id(2)
    fwd = d == 0
    t = k_ref.shape[0]
    dh = M_HEAD_DIM

    @pl.when(j == 0)
    def _():
        c_sc[...] = c0_ref[...]
        n_sc[...] = n0_ref[...]
        m_sc[...] = m0_ref[...]

    r = lax.broadcasted_iota(jnp.int32, (t, t), 0)
    c = lax.broadcasted_iota(jnp.int32, (t, t), 1)
    causal = jnp.where(fwd, r - c, c - r) >= 0
    bc_all = bc_ref[...]
    ac_all = ac_ref[...]
    ar_all = ar_ref[...]
    for hd in range(M_HEADS):
        sl = slice(hd * dh, (hd + 1) * dh)
        bc = jnp.where(fwd, bc_all[:, hd:hd + 1], bc_all[:, M_HEADS + hd:M_HEADS + hd + 1])
        ac = jnp.where(fwd, ac_all[:, hd:hd + 1], ac_all[:, M_HEADS + hd:M_HEADS + hd + 1])
        ar = jnp.where(fwd, ar_all[hd:hd + 1, :], ar_all[M_HEADS + hd:M_HEADS + hd + 1, :])
        b_tot = jnp.where(fwd, bc[t - 1:t, :], bc[0:1, :])
        m_prev = m_sc[hd][:, 0:1]
        k_h = k_ref[:, sl]
        v_h = v_ref[:, sl]
        if emit_h:
            q_h = q_ref[:, sl]
            dm = jnp.where(causal, bc + ar, NEG_BIG)
            inter = bc + m_prev
            m_t = jnp.maximum(inter, jnp.max(dm, axis=1, keepdims=True))
            qk = lax.dot_general(q_h, k_h, (((1,), (1,)), ((), ())), preferred_element_type=F32)
            s = qk * jnp.exp(dm - m_t)
            carry = jnp.exp(inter - m_t)
            num = (jnp.dot(s.astype(BF16), v_h, preferred_element_type=F32)
                   + carry * jnp.dot(q_h, c_sc[hd].astype(BF16), preferred_element_type=F32))
            den = (jnp.sum(s, axis=1, keepdims=True)
                   + carry * jnp.sum(q_h.astype(F32) * n_sc[hd], axis=1, keepdims=True))
            h_ref[:, sl] = (num / jnp.maximum(jnp.abs(den), jnp.exp(-m_t))).astype(h_ref.dtype)
        g = b_tot + ac
        m_new = jnp.maximum(b_tot + m_prev, jnp.max(g, axis=0, keepdims=True))
        wgt = jnp.exp(g - m_new)
        decay = jnp.exp(b_tot + m_prev - m_new)
        kw = k_h.astype(F32) * wgt
        c_sc[hd] = decay * c_sc[hd] + lax.dot_general(kw.astype(BF16), v_h, (((0,), (0,)), ((), ())),
                                                      preferred_element_type=F32)
        n_sc[hd] = decay * n_sc[hd] + jnp.sum(kw, axis=0, keepdims=True)
        m_sc[hd] = jnp.broadcast_to(m_new, (1, LANES))

    @pl.when(j == n_chunks - 1)
    def _():
        cf_ref[...] = c_sc[...]
        nf_ref[...] = n_sc[...]
        mf_ref[...] = m_sc[...]


def _mlstm(q, k, v, bc, ac, ar, state, emit_h, t):
    bsz, L, _ = k[0].shape
    nc = L // t
    seq = lambda b_, d, j: (b_, j + d * (nc - 1 - 2 * j), 0)
    st = lambda b_, d, j: (b_, d, 0, 0, 0)

    def tok(col):
        return pl.BlockSpec((None, t, M_WIDTH), lambda b_, d, j: (b_, j + d * (nc - 1 - 2 * j), col))

    gate_spec = pl.BlockSpec((None, t, LANES), seq)
    ar_spec = pl.BlockSpec((None, FG_LANE0, t), lambda b_, d, j: (b_, 0, j + d * (nc - 1 - 2 * j)))
    c_spec = pl.BlockSpec((None, None, M_HEADS, M_HEAD_DIM, M_HEAD_DIM), st)
    n_spec = pl.BlockSpec((None, None, M_HEADS, 1, M_HEAD_DIM), st)
    m_spec = pl.BlockSpec((None, None, M_HEADS, 1, LANES), st)
    state_shapes = [jax.ShapeDtypeStruct((bsz, 2, M_HEADS, M_HEAD_DIM, M_HEAD_DIM), F32),
                    jax.ShapeDtypeStruct((bsz, 2, M_HEADS, 1, M_HEAD_DIM), F32),
                    jax.ShapeDtypeStruct((bsz, 2, M_HEADS, 1, LANES), F32)]
    in_specs = [tok(k[1]), tok(v[1]), gate_spec, gate_spec, ar_spec, c_spec, n_spec, m_spec]
    args = [k[0], v[0], bc, ac, ar, *state]
    out_specs = [c_spec, n_spec, m_spec]
    out_shape = list(state_shapes)
    if emit_h:
        in_specs = [tok(q[1])] + in_specs
        args = [q[0]] + args
        out_specs = [pl.BlockSpec((None, None, t, M_WIDTH),
                                  lambda b_, d, j: (d, b_, j + d * (nc - 1 - 2 * j), 0))] + out_specs
        out_shape = [jax.ShapeDtypeStruct((2, bsz, L, M_WIDTH), BF16)] + out_shape
    outs = pl.pallas_call(
        functools.partial(_mlstm_kernel, emit_h=emit_h, n_chunks=nc),
        grid=(bsz, 2, nc),
        in_specs=in_specs,
        out_specs=out_specs,
        out_shape=out_shape,
        scratch_shapes=[pltpu.VMEM((M_HEADS, M_HEAD_DIM, M_HEAD_DIM), F32),
                        pltpu.VMEM((M_HEADS, 1, M_HEAD_DIM), F32),
                        pltpu.VMEM((M_HEADS, 1, LANES), F32)],
        compiler_params=_cparams("parallel", "parallel", "arbitrary"),
        name="mlstm" if emit_h else "mlstm_state",
    )(*args)
    if emit_h:
        return outs[0], tuple(outs[1:])
    return None, tuple(outs)


DFT_M_TILE = 8
DFT_C_TILE = 512
FEAT_ROWS = 16


def _filter_outer_kernel(bands_ref, w1t_ref, b1_ref, w2t_ref, b2_ref, w3p_ref, w3f_ref, fr_ref, dl_ref, l_ref,
                         a_ref, ss_ref, *, L, n1, n2):
    i = pl.program_id(0)
    h = n1 // 2
    cols = DFT_M_TILE * h

    def positions(shape, axis, side):
        q = lax.broadcasted_iota(jnp.int32, shape, axis)
        mm, jj = q // h, q % h
        n = n2 * (jj + side * h) + i * DFT_M_TILE + mm
        return n, jnp.where(n < L, n, 2 * L - n).astype(F32)

    taps = []
    sumsq = jnp.zeros((1, a_ref.shape[-1]), F32)
    for side, w3_ref in ((0, w3p_ref), (1, w3f_ref)):
        _, p_row = positions((1, cols), 1, side)
        t_row = p_row / float(max(L - 1, 1))
        ang = ((2 * math.pi / L) * p_row) * bands_ref[...]
        row = lax.broadcasted_iota(jnp.int32, (FEAT_ROWS, cols), 0)
        feats = jnp.concatenate([jnp.where(row == 0, t_row, 0.0), jnp.cos(ang), -jnp.sin(ang)], axis=0)
        fr = fr_ref[...]
        hid = jnp.sin(fr * (jnp.dot(w1t_ref[...], feats.astype(BF16), preferred_element_type=F32) + b1_ref[...]))
        hid = jnp.sin(fr * (jnp.dot(w2t_ref[...], hid.astype(BF16), preferred_element_type=F32) + b2_ref[...]))
        filt = lax.dot_general(hid.astype(BF16), w3_ref[...], (((0,), (0,)), ((), ())),
                               preferred_element_type=F32)
        n_col, p_col = positions((cols, 1), 0, side)
        t_col = p_col / float(max(L - 1, 1))
        kern = filt * jnp.exp(-t_col * jnp.abs(dl_ref[...]))
        kern = jnp.where(n_col == L, 0.0, kern)
        sumsq = sumsq + jnp.sum(kern * kern, axis=0, keepdims=True)
        taps.append(kern)

    for mm in range(DFT_M_TILE):
        x = jnp.concatenate([taps[0][mm * h:(mm + 1) * h], taps[1][mm * h:(mm + 1) * h]], axis=0)
        out = jnp.dot(l_ref[...], x.astype(BF16), preferred_element_type=F32)
        a_ref[0, :, mm, :] = out[:n1]
        a_ref[1, :, mm, :] = out[n1:]

    @pl.when(i == 0)
    def _():
        ss_ref[...] = jnp.zeros_like(ss_ref)

    ss_ref[...] += sumsq


def _filter_outer(L, n1, n2, fwd_r, w1, b1, w2, b2, w3, freq):
    hid = H_FILTER_HIDDEN
    bands = jnp.linspace(1e-4, H_POS_BANDS - 1, H_POS_BANDS, dtype=F32).reshape(H_POS_BANDS, 1)
    w1t = jnp.zeros((hid, 3 * FEAT_ROWS), F32)
    w1t = w1t.at[:, 0].set(w1[0]).at[:, FEAT_ROWS:2 * FEAT_ROWS].set(w1[1:1 + H_POS_BANDS].T)
    w1t = w1t.at[:, 2 * FEAT_ROWS:].set(w1[1 + H_POS_BANDS:].T).astype(BF16)
    w3h = w3.astype(BF16)
    max_decay = math.log(H_DECAY_TARGET) / H_FAST_DECAY_PCT
    min_decay = math.log(H_DECAY_TARGET) / H_SLOW_DECAY_PCT
    deltas = jnp.linspace(min_decay, max_decay, H_WIDTH, dtype=F32).reshape(1, H_WIDTH)
    col = lambda v: v.reshape(hid, 1)
    full = lambda a: pl.BlockSpec(a.shape, lambda i: (0,) * a.ndim)
    args = [bands, w1t, col(b1), w2.T.astype(BF16), col(b2)]
    return pl.pallas_call(
        functools.partial(_filter_outer_kernel, L=L, n1=n1, n2=n2),
        grid=(n2 // DFT_M_TILE,),
        in_specs=[full(a) for a in args]
        + [pl.BlockSpec((hid, H_WIDTH), lambda i: (0, 0)), pl.BlockSpec((hid, H_WIDTH), lambda i: (0, 1)),
           full(col(freq)), full(deltas), full(fwd_r)],
        out_specs=[pl.BlockSpec((2, n1, DFT_M_TILE, H_WIDTH), lambda i: (0, 0, i, 0)),
                   pl.BlockSpec((1, H_WIDTH), lambda i: (0, 0))],
        out_shape=[jax.ShapeDtypeStruct((2, n1, n2, H_WIDTH), F32),
                   jax.ShapeDtypeStruct((1, H_WIDTH), F32)],
        compiler_params=_cparams("arbitrary"),
        name="hyena_filter_outer",
    )(*args, w3h, w3h, col(freq), deltas, fwd_r)


def _dft_factors(n):
    lg = int(round(math.log2(n)))
    n1 = 1 << ((lg + 1) // 2)
    return n1, n // n1


def _dft_outer_matrices(n1):
    k = np.arange(n1)[:, None]
    n = np.arange(n1)[None, :]
    ang = 2.0 * np.pi * ((k * n) % n1) / n1
    cr, ci = np.cos(ang), -np.sin(ang)
    h = n1 // 2
    fwd_c = np.block([[cr[:, :h], -ci[:, :h]], [ci[:, :h], cr[:, :h]]])
    fwd_r = np.concatenate([cr, ci], axis=0)
    ir, ii = cr[:h, :], -ci[:h, :]
    inv = np.block([[ir, -ii], [ii, ir]])
    return (jnp.asarray(fwd_c, F32).astype(BF16), jnp.asarray(fwd_r, F32).astype(BF16),
            jnp.asarray(inv, F32).astype(BF16))


def _dft_inner_matrices(n1, n2):
    n = n1 * n2
    k2 = np.arange(n2)[:, None]
    m = np.arange(n2)[None, :]
    ang = 2.0 * np.pi * ((k2 * m) % n2) / n2
    fr, fi = np.cos(ang), -np.sin(ang)
    f = np.block([[fr, -fi], [fi, fr]])
    k1 = jnp.arange(n1, dtype=jnp.int32)[:, None]
    tw_ang = ((jnp.arange(n2, dtype=jnp.int32)[None, :] * k1) % n).astype(F32) * (2.0 * math.pi / n)
    rep = lambda t: jnp.broadcast_to(t[:, :, None], (n1, n2, LANES))
    return (jnp.asarray(f, F32).astype(BF16), jnp.asarray(f.T, F32).astype(BF16),
            rep(jnp.cos(tw_ang)), rep(-jnp.sin(tw_ang)))


def _outer_fwd_kernel(l_ref, s_ref, a_ref):
    n1 = a_ref.shape[1]
    for mm in range(s_ref.shape[1]):
        x = jnp.concatenate([s_ref[0, mm], s_ref[1, mm]], axis=0).astype(BF16)
        out = jnp.dot(l_ref[...], x, preferred_element_type=F32)
        a_ref[0, :, mm, :] = out[:n1]
        a_ref[1, :, mm, :] = out[n1:]


def _outer_fwd(lmat, s_t):
    _, n2, n1h, c = s_t.shape
    n1 = 2 * n1h
    tc = min(DFT_C_TILE, c)
    return pl.pallas_call(
        _outer_fwd_kernel,
        grid=(n2 // DFT_M_TILE, c // tc),
        in_specs=[pl.BlockSpec(lmat.shape, lambda m, j: (0, 0)),
                  pl.BlockSpec((2, DFT_M_TILE, n1h, tc), lambda m, j: (0, m, 0, j))],
        out_specs=pl.BlockSpec((2, n1, DFT_M_TILE, tc), lambda m, j: (0, 0, m, j)),
        out_shape=jax.ShapeDtypeStruct((2, n1, n2, c), F32),
        compiler_params=_cparams("parallel", "parallel"),
        name="dft_outer_fwd",
    )(lmat, s_t)


def _outer_inv_kernel(l_ref, b_ref, s_ref, x0_ref, ysc_ref, hb_ref, o_ref):
    n1h = s_ref.shape[2]
    for mm in range(b_ref.shape[1]):
        y = jnp.concatenate([b_ref[0, mm], b_ref[1, mm]], axis=0).astype(BF16)
        out = jnp.dot(l_ref[...], y, preferred_element_type=F32)
        for b in range(2):
            conv = out[b * n1h:(b + 1) * n1h]
            x0 = _unpack_bf16_pairs(x0_ref[b, mm]).astype(F32)
            hy = x0 * (conv * ysc_ref[...] + hb_ref[...] * s_ref[b, mm])
            o_ref[b, :, mm, :] = _pack_bf16_pairs(hy)


def _outer_inv(lmat, b_t, s_t, x0_t, yscale, h_bias):
    _, n2, n1, c = b_t.shape
    n1h = n1 // 2
    tc = min(DFT_C_TILE, c)
    vec = pl.BlockSpec((1, tc), lambda m, j: (0, j))
    hy = pl.pallas_call(
        _outer_inv_kernel,
        grid=(n2 // DFT_M_TILE, c // tc),
        in_specs=[pl.BlockSpec(lmat.shape, lambda m, j: (0, 0)),
                  pl.BlockSpec((2, DFT_M_TILE, n1, tc), lambda m, j: (0, m, 0, j)),
                  pl.BlockSpec((2, DFT_M_TILE, n1h, tc), lambda m, j: (0, m, 0, j)),
                  pl.BlockSpec((2, DFT_M_TILE, n1h, tc // 2), lambda m, j: (0, m, 0, j)),
                  vec, vec],
        out_specs=pl.BlockSpec((2, n1h, DFT_M_TILE, tc // 2), lambda m, j: (0, 0, m, j)),
        out_shape=jax.ShapeDtypeStruct((2, n1h, n2, c // 2), jnp.uint32),
        compiler_params=_cparams("parallel", "parallel"),
        name="dft_outer_inv",
    )(lmat, b_t, s_t, x0_t, yscale, h_bias.reshape(1, c))
    return hy.reshape(2, n1h * n2, c // 2)


DFT_K_TILE = 8


def _twiddled_inner_dft(f_ref, twr_ref, twi_ref, a_ref, kk):
    n2, c = a_ref.shape[2], a_ref.shape[3]
    twr = jnp.tile(twr_ref[kk], (1, c // LANES))
    twi = jnp.tile(twi_ref[kk], (1, c // LANES))
    ar, ai = a_ref[0, kk], a_ref[1, kk]
    a = jnp.concatenate([(ar * twr - ai * twi).astype(BF16), (ar * twi + ai * twr).astype(BF16)], axis=0)
    x = jnp.dot(f_ref[...], a, preferred_element_type=F32)
    return x[:n2], x[n2:], twr, twi


def _inner_fwd_kernel(f_ref, twr_ref, twi_ref, a_ref, o_ref):
    for kk in range(a_ref.shape[1]):
        xr, xi, _, _ = _twiddled_inner_dft(f_ref, twr_ref, twi_ref, a_ref, kk)
        o_ref[0, kk] = xr.astype(o_ref.dtype)
        o_ref[1, kk] = xi.astype(o_ref.dtype)


def _inner_specs(n1, n2, c):
    tc = min(DFT_C_TILE, c)
    kt = min(DFT_K_TILE, n1)
    blk = pl.BlockSpec((2, kt, n2, tc), lambda k, j: (0, k, 0, j))
    mat = pl.BlockSpec((2 * n2, 2 * n2), lambda k, j: (0, 0))
    tw = pl.BlockSpec((kt, n2, LANES), lambda k, j: (k, 0, 0))
    return blk, mat, tw, (n1 // kt, c // tc), kt, tc


def _inner_fwd(f, twr, twi, a):
    _, n1, n2, c = a.shape
    blk, mat, tw, grid, _, _ = _inner_specs(n1, n2, c)
    return pl.pallas_call(
        _inner_fwd_kernel,
        grid=grid,
        in_specs=[mat, tw, tw, blk],
        out_specs=blk,
        out_shape=jax.ShapeDtypeStruct((2, n1, n2, c), BF16),
        compiler_params=_cparams("parallel", "parallel"),
        name="dft_inner_filter",
    )(f, twr, twi, a)


def _inner_conv_kernel(f_ref, ft_ref, twr_ref, twi_ref, a_ref, k_ref, o_ref):
    n2 = a_ref.shape[2]
    for kk in range(a_ref.shape[1]):
        xr, xi, twr, twi = _twiddled_inner_dft(f_ref, twr_ref, twi_ref, a_ref, kk)
        kr, ki = k_ref[0, kk].astype(F32), k_ref[1, kk].astype(F32)
        yr = xr * kr - xi * ki
        yi = xr * ki + xi * kr
        y = jnp.concatenate([yr.astype(BF16), yi.astype(BF16)], axis=0)
        b = jnp.dot(ft_ref[...], y, preferred_element_type=F32)
        br, bi = b[:n2], b[n2:]
        o_ref[0, :, kk, :] = br * twr + bi * twi
        o_ref[1, :, kk, :] = bi * twr - br * twi


def _inner_conv(f, ft, twr, twi, a, kf):
    _, n1, n2, c = a.shape
    blk, mat, tw, grid, kt, tc = _inner_specs(n1, n2, c)
    return pl.pallas_call(
        _inner_conv_kernel,
        grid=grid,
        in_specs=[mat, mat, tw, tw, blk, blk],
        out_specs=pl.BlockSpec((2, n2, kt, tc), lambda k, j: (0, 0, k, j)),
        out_shape=jax.ShapeDtypeStruct((2, n2, n1, c), F32),
        compiler_params=_cparams("parallel", "parallel"),
        name="dft_inner_conv",
    )(f, ft, twr, twi, a, kf)


def _hyena_long_conv(s_t, x0_t, h_bias, w1, b1, w2, b2, w3, freq):
    bsz, n2, n1h, c = s_t.shape
    assert bsz == 2
    n1 = 2 * n1h
    L = n1h * n2
    fwd_c, fwd_r, inv = _dft_outer_matrices(n1)
    f, ft, twr, twi = _dft_inner_matrices(n1, n2)
    af, sumsq = _filter_outer(L, n1, n2, fwd_r, w1, b1, w2, b2, w3, freq)
    kf = _inner_fwd(f, twr, twi, af)
    a = _outer_fwd(fwd_c, s_t)
    b_t = _inner_conv(f, ft, twr, twi, a, kf)
    yscale = lax.rsqrt(sumsq + EPS) * (1.0 / (2 * L))
    return _outer_inv(inv, b_t, s_t, x0_t, yscale, h_bias)


def _pack_bf16_pairs(x):
    half = x.shape[1] // 2
    lo = pltpu.bitcast(x[:, :half].astype(BF16).astype(F32), jnp.uint32) >> 16
    hi = pltpu.bitcast(x[:, half:].astype(BF16).astype(F32), jnp.uint32) & jnp.uint32(0xFFFF0000)
    return lo | hi


def _unpack_bf16_pairs(p):
    lo = pltpu.bitcast(p << 16, F32).astype(BF16)
    hi = pltpu.bitcast(p & jnp.uint32(0xFFFF0000), F32).astype(BF16)
    return jnp.concatenate([lo, hi], axis=1)


def _merge_kernel(hf_ref, hb_ref, o_ref, hy_ref, ga_ref, gb_ref, x_ref,
                  gate_ref, g2_ref, sh_ref, sc_ref, wa_ref, wb_ref, wo_ref, x1_ref, h2_ref):
    a = o_ref[...].astype(F32) * (hf_ref[...].astype(F32) + hb_ref[...].astype(F32))
    half = DFT_C_TILE // 2
    hy = jnp.concatenate([_unpack_bf16_pairs(hy_ref[:, c * half:(c + 1) * half])
                          for c in range(hy_ref.shape[1] // half)], axis=1)
    pa = jnp.dot(a.astype(BF16), wa_ref[...], preferred_element_type=F32)
    pb = jnp.dot(hy, wb_ref[...], preferred_element_type=F32)
    mix = ga_ref[...].astype(F32) * pa + gb_ref[...].astype(F32) * pb
    out = jnp.dot(mix.astype(BF16), wo_ref[...], preferred_element_type=F32)
    x1 = x_ref[...] + gate_ref[...] * out
    x1_ref[...] = x1
    y = x1 * lax.rsqrt(jnp.mean(x1 * x1, axis=-1, keepdims=True) + EPS) * g2_ref[...]
    h2_ref[...] = _pack_bf16_pairs(y * (1.0 + sc_ref[...]) + sh_ref[...])


def _merge(hdirs, pm, hy, x, gate1, g2, shift2, scale2, w_a, w_b, w_out, tm=256):
    bsz, L, d = x.shape
    tok = pl.BlockSpec((None, tm, d), lambda b, i: (b, i, 0))

    def pm_tile(col):
        return pl.BlockSpec((None, tm, d), lambda b, i: (b, i, col))

    packed = pl.BlockSpec((None, tm, d // 2), lambda b, i: (b, i, 0))
    vec = pl.BlockSpec((1, d), lambda b, i: (0, 0))
    bvec = pl.BlockSpec((None, 1, d), lambda b, i: (b, 0, 0))
    wsp = pl.BlockSpec((d, d), lambda b, i: (0, 0))
    return pl.pallas_call(
        _merge_kernel,
        grid=(bsz, L // tm),
        in_specs=[pl.BlockSpec((None, None, tm, d), lambda b, i: (0, b, i, 0)),
                  pl.BlockSpec((None, None, tm, d), lambda b, i: (1, b, i, 0)),
                  pm_tile(PM_O), packed, pm_tile(PM_GA), pm_tile(PM_GB), tok,
                  bvec, vec, bvec, bvec, wsp, wsp, wsp],
        out_specs=[tok, packed],
        out_shape=[jax.ShapeDtypeStruct((bsz, L, d), F32), jax.ShapeDtypeStruct((bsz, L, d // 2), jnp.uint32)],
        compiler_params=_cparams("parallel", "parallel"),
        name="merge",
    )(hdirs, hdirs, pm, hy, pm, pm, x, gate1, g2.reshape(1, d), shift2, scale2, w_a, w_b, w_out)


MOE_BLOCK = 256
ROUTE_E1, ROUTE_E2, ROUTE_W1, ROUTE_W2 = 0, 1, 2, 3
EXP_LANE0 = N_GROUPS


def _first_lane_of_max(val, valid, lane):
    masked = jnp.where(valid, val, NEG_BIG)
    mx = jnp.max(masked, axis=1, keepdims=True)
    idx = jnp.min(jnp.where(valid & (masked == mx), lane, LANES), axis=1, keepdims=True)
    return mx, idx


def _router_kernel(h_ref, w_ref, b_ref, r_ref):
    logits = jnp.dot(_unpack_bf16_pairs(h_ref[...]), w_ref[...], preferred_element_type=F32) + b_ref[...]
    lane = lax.broadcasted_iota(jnp.int32, logits.shape, 1)
    is_g = lane < N_GROUPS
    gmax, gsel = _first_lane_of_max(logits, is_g, lane)
    gsum = jnp.sum(jnp.where(is_g, jnp.exp(logits - gmax), 0.0), axis=1, keepdims=True)
    gw = 1.0 / gsum
    lo = EXP_LANE0 + gsel * EXPERTS_PER_GROUP
    in_grp = (lane >= lo) & (lane < lo + EXPERTS_PER_GROUP)
    emax, l1 = _first_lane_of_max(logits, in_grp, lane)
    esum = jnp.sum(jnp.where(in_grp, jnp.exp(logits - emax), 0.0), axis=1, keepdims=True)
    e2max, l2 = _first_lane_of_max(logits, in_grp & (lane != l1), lane)
    v1 = 1.0 / esum
    v2 = jnp.exp(e2max - emax) / esum
    vs = v1 + v2
    w1 = gw * v1 / vs
    w2 = gw * v2 / vs
    e1 = (l1 - EXP_LANE0).astype(F32)
    e2 = (l2 - EXP_LANE0).astype(F32)
    r_ref[...] = jnp.where(lane == ROUTE_E1, e1,
                           jnp.where(lane == ROUTE_E2, e2,
                                     jnp.where(lane == ROUTE_W1, w1,
                                               jnp.where(lane == ROUTE_W2, w2, 0.0))))


def _router(h2, w_group, b_group, w_router, b_router, tm=1024):
    n, dp = h2.shape
    d = 2 * dp
    w = jnp.zeros((d, LANES), F32).at[:, :N_GROUPS].set(w_group).at[
        :, EXP_LANE0:EXP_LANE0 + N_EXPERTS].set(w_router).astype(BF16)
    b = jnp.zeros((1, LANES), F32).at[0, :N_GROUPS].set(b_group).at[
        0, EXP_LANE0:EXP_LANE0 + N_EXPERTS].set(b_router)
    return pl.pallas_call(
        _router_kernel,
        grid=(n // tm,),
        in_specs=[pl.BlockSpec((tm, dp), lambda i: (i, 0)),
                  pl.BlockSpec((d, LANES), lambda i: (0, 0)),
                  pl.BlockSpec((1, LANES), lambda i: (0, 0))],
        out_specs=pl.BlockSpec((tm, LANES), lambda i: (i, 0)),
        out_shape=jax.ShapeDtypeStruct((n, LANES), F32),
        compiler_params=_cparams("parallel"),
        name="moe_router",
    )(h2, w, b)


def _slots_kernel(r_ref, dest_ref, cnt_ref, run_sc, start_sc):
    ph = pl.program_id(0)
    i = pl.program_id(1)
    rec = r_ref[...]
    tm = rec.shape[0]
    lane = lax.broadcasted_iota(jnp.int32, rec.shape, 1)
    e1 = rec[:, ROUTE_E1:ROUTE_E1 + 1].astype(jnp.int32)
    e2 = rec[:, ROUTE_E2:ROUTE_E2 + 1].astype(jnp.int32)
    oh1 = lane == e1
    oh2 = lane == e2
    oh = (oh1 | oh2).astype(F32)

    @pl.when((ph == 0) & (i == 0))
    def _():
        run_sc[...] = jnp.zeros_like(run_sc)

    @pl.when(ph == 0)
    def _():
        run_sc[...] += jnp.sum(oh, axis=0, keepdims=True)

    @pl.when((ph == 1) & (i == 0))
    def _():
        counts = run_sc[...]
        cnt_ref[...] = counts
        nblk = jnp.floor((counts + (MOE_BLOCK - 1)) * (1.0 / MOE_BLOCK))
        rr = lax.broadcasted_iota(jnp.int32, (LANES, LANES), 0)
        cc = lax.broadcasted_iota(jnp.int32, (LANES, LANES), 1)
        before = (rr < cc).astype(BF16)
        first = jnp.dot(nblk.astype(BF16), before, preferred_element_type=F32)
        start_sc[...] = first * float(MOE_BLOCK)
        run_sc[...] = jnp.zeros_like(run_sc)

    @pl.when(ph == 1)
    def _():
        r = lax.broadcasted_iota(jnp.int32, (tm, tm), 0)
        c = lax.broadcasted_iota(jnp.int32, (tm, tm), 1)
        earlier = (r > c).astype(BF16)
        rank = jnp.dot(earlier, oh.astype(BF16), preferred_element_type=F32) + run_sc[...] + start_sc[...]
        d1 = jnp.sum(jnp.where(oh1, rank, 0.0), axis=1, keepdims=True)
        d2 = jnp.sum(jnp.where(oh2, rank, 0.0), axis=1, keepdims=True)
        dest_ref[...] = jnp.where(lane == 0, d1, jnp.where(lane == 1, d2, 0.0)).astype(jnp.int32)
        run_sc[...] += jnp.sum(oh, axis=0, keepdims=True)


def _slots(route, tm=512):
    n = route.shape[0]
    return pl.pallas_call(
        _slots_kernel,
        grid=(2, n // tm),
        in_specs=[pl.BlockSpec((tm, LANES), lambda p, i: (i, 0))],
        out_specs=[pl.BlockSpec((tm, LANES), lambda p, i: (i * p, 0)),
                   pl.BlockSpec((1, LANES), lambda p, i: (0, 0))],
        out_shape=[jax.ShapeDtypeStruct((n, LANES), jnp.int32), jax.ShapeDtypeStruct((1, LANES), F32)],
        scratch_shapes=[pltpu.VMEM((1, LANES), F32), pltpu.VMEM((1, LANES), F32)],
        compiler_params=_cparams("arbitrary", "arbitrary"),
        name="moe_slots",
    )(route)


def _experts_kernel(be_ref, first_ref, nxt_ref, par_ref, nu_ref, x_ref, w1_hbm, w3_hbm, w2_hbm, o_ref,
                    w1f, w3f, w2f, w1b, w3b, w2b, sems):
    i = pl.program_id(0)

    def weight_copies(e, slot):
        return (pltpu.make_async_copy(w1_hbm.at[e], w1f.at[slot], sems.at[0, slot]),
                pltpu.make_async_copy(w3_hbm.at[e], w3f.at[slot], sems.at[1, slot]),
                pltpu.make_async_copy(w2_hbm.at[e], w2f.at[slot], sems.at[2, slot]))

    @pl.when(i == 0)
    def _():
        for cp in weight_copies(be_ref[0], 0):
            cp.start()

    @pl.when(first_ref[i] == 1)
    def _():
        slot = par_ref[i]

        @pl.when(nxt_ref[i] >= 0)
        def _():
            for cp in weight_copies(nxt_ref[i], 1 - slot):
                cp.start()

        for cp in weight_copies(be_ref[i], slot):
            cp.wait()
        w1b[...] = w1f[slot].astype(BF16)
        w3b[...] = w3f[slot].astype(BF16)
        w2b[...] = w2f[slot].astype(BF16)

    @pl.when(i < nu_ref[0])
    def _():
        x = _unpack_bf16_pairs(x_ref[...])
        a = jnp.dot(x, w1b[...], preferred_element_type=F32)
        b = jnp.dot(x, w3b[...], preferred_element_type=F32)
        hmid = (a * jax.nn.sigmoid(a)) * b
        o_ref[...] = _pack_bf16_pairs(jnp.dot(hmid.astype(BF16), w2b[...], preferred_element_type=F32))

    @pl.when(i >= nu_ref[0])
    def _():
        o_ref[...] = jnp.zeros_like(o_ref)


def _experts(xs, nb, block_e, n_used, w1_e, w3_e, w2_e):
    dp = xs.shape[1]
    d, de = w1_e.shape[1], w1_e.shape[2]
    idx = jnp.arange(nb, dtype=jnp.int32)
    used = idx < n_used[0]
    first = used & ((idx == 0) | (block_e != jnp.roll(block_e, 1)))
    ordinal = jnp.cumsum(first.astype(jnp.int32)) - 1
    par = (ordinal % 2).astype(jnp.int32)
    first_pos = jnp.where(first, idx, nb)
    next_first = lax.cummin(jnp.concatenate([first_pos[1:], jnp.full((1,), nb, jnp.int32)]), reverse=True)
    nxt = jnp.where(next_first < nb, block_e[jnp.minimum(next_first, nb - 1)], -1).astype(jnp.int32)
    any_spec = pl.BlockSpec(memory_space=pl.ANY)
    grid_spec = pltpu.PrefetchScalarGridSpec(
        num_scalar_prefetch=5,
        grid=(nb,),
        in_specs=[pl.BlockSpec((MOE_BLOCK, dp), lambda i, *_: (i, 0)), any_spec, any_spec, any_spec],
        out_specs=pl.BlockSpec((MOE_BLOCK, dp), lambda i, *_: (i, 0)),
        scratch_shapes=[pltpu.VMEM((2, d, de), F32), pltpu.VMEM((2, d, de), F32), pltpu.VMEM((2, de, d), F32),
                        pltpu.VMEM((d, de), BF16), pltpu.VMEM((d, de), BF16), pltpu.VMEM((de, d), BF16),
                        pltpu.SemaphoreType.DMA((3, 2))],
    )
    return pl.pallas_call(
        _experts_kernel,
        grid_spec=grid_spec,
        out_shape=jax.ShapeDtypeStruct((nb * MOE_BLOCK, dp), xs.dtype),
        compiler_params=_cparams("arbitrary"),
        name="moe_experts",
    )(block_e, first.astype(jnp.int32), nxt, par, n_used, xs, w1_e, w3_e, w2_e)


SC_WINDOW = 128
SC_CORES, SC_SUBCORES = 2, 16
SC_WORKERS = SC_CORES * SC_SUBCORES


def _sc_worker_id():
    return lax.axis_index("c") * SC_SUBCORES + lax.axis_index("s")


def _sc_mesh():
    return plsc.VectorSubcoreMesh(core_axis_name="c", subcore_axis_name="s")


def _sc_dispatch(rows, dest0, dest1, pad_slots, n_rows):
    n, dv = rows.shape
    nwin, pwin = n // SC_WINDOW, pad_slots.shape[0] // SC_WINDOW
    assert n % (SC_WINDOW * SC_WORKERS) == 0 and pad_slots.shape[0] % (SC_WINDOW * SC_WORKERS) == 0
    zeros = jnp.zeros((SC_WINDOW, dv), rows.dtype)

    @pl.kernel(out_type=jax.ShapeDtypeStruct((n_rows, dv), rows.dtype), mesh=_sc_mesh(),
               scratch_types=[pltpu.VMEM((1, SC_WINDOW), jnp.int32), pltpu.VMEM((SC_WINDOW, dv), rows.dtype)],
               name="moe_dispatch_sc")
    def scatter(x_hbm, d0_hbm, d1_hbm, p_hbm, z_hbm, o_hbm, idx, buf):
        wid = _sc_worker_id()
        pltpu.sync_copy(z_hbm, buf)

        @pl.loop(0, pwin // SC_WORKERS)
        def _(t):
            w = t * SC_WORKERS + wid
            pltpu.sync_copy(p_hbm.at[pl.ds(w, 1)], idx)
            pltpu.sync_copy(buf, o_hbm.at[idx.at[0]])

        @pl.loop(0, nwin // SC_WORKERS)
        def _(t):
            w = t * SC_WORKERS + wid
            pltpu.sync_copy(x_hbm.at[pl.ds(w * SC_WINDOW, SC_WINDOW)], buf)
            for d_hbm in (d0_hbm, d1_hbm):
                pltpu.sync_copy(d_hbm.at[pl.ds(w, 1)], idx)
                pltpu.sync_copy(buf, o_hbm.at[idx.at[0]])

    return scatter(rows, dest0.reshape(nwin, SC_WINDOW), dest1.reshape(nwin, SC_WINDOW),
                   pad_slots.reshape(pwin, SC_WINDOW), zeros)


def _sc_gather(table, index):
    m = index.shape[0]
    dv = table.shape[1]
    nwin = m // SC_WINDOW
    assert m % (SC_WINDOW * SC_WORKERS) == 0

    @pl.kernel(out_type=jax.ShapeDtypeStruct((m, dv), table.dtype), mesh=_sc_mesh(),
               scratch_types=[pltpu.VMEM((1, SC_WINDOW), jnp.int32), pltpu.VMEM((SC_WINDOW, dv), table.dtype)],
               name="moe_gather_sc")
    def gather(x_hbm, i_hbm, o_hbm, idx, buf):
        wid = _sc_worker_id()

        @pl.loop(0, nwin // SC_WORKERS)
        def _(t):
            w = t * SC_WORKERS + wid
            pltpu.sync_copy(i_hbm.at[pl.ds(w, 1)], idx)
            pltpu.sync_copy(x_hbm.at[idx.at[0]], buf)
            pltpu.sync_copy(buf, o_hbm.at[pl.ds(w * SC_WINDOW, SC_WINDOW)])

    return gather(table, index.reshape(nwin, SC_WINDOW))


def _combine_planes_kernel(r_ref, ya_ref, yb_ref, x_ref, gate_ref, gf_ref, o_ref):
    rec = r_ref[...]
    y = (_unpack_bf16_pairs(ya_ref[...]).astype(F32) * rec[:, ROUTE_W1:ROUTE_W1 + 1]
         + _unpack_bf16_pairs(yb_ref[...]).astype(F32) * rec[:, ROUTE_W2:ROUTE_W2 + 1])
    x2 = x_ref[...] + gate_ref[...] * y
    o_ref[...] = x2 * lax.rsqrt(jnp.mean(x2 * x2, axis=-1, keepdims=True) + EPS) * gf_ref[...]


def _combine_planes(g, route, x1, gate2, g_final, tm=512):
    bsz, L, d = x1.shape
    tpb = L // tm
    dp = g.shape[-1]
    return pl.pallas_call(
        _combine_planes_kernel,
        grid=(bsz, tpb),
        in_specs=[pl.BlockSpec((tm, LANES), lambda b, i: (b * tpb + i, 0)),
                  pl.BlockSpec((None, tm, dp), lambda b, i: (0, b * tpb + i, 0)),
                  pl.BlockSpec((None, tm, dp), lambda b, i: (1, b * tpb + i, 0)),
                  pl.BlockSpec((None, tm, d), lambda b, i: (b, i, 0)),
                  pl.BlockSpec((None, 1, d), lambda b, i: (b, 0, 0)),
                  pl.BlockSpec((1, d), lambda b, i: (0, 0))],
        out_specs=pl.BlockSpec((None, tm, d), lambda b, i: (b, i, 0)),
        out_shape=jax.ShapeDtypeStruct((bsz, L, d), F32),
        compiler_params=_cparams("parallel", "parallel"),
        name="moe_combine",
    )(route, g, g, x1, gate2, g_final.reshape(1, d))


def _moe(h2, x1, gate2, g_final, w_group, b_group, w_router, b_router, w1_e, w3_e, w2_e):
    bsz, L, d = x1.shape
    n = bsz * L
    h2f = h2.reshape(n, h2.shape[-1])
    route = _router(h2f, w_group, b_group, w_router, b_router)
    dest_rec, counts = _slots(route)
    nb = (2 * n) // MOE_BLOCK + N_EXPERTS
    cnt = counts[0, :N_EXPERTS].astype(jnp.int32)
    blocks_per_e = (cnt + MOE_BLOCK - 1) // MOE_BLOCK
    ends = jnp.cumsum(blocks_per_e)
    block_e = jnp.minimum(jnp.sum(ends[None, :] <= jnp.arange(nb, dtype=jnp.int32)[:, None], axis=1),
                          N_EXPERTS - 1).astype(jnp.int32)
    n_used = ends[-1:].astype(jnp.int32)
    n_slots = nb * MOE_BLOCK
    pad_j = jnp.arange(MOE_BLOCK, dtype=jnp.int32)[None, :]
    spare = n_slots + jnp.arange(N_EXPERTS * MOE_BLOCK, dtype=jnp.int32).reshape(N_EXPERTS, MOE_BLOCK)
    first_slot = ((ends - blocks_per_e) * MOE_BLOCK)[:, None]
    is_pad = cnt[:, None] + pad_j < blocks_per_e[:, None] * MOE_BLOCK
    pad_slots = jnp.where(is_pad, first_slot + cnt[:, None] + pad_j, spare).reshape(-1)
    xs = _sc_dispatch(h2f, dest_rec[:, 0], dest_rec[:, 1], pad_slots, n_slots + N_EXPERTS * MOE_BLOCK)
    ys = _experts(xs, nb, block_e, n_used, w1_e, w3_e, w2_e)
    g = _sc_gather(ys, jnp.concatenate([dest_rec[:, 0], dest_rec[:, 1]]))
    return _combine_planes(g.reshape(2, n, g.shape[-1]), route, x1, gate2, g_final)


def kernel(x, c, ctx, c_ctx, w_mod, b_mod, g_norm1, g_norm2, w_in, b_in, w_qk_conv, b_qk_conv,
           w_h_conv, b_h_conv, hf_w1, hf_b1, hf_w2, hf_b2, hf_w3, hf_freq, h_bias, w_a, w_b, w_out,
           w_group, b_group, w_router, b_router, w1_e, w3_e, w2_e, g_final):
    assert w_mod.shape[0] == 1, "single-layer block"
    (w_mod, b_mod, g_norm1, g_norm2, w_in, b_in, w_qk_conv, b_qk_conv, w_h_conv, b_h_conv, hf_w1, hf_b1, hf_w2,
     hf_b2, hf_w3, hf_freq, h_bias, w_a, w_b, w_out, w_group, b_group, w_router, b_router, w1_e, w3_e, w2_e) = (
        t[0] for t in (w_mod, b_mod, g_norm1, g_norm2, w_in, b_in, w_qk_conv, b_qk_conv, w_h_conv, b_h_conv,
                       hf_w1, hf_b1, hf_w2, hf_b2, hf_w3, hf_freq, h_bias, w_a, w_b, w_out, w_group, b_group,
                       w_router, b_router, w1_e, w3_e, w2_e))
    bsz, L, d = x.shape
    lc = ctx.shape[1]
    seg = L // (L // GRID_W)
    chunk_c = min(lc, MLSTM_CHUNK)
    assert bsz + 1 <= 8 and lc % chunk_c == 0 and L % MLSTM_CHUNK == 0

    cond = jnp.zeros((8, d), F32).at[:bsz].set(c).at[bsz].set(c_ctx)
    mod = _adaln(cond, w_mod, b_mod).reshape(8, 6, d)
    modx = mod[:bsz]
    shift1, scale1, gate1, shift2, scale2, gate2 = (modx[:, i:i + 1] for i in range(6))
    shift1c = jnp.broadcast_to(mod[bsz, 0].reshape(1, 1, d), (bsz, 1, d))
    scale1c = jnp.broadcast_to(mod[bsz, 1].reshape(1, 1, d), (bsz, 1, d))

    w_in16 = w_in.astype(BF16)
    k_scale = jnp.full((M_WIDTH,), M_HEAD_DIM ** -0.5, F32)
    qk_scale = jnp.concatenate([jnp.ones((M_WIDTH,), F32), k_scale])
    w_gates, b_gates = w_in[:, IG0:M_COLS], b_in[IG0:M_COLS]

    hc = _norm_mod(ctx, g_norm1, shift1c, scale1c, lc)
    kc = _proj_conv_silu(hc, w_in16[:, K0:V0], b_in[K0:V0], w_qk_conv[:, M_WIDTH:], b_qk_conv[M_WIDTH:],
                         k_scale, lc, lc)
    vc = _proj_act(hc, w_in16[:, V0:O0], b_in[V0:O0], "none", BF16, lc)
    bcc, acc, arc = _gates(hc, w_gates, b_gates, chunk_c)
    zero_state = (jnp.zeros((bsz, 2, M_HEADS, M_HEAD_DIM, M_HEAD_DIM), F32),
                  jnp.zeros((bsz, 2, M_HEADS, 1, M_HEAD_DIM), F32),
                  jnp.zeros((bsz, 2, M_HEADS, 1, LANES), F32))
    _, ctx_state = _mlstm(None, (kc, 0), (vc, 0), bcc, acc, arc, zero_state, False, chunk_c)

    tm = 1024
    w_main = jnp.concatenate([w_in16[:, Q0:IG0], w_in16[:, GA0:IN_COLS]], axis=1)
    b_main = jnp.concatenate([b_in[Q0:IG0], b_in[GA0:IN_COLS]])
    _, dft_fast = _dft_factors(2 * L)
    pm, h, h_il = _proj_main(x, g_norm1, shift1, scale1, w_main, b_main, w_qk_conv, b_qk_conv, qk_scale,
                             seg, tm, dft_fast)
    bc, ac, ar = _gates(h, w_gates, b_gates, MLSTM_CHUNK)
    hdirs, _ = _mlstm((pm, PM_Q), (pm, PM_K), (pm, PM_V), bc, ac, ar, ctx_state, True, MLSTM_CHUNK)

    x0_t, s_t = _proj_hyena(h_il, w_in16[:, HY0:GA0], b_in[HY0:GA0], w_h_conv, b_h_conv, seg)
    hy = _hyena_long_conv(s_t, x0_t, h_bias, hf_w1, hf_b1, hf_w2, hf_b2, hf_w3, hf_freq)

    x1, h2 = _merge(hdirs, pm, hy, x, gate1, g_norm2, shift2, scale2,
                    w_a.astype(BF16), w_b.astype(BF16), w_out.astype(BF16))
    return _moe(h2, x1, gate2, g_final, w_group, b_group, w_router, b_router, w1_e, w3_e, w2_e)
```

```python
import functools
import math

import jax
import jax.numpy as jnp
import numpy as np
from jax import lax
from jax.experimental import pallas as pl
from jax.experimental.pallas import tpu as pltpu
from jax.experimental.pallas import tpu_sc as plsc

F32 = jnp.float32
BF16 = jnp.bfloat16

D_MODEL = 1024
GRID_W = 64
EPS = 1e-6
M_HEADS = 4
M_HEAD_DIM = 256
M_WIDTH = M_HEADS * M_HEAD_DIM
H_WIDTH = 1024
H_POS_BANDS = 16
H_FILTER_HIDDEN = 64
H_FAST_DECAY_PCT = 0.3
H_SLOW_DECAY_PCT = 1.5
H_DECAY_TARGET = 1e-2
N_GROUPS = 8
EXPERTS_PER_GROUP = 8
N_EXPERTS = N_GROUPS * EXPERTS_PER_GROUP
D_EXPERT = 512
Q0 = 0
K0 = Q0 + M_WIDTH
V0 = K0 + M_WIDTH
O0 = V0 + M_WIDTH
IG0 = O0 + M_WIDTH
FG0 = IG0 + 2 * M_HEADS
M_COLS = FG0 + 2 * M_HEADS
HY0 = M_COLS
GA0 = HY0 + 3 * H_WIDTH
GB0 = GA0 + D_MODEL
IN_COLS = GB0 + D_MODEL

LANES = 128
MLSTM_CHUNK = 512
NEG_BIG = -1e30
VMEM_LIMIT = 48 * 1024 * 1024


def _cparams(*sem):
    return pltpu.CompilerParams(dimension_semantics=sem, vmem_limit_bytes=VMEM_LIMIT)


def _adaln_kernel(c_ref, w_ref, b_ref, o_ref):
    s = c_ref[...]
    s = s * jax.nn.sigmoid(s)
    o_ref[...] = jnp.dot(s.astype(BF16), w_ref[...].astype(BF16), preferred_element_type=F32) + b_ref[...]


def _adaln(cond, w_mod, b_mod):
    n = w_mod.shape[1]
    tn = 1536
    return pl.pallas_call(
        _adaln_kernel,
        grid=(n // tn,),
        in_specs=[pl.BlockSpec((8, D_MODEL), lambda j: (0, 0)),
                  pl.BlockSpec((D_MODEL, tn), lambda j: (0, j)),
                  pl.BlockSpec((1, tn), lambda j: (0, j))],
        out_specs=pl.BlockSpec((8, tn), lambda j: (0, j)),
        out_shape=jax.ShapeDtypeStruct((8, n), F32),
        compiler_params=_cparams("arbitrary"),
        name="adaln",
    )(cond, w_mod, b_mod.reshape(1, n))


def _norm_mod_kernel(x_ref, g_ref, sh_ref, sc_ref, o_ref):
    x = x_ref[...]
    y = x * lax.rsqrt(jnp.mean(x * x, axis=-1, keepdims=True) + EPS)
    y = y * g_ref[...]
    o_ref[...] = (y * (1.0 + sc_ref[...]) + sh_ref[...]).astype(o_ref.dtype)


def _norm_mod(x, g, shift, scale, tm):
    bsz, L, d = x.shape
    return pl.pallas_call(
        _norm_mod_kernel,
        grid=(bsz, L // tm),
        in_specs=[pl.BlockSpec((None, tm, d), lambda b, i: (b, i, 0)),
                  pl.BlockSpec((1, d), lambda b, i: (0, 0)),
                  pl.BlockSpec((None, 1, d), lambda b, i: (b, 0, 0)),
                  pl.BlockSpec((None, 1, d), lambda b, i: (b, 0, 0))],
        out_specs=pl.BlockSpec((None, tm, d), lambda b, i: (b, i, 0)),
        out_shape=jax.ShapeDtypeStruct((bsz, L, d), BF16),
        compiler_params=_cparams("parallel", "parallel"),
        name="norm_mod",
    )(x, g.reshape(1, d), shift, scale)


def _conv3(z, wc, bc, seg):
    tm = z.shape[0]
    pos = lax.broadcasted_iota(jnp.int32, z.shape, 0) & (seg - 1)
    zp = jnp.where(pos == 0, 0.0, pltpu.roll(z, 1, 0))
    zn = jnp.where(pos == seg - 1, 0.0, pltpu.roll(z, tm - 1, 0))
    return zp * wc[0:1, :] + z * wc[1:2, :] + zn * wc[2:3, :] + bc


def _proj_act_kernel(h_ref, w_ref, b_ref, o_ref, *, act):
    z = jnp.dot(h_ref[...], w_ref[...], preferred_element_type=F32) + b_ref[...]
    if act == "sigmoid":
        z = jax.nn.sigmoid(z)
    o_ref[...] = z.astype(o_ref.dtype)


def _proj_act(h, w, b, act, out_dtype, tm, tn=512):
    bsz, L, d = h.shape
    n = w.shape[1]
    return pl.pallas_call(
        functools.partial(_proj_act_kernel, act=act),
        grid=(bsz, L // tm, n // tn),
        in_specs=[pl.BlockSpec((None, tm, d), lambda b_, i, j: (b_, i, 0)),
                  pl.BlockSpec((d, tn), lambda b_, i, j: (0, j)),
                  pl.BlockSpec((1, tn), lambda b_, i, j: (0, j))],
        out_specs=pl.BlockSpec((None, tm, tn), lambda b_, i, j: (b_, i, j)),
        out_shape=jax.ShapeDtypeStruct((bsz, L, n), out_dtype),
        compiler_params=_cparams("parallel", "parallel", "arbitrary"),
        name="proj_" + act,
    )(h, w, b.reshape(1, n))


def _proj_conv_silu_kernel(h_ref, w_ref, b_ref, wc_ref, bc_ref, cs_ref, o_ref, *, seg):
    z = jnp.dot(h_ref[...], w_ref[...], preferred_element_type=F32) + b_ref[...]
    y = _conv3(z, wc_ref[...], bc_ref[...], seg)
    y = y * jax.nn.sigmoid(y)
    o_ref[...] = (y * cs_ref[...]).astype(o_ref.dtype)


def _proj_conv_silu(h, w, b, wc, bc, colscale, seg, tm, tn=512):
    bsz, L, d = h.shape
    n = w.shape[1]
    col = lambda b_, i, j: (0, j)
    return pl.pallas_call(
        functools.partial(_proj_conv_silu_kernel, seg=seg),
        grid=(bsz, L // tm, n // tn),
        in_specs=[pl.BlockSpec((None, tm, d), lambda b_, i, j: (b_, i, 0)),
                  pl.BlockSpec((d, tn), col),
                  pl.BlockSpec((1, tn), col),
                  pl.BlockSpec((3, tn), col),
                  pl.BlockSpec((1, tn), col),
                  pl.BlockSpec((1, tn), col)],
        out_specs=pl.BlockSpec((None, tm, tn), lambda b_, i, j: (b_, i, j)),
        out_shape=jax.ShapeDtypeStruct((bsz, L, n), BF16),
        compiler_params=_cparams("parallel", "parallel", "arbitrary"),
        name="proj_conv_silu",
    )(h, w, b.reshape(1, n), wc, bc.reshape(1, n), colscale.reshape(1, n))


PROJ_TN = 1024
PROJ_SUB = 512
PM_Q, PM_K, PM_V, PM_O, PM_GA, PM_GB = range(6)


def _proj_main_kernel(x_ref, g_ref, sh_ref, sc_ref, w_ref, b_ref, wc_ref, bc_ref, cs_ref,
                      o_ref, h_ref, hi_hbm, hp_sc, sem, *, seg):
    b, i, j = pl.program_id(0), pl.program_id(1), pl.program_id(2)
    n2, jt = hi_hbm.shape[2], hi_hbm.shape[3]

    def interleave_copy(jj):
        return pltpu.make_async_copy(hp_sc.at[pl.ds(jj * n2, n2)], hi_hbm.at[b, i, :, jj, :], sem)

    @pl.when(j == 0)
    def _():
        x = x_ref[...]
        y = x * lax.rsqrt(jnp.mean(x * x, axis=-1, keepdims=True) + EPS) * g_ref[...]
        y = y * (1.0 + sc_ref[...]) + sh_ref[...]
        h_ref[...] = y.astype(h_ref.dtype)
        hp_sc[...] = _pack_bf16_pairs(y)
        for jj in range(jt):
            interleave_copy(jj).start()

    @pl.when(j == pl.num_programs(2) - 1)
    def _():
        for jj in range(jt):
            interleave_copy(jj).wait()

    def run(epilogue):
        for c in range(PROJ_TN // PROJ_SUB):
            sl = slice(c * PROJ_SUB, (c + 1) * PROJ_SUB)
            z = jnp.dot(h_ref[...], w_ref[:, sl], preferred_element_type=F32) + b_ref[:, sl]
            o_ref[:, sl] = epilogue(z, sl).astype(o_ref.dtype)

    def conv_silu(z, sl):
        y = _conv3(z, wc_ref[:, sl], bc_ref[:, sl], seg)
        return (y * jax.nn.sigmoid(y)) * cs_ref[:, sl]

    @pl.when(j <= PM_K)
    def _():
        run(conv_silu)

    @pl.when(j == PM_V)
    def _():
        run(lambda z, sl: z)

    @pl.when(j >= PM_O)
    def _():
        run(lambda z, sl: jax.nn.sigmoid(z))


def _proj_main(x, g, shift, scale, w, b, wc, bc, colscale, seg, tm, n2):
    bsz, L, d = x.shape
    n = w.shape[1]
    jt = tm // n2
    qk = lambda b_, i, j: (0, jnp.minimum(j, PM_K))
    row = pl.BlockSpec((None, tm, d), lambda b_, i, j: (b_, i, 0))
    bvec = pl.BlockSpec((None, 1, d), lambda b_, i, j: (b_, 0, 0))
    return pl.pallas_call(
        functools.partial(_proj_main_kernel, seg=seg),
        grid=(bsz, L // tm, n // PROJ_TN),
        in_specs=[row, pl.BlockSpec((1, d), lambda b_, i, j: (0, 0)), bvec, bvec,
                  pl.BlockSpec((d, PROJ_TN), lambda b_, i, j: (0, j)),
                  pl.BlockSpec((1, PROJ_TN), lambda b_, i, j: (0, j)),
                  pl.BlockSpec((3, PROJ_TN), qk),
                  pl.BlockSpec((1, PROJ_TN), qk),
                  pl.BlockSpec((1, PROJ_TN), qk)],
        out_specs=[pl.BlockSpec((None, tm, PROJ_TN), lambda b_, i, j: (b_, i, j)), row,
                   pl.BlockSpec(memory_space=pl.ANY)],
        out_shape=[jax.ShapeDtypeStruct((bsz, L, n), BF16), jax.ShapeDtypeStruct((bsz, L, d), BF16),
                   jax.ShapeDtypeStruct((bsz, L // tm, n2, jt, d // 2), jnp.uint32)],
        scratch_shapes=[pltpu.VMEM((tm, d // 2), jnp.uint32), pltpu.SemaphoreType.DMA(())],
        compiler_params=_cparams("parallel", "parallel", "arbitrary"),
        name="proj_main",
    )(x, g.reshape(1, d), shift, scale, w, b.reshape(1, n), wc, bc.reshape(1, -1), colscale.reshape(1, -1))


def _conv3_interleaved(z, wc, bc, seg, jt):
    grp = seg * jt
    pad = jnp.zeros((jt, z.shape[1]), z.dtype)
    prev, nxt = [], []
    for g0 in range(0, z.shape[0], grp):
        zg = z[g0:g0 + grp]
        prev += [pad, zg[:grp - jt]]
        nxt += [zg[jt:], pad]
    zp = jnp.concatenate(prev, axis=0)
    zn = jnp.concatenate(nxt, axis=0)
    return zp * wc[0:1, :] + z * wc[1:2, :] + zn * wc[2:3, :] + bc


def _proj_hyena_kernel(h_ref, w0_ref, w1_ref, w2_ref, b_ref, wc_ref, bc_ref, x0_ref, s_ref, *, seg):
    n2, jt = s_ref.shape[0], s_ref.shape[1]
    h = _unpack_bf16_pairs(h_ref[...].reshape(n2 * jt, h_ref.shape[2]))
    us = []
    for g, w_ref in enumerate((w0_ref, w1_ref, w2_ref)):
        z = jnp.dot(h, w_ref[...], preferred_element_type=F32) + b_ref[g]
        us.append(_conv3_interleaved(z, wc_ref[g], bc_ref[g], seg, jt))
    x0_ref[...] = _pack_bf16_pairs(us[0]).reshape(x0_ref.shape)
    s_ref[...] = (us[1] * us[2]).reshape(s_ref.shape)


def _proj_hyena(hi, w, b, wc, bc, seg):
    bsz, nt, n2, jt, dp = hi.shape
    d, tm = 2 * dp, n2 * jt
    L = nt * tm
    tn = DFT_C_TILE
    nblk = H_WIDTH // tn
    assert n2 % seg == 0 and (jt % 8 == 0 or nt == 1)
    b3 = b.reshape(3, 1, H_WIDTH)
    wc3 = wc.reshape(3, 3, H_WIDTH).transpose(1, 0, 2)
    bc3 = bc.reshape(3, 1, H_WIDTH)
    return pl.pallas_call(
        functools.partial(_proj_hyena_kernel, seg=seg),
        grid=(bsz, nt, nblk),
        in_specs=[pl.BlockSpec((None, None, n2, jt, dp), lambda b_, i, j: (b_, i, 0, 0, 0)),
                  pl.BlockSpec((d, tn), lambda b_, i, j: (0, j)),
                  pl.BlockSpec((d, tn), lambda b_, i, j: (0, nblk + j)),
                  pl.BlockSpec((d, tn), lambda b_, i, j: (0, 2 * nblk + j)),
                  pl.BlockSpec((3, 1, tn), lambda b_, i, j: (0, 0, j)),
                  pl.BlockSpec((3, 3, tn), lambda b_, i, j: (0, 0, j)),
                  pl.BlockSpec((3, 1, tn), lambda b_, i, j: (0, 0, j))],
        out_specs=[pl.BlockSpec((None, n2, jt, tn // 2), lambda b_, i, j: (b_, 0, i, j)),
                   pl.BlockSpec((None, n2, jt, tn), lambda b_, i, j: (b_, 0, i, j))],
        out_shape=[jax.ShapeDtypeStruct((bsz, n2, L // n2, H_WIDTH // 2), jnp.uint32),
                   jax.ShapeDtypeStruct((bsz, n2, L // n2, H_WIDTH), F32)],
        compiler_params=_cparams("parallel", "parallel", "arbitrary"),
        name="proj_hyena",
    )(hi, w, w, w, b3, wc3, bc3)


N_GATES = 4 * M_HEADS


def _split3(x):
    hi = x.astype(BF16)
    r1 = x - hi.astype(F32)
    mid = r1.astype(BF16)
    lo = (r1 - mid.astype(F32)).astype(BF16)
    return hi, mid, lo


def _log_sigmoid(x):
    return jnp.minimum(x, 0.0) - jnp.log1p(jnp.exp(-jnp.abs(x)))


def _gates_kernel(h_ref, w_ref, wt_ref, b_ref, bt_ref, bc_ref, ac_ref, ar_ref):
    h = h_ref[...]
    t = h.shape[0]
    z = jnp.dot(h, w_ref[...], preferred_element_type=F32) + b_ref[...]
    zt = lax.dot_general(wt_ref[...], h, (((1,), (1,)), ((), ())),
                         preferred_element_type=F32) + bt_ref[...]
    r = lax.broadcasted_iota(jnp.int32, (t, t), 0)
    c = lax.broadcasted_iota(jnp.int32, (t, t), 1)
    lower = (r >= c).astype(BF16)
    upper = (r <= c).astype(BF16)
    g8 = FG_LANE0

    lf = _log_sigmoid(z)
    lane = lax.broadcasted_iota(jnp.int32, z.shape, 1)
    is_fg = (lane >= g8) & (lane < 2 * g8)
    terms = [jnp.where(is_fg, p.astype(F32), 0.0) for p in _split3(lf)]
    packed = terms[0] + pltpu.roll(terms[1], 2 * g8, 1) + pltpu.roll(terms[2], 4 * g8, 1)
    cfp = jnp.dot(lower, packed.astype(BF16), preferred_element_type=F32)
    cf = cfp + pltpu.roll(cfp, LANES - 2 * g8, 1) + pltpu.roll(cfp, LANES - 4 * g8, 1)
    cb = cf[t - 1:t, :] - cf + lf
    bc = jnp.where(lane < g8 + M_HEADS, cf, cb)
    bc = pltpu.roll(bc, LANES - g8, 1)
    bc_ref[...] = bc
    ac_ref[...] = z - bc

    lft = _log_sigmoid(zt[g8:, :])
    stacked = jnp.concatenate([p.astype(F32) for p in _split3(lft)] + [jnp.zeros_like(lft)], axis=0)
    cft3 = jnp.dot(stacked.astype(BF16), upper, preferred_element_type=F32)
    cft = cft3[0:g8] + cft3[g8:2 * g8] + cft3[2 * g8:3 * g8]
    cbt = cft[:, t - 1:t] - cft + lft
    row = lax.broadcasted_iota(jnp.int32, cft.shape, 0)
    ar_ref[...] = zt[:g8, :] - jnp.where(row < M_HEADS, cft, cbt)


FG_LANE0 = 2 * M_HEADS


def _gates(h, w_g, b_g, chunk):
    bsz, L, d = h.shape
    w_pad = jnp.zeros((d, LANES), F32).at[:, :N_GATES].set(w_g).astype(BF16)
    b_pad = jnp.zeros((1, LANES), F32).at[0, :N_GATES].set(b_g)
    wt = w_g.T.astype(BF16)
    bt = b_g.reshape(N_GATES, 1)
    tok = pl.BlockSpec((None, chunk, LANES), lambda b_, i: (b_, i, 0))
    return pl.pallas_call(
        _gates_kernel,
        grid=(bsz, L // chunk),
        in_specs=[pl.BlockSpec((None, chunk, d), lambda b_, i: (b_, i, 0)),
                  pl.BlockSpec((d, LANES), lambda b_, i: (0, 0)),
                  pl.BlockSpec((N_GATES, d), lambda b_, i: (0, 0)),
                  pl.BlockSpec((1, LANES), lambda b_, i: (0, 0)),
                  pl.BlockSpec((N_GATES, 1), lambda b_, i: (0, 0))],
        out_specs=[tok, tok, pl.BlockSpec((None, FG_LANE0, chunk), lambda b_, i: (b_, 0, i))],
        out_shape=[jax.ShapeDtypeStruct((bsz, L, LANES), F32),
                   jax.ShapeDtypeStruct((bsz, L, LANES), F32),
                   jax.ShapeDtypeStruct((bsz, FG_LANE0, L), F32)],
        compiler_params=_cparams("parallel", "parallel"),
        name="mlstm_gates",
    )(h, w_pad, wt, b_pad, bt)


def _mlstm_kernel(*refs, emit_h, n_chunks):
    if emit_h:
        (q_ref, k_ref, v_ref, bc_ref, ac_ref, ar_ref, c0_ref, n0_ref, m0_ref,
         h_ref, cf_ref, nf_ref, mf_ref, c_sc, n_sc, m_sc) = refs
    else:
        (k_ref, v_ref, bc_ref, ac_ref, ar_ref, c0_ref, n0_ref, m0_ref,
         cf_ref, nf_ref, mf_ref, c_sc, n_sc, m_sc) = refs
    d = pl.program_id(1)
    j = pl.program_id(2)
    fwd = d == 0
    t = k_ref.shape[0]
    dh = M_HEAD_DIM

    @pl.when(j == 0)
    def _():
        c_sc[...] = c0_ref[...]
        n_sc[...] = n0_ref[...]
        m_sc[...] = m0_ref[...]

    r = lax.broadcasted_iota(jnp.int32, (t, t), 0)
    c = lax.broadcasted_iota(jnp.int32, (t, t), 1)
    causal = jnp.where(fwd, r - c, c - r) >= 0
    bc_all = bc_ref[...]
    ac_all = ac_ref[...]
    ar_all = ar_ref[...]
    for hd in range(M_HEADS):
        sl = slice(hd * dh, (hd + 1) * dh)
        bc = jnp.where(fwd, bc_all[:, hd:hd + 1], bc_all[:, M_HEADS + hd:M_HEADS + hd + 1])
        ac = jnp.where(fwd, ac_all[:, hd:hd + 1], ac_all[:, M_HEADS + hd:M_HEADS + hd + 1])
        ar = jnp.where(fwd, ar_all[hd:hd + 1, :], ar_all[M_HEADS + hd:M_HEADS + hd + 1, :])
        b_tot = jnp.where(fwd, bc[t - 1:t, :], bc[0:1, :])
        m_prev = m_sc[hd][:, 0:1]
        k_h = k_ref[:, sl]
        v_h = v_ref[:, sl]
        if emit_h:
            q_h = q_ref[:, sl]
            dm = jnp.where(causal, bc + ar, NEG_BIG)
            inter = bc + m_prev
            m_t = jnp.maximum(inter, jnp.max(dm, axis=1, keepdims=True))
            qk = lax.dot_general(q_h, k_h, (((1,), (1,)), ((), ())), preferred_element_type=F32)
            s = qk * jnp.exp(dm - m_t)
            carry = jnp.exp(inter - m_t)
            num = (jnp.dot(s.astype(BF16), v_h, preferred_element_type=F32)
                   + carry * jnp.dot(q_h, c_sc[hd].astype(BF16), preferred_element_type=F32))
            den = (jnp.sum(s, axis=1, keepdims=True)
                   + carry * jnp.sum(q_h.astype(F32) * n_sc[hd], axis=1, keepdims=True))
            h_ref[:, sl] = (num / jnp.maximum(jnp.abs(den), jnp.exp(-m_t))).astype(h_ref.dtype)
        g = b_tot + ac
        m_new = jnp.maximum(b_tot + m_prev, jnp.max(g, axis=0, keepdims=True))
        wgt = jnp.exp(g - m_new)
        decay = jnp.exp(b_tot + m_prev - m_new)
        kw = k_h.astype(F32) * wgt
        c_sc[hd] = decay * c_sc[hd] + lax.dot_general(kw.astype(BF16), v_h, (((0,), (0,)), ((), ())),
                                                      preferred_element_type=F32)
        n_sc[hd] = decay * n_sc[hd] + jnp.sum(kw, axis=0, keepdims=True)
        m_sc[hd] = jnp.broadcast_to(m_new, (1, LANES))

    @pl.when(j == n_chunks - 1)
    def _():
        cf_ref[...] = c_sc[...]
        nf_ref[...] = n_sc[...]
        mf_ref[...] = m_sc[...]


def _mlstm(q, k, v, bc, ac, ar, state, emit_h, t):
    bsz, L, _ = k[0].shape
    nc = L // t
    seq = lambda b_, d, j: (b_, j + d * (nc - 1 - 2 * j), 0)
    st = lambda b_, d, j: (b_, d, 0, 0, 0)

    def tok(col):
        return pl.BlockSpec((None, t, M_WIDTH), lambda b_, d, j: (b_, j + d * (nc - 1 - 2 * j), col))

    gate_spec = pl.BlockSpec((None, t, LANES), seq)
    ar_spec = pl.BlockSpec((None, FG_LANE0, t), lambda b_, d, j: (b_, 0, j + d * (nc - 1 - 2 * j)))
    c_spec = pl.BlockSpec((None, None, M_HEADS, M_HEAD_DIM, M_HEAD_DIM), st)
    n_spec = pl.BlockSpec((None, None, M_HEADS, 1, M_HEAD_DIM), st)
    m_spec = pl.BlockSpec((None, None, M_HEADS, 1, LANES), st)
    state_shapes = [jax.ShapeDtypeStruct((bsz, 2, M_HEADS, M_HEAD_DIM, M_HEAD_DIM), F32),
                    jax.ShapeDtypeStruct((bsz, 2, M_HEADS, 1, M_HEAD_DIM), F32),
                    jax.ShapeDtypeStruct((bsz, 2, M_HEADS, 1, LANES), F32)]
    in_specs = [tok(k[1]), tok(v[1]), gate_spec, gate_spec, ar_spec, c_spec, n_spec, m_spec]
    args = [k[0], v[0], bc, ac, ar, *state]
    out_specs = [c_spec, n_spec, m_spec]
    out_shape = list(state_shapes)
    if emit_h:
        in_specs = [tok(q[1])] + in_specs
        args = [q[0]] + args
        out_specs = [pl.BlockSpec((None, None, t, M_WIDTH),
                                  lambda b_, d, j: (d, b_, j + d * (nc - 1 - 2 * j), 0))] + out_specs
        out_shape = [jax.ShapeDtypeStruct((2, bsz, L, M_WIDTH), BF16)] + out_shape
    outs = pl.pallas_call(
        functools.partial(_mlstm_kernel, emit_h=emit_h, n_chunks=nc),
        grid=(bsz, 2, nc),
        in_specs=in_specs,
        out_specs=out_specs,
        out_shape=out_shape,
        scratch_shapes=[pltpu.VMEM((M_HEADS, M_HEAD_DIM, M_HEAD_DIM), F32),
                        pltpu.VMEM((M_HEADS, 1, M_HEAD_DIM), F32),
                        pltpu.VMEM((M_HEADS, 1, LANES), F32)],
        compiler_params=_cparams("parallel", "parallel", "arbitrary"),
        name="mlstm" if emit_h else "mlstm_state",
    )(*args)
    if emit_h:
        return outs[0], tuple(outs[1:])
    return None, tuple(outs)


DFT_M_TILE = 8
DFT_C_TILE = 512
FEAT_ROWS = 16


def _filter_outer_kernel(bands_ref, w1t_ref, b1_ref, w2t_ref, b2_ref, w3p_ref, w3f_ref, fr_ref, dl_ref, l_ref,
                         a_ref, ss_ref, *, L, n1, n2):
    i = pl.program_id(0)
    h = n1 // 2
    cols = DFT_M_TILE * h

    def positions(shape, axis, side):
        q = lax.broadcasted_iota(jnp.int32, shape, axis)
        mm, jj = q // h, q % h
        n = n2 * (jj + side * h) + i * DFT_M_TILE + mm
        return n, jnp.where(n < L, n, 2 * L - n).astype(F32)

    taps = []
    sumsq = jnp.zeros((1, a_ref.shape[-1]), F32)
    for side, w3_ref in ((0, w3p_ref), (1, w3f_ref)):
        _, p_row = positions((1, cols), 1, side)
        t_row = p_row / float(max(L - 1, 1))
        ang = ((2 * math.pi / L) * p_row) * bands_ref[...]
        row = lax.broadcasted_iota(jnp.int32, (FEAT_ROWS, cols), 0)
        feats = jnp.concatenate([jnp.where(row == 0, t_row, 0.0), jnp.cos(ang), -jnp.sin(ang)], axis=0)
        fr = fr_ref[...]
        hid = jnp.sin(fr * (jnp.dot(w1t_ref[...], feats.astype(BF16), preferred_element_type=F32) + b1_ref[...]))
        hid = jnp.sin(fr * (jnp.dot(w2t_ref[...], hid.astype(BF16), preferred_element_type=F32) + b2_ref[...]))
        filt = lax.dot_general(hid.astype(BF16), w3_ref[...], (((0,), (0,)), ((), ())),
                               preferred_element_type=F32)
        n_col, p_col = positions((cols, 1), 0, side)
        t_col = p_col / float(max(L - 1, 1))
        kern = filt * jnp.exp(-t_col * jnp.abs(dl_ref[...]))
        kern = jnp.where(n_col == L, 0.0, kern)
        sumsq = sumsq + jnp.sum(kern * kern, axis=0, keepdims=True)
        taps.append(kern)

    for mm in range(DFT_M_TILE):
        x = jnp.concatenate([taps[0][mm * h:(mm + 1) * h], taps[1][mm * h:(mm + 1) * h]], axis=0)
        out = jnp.dot(l_ref[...], x.astype(BF16), preferred_element_type=F32)
        a_ref[0, :, mm, :] = out[:n1]
        a_ref[1, :, mm, :] = out[n1:]

    @pl.when(i == 0)
    def _():
        ss_ref[...] = jnp.zeros_like(ss_ref)

    ss_ref[...] += sumsq


def _filter_outer(L, n1, n2, fwd_r, w1, b1, w2, b2, w3, freq):
    hid = H_FILTER_HIDDEN
    bands = jnp.linspace(1e-4, H_POS_BANDS - 1, H_POS_BANDS, dtype=F32).reshape(H_POS_BANDS, 1)
    w1t = jnp.zeros((hid, 3 * FEAT_ROWS), F32)
    w1t = w1t.at[:, 0].set(w1[0]).at[:, FEAT_ROWS:2 * FEAT_ROWS].set(w1[1:1 + H_POS_BANDS].T)
    w1t = w1t.at[:, 2 * FEAT_ROWS:].set(w1[1 + H_POS_BANDS:].T).astype(BF16)
    w3h = w3.astype(BF16)
    max_decay = math.log(H_DECAY_TARGET) / H_FAST_DECAY_PCT
    min_decay = math.log(H_DECAY_TARGET) / H_SLOW_DECAY_PCT
    deltas = jnp.linspace(min_decay, max_decay, H_WIDTH, dtype=F32).reshape(1, H_WIDTH)
    col = lambda v: v.reshape(hid, 1)
    full = lambda a: pl.BlockSpec(a.shape, lambda i: (0,) * a.ndim)
    args = [bands, w1t, col(b1), w2.T.astype(BF16), col(b2)]
    return pl.pallas_call(
        functools.partial(_filter_outer_kernel, L=L, n1=n1, n2=n2),
        grid=(n2 // DFT_M_TILE,),
        in_specs=[full(a) for a in args]
        + [pl.BlockSpec((hid, H_WIDTH), lambda i: (0, 0)), pl.BlockSpec((hid, H_WIDTH), lambda i: (0, 1)),
           full(col(freq)), full(deltas), full(fwd_r)],
        out_specs=[pl.BlockSpec((2, n1, DFT_M_TILE, H_WIDTH), lambda i: (0, 0, i, 0)),
                   pl.BlockSpec((1, H_WIDTH), lambda i: (0, 0))],
        out_shape=[jax.ShapeDtypeStruct((2, n1, n2, H_WIDTH), F32),
                   jax.ShapeDtypeStruct((1, H_WIDTH), F32)],
        compiler_params=_cparams("arbitrary"),
        name="hyena_filter_outer",
    )(*args, w3h, w3h, col(freq), deltas, fwd_r)


def _dft_factors(n):
    lg = int(round(math.log2(n)))
    n1 = 1 << ((lg + 1) // 2)
    return n1, n // n1


def _dft_outer_matrices(n1):
    k = np.arange(n1)[:, None]
    n = np.arange(n1)[None, :]
    ang = 2.0 * np.pi * ((k * n) % n1) / n1
    cr, ci = np.cos(ang), -np.sin(ang)
    h = n1 // 2
    fwd_c = np.block([[cr[:, :h], -ci[:, :h]], [ci[:, :h], cr[:, :h]]])
    fwd_r = np.concatenate([cr, ci], axis=0)
    ir, ii = cr[:h, :], -ci[:h, :]
    inv = np.block([[ir, -ii], [ii, ir]])
    return (jnp.asarray(fwd_c, F32).astype(BF16), jnp.asarray(fwd_r, F32).astype(BF16),
            jnp.asarray(inv, F32).astype(BF16))


def _dft_inner_matrices(n1, n2):
    n = n1 * n2
    k2 = np.arange(n2)[:, None]
    m = np.arange(n2)[None, :]
    ang = 2.0 * np.pi * ((k2 * m) % n2) / n2
    fr, fi = np.cos(ang), -np.sin(ang)
    f = np.block([[fr, -fi], [fi, fr]])
    k1 = jnp.arange(n1, dtype=jnp.int32)[:, None]
    tw_ang = ((jnp.arange(n2, dtype=jnp.int32)[None, :] * k1) % n).astype(F32) * (2.0 * math.pi / n)
    rep = lambda t: jnp.broadcast_to(t[:, :, None], (n1, n2, LANES))
    return (jnp.asarray(f, F32).astype(BF16), jnp.asarray(f.T, F32).astype(BF16),
            rep(jnp.cos(tw_ang)), rep(-jnp.sin(tw_ang)))


def _outer_fwd_kernel(l_ref, s_ref, a_ref):
    n1 = a_ref.shape[1]
    for mm in range(s_ref.shape[1]):
        x = jnp.concatenate([s_ref[0, mm], s_ref[1, mm]], axis=0).astype(BF16)
        out = jnp.dot(l_ref[...], x, preferred_element_type=F32)
        a_ref[0, :, mm, :] = out[:n1]
        a_ref[1, :, mm, :] = out[n1:]


def _outer_fwd(lmat, s_t):
    _, n2, n1h, c = s_t.shape
    n1 = 2 * n1h
    tc = min(DFT_C_TILE, c)
    return pl.pallas_call(
        _outer_fwd_kernel,
        grid=(n2 // DFT_M_TILE, c // tc),
        in_specs=[pl.BlockSpec(lmat.shape, lambda m, j: (0, 0)),
                  pl.BlockSpec((2, DFT_M_TILE, n1h, tc), lambda m, j: (0, m, 0, j))],
        out_specs=pl.BlockSpec((2, n1, DFT_M_TILE, tc), lambda m, j: (0, 0, m, j)),
        out_shape=jax.ShapeDtypeStruct((2, n1, n2, c), F32),
        compiler_params=_cparams("parallel", "parallel"),
        name="dft_outer_fwd",
    )(lmat, s_t)


def _outer_inv_kernel(l_ref, b_ref, s_ref, x0_ref, ysc_ref, hb_ref, o_ref):
    n1h = s_ref.shape[2]
    for mm in range(b_ref.shape[1]):
        y = jnp.concatenate([b_ref[0, mm], b_ref[1, mm]], axis=0).astype(BF16)
        out = jnp.dot(l_ref[...], y, preferred_element_type=F32)
        for b in range(2):
            conv = out[b * n1h:(b + 1) * n1h]
            x0 = _unpack_bf16_pairs(x0_ref[b, mm]).astype(F32)
            hy = x0 * (conv * ysc_ref[...] + hb_ref[...] * s_ref[b, mm])
            o_ref[b, :, mm, :] = _pack_bf16_pairs(hy)


def _outer_inv(lmat, b_t, s_t, x0_t, yscale, h_bias):
    _, n2, n1, c = b_t.shape
    n1h = n1 // 2
    tc = min(DFT_C_TILE, c)
    vec = pl.BlockSpec((1, tc), lambda m, j: (0, j))
    hy = pl.pallas_call(
        _outer_inv_kernel,
        grid=(n2 // DFT_M_TILE, c // tc),
        in_specs=[pl.BlockSpec(lmat.shape, lambda m, j: (0, 0)),
                  pl.BlockSpec((2, DFT_M_TILE, n1, tc), lambda m, j: (0, m, 0, j)),
                  pl.BlockSpec((2, DFT_M_TILE, n1h, tc), lambda m, j: (0, m, 0, j)),
                  pl.BlockSpec((2, DFT_M_TILE, n1h, tc // 2), lambda m, j: (0, m, 0, j)),
                  vec, vec],
        out_specs=pl.BlockSpec((2, n1h, DFT_M_TILE, tc // 2), lambda m, j: (0, 0, m, j)),
        out_shape=jax.ShapeDtypeStruct((2, n1h, n2, c // 2), jnp.uint32),
        compiler_params=_cparams("parallel", "parallel"),
        name="dft_outer_inv",
    )(lmat, b_t, s_t, x0_t, yscale, h_bias.reshape(1, c))
    return hy.reshape(2, n1h * n2, c // 2)


DFT_K_TILE = 8


def _twiddled_inner_dft(f_ref, twr_ref, twi_ref, a_ref, kk):
    n2, c = a_ref.shape[2], a_ref.shape[3]
    twr = jnp.tile(twr_ref[kk], (1, c // LANES))
    twi = jnp.tile(twi_ref[kk], (1, c // LANES))
    ar, ai = a_ref[0, kk], a_ref[1, kk]
    a = jnp.concatenate([(ar * twr - ai * twi).astype(BF16), (ar * twi + ai * twr).astype(BF16)], axis=0)
    x = jnp.dot(f_ref[...], a, preferred_element_type=F32)
    return x[:n2], x[n2:], twr, twi


def _inner_fwd_kernel(f_ref, twr_ref, twi_ref, a_ref, o_ref):
    for kk in range(a_ref.shape[1]):
        xr, xi, _, _ = _twiddled_inner_dft(f_ref, twr_ref, twi_ref, a_ref, kk)
        o_ref[0, kk] = xr.astype(o_ref.dtype)
        o_ref[1, kk] = xi.astype(o_ref.dtype)


def _inner_specs(n1, n2, c):
    tc = min(DFT_C_TILE, c)
    kt = min(DFT_K_TILE, n1)
    blk = pl.BlockSpec((2, kt, n2, tc), lambda k, j: (0, k, 0, j))
    mat = pl.BlockSpec((2 * n2, 2 * n2), lambda k, j: (0, 0))
    tw = pl.BlockSpec((kt, n2, LANES), lambda k, j: (k, 0, 0))
    return blk, mat, tw, (n1 // kt, c // tc), kt, tc


def _inner_fwd(f, twr, twi, a):
    _, n1, n2, c = a.shape
    blk, mat, tw, grid, _, _ = _inner_specs(n1, n2, c)
    return pl.pallas_call(
        _inner_fwd_kernel,
        grid=grid,
        in_specs=[mat, tw, tw, blk],
        out_specs=blk,
        out_shape=jax.ShapeDtypeStruct((2, n1, n2, c), BF16),
        compiler_params=_cparams("parallel", "parallel"),
        name="dft_inner_filter",
    )(f, twr, twi, a)


def _inner_conv_kernel(f_ref, ft_ref, twr_ref, twi_ref, a_ref, k_ref, o_ref):
    n2 = a_ref.shape[2]
    for kk in range(a_ref.shape[1]):
        xr, xi, twr, twi = _twiddled_inner_dft(f_ref, twr_ref, twi_ref, a_ref, kk)
        kr, ki = k_ref[0, kk].astype(F32), k_ref[1, kk].astype(F32)
        yr = xr * kr - xi * ki
        yi = xr * ki + xi * kr
        y = jnp.concatenate([yr.astype(BF16), yi.astype(BF16)], axis=0)
        b = jnp.dot(ft_ref[...], y, preferred_element_type=F32)
        br, bi = b[:n2], b[n2:]
        o_ref[0, :, kk, :] = br * twr + bi * twi
        o_ref[1, :, kk, :] = bi * twr - br * twi


def _inner_conv(f, ft, twr, twi, a, kf):
    _, n1, n2, c = a.shape
    blk, mat, tw, grid, kt, tc = _inner_specs(n1, n2, c)
    return pl.pallas_call(
        _inner_conv_kernel,
        grid=grid,
        in_specs=[mat, mat, tw, tw, blk, blk],
        out_specs=pl.BlockSpec((2, n2, kt, tc), lambda k, j: (0, 0, k, j)),
        out_shape=jax.ShapeDtypeStruct((2, n2, n1, c), F32),
        compiler_params=_cparams("parallel", "parallel"),
        name="dft_inner_conv",
    )(f, ft, twr, twi, a, kf)


def _hyena_long_conv(s_t, x0_t, h_bias, w1, b1, w2, b2, w3, freq):
    bsz, n2, n1h, c = s_t.shape
    assert bsz == 2
    n1 = 2 * n1h
    L = n1h * n2
    fwd_c, fwd_r, inv = _dft_outer_matrices(n1)
    f, ft, twr, twi = _dft_inner_matrices(n1, n2)
    af, sumsq = _filter_outer(L, n1, n2, fwd_r, w1, b1, w2, b2, w3, freq)
    kf = _inner_fwd(f, twr, twi, af)
    a = _outer_fwd(fwd_c, s_t)
    b_t = _inner_conv(f, ft, twr, twi, a, kf)
    yscale = lax.rsqrt(sumsq + EPS) * (1.0 / (2 * L))
    return _outer_inv(inv, b_t, s_t, x0_t, yscale, h_bias)


def _pack_bf16_pairs(x):
    half = x.shape[1] // 2
    lo = pltpu.bitcast(x[:, :half].astype(BF16).astype(F32), jnp.uint32) >> 16
    hi = pltpu.bitcast(x[:, half:].astype(BF16).astype(F32), jnp.uint32) & jnp.uint32(0xFFFF0000)
    return lo | hi


def _unpack_bf16_pairs(p):
    lo = pltpu.bitcast(p << 16, F32).astype(BF16)
    hi = pltpu.bitcast(p & jnp.uint32(0xFFFF0000), F32).astype(BF16)
    return jnp.concatenate([lo, hi], axis=1)


def _merge_kernel(hf_ref, hb_ref, o_ref, hy_ref, ga_ref, gb_ref, x_ref,
                  gate_ref, g2_ref, sh_ref, sc_ref, wa_ref, wb_ref, wo_ref, x1_ref, h2_ref):
    a = o_ref[...].astype(F32) * (hf_ref[...].astype(F32) + hb_ref[...].astype(F32))
    half = DFT_C_TILE // 2
    hy = jnp.concatenate([_unpack_bf16_pairs(hy_ref[:, c * half:(c + 1) * half])
                          for c in range(hy_ref.shape[1] // half)], axis=1)
    pa = jnp.dot(a.astype(BF16), wa_ref[...], preferred_element_type=F32)
    pb = jnp.dot(hy, wb_ref[...], preferred_element_type=F32)
    mix = ga_ref[...].astype(F32) * pa + gb_ref[...].astype(F32) * pb
    out = jnp.dot(mix.astype(BF16), wo_ref[...], preferred_element_type=F32)
    x1 = x_ref[...] + gate_ref[...] * out
    x1_ref[...] = x1
    y = x1 * lax.rsqrt(jnp.mean(x1 * x1, axis=-1, keepdims=True) + EPS) * g2_ref[...]
    h2_ref[...] = _pack_bf16_pairs(y * (1.0 + sc_ref[...]) + sh_ref[...])


def _merge(hdirs, pm, hy, x, gate1, g2, shift2, scale2, w_a, w_b, w_out, tm=256):
    bsz, L, d = x.shape
    tok = pl.BlockSpec((None, tm, d), lambda b, i: (b, i, 0))

    def pm_tile(col):
        return pl.BlockSpec((None, tm, d), lambda b, i: (b, i, col))

    packed = pl.BlockSpec((None, tm, d // 2), lambda b, i: (b, i, 0))
    vec = pl.BlockSpec((1, d), lambda b, i: (0, 0))
    bvec = pl.BlockSpec((None, 1, d), lambda b, i: (b, 0, 0))
    wsp = pl.BlockSpec((d, d), lambda b, i: (0, 0))
    return pl.pallas_call(
        _merge_kernel,
        grid=(bsz, L // tm),
        in_specs=[pl.BlockSpec((None, None, tm, d), lambda b, i: (0, b, i, 0)),
                  pl.BlockSpec((None, None, tm, d), lambda b, i: (1, b, i, 0)),
                  pm_tile(PM_O), packed, pm_tile(PM_GA), pm_tile(PM_GB), tok,
                  bvec, vec, bvec, bvec, wsp, wsp, wsp],
        out_specs=[tok, packed],
        out_shape=[jax.ShapeDtypeStruct((bsz, L, d), F32), jax.ShapeDtypeStruct((bsz, L, d // 2), jnp.uint32)],
        compiler_params=_cparams("parallel", "parallel"),
        name="merge",
    )(hdirs, hdirs, pm, hy, pm, pm, x, gate1, g2.reshape(1, d), shift2, scale2, w_a, w_b, w_out)


MOE_BLOCK = 256
ROUTE_E1, ROUTE_E2, ROUTE_W1, ROUTE_W2 = 0, 1, 2, 3
EXP_LANE0 = N_GROUPS


def _first_lane_of_max(val, valid, lane):
    masked = jnp.where(valid, val, NEG_BIG)
    mx = jnp.max(masked, axis=1, keepdims=True)
    idx = jnp.min(jnp.where(valid & (masked == mx), lane, LANES), axis=1, keepdims=True)
    return mx, idx


MOE_TM = 1024


def _expert_onehots(rec):
    lane = lax.broadcasted_iota(jnp.int32, rec.shape, 1)
    oh1 = lane == rec[:, ROUTE_E1:ROUTE_E1 + 1].astype(jnp.int32)
    oh2 = lane == rec[:, ROUTE_E2:ROUTE_E2 + 1].astype(jnp.int32)
    return oh1, oh2


def _router_kernel(h_ref, w_ref, b_ref, r_ref, cnt_ref):
    logits = jnp.dot(_unpack_bf16_pairs(h_ref[...]), w_ref[...], preferred_element_type=F32) + b_ref[...]
    lane = lax.broadcasted_iota(jnp.int32, logits.shape, 1)
    is_g = lane < N_GROUPS
    gmax, gsel = _first_lane_of_max(logits, is_g, lane)
    gsum = jnp.sum(jnp.where(is_g, jnp.exp(logits - gmax), 0.0), axis=1, keepdims=True)
    gw = 1.0 / gsum
    lo = EXP_LANE0 + gsel * EXPERTS_PER_GROUP
    in_grp = (lane >= lo) & (lane < lo + EXPERTS_PER_GROUP)
    emax, l1 = _first_lane_of_max(logits, in_grp, lane)
    esum = jnp.sum(jnp.where(in_grp, jnp.exp(logits - emax), 0.0), axis=1, keepdims=True)
    e2max, l2 = _first_lane_of_max(logits, in_grp & (lane != l1), lane)
    v1 = 1.0 / esum
    v2 = jnp.exp(e2max - emax) / esum
    vs = v1 + v2
    w1 = gw * v1 / vs
    w2 = gw * v2 / vs
    e1 = (l1 - EXP_LANE0).astype(F32)
    e2 = (l2 - EXP_LANE0).astype(F32)
    rec = jnp.where(lane == ROUTE_E1, e1,
                    jnp.where(lane == ROUTE_E2, e2,
                              jnp.where(lane == ROUTE_W1, w1,
                                        jnp.where(lane == ROUTE_W2, w2, 0.0))))
    r_ref[...] = rec
    oh1, oh2 = _expert_onehots(rec)
    counts = jnp.sum((oh1 | oh2).astype(F32), axis=0, keepdims=True)
    cnt_ref[...] = jnp.broadcast_to(counts, cnt_ref.shape)


def _router(h2, w_group, b_group, w_router, b_router):
    n, dp = h2.shape
    d = 2 * dp
    tm = MOE_TM
    w = jnp.zeros((d, LANES), F32).at[:, :N_GROUPS].set(w_group).at[
        :, EXP_LANE0:EXP_LANE0 + N_EXPERTS].set(w_router).astype(BF16)
    b = jnp.zeros((1, LANES), F32).at[0, :N_GROUPS].set(b_group).at[
        0, EXP_LANE0:EXP_LANE0 + N_EXPERTS].set(b_router)
    return pl.pallas_call(
        _router_kernel,
        grid=(n // tm,),
        in_specs=[pl.BlockSpec((tm, dp), lambda i: (i, 0)),
                  pl.BlockSpec((d, LANES), lambda i: (0, 0)),
                  pl.BlockSpec((1, LANES), lambda i: (0, 0))],
        out_specs=[pl.BlockSpec((tm, LANES), lambda i: (i, 0)),
                   pl.BlockSpec((None, 8, LANES), lambda i: (i, 0, 0))],
        out_shape=[jax.ShapeDtypeStruct((n, LANES), F32), jax.ShapeDtypeStruct((n // tm, 8, LANES), F32)],
        compiler_params=_cparams("parallel"),
        name="moe_router",
    )(h2, w, b)


def _slots_kernel(r_ref, base_ref, dest_ref):
    rec = r_ref[...]
    tm = rec.shape[0]
    lane = lax.broadcasted_iota(jnp.int32, rec.shape, 1)
    oh1, oh2 = _expert_onehots(rec)
    r = lax.broadcasted_iota(jnp.int32, (tm, tm), 0)
    c = lax.broadcasted_iota(jnp.int32, (tm, tm), 1)
    earlier = (r > c).astype(BF16)
    rank = jnp.dot(earlier, (oh1 | oh2).astype(BF16), preferred_element_type=F32) + base_ref[0:1, :]
    d1 = jnp.sum(jnp.where(oh1, rank, 0.0), axis=1, keepdims=True)
    d2 = jnp.sum(jnp.where(oh2, rank, 0.0), axis=1, keepdims=True)
    dest_ref[...] = jnp.where(lane == 0, d1, jnp.where(lane == 1, d2, 0.0)).astype(jnp.int32)


def _slots(route, tile_counts):
    n = route.shape[0]
    tm = MOE_TM
    cnt = tile_counts[:, 0, :]
    totals = jnp.sum(cnt, axis=0)
    nblk = jnp.ceil(totals * (1.0 / MOE_BLOCK))
    first_slot = (jnp.cumsum(nblk) - nblk) * float(MOE_BLOCK)
    base = first_slot[None, :] + jnp.cumsum(cnt, axis=0) - cnt
    base = jnp.broadcast_to(base[:, None, :], tile_counts.shape)
    dest = pl.pallas_call(
        _slots_kernel,
        grid=(n // tm,),
        in_specs=[pl.BlockSpec((tm, LANES), lambda i: (i, 0)),
                  pl.BlockSpec((None, 8, LANES), lambda i: (i, 0, 0))],
        out_specs=pl.BlockSpec((tm, LANES), lambda i: (i, 0)),
        out_shape=jax.ShapeDtypeStruct((n, LANES), jnp.int32),
        compiler_params=_cparams("parallel"),
        name="moe_slots",
    )(route, base)
    return dest, totals


EXPERT_STEP_BLOCKS = 2


def _experts_kernel(be_ref, first_ref, nxt_ref, par_ref, nu_ref, x_ref, w1_hbm, w3_hbm, w2_hbm, o_ref,
                    w1f, w3f, w2f, w1b, w3b, w2b, sems):
    step = pl.program_id(0)

    def weight_copies(e, slot):
        return (pltpu.make_async_copy(w1_hbm.at[e], w1f.at[slot], sems.at[0, slot]),
                pltpu.make_async_copy(w3_hbm.at[e], w3f.at[slot], sems.at[1, slot]),
                pltpu.make_async_copy(w2_hbm.at[e], w2f.at[slot], sems.at[2, slot]))

    @pl.when(step == 0)
    def _():
        for cp in weight_copies(be_ref[0], 0):
            cp.start()

    for sub in range(EXPERT_STEP_BLOCKS):
        i = step * EXPERT_STEP_BLOCKS + sub
        rows = pl.ds(sub * MOE_BLOCK, MOE_BLOCK)

        @pl.when(first_ref[i] == 1)
        def _():
            slot = par_ref[i]

            @pl.when(nxt_ref[i] >= 0)
            def _():
                for cp in weight_copies(nxt_ref[i], 1 - slot):
                    cp.start()

            for cp in weight_copies(be_ref[i], slot):
                cp.wait()
            w1b[...] = w1f[slot].astype(BF16)
            w3b[...] = w3f[slot].astype(BF16)
            w2b[...] = w2f[slot].astype(BF16)

        @pl.when(i < nu_ref[0])
        def _():
            x = _unpack_bf16_pairs(x_ref[rows, :])
            a = jnp.dot(x, w1b[...], preferred_element_type=F32)
            b = jnp.dot(x, w3b[...], preferred_element_type=F32)
            hmid = (a * jax.nn.sigmoid(a)) * b
            o_ref[rows, :] = _pack_bf16_pairs(jnp.dot(hmid.astype(BF16), w2b[...], preferred_element_type=F32))

        @pl.when(i >= nu_ref[0])
        def _():
            o_ref[rows, :] = jnp.zeros((MOE_BLOCK, o_ref.shape[1]), o_ref.dtype)


def _experts(xs, nb, block_e, n_used, w1_e, w3_e, w2_e):
    dp = xs.shape[1]
    d, de = w1_e.shape[1], w1_e.shape[2]
    idx = jnp.arange(nb, dtype=jnp.int32)
    used = idx < n_used[0]
    first = used & ((idx == 0) | (block_e != jnp.roll(block_e, 1)))
    ordinal = jnp.cumsum(first.astype(jnp.int32)) - 1
    par = (ordinal % 2).astype(jnp.int32)
    first_pos = jnp.where(first, idx, nb)
    next_first = lax.cummin(jnp.concatenate([first_pos[1:], jnp.full((1,), nb, jnp.int32)]), reverse=True)
    nxt = jnp.where(next_first < nb, block_e[jnp.minimum(next_first, nb - 1)], -1).astype(jnp.int32)
    any_spec = pl.BlockSpec(memory_space=pl.ANY)
    assert nb % EXPERT_STEP_BLOCKS == 0
    step_rows = EXPERT_STEP_BLOCKS * MOE_BLOCK
    grid_spec = pltpu.PrefetchScalarGridSpec(
        num_scalar_prefetch=5,
        grid=(nb // EXPERT_STEP_BLOCKS,),
        in_specs=[pl.BlockSpec((step_rows, dp), lambda i, *_: (i, 0)), any_spec, any_spec, any_spec],
        out_specs=pl.BlockSpec((step_rows, dp), lambda i, *_: (i, 0)),
        scratch_shapes=[pltpu.VMEM((2, d, de), F32), pltpu.VMEM((2, d, de), F32), pltpu.VMEM((2, de, d), F32),
                        pltpu.VMEM((d, de), BF16), pltpu.VMEM((d, de), BF16), pltpu.VMEM((de, d), BF16),
                        pltpu.SemaphoreType.DMA((3, 2))],
    )
    return pl.pallas_call(
        _experts_kernel,
        grid_spec=grid_spec,
        out_shape=jax.ShapeDtypeStruct((nb * MOE_BLOCK, dp), xs.dtype),
        compiler_params=_cparams("arbitrary"),
        name="moe_experts",
    )(block_e, first.astype(jnp.int32), nxt, par, n_used, xs, w1_e, w3_e, w2_e)


SC_WINDOW = 128
SC_CORES, SC_SUBCORES = 2, 16
SC_WORKERS = SC_CORES * SC_SUBCORES


def _sc_worker_id():
    return lax.axis_index("c") * SC_SUBCORES + lax.axis_index("s")


def _sc_mesh():
    return plsc.VectorSubcoreMesh(core_axis_name="c", subcore_axis_name="s")


def _sc_dispatch(rows, dest0, dest1, pad_slots, n_rows):
    n, dv = rows.shape
    nwin, pwin = n // SC_WINDOW, pad_slots.shape[0] // SC_WINDOW
    assert n % (SC_WINDOW * SC_WORKERS) == 0 and pad_slots.shape[0] % (SC_WINDOW * SC_WORKERS) == 0
    zeros = jnp.zeros((SC_WINDOW, dv), rows.dtype)

    @pl.kernel(out_type=jax.ShapeDtypeStruct((n_rows, dv), rows.dtype), mesh=_sc_mesh(),
               scratch_types=[pltpu.VMEM((1, SC_WINDOW), jnp.int32), pltpu.VMEM((SC_WINDOW, dv), rows.dtype)],
               name="moe_dispatch_sc")
    def scatter(x_hbm, d0_hbm, d1_hbm, p_hbm, z_hbm, o_hbm, idx, buf):
        wid = _sc_worker_id()
        pltpu.sync_copy(z_hbm, buf)

        @pl.loop(0, pwin // SC_WORKERS)
        def _(t):
            w = t * SC_WORKERS + wid
            pltpu.sync_copy(p_hbm.at[pl.ds(w, 1)], idx)
            pltpu.sync_copy(buf, o_hbm.at[idx.at[0]])

        @pl.loop(0, nwin // SC_WORKERS)
        def _(t):
            w = t * SC_WORKERS + wid
            pltpu.sync_copy(x_hbm.at[pl.ds(w * SC_WINDOW, SC_WINDOW)], buf)
            for d_hbm in (d0_hbm, d1_hbm):
                pltpu.sync_copy(d_hbm.at[pl.ds(w, 1)], idx)
                pltpu.sync_copy(buf, o_hbm.at[idx.at[0]])

    return scatter(rows, dest0.reshape(nwin, SC_WINDOW), dest1.reshape(nwin, SC_WINDOW),
                   pad_slots.reshape(pwin, SC_WINDOW), zeros)


def _sc_gather(table, index):
    m = index.shape[0]
    dv = table.shape[1]
    nwin = m // SC_WINDOW
    assert m % (SC_WINDOW * SC_WORKERS) == 0

    @pl.kernel(out_type=jax.ShapeDtypeStruct((m, dv), table.dtype), mesh=_sc_mesh(),
               scratch_types=[pltpu.VMEM((1, SC_WINDOW), jnp.int32), pltpu.VMEM((SC_WINDOW, dv), table.dtype)],
               name="moe_gather_sc")
    def gather(x_hbm, i_hbm, o_hbm, idx, buf):
        wid = _sc_worker_id()

        @pl.loop(0, nwin // SC_WORKERS)
        def _(t):
            w = t * SC_WORKERS + wid
            pltpu.sync_copy(i_hbm.at[pl.ds(w, 1)], idx)
            pltpu.sync_copy(x_hbm.at[idx.at[0]], buf)
            pltpu.sync_copy(buf, o_hbm.at[pl.ds(w * SC_WINDOW, SC_WINDOW)])

    return gather(table, index.reshape(nwin, SC_WINDOW))


def _combine_planes_kernel(r_ref, ya_ref, yb_ref, x_ref, gate_ref, gf_ref, o_ref):
    rec = r_ref[...]
    y = (_unpack_bf16_pairs(ya_ref[...]).astype(F32) * rec[:, ROUTE_W1:ROUTE_W1 + 1]
         + _unpack_bf16_pairs(yb_ref[...]).astype(F32) * rec[:, ROUTE_W2:ROUTE_W2 + 1])
    x2 = x_ref[...] + gate_ref[...] * y
    o_ref[...] = x2 * lax.rsqrt(jnp.mean(x2 * x2, axis=-1, keepdims=True) + EPS) * gf_ref[...]


def _combine_planes(g, route, x1, gate2, g_final, tm=512):
    bsz, L, d = x1.shape
    tpb = L // tm
    dp = g.shape[-1]
    return pl.pallas_call(
        _combine_planes_kernel,
        grid=(bsz, tpb),
        in_specs=[pl.BlockSpec((tm, LANES), lambda b, i: (b * tpb + i, 0)),
                  pl.BlockSpec((None, tm, dp), lambda b, i: (0, b * tpb + i, 0)),
                  pl.BlockSpec((None, tm, dp), lambda b, i: (1, b * tpb + i, 0)),
                  pl.BlockSpec((None, tm, d), lambda b, i: (b, i, 0)),
                  pl.BlockSpec((None, 1, d), lambda b, i: (b, 0, 0)),
                  pl.BlockSpec((1, d), lambda b, i: (0, 0))],
        out_specs=pl.BlockSpec((None, tm, d), lambda b, i: (b, i, 0)),
        out_shape=jax.ShapeDtypeStruct((bsz, L, d), F32),
        compiler_params=_cparams("parallel", "parallel"),
        name="moe_combine",
    )(route, g, g, x1, gate2, g_final.reshape(1, d))


def _moe(h2, x1, gate2, g_final, w_group, b_group, w_router, b_router, w1_e, w3_e, w2_e):
    bsz, L, d = x1.shape
    n = bsz * L
    h2f = h2.reshape(n, h2.shape[-1])
    route, tile_counts = _router(h2f, w_group, b_group, w_router, b_router)
    dest_rec, counts = _slots(route, tile_counts)
    nb = (2 * n) // MOE_BLOCK + N_EXPERTS
    cnt = counts[:N_EXPERTS].astype(jnp.int32)
    blocks_per_e = (cnt + MOE_BLOCK - 1) // MOE_BLOCK
    ends = jnp.cumsum(blocks_per_e)
    block_e = jnp.minimum(jnp.sum(ends[None, :] <= jnp.arange(nb, dtype=jnp.int32)[:, None], axis=1),
                          N_EXPERTS - 1).astype(jnp.int32)
    n_used = ends[-1:].astype(jnp.int32)
    n_slots = nb * MOE_BLOCK
    pad_j = jnp.arange(MOE_BLOCK, dtype=jnp.int32)[None, :]
    spare = n_slots + jnp.arange(N_EXPERTS * MOE_BLOCK, dtype=jnp.int32).reshape(N_EXPERTS, MOE_BLOCK)
    first_slot = ((ends - blocks_per_e) * MOE_BLOCK)[:, None]
    is_pad = cnt[:, None] + pad_j < blocks_per_e[:, None] * MOE_BLOCK
    pad_slots = jnp.where(is_pad, first_slot + cnt[:, None] + pad_j, spare).reshape(-1)
    xs = _sc_dispatch(h2f, dest_rec[:, 0], dest_rec[:, 1], pad_slots, n_slots + N_EXPERTS * MOE_BLOCK)
    ys = _experts(xs, nb, block_e, n_used, w1_e, w3_e, w2_e)
    g = _sc_gather(ys, jnp.concatenate([dest_rec[:, 0], dest_rec[:, 1]]))
    return _combine_planes(g.reshape(2, n, g.shape[-1]), route, x1, gate2, g_final)


def kernel(x, c, ctx, c_ctx, w_mod, b_mod, g_norm1, g_norm2, w_in, b_in, w_qk_conv, b_qk_conv,
           w_h_conv, b_h_conv, hf_w1, hf_b1, hf_w2, hf_b2, hf_w3, hf_freq, h_bias, w_a, w_b, w_out,
           w_group, b_group, w_router, b_router, w1_e, w3_e, w2_e, g_final):
    assert w_mod.shape[0] == 1, "single-layer block"
    (w_mod, b_mod, g_norm1, g_norm2, w_in, b_in, w_qk_conv, b_qk_conv, w_h_conv, b_h_conv, hf_w1, hf_b1, hf_w2,
     hf_b2, hf_w3, hf_freq, h_bias, w_a, w_b, w_out, w_group, b_group, w_router, b_router, w1_e, w3_e, w2_e) = (
        t[0] for t in (w_mod, b_mod, g_norm1, g_norm2, w_in, b_in, w_qk_conv, b_qk_conv, w_h_conv, b_h_conv,
                       hf_w1, hf_b1, hf_w2, hf_b2, hf_w3, hf_freq, h_bias, w_a, w_b, w_out, w_group, b_group,
                       w_router, b_router, w1_e, w3_e, w2_e))
    bsz, L, d = x.shape
    lc = ctx.shape[1]
    seg = L // (L // GRID_W)
    chunk_c = min(lc, MLSTM_CHUNK)
    assert bsz + 1 <= 8 and lc % chunk_c == 0 and L % MLSTM_CHUNK == 0

    cond = jnp.zeros((8, d), F32).at[:bsz].set(c).at[bsz].set(c_ctx)
    mod = _adaln(cond, w_mod, b_mod).reshape(8, 6, d)
    modx = mod[:bsz]
    shift1, scale1, gate1, shift2, scale2, gate2 = (modx[:, i:i + 1] for i in range(6))
    shift1c = jnp.broadcast_to(mod[bsz, 0].reshape(1, 1, d), (bsz, 1, d))
    scale1c = jnp.broadcast_to(mod[bsz, 1].reshape(1, 1, d), (bsz, 1, d))

    w_in16 = w_in.astype(BF16)
    k_scale = jnp.full((M_WIDTH,), M_HEAD_DIM ** -0.5, F32)
    qk_scale = jnp.concatenate([jnp.ones((M_WIDTH,), F32), k_scale])
    w_gates, b_gates = w_in[:, IG0:M_COLS], b_in[IG0:M_COLS]

    hc = _norm_mod(ctx, g_norm1, shift1c, scale1c, lc)
    kc = _proj_conv_silu(hc, w_in16[:, K0:V0], b_in[K0:V0], w_qk_conv[:, M_WIDTH:], b_qk_conv[M_WIDTH:],
                         k_scale, lc, lc)
    vc = _proj_act(hc, w_in16[:, V0:O0], b_in[V0:O0], "none", BF16, lc)
    bcc, acc, arc = _gates(hc, w_gates, b_gates, chunk_c)
    zero_state = (jnp.zeros((bsz, 2, M_HEADS, M_HEAD_DIM, M_HEAD_DIM), F32),
                  jnp.zeros((bsz, 2, M_HEADS, 1, M_HEAD_DIM), F32),
                  jnp.zeros((bsz, 2, M_HEADS, 1, LANES), F32))
    _, ctx_state = _mlstm(None, (kc, 0), (vc, 0), bcc, acc, arc, zero_state, False, chunk_c)

    tm = 1024
    w_main = jnp.concatenate([w_in16[:, Q0:IG0], w_in16[:, GA0:IN_COLS]], axis=1)
    b_main = jnp.concatenate([b_in[Q0:IG0], b_in[GA0:IN_COLS]])
    _, dft_fast = _dft_factors(2 * L)
    pm, h, h_il = _proj_main(x, g_norm1, shift1, scale1, w_main, b_main, w_qk_conv, b_qk_conv, qk_scale,
                             seg, tm, dft_fast)
    bc, ac, ar = _gates(h, w_gates, b_gates, MLSTM_CHUNK)
    hdirs, _ = _mlstm((pm, PM_Q), (pm, PM_K), (pm, PM_V), bc, ac, ar, ctx_state, True, MLSTM_CHUNK)

    x0_t, s_t = _proj_hyena(h_il, w_in16[:, HY0:GA0], b_in[HY0:GA0], w_h_conv, b_h_conv, seg)
    hy = _hyena_long_conv(s_t, x0_t, h_bias, hf_w1, hf_b1, hf_w2, hf_b2, hf_w3, hf_freq)

    x1, h2 = _merge(hdirs, pm, hy, x, gate1, g_norm2, shift2, scale2,
                    w_a.astype(BF16), w_b.astype(BF16), w_out.astype(BF16))
    return _moe(h2, x1, gate2, g_final, w_group, b_group, w_router, b_router, w1_e, w3_e, w2_e)
```

```python
import functools
import math

import jax
import jax.numpy as jnp
import numpy as np
from jax import lax
from jax.experimental import pallas as pl
from jax.experimental.pallas import tpu as pltpu
from jax.experimental.pallas import tpu_sc as plsc

F32 = jnp.float32
BF16 = jnp.bfloat16

D_MODEL = 1024
GRID_W = 64
EPS = 1e-6
M_HEADS = 4
M_HEAD_DIM = 256
M_WIDTH = M_HEADS * M_HEAD_DIM
H_WIDTH = 1024
H_POS_BANDS = 16
H_FILTER_HIDDEN = 64
H_FAST_DECAY_PCT = 0.3
H_SLOW_DECAY_PCT = 1.5
H_DECAY_TARGET = 1e-2
N_GROUPS = 8
EXPERTS_PER_GROUP = 8
N_EXPERTS = N_GROUPS * EXPERTS_PER_GROUP
D_EXPERT = 512
Q0 = 0
K0 = Q0 + M_WIDTH
V0 = K0 + M_WIDTH
O0 = V0 + M_WIDTH
IG0 = O0 + M_WIDTH
FG0 = IG0 + 2 * M_HEADS
M_COLS = FG0 + 2 * M_HEADS
HY0 = M_COLS
GA0 = HY0 + 3 * H_WIDTH
GB0 = GA0 + D_MODEL
IN_COLS = GB0 + D_MODEL

LANES = 128
MLSTM_CHUNK = 512
NEG_BIG = -1e30
VMEM_LIMIT = 48 * 1024 * 1024


def _cparams(*sem):
    return pltpu.CompilerParams(dimension_semantics=sem, vmem_limit_bytes=VMEM_LIMIT)


def _adaln_kernel(c_ref, w_ref, b_ref, o_ref):
    s = c_ref[...]
    s = s * jax.nn.sigmoid(s)
    o_ref[...] = jnp.dot(s.astype(BF16), w_ref[...].astype(BF16), preferred_element_type=F32) + b_ref[...]


def _adaln(cond, w_mod, b_mod):
    n = w_mod.shape[1]
    tn = 1536
    return pl.pallas_call(
        _adaln_kernel,
        grid=(n // tn,),
        in_specs=[pl.BlockSpec((8, D_MODEL), lambda j: (0, 0)),
                  pl.BlockSpec((D_MODEL, tn), lambda j: (0, j)),
                  pl.BlockSpec((1, tn), lambda j: (0, j))],
        out_specs=pl.BlockSpec((8, tn), lambda j: (0, j)),
        out_shape=jax.ShapeDtypeStruct((8, n), F32),
        compiler_params=_cparams("arbitrary"),
        name="adaln",
    )(cond, w_mod, b_mod.reshape(1, n))


def _norm_mod_kernel(x_ref, g_ref, sh_ref, sc_ref, o_ref):
    x = x_ref[...]
    y = x * lax.rsqrt(jnp.mean(x * x, axis=-1, keepdims=True) + EPS)
    y = y * g_ref[...]
    o_ref[...] = (y * (1.0 + sc_ref[...]) + sh_ref[...]).astype(o_ref.dtype)


def _norm_mod(x, g, shift, scale, tm):
    bsz, L, d = x.shape
    return pl.pallas_call(
        _norm_mod_kernel,
        grid=(bsz, L // tm),
        in_specs=[pl.BlockSpec((None, tm, d), lambda b, i: (b, i, 0)),
                  pl.BlockSpec((1, d), lambda b, i: (0, 0)),
                  pl.BlockSpec((None, 1, d), lambda b, i: (b, 0, 0)),
                  pl.BlockSpec((None, 1, d), lambda b, i: (b, 0, 0))],
        out_specs=pl.BlockSpec((None, tm, d), lambda b, i: (b, i, 0)),
        out_shape=jax.ShapeDtypeStruct((bsz, L, d), BF16),
        compiler_params=_cparams("parallel", "parallel"),
        name="norm_mod",
    )(x, g.reshape(1, d), shift, scale)


def _conv3(z, wc, bc, seg):
    tm = z.shape[0]
    pos = lax.broadcasted_iota(jnp.int32, z.shape, 0) & (seg - 1)
    zp = jnp.where(pos == 0, 0.0, pltpu.roll(z, 1, 0))
    zn = jnp.where(pos == seg - 1, 0.0, pltpu.roll(z, tm - 1, 0))
    return zp * wc[0:1, :] + z * wc[1:2, :] + zn * wc[2:3, :] + bc


def _proj_act_kernel(h_ref, w_ref, b_ref, o_ref, *, act):
    z = jnp.dot(h_ref[...], w_ref[...], preferred_element_type=F32) + b_ref[...]
    if act == "sigmoid":
        z = jax.nn.sigmoid(z)
    o_ref[...] = z.astype(o_ref.dtype)


def _proj_act(h, w, b, act, out_dtype, tm, tn=512):
    bsz, L, d = h.shape
    n = w.shape[1]
    return pl.pallas_call(
        functools.partial(_proj_act_kernel, act=act),
        grid=(bsz, L // tm, n // tn),
        in_specs=[pl.BlockSpec((None, tm, d), lambda b_, i, j: (b_, i, 0)),
                  pl.BlockSpec((d, tn), lambda b_, i, j: (0, j)),
                  pl.BlockSpec((1, tn), lambda b_, i, j: (0, j))],
        out_specs=pl.BlockSpec((None, tm, tn), lambda b_, i, j: (b_, i, j)),
        out_shape=jax.ShapeDtypeStruct((bsz, L, n), out_dtype),
        compiler_params=_cparams("parallel", "parallel", "arbitrary"),
        name="proj_" + act,
    )(h, w, b.reshape(1, n))


def _proj_conv_silu_kernel(h_ref, w_ref, b_ref, wc_ref, bc_ref, cs_ref, o_ref, *, seg):
    z = jnp.dot(h_ref[...], w_ref[...], preferred_element_type=F32) + b_ref[...]
    y = _conv3(z, wc_ref[...], bc_ref[...], seg)
    y = y * jax.nn.sigmoid(y)
    o_ref[...] = (y * cs_ref[...]).astype(o_ref.dtype)


def _proj_conv_silu(h, w, b, wc, bc, colscale, seg, tm, tn=512):
    bsz, L, d = h.shape
    n = w.shape[1]
    col = lambda b_, i, j: (0, j)
    return pl.pallas_call(
        functools.partial(_proj_conv_silu_kernel, seg=seg),
        grid=(bsz, L // tm, n // tn),
        in_specs=[pl.BlockSpec((None, tm, d), lambda b_, i, j: (b_, i, 0)),
                  pl.BlockSpec((d, tn), col),
                  pl.BlockSpec((1, tn), col),
                  pl.BlockSpec((3, tn), col),
                  pl.BlockSpec((1, tn), col),
                  pl.BlockSpec((1, tn), col)],
        out_specs=pl.BlockSpec((None, tm, tn), lambda b_, i, j: (b_, i, j)),
        out_shape=jax.ShapeDtypeStruct((bsz, L, n), BF16),
        compiler_params=_cparams("parallel", "parallel", "arbitrary"),
        name="proj_conv_silu",
    )(h, w, b.reshape(1, n), wc, bc.reshape(1, n), colscale.reshape(1, n))


PROJ_TN = 1024
PROJ_SUB = 512
PM_Q, PM_K, PM_V, PM_O, PM_GA, PM_GB = range(6)


def _proj_main_kernel(x_ref, g_ref, sh_ref, sc_ref, w_ref, b_ref, wc_ref, bc_ref, cs_ref,
                      o_ref, h_ref, hi_hbm, hp_sc, sem, *, seg):
    b, i, j = pl.program_id(0), pl.program_id(1), pl.program_id(2)
    n2, jt = hi_hbm.shape[2], hi_hbm.shape[3]

    def interleave_copy(jj):
        return pltpu.make_async_copy(hp_sc.at[pl.ds(jj * n2, n2)], hi_hbm.at[b, i, :, jj, :], sem)

    @pl.when(j == 0)
    def _():
        x = x_ref[...]
        y = x * lax.rsqrt(jnp.mean(x * x, axis=-1, keepdims=True) + EPS) * g_ref[...]
        y = y * (1.0 + sc_ref[...]) + sh_ref[...]
        h_ref[...] = y.astype(h_ref.dtype)
        hp_sc[...] = _pack_bf16_pairs(y)
        for jj in range(jt):
            interleave_copy(jj).start()

    @pl.when(j == pl.num_programs(2) - 1)
    def _():
        for jj in range(jt):
            interleave_copy(jj).wait()

    def run(epilogue):
        for c in range(PROJ_TN // PROJ_SUB):
            sl = slice(c * PROJ_SUB, (c + 1) * PROJ_SUB)
            z = jnp.dot(h_ref[...], w_ref[:, sl], preferred_element_type=F32) + b_ref[:, sl]
            o_ref[:, sl] = epilogue(z, sl).astype(o_ref.dtype)

    def conv_silu(z, sl):
        y = _conv3(z, wc_ref[:, sl], bc_ref[:, sl], seg)
        return (y * jax.nn.sigmoid(y)) * cs_ref[:, sl]

    @pl.when(j <= PM_K)
    def _():
        run(conv_silu)

    @pl.when(j == PM_V)
    def _():
        run(lambda z, sl: z)

    @pl.when(j >= PM_O)
    def _():
        run(lambda z, sl: jax.nn.sigmoid(z))


def _proj_main(x, g, shift, scale, w, b, wc, bc, colscale, seg, tm, n2):
    bsz, L, d = x.shape
    n = w.shape[1]
    jt = tm // n2
    qk = lambda b_, i, j: (0, jnp.minimum(j, PM_K))
    row = pl.BlockSpec((None, tm, d), lambda b_, i, j: (b_, i, 0))
    bvec = pl.BlockSpec((None, 1, d), lambda b_, i, j: (b_, 0, 0))
    return pl.pallas_call(
        functools.partial(_proj_main_kernel, seg=seg),
        grid=(bsz, L // tm, n // PROJ_TN),
        in_specs=[row, pl.BlockSpec((1, d), lambda b_, i, j: (0, 0)), bvec, bvec,
                  pl.BlockSpec((d, PROJ_TN), lambda b_, i, j: (0, j)),
                  pl.BlockSpec((1, PROJ_TN), lambda b_, i, j: (0, j)),
                  pl.BlockSpec((3, PROJ_TN), qk),
                  pl.BlockSpec((1, PROJ_TN), qk),
                  pl.BlockSpec((1, PROJ_TN), qk)],
        out_specs=[pl.BlockSpec((None, tm, PROJ_TN), lambda b_, i, j: (b_, i, j)), row,
                   pl.BlockSpec(memory_space=pl.ANY)],
        out_shape=[jax.ShapeDtypeStruct((bsz, L, n), BF16), jax.ShapeDtypeStruct((bsz, L, d), BF16),
                   jax.ShapeDtypeStruct((bsz, L // tm, n2, jt, d // 2), jnp.uint32)],
        scratch_shapes=[pltpu.VMEM((tm, d // 2), jnp.uint32), pltpu.SemaphoreType.DMA(())],
        compiler_params=_cparams("parallel", "parallel", "arbitrary"),
        name="proj_main",
    )(x, g.reshape(1, d), shift, scale, w, b.reshape(1, n), wc, bc.reshape(1, -1), colscale.reshape(1, -1))


def _conv3_interleaved(z, wc, bc, seg, jt):
    grp = seg * jt
    pad = jnp.zeros((jt, z.shape[1]), z.dtype)
    prev, nxt = [], []
    for g0 in range(0, z.shape[0], grp):
        zg = z[g0:g0 + grp]
        prev += [pad, zg[:grp - jt]]
        nxt += [zg[jt:], pad]
    zp = jnp.concatenate(prev, axis=0)
    zn = jnp.concatenate(nxt, axis=0)
    return zp * wc[0:1, :] + z * wc[1:2, :] + zn * wc[2:3, :] + bc


def _proj_hyena_kernel(h_ref, w0_ref, w1_ref, w2_ref, b_ref, wc_ref, bc_ref, x0_ref, s_ref, *, seg):
    n2, jt = s_ref.shape[0], s_ref.shape[1]
    h = _unpack_bf16_pairs(h_ref[...].reshape(n2 * jt, h_ref.shape[2]))
    us = []
    for g, w_ref in enumerate((w0_ref, w1_ref, w2_ref)):
        z = jnp.dot(h, w_ref[...], preferred_element_type=F32) + b_ref[g]
        us.append(_conv3_interleaved(z, wc_ref[g], bc_ref[g], seg, jt))
    x0_ref[...] = _pack_bf16_pairs(us[0]).reshape(x0_ref.shape)
    s_ref[...] = (us[1] * us[2]).reshape(s_ref.shape)


def _proj_hyena(hi, w, b, wc, bc, seg):
    bsz, nt, n2, jt, dp = hi.shape
    d, tm = 2 * dp, n2 * jt
    L = nt * tm
    tn = DFT_C_TILE
    nblk = H_WIDTH // tn
    assert n2 % seg == 0 and (jt % 8 == 0 or nt == 1)
    b3 = b.reshape(3, 1, H_WIDTH)
    wc3 = wc.reshape(3, 3, H_WIDTH).transpose(1, 0, 2)
    bc3 = bc.reshape(3, 1, H_WIDTH)
    return pl.pallas_call(
        functools.partial(_proj_hyena_kernel, seg=seg),
        grid=(bsz, nt, nblk),
        in_specs=[pl.BlockSpec((None, None, n2, jt, dp), lambda b_, i, j: (b_, i, 0, 0, 0)),
                  pl.BlockSpec((d, tn), lambda b_, i, j: (0, j)),
                  pl.BlockSpec((d, tn), lambda b_, i, j: (0, nblk + j)),
                  pl.BlockSpec((d, tn), lambda b_, i, j: (0, 2 * nblk + j)),
                  pl.BlockSpec((3, 1, tn), lambda b_, i, j: (0, 0, j)),
                  pl.BlockSpec((3, 3, tn), lambda b_, i, j: (0, 0, j)),
                  pl.BlockSpec((3, 1, tn), lambda b_, i, j: (0, 0, j))],
        out_specs=[pl.BlockSpec((None, n2, jt, tn // 2), lambda b_, i, j: (b_, 0, i, j)),
                   pl.BlockSpec((None, n2, jt, tn), lambda b_, i, j: (b_, 0, i, j))],
        out_shape=[jax.ShapeDtypeStruct((bsz, n2, L // n2, H_WIDTH // 2), jnp.uint32),
                   jax.ShapeDtypeStruct((bsz, n2, L // n2, H_WIDTH), F32)],
        compiler_params=_cparams("parallel", "parallel", "arbitrary"),
        name="proj_hyena",
    )(hi, w, w, w, b3, wc3, bc3)


N_GATES = 4 * M_HEADS


def _split3(x):
    hi = x.astype(BF16)
    r1 = x - hi.astype(F32)
    mid = r1.astype(BF16)
    lo = (r1 - mid.astype(F32)).astype(BF16)
    return hi, mid, lo


def _log_sigmoid(x):
    return jnp.minimum(x, 0.0) - jnp.log1p(jnp.exp(-jnp.abs(x)))


def _gates_kernel(h_ref, w_ref, wt_ref, b_ref, bt_ref, bc_ref, ac_ref, ar_ref):
    h = h_ref[...]
    t = h.shape[0]
    z = jnp.dot(h, w_ref[...], preferred_element_type=F32) + b_ref[...]
    zt = lax.dot_general(wt_ref[...], h, (((1,), (1,)), ((), ())),
                         preferred_element_type=F32) + bt_ref[...]
    r = lax.broadcasted_iota(jnp.int32, (t, t), 0)
    c = lax.broadcasted_iota(jnp.int32, (t, t), 1)
    lower = (r >= c).astype(BF16)
    upper = (r <= c).astype(BF16)
    g8 = FG_LANE0

    lf = _log_sigmoid(z)
    lane = lax.broadcasted_iota(jnp.int32, z.shape, 1)
    is_fg = (lane >= g8) & (lane < 2 * g8)
    terms = [jnp.where(is_fg, p.astype(F32), 0.0) for p in _split3(lf)]
    packed = terms[0] + pltpu.roll(terms[1], 2 * g8, 1) + pltpu.roll(terms[2], 4 * g8, 1)
    cfp = jnp.dot(lower, packed.astype(BF16), preferred_element_type=F32)
    cf = cfp + pltpu.roll(cfp, LANES - 2 * g8, 1) + pltpu.roll(cfp, LANES - 4 * g8, 1)
    cb = cf[t - 1:t, :] - cf + lf
    bc = jnp.where(lane < g8 + M_HEADS, cf, cb)
    bc = pltpu.roll(bc, LANES - g8, 1)
    bc_ref[...] = bc
    ac_ref[...] = z - bc

    lft = _log_sigmoid(zt[g8:, :])
    stacked = jnp.concatenate([p.astype(F32) for p in _split3(lft)] + [jnp.zeros_like(lft)], axis=0)
    cft3 = jnp.dot(stacked.astype(BF16), upper, preferred_element_type=F32)
    cft = cft3[0:g8] + cft3[g8:2 * g8] + cft3[2 * g8:3 * g8]
    cbt = cft[:, t - 1:t] - cft + lft
    row = lax.broadcasted_iota(jnp.int32, cft.shape, 0)
    ar_ref[...] = zt[:g8, :] - jnp.where(row < M_HEADS, cft, cbt)


FG_LANE0 = 2 * M_HEADS


def _gates(h, w_g, b_g, chunk):
    bsz, L, d = h.shape
    w_pad = jnp.zeros((d, LANES), F32).at[:, :N_GATES].set(w_g).astype(BF16)
    b_pad = jnp.zeros((1, LANES), F32).at[0, :N_GATES].set(b_g)
    wt = w_g.T.astype(BF16)
    bt = b_g.reshape(N_GATES, 1)
    tok = pl.BlockSpec((None, chunk, LANES), lambda b_, i: (b_, i, 0))
    return pl.pallas_call(
        _gates_kernel,
        grid=(bsz, L // chunk),
        in_specs=[pl.BlockSpec((None, chunk, d), lambda b_, i: (b_, i, 0)),
                  pl.BlockSpec((d, LANES), lambda b_, i: (0, 0)),
                  pl.BlockSpec((N_GATES, d), lambda b_, i: (0, 0)),
                  pl.BlockSpec((1, LANES), lambda b_, i: (0, 0)),
                  pl.BlockSpec((N_GATES, 1), lambda b_, i: (0, 0))],
        out_specs=[tok, tok, pl.BlockSpec((None, FG_LANE0, chunk), lambda b_, i: (b_, 0, i))],
        out_shape=[jax.ShapeDtypeStruct((bsz, L, LANES), F32),
                   jax.ShapeDtypeStruct((bsz, L, LANES), F32),
                   jax.ShapeDtypeStruct((bsz, FG_LANE0, L), F32)],
        compiler_params=_cparams("parallel", "parallel"),
        name="mlstm_gates",
    )(h, w_pad, wt, b_pad, bt)


def _mlstm_kernel(*refs, emit_h, n_chunks):
    if emit_h:
        (q_ref, k_ref, v_ref, bc_ref, ac_ref, ar_ref, c0_ref, n0_ref, m0_ref,
         h_ref, cf_ref, nf_ref, mf_ref, c_sc, n_sc, m_sc) = refs
    else:
        (k_ref, v_ref, bc_ref, ac_ref, ar_ref, c0_ref, n0_ref, m0_ref,
         cf_ref, nf_ref, mf_ref, c_sc, n_sc, m_sc) = refs
    d = pl.program_id(1)
    j = pl.program_id(2)
    fwd = d == 0
    t = k_ref.shape[0]
    dh = M_HEAD_DIM

    @pl.when(j == 0)
    def _():
        c_sc[...] = c0_ref[...]
        n_sc[...] = n0_ref[...]
        m_sc[...] = m0_ref[...]

    r = lax.broadcasted_iota(jnp.int32, (t, t), 0)
    c = lax.broadcasted_iota(jnp.int32, (t, t), 1)
    causal = jnp.where(fwd, r - c, c - r) >= 0
    bc_all = bc_ref[...]
    ac_all = ac_ref[...]
    ar_all = ar_ref[...]
    for hd in range(M_HEADS):
        sl = slice(hd * dh, (hd + 1) * dh)
        bc = jnp.where(fwd, bc_all[:, hd:hd + 1], bc_all[:, M_HEADS + hd:M_HEADS + hd + 1])
        ac = jnp.where(fwd, ac_all[:, hd:hd + 1], ac_all[:, M_HEADS + hd:M_HEADS + hd + 1])
        ar = jnp.where(fwd, ar_all[hd:hd + 1, :], ar_all[M_HEADS + hd:M_HEADS + hd + 1, :])
        b_tot = jnp.where(fwd, bc[t - 1:t, :], bc[0:1, :])
        m_prev = m_sc[hd][:, 0:1]
        k_h = k_ref[:, sl]
        v_h = v_ref[:, sl]
        if emit_h:
            q_h = q_ref[:, sl]
            dm = jnp.where(causal, bc + ar, NEG_BIG)
            inter = bc + m_prev
            m_t = jnp.maximum(inter, jnp.max(dm, axis=1, keepdims=True))
            qk = lax.dot_general(q_h, k_h, (((1,), (1,)), ((), ())), preferred_element_type=F32)
            s = qk * jnp.exp(dm - m_t)
            carry = jnp.exp(inter - m_t)
            num = (jnp.dot(s.astype(BF16), v_h, preferred_element_type=F32)
                   + carry * jnp.dot(q_h, c_sc[hd].astype(BF16), preferred_element_type=F32))
            den = (jnp.sum(s, axis=1, keepdims=True)
                   + carry * jnp.sum(q_h.astype(F32) * n_sc[hd], axis=1, keepdims=True))
            h_ref[:, sl] = (num / jnp.maximum(jnp.abs(den), jnp.exp(-m_t))).astype(h_ref.dtype)
        g = b_tot + ac
        m_new = jnp.maximum(b_tot + m_prev, jnp.max(g, axis=0, keepdims=True))
        wgt = jnp.exp(g - m_new)
        decay = jnp.exp(b_tot + m_prev - m_new)
        kw = k_h.astype(F32) * wgt
        c_sc[hd] = decay * c_sc[hd] + lax.dot_general(kw.astype(BF16), v_h, (((0,), (0,)), ((), ())),
                                                      preferred_element_type=F32)
        n_sc[hd] = decay * n_sc[hd] + jnp.sum(kw, axis=0, keepdims=True)
        m_sc[hd] = jnp.broadcast_to(m_new, (1, LANES))

    @pl.when(j == n_chunks - 1)
    def _():
        cf_ref[...] = c_sc[...]
        nf_ref[...] = n_sc[...]
        mf_ref[...] = m_sc[...]


def _mlstm(q, k, v, bc, ac, ar, state, emit_h, t):
    bsz, L, _ = k[0].shape
    nc = L // t
    seq = lambda b_, d, j: (b_, j + d * (nc - 1 - 2 * j), 0)
    st = lambda b_, d, j: (b_, d, 0, 0, 0)

    def tok(col):
        return pl.BlockSpec((None, t, M_WIDTH), lambda b_, d, j: (b_, j + d * (nc - 1 - 2 * j), col))

    gate_spec = pl.BlockSpec((None, t, LANES), seq)
    ar_spec = pl.BlockSpec((None, FG_LANE0, t), lambda b_, d, j: (b_, 0, j + d * (nc - 1 - 2 * j)))
    c_spec = pl.BlockSpec((None, None, M_HEADS, M_HEAD_DIM, M_HEAD_DIM), st)
    n_spec = pl.BlockSpec((None, None, M_HEADS, 1, M_HEAD_DIM), st)
    m_spec = pl.BlockSpec((None, None, M_HEADS, 1, LANES), st)
    state_shapes = [jax.ShapeDtypeStruct((bsz, 2, M_HEADS, M_HEAD_DIM, M_HEAD_DIM), F32),
                    jax.ShapeDtypeStruct((bsz, 2, M_HEADS, 1, M_HEAD_DIM), F32),
                    jax.ShapeDtypeStruct((bsz, 2, M_HEADS, 1, LANES), F32)]
    in_specs = [tok(k[1]), tok(v[1]), gate_spec, gate_spec, ar_spec, c_spec, n_spec, m_spec]
    args = [k[0], v[0], bc, ac, ar, *state]
    out_specs = [c_spec, n_spec, m_spec]
    out_shape = list(state_shapes)
    if emit_h:
        in_specs = [tok(q[1])] + in_specs
        args = [q[0]] + args
        out_specs = [pl.BlockSpec((None, None, t, M_WIDTH),
                                  lambda b_, d, j: (d, b_, j + d * (nc - 1 - 2 * j), 0))] + out_specs
        out_shape = [jax.ShapeDtypeStruct((2, bsz, L, M_WIDTH), BF16)] + out_shape
    outs = pl.pallas_call(
        functools.partial(_mlstm_kernel, emit_h=emit_h, n_chunks=nc),
        grid=(bsz, 2, nc),
        in_specs=in_specs,
        out_specs=out_specs,
        out_shape=out_shape,
        scratch_shapes=[pltpu.VMEM((M_HEADS, M_HEAD_DIM, M_HEAD_DIM), F32),
                        pltpu.VMEM((M_HEADS, 1, M_HEAD_DIM), F32),
                        pltpu.VMEM((M_HEADS, 1, LANES), F32)],
        compiler_params=_cparams("parallel", "parallel", "arbitrary"),
        name="mlstm" if emit_h else "mlstm_state",
    )(*args)
    if emit_h:
        return outs[0], tuple(outs[1:])
    return None, tuple(outs)


DFT_M_TILE = 8
DFT_C_TILE = 512
FEAT_ROWS = 16


def _filter_outer_kernel(bands_ref, w1t_ref, b1_ref, w2t_ref, b2_ref, w3p_ref, w3f_ref, fr_ref, dl_ref, l_ref,
                         a_ref, ss_ref, *, L, n1, n2):
    i = pl.program_id(0)
    h = n1 // 2
    cols = DFT_M_TILE * h

    def positions(shape, axis, side):
        q = lax.broadcasted_iota(jnp.int32, shape, axis)
        mm, jj = q // h, q % h
        n = n2 * (jj + side * h) + i * DFT_M_TILE + mm
        return n, jnp.where(n < L, n, 2 * L - n).astype(F32)

    taps = []
    sumsq = jnp.zeros((1, a_ref.shape[-1]), F32)
    for side, w3_ref in ((0, w3p_ref), (1, w3f_ref)):
        _, p_row = positions((1, cols), 1, side)
        t_row = p_row / float(max(L - 1, 1))
        ang = ((2 * math.pi / L) * p_row) * bands_ref[...]
        row = lax.broadcasted_iota(jnp.int32, (FEAT_ROWS, cols), 0)
        feats = jnp.concatenate([jnp.where(row == 0, t_row, 0.0), jnp.cos(ang), -jnp.sin(ang)], axis=0)
        fr = fr_ref[...]
        hid = jnp.sin(fr * (jnp.dot(w1t_ref[...], feats.astype(BF16), preferred_element_type=F32) + b1_ref[...]))
        hid = jnp.sin(fr * (jnp.dot(w2t_ref[...], hid.astype(BF16), preferred_element_type=F32) + b2_ref[...]))
        filt = lax.dot_general(hid.astype(BF16), w3_ref[...], (((0,), (0,)), ((), ())),
                               preferred_element_type=F32)
        n_col, p_col = positions((cols, 1), 0, side)
        t_col = p_col / float(max(L - 1, 1))
        kern = filt * jnp.exp(-t_col * jnp.abs(dl_ref[...]))
        kern = jnp.where(n_col == L, 0.0, kern)
        sumsq = sumsq + jnp.sum(kern * kern, axis=0, keepdims=True)
        taps.append(kern)

    for mm in range(DFT_M_TILE):
        x = jnp.concatenate([taps[0][mm * h:(mm + 1) * h], taps[1][mm * h:(mm + 1) * h]], axis=0)
        out = jnp.dot(l_ref[...], x.astype(BF16), preferred_element_type=F32)
        a_ref[0, :, mm, :] = out[:n1]
        a_ref[1, :, mm, :] = out[n1:]

    @pl.when(i == 0)
    def _():
        ss_ref[...] = jnp.zeros_like(ss_ref)

    ss_ref[...] += sumsq


def _filter_outer(L, n1, n2, fwd_r, w1, b1, w2, b2, w3, freq):
    hid = H_FILTER_HIDDEN
    bands = jnp.linspace(1e-4, H_POS_BANDS - 1, H_POS_BANDS, dtype=F32).reshape(H_POS_BANDS, 1)
    w1t = jnp.zeros((hid, 3 * FEAT_ROWS), F32)
    w1t = w1t.at[:, 0].set(w1[0]).at[:, FEAT_ROWS:2 * FEAT_ROWS].set(w1[1:1 + H_POS_BANDS].T)
    w1t = w1t.at[:, 2 * FEAT_ROWS:].set(w1[1 + H_POS_BANDS:].T).astype(BF16)
    w3h = w3.astype(BF16)
    max_decay = math.log(H_DECAY_TARGET) / H_FAST_DECAY_PCT
    min_decay = math.log(H_DECAY_TARGET) / H_SLOW_DECAY_PCT
    deltas = jnp.linspace(min_decay, max_decay, H_WIDTH, dtype=F32).reshape(1, H_WIDTH)
    col = lambda v: v.reshape(hid, 1)
    full = lambda a: pl.BlockSpec(a.shape, lambda i: (0,) * a.ndim)
    args = [bands, w1t, col(b1), w2.T.astype(BF16), col(b2)]
    return pl.pallas_call(
        functools.partial(_filter_outer_kernel, L=L, n1=n1, n2=n2),
        grid=(n2 // DFT_M_TILE,),
        in_specs=[full(a) for a in args]
        + [pl.BlockSpec((hid, H_WIDTH), lambda i: (0, 0)), pl.BlockSpec((hid, H_WIDTH), lambda i: (0, 1)),
           full(col(freq)), full(deltas), full(fwd_r)],
        out_specs=[pl.BlockSpec((2, n1, DFT_M_TILE, H_WIDTH), lambda i: (0, 0, i, 0)),
                   pl.BlockSpec((1, H_WIDTH), lambda i: (0, 0))],
        out_shape=[jax.ShapeDtypeStruct((2, n1, n2, H_WIDTH), F32),
                   jax.ShapeDtypeStruct((1, H_WIDTH), F32)],
        compiler_params=_cparams("arbitrary"),
        name="hyena_filter_outer",
    )(*args, w3h, w3h, col(freq), deltas, fwd_r)


def _dft_factors(n):
    lg = int(round(math.log2(n)))
    n1 = 1 << ((lg + 1) // 2)
    return n1, n // n1


def _dft_outer_matrices(n1):
    k = np.arange(n1)[:, None]
    n = np.arange(n1)[None, :]
    ang = 2.0 * np.pi * ((k * n) % n1) / n1
    cr, ci = np.cos(ang), -np.sin(ang)
    h = n1 // 2
    fwd_c = np.block([[cr[:, :h], -ci[:, :h]], [ci[:, :h], cr[:, :h]]])
    fwd_r = np.concatenate([cr, ci], axis=0)
    ir, ii = cr[:h, :], -ci[:h, :]
    inv = np.block([[ir, -ii], [ii, ir]])
    return (jnp.asarray(fwd_c, F32).astype(BF16), jnp.asarray(fwd_r, F32).astype(BF16),
            jnp.asarray(inv, F32).astype(BF16))


def _dft_inner_matrices(n1, n2):
    n = n1 * n2
    k2 = np.arange(n2)[:, None]
    m = np.arange(n2)[None, :]
    ang = 2.0 * np.pi * ((k2 * m) % n2) / n2
    fr, fi = np.cos(ang), -np.sin(ang)
    f = np.block([[fr, -fi], [fi, fr]])
    k1 = jnp.arange(n1, dtype=jnp.int32)[:, None]
    tw_ang = ((jnp.arange(n2, dtype=jnp.int32)[None, :] * k1) % n).astype(F32) * (2.0 * math.pi / n)
    rep = lambda t: jnp.broadcast_to(t[:, :, None], (n1, n2, LANES))
    return (jnp.asarray(f, F32).astype(BF16), jnp.asarray(f.T, F32).astype(BF16),
            rep(jnp.cos(tw_ang)), rep(-jnp.sin(tw_ang)))


def _outer_fwd_kernel(l_ref, s_ref, a_ref):
    n1 = a_ref.shape[1]
    for mm in range(s_ref.shape[1]):
        x = jnp.concatenate([s_ref[0, mm], s_ref[1, mm]], axis=0).astype(BF16)
        out = jnp.dot(l_ref[...], x, preferred_element_type=F32)
        a_ref[0, :, mm, :] = out[:n1]
        a_ref[1, :, mm, :] = out[n1:]


def _outer_fwd(lmat, s_t):
    _, n2, n1h, c = s_t.shape
    n1 = 2 * n1h
    tc = min(DFT_C_TILE, c)
    return pl.pallas_call(
        _outer_fwd_kernel,
        grid=(n2 // DFT_M_TILE, c // tc),
        in_specs=[pl.BlockSpec(lmat.shape, lambda m, j: (0, 0)),
                  pl.BlockSpec((2, DFT_M_TILE, n1h, tc), lambda m, j: (0, m, 0, j))],
        out_specs=pl.BlockSpec((2, n1, DFT_M_TILE, tc), lambda m, j: (0, 0, m, j)),
        out_shape=jax.ShapeDtypeStruct((2, n1, n2, c), F32),
        compiler_params=_cparams("parallel", "parallel"),
        name="dft_outer_fwd",
    )(lmat, s_t)


def _outer_inv_kernel(l_ref, b_ref, s_ref, x0_ref, ysc_ref, hb_ref, o_ref):
    n1h = s_ref.shape[2]
    for mm in range(b_ref.shape[1]):
        y = jnp.concatenate([b_ref[0, mm], b_ref[1, mm]], axis=0).astype(BF16)
        out = jnp.dot(l_ref[...], y, preferred_element_type=F32)
        for b in range(2):
            conv = out[b * n1h:(b + 1) * n1h]
            x0 = _unpack_bf16_pairs(x0_ref[b, mm]).astype(F32)
            hy = x0 * (conv * ysc_ref[...] + hb_ref[...] * s_ref[b, mm])
            o_ref[b, :, mm, :] = _pack_bf16_pairs(hy)


def _outer_inv(lmat, b_t, s_t, x0_t, yscale, h_bias):
    _, n2, n1, c = b_t.shape
    n1h = n1 // 2
    tc = min(DFT_C_TILE, c)
    vec = pl.BlockSpec((1, tc), lambda m, j: (0, j))
    hy = pl.pallas_call(
        _outer_inv_kernel,
        grid=(n2 // DFT_M_TILE, c // tc),
        in_specs=[pl.BlockSpec(lmat.shape, lambda m, j: (0, 0)),
                  pl.BlockSpec((2, DFT_M_TILE, n1, tc), lambda m, j: (0, m, 0, j)),
                  pl.BlockSpec((2, DFT_M_TILE, n1h, tc), lambda m, j: (0, m, 0, j)),
                  pl.BlockSpec((2, DFT_M_TILE, n1h, tc // 2), lambda m, j: (0, m, 0, j)),
                  vec, vec],
        out_specs=pl.BlockSpec((2, n1h, DFT_M_TILE, tc // 2), lambda m, j: (0, 0, m, j)),
        out_shape=jax.ShapeDtypeStruct((2, n1h, n2, c // 2), jnp.uint32),
        compiler_params=_cparams("parallel", "parallel"),
        name="dft_outer_inv",
    )(lmat, b_t, s_t, x0_t, yscale, h_bias.reshape(1, c))
    return hy.reshape(2, n1h * n2, c // 2)


DFT_K_TILE = 8


def _twiddled_inner_dft(f_ref, twr_ref, twi_ref, a_ref, kk):
    n2, c = a_ref.shape[2], a_ref.shape[3]
    twr = jnp.tile(twr_ref[kk], (1, c // LANES))
    twi = jnp.tile(twi_ref[kk], (1, c // LANES))
    ar, ai = a_ref[0, kk], a_ref[1, kk]
    a = jnp.concatenate([(ar * twr - ai * twi).astype(BF16), (ar * twi + ai * twr).astype(BF16)], axis=0)
    x = jnp.dot(f_ref[...], a, preferred_element_type=F32)
    return x[:n2], x[n2:], twr, twi


def _inner_fwd_kernel(f_ref, twr_ref, twi_ref, a_ref, o_ref):
    for kk in range(a_ref.shape[1]):
        xr, xi, _, _ = _twiddled_inner_dft(f_ref, twr_ref, twi_ref, a_ref, kk)
        o_ref[0, kk] = xr.astype(o_ref.dtype)
        o_ref[1, kk] = xi.astype(o_ref.dtype)


def _inner_specs(n1, n2, c):
    tc = min(DFT_C_TILE, c)
    kt = min(DFT_K_TILE, n1)
    blk = pl.BlockSpec((2, kt, n2, tc), lambda k, j: (0, k, 0, j))
    mat = pl.BlockSpec((2 * n2, 2 * n2), lambda k, j: (0, 0))
    tw = pl.BlockSpec((kt, n2, LANES), lambda k, j: (k, 0, 0))
    return blk, mat, tw, (n1 // kt, c // tc), kt, tc


def _inner_fwd(f, twr, twi, a):
    _, n1, n2, c = a.shape
    blk, mat, tw, grid, _, _ = _inner_specs(n1, n2, c)
    return pl.pallas_call(
        _inner_fwd_kernel,
        grid=grid,
        in_specs=[mat, tw, tw, blk],
        out_specs=blk,
        out_shape=jax.ShapeDtypeStruct((2, n1, n2, c), BF16),
        compiler_params=_cparams("parallel", "parallel"),
        name="dft_inner_filter",
    )(f, twr, twi, a)


def _inner_conv_kernel(f_ref, ft_ref, twr_ref, twi_ref, a_ref, k_ref, o_ref):
    n2 = a_ref.shape[2]
    for kk in range(a_ref.shape[1]):
        xr, xi, twr, twi = _twiddled_inner_dft(f_ref, twr_ref, twi_ref, a_ref, kk)
        kr, ki = k_ref[0, kk].astype(F32), k_ref[1, kk].astype(F32)
        yr = xr * kr - xi * ki
        yi = xr * ki + xi * kr
        y = jnp.concatenate([yr.astype(BF16), yi.astype(BF16)], axis=0)
        b = jnp.dot(ft_ref[...], y, preferred_element_type=F32)
        br, bi = b[:n2], b[n2:]
        o_ref[0, :, kk, :] = br * twr + bi * twi
        o_ref[1, :, kk, :] = bi * twr - br * twi


def _inner_conv(f, ft, twr, twi, a, kf):
    _, n1, n2, c = a.shape
    blk, mat, tw, grid, kt, tc = _inner_specs(n1, n2, c)
    return pl.pallas_call(
        _inner_conv_kernel,
        grid=grid,
        in_specs=[mat, mat, tw, tw, blk, blk],
        out_specs=pl.BlockSpec((2, n2, kt, tc), lambda k, j: (0, 0, k, j)),
        out_shape=jax.ShapeDtypeStruct((2, n2, n1, c), F32),
        compiler_params=_cparams("parallel", "parallel"),
        name="dft_inner_conv",
    )(f, ft, twr, twi, a, kf)


def _hyena_long_conv(s_t, x0_t, h_bias, w1, b1, w2, b2, w3, freq):
    bsz, n2, n1h, c = s_t.shape
    assert bsz == 2
    n1 = 2 * n1h
    L = n1h * n2
    fwd_c, fwd_r, inv = _dft_outer_matrices(n1)
    f, ft, twr, twi = _dft_inner_matrices(n1, n2)
    af, sumsq = _filter_outer(L, n1, n2, fwd_r, w1, b1, w2, b2, w3, freq)
    kf = _inner_fwd(f, twr, twi, af)
    a = _outer_fwd(fwd_c, s_t)
    b_t = _inner_conv(f, ft, twr, twi, a, kf)
    yscale = lax.rsqrt(sumsq + EPS) * (1.0 / (2 * L))
    return _outer_inv(inv, b_t, s_t, x0_t, yscale, h_bias)


def _pack_bf16_pairs(x):
    half = x.shape[1] // 2
    lo = pltpu.bitcast(x[:, :half].astype(BF16).astype(F32), jnp.uint32) >> 16
    hi = pltpu.bitcast(x[:, half:].astype(BF16).astype(F32), jnp.uint32) & jnp.uint32(0xFFFF0000)
    return lo | hi


def _unpack_bf16_pairs(p):
    lo = pltpu.bitcast(p << 16, F32).astype(BF16)
    hi = pltpu.bitcast(p & jnp.uint32(0xFFFF0000), F32).astype(BF16)
    return jnp.concatenate([lo, hi], axis=1)


def _merge_kernel(hf_ref, hb_ref, o_ref, hy_ref, ga_ref, gb_ref, x_ref,
                  gate_ref, g2_ref, sh_ref, sc_ref, wa_ref, wb_ref, wo_ref, x1_ref, h2_ref):
    a = o_ref[...].astype(F32) * (hf_ref[...].astype(F32) + hb_ref[...].astype(F32))
    half = DFT_C_TILE // 2
    hy = jnp.concatenate([_unpack_bf16_pairs(hy_ref[:, c * half:(c + 1) * half])
                          for c in range(hy_ref.shape[1] // half)], axis=1)
    pa = jnp.dot(a.astype(BF16), wa_ref[...], preferred_element_type=F32)
    pb = jnp.dot(hy, wb_ref[...], preferred_element_type=F32)
    mix = ga_ref[...].astype(F32) * pa + gb_ref[...].astype(F32) * pb
    out = jnp.dot(mix.astype(BF16), wo_ref[...], preferred_element_type=F32)
    x1 = x_ref[...] + gate_ref[...] * out
    x1_ref[...] = x1
    y = x1 * lax.rsqrt(jnp.mean(x1 * x1, axis=-1, keepdims=True) + EPS) * g2_ref[...]
    h2_ref[...] = _pack_bf16_pairs(y * (1.0 + sc_ref[...]) + sh_ref[...])


def _merge(hdirs, pm, hy, x, gate1, g2, shift2, scale2, w_a, w_b, w_out, tm=512):
    bsz, L, d = x.shape
    tok = pl.BlockSpec((None, tm, d), lambda b, i: (b, i, 0))

    def pm_tile(col):
        return pl.BlockSpec((None, tm, d), lambda b, i: (b, i, col))

    packed = pl.BlockSpec((None, tm, d // 2), lambda b, i: (b, i, 0))
    vec = pl.BlockSpec((1, d), lambda b, i: (0, 0))
    bvec = pl.BlockSpec((None, 1, d), lambda b, i: (b, 0, 0))
    wsp = pl.BlockSpec((d, d), lambda b, i: (0, 0), pipeline_mode=pl.Buffered(1))
    return pl.pallas_call(
        _merge_kernel,
        grid=(bsz, L // tm),
        in_specs=[pl.BlockSpec((None, None, tm, d), lambda b, i: (0, b, i, 0)),
                  pl.BlockSpec((None, None, tm, d), lambda b, i: (1, b, i, 0)),
                  pm_tile(PM_O), packed, pm_tile(PM_GA), pm_tile(PM_GB), tok,
                  bvec, vec, bvec, bvec, wsp, wsp, wsp],
        out_specs=[tok, packed],
        out_shape=[jax.ShapeDtypeStruct((bsz, L, d), F32), jax.ShapeDtypeStruct((bsz, L, d // 2), jnp.uint32)],
        compiler_params=_cparams("parallel", "parallel"),
        name="merge",
    )(hdirs, hdirs, pm, hy, pm, pm, x, gate1, g2.reshape(1, d), shift2, scale2, w_a, w_b, w_out)


MOE_BLOCK = 256
ROUTE_E1, ROUTE_E2, ROUTE_W1, ROUTE_W2 = 0, 1, 2, 3
EXP_LANE0 = N_GROUPS


def _first_lane_of_max(val, valid, lane):
    masked = jnp.where(valid, val, NEG_BIG)
    mx = jnp.max(masked, axis=1, keepdims=True)
    idx = jnp.min(jnp.where(valid & (masked == mx), lane, LANES), axis=1, keepdims=True)
    return mx, idx


MOE_TM = 1024


def _expert_onehots(rec):
    lane = lax.broadcasted_iota(jnp.int32, rec.shape, 1)
    oh1 = lane == rec[:, ROUTE_E1:ROUTE_E1 + 1].astype(jnp.int32)
    oh2 = lane == rec[:, ROUTE_E2:ROUTE_E2 + 1].astype(jnp.int32)
    return oh1, oh2


def _router_kernel(h_ref, w_ref, b_ref, r_ref, cnt_ref):
    logits = jnp.dot(_unpack_bf16_pairs(h_ref[...]), w_ref[...], preferred_element_type=F32) + b_ref[...]
    lane = lax.broadcasted_iota(jnp.int32, logits.shape, 1)
    is_g = lane < N_GROUPS
    gmax, gsel = _first_lane_of_max(logits, is_g, lane)
    gsum = jnp.sum(jnp.where(is_g, jnp.exp(logits - gmax), 0.0), axis=1, keepdims=True)
    gw = 1.0 / gsum
    lo = EXP_LANE0 + gsel * EXPERTS_PER_GROUP
    in_grp = (lane >= lo) & (lane < lo + EXPERTS_PER_GROUP)
    emax, l1 = _first_lane_of_max(logits, in_grp, lane)
    esum = jnp.sum(jnp.where(in_grp, jnp.exp(logits - emax), 0.0), axis=1, keepdims=True)
    e2max, l2 = _first_lane_of_max(logits, in_grp & (lane != l1), lane)
    v1 = 1.0 / esum
    v2 = jnp.exp(e2max - emax) / esum
    vs = v1 + v2
    w1 = gw * v1 / vs
    w2 = gw * v2 / vs
    e1 = (l1 - EXP_LANE0).astype(F32)
    e2 = (l2 - EXP_LANE0).astype(F32)
    rec = jnp.where(lane == ROUTE_E1, e1,
                    jnp.where(lane == ROUTE_E2, e2,
                              jnp.where(lane == ROUTE_W1, w1,
                                        jnp.where(lane == ROUTE_W2, w2, 0.0))))
    r_ref[...] = rec
    oh1, oh2 = _expert_onehots(rec)
    counts = jnp.sum((oh1 | oh2).astype(F32), axis=0, keepdims=True)
    cnt_ref[...] = jnp.broadcast_to(counts, cnt_ref.shape)


def _router(h2, w_group, b_group, w_router, b_router):
    n, dp = h2.shape
    d = 2 * dp
    tm = MOE_TM
    w = jnp.zeros((d, LANES), F32).at[:, :N_GROUPS].set(w_group).at[
        :, EXP_LANE0:EXP_LANE0 + N_EXPERTS].set(w_router).astype(BF16)
    b = jnp.zeros((1, LANES), F32).at[0, :N_GROUPS].set(b_group).at[
        0, EXP_LANE0:EXP_LANE0 + N_EXPERTS].set(b_router)
    return pl.pallas_call(
        _router_kernel,
        grid=(n // tm,),
        in_specs=[pl.BlockSpec((tm, dp), lambda i: (i, 0)),
                  pl.BlockSpec((d, LANES), lambda i: (0, 0)),
                  pl.BlockSpec((1, LANES), lambda i: (0, 0))],
        out_specs=[pl.BlockSpec((tm, LANES), lambda i: (i, 0)),
                   pl.BlockSpec((None, 8, LANES), lambda i: (i, 0, 0))],
        out_shape=[jax.ShapeDtypeStruct((n, LANES), F32), jax.ShapeDtypeStruct((n // tm, 8, LANES), F32)],
        compiler_params=_cparams("parallel"),
        name="moe_router",
    )(h2, w, b)


def _slots_kernel(r_ref, base_ref, dest_ref):
    rec = r_ref[...]
    tm = rec.shape[0]
    lane = lax.broadcasted_iota(jnp.int32, rec.shape, 1)
    oh1, oh2 = _expert_onehots(rec)
    r = lax.broadcasted_iota(jnp.int32, (tm, tm), 0)
    c = lax.broadcasted_iota(jnp.int32, (tm, tm), 1)
    earlier = (r > c).astype(BF16)
    rank = jnp.dot(earlier, (oh1 | oh2).astype(BF16), preferred_element_type=F32) + base_ref[0:1, :]
    d1 = jnp.sum(jnp.where(oh1, rank, 0.0), axis=1, keepdims=True)
    d2 = jnp.sum(jnp.where(oh2, rank, 0.0), axis=1, keepdims=True)
    dest_ref[...] = jnp.where(lane == 0, d1, jnp.where(lane == 1, d2, 0.0)).astype(jnp.int32)


def _slots(route, tile_counts):
    n = route.shape[0]
    tm = MOE_TM
    cnt = tile_counts[:, 0, :]
    totals = jnp.sum(cnt, axis=0)
    nblk = jnp.ceil(totals * (1.0 / MOE_BLOCK))
    first_slot = (jnp.cumsum(nblk) - nblk) * float(MOE_BLOCK)
    base = first_slot[None, :] + jnp.cumsum(cnt, axis=0) - cnt
    base = jnp.broadcast_to(base[:, None, :], tile_counts.shape)
    dest = pl.pallas_call(
        _slots_kernel,
        grid=(n // tm,),
        in_specs=[pl.BlockSpec((tm, LANES), lambda i: (i, 0)),
                  pl.BlockSpec((None, 8, LANES), lambda i: (i, 0, 0))],
        out_specs=pl.BlockSpec((tm, LANES), lambda i: (i, 0)),
        out_shape=jax.ShapeDtypeStruct((n, LANES), jnp.int32),
        compiler_params=_cparams("parallel"),
        name="moe_slots",
    )(route, base)
    return dest, totals


EXPERT_STEP_BLOCKS = 4


def _experts_kernel(be_ref, first_ref, nxt_ref, par_ref, nu_ref, x_ref, w1_hbm, w3_hbm, w2_hbm, o_ref,
                    w1f, w3f, w2f, w1b, w3b, w2b, sems):
    step = pl.program_id(0)

    def weight_copies(e, slot):
        return (pltpu.make_async_copy(w1_hbm.at[e], w1f.at[slot], sems.at[0, slot]),
                pltpu.make_async_copy(w3_hbm.at[e], w3f.at[slot], sems.at[1, slot]),
                pltpu.make_async_copy(w2_hbm.at[e], w2f.at[slot], sems.at[2, slot]))

    @pl.when(step == 0)
    def _():
        for cp in weight_copies(be_ref[0], 0):
            cp.start()

    for sub in range(EXPERT_STEP_BLOCKS):
        i = step * EXPERT_STEP_BLOCKS + sub
        rows = pl.ds(sub * MOE_BLOCK, MOE_BLOCK)

        @pl.when(first_ref[i] == 1)
        def _():
            slot = par_ref[i]

            @pl.when(nxt_ref[i] >= 0)
            def _():
                for cp in weight_copies(nxt_ref[i], 1 - slot):
                    cp.start()

            for cp in weight_copies(be_ref[i], slot):
                cp.wait()
            w1b[...] = w1f[slot].astype(BF16)
            w3b[...] = w3f[slot].astype(BF16)
            w2b[...] = w2f[slot].astype(BF16)

        @pl.when(i < nu_ref[0])
        def _():
            x = _unpack_bf16_pairs(x_ref[rows, :])
            a = jnp.dot(x, w1b[...], preferred_element_type=F32)
            b = jnp.dot(x, w3b[...], preferred_element_type=F32)
            hmid = (a * jax.nn.sigmoid(a)) * b
            o_ref[rows, :] = _pack_bf16_pairs(jnp.dot(hmid.astype(BF16), w2b[...], preferred_element_type=F32))

        @pl.when(i >= nu_ref[0])
        def _():
            o_ref[rows, :] = jnp.zeros((MOE_BLOCK, o_ref.shape[1]), o_ref.dtype)


def _experts(xs, nb, block_e, n_used, w1_e, w3_e, w2_e):
    dp = xs.shape[1]
    d, de = w1_e.shape[1], w1_e.shape[2]
    idx = jnp.arange(nb, dtype=jnp.int32)
    used = idx < n_used[0]
    first = used & ((idx == 0) | (block_e != jnp.roll(block_e, 1)))
    ordinal = jnp.cumsum(first.astype(jnp.int32)) - 1
    par = (ordinal % 2).astype(jnp.int32)
    first_pos = jnp.where(first, idx, nb)
    next_first = lax.cummin(jnp.concatenate([first_pos[1:], jnp.full((1,), nb, jnp.int32)]), reverse=True)
    nxt = jnp.where(next_first < nb, block_e[jnp.minimum(next_first, nb - 1)], -1).astype(jnp.int32)
    any_spec = pl.BlockSpec(memory_space=pl.ANY)
    assert nb % EXPERT_STEP_BLOCKS == 0
    step_rows = EXPERT_STEP_BLOCKS * MOE_BLOCK
    grid_spec = pltpu.PrefetchScalarGridSpec(
        num_scalar_prefetch=5,
        grid=(nb // EXPERT_STEP_BLOCKS,),
        in_specs=[pl.BlockSpec((step_rows, dp), lambda i, *_: (i, 0)), any_spec, any_spec, any_spec],
        out_specs=pl.BlockSpec((step_rows, dp), lambda i, *_: (i, 0)),
        scratch_shapes=[pltpu.VMEM((2, d, de), F32), pltpu.VMEM((2, d, de), F32), pltpu.VMEM((2, de, d), F32),
                        pltpu.VMEM((d, de), BF16), pltpu.VMEM((d, de), BF16), pltpu.VMEM((de, d), BF16),
                        pltpu.SemaphoreType.DMA((3, 2))],
    )
    return pl.pallas_call(
        _experts_kernel,
        grid_spec=grid_spec,
        out_shape=jax.ShapeDtypeStruct((nb * MOE_BLOCK, dp), xs.dtype),
        compiler_params=_cparams("arbitrary"),
        name="moe_experts",
    )(block_e, first.astype(jnp.int32), nxt, par, n_used, xs, w1_e, w3_e, w2_e)


SC_WINDOW = 128
SC_CORES, SC_SUBCORES = 2, 16
SC_WORKERS = SC_CORES * SC_SUBCORES


def _sc_worker_id():
    return lax.axis_index("c") * SC_SUBCORES + lax.axis_index("s")


def _sc_mesh():
    return plsc.VectorSubcoreMesh(core_axis_name="c", subcore_axis_name="s")


def _sc_dispatch(rows, dest0, dest1, pad_slots, n_rows):
    n, dv = rows.shape
    nwin, pwin = n // SC_WINDOW, pad_slots.shape[0] // SC_WINDOW
    assert n % (SC_WINDOW * SC_WORKERS) == 0 and pad_slots.shape[0] % (SC_WINDOW * SC_WORKERS) == 0
    zeros = jnp.zeros((SC_WINDOW, dv), rows.dtype)

    @pl.kernel(out_type=jax.ShapeDtypeStruct((n_rows, dv), rows.dtype), mesh=_sc_mesh(),
               scratch_types=[pltpu.VMEM((1, SC_WINDOW), jnp.int32), pltpu.VMEM((SC_WINDOW, dv), rows.dtype)],
               name="moe_dispatch_sc")
    def scatter(x_hbm, d0_hbm, d1_hbm, p_hbm, z_hbm, o_hbm, idx, buf):
        wid = _sc_worker_id()
        pltpu.sync_copy(z_hbm, buf)

        @pl.loop(0, pwin // SC_WORKERS)
        def _(t):
            w = t * SC_WORKERS + wid
            pltpu.sync_copy(p_hbm.at[pl.ds(w, 1)], idx)
            pltpu.sync_copy(buf, o_hbm.at[idx.at[0]])

        @pl.loop(0, nwin // SC_WORKERS)
        def _(t):
            w = t * SC_WORKERS + wid
            pltpu.sync_copy(x_hbm.at[pl.ds(w * SC_WINDOW, SC_WINDOW)], buf)
            for d_hbm in (d0_hbm, d1_hbm):
                pltpu.sync_copy(d_hbm.at[pl.ds(w, 1)], idx)
                pltpu.sync_copy(buf, o_hbm.at[idx.at[0]])

    return scatter(rows, dest0.reshape(nwin, SC_WINDOW), dest1.reshape(nwin, SC_WINDOW),
                   pad_slots.reshape(pwin, SC_WINDOW), zeros)


def _sc_gather(table, index):
    m = index.shape[0]
    dv = table.shape[1]
    nwin = m // SC_WINDOW
    assert m % (SC_WINDOW * SC_WORKERS) == 0

    @pl.kernel(out_type=jax.ShapeDtypeStruct((m, dv), table.dtype), mesh=_sc_mesh(),
               scratch_types=[pltpu.VMEM((1, SC_WINDOW), jnp.int32), pltpu.VMEM((SC_WINDOW, dv), table.dtype)],
               name="moe_gather_sc")
    def gather(x_hbm, i_hbm, o_hbm, idx, buf):
        wid = _sc_worker_id()

        @pl.loop(0, nwin // SC_WORKERS)
        def _(t):
            w = t * SC_WORKERS + wid
            pltpu.sync_copy(i_hbm.at[pl.ds(w, 1)], idx)
            pltpu.sync_copy(x_hbm.at[idx.at[0]], buf)
            pltpu.sync_copy(buf, o_hbm.at[pl.ds(w * SC_WINDOW, SC_WINDOW)])

    return gather(table, index.reshape(nwin, SC_WINDOW))


def _combine_planes_kernel(r_ref, ya_ref, yb_ref, x_ref, gate_ref, gf_ref, o_ref):
    rec = r_ref[...]
    y = (_unpack_bf16_pairs(ya_ref[...]).astype(F32) * rec[:, ROUTE_W1:ROUTE_W1 + 1]
         + _unpack_bf16_pairs(yb_ref[...]).astype(F32) * rec[:, ROUTE_W2:ROUTE_W2 + 1])
    x2 = x_ref[...] + gate_ref[...] * y
    o_ref[...] = x2 * lax.rsqrt(jnp.mean(x2 * x2, axis=-1, keepdims=True) + EPS) * gf_ref[...]


def _combine_planes(g, route, x1, gate2, g_final, tm=512):
    bsz, L, d = x1.shape
    tpb = L // tm
    dp = g.shape[-1]
    return pl.pallas_call(
        _combine_planes_kernel,
        grid=(bsz, tpb),
        in_specs=[pl.BlockSpec((tm, LANES), lambda b, i: (b * tpb + i, 0)),
                  pl.BlockSpec((None, tm, dp), lambda b, i: (0, b * tpb + i, 0)),
                  pl.BlockSpec((None, tm, dp), lambda b, i: (1, b * tpb + i, 0)),
                  pl.BlockSpec((None, tm, d), lambda b, i: (b, i, 0)),
                  pl.BlockSpec((None, 1, d), lambda b, i: (b, 0, 0)),
                  pl.BlockSpec((1, d), lambda b, i: (0, 0))],
        out_specs=pl.BlockSpec((None, tm, d), lambda b, i: (b, i, 0)),
        out_shape=jax.ShapeDtypeStruct((bsz, L, d), F32),
        compiler_params=_cparams("parallel", "parallel"),
        name="moe_combine",
    )(route, g, g, x1, gate2, g_final.reshape(1, d))


def _moe(h2, x1, gate2, g_final, w_group, b_group, w_router, b_router, w1_e, w3_e, w2_e):
    bsz, L, d = x1.shape
    n = bsz * L
    h2f = h2.reshape(n, h2.shape[-1])
    route, tile_counts = _router(h2f, w_group, b_group, w_router, b_router)
    dest_rec, counts = _slots(route, tile_counts)
    nb = (2 * n) // MOE_BLOCK + N_EXPERTS
    cnt = counts[:N_EXPERTS].astype(jnp.int32)
    blocks_per_e = (cnt + MOE_BLOCK - 1) // MOE_BLOCK
    ends = jnp.cumsum(blocks_per_e)
    block_e = jnp.minimum(jnp.sum(ends[None, :] <= jnp.arange(nb, dtype=jnp.int32)[:, None], axis=1),
                          N_EXPERTS - 1).astype(jnp.int32)
    n_used = ends[-1:].astype(jnp.int32)
    n_slots = nb * MOE_BLOCK
    pad_j = jnp.arange(MOE_BLOCK, dtype=jnp.int32)[None, :]
    spare = n_slots + jnp.arange(N_EXPERTS * MOE_BLOCK, dtype=jnp.int32).reshape(N_EXPERTS, MOE_BLOCK)
    first_slot = ((ends - blocks_per_e) * MOE_BLOCK)[:, None]
    is_pad = cnt[:, None] + pad_j < blocks_per_e[:, None] * MOE_BLOCK
    pad_slots = jnp.where(is_pad, first_slot + cnt[:, None] + pad_j, spare).reshape(-1)
    xs = _sc_dispatch(h2f, dest_rec[:, 0], dest_rec[:, 1], pad_slots, n_slots + N_EXPERTS * MOE_BLOCK)
    ys = _experts(xs, nb, block_e, n_used, w1_e, w3_e, w2_e)
    g = _sc_gather(ys, jnp.concatenate([dest_rec[:, 0], dest_rec[:, 1]]))
    return _combine_planes(g.reshape(2, n, g.shape[-1]), route, x1, gate2, g_final)


def kernel(x, c, ctx, c_ctx, w_mod, b_mod, g_norm1, g_norm2, w_in, b_in, w_qk_conv, b_qk_conv,
           w_h_conv, b_h_conv, hf_w1, hf_b1, hf_w2, hf_b2, hf_w3, hf_freq, h_bias, w_a, w_b, w_out,
           w_group, b_group, w_router, b_router, w1_e, w3_e, w2_e, g_final):
    assert w_mod.shape[0] == 1, "single-layer block"
    (w_mod, b_mod, g_norm1, g_norm2, w_in, b_in, w_qk_conv, b_qk_conv, w_h_conv, b_h_conv, hf_w1, hf_b1, hf_w2,
     hf_b2, hf_w3, hf_freq, h_bias, w_a, w_b, w_out, w_group, b_group, w_router, b_router, w1_e, w3_e, w2_e) = (
        t[0] for t in (w_mod, b_mod, g_norm1, g_norm2, w_in, b_in, w_qk_conv, b_qk_conv, w_h_conv, b_h_conv,
                       hf_w1, hf_b1, hf_w2, hf_b2, hf_w3, hf_freq, h_bias, w_a, w_b, w_out, w_group, b_group,
                       w_router, b_router, w1_e, w3_e, w2_e))
    bsz, L, d = x.shape
    lc = ctx.shape[1]
    seg = L // (L // GRID_W)
    chunk_c = min(lc, MLSTM_CHUNK)
    assert bsz + 1 <= 8 and lc % chunk_c == 0 and L % MLSTM_CHUNK == 0

    cond = jnp.zeros((8, d), F32).at[:bsz].set(c).at[bsz].set(c_ctx)
    mod = _adaln(cond, w_mod, b_mod).reshape(8, 6, d)
    modx = mod[:bsz]
    shift1, scale1, gate1, shift2, scale2, gate2 = (modx[:, i:i + 1] for i in range(6))
    shift1c = jnp.broadcast_to(mod[bsz, 0].reshape(1, 1, d), (bsz, 1, d))
    scale1c = jnp.broadcast_to(mod[bsz, 1].reshape(1, 1, d), (bsz, 1, d))

    w_in16 = w_in.astype(BF16)
    k_scale = jnp.full((M_WIDTH,), M_HEAD_DIM ** -0.5, F32)
    qk_scale = jnp.concatenate([jnp.ones((M_WIDTH,), F32), k_scale])
    w_gates, b_gates = w_in[:, IG0:M_COLS], b_in[IG0:M_COLS]

    hc = _norm_mod(ctx, g_norm1, shift1c, scale1c, lc)
    kc = _proj_conv_silu(hc, w_in16[:, K0:V0], b_in[K0:V0], w_qk_conv[:, M_WIDTH:], b_qk_conv[M_WIDTH:],
                         k_scale, lc, lc)
    vc = _proj_act(hc, w_in16[:, V0:O0], b_in[V0:O0], "none", BF16, lc)
    bcc, acc, arc = _gates(hc, w_gates, b_gates, chunk_c)
    zero_state = (jnp.zeros((bsz, 2, M_HEADS, M_HEAD_DIM, M_HEAD_DIM), F32),
                  jnp.zeros((bsz, 2, M_HEADS, 1, M_HEAD_DIM), F32),
                  jnp.zeros((bsz, 2, M_HEADS, 1, LANES), F32))
    _, ctx_state = _mlstm(None, (kc, 0), (vc, 0), bcc, acc, arc, zero_state, False, chunk_c)

    tm = 1024
    w_main = jnp.concatenate([w_in16[:, Q0:IG0], w_in16[:, GA0:IN_COLS]], axis=1)
    b_main = jnp.concatenate([b_in[Q0:IG0], b_in[GA0:IN_COLS]])
    _, dft_fast = _dft_factors(2 * L)
    pm, h, h_il = _proj_main(x, g_norm1, shift1, scale1, w_main, b_main, w_qk_conv, b_qk_conv, qk_scale,
                             seg, tm, dft_fast)
    bc, ac, ar = _gates(h, w_gates, b_gates, MLSTM_CHUNK)
    hdirs, _ = _mlstm((pm, PM_Q), (pm, PM_K), (pm, PM_V), bc, ac, ar, ctx_state, True, MLSTM_CHUNK)

    x0_t, s_t = _proj_hyena(h_il, w_in16[:, HY0:GA0], b_in[HY0:GA0], w_h_conv, b_h_conv, seg)
    hy = _hyena_long_conv(s_t, x0_t, h_bias, hf_w1, hf_b1, hf_w2, hf_b2, hf_w3, hf_freq)

    x1, h2 = _merge(hdirs, pm, hy, x, gate1, g_norm2, shift2, scale2,
                    w_a.astype(BF16), w_b.astype(BF16), w_out.astype(BF16))
    return _moe(h2, x1, gate2, g_final, w_group, b_group, w_router, b_router, w1_e, w3_e, w2_e)
```

```python
import functools
import math

import jax
import jax.numpy as jnp
import numpy as np
from jax import lax
from jax.experimental import pallas as pl
from jax.experimental.pallas import tpu as pltpu
from jax.experimental.pallas import tpu_sc as plsc

F32 = jnp.float32
BF16 = jnp.bfloat16

D_MODEL = 1024
GRID_W = 64
EPS = 1e-6
M_HEADS = 4
M_HEAD_DIM = 256
M_WIDTH = M_HEADS * M_HEAD_DIM
H_WIDTH = 1024
H_POS_BANDS = 16
H_FILTER_HIDDEN = 64
H_FAST_DECAY_PCT = 0.3
H_SLOW_DECAY_PCT = 1.5
H_DECAY_TARGET = 1e-2
N_GROUPS = 8
EXPERTS_PER_GROUP = 8
N_EXPERTS = N_GROUPS * EXPERTS_PER_GROUP
D_EXPERT = 512
Q0 = 0
K0 = Q0 + M_WIDTH
V0 = K0 + M_WIDTH
O0 = V0 + M_WIDTH
IG0 = O0 + M_WIDTH
FG0 = IG0 + 2 * M_HEADS
M_COLS = FG0 + 2 * M_HEADS
HY0 = M_COLS
GA0 = HY0 + 3 * H_WIDTH
GB0 = GA0 + D_MODEL
IN_COLS = GB0 + D_MODEL

LANES = 128
MLSTM_CHUNK = 512
NEG_BIG = -1e30
VMEM_LIMIT = 48 * 1024 * 1024


def _cparams(*sem):
    return pltpu.CompilerParams(dimension_semantics=sem, vmem_limit_bytes=VMEM_LIMIT)


def _adaln_kernel(c_ref, w_ref, b_ref, o_ref):
    s = c_ref[...]
    s = s * jax.nn.sigmoid(s)
    o_ref[...] = jnp.dot(s.astype(BF16), w_ref[...].astype(BF16), preferred_element_type=F32) + b_ref[...]


def _adaln(cond, w_mod, b_mod):
    n = w_mod.shape[1]
    tn = 1536
    return pl.pallas_call(
        _adaln_kernel,
        grid=(n // tn,),
        in_specs=[pl.BlockSpec((8, D_MODEL), lambda j: (0, 0)),
                  pl.BlockSpec((D_MODEL, tn), lambda j: (0, j)),
                  pl.BlockSpec((1, tn), lambda j: (0, j))],
        out_specs=pl.BlockSpec((8, tn), lambda j: (0, j)),
        out_shape=jax.ShapeDtypeStruct((8, n), F32),
        compiler_params=_cparams("arbitrary"),
        name="adaln",
    )(cond, w_mod, b_mod.reshape(1, n))


def _norm_mod_kernel(x_ref, g_ref, sh_ref, sc_ref, o_ref):
    x = x_ref[...]
    y = x * lax.rsqrt(jnp.mean(x * x, axis=-1, keepdims=True) + EPS)
    y = y * g_ref[...]
    o_ref[...] = (y * (1.0 + sc_ref[...]) + sh_ref[...]).astype(o_ref.dtype)


def _norm_mod(x, g, shift, scale, tm):
    bsz, L, d = x.shape
    return pl.pallas_call(
        _norm_mod_kernel,
        grid=(bsz, L // tm),
        in_specs=[pl.BlockSpec((None, tm, d), lambda b, i: (b, i, 0)),
                  pl.BlockSpec((1, d), lambda b, i: (0, 0)),
                  pl.BlockSpec((None, 1, d), lambda b, i: (b, 0, 0)),
                  pl.BlockSpec((None, 1, d), lambda b, i: (b, 0, 0))],
        out_specs=pl.BlockSpec((None, tm, d), lambda b, i: (b, i, 0)),
        out_shape=jax.ShapeDtypeStruct((bsz, L, d), BF16),
        compiler_params=_cparams("parallel", "parallel"),
        name="norm_mod",
    )(x, g.reshape(1, d), shift, scale)


def _conv3(z, wc, bc, seg):
    tm = z.shape[0]
    pos = lax.broadcasted_iota(jnp.int32, z.shape, 0) & (seg - 1)
    zp = jnp.where(pos == 0, 0.0, pltpu.roll(z, 1, 0))
    zn = jnp.where(pos == seg - 1, 0.0, pltpu.roll(z, tm - 1, 0))
    return zp * wc[0:1, :] + z * wc[1:2, :] + zn * wc[2:3, :] + bc


def _proj_act_kernel(h_ref, w_ref, b_ref, o_ref, *, act):
    z = jnp.dot(h_ref[...], w_ref[...], preferred_element_type=F32) + b_ref[...]
    if act == "sigmoid":
        z = jax.nn.sigmoid(z)
    o_ref[...] = z.astype(o_ref.dtype)


def _proj_act(h, w, b, act, out_dtype, tm, tn=512):
    bsz, L, d = h.shape
    n = w.shape[1]
    return pl.pallas_call(
        functools.partial(_proj_act_kernel, act=act),
        grid=(bsz, L // tm, n // tn),
        in_specs=[pl.BlockSpec((None, tm, d), lambda b_, i, j: (b_, i, 0)),
                  pl.BlockSpec((d, tn), lambda b_, i, j: (0, j)),
                  pl.BlockSpec((1, tn), lambda b_, i, j: (0, j))],
        out_specs=pl.BlockSpec((None, tm, tn), lambda b_, i, j: (b_, i, j)),
        out_shape=jax.ShapeDtypeStruct((bsz, L, n), out_dtype),
        compiler_params=_cparams("parallel", "parallel", "arbitrary"),
        name="proj_" + act,
    )(h, w, b.reshape(1, n))


def _proj_conv_silu_kernel(h_ref, w_ref, b_ref, wc_ref, bc_ref, cs_ref, o_ref, *, seg):
    z = jnp.dot(h_ref[...], w_ref[...], preferred_element_type=F32) + b_ref[...]
    y = _conv3(z, wc_ref[...], bc_ref[...], seg)
    y = y * jax.nn.sigmoid(y)
    o_ref[...] = (y * cs_ref[...]).astype(o_ref.dtype)


def _proj_conv_silu(h, w, b, wc, bc, colscale, seg, tm, tn=512):
    bsz, L, d = h.shape
    n = w.shape[1]
    col = lambda b_, i, j: (0, j)
    return pl.pallas_call(
        functools.partial(_proj_conv_silu_kernel, seg=seg),
        grid=(bsz, L // tm, n // tn),
        in_specs=[pl.BlockSpec((None, tm, d), lambda b_, i, j: (b_, i, 0)),
                  pl.BlockSpec((d, tn), col),
                  pl.BlockSpec((1, tn), col),
                  pl.BlockSpec((3, tn), col),
                  pl.BlockSpec((1, tn), col),
                  pl.BlockSpec((1, tn), col)],
        out_specs=pl.BlockSpec((None, tm, tn), lambda b_, i, j: (b_, i, j)),
        out_shape=jax.ShapeDtypeStruct((bsz, L, n), BF16),
        compiler_params=_cparams("parallel", "parallel", "arbitrary"),
        name="proj_conv_silu",
    )(h, w, b.reshape(1, n), wc, bc.reshape(1, n), colscale.reshape(1, n))


PROJ_TN = 1024
PROJ_SUB = 512
PM_Q, PM_K, PM_V, PM_O, PM_GA, PM_GB = range(6)


def _proj_main_kernel(x_ref, g_ref, sh_ref, sc_ref, w_ref, b_ref, wc_ref, bc_ref, cs_ref,
                      o_ref, h_ref, hi_hbm, hp_sc, sem, *, seg):
    b, i, j = pl.program_id(0), pl.program_id(1), pl.program_id(2)
    n2, jt = hi_hbm.shape[2], hi_hbm.shape[3]

    def interleave_copy(jj):
        return pltpu.make_async_copy(hp_sc.at[pl.ds(jj * n2, n2)], hi_hbm.at[b, i, :, jj, :], sem)

    @pl.when(j == 0)
    def _():
        x = x_ref[...]
        y = x * lax.rsqrt(jnp.mean(x * x, axis=-1, keepdims=True) + EPS) * g_ref[...]
        y = y * (1.0 + sc_ref[...]) + sh_ref[...]
        h_ref[...] = y.astype(h_ref.dtype)
        hp_sc[...] = _pack_bf16_pairs(y)
        for jj in range(jt):
            interleave_copy(jj).start()

    @pl.when(j == pl.num_programs(2) - 1)
    def _():
        for jj in range(jt):
            interleave_copy(jj).wait()

    def run(epilogue):
        for c in range(PROJ_TN // PROJ_SUB):
            sl = slice(c * PROJ_SUB, (c + 1) * PROJ_SUB)
            z = jnp.dot(h_ref[...], w_ref[:, sl], preferred_element_type=F32) + b_ref[:, sl]
            o_ref[:, sl] = epilogue(z, sl).astype(o_ref.dtype)

    def conv_silu(z, sl):
        y = _conv3(z, wc_ref[:, sl], bc_ref[:, sl], seg)
        return (y * jax.nn.sigmoid(y)) * cs_ref[:, sl]

    @pl.when(j <= PM_K)
    def _():
        run(conv_silu)

    @pl.when(j == PM_V)
    def _():
        run(lambda z, sl: z)

    @pl.when(j >= PM_O)
    def _():
        run(lambda z, sl: jax.nn.sigmoid(z))


def _proj_main(x, g, shift, scale, w, b, wc, bc, colscale, seg, tm, n2):
    bsz, L, d = x.shape
    n = w.shape[1]
    jt = tm // n2
    qk = lambda b_, i, j: (0, jnp.minimum(j, PM_K))
    row = pl.BlockSpec((None, tm, d), lambda b_, i, j: (b_, i, 0))
    bvec = pl.BlockSpec((None, 1, d), lambda b_, i, j: (b_, 0, 0))
    return pl.pallas_call(
        functools.partial(_proj_main_kernel, seg=seg),
        grid=(bsz, L // tm, n // PROJ_TN),
        in_specs=[row, pl.BlockSpec((1, d), lambda b_, i, j: (0, 0)), bvec, bvec,
                  pl.BlockSpec((d, PROJ_TN), lambda b_, i, j: (0, j)),
                  pl.BlockSpec((1, PROJ_TN), lambda b_, i, j: (0, j)),
                  pl.BlockSpec((3, PROJ_TN), qk),
                  pl.BlockSpec((1, PROJ_TN), qk),
                  pl.BlockSpec((1, PROJ_TN), qk)],
        out_specs=[pl.BlockSpec((None, tm, PROJ_TN), lambda b_, i, j: (b_, i, j)), row,
                   pl.BlockSpec(memory_space=pl.ANY)],
        out_shape=[jax.ShapeDtypeStruct((bsz, L, n), BF16), jax.ShapeDtypeStruct((bsz, L, d), BF16),
                   jax.ShapeDtypeStruct((bsz, L // tm, n2, jt, d // 2), jnp.uint32)],
        scratch_shapes=[pltpu.VMEM((tm, d // 2), jnp.uint32), pltpu.SemaphoreType.DMA(())],
        compiler_params=_cparams("parallel", "parallel", "arbitrary"),
        name="proj_main",
    )(x, g.reshape(1, d), shift, scale, w, b.reshape(1, n), wc, bc.reshape(1, -1), colscale.reshape(1, -1))


def _conv3_interleaved(z, wc, bc, seg, jt):
    grp = seg * jt
    pad = jnp.zeros((jt, z.shape[1]), z.dtype)
    prev, nxt = [], []
    for g0 in range(0, z.shape[0], grp):
        zg = z[g0:g0 + grp]
        prev += [pad, zg[:grp - jt]]
        nxt += [zg[jt:], pad]
    zp = jnp.concatenate(prev, axis=0)
    zn = jnp.concatenate(nxt, axis=0)
    return zp * wc[0:1, :] + z * wc[1:2, :] + zn * wc[2:3, :] + bc


def _proj_hyena_kernel(h_ref, w0_ref, w1_ref, w2_ref, b_ref, wc_ref, bc_ref, x0_ref, s_ref, *, seg):
    n2, jt = s_ref.shape[0], s_ref.shape[1]
    h = _unpack_bf16_pairs(h_ref[...].reshape(n2 * jt, h_ref.shape[2]))
    us = []
    for g, w_ref in enumerate((w0_ref, w1_ref, w2_ref)):
        z = jnp.dot(h, w_ref[...], preferred_element_type=F32) + b_ref[g]
        us.append(_conv3_interleaved(z, wc_ref[g], bc_ref[g], seg, jt))
    x0_ref[...] = _pack_bf16_pairs(us[0]).reshape(x0_ref.shape)
    s_ref[...] = (us[1] * us[2]).reshape(s_ref.shape)


def _proj_hyena(hi, w, b, wc, bc, seg):
    bsz, nt, n2, jt, dp = hi.shape
    d, tm = 2 * dp, n2 * jt
    L = nt * tm
    tn = DFT_C_TILE
    nblk = H_WIDTH // tn
    assert n2 % seg == 0 and (jt % 8 == 0 or nt == 1)
    b3 = b.reshape(3, 1, H_WIDTH)
    wc3 = wc.reshape(3, 3, H_WIDTH).transpose(1, 0, 2)
    bc3 = bc.reshape(3, 1, H_WIDTH)
    return pl.pallas_call(
        functools.partial(_proj_hyena_kernel, seg=seg),
        grid=(bsz, nt, nblk),
        in_specs=[pl.BlockSpec((None, None, n2, jt, dp), lambda b_, i, j: (b_, i, 0, 0, 0)),
                  pl.BlockSpec((d, tn), lambda b_, i, j: (0, j)),
                  pl.BlockSpec((d, tn), lambda b_, i, j: (0, nblk + j)),
                  pl.BlockSpec((d, tn), lambda b_, i, j: (0, 2 * nblk + j)),
                  pl.BlockSpec((3, 1, tn), lambda b_, i, j: (0, 0, j)),
                  pl.BlockSpec((3, 3, tn), lambda b_, i, j: (0, 0, j)),
                  pl.BlockSpec((3, 1, tn), lambda b_, i, j: (0, 0, j))],
        out_specs=[pl.BlockSpec((None, n2, jt, tn // 2), lambda b_, i, j: (b_, 0, i, j)),
                   pl.BlockSpec((None, n2, jt, tn), lambda b_, i, j: (b_, 0, i, j))],
        out_shape=[jax.ShapeDtypeStruct((bsz, n2, L // n2, H_WIDTH // 2), jnp.uint32),
                   jax.ShapeDtypeStruct((bsz, n2, L // n2, H_WIDTH), F32)],
        compiler_params=_cparams("parallel", "parallel", "arbitrary"),
        name="proj_hyena",
    )(hi, w, w, w, b3, wc3, bc3)


N_GATES = 4 * M_HEADS


def _split3(x):
    hi = x.astype(BF16)
    r1 = x - hi.astype(F32)
    mid = r1.astype(BF16)
    lo = (r1 - mid.astype(F32)).astype(BF16)
    return hi, mid, lo


def _log_sigmoid(x):
    return jnp.minimum(x, 0.0) - jnp.log1p(jnp.exp(-jnp.abs(x)))


def _gates_kernel(h_ref, w_ref, wt_ref, b_ref, bt_ref, bc_ref, ac_ref, ar_ref):
    h = h_ref[...]
    t = h.shape[0]
    z = jnp.dot(h, w_ref[...], preferred_element_type=F32) + b_ref[...]
    zt = lax.dot_general(wt_ref[...], h, (((1,), (1,)), ((), ())),
                         preferred_element_type=F32) + bt_ref[...]
    r = lax.broadcasted_iota(jnp.int32, (t, t), 0)
    c = lax.broadcasted_iota(jnp.int32, (t, t), 1)
    lower = (r >= c).astype(BF16)
    upper = (r <= c).astype(BF16)
    g8 = FG_LANE0

    lf = _log_sigmoid(z)
    lane = lax.broadcasted_iota(jnp.int32, z.shape, 1)
    is_fg = (lane >= g8) & (lane < 2 * g8)
    terms = [jnp.where(is_fg, p.astype(F32), 0.0) for p in _split3(lf)]
    packed = terms[0] + pltpu.roll(terms[1], 2 * g8, 1) + pltpu.roll(terms[2], 4 * g8, 1)
    cfp = jnp.dot(lower, packed.astype(BF16), preferred_element_type=F32)
    cf = cfp + pltpu.roll(cfp, LANES - 2 * g8, 1) + pltpu.roll(cfp, LANES - 4 * g8, 1)
    cb = cf[t - 1:t, :] - cf + lf
    bc = jnp.where(lane < g8 + M_HEADS, cf, cb)
    bc = pltpu.roll(bc, LANES - g8, 1)
    bc_ref[...] = bc
    ac_ref[...] = z - bc

    lft = _log_sigmoid(zt[g8:, :])
    stacked = jnp.concatenate([p.astype(F32) for p in _split3(lft)] + [jnp.zeros_like(lft)], axis=0)
    cft3 = jnp.dot(stacked.astype(BF16), upper, preferred_element_type=F32)
    cft = cft3[0:g8] + cft3[g8:2 * g8] + cft3[2 * g8:3 * g8]
    cbt = cft[:, t - 1:t] - cft + lft
    row = lax.broadcasted_iota(jnp.int32, cft.shape, 0)
    ar_ref[...] = zt[:g8, :] - jnp.where(row < M_HEADS, cft, cbt)


FG_LANE0 = 2 * M_HEADS


def _gates(h, w_g, b_g, chunk):
    bsz, L, d = h.shape
    w_pad = jnp.zeros((d, LANES), F32).at[:, :N_GATES].set(w_g).astype(BF16)
    b_pad = jnp.zeros((1, LANES), F32).at[0, :N_GATES].set(b_g)
    wt = w_g.T.astype(BF16)
    bt = b_g.reshape(N_GATES, 1)
    tok = pl.BlockSpec((None, chunk, LANES), lambda b_, i: (b_, i, 0))
    return pl.pallas_call(
        _gates_kernel,
        grid=(bsz, L // chunk),
        in_specs=[pl.BlockSpec((None, chunk, d), lambda b_, i: (b_, i, 0)),
                  pl.BlockSpec((d, LANES), lambda b_, i: (0, 0)),
                  pl.BlockSpec((N_GATES, d), lambda b_, i: (0, 0)),
                  pl.BlockSpec((1, LANES), lambda b_, i: (0, 0)),
                  pl.BlockSpec((N_GATES, 1), lambda b_, i: (0, 0))],
        out_specs=[tok, tok, pl.BlockSpec((None, FG_LANE0, chunk), lambda b_, i: (b_, 0, i))],
        out_shape=[jax.ShapeDtypeStruct((bsz, L, LANES), F32),
                   jax.ShapeDtypeStruct((bsz, L, LANES), F32),
                   jax.ShapeDtypeStruct((bsz, FG_LANE0, L), F32)],
        compiler_params=_cparams("parallel", "parallel"),
        name="mlstm_gates",
    )(h, w_pad, wt, b_pad, bt)


def _mlstm_kernel(*refs, emit_h, n_chunks):
    if emit_h:
        (q_ref, k_ref, v_ref, bc_ref, ac_ref, ar_ref, c0_ref, n0_ref, m0_ref,
         h_ref, cf_ref, nf_ref, mf_ref, c_sc, n_sc, m_sc) = refs
    else:
        (k_ref, v_ref, bc_ref, ac_ref, ar_ref, c0_ref, n0_ref, m0_ref,
         cf_ref, nf_ref, mf_ref, c_sc, n_sc, m_sc) = refs
    d = pl.program_id(1)
    j = pl.program_id(2)
    fwd = d == 0
    t = k_ref.shape[0]
    dh = M_HEAD_DIM

    @pl.when(j == 0)
    def _():
        c_sc[...] = c0_ref[...]
        n_sc[...] = n0_ref[...]
        m_sc[...] = m0_ref[...]

    r = lax.broadcasted_iota(jnp.int32, (t, t), 0)
    c = lax.broadcasted_iota(jnp.int32, (t, t), 1)
    causal = jnp.where(fwd, r - c, c - r) >= 0
    bc_all = bc_ref[...]
    ac_all = ac_ref[...]
    ar_all = ar_ref[...]
    for hd in range(M_HEADS):
        sl = slice(hd * dh, (hd + 1) * dh)
        bc = jnp.where(fwd, bc_all[:, hd:hd + 1], bc_all[:, M_HEADS + hd:M_HEADS + hd + 1])
        ac = jnp.where(fwd, ac_all[:, hd:hd + 1], ac_all[:, M_HEADS + hd:M_HEADS + hd + 1])
        ar = jnp.where(fwd, ar_all[hd:hd + 1, :], ar_all[M_HEADS + hd:M_HEADS + hd + 1, :])
        b_tot = jnp.where(fwd, bc[t - 1:t, :], bc[0:1, :])
        m_prev = m_sc[hd][:, 0:1]
        k_h = k_ref[:, sl]
        v_h = v_ref[:, sl]
        if emit_h:
            q_h = q_ref[:, sl]
            dm = jnp.where(causal, bc + ar, NEG_BIG)
            inter = bc + m_prev
            m_t = jnp.maximum(inter, jnp.max(dm, axis=1, keepdims=True))
            qk = lax.dot_general(q_h, k_h, (((1,), (1,)), ((), ())), preferred_element_type=F32)
            s = qk * jnp.exp(dm - m_t)
            carry = jnp.exp(inter - m_t)
            num = (jnp.dot(s.astype(BF16), v_h, preferred_element_type=F32)
                   + carry * jnp.dot(q_h, c_sc[hd].astype(BF16), preferred_element_type=F32))
            den = (jnp.sum(s, axis=1, keepdims=True)
                   + carry * jnp.sum(q_h.astype(F32) * n_sc[hd], axis=1, keepdims=True))
            h_ref[:, sl] = (num / jnp.maximum(jnp.abs(den), jnp.exp(-m_t))).astype(h_ref.dtype)
        g = b_tot + ac
        m_new = jnp.maximum(b_tot + m_prev, jnp.max(g, axis=0, keepdims=True))
        wgt = jnp.exp(g - m_new)
        decay = jnp.exp(b_tot + m_prev - m_new)
        kw = k_h.astype(F32) * wgt
        c_sc[hd] = decay * c_sc[hd] + lax.dot_general(kw.astype(BF16), v_h, (((0,), (0,)), ((), ())),
                                                      preferred_element_type=F32)
        n_sc[hd] = decay * n_sc[hd] + jnp.sum(kw, axis=0, keepdims=True)
        m_sc[hd] = jnp.broadcast_to(m_new, (1, LANES))

    @pl.when(j == n_chunks - 1)
    def _():
        cf_ref[...] = c_sc[...]
        nf_ref[...] = n_sc[...]
        mf_ref[...] = m_sc[...]


def _mlstm(q, k, v, bc, ac, ar, state, emit_h, t):
    bsz, L, _ = k[0].shape
    nc = L // t
    seq = lambda b_, d, j: (b_, j + d * (nc - 1 - 2 * j), 0)
    st = lambda b_, d, j: (b_, d, 0, 0, 0)

    def tok(col):
        return pl.BlockSpec((None, t, M_WIDTH), lambda b_, d, j: (b_, j + d * (nc - 1 - 2 * j), col))

    gate_spec = pl.BlockSpec((None, t, LANES), seq)
    ar_spec = pl.BlockSpec((None, FG_LANE0, t), lambda b_, d, j: (b_, 0, j + d * (nc - 1 - 2 * j)))
    c_spec = pl.BlockSpec((None, None, M_HEADS, M_HEAD_DIM, M_HEAD_DIM), st)
    n_spec = pl.BlockSpec((None, None, M_HEADS, 1, M_HEAD_DIM), st)
    m_spec = pl.BlockSpec((None, None, M_HEADS, 1, LANES), st)
    state_shapes = [jax.ShapeDtypeStruct((bsz, 2, M_HEADS, M_HEAD_DIM, M_HEAD_DIM), F32),
                    jax.ShapeDtypeStruct((bsz, 2, M_HEADS, 1, M_HEAD_DIM), F32),
                    jax.ShapeDtypeStruct((bsz, 2, M_HEADS, 1, LANES), F32)]
    in_specs = [tok(k[1]), tok(v[1]), gate_spec, gate_spec, ar_spec, c_spec, n_spec, m_spec]
    args = [k[0], v[0], bc, ac, ar, *state]
    out_specs = [c_spec, n_spec, m_spec]
    out_shape = list(state_shapes)
    if emit_h:
        in_specs = [tok(q[1])] + in_specs
        args = [q[0]] + args
        out_specs = [pl.BlockSpec((None, None, t, M_WIDTH),
                                  lambda b_, d, j: (d, b_, j + d * (nc - 1 - 2 * j), 0))] + out_specs
        out_shape = [jax.ShapeDtypeStruct((2, bsz, L, M_WIDTH), BF16)] + out_shape
    outs = pl.pallas_call(
        functools.partial(_mlstm_kernel, emit_h=emit_h, n_chunks=nc),
        grid=(bsz, 2, nc),
        in_specs=in_specs,
        out_specs=out_specs,
        out_shape=out_shape,
        scratch_shapes=[pltpu.VMEM((M_HEADS, M_HEAD_DIM, M_HEAD_DIM), F32),
                        pltpu.VMEM((M_HEADS, 1, M_HEAD_DIM), F32),
                        pltpu.VMEM((M_HEADS, 1, LANES), F32)],
        compiler_params=_cparams("parallel", "parallel", "arbitrary"),
        name="mlstm" if emit_h else "mlstm_state",
    )(*args)
    if emit_h:
        return outs[0], tuple(outs[1:])
    return None, tuple(outs)


DFT_M_TILE = 8
DFT_C_TILE = 1024
DFT_INNER_C_TILE = 512
FEAT_ROWS = 16


def _filter_outer_kernel(bands_ref, w1t_ref, b1_ref, w2t_ref, b2_ref, w3p_ref, w3f_ref, fr_ref, dl_ref, l_ref,
                         a_ref, ss_ref, *, L, n1, n2):
    i = pl.program_id(0)
    h = n1 // 2
    cols = DFT_M_TILE * h

    def positions(shape, axis, side):
        q = lax.broadcasted_iota(jnp.int32, shape, axis)
        mm, jj = q // h, q % h
        n = n2 * (jj + side * h) + i * DFT_M_TILE + mm
        return n, jnp.where(n < L, n, 2 * L - n).astype(F32)

    taps = []
    sumsq = jnp.zeros((1, a_ref.shape[-1]), F32)
    for side, w3_ref in ((0, w3p_ref), (1, w3f_ref)):
        _, p_row = positions((1, cols), 1, side)
        t_row = p_row / float(max(L - 1, 1))
        ang = ((2 * math.pi / L) * p_row) * bands_ref[...]
        row = lax.broadcasted_iota(jnp.int32, (FEAT_ROWS, cols), 0)
        feats = jnp.concatenate([jnp.where(row == 0, t_row, 0.0), jnp.cos(ang), -jnp.sin(ang)], axis=0)
        fr = fr_ref[...]
        hid = jnp.sin(fr * (jnp.dot(w1t_ref[...], feats.astype(BF16), preferred_element_type=F32) + b1_ref[...]))
        hid = jnp.sin(fr * (jnp.dot(w2t_ref[...], hid.astype(BF16), preferred_element_type=F32) + b2_ref[...]))
        filt = lax.dot_general(hid.astype(BF16), w3_ref[...], (((0,), (0,)), ((), ())),
                               preferred_element_type=F32)
        n_col, p_col = positions((cols, 1), 0, side)
        t_col = p_col / float(max(L - 1, 1))
        kern = filt * jnp.exp(-t_col * jnp.abs(dl_ref[...]))
        kern = jnp.where(n_col == L, 0.0, kern)
        sumsq = sumsq + jnp.sum(kern * kern, axis=0, keepdims=True)
        taps.append(kern)

    for mm in range(DFT_M_TILE):
        x = jnp.concatenate([taps[0][mm * h:(mm + 1) * h], taps[1][mm * h:(mm + 1) * h]], axis=0)
        out = jnp.dot(l_ref[...], x.astype(BF16), preferred_element_type=F32)
        a_ref[0, :, mm, :] = out[:n1]
        a_ref[1, :, mm, :] = out[n1:]

    @pl.when(i == 0)
    def _():
        ss_ref[...] = jnp.zeros_like(ss_ref)

    ss_ref[...] += sumsq


def _filter_outer(L, n1, n2, fwd_r, w1, b1, w2, b2, w3, freq):
    hid = H_FILTER_HIDDEN
    bands = jnp.linspace(1e-4, H_POS_BANDS - 1, H_POS_BANDS, dtype=F32).reshape(H_POS_BANDS, 1)
    w1t = jnp.zeros((hid, 3 * FEAT_ROWS), F32)
    w1t = w1t.at[:, 0].set(w1[0]).at[:, FEAT_ROWS:2 * FEAT_ROWS].set(w1[1:1 + H_POS_BANDS].T)
    w1t = w1t.at[:, 2 * FEAT_ROWS:].set(w1[1 + H_POS_BANDS:].T).astype(BF16)
    w3h = w3.astype(BF16)
    max_decay = math.log(H_DECAY_TARGET) / H_FAST_DECAY_PCT
    min_decay = math.log(H_DECAY_TARGET) / H_SLOW_DECAY_PCT
    deltas = jnp.linspace(min_decay, max_decay, H_WIDTH, dtype=F32).reshape(1, H_WIDTH)
    col = lambda v: v.reshape(hid, 1)
    full = lambda a: pl.BlockSpec(a.shape, lambda i: (0,) * a.ndim)
    args = [bands, w1t, col(b1), w2.T.astype(BF16), col(b2)]
    return pl.pallas_call(
        functools.partial(_filter_outer_kernel, L=L, n1=n1, n2=n2),
        grid=(n2 // DFT_M_TILE,),
        in_specs=[full(a) for a in args]
        + [pl.BlockSpec((hid, H_WIDTH), lambda i: (0, 0)), pl.BlockSpec((hid, H_WIDTH), lambda i: (0, 1)),
           full(col(freq)), full(deltas), full(fwd_r)],
        out_specs=[pl.BlockSpec((2, n1, DFT_M_TILE, H_WIDTH), lambda i: (0, 0, i, 0)),
                   pl.BlockSpec((1, H_WIDTH), lambda i: (0, 0))],
        out_shape=[jax.ShapeDtypeStruct((2, n1, n2, H_WIDTH), F32),
                   jax.ShapeDtypeStruct((1, H_WIDTH), F32)],
        compiler_params=_cparams("arbitrary"),
        name="hyena_filter_outer",
    )(*args, w3h, w3h, col(freq), deltas, fwd_r)


def _dft_factors(n):
    lg = int(round(math.log2(n)))
    n1 = 1 << ((lg + 1) // 2)
    return n1, n // n1


def _dft_outer_matrices(n1):
    k = np.arange(n1)[:, None]
    n = np.arange(n1)[None, :]
    ang = 2.0 * np.pi * ((k * n) % n1) / n1
    cr, ci = np.cos(ang), -np.sin(ang)
    h = n1 // 2
    fwd_c = np.block([[cr[:, :h], -ci[:, :h]], [ci[:, :h], cr[:, :h]]])
    fwd_r = np.concatenate([cr, ci], axis=0)
    ir, ii = cr[:h, :], -ci[:h, :]
    inv = np.block([[ir, -ii], [ii, ir]])
    return (jnp.asarray(fwd_c, F32).astype(BF16), jnp.asarray(fwd_r, F32).astype(BF16),
            jnp.asarray(inv, F32).astype(BF16))


def _dft_inner_matrices(n1, n2):
    n = n1 * n2
    k2 = np.arange(n2)[:, None]
    m = np.arange(n2)[None, :]
    ang = 2.0 * np.pi * ((k2 * m) % n2) / n2
    fr, fi = np.cos(ang), -np.sin(ang)
    f = np.block([[fr, -fi], [fi, fr]])
    k1 = jnp.arange(n1, dtype=jnp.int32)[:, None]
    tw_ang = ((jnp.arange(n2, dtype=jnp.int32)[None, :] * k1) % n).astype(F32) * (2.0 * math.pi / n)
    rep = lambda t: jnp.broadcast_to(t[:, :, None], (n1, n2, LANES))
    return (jnp.asarray(f, F32).astype(BF16), jnp.asarray(f.T, F32).astype(BF16),
            rep(jnp.cos(tw_ang)), rep(-jnp.sin(tw_ang)))


def _outer_fwd_kernel(l_ref, s_ref, a_ref):
    n1 = a_ref.shape[1]
    for mm in range(s_ref.shape[1]):
        x = jnp.concatenate([s_ref[0, mm], s_ref[1, mm]], axis=0).astype(BF16)
        out = jnp.dot(l_ref[...], x, preferred_element_type=F32)
        a_ref[0, :, mm, :] = out[:n1]
        a_ref[1, :, mm, :] = out[n1:]


def _outer_fwd(lmat, s_t):
    _, n2, n1h, c = s_t.shape
    n1 = 2 * n1h
    tc = min(DFT_C_TILE, c)
    return pl.pallas_call(
        _outer_fwd_kernel,
        grid=(n2 // DFT_M_TILE, c // tc),
        in_specs=[pl.BlockSpec(lmat.shape, lambda m, j: (0, 0)),
                  pl.BlockSpec((2, DFT_M_TILE, n1h, tc), lambda m, j: (0, m, 0, j))],
        out_specs=pl.BlockSpec((2, n1, DFT_M_TILE, tc), lambda m, j: (0, 0, m, j)),
        out_shape=jax.ShapeDtypeStruct((2, n1, n2, c), F32),
        compiler_params=_cparams("parallel", "parallel"),
        name="dft_outer_fwd",
    )(lmat, s_t)


def _outer_inv_kernel(l_ref, b_ref, s_ref, x0_ref, ysc_ref, hb_ref, o_ref):
    n1h = s_ref.shape[2]
    for mm in range(b_ref.shape[1]):
        y = jnp.concatenate([b_ref[0, mm], b_ref[1, mm]], axis=0).astype(BF16)
        out = jnp.dot(l_ref[...], y, preferred_element_type=F32)
        for b in range(2):
            conv = out[b * n1h:(b + 1) * n1h]
            x0 = _unpack_bf16_pairs(x0_ref[b, mm]).astype(F32)
            hy = x0 * (conv * ysc_ref[...] + hb_ref[...] * s_ref[b, mm])
            o_ref[b, :, mm, :] = _pack_bf16_pairs(hy)


def _outer_inv(lmat, b_t, s_t, x0_t, yscale, h_bias):
    _, n2, n1, c = b_t.shape
    n1h = n1 // 2
    tc = min(DFT_C_TILE, c)
    vec = pl.BlockSpec((1, tc), lambda m, j: (0, j))
    hy = pl.pallas_call(
        _outer_inv_kernel,
        grid=(n2 // DFT_M_TILE, c // tc),
        in_specs=[pl.BlockSpec(lmat.shape, lambda m, j: (0, 0)),
                  pl.BlockSpec((2, DFT_M_TILE, n1, tc), lambda m, j: (0, m, 0, j)),
                  pl.BlockSpec((2, DFT_M_TILE, n1h, tc), lambda m, j: (0, m, 0, j)),
                  pl.BlockSpec((2, DFT_M_TILE, n1h, tc // 2), lambda m, j: (0, m, 0, j)),
                  vec, vec],
        out_specs=pl.BlockSpec((2, n1h, DFT_M_TILE, tc // 2), lambda m, j: (0, 0, m, j)),
        out_shape=jax.ShapeDtypeStruct((2, n1h, n2, c // 2), jnp.uint32),
        compiler_params=_cparams("parallel", "parallel"),
        name="dft_outer_inv",
    )(lmat, b_t, s_t, x0_t, yscale, h_bias.reshape(1, c))
    return hy.reshape(2, n1h * n2, c // 2)


DFT_K_TILE = 8


def _twiddled_inner_dft(f_ref, twr_ref, twi_ref, a_ref, kk):
    n2, c = a_ref.shape[2], a_ref.shape[3]
    twr = jnp.tile(twr_ref[kk], (1, c // LANES))
    twi = jnp.tile(twi_ref[kk], (1, c // LANES))
    ar, ai = a_ref[0, kk], a_ref[1, kk]
    a = jnp.concatenate([(ar * twr - ai * twi).astype(BF16), (ar * twi + ai * twr).astype(BF16)], axis=0)
    x = jnp.dot(f_ref[...], a, preferred_element_type=F32)
    return x[:n2], x[n2:], twr, twi


def _inner_fwd_kernel(f_ref, twr_ref, twi_ref, a_ref, o_ref):
    for kk in range(a_ref.shape[1]):
        xr, xi, _, _ = _twiddled_inner_dft(f_ref, twr_ref, twi_ref, a_ref, kk)
        o_ref[0, kk] = xr.astype(o_ref.dtype)
        o_ref[1, kk] = xi.astype(o_ref.dtype)


def _inner_specs(n1, n2, c):
    tc = min(DFT_INNER_C_TILE, c)
    kt = min(DFT_K_TILE, n1)
    blk = pl.BlockSpec((2, kt, n2, tc), lambda k, j: (0, k, 0, j))
    mat = pl.BlockSpec((2 * n2, 2 * n2), lambda k, j: (0, 0))
    tw = pl.BlockSpec((kt, n2, LANES), lambda k, j: (k, 0, 0))
    return blk, mat, tw, (n1 // kt, c // tc), kt, tc


def _inner_fwd(f, twr, twi, a):
    _, n1, n2, c = a.shape
    blk, mat, tw, grid, _, _ = _inner_specs(n1, n2, c)
    return pl.pallas_call(
        _inner_fwd_kernel,
        grid=grid,
        in_specs=[mat, tw, tw, blk],
        out_specs=blk,
        out_shape=jax.ShapeDtypeStruct((2, n1, n2, c), BF16),
        compiler_params=_cparams("parallel", "parallel"),
        name="dft_inner_filter",
    )(f, twr, twi, a)


def _inner_conv_kernel(f_ref, ft_ref, twr_ref, twi_ref, a_ref, k_ref, o_ref):
    n2 = a_ref.shape[2]
    for kk in range(a_ref.shape[1]):
        xr, xi, twr, twi = _twiddled_inner_dft(f_ref, twr_ref, twi_ref, a_ref, kk)
        kr, ki = k_ref[0, kk].astype(F32), k_ref[1, kk].astype(F32)
        yr = xr * kr - xi * ki
        yi = xr * ki + xi * kr
        y = jnp.concatenate([yr.astype(BF16), yi.astype(BF16)], axis=0)
        b = jnp.dot(ft_ref[...], y, preferred_element_type=F32)
        br, bi = b[:n2], b[n2:]
        o_ref[0, :, kk, :] = br * twr + bi * twi
        o_ref[1, :, kk, :] = bi * twr - br * twi


def _inner_conv(f, ft, twr, twi, a, kf):
    _, n1, n2, c = a.shape
    blk, mat, tw, grid, kt, tc = _inner_specs(n1, n2, c)
    return pl.pallas_call(
        _inner_conv_kernel,
        grid=grid,
        in_specs=[mat, mat, tw, tw, blk, blk],
        out_specs=pl.BlockSpec((2, n2, kt, tc), lambda k, j: (0, 0, k, j)),
        out_shape=jax.ShapeDtypeStruct((2, n2, n1, c), F32),
        compiler_params=_cparams("parallel", "parallel"),
        name="dft_inner_conv",
    )(f, ft, twr, twi, a, kf)


def _hyena_long_conv(s_t, x0_t, h_bias, w1, b1, w2, b2, w3, freq):
    bsz, n2, n1h, c = s_t.shape
    assert bsz == 2
    n1 = 2 * n1h
    L = n1h * n2
    fwd_c, fwd_r, inv = _dft_outer_matrices(n1)
    f, ft, twr, twi = _dft_inner_matrices(n1, n2)
    af, sumsq = _filter_outer(L, n1, n2, fwd_r, w1, b1, w2, b2, w3, freq)
    kf = _inner_fwd(f, twr, twi, af)
    a = _outer_fwd(fwd_c, s_t)
    b_t = _inner_conv(f, ft, twr, twi, a, kf)
    yscale = lax.rsqrt(sumsq + EPS) * (1.0 / (2 * L))
    return _outer_inv(inv, b_t, s_t, x0_t, yscale, h_bias)


def _pack_bf16_pairs(x):
    half = x.shape[1] // 2
    lo = pltpu.bitcast(x[:, :half].astype(BF16).astype(F32), jnp.uint32) >> 16
    hi = pltpu.bitcast(x[:, half:].astype(BF16).astype(F32), jnp.uint32) & jnp.uint32(0xFFFF0000)
    return lo | hi


def _unpack_bf16_pairs(p):
    lo = pltpu.bitcast(p << 16, F32).astype(BF16)
    hi = pltpu.bitcast(p & jnp.uint32(0xFFFF0000), F32).astype(BF16)
    return jnp.concatenate([lo, hi], axis=1)


def _merge_kernel(hf_ref, hb_ref, o_ref, hy_ref, ga_ref, gb_ref, x_ref,
                  gate_ref, g2_ref, sh_ref, sc_ref, wa_ref, wb_ref, wo_ref, x1_ref, h2_ref):
    a = o_ref[...].astype(F32) * (hf_ref[...].astype(F32) + hb_ref[...].astype(F32))
    half = DFT_C_TILE // 2
    hy = jnp.concatenate([_unpack_bf16_pairs(hy_ref[:, c * half:(c + 1) * half])
                          for c in range(hy_ref.shape[1] // half)], axis=1)
    pa = jnp.dot(a.astype(BF16), wa_ref[...], preferred_element_type=F32)
    pb = jnp.dot(hy, wb_ref[...], preferred_element_type=F32)
    mix = ga_ref[...].astype(F32) * pa + gb_ref[...].astype(F32) * pb
    out = jnp.dot(mix.astype(BF16), wo_ref[...], preferred_element_type=F32)
    x1 = x_ref[...] + gate_ref[...] * out
    x1_ref[...] = x1
    y = x1 * lax.rsqrt(jnp.mean(x1 * x1, axis=-1, keepdims=True) + EPS) * g2_ref[...]
    h2_ref[...] = _pack_bf16_pairs(y * (1.0 + sc_ref[...]) + sh_ref[...])


def _merge(hdirs, pm, hy, x, gate1, g2, shift2, scale2, w_a, w_b, w_out, tm=512):
    bsz, L, d = x.shape
    tok = pl.BlockSpec((None, tm, d), lambda b, i: (b, i, 0))

    def pm_tile(col):
        return pl.BlockSpec((None, tm, d), lambda b, i: (b, i, col))

    packed = pl.BlockSpec((None, tm, d // 2), lambda b, i: (b, i, 0))
    vec = pl.BlockSpec((1, d), lambda b, i: (0, 0))
    bvec = pl.BlockSpec((None, 1, d), lambda b, i: (b, 0, 0))
    wsp = pl.BlockSpec((d, d), lambda b, i: (0, 0), pipeline_mode=pl.Buffered(1))
    return pl.pallas_call(
        _merge_kernel,
        grid=(bsz, L // tm),
        in_specs=[pl.BlockSpec((None, None, tm, d), lambda b, i: (0, b, i, 0)),
                  pl.BlockSpec((None, None, tm, d), lambda b, i: (1, b, i, 0)),
                  pm_tile(PM_O), packed, pm_tile(PM_GA), pm_tile(PM_GB), tok,
                  bvec, vec, bvec, bvec, wsp, wsp, wsp],
        out_specs=[tok, packed],
        out_shape=[jax.ShapeDtypeStruct((bsz, L, d), F32), jax.ShapeDtypeStruct((bsz, L, d // 2), jnp.uint32)],
        compiler_params=_cparams("parallel", "parallel"),
        name="merge",
    )(hdirs, hdirs, pm, hy, pm, pm, x, gate1, g2.reshape(1, d), shift2, scale2, w_a, w_b, w_out)


MOE_BLOCK = 256
ROUTE_E1, ROUTE_E2, ROUTE_W1, ROUTE_W2 = 0, 1, 2, 3
EXP_LANE0 = N_GROUPS


def _first_lane_of_max(val, valid, lane):
    masked = jnp.where(valid, val, NEG_BIG)
    mx = jnp.max(masked, axis=1, keepdims=True)
    idx = jnp.min(jnp.where(valid & (masked == mx), lane, LANES), axis=1, keepdims=True)
    return mx, idx


MOE_TM = 1024


def _expert_onehots(rec):
    lane = lax.broadcasted_iota(jnp.int32, rec.shape, 1)
    oh1 = lane == rec[:, ROUTE_E1:ROUTE_E1 + 1].astype(jnp.int32)
    oh2 = lane == rec[:, ROUTE_E2:ROUTE_E2 + 1].astype(jnp.int32)
    return oh1, oh2


def _router_kernel(h_ref, w_ref, b_ref, r_ref, cnt_ref):
    logits = jnp.dot(_unpack_bf16_pairs(h_ref[...]), w_ref[...], preferred_element_type=F32) + b_ref[...]
    lane = lax.broadcasted_iota(jnp.int32, logits.shape, 1)
    is_g = lane < N_GROUPS
    gmax, gsel = _first_lane_of_max(logits, is_g, lane)
    gsum = jnp.sum(jnp.where(is_g, jnp.exp(logits - gmax), 0.0), axis=1, keepdims=True)
    gw = 1.0 / gsum
    lo = EXP_LANE0 + gsel * EXPERTS_PER_GROUP
    in_grp = (lane >= lo) & (lane < lo + EXPERTS_PER_GROUP)
    emax, l1 = _first_lane_of_max(logits, in_grp, lane)
    esum = jnp.sum(jnp.where(in_grp, jnp.exp(logits - emax), 0.0), axis=1, keepdims=True)
    e2max, l2 = _first_lane_of_max(logits, in_grp & (lane != l1), lane)
    v1 = 1.0 / esum
    v2 = jnp.exp(e2max - emax) / esum
    vs = v1 + v2
    w1 = gw * v1 / vs
    w2 = gw * v2 / vs
    e1 = (l1 - EXP_LANE0).astype(F32)
    e2 = (l2 - EXP_LANE0).astype(F32)
    rec = jnp.where(lane == ROUTE_E1, e1,
                    jnp.where(lane == ROUTE_E2, e2,
                              jnp.where(lane == ROUTE_W1, w1,
                                        jnp.where(lane == ROUTE_W2, w2, 0.0))))
    r_ref[...] = rec
    oh1, oh2 = _expert_onehots(rec)
    counts = jnp.sum((oh1 | oh2).astype(F32), axis=0, keepdims=True)
    cnt_ref[...] = jnp.broadcast_to(counts, cnt_ref.shape)


def _router(h2, w_group, b_group, w_router, b_router):
    n, dp = h2.shape
    d = 2 * dp
    tm = MOE_TM
    w = jnp.zeros((d, LANES), F32).at[:, :N_GROUPS].set(w_group).at[
        :, EXP_LANE0:EXP_LANE0 + N_EXPERTS].set(w_router).astype(BF16)
    b = jnp.zeros((1, LANES), F32).at[0, :N_GROUPS].set(b_group).at[
        0, EXP_LANE0:EXP_LANE0 + N_EXPERTS].set(b_router)
    return pl.pallas_call(
        _router_kernel,
        grid=(n // tm,),
        in_specs=[pl.BlockSpec((tm, dp), lambda i: (i, 0)),
                  pl.BlockSpec((d, LANES), lambda i: (0, 0)),
                  pl.BlockSpec((1, LANES), lambda i: (0, 0))],
        out_specs=[pl.BlockSpec((tm, LANES), lambda i: (i, 0)),
                   pl.BlockSpec((None, 8, LANES), lambda i: (i, 0, 0))],
        out_shape=[jax.ShapeDtypeStruct((n, LANES), F32), jax.ShapeDtypeStruct((n // tm, 8, LANES), F32)],
        compiler_params=_cparams("parallel"),
        name="moe_router",
    )(h2, w, b)


def _slots_kernel(r_ref, base_ref, dest_ref):
    rec = r_ref[...]
    tm = rec.shape[0]
    lane = lax.broadcasted_iota(jnp.int32, rec.shape, 1)
    oh1, oh2 = _expert_onehots(rec)
    r = lax.broadcasted_iota(jnp.int32, (tm, tm), 0)
    c = lax.broadcasted_iota(jnp.int32, (tm, tm), 1)
    earlier = (r > c).astype(BF16)
    rank = jnp.dot(earlier, (oh1 | oh2).astype(BF16), preferred_element_type=F32) + base_ref[0:1, :]
    d1 = jnp.sum(jnp.where(oh1, rank, 0.0), axis=1, keepdims=True)
    d2 = jnp.sum(jnp.where(oh2, rank, 0.0), axis=1, keepdims=True)
    dest_ref[...] = jnp.where(lane == 0, d1, jnp.where(lane == 1, d2, 0.0)).astype(jnp.int32)


def _slots(route, tile_counts):
    n = route.shape[0]
    tm = MOE_TM
    cnt = tile_counts[:, 0, :]
    totals = jnp.sum(cnt, axis=0)
    nblk = jnp.ceil(totals * (1.0 / MOE_BLOCK))
    first_slot = (jnp.cumsum(nblk) - nblk) * float(MOE_BLOCK)
    base = first_slot[None, :] + jnp.cumsum(cnt, axis=0) - cnt
    base = jnp.broadcast_to(base[:, None, :], tile_counts.shape)
    dest = pl.pallas_call(
        _slots_kernel,
        grid=(n // tm,),
        in_specs=[pl.BlockSpec((tm, LANES), lambda i: (i, 0)),
                  pl.BlockSpec((None, 8, LANES), lambda i: (i, 0, 0))],
        out_specs=pl.BlockSpec((tm, LANES), lambda i: (i, 0)),
        out_shape=jax.ShapeDtypeStruct((n, LANES), jnp.int32),
        compiler_params=_cparams("parallel"),
        name="moe_slots",
    )(route, base)
    return dest, totals


EXPERT_STEP_BLOCKS = 4


def _experts_kernel(be_ref, first_ref, nxt_ref, par_ref, nu_ref, x_ref, w1_hbm, w3_hbm, w2_hbm, o_ref,
                    w1f, w3f, w2f, w1b, w3b, w2b, sems):
    step = pl.program_id(0)

    def weight_copies(e, slot):
        return (pltpu.make_async_copy(w1_hbm.at[e], w1f.at[slot], sems.at[0, slot]),
                pltpu.make_async_copy(w3_hbm.at[e], w3f.at[slot], sems.at[1, slot]),
                pltpu.make_async_copy(w2_hbm.at[e], w2f.at[slot], sems.at[2, slot]))

    @pl.when(step == 0)
    def _():
        for cp in weight_copies(be_ref[0], 0):
            cp.start()

    for sub in range(EXPERT_STEP_BLOCKS):
        i = step * EXPERT_STEP_BLOCKS + sub
        rows = pl.ds(sub * MOE_BLOCK, MOE_BLOCK)

        @pl.when(first_ref[i] == 1)
        def _():
            slot = par_ref[i]

            @pl.when(nxt_ref[i] >= 0)
            def _():
                for cp in weight_copies(nxt_ref[i], 1 - slot):
                    cp.start()

            for cp in weight_copies(be_ref[i], slot):
                cp.wait()
            w1b[...] = w1f[slot].astype(BF16)
            w3b[...] = w3f[slot].astype(BF16)
            w2b[...] = w2f[slot].astype(BF16)

        @pl.when(i < nu_ref[0])
        def _():
            x = _unpack_bf16_pairs(x_ref[rows, :])
            a = jnp.dot(x, w1b[...], preferred_element_type=F32)
            b = jnp.dot(x, w3b[...], preferred_element_type=F32)
            hmid = (a * jax.nn.sigmoid(a)) * b
            o_ref[rows, :] = _pack_bf16_pairs(jnp.dot(hmid.astype(BF16), w2b[...], preferred_element_type=F32))

        @pl.when(i >= nu_ref[0])
        def _():
            o_ref[rows, :] = jnp.zeros((MOE_BLOCK, o_ref.shape[1]), o_ref.dtype)


def _experts(xs, nb, block_e, n_used, w1_e, w3_e, w2_e):
    dp = xs.shape[1]
    d, de = w1_e.shape[1], w1_e.shape[2]
    idx = jnp.arange(nb, dtype=jnp.int32)
    used = idx < n_used[0]
    first = used & ((idx == 0) | (block_e != jnp.roll(block_e, 1)))
    ordinal = jnp.cumsum(first.astype(jnp.int32)) - 1
    par = (ordinal % 2).astype(jnp.int32)
    first_pos = jnp.where(first, idx, nb)
    next_first = lax.cummin(jnp.concatenate([first_pos[1:], jnp.full((1,), nb, jnp.int32)]), reverse=True)
    nxt = jnp.where(next_first < nb, block_e[jnp.minimum(next_first, nb - 1)], -1).astype(jnp.int32)
    any_spec = pl.BlockSpec(memory_space=pl.ANY)
    assert nb % EXPERT_STEP_BLOCKS == 0
    step_rows = EXPERT_STEP_BLOCKS * MOE_BLOCK
    grid_spec = pltpu.PrefetchScalarGridSpec(
        num_scalar_prefetch=5,
        grid=(nb // EXPERT_STEP_BLOCKS,),
        in_specs=[pl.BlockSpec((step_rows, dp), lambda i, *_: (i, 0)), any_spec, any_spec, any_spec],
        out_specs=pl.BlockSpec((step_rows, dp), lambda i, *_: (i, 0)),
        scratch_shapes=[pltpu.VMEM((2, d, de), F32), pltpu.VMEM((2, d, de), F32), pltpu.VMEM((2, de, d), F32),
                        pltpu.VMEM((d, de), BF16), pltpu.VMEM((d, de), BF16), pltpu.VMEM((de, d), BF16),
                        pltpu.SemaphoreType.DMA((3, 2))],
    )
    return pl.pallas_call(
        _experts_kernel,
        grid_spec=grid_spec,
        out_shape=jax.ShapeDtypeStruct((nb * MOE_BLOCK, dp), xs.dtype),
        compiler_params=_cparams("arbitrary"),
        name="moe_experts",
    )(block_e, first.astype(jnp.int32), nxt, par, n_used, xs, w1_e, w3_e, w2_e)


SC_WINDOW = 128
SC_CORES, SC_SUBCORES = 2, 16
SC_WORKERS = SC_CORES * SC_SUBCORES


def _sc_worker_id():
    return lax.axis_index("c") * SC_SUBCORES + lax.axis_index("s")


def _sc_mesh():
    return plsc.VectorSubcoreMesh(core_axis_name="c", subcore_axis_name="s")


def _sc_dispatch(rows, dest0, dest1, pad_slots, n_rows):
    n, dv = rows.shape
    nwin, pwin = n // SC_WINDOW, pad_slots.shape[0] // SC_WINDOW
    assert n % (SC_WINDOW * SC_WORKERS) == 0 and pad_slots.shape[0] % (SC_WINDOW * SC_WORKERS) == 0
    zeros = jnp.zeros((SC_WINDOW, dv), rows.dtype)

    @pl.kernel(out_type=jax.ShapeDtypeStruct((n_rows, dv), rows.dtype), mesh=_sc_mesh(),
               scratch_types=[pltpu.VMEM((1, SC_WINDOW), jnp.int32), pltpu.VMEM((SC_WINDOW, dv), rows.dtype)],
               name="moe_dispatch_sc")
    def scatter(x_hbm, d0_hbm, d1_hbm, p_hbm, z_hbm, o_hbm, idx, buf):
        wid = _sc_worker_id()
        pltpu.sync_copy(z_hbm, buf)

        @pl.loop(0, pwin // SC_WORKERS)
        def _(t):
            w = t * SC_WORKERS + wid
            pltpu.sync_copy(p_hbm.at[pl.ds(w, 1)], idx)
            pltpu.sync_copy(buf, o_hbm.at[idx.at[0]])

        @pl.loop(0, nwin // SC_WORKERS)
        def _(t):
            w = t * SC_WORKERS + wid
            pltpu.sync_copy(x_hbm.at[pl.ds(w * SC_WINDOW, SC_WINDOW)], buf)
            for d_hbm in (d0_hbm, d1_hbm):
                pltpu.sync_copy(d_hbm.at[pl.ds(w, 1)], idx)
                pltpu.sync_copy(buf, o_hbm.at[idx.at[0]])

    return scatter(rows, dest0.reshape(nwin, SC_WINDOW), dest1.reshape(nwin, SC_WINDOW),
                   pad_slots.reshape(pwin, SC_WINDOW), zeros)


def _sc_gather(table, index):
    m = index.shape[0]
    dv = table.shape[1]
    nwin = m // SC_WINDOW
    assert m % (SC_WINDOW * SC_WORKERS) == 0

    @pl.kernel(out_type=jax.ShapeDtypeStruct((m, dv), table.dtype), mesh=_sc_mesh(),
               scratch_types=[pltpu.VMEM((1, SC_WINDOW), jnp.int32), pltpu.VMEM((SC_WINDOW, dv), table.dtype)],
               name="moe_gather_sc")
    def gather(x_hbm, i_hbm, o_hbm, idx, buf):
        wid = _sc_worker_id()

        @pl.loop(0, nwin // SC_WORKERS)
        def _(t):
            w = t * SC_WORKERS + wid
            pltpu.sync_copy(i_hbm.at[pl.ds(w, 1)], idx)
            pltpu.sync_copy(x_hbm.at[idx.at[0]], buf)
            pltpu.sync_copy(buf, o_hbm.at[pl.ds(w * SC_WINDOW, SC_WINDOW)])

    return gather(table, index.reshape(nwin, SC_WINDOW))


def _combine_planes_kernel(r_ref, ya_ref, yb_ref, x_ref, gate_ref, gf_ref, o_ref):
    rec = r_ref[...]
    y = (_unpack_bf16_pairs(ya_ref[...]).astype(F32) * rec[:, ROUTE_W1:ROUTE_W1 + 1]
         + _unpack_bf16_pairs(yb_ref[...]).astype(F32) * rec[:, ROUTE_W2:ROUTE_W2 + 1])
    x2 = x_ref[...] + gate_ref[...] * y
    o_ref[...] = x2 * lax.rsqrt(jnp.mean(x2 * x2, axis=-1, keepdims=True) + EPS) * gf_ref[...]


def _combine_planes(g, route, x1, gate2, g_final, tm=512):
    bsz, L, d = x1.shape
    tpb = L // tm
    dp = g.shape[-1]
    return pl.pallas_call(
        _combine_planes_kernel,
        grid=(bsz, tpb),
        in_specs=[pl.BlockSpec((tm, LANES), lambda b, i: (b * tpb + i, 0)),
                  pl.BlockSpec((None, tm, dp), lambda b, i: (0, b * tpb + i, 0)),
                  pl.BlockSpec((None, tm, dp), lambda b, i: (1, b * tpb + i, 0)),
                  pl.BlockSpec((None, tm, d), lambda b, i: (b, i, 0)),
                  pl.BlockSpec((None, 1, d), lambda b, i: (b, 0, 0)),
                  pl.BlockSpec((1, d), lambda b, i: (0, 0))],
        out_specs=pl.BlockSpec((None, tm, d), lambda b, i: (b, i, 0)),
        out_shape=jax.ShapeDtypeStruct((bsz, L, d), F32),
        compiler_params=_cparams("parallel", "parallel"),
        name="moe_combine",
    )(route, g, g, x1, gate2, g_final.reshape(1, d))


def _moe(h2, x1, gate2, g_final, w_group, b_group, w_router, b_router, w1_e, w3_e, w2_e):
    bsz, L, d = x1.shape
    n = bsz * L
    h2f = h2.reshape(n, h2.shape[-1])
    route, tile_counts = _router(h2f, w_group, b_group, w_router, b_router)
    dest_rec, counts = _slots(route, tile_counts)
    nb = (2 * n) // MOE_BLOCK + N_EXPERTS
    cnt = counts[:N_EXPERTS].astype(jnp.int32)
    blocks_per_e = (cnt + MOE_BLOCK - 1) // MOE_BLOCK
    ends = jnp.cumsum(blocks_per_e)
    block_e = jnp.minimum(jnp.sum(ends[None, :] <= jnp.arange(nb, dtype=jnp.int32)[:, None], axis=1),
                          N_EXPERTS - 1).astype(jnp.int32)
    n_used = ends[-1:].astype(jnp.int32)
    n_slots = nb * MOE_BLOCK
    pad_j = jnp.arange(MOE_BLOCK, dtype=jnp.int32)[None, :]
    spare = n_slots + jnp.arange(N_EXPERTS * MOE_BLOCK, dtype=jnp.int32).reshape(N_EXPERTS, MOE_BLOCK)
    first_slot = ((ends - blocks_per_e) * MOE_BLOCK)[:, None]
    is_pad = cnt[:, None] + pad_j < blocks_per_e[:, None] * MOE_BLOCK
    pad_slots = jnp.where(is_pad, first_slot + cnt[:, None] + pad_j, spare).reshape(-1)
    xs = _sc_dispatch(h2f, dest_rec[:, 0], dest_rec[:, 1], pad_slots, n_slots + N_EXPERTS * MOE_BLOCK)
    ys = _experts(xs, nb, block_e, n_used, w1_e, w3_e, w2_e)
    g = _sc_gather(ys, jnp.concatenate([dest_rec[:, 0], dest_rec[:, 1]]))
    return _combine_planes(g.reshape(2, n, g.shape[-1]), route, x1, gate2, g_final)


def kernel(x, c, ctx, c_ctx, w_mod, b_mod, g_norm1, g_norm2, w_in, b_in, w_qk_conv, b_qk_conv,
           w_h_conv, b_h_conv, hf_w1, hf_b1, hf_w2, hf_b2, hf_w3, hf_freq, h_bias, w_a, w_b, w_out,
           w_group, b_group, w_router, b_router, w1_e, w3_e, w2_e, g_final):
    assert w_mod.shape[0] == 1, "single-layer block"
    (w_mod, b_mod, g_norm1, g_norm2, w_in, b_in, w_qk_conv, b_qk_conv, w_h_conv, b_h_conv, hf_w1, hf_b1, hf_w2,
     hf_b2, hf_w3, hf_freq, h_bias, w_a, w_b, w_out, w_group, b_group, w_router, b_router, w1_e, w3_e, w2_e) = (
        t[0] for t in (w_mod, b_mod, g_norm1, g_norm2, w_in, b_in, w_qk_conv, b_qk_conv, w_h_conv, b_h_conv,
                       hf_w1, hf_b1, hf_w2, hf_b2, hf_w3, hf_freq, h_bias, w_a, w_b, w_out, w_group, b_group,
                       w_router, b_router, w1_e, w3_e, w2_e))
    bsz, L, d = x.shape
    lc = ctx.shape[1]
    seg = L // (L // GRID_W)
    chunk_c = min(lc, MLSTM_CHUNK)
    assert bsz + 1 <= 8 and lc % chunk_c == 0 and L % MLSTM_CHUNK == 0

    cond = jnp.zeros((8, d), F32).at[:bsz].set(c).at[bsz].set(c_ctx)
    mod = _adaln(cond, w_mod, b_mod).reshape(8, 6, d)
    modx = mod[:bsz]
    shift1, scale1, gate1, shift2, scale2, gate2 = (modx[:, i:i + 1] for i in range(6))
    shift1c = jnp.broadcast_to(mod[bsz, 0].reshape(1, 1, d), (bsz, 1, d))
    scale1c = jnp.broadcast_to(mod[bsz, 1].reshape(1, 1, d), (bsz, 1, d))

    w_in16 = w_in.astype(BF16)
    k_scale = jnp.full((M_WIDTH,), M_HEAD_DIM ** -0.5, F32)
    qk_scale = jnp.concatenate([jnp.ones((M_WIDTH,), F32), k_scale])
    w_gates, b_gates = w_in[:, IG0:M_COLS], b_in[IG0:M_COLS]

    hc = _norm_mod(ctx, g_norm1, shift1c, scale1c, lc)
    kc = _proj_conv_silu(hc, w_in16[:, K0:V0], b_in[K0:V0], w_qk_conv[:, M_WIDTH:], b_qk_conv[M_WIDTH:],
                         k_scale, lc, lc)
    vc = _proj_act(hc, w_in16[:, V0:O0], b_in[V0:O0], "none", BF16, lc)
    bcc, acc, arc = _gates(hc, w_gates, b_gates, chunk_c)
    zero_state = (jnp.zeros((bsz, 2, M_HEADS, M_HEAD_DIM, M_HEAD_DIM), F32),
                  jnp.zeros((bsz, 2, M_HEADS, 1, M_HEAD_DIM), F32),
                  jnp.zeros((bsz, 2, M_HEADS, 1, LANES), F32))
    _, ctx_state = _mlstm(None, (kc, 0), (vc, 0), bcc, acc, arc, zero_state, False, chunk_c)

    tm = 1024
    w_main = jnp.concatenate([w_in16[:, Q0:IG0], w_in16[:, GA0:IN_COLS]], axis=1)
    b_main = jnp.concatenate([b_in[Q0:IG0], b_in[GA0:IN_COLS]])
    _, dft_fast = _dft_factors(2 * L)
    pm, h, h_il = _proj_main(x, g_norm1, shift1, scale1, w_main, b_main, w_qk_conv, b_qk_conv, qk_scale,
                             seg, tm, dft_fast)
    bc, ac, ar = _gates(h, w_gates, b_gates, MLSTM_CHUNK)
    hdirs, _ = _mlstm((pm, PM_Q), (pm, PM_K), (pm, PM_V), bc, ac, ar, ctx_state, True, MLSTM_CHUNK)

    x0_t, s_t = _proj_hyena(h_il, w_in16[:, HY0:GA0], b_in[HY0:GA0], w_h_conv, b_h_conv, seg)
    hy = _hyena_long_conv(s_t, x0_t, h_bias, hf_w1, hf_b1, hf_w2, hf_b2, hf_w3, hf_freq)

    x1, h2 = _merge(hdirs, pm, hy, x, gate1, g_norm2, shift2, scale2,
                    w_a.astype(BF16), w_b.astype(BF16), w_out.astype(BF16))
    return _moe(h2, x1, gate2, g_final, w_group, b_group, w_router, b_router, w1_e, w3_e, w2_e)
```

```python
import functools
import math

import jax
import jax.numpy as jnp
import numpy as np
from jax import lax
from jax.experimental import pallas as pl
from jax.experimental.pallas import tpu as pltpu
from jax.experimental.pallas import tpu_sc as plsc

F32 = jnp.float32
BF16 = jnp.bfloat16

D_MODEL = 1024
GRID_W = 64
EPS = 1e-6
M_HEADS = 4
M_HEAD_DIM = 256
M_WIDTH = M_HEADS * M_HEAD_DIM
H_WIDTH = 1024
H_POS_BANDS = 16
H_FILTER_HIDDEN = 64
H_FAST_DECAY_PCT = 0.3
H_SLOW_DECAY_PCT = 1.5
H_DECAY_TARGET = 1e-2
N_GROUPS = 8
EXPERTS_PER_GROUP = 8
N_EXPERTS = N_GROUPS * EXPERTS_PER_GROUP
D_EXPERT = 512
Q0 = 0
K0 = Q0 + M_WIDTH
V0 = K0 + M_WIDTH
O0 = V0 + M_WIDTH
IG0 = O0 + M_WIDTH
FG0 = IG0 + 2 * M_HEADS
M_COLS = FG0 + 2 * M_HEADS
HY0 = M_COLS
GA0 = HY0 + 3 * H_WIDTH
GB0 = GA0 + D_MODEL
IN_COLS = GB0 + D_MODEL

LANES = 128
MLSTM_CHUNK = 512
NEG_BIG = -1e30
VMEM_LIMIT = 48 * 1024 * 1024
ROW_TILE = 1024
ADALN_TN = 1536
SMALL_TN = 512
MERGE_TM = 512
COMBINE_TM = 512


def _cparams(*sem):
    return pltpu.CompilerParams(dimension_semantics=sem, vmem_limit_bytes=VMEM_LIMIT)


def _adaln_kernel(c_ref, w_ref, b_ref, o_ref):
    s = c_ref[...]
    s = s * jax.nn.sigmoid(s)
    o_ref[...] = jnp.dot(s.astype(BF16), w_ref[...].astype(BF16), preferred_element_type=F32) + b_ref[...]


def _adaln(cond, w_mod, b_mod):
    n = w_mod.shape[1]
    tn = ADALN_TN
    return pl.pallas_call(
        _adaln_kernel,
        grid=(n // tn,),
        in_specs=[pl.BlockSpec((8, D_MODEL), lambda j: (0, 0)),
                  pl.BlockSpec((D_MODEL, tn), lambda j: (0, j)),
                  pl.BlockSpec((1, tn), lambda j: (0, j))],
        out_specs=pl.BlockSpec((8, tn), lambda j: (0, j)),
        out_shape=jax.ShapeDtypeStruct((8, n), F32),
        compiler_params=_cparams("arbitrary"),
        name="adaln",
    )(cond, w_mod, b_mod.reshape(1, n))


def _norm_mod_kernel(x_ref, g_ref, sh_ref, sc_ref, o_ref):
    x = x_ref[...]
    y = x * lax.rsqrt(jnp.mean(x * x, axis=-1, keepdims=True) + EPS)
    y = y * g_ref[...]
    o_ref[...] = (y * (1.0 + sc_ref[...]) + sh_ref[...]).astype(o_ref.dtype)


def _norm_mod(x, g, shift, scale, tm):
    bsz, L, d = x.shape
    return pl.pallas_call(
        _norm_mod_kernel,
        grid=(bsz, L // tm),
        in_specs=[pl.BlockSpec((None, tm, d), lambda b, i: (b, i, 0)),
                  pl.BlockSpec((1, d), lambda b, i: (0, 0)),
                  pl.BlockSpec((None, 1, d), lambda b, i: (b, 0, 0)),
                  pl.BlockSpec((None, 1, d), lambda b, i: (b, 0, 0))],
        out_specs=pl.BlockSpec((None, tm, d), lambda b, i: (b, i, 0)),
        out_shape=jax.ShapeDtypeStruct((bsz, L, d), BF16),
        compiler_params=_cparams("parallel", "parallel"),
        name="norm_mod",
    )(x, g.reshape(1, d), shift, scale)


def _conv3(z, wc, bc, seg):
    tm = z.shape[0]
    pos = lax.broadcasted_iota(jnp.int32, z.shape, 0) & (seg - 1)
    zp = jnp.where(pos == 0, 0.0, pltpu.roll(z, 1, 0))
    zn = jnp.where(pos == seg - 1, 0.0, pltpu.roll(z, tm - 1, 0))
    return zp * wc[0:1, :] + z * wc[1:2, :] + zn * wc[2:3, :] + bc


def _proj_act_kernel(h_ref, w_ref, b_ref, o_ref, *, act):
    z = jnp.dot(h_ref[...], w_ref[...], preferred_element_type=F32) + b_ref[...]
    if act == "sigmoid":
        z = jax.nn.sigmoid(z)
    o_ref[...] = z.astype(o_ref.dtype)


def _proj_act(h, w, b, act, out_dtype, tm, tn=SMALL_TN):
    bsz, L, d = h.shape
    n = w.shape[1]
    return pl.pallas_call(
        functools.partial(_proj_act_kernel, act=act),
        grid=(bsz, L // tm, n // tn),
        in_specs=[pl.BlockSpec((None, tm, d), lambda b_, i, j: (b_, i, 0)),
                  pl.BlockSpec((d, tn), lambda b_, i, j: (0, j)),
                  pl.BlockSpec((1, tn), lambda b_, i, j: (0, j))],
        out_specs=pl.BlockSpec((None, tm, tn), lambda b_, i, j: (b_, i, j)),
        out_shape=jax.ShapeDtypeStruct((bsz, L, n), out_dtype),
        compiler_params=_cparams("parallel", "parallel", "arbitrary"),
        name="proj_" + act,
    )(h, w, b.reshape(1, n))


def _proj_conv_silu_kernel(h_ref, w_ref, b_ref, wc_ref, bc_ref, cs_ref, o_ref, *, seg):
    z = jnp.dot(h_ref[...], w_ref[...], preferred_element_type=F32) + b_ref[...]
    y = _conv3(z, wc_ref[...], bc_ref[...], seg)
    y = y * jax.nn.sigmoid(y)
    o_ref[...] = (y * cs_ref[...]).astype(o_ref.dtype)


def _proj_conv_silu(h, w, b, wc, bc, colscale, seg, tm, tn=SMALL_TN):
    bsz, L, d = h.shape
    n = w.shape[1]
    col = lambda b_, i, j: (0, j)
    return pl.pallas_call(
        functools.partial(_proj_conv_silu_kernel, seg=seg),
        grid=(bsz, L // tm, n // tn),
        in_specs=[pl.BlockSpec((None, tm, d), lambda b_, i, j: (b_, i, 0)),
                  pl.BlockSpec((d, tn), col),
                  pl.BlockSpec((1, tn), col),
                  pl.BlockSpec((3, tn), col),
                  pl.BlockSpec((1, tn), col),
                  pl.BlockSpec((1, tn), col)],
        out_specs=pl.BlockSpec((None, tm, tn), lambda b_, i, j: (b_, i, j)),
        out_shape=jax.ShapeDtypeStruct((bsz, L, n), BF16),
        compiler_params=_cparams("parallel", "parallel", "arbitrary"),
        name="proj_conv_silu",
    )(h, w, b.reshape(1, n), wc, bc.reshape(1, n), colscale.reshape(1, n))


PROJ_TN = 1024
PROJ_SUB = 512
PM_Q, PM_K, PM_V, PM_O, PM_GA, PM_GB = range(6)


def _proj_main_kernel(x_ref, g_ref, sh_ref, sc_ref, w_ref, b_ref, wc_ref, bc_ref, cs_ref,
                      o_ref, h_ref, hi_hbm, hp_sc, sem, *, seg):
    b, i, j = pl.program_id(0), pl.program_id(1), pl.program_id(2)
    n2, jt = hi_hbm.shape[2], hi_hbm.shape[3]

    def interleave_copy(jj):
        return pltpu.make_async_copy(hp_sc.at[pl.ds(jj * n2, n2)], hi_hbm.at[b, i, :, jj, :], sem)

    @pl.when(j == 0)
    def _():
        x = x_ref[...]
        y = x * lax.rsqrt(jnp.mean(x * x, axis=-1, keepdims=True) + EPS) * g_ref[...]
        y = y * (1.0 + sc_ref[...]) + sh_ref[...]
        h_ref[...] = y.astype(h_ref.dtype)
        hp_sc[...] = _pack_bf16_pairs(y)
        for jj in range(jt):
            interleave_copy(jj).start()

    @pl.when(j == pl.num_programs(2) - 1)
    def _():
        for jj in range(jt):
            interleave_copy(jj).wait()

    def run(epilogue):
        for c in range(PROJ_TN // PROJ_SUB):
            sl = slice(c * PROJ_SUB, (c + 1) * PROJ_SUB)
            z = jnp.dot(h_ref[...], w_ref[:, sl], preferred_element_type=F32) + b_ref[:, sl]
            o_ref[:, sl] = epilogue(z, sl).astype(o_ref.dtype)

    def conv_silu(z, sl):
        y = _conv3(z, wc_ref[:, sl], bc_ref[:, sl], seg)
        return (y * jax.nn.sigmoid(y)) * cs_ref[:, sl]

    @pl.when(j <= PM_K)
    def _():
        run(conv_silu)

    @pl.when(j == PM_V)
    def _():
        run(lambda z, sl: z)

    @pl.when(j >= PM_O)
    def _():
        run(lambda z, sl: jax.nn.sigmoid(z))


def _proj_main(x, g, shift, scale, w, b, wc, bc, colscale, seg, tm, n2):
    bsz, L, d = x.shape
    n = w.shape[1]
    jt = tm // n2
    qk = lambda b_, i, j: (0, jnp.minimum(j, PM_K))
    row = pl.BlockSpec((None, tm, d), lambda b_, i, j: (b_, i, 0))
    bvec = pl.BlockSpec((None, 1, d), lambda b_, i, j: (b_, 0, 0))
    return pl.pallas_call(
        functools.partial(_proj_main_kernel, seg=seg),
        grid=(bsz, L // tm, n // PROJ_TN),
        in_specs=[row, pl.BlockSpec((1, d), lambda b_, i, j: (0, 0)), bvec, bvec,
                  pl.BlockSpec((d, PROJ_TN), lambda b_, i, j: (0, j)),
                  pl.BlockSpec((1, PROJ_TN), lambda b_, i, j: (0, j)),
                  pl.BlockSpec((3, PROJ_TN), qk),
                  pl.BlockSpec((1, PROJ_TN), qk),
                  pl.BlockSpec((1, PROJ_TN), qk)],
        out_specs=[pl.BlockSpec((None, tm, PROJ_TN), lambda b_, i, j: (b_, i, j)), row,
                   pl.BlockSpec(memory_space=pl.ANY)],
        out_shape=[jax.ShapeDtypeStruct((bsz, L, n), BF16), jax.ShapeDtypeStruct((bsz, L, d), BF16),
                   jax.ShapeDtypeStruct((bsz, L // tm, n2, jt, d // 2), jnp.uint32)],
        scratch_shapes=[pltpu.VMEM((tm, d // 2), jnp.uint32), pltpu.SemaphoreType.DMA(())],
        compiler_params=_cparams("parallel", "parallel", "arbitrary"),
        name="proj_main",
    )(x, g.reshape(1, d), shift, scale, w, b.reshape(1, n), wc, bc.reshape(1, -1), colscale.reshape(1, -1))


def _conv3_interleaved(z, wc, bc, seg, jt):
    grp = seg * jt
    pad = jnp.zeros((jt, z.shape[1]), z.dtype)
    prev, nxt = [], []
    for g0 in range(0, z.shape[0], grp):
        zg = z[g0:g0 + grp]
        prev += [pad, zg[:grp - jt]]
        nxt += [zg[jt:], pad]
    zp = jnp.concatenate(prev, axis=0)
    zn = jnp.concatenate(nxt, axis=0)
    return zp * wc[0:1, :] + z * wc[1:2, :] + zn * wc[2:3, :] + bc


def _proj_hyena_kernel(h_ref, w0_ref, w1_ref, w2_ref, b_ref, wc_ref, bc_ref, x0_ref, s_ref, *, seg):
    n2, jt = s_ref.shape[0], s_ref.shape[1]
    h = _unpack_bf16_pairs(h_ref[...].reshape(n2 * jt, h_ref.shape[2]))
    us = []
    for g, w_ref in enumerate((w0_ref, w1_ref, w2_ref)):
        z = jnp.dot(h, w_ref[...], preferred_element_type=F32) + b_ref[g]
        us.append(_conv3_interleaved(z, wc_ref[g], bc_ref[g], seg, jt))
    x0_ref[...] = _pack_bf16_pairs(us[0]).reshape(x0_ref.shape)
    s_ref[...] = (us[1] * us[2]).reshape(s_ref.shape)


def _proj_hyena(hi, w, b, wc, bc, seg):
    bsz, nt, n2, jt, dp = hi.shape
    d, tm = 2 * dp, n2 * jt
    L = nt * tm
    tn = DFT_C_TILE
    nblk = H_WIDTH // tn
    assert n2 % seg == 0 and (jt % 8 == 0 or nt == 1)
    b3 = b.reshape(3, 1, H_WIDTH)
    wc3 = wc.reshape(3, 3, H_WIDTH).transpose(1, 0, 2)
    bc3 = bc.reshape(3, 1, H_WIDTH)
    return pl.pallas_call(
        functools.partial(_proj_hyena_kernel, seg=seg),
        grid=(bsz, nt, nblk),
        in_specs=[pl.BlockSpec((None, None, n2, jt, dp), lambda b_, i, j: (b_, i, 0, 0, 0)),
                  pl.BlockSpec((d, tn), lambda b_, i, j: (0, j)),
                  pl.BlockSpec((d, tn), lambda b_, i, j: (0, nblk + j)),
                  pl.BlockSpec((d, tn), lambda b_, i, j: (0, 2 * nblk + j)),
                  pl.BlockSpec((3, 1, tn), lambda b_, i, j: (0, 0, j)),
                  pl.BlockSpec((3, 3, tn), lambda b_, i, j: (0, 0, j)),
                  pl.BlockSpec((3, 1, tn), lambda b_, i, j: (0, 0, j))],
        out_specs=[pl.BlockSpec((None, n2, jt, tn // 2), lambda b_, i, j: (b_, 0, i, j)),
                   pl.BlockSpec((None, n2, jt, tn), lambda b_, i, j: (b_, 0, i, j))],
        out_shape=[jax.ShapeDtypeStruct((bsz, n2, L // n2, H_WIDTH // 2), jnp.uint32),
                   jax.ShapeDtypeStruct((bsz, n2, L // n2, H_WIDTH), F32)],
        compiler_params=_cparams("parallel", "parallel", "arbitrary"),
        name="proj_hyena",
    )(hi, w, w, w, b3, wc3, bc3)


N_GATES = 4 * M_HEADS


def _split3(x):
    hi = x.astype(BF16)
    r1 = x - hi.astype(F32)
    mid = r1.astype(BF16)
    lo = (r1 - mid.astype(F32)).astype(BF16)
    return hi, mid, lo


def _log_sigmoid(x):
    return jnp.minimum(x, 0.0) - jnp.log1p(jnp.exp(-jnp.abs(x)))


def _gates_kernel(h_ref, w_ref, wt_ref, b_ref, bt_ref, bc_ref, ac_ref, ar_ref):
    h = h_ref[...]
    t = h.shape[0]
    z = jnp.dot(h, w_ref[...], preferred_element_type=F32) + b_ref[...]
    zt = lax.dot_general(wt_ref[...], h, (((1,), (1,)), ((), ())),
                         preferred_element_type=F32) + bt_ref[...]
    r = lax.broadcasted_iota(jnp.int32, (t, t), 0)
    c = lax.broadcasted_iota(jnp.int32, (t, t), 1)
    lower = (r >= c).astype(BF16)
    upper = (r <= c).astype(BF16)
    g8 = FG_LANE0

    lf = _log_sigmoid(z)
    lane = lax.broadcasted_iota(jnp.int32, z.shape, 1)
    is_fg = (lane >= g8) & (lane < 2 * g8)
    terms = [jnp.where(is_fg, p.astype(F32), 0.0) for p in _split3(lf)]
    packed = terms[0] + pltpu.roll(terms[1], 2 * g8, 1) + pltpu.roll(terms[2], 4 * g8, 1)
    cfp = jnp.dot(lower, packed.astype(BF16), preferred_element_type=F32)
    cf = cfp + pltpu.roll(cfp, LANES - 2 * g8, 1) + pltpu.roll(cfp, LANES - 4 * g8, 1)
    cb = cf[t - 1:t, :] - cf + lf
    bc = jnp.where(lane < g8 + M_HEADS, cf, cb)
    bc = pltpu.roll(bc, LANES - g8, 1)
    bc_ref[...] = bc
    ac_ref[...] = z - bc

    lft = _log_sigmoid(zt[g8:, :])
    stacked = jnp.concatenate([p.astype(F32) for p in _split3(lft)] + [jnp.zeros_like(lft)], axis=0)
    cft3 = jnp.dot(stacked.astype(BF16), upper, preferred_element_type=F32)
    cft = cft3[0:g8] + cft3[g8:2 * g8] + cft3[2 * g8:3 * g8]
    cbt = cft[:, t - 1:t] - cft + lft
    row = lax.broadcasted_iota(jnp.int32, cft.shape, 0)
    ar_ref[...] = zt[:g8, :] - jnp.where(row < M_HEADS, cft, cbt)


FG_LANE0 = 2 * M_HEADS


def _gates(h, w_g, b_g, chunk):
    bsz, L, d = h.shape
    w_pad = jnp.zeros((d, LANES), F32).at[:, :N_GATES].set(w_g).astype(BF16)
    b_pad = jnp.zeros((1, LANES), F32).at[0, :N_GATES].set(b_g)
    wt = w_g.T.astype(BF16)
    bt = b_g.reshape(N_GATES, 1)
    tok = pl.BlockSpec((None, chunk, LANES), lambda b_, i: (b_, i, 0))
    return pl.pallas_call(
        _gates_kernel,
        grid=(bsz, L // chunk),
        in_specs=[pl.BlockSpec((None, chunk, d), lambda b_, i: (b_, i, 0)),
                  pl.BlockSpec((d, LANES), lambda b_, i: (0, 0)),
                  pl.BlockSpec((N_GATES, d), lambda b_, i: (0, 0)),
                  pl.BlockSpec((1, LANES), lambda b_, i: (0, 0)),
                  pl.BlockSpec((N_GATES, 1), lambda b_, i: (0, 0))],
        out_specs=[tok, tok, pl.BlockSpec((None, FG_LANE0, chunk), lambda b_, i: (b_, 0, i))],
        out_shape=[jax.ShapeDtypeStruct((bsz, L, LANES), F32),
                   jax.ShapeDtypeStruct((bsz, L, LANES), F32),
                   jax.ShapeDtypeStruct((bsz, FG_LANE0, L), F32)],
        compiler_params=_cparams("parallel", "parallel"),
        name="mlstm_gates",
    )(h, w_pad, wt, b_pad, bt)


def _mlstm_kernel(*refs, emit_h, n_chunks):
    if emit_h:
        (q_ref, k_ref, v_ref, bc_ref, ac_ref, ar_ref, c0_ref, n0_ref, m0_ref,
         h_ref, cf_ref, nf_ref, mf_ref, c_sc, n_sc, m_sc) = refs
    else:
        (k_ref, v_ref, bc_ref, ac_ref, ar_ref, c0_ref, n0_ref, m0_ref,
         cf_ref, nf_ref, mf_ref, c_sc, n_sc, m_sc) = refs
    d = pl.program_id(1)
    j = pl.program_id(2)
    fwd = d == 0
    t = k_ref.shape[0]
    dh = M_HEAD_DIM

    @pl.when(j == 0)
    def _():
        c_sc[...] = c0_ref[...]
        n_sc[...] = n0_ref[...]
        m_sc[...] = m0_ref[...]

    r = lax.broadcasted_iota(jnp.int32, (t, t), 0)
    c = lax.broadcasted_iota(jnp.int32, (t, t), 1)
    causal = jnp.where(fwd, r - c, c - r) >= 0
    bc_all = bc_ref[...]
    ac_all = ac_ref[...]
    ar_all = ar_ref[...]
    for hd in range(M_HEADS):
        sl = slice(hd * dh, (hd + 1) * dh)
        bc = jnp.where(fwd, bc_all[:, hd:hd + 1], bc_all[:, M_HEADS + hd:M_HEADS + hd + 1])
        ac = jnp.where(fwd, ac_all[:, hd:hd + 1], ac_all[:, M_HEADS + hd:M_HEADS + hd + 1])
        ar = jnp.where(fwd, ar_all[hd:hd + 1, :], ar_all[M_HEADS + hd:M_HEADS + hd + 1, :])
        b_tot = jnp.where(fwd, bc[t - 1:t, :], bc[0:1, :])
        m_prev = m_sc[hd][:, 0:1]
        k_h = k_ref[:, sl]
        v_h = v_ref[:, sl]
        if emit_h:
            q_h = q_ref[:, sl]
            dm = jnp.where(causal, bc + ar, NEG_BIG)
            inter = bc + m_prev
            m_t = jnp.maximum(inter, jnp.max(dm, axis=1, keepdims=True))
            qk = lax.dot_general(q_h, k_h, (((1,), (1,)), ((), ())), preferred_element_type=F32)
            s = qk * jnp.exp(dm - m_t)
            carry = jnp.exp(inter - m_t)
            num = (jnp.dot(s.astype(BF16), v_h, preferred_element_type=F32)
                   + carry * jnp.dot(q_h, c_sc[hd].astype(BF16), preferred_element_type=F32))
            den = (jnp.sum(s, axis=1, keepdims=True)
                   + carry * jnp.sum(q_h.astype(F32) * n_sc[hd], axis=1, keepdims=True))
            h_ref[:, sl] = (num / jnp.maximum(jnp.abs(den), jnp.exp(-m_t))).astype(h_ref.dtype)
        g = b_tot + ac
        m_new = jnp.maximum(b_tot + m_prev, jnp.max(g, axis=0, keepdims=True))
        wgt = jnp.exp(g - m_new)
        decay = jnp.exp(b_tot + m_prev - m_new)
        kw = k_h.astype(F32) * wgt
        c_sc[hd] = decay * c_sc[hd] + lax.dot_general(kw.astype(BF16), v_h, (((0,), (0,)), ((), ())),
                                                      preferred_element_type=F32)
        n_sc[hd] = decay * n_sc[hd] + jnp.sum(kw, axis=0, keepdims=True)
        m_sc[hd] = jnp.broadcast_to(m_new, (1, LANES))

    @pl.when(j == n_chunks - 1)
    def _():
        cf_ref[...] = c_sc[...]
        nf_ref[...] = n_sc[...]
        mf_ref[...] = m_sc[...]


def _mlstm(q, k, v, bc, ac, ar, state, emit_h, t):
    bsz, L, _ = k[0].shape
    nc = L // t
    seq = lambda b_, d, j: (b_, j + d * (nc - 1 - 2 * j), 0)
    st = lambda b_, d, j: (b_, d, 0, 0, 0)

    def tok(col):
        return pl.BlockSpec((None, t, M_WIDTH), lambda b_, d, j: (b_, j + d * (nc - 1 - 2 * j), col))

    gate_spec = pl.BlockSpec((None, t, LANES), seq)
    ar_spec = pl.BlockSpec((None, FG_LANE0, t), lambda b_, d, j: (b_, 0, j + d * (nc - 1 - 2 * j)))
    c_spec = pl.BlockSpec((None, None, M_HEADS, M_HEAD_DIM, M_HEAD_DIM), st)
    n_spec = pl.BlockSpec((None, None, M_HEADS, 1, M_HEAD_DIM), st)
    m_spec = pl.BlockSpec((None, None, M_HEADS, 1, LANES), st)
    state_shapes = [jax.ShapeDtypeStruct((bsz, 2, M_HEADS, M_HEAD_DIM, M_HEAD_DIM), F32),
                    jax.ShapeDtypeStruct((bsz, 2, M_HEADS, 1, M_HEAD_DIM), F32),
                    jax.ShapeDtypeStruct((bsz, 2, M_HEADS, 1, LANES), F32)]
    in_specs = [tok(k[1]), tok(v[1]), gate_spec, gate_spec, ar_spec, c_spec, n_spec, m_spec]
    args = [k[0], v[0], bc, ac, ar, *state]
    out_specs = [c_spec, n_spec, m_spec]
    out_shape = list(state_shapes)
    if emit_h:
        in_specs = [tok(q[1])] + in_specs
        args = [q[0]] + args
        out_specs = [pl.BlockSpec((None, None, t, M_WIDTH),
                                  lambda b_, d, j: (d, b_, j + d * (nc - 1 - 2 * j), 0))] + out_specs
        out_shape = [jax.ShapeDtypeStruct((2, bsz, L, M_WIDTH), BF16)] + out_shape
    outs = pl.pallas_call(
        functools.partial(_mlstm_kernel, emit_h=emit_h, n_chunks=nc),
        grid=(bsz, 2, nc),
        in_specs=in_specs,
        out_specs=out_specs,
        out_shape=out_shape,
        scratch_shapes=[pltpu.VMEM((M_HEADS, M_HEAD_DIM, M_HEAD_DIM), F32),
                        pltpu.VMEM((M_HEADS, 1, M_HEAD_DIM), F32),
                        pltpu.VMEM((M_HEADS, 1, LANES), F32)],
        compiler_params=_cparams("parallel", "parallel", "arbitrary"),
        name="mlstm" if emit_h else "mlstm_state",
    )(*args)
    if emit_h:
        return outs[0], tuple(outs[1:])
    return None, tuple(outs)


DFT_M_TILE = 8
DFT_C_TILE = 1024
DFT_INNER_C_TILE = 512
FEAT_ROWS = 16


def _filter_outer_kernel(bands_ref, w1t_ref, b1_ref, w2t_ref, b2_ref, w3p_ref, w3f_ref, fr_ref, dl_ref, l_ref,
                         a_ref, ss_ref, *, L, n1, n2):
    i = pl.program_id(0)
    h = n1 // 2
    cols = DFT_M_TILE * h

    def positions(shape, axis, side):
        q = lax.broadcasted_iota(jnp.int32, shape, axis)
        mm, jj = q // h, q % h
        n = n2 * (jj + side * h) + i * DFT_M_TILE + mm
        return n, jnp.where(n < L, n, 2 * L - n).astype(F32)

    taps = []
    sumsq = jnp.zeros((1, a_ref.shape[-1]), F32)
    for side, w3_ref in ((0, w3p_ref), (1, w3f_ref)):
        _, p_row = positions((1, cols), 1, side)
        t_row = p_row / float(max(L - 1, 1))
        ang = ((2 * math.pi / L) * p_row) * bands_ref[...]
        row = lax.broadcasted_iota(jnp.int32, (FEAT_ROWS, cols), 0)
        feats = jnp.concatenate([jnp.where(row == 0, t_row, 0.0), jnp.cos(ang), -jnp.sin(ang)], axis=0)
        fr = fr_ref[...]
        hid = jnp.sin(fr * (jnp.dot(w1t_ref[...], feats.astype(BF16), preferred_element_type=F32) + b1_ref[...]))
        hid = jnp.sin(fr * (jnp.dot(w2t_ref[...], hid.astype(BF16), preferred_element_type=F32) + b2_ref[...]))
        filt = lax.dot_general(hid.astype(BF16), w3_ref[...], (((0,), (0,)), ((), ())),
                               preferred_element_type=F32)
        n_col, p_col = positions((cols, 1), 0, side)
        t_col = p_col / float(max(L - 1, 1))
        kern = filt * jnp.exp(-t_col * jnp.abs(dl_ref[...]))
        kern = jnp.where(n_col == L, 0.0, kern)
        sumsq = sumsq + jnp.sum(kern * kern, axis=0, keepdims=True)
        taps.append(kern)

    for mm in range(DFT_M_TILE):
        x = jnp.concatenate([taps[0][mm * h:(mm + 1) * h], taps[1][mm * h:(mm + 1) * h]], axis=0)
        out = jnp.dot(l_ref[...], x.astype(BF16), preferred_element_type=F32)
        a_ref[0, :, mm, :] = out[:n1]
        a_ref[1, :, mm, :] = out[n1:]

    @pl.when(i == 0)
    def _():
        ss_ref[...] = jnp.zeros_like(ss_ref)

    ss_ref[...] += sumsq


def _filter_outer(L, n1, n2, fwd_r, w1, b1, w2, b2, w3, freq):
    hid = H_FILTER_HIDDEN
    bands = jnp.linspace(1e-4, H_POS_BANDS - 1, H_POS_BANDS, dtype=F32).reshape(H_POS_BANDS, 1)
    w1t = jnp.zeros((hid, 3 * FEAT_ROWS), F32)
    w1t = w1t.at[:, 0].set(w1[0]).at[:, FEAT_ROWS:2 * FEAT_ROWS].set(w1[1:1 + H_POS_BANDS].T)
    w1t = w1t.at[:, 2 * FEAT_ROWS:].set(w1[1 + H_POS_BANDS:].T).astype(BF16)
    w3h = w3.astype(BF16)
    max_decay = math.log(H_DECAY_TARGET) / H_FAST_DECAY_PCT
    min_decay = math.log(H_DECAY_TARGET) / H_SLOW_DECAY_PCT
    deltas = jnp.linspace(min_decay, max_decay, H_WIDTH, dtype=F32).reshape(1, H_WIDTH)
    col = lambda v: v.reshape(hid, 1)
    full = lambda a: pl.BlockSpec(a.shape, lambda i: (0,) * a.ndim)
    args = [bands, w1t, col(b1), w2.T.astype(BF16), col(b2)]
    return pl.pallas_call(
        functools.partial(_filter_outer_kernel, L=L, n1=n1, n2=n2),
        grid=(n2 // DFT_M_TILE,),
        in_specs=[full(a) for a in args]
        + [pl.BlockSpec((hid, H_WIDTH), lambda i: (0, 0)), pl.BlockSpec((hid, H_WIDTH), lambda i: (0, 1)),
           full(col(freq)), full(deltas), full(fwd_r)],
        out_specs=[pl.BlockSpec((2, n1, DFT_M_TILE, H_WIDTH), lambda i: (0, 0, i, 0)),
                   pl.BlockSpec((1, H_WIDTH), lambda i: (0, 0))],
        out_shape=[jax.ShapeDtypeStruct((2, n1, n2, H_WIDTH), F32),
                   jax.ShapeDtypeStruct((1, H_WIDTH), F32)],
        compiler_params=_cparams("arbitrary"),
        name="hyena_filter_outer",
    )(*args, w3h, w3h, col(freq), deltas, fwd_r)


def _dft_factors(n):
    lg = int(round(math.log2(n)))
    n1 = 1 << ((lg + 1) // 2)
    return n1, n // n1


def _dft_outer_matrices(n1):
    k = np.arange(n1)[:, None]
    n = np.arange(n1)[None, :]
    ang = 2.0 * np.pi * ((k * n) % n1) / n1
    cr, ci = np.cos(ang), -np.sin(ang)
    h = n1 // 2
    fwd_c = np.block([[cr[:, :h], -ci[:, :h]], [ci[:, :h], cr[:, :h]]])
    fwd_r = np.concatenate([cr, ci], axis=0)
    ir, ii = cr[:h, :], -ci[:h, :]
    inv = np.block([[ir, -ii], [ii, ir]])
    return (jnp.asarray(fwd_c, F32).astype(BF16), jnp.asarray(fwd_r, F32).astype(BF16),
            jnp.asarray(inv, F32).astype(BF16))


def _dft_inner_matrices(n1, n2):
    n = n1 * n2
    k2 = np.arange(n2)[:, None]
    m = np.arange(n2)[None, :]
    ang = 2.0 * np.pi * ((k2 * m) % n2) / n2
    fr, fi = np.cos(ang), -np.sin(ang)
    f = np.block([[fr, -fi], [fi, fr]])
    k1 = jnp.arange(n1, dtype=jnp.int32)[:, None]
    tw_ang = ((jnp.arange(n2, dtype=jnp.int32)[None, :] * k1) % n).astype(F32) * (2.0 * math.pi / n)
    rep = lambda t: jnp.broadcast_to(t[:, :, None], (n1, n2, LANES))
    return (jnp.asarray(f, F32).astype(BF16), jnp.asarray(f.T, F32).astype(BF16),
            rep(jnp.cos(tw_ang)), rep(-jnp.sin(tw_ang)))


def _outer_fwd_kernel(l_ref, s_ref, a_ref):
    n1 = a_ref.shape[1]
    for mm in range(s_ref.shape[1]):
        x = jnp.concatenate([s_ref[0, mm], s_ref[1, mm]], axis=0).astype(BF16)
        out = jnp.dot(l_ref[...], x, preferred_element_type=F32)
        a_ref[0, :, mm, :] = out[:n1]
        a_ref[1, :, mm, :] = out[n1:]


def _outer_fwd(lmat, s_t):
    _, n2, n1h, c = s_t.shape
    n1 = 2 * n1h
    tc = min(DFT_C_TILE, c)
    return pl.pallas_call(
        _outer_fwd_kernel,
        grid=(n2 // DFT_M_TILE, c // tc),
        in_specs=[pl.BlockSpec(lmat.shape, lambda m, j: (0, 0)),
                  pl.BlockSpec((2, DFT_M_TILE, n1h, tc), lambda m, j: (0, m, 0, j))],
        out_specs=pl.BlockSpec((2, n1, DFT_M_TILE, tc), lambda m, j: (0, 0, m, j)),
        out_shape=jax.ShapeDtypeStruct((2, n1, n2, c), F32),
        compiler_params=_cparams("parallel", "parallel"),
        name="dft_outer_fwd",
    )(lmat, s_t)


def _outer_inv_kernel(l_ref, b_ref, s_ref, x0_ref, ysc_ref, hb_ref, o_ref):
    n1h = s_ref.shape[2]
    for mm in range(b_ref.shape[1]):
        y = jnp.concatenate([b_ref[0, mm], b_ref[1, mm]], axis=0).astype(BF16)
        out = jnp.dot(l_ref[...], y, preferred_element_type=F32)
        for b in range(2):
            conv = out[b * n1h:(b + 1) * n1h]
            x0 = _unpack_bf16_pairs(x0_ref[b, mm]).astype(F32)
            hy = x0 * (conv * ysc_ref[...] + hb_ref[...] * s_ref[b, mm])
            o_ref[b, :, mm, :] = _pack_bf16_pairs(hy)


def _outer_inv(lmat, b_t, s_t, x0_t, yscale, h_bias):
    _, n2, n1, c = b_t.shape
    n1h = n1 // 2
    tc = min(DFT_C_TILE, c)
    vec = pl.BlockSpec((1, tc), lambda m, j: (0, j))
    hy = pl.pallas_call(
        _outer_inv_kernel,
        grid=(n2 // DFT_M_TILE, c // tc),
        in_specs=[pl.BlockSpec(lmat.shape, lambda m, j: (0, 0)),
                  pl.BlockSpec((2, DFT_M_TILE, n1, tc), lambda m, j: (0, m, 0, j)),
                  pl.BlockSpec((2, DFT_M_TILE, n1h, tc), lambda m, j: (0, m, 0, j)),
                  pl.BlockSpec((2, DFT_M_TILE, n1h, tc // 2), lambda m, j: (0, m, 0, j)),
                  vec, vec],
        out_specs=pl.BlockSpec((2, n1h, DFT_M_TILE, tc // 2), lambda m, j: (0, 0, m, j)),
        out_shape=jax.ShapeDtypeStruct((2, n1h, n2, c // 2), jnp.uint32),
        compiler_params=_cparams("parallel", "parallel"),
        name="dft_outer_inv",
    )(lmat, b_t, s_t, x0_t, yscale, h_bias.reshape(1, c))
    return hy.reshape(2, n1h * n2, c // 2)


DFT_K_TILE = 8


def _twiddled_inner_dft(f_ref, twr_ref, twi_ref, a_ref, kk):
    n2, c = a_ref.shape[2], a_ref.shape[3]
    twr = jnp.tile(twr_ref[kk], (1, c // LANES))
    twi = jnp.tile(twi_ref[kk], (1, c // LANES))
    ar, ai = a_ref[0, kk], a_ref[1, kk]
    a = jnp.concatenate([(ar * twr - ai * twi).astype(BF16), (ar * twi + ai * twr).astype(BF16)], axis=0)
    x = jnp.dot(f_ref[...], a, preferred_element_type=F32)
    return x[:n2], x[n2:], twr, twi


def _inner_fwd_kernel(f_ref, twr_ref, twi_ref, a_ref, o_ref):
    for kk in range(a_ref.shape[1]):
        xr, xi, _, _ = _twiddled_inner_dft(f_ref, twr_ref, twi_ref, a_ref, kk)
        o_ref[0, kk] = xr.astype(o_ref.dtype)
        o_ref[1, kk] = xi.astype(o_ref.dtype)


def _inner_specs(n1, n2, c):
    tc = min(DFT_INNER_C_TILE, c)
    kt = min(DFT_K_TILE, n1)
    blk = pl.BlockSpec((2, kt, n2, tc), lambda k, j: (0, k, 0, j))
    mat = pl.BlockSpec((2 * n2, 2 * n2), lambda k, j: (0, 0))
    tw = pl.BlockSpec((kt, n2, LANES), lambda k, j: (k, 0, 0))
    return blk, mat, tw, (n1 // kt, c // tc), kt, tc


def _inner_fwd(f, twr, twi, a):
    _, n1, n2, c = a.shape
    blk, mat, tw, grid, _, _ = _inner_specs(n1, n2, c)
    return pl.pallas_call(
        _inner_fwd_kernel,
        grid=grid,
        in_specs=[mat, tw, tw, blk],
        out_specs=blk,
        out_shape=jax.ShapeDtypeStruct((2, n1, n2, c), BF16),
        compiler_params=_cparams("parallel", "parallel"),
        name="dft_inner_filter",
    )(f, twr, twi, a)


def _inner_conv_kernel(f_ref, ft_ref, twr_ref, twi_ref, a_ref, k_ref, o_ref):
    n2 = a_ref.shape[2]
    for kk in range(a_ref.shape[1]):
        xr, xi, twr, twi = _twiddled_inner_dft(f_ref, twr_ref, twi_ref, a_ref, kk)
        kr, ki = k_ref[0, kk].astype(F32), k_ref[1, kk].astype(F32)
        yr = xr * kr - xi * ki
        yi = xr * ki + xi * kr
        y = jnp.concatenate([yr.astype(BF16), yi.astype(BF16)], axis=0)
        b = jnp.dot(ft_ref[...], y, preferred_element_type=F32)
        br, bi = b[:n2], b[n2:]
        o_ref[0, :, kk, :] = br * twr + bi * twi
        o_ref[1, :, kk, :] = bi * twr - br * twi


def _inner_conv(f, ft, twr, twi, a, kf):
    _, n1, n2, c = a.shape
    blk, mat, tw, grid, kt, tc = _inner_specs(n1, n2, c)
    return pl.pallas_call(
        _inner_conv_kernel,
        grid=grid,
        in_specs=[mat, mat, tw, tw, blk, blk],
        out_specs=pl.BlockSpec((2, n2, kt, tc), lambda k, j: (0, 0, k, j)),
        out_shape=jax.ShapeDtypeStruct((2, n2, n1, c), F32),
        compiler_params=_cparams("parallel", "parallel"),
        name="dft_inner_conv",
    )(f, ft, twr, twi, a, kf)


def _hyena_long_conv(s_t, x0_t, h_bias, w1, b1, w2, b2, w3, freq):
    bsz, n2, n1h, c = s_t.shape
    assert bsz == 2
    n1 = 2 * n1h
    L = n1h * n2
    fwd_c, fwd_r, inv = _dft_outer_matrices(n1)
    f, ft, twr, twi = _dft_inner_matrices(n1, n2)
    af, sumsq = _filter_outer(L, n1, n2, fwd_r, w1, b1, w2, b2, w3, freq)
    kf = _inner_fwd(f, twr, twi, af)
    a = _outer_fwd(fwd_c, s_t)
    b_t = _inner_conv(f, ft, twr, twi, a, kf)
    yscale = lax.rsqrt(sumsq + EPS) * (1.0 / (2 * L))
    return _outer_inv(inv, b_t, s_t, x0_t, yscale, h_bias)


def _pack_bf16_pairs(x):
    half = x.shape[1] // 2
    lo = pltpu.bitcast(x[:, :half].astype(BF16).astype(F32), jnp.uint32) >> 16
    hi = pltpu.bitcast(x[:, half:].astype(BF16).astype(F32), jnp.uint32) & jnp.uint32(0xFFFF0000)
    return lo | hi


def _unpack_bf16_pairs(p):
    lo = pltpu.bitcast(p << 16, F32).astype(BF16)
    hi = pltpu.bitcast(p & jnp.uint32(0xFFFF0000), F32).astype(BF16)
    return jnp.concatenate([lo, hi], axis=1)


def _merge_kernel(hf_ref, hb_ref, o_ref, hy_ref, ga_ref, gb_ref, x_ref,
                  gate_ref, g2_ref, sh_ref, sc_ref, wa_ref, wb_ref, wo_ref, x1_ref, h2_ref):
    a = o_ref[...].astype(F32) * (hf_ref[...].astype(F32) + hb_ref[...].astype(F32))
    half = DFT_C_TILE // 2
    hy = jnp.concatenate([_unpack_bf16_pairs(hy_ref[:, c * half:(c + 1) * half])
                          for c in range(hy_ref.shape[1] // half)], axis=1)
    pa = jnp.dot(a.astype(BF16), wa_ref[...], preferred_element_type=F32)
    pb = jnp.dot(hy, wb_ref[...], preferred_element_type=F32)
    mix = ga_ref[...].astype(F32) * pa + gb_ref[...].astype(F32) * pb
    out = jnp.dot(mix.astype(BF16), wo_ref[...], preferred_element_type=F32)
    x1 = x_ref[...] + gate_ref[...] * out
    x1_ref[...] = x1
    y = x1 * lax.rsqrt(jnp.mean(x1 * x1, axis=-1, keepdims=True) + EPS) * g2_ref[...]
    h2_ref[...] = _pack_bf16_pairs(y * (1.0 + sc_ref[...]) + sh_ref[...])


def _merge(hdirs, pm, hy, x, gate1, g2, shift2, scale2, w_a, w_b, w_out, tm=MERGE_TM):
    bsz, L, d = x.shape
    tok = pl.BlockSpec((None, tm, d), lambda b, i: (b, i, 0))

    def pm_tile(col):
        return pl.BlockSpec((None, tm, d), lambda b, i: (b, i, col))

    packed = pl.BlockSpec((None, tm, d // 2), lambda b, i: (b, i, 0))
    vec = pl.BlockSpec((1, d), lambda b, i: (0, 0))
    bvec = pl.BlockSpec((None, 1, d), lambda b, i: (b, 0, 0))
    wsp = pl.BlockSpec((d, d), lambda b, i: (0, 0), pipeline_mode=pl.Buffered(1))
    return pl.pallas_call(
        _merge_kernel,
        grid=(bsz, L // tm),
        in_specs=[pl.BlockSpec((None, None, tm, d), lambda b, i: (0, b, i, 0)),
                  pl.BlockSpec((None, None, tm, d), lambda b, i: (1, b, i, 0)),
                  pm_tile(PM_O), packed, pm_tile(PM_GA), pm_tile(PM_GB), tok,
                  bvec, vec, bvec, bvec, wsp, wsp, wsp],
        out_specs=[tok, packed],
        out_shape=[jax.ShapeDtypeStruct((bsz, L, d), F32), jax.ShapeDtypeStruct((bsz, L, d // 2), jnp.uint32)],
        compiler_params=_cparams("parallel", "parallel"),
        name="merge",
    )(hdirs, hdirs, pm, hy, pm, pm, x, gate1, g2.reshape(1, d), shift2, scale2, w_a, w_b, w_out)


MOE_BLOCK = 256
ROUTE_E1, ROUTE_E2, ROUTE_W1, ROUTE_W2 = 0, 1, 2, 3
EXP_LANE0 = N_GROUPS


def _first_lane_of_max(val, valid, lane):
    masked = jnp.where(valid, val, NEG_BIG)
    mx = jnp.max(masked, axis=1, keepdims=True)
    idx = jnp.min(jnp.where(valid & (masked == mx), lane, LANES), axis=1, keepdims=True)
    return mx, idx


MOE_TM = 1024


def _expert_onehots(rec):
    lane = lax.broadcasted_iota(jnp.int32, rec.shape, 1)
    oh1 = lane == rec[:, ROUTE_E1:ROUTE_E1 + 1].astype(jnp.int32)
    oh2 = lane == rec[:, ROUTE_E2:ROUTE_E2 + 1].astype(jnp.int32)
    return oh1, oh2


def _router_kernel(h_ref, w_ref, b_ref, r_ref, cnt_ref):
    logits = jnp.dot(_unpack_bf16_pairs(h_ref[...]), w_ref[...], preferred_element_type=F32) + b_ref[...]
    lane = lax.broadcasted_iota(jnp.int32, logits.shape, 1)
    is_g = lane < N_GROUPS
    gmax, gsel = _first_lane_of_max(logits, is_g, lane)
    gsum = jnp.sum(jnp.where(is_g, jnp.exp(logits - gmax), 0.0), axis=1, keepdims=True)
    gw = 1.0 / gsum
    lo = EXP_LANE0 + gsel * EXPERTS_PER_GROUP
    in_grp = (lane >= lo) & (lane < lo + EXPERTS_PER_GROUP)
    emax, l1 = _first_lane_of_max(logits, in_grp, lane)
    esum = jnp.sum(jnp.where(in_grp, jnp.exp(logits - emax), 0.0), axis=1, keepdims=True)
    e2max, l2 = _first_lane_of_max(logits, in_grp & (lane != l1), lane)
    v1 = 1.0 / esum
    v2 = jnp.exp(e2max - emax) / esum
    vs = v1 + v2
    w1 = gw * v1 / vs
    w2 = gw * v2 / vs
    e1 = (l1 - EXP_LANE0).astype(F32)
    e2 = (l2 - EXP_LANE0).astype(F32)
    rec = jnp.where(lane == ROUTE_E1, e1,
                    jnp.where(lane == ROUTE_E2, e2,
                              jnp.where(lane == ROUTE_W1, w1,
                                        jnp.where(lane == ROUTE_W2, w2, 0.0))))
    r_ref[...] = rec
    oh1, oh2 = _expert_onehots(rec)
    counts = jnp.sum((oh1 | oh2).astype(F32), axis=0, keepdims=True)
    cnt_ref[...] = jnp.broadcast_to(counts, cnt_ref.shape)


def _router(h2, w_group, b_group, w_router, b_router):
    n, dp = h2.shape
    d = 2 * dp
    tm = MOE_TM
    w = jnp.zeros((d, LANES), F32).at[:, :N_GROUPS].set(w_group).at[
        :, EXP_LANE0:EXP_LANE0 + N_EXPERTS].set(w_router).astype(BF16)
    b = jnp.zeros((1, LANES), F32).at[0, :N_GROUPS].set(b_group).at[
        0, EXP_LANE0:EXP_LANE0 + N_EXPERTS].set(b_router)
    return pl.pallas_call(
        _router_kernel,
        grid=(n // tm,),
        in_specs=[pl.BlockSpec((tm, dp), lambda i: (i, 0)),
                  pl.BlockSpec((d, LANES), lambda i: (0, 0)),
                  pl.BlockSpec((1, LANES), lambda i: (0, 0))],
        out_specs=[pl.BlockSpec((tm, LANES), lambda i: (i, 0)),
                   pl.BlockSpec((None, 8, LANES), lambda i: (i, 0, 0))],
        out_shape=[jax.ShapeDtypeStruct((n, LANES), F32), jax.ShapeDtypeStruct((n // tm, 8, LANES), F32)],
        compiler_params=_cparams("parallel"),
        name="moe_router",
    )(h2, w, b)


def _slots_kernel(r_ref, base_ref, dest_ref):
    rec = r_ref[...]
    tm = rec.shape[0]
    lane = lax.broadcasted_iota(jnp.int32, rec.shape, 1)
    oh1, oh2 = _expert_onehots(rec)
    r = lax.broadcasted_iota(jnp.int32, (tm, tm), 0)
    c = lax.broadcasted_iota(jnp.int32, (tm, tm), 1)
    earlier = (r > c).astype(BF16)
    rank = jnp.dot(earlier, (oh1 | oh2).astype(BF16), preferred_element_type=F32) + base_ref[0:1, :]
    d1 = jnp.sum(jnp.where(oh1, rank, 0.0), axis=1, keepdims=True)
    d2 = jnp.sum(jnp.where(oh2, rank, 0.0), axis=1, keepdims=True)
    dest_ref[...] = jnp.where(lane == 0, d1, jnp.where(lane == 1, d2, 0.0)).astype(jnp.int32)


def _slots(route, tile_counts):
    n = route.shape[0]
    tm = MOE_TM
    cnt = tile_counts[:, 0, :]
    totals = jnp.sum(cnt, axis=0)
    nblk = jnp.ceil(totals * (1.0 / MOE_BLOCK))
    first_slot = (jnp.cumsum(nblk) - nblk) * float(MOE_BLOCK)
    base = first_slot[None, :] + jnp.cumsum(cnt, axis=0) - cnt
    base = jnp.broadcast_to(base[:, None, :], tile_counts.shape)
    dest = pl.pallas_call(
        _slots_kernel,
        grid=(n // tm,),
        in_specs=[pl.BlockSpec((tm, LANES), lambda i: (i, 0)),
                  pl.BlockSpec((None, 8, LANES), lambda i: (i, 0, 0))],
        out_specs=pl.BlockSpec((tm, LANES), lambda i: (i, 0)),
        out_shape=jax.ShapeDtypeStruct((n, LANES), jnp.int32),
        compiler_params=_cparams("parallel"),
        name="moe_slots",
    )(route, base)
    return dest, totals


EXPERT_STEP_BLOCKS = 4


def _experts_kernel(be_ref, first_ref, nxt_ref, par_ref, nu_ref, x_ref, w1_hbm, w3_hbm, w2_hbm, o_ref,
                    w1f, w3f, w2f, w1b, w3b, w2b, sems):
    step = pl.program_id(0)

    def weight_copies(e, slot):
        return (pltpu.make_async_copy(w1_hbm.at[e], w1f.at[slot], sems.at[0, slot]),
                pltpu.make_async_copy(w3_hbm.at[e], w3f.at[slot], sems.at[1, slot]),
                pltpu.make_async_copy(w2_hbm.at[e], w2f.at[slot], sems.at[2, slot]))

    @pl.when(step == 0)
    def _():
        for cp in weight_copies(be_ref[0], 0):
            cp.start()

    for sub in range(EXPERT_STEP_BLOCKS):
        i = step * EXPERT_STEP_BLOCKS + sub
        rows = pl.ds(sub * MOE_BLOCK, MOE_BLOCK)

        @pl.when(first_ref[i] == 1)
        def _():
            slot = par_ref[i]

            @pl.when(nxt_ref[i] >= 0)
            def _():
                for cp in weight_copies(nxt_ref[i], 1 - slot):
                    cp.start()

            for cp in weight_copies(be_ref[i], slot):
                cp.wait()
            w1b[...] = w1f[slot].astype(BF16)
            w3b[...] = w3f[slot].astype(BF16)
            w2b[...] = w2f[slot].astype(BF16)

        @pl.when(i < nu_ref[0])
        def _():
            x = _unpack_bf16_pairs(x_ref[rows, :])
            a = jnp.dot(x, w1b[...], preferred_element_type=F32)
            b = jnp.dot(x, w3b[...], preferred_element_type=F32)
            hmid = (a * jax.nn.sigmoid(a)) * b
            o_ref[rows, :] = _pack_bf16_pairs(jnp.dot(hmid.astype(BF16), w2b[...], preferred_element_type=F32))

        @pl.when(i >= nu_ref[0])
        def _():
            o_ref[rows, :] = jnp.zeros((MOE_BLOCK, o_ref.shape[1]), o_ref.dtype)


def _experts(xs, nb, block_e, n_used, w1_e, w3_e, w2_e):
    dp = xs.shape[1]
    d, de = w1_e.shape[1], w1_e.shape[2]
    idx = jnp.arange(nb, dtype=jnp.int32)
    used = idx < n_used[0]
    first = used & ((idx == 0) | (block_e != jnp.roll(block_e, 1)))
    ordinal = jnp.cumsum(first.astype(jnp.int32)) - 1
    par = (ordinal % 2).astype(jnp.int32)
    first_pos = jnp.where(first, idx, nb)
    next_first = lax.cummin(jnp.concatenate([first_pos[1:], jnp.full((1,), nb, jnp.int32)]), reverse=True)
    nxt = jnp.where(next_first < nb, block_e[jnp.minimum(next_first, nb - 1)], -1).astype(jnp.int32)
    any_spec = pl.BlockSpec(memory_space=pl.ANY)
    assert nb % EXPERT_STEP_BLOCKS == 0
    step_rows = EXPERT_STEP_BLOCKS * MOE_BLOCK
    grid_spec = pltpu.PrefetchScalarGridSpec(
        num_scalar_prefetch=5,
        grid=(nb // EXPERT_STEP_BLOCKS,),
        in_specs=[pl.BlockSpec((step_rows, dp), lambda i, *_: (i, 0)), any_spec, any_spec, any_spec],
        out_specs=pl.BlockSpec((step_rows, dp), lambda i, *_: (i, 0)),
        scratch_shapes=[pltpu.VMEM((2, d, de), F32), pltpu.VMEM((2, d, de), F32), pltpu.VMEM((2, de, d), F32),
                        pltpu.VMEM((d, de), BF16), pltpu.VMEM((d, de), BF16), pltpu.VMEM((de, d), BF16),
                        pltpu.SemaphoreType.DMA((3, 2))],
    )
    return pl.pallas_call(
        _experts_kernel,
        grid_spec=grid_spec,
        out_shape=jax.ShapeDtypeStruct((nb * MOE_BLOCK, dp), xs.dtype),
        compiler_params=_cparams("arbitrary"),
        name="moe_experts",
    )(block_e, first.astype(jnp.int32), nxt, par, n_used, xs, w1_e, w3_e, w2_e)


SC_WINDOW = 128
SC_CORES, SC_SUBCORES = 2, 16
SC_WORKERS = SC_CORES * SC_SUBCORES


def _sc_worker_id():
    return lax.axis_index("c") * SC_SUBCORES + lax.axis_index("s")


def _sc_mesh():
    return plsc.VectorSubcoreMesh(core_axis_name="c", subcore_axis_name="s")


def _sc_dispatch(rows, dest0, dest1, pad_slots, n_rows):
    n, dv = rows.shape
    nwin, pwin = n // SC_WINDOW, pad_slots.shape[0] // SC_WINDOW
    assert n % (SC_WINDOW * SC_WORKERS) == 0 and pad_slots.shape[0] % (SC_WINDOW * SC_WORKERS) == 0
    zeros = jnp.zeros((SC_WINDOW, dv), rows.dtype)

    @pl.kernel(out_type=jax.ShapeDtypeStruct((n_rows, dv), rows.dtype), mesh=_sc_mesh(),
               scratch_types=[pltpu.VMEM((1, SC_WINDOW), jnp.int32), pltpu.VMEM((SC_WINDOW, dv), rows.dtype)],
               name="moe_dispatch_sc")
    def scatter(x_hbm, d0_hbm, d1_hbm, p_hbm, z_hbm, o_hbm, idx, buf):
        wid = _sc_worker_id()
        pltpu.sync_copy(z_hbm, buf)

        @pl.loop(0, pwin // SC_WORKERS)
        def _(t):
            w = t * SC_WORKERS + wid
            pltpu.sync_copy(p_hbm.at[pl.ds(w, 1)], idx)
            pltpu.sync_copy(buf, o_hbm.at[idx.at[0]])

        @pl.loop(0, nwin // SC_WORKERS)
        def _(t):
            w = t * SC_WORKERS + wid
            pltpu.sync_copy(x_hbm.at[pl.ds(w * SC_WINDOW, SC_WINDOW)], buf)
            for d_hbm in (d0_hbm, d1_hbm):
                pltpu.sync_copy(d_hbm.at[pl.ds(w, 1)], idx)
                pltpu.sync_copy(buf, o_hbm.at[idx.at[0]])

    return scatter(rows, dest0.reshape(nwin, SC_WINDOW), dest1.reshape(nwin, SC_WINDOW),
                   pad_slots.reshape(pwin, SC_WINDOW), zeros)


def _sc_gather(table, index):
    m = index.shape[0]
    dv = table.shape[1]
    nwin = m // SC_WINDOW
    assert m % (SC_WINDOW * SC_WORKERS) == 0

    @pl.kernel(out_type=jax.ShapeDtypeStruct((m, dv), table.dtype), mesh=_sc_mesh(),
               scratch_types=[pltpu.VMEM((1, SC_WINDOW), jnp.int32), pltpu.VMEM((SC_WINDOW, dv), table.dtype)],
               name="moe_gather_sc")
    def gather(x_hbm, i_hbm, o_hbm, idx, buf):
        wid = _sc_worker_id()

        @pl.loop(0, nwin // SC_WORKERS)
        def _(t):
            w = t * SC_WORKERS + wid
            pltpu.sync_copy(i_hbm.at[pl.ds(w, 1)], idx)
            pltpu.sync_copy(x_hbm.at[idx.at[0]], buf)
            pltpu.sync_copy(buf, o_hbm.at[pl.ds(w * SC_WINDOW, SC_WINDOW)])

    return gather(table, index.reshape(nwin, SC_WINDOW))


def _combine_planes_kernel(r_ref, ya_ref, yb_ref, x_ref, gate_ref, gf_ref, o_ref):
    rec = r_ref[...]
    y = (_unpack_bf16_pairs(ya_ref[...]).astype(F32) * rec[:, ROUTE_W1:ROUTE_W1 + 1]
         + _unpack_bf16_pairs(yb_ref[...]).astype(F32) * rec[:, ROUTE_W2:ROUTE_W2 + 1])
    x2 = x_ref[...] + gate_ref[...] * y
    o_ref[...] = x2 * lax.rsqrt(jnp.mean(x2 * x2, axis=-1, keepdims=True) + EPS) * gf_ref[...]


def _combine_planes(g, route, x1, gate2, g_final, tm=COMBINE_TM):
    bsz, L, d = x1.shape
    tpb = L // tm
    dp = g.shape[-1]
    return pl.pallas_call(
        _combine_planes_kernel,
        grid=(bsz, tpb),
        in_specs=[pl.BlockSpec((tm, LANES), lambda b, i: (b * tpb + i, 0)),
                  pl.BlockSpec((None, tm, dp), lambda b, i: (0, b * tpb + i, 0)),
                  pl.BlockSpec((None, tm, dp), lambda b, i: (1, b * tpb + i, 0)),
                  pl.BlockSpec((None, tm, d), lambda b, i: (b, i, 0)),
                  pl.BlockSpec((None, 1, d), lambda b, i: (b, 0, 0)),
                  pl.BlockSpec((1, d), lambda b, i: (0, 0))],
        out_specs=pl.BlockSpec((None, tm, d), lambda b, i: (b, i, 0)),
        out_shape=jax.ShapeDtypeStruct((bsz, L, d), F32),
        compiler_params=_cparams("parallel", "parallel"),
        name="moe_combine",
    )(route, g, g, x1, gate2, g_final.reshape(1, d))


def _moe(h2, x1, gate2, g_final, w_group, b_group, w_router, b_router, w1_e, w3_e, w2_e):
    bsz, L, d = x1.shape
    n = bsz * L
    h2f = h2.reshape(n, h2.shape[-1])
    route, tile_counts = _router(h2f, w_group, b_group, w_router, b_router)
    dest_rec, counts = _slots(route, tile_counts)
    nb = (2 * n) // MOE_BLOCK + N_EXPERTS
    cnt = counts[:N_EXPERTS].astype(jnp.int32)
    blocks_per_e = (cnt + MOE_BLOCK - 1) // MOE_BLOCK
    ends = jnp.cumsum(blocks_per_e)
    block_e = jnp.minimum(jnp.sum(ends[None, :] <= jnp.arange(nb, dtype=jnp.int32)[:, None], axis=1),
                          N_EXPERTS - 1).astype(jnp.int32)
    n_used = ends[-1:].astype(jnp.int32)
    n_slots = nb * MOE_BLOCK
    pad_j = jnp.arange(MOE_BLOCK, dtype=jnp.int32)[None, :]
    spare = n_slots + jnp.arange(N_EXPERTS * MOE_BLOCK, dtype=jnp.int32).reshape(N_EXPERTS, MOE_BLOCK)
    first_slot = ((ends - blocks_per_e) * MOE_BLOCK)[:, None]
    is_pad = cnt[:, None] + pad_j < blocks_per_e[:, None] * MOE_BLOCK
    pad_slots = jnp.where(is_pad, first_slot + cnt[:, None] + pad_j, spare).reshape(-1)
    xs = _sc_dispatch(h2f, dest_rec[:, 0], dest_rec[:, 1], pad_slots, n_slots + N_EXPERTS * MOE_BLOCK)
    ys = _experts(xs, nb, block_e, n_used, w1_e, w3_e, w2_e)
    g = _sc_gather(ys, jnp.concatenate([dest_rec[:, 0], dest_rec[:, 1]]))
    return _combine_planes(g.reshape(2, n, g.shape[-1]), route, x1, gate2, g_final)


def kernel(x, c, ctx, c_ctx, w_mod, b_mod, g_norm1, g_norm2, w_in, b_in, w_qk_conv, b_qk_conv,
           w_h_conv, b_h_conv, hf_w1, hf_b1, hf_w2, hf_b2, hf_w3, hf_freq, h_bias, w_a, w_b, w_out,
           w_group, b_group, w_router, b_router, w1_e, w3_e, w2_e, g_final):
    assert w_mod.shape[0] == 1, "single-layer block"
    (w_mod, b_mod, g_norm1, g_norm2, w_in, b_in, w_qk_conv, b_qk_conv, w_h_conv, b_h_conv, hf_w1, hf_b1, hf_w2,
     hf_b2, hf_w3, hf_freq, h_bias, w_a, w_b, w_out, w_group, b_group, w_router, b_router, w1_e, w3_e, w2_e) = (
        t[0] for t in (w_mod, b_mod, g_norm1, g_norm2, w_in, b_in, w_qk_conv, b_qk_conv, w_h_conv, b_h_conv,
                       hf_w1, hf_b1, hf_w2, hf_b2, hf_w3, hf_freq, h_bias, w_a, w_b, w_out, w_group, b_group,
                       w_router, b_router, w1_e, w3_e, w2_e))
    bsz, L, d = x.shape
    lc = ctx.shape[1]
    seg = L // (L // GRID_W)
    chunk_c = min(lc, MLSTM_CHUNK)
    assert bsz + 1 <= 8 and lc % chunk_c == 0 and L % MLSTM_CHUNK == 0

    cond = jnp.zeros((8, d), F32).at[:bsz].set(c).at[bsz].set(c_ctx)
    mod = _adaln(cond, w_mod, b_mod).reshape(8, 6, d)
    modx = mod[:bsz]
    shift1, scale1, gate1, shift2, scale2, gate2 = (modx[:, i:i + 1] for i in range(6))
    shift1c = jnp.broadcast_to(mod[bsz, 0].reshape(1, 1, d), (bsz, 1, d))
    scale1c = jnp.broadcast_to(mod[bsz, 1].reshape(1, 1, d), (bsz, 1, d))

    w_in16 = w_in.astype(BF16)
    k_scale = jnp.full((M_WIDTH,), M_HEAD_DIM ** -0.5, F32)
    qk_scale = jnp.concatenate([jnp.ones((M_WIDTH,), F32), k_scale])
    w_gates, b_gates = w_in[:, IG0:M_COLS], b_in[IG0:M_COLS]

    hc = _norm_mod(ctx, g_norm1, shift1c, scale1c, lc)
    kc = _proj_conv_silu(hc, w_in16[:, K0:V0], b_in[K0:V0], w_qk_conv[:, M_WIDTH:], b_qk_conv[M_WIDTH:],
                         k_scale, lc, lc)
    vc = _proj_act(hc, w_in16[:, V0:O0], b_in[V0:O0], "none", BF16, lc)
    bcc, acc, arc = _gates(hc, w_gates, b_gates, chunk_c)
    zero_state = (jnp.zeros((bsz, 2, M_HEADS, M_HEAD_DIM, M_HEAD_DIM), F32),
                  jnp.zeros((bsz, 2, M_HEADS, 1, M_HEAD_DIM), F32),
                  jnp.zeros((bsz, 2, M_HEADS, 1, LANES), F32))
    _, ctx_state = _mlstm(None, (kc, 0), (vc, 0), bcc, acc, arc, zero_state, False, chunk_c)

    tm = ROW_TILE
    w_main = jnp.concatenate([w_in16[:, Q0:IG0], w_in16[:, GA0:IN_COLS]], axis=1)
    b_main = jnp.concatenate([b_in[Q0:IG0], b_in[GA0:IN_COLS]])
    _, dft_fast = _dft_factors(2 * L)
    pm, h, h_il = _proj_main(x, g_norm1, shift1, scale1, w_main, b_main, w_qk_conv, b_qk_conv, qk_scale,
                             seg, tm, dft_fast)
    bc, ac, ar = _gates(h, w_gates, b_gates, MLSTM_CHUNK)
    hdirs, _ = _mlstm((pm, PM_Q), (pm, PM_K), (pm, PM_V), bc, ac, ar, ctx_state, True, MLSTM_CHUNK)

    x0_t, s_t = _proj_hyena(h_il, w_in16[:, HY0:GA0], b_in[HY0:GA0], w_h_conv, b_h_conv, seg)
    hy = _hyena_long_conv(s_t, x0_t, h_bias, hf_w1, hf_b1, hf_w2, hf_b2, hf_w3, hf_freq)

    x1, h2 = _merge(hdirs, pm, hy, x, gate1, g_norm2, shift2, scale2,
                    w_a.astype(BF16), w_b.astype(BF16), w_out.astype(BF16))
    return _moe(h2, x1, gate2, g_final, w_group, b_group, w_router, b_router, w1_e, w3_e, w2_e)
```

```python
import functools
import math

import jax
import jax.numpy as jnp
import numpy as np
from jax import lax
from jax.experimental import pallas as pl
from jax.experimental.pallas import tpu as pltpu
from jax.experimental.pallas import tpu_sc as plsc

F32 = jnp.float32
BF16 = jnp.bfloat16

D_MODEL = 1024
GRID_W = 64
EPS = 1e-6
M_HEADS = 4
M_HEAD_DIM = 256
M_WIDTH = M_HEADS * M_HEAD_DIM
H_WIDTH = 1024
H_POS_BANDS = 16
H_FILTER_HIDDEN = 64
H_FAST_DECAY_PCT = 0.3
H_SLOW_DECAY_PCT = 1.5
H_DECAY_TARGET = 1e-2
N_GROUPS = 8
EXPERTS_PER_GROUP = 8
N_EXPERTS = N_GROUPS * EXPERTS_PER_GROUP
D_EXPERT = 512
Q0 = 0
K0 = Q0 + M_WIDTH
V0 = K0 + M_WIDTH
O0 = V0 + M_WIDTH
IG0 = O0 + M_WIDTH
FG0 = IG0 + 2 * M_HEADS
M_COLS = FG0 + 2 * M_HEADS
HY0 = M_COLS
GA0 = HY0 + 3 * H_WIDTH
GB0 = GA0 + D_MODEL
IN_COLS = GB0 + D_MODEL

LANES = 128
MLSTM_CHUNK = 512
NEG_BIG = -1e30
VMEM_LIMIT = 48 * 1024 * 1024
ROW_TILE = 1024
ADALN_TN = 1536
SMALL_TN = 512
MERGE_TM = 512
COMBINE_TM = 512


def _cparams(*sem):
    return pltpu.CompilerParams(dimension_semantics=sem, vmem_limit_bytes=VMEM_LIMIT)


def _adaln_kernel(c_ref, w_ref, b_ref, o_ref):
    s = c_ref[...]
    s = s * jax.nn.sigmoid(s)
    o_ref[...] = jnp.dot(s.astype(BF16), w_ref[...].astype(BF16), preferred_element_type=F32) + b_ref[...]


def _adaln(cond, w_mod, b_mod):
    n = w_mod.shape[1]
    tn = ADALN_TN
    return pl.pallas_call(
        _adaln_kernel,
        grid=(n // tn,),
        in_specs=[pl.BlockSpec((8, D_MODEL), lambda j: (0, 0)),
                  pl.BlockSpec((D_MODEL, tn), lambda j: (0, j)),
                  pl.BlockSpec((1, tn), lambda j: (0, j))],
        out_specs=pl.BlockSpec((8, tn), lambda j: (0, j)),
        out_shape=jax.ShapeDtypeStruct((8, n), F32),
        compiler_params=_cparams("arbitrary"),
        name="adaln",
    )(cond, w_mod, b_mod.reshape(1, n))


def _norm_mod_kernel(x_ref, g_ref, sh_ref, sc_ref, o_ref):
    x = x_ref[...]
    y = x * lax.rsqrt(jnp.mean(x * x, axis=-1, keepdims=True) + EPS)
    y = y * g_ref[...]
    o_ref[...] = (y * (1.0 + sc_ref[...]) + sh_ref[...]).astype(o_ref.dtype)


def _norm_mod(x, g, shift, scale, tm):
    bsz, L, d = x.shape
    return pl.pallas_call(
        _norm_mod_kernel,
        grid=(bsz, L // tm),
        in_specs=[pl.BlockSpec((None, tm, d), lambda b, i: (b, i, 0)),
                  pl.BlockSpec((1, d), lambda b, i: (0, 0)),
                  pl.BlockSpec((None, 1, d), lambda b, i: (b, 0, 0)),
                  pl.BlockSpec((None, 1, d), lambda b, i: (b, 0, 0))],
        out_specs=pl.BlockSpec((None, tm, d), lambda b, i: (b, i, 0)),
        out_shape=jax.ShapeDtypeStruct((bsz, L, d), BF16),
        compiler_params=_cparams("parallel", "parallel"),
        name="norm_mod",
    )(x, g.reshape(1, d), shift, scale)


def _conv3(z, wc, bc, seg):
    tm = z.shape[0]
    pos = lax.broadcasted_iota(jnp.int32, z.shape, 0) & (seg - 1)
    zp = jnp.where(pos == 0, 0.0, pltpu.roll(z, 1, 0))
    zn = jnp.where(pos == seg - 1, 0.0, pltpu.roll(z, tm - 1, 0))
    return zp * wc[0:1, :] + z * wc[1:2, :] + zn * wc[2:3, :] + bc


def _proj_act_kernel(h_ref, w_ref, b_ref, o_ref, *, act):
    z = jnp.dot(h_ref[...], w_ref[...], preferred_element_type=F32) + b_ref[...]
    if act == "sigmoid":
        z = jax.nn.sigmoid(z)
    o_ref[...] = z.astype(o_ref.dtype)


def _proj_act(h, w, b, act, out_dtype, tm, tn=SMALL_TN):
    bsz, L, d = h.shape
    n = w.shape[1]
    return pl.pallas_call(
        functools.partial(_proj_act_kernel, act=act),
        grid=(bsz, L // tm, n // tn),
        in_specs=[pl.BlockSpec((None, tm, d), lambda b_, i, j: (b_, i, 0)),
                  pl.BlockSpec((d, tn), lambda b_, i, j: (0, j)),
                  pl.BlockSpec((1, tn), lambda b_, i, j: (0, j))],
        out_specs=pl.BlockSpec((None, tm, tn), lambda b_, i, j: (b_, i, j)),
        out_shape=jax.ShapeDtypeStruct((bsz, L, n), out_dtype),
        compiler_params=_cparams("parallel", "parallel", "arbitrary"),
        name="proj_" + act,
    )(h, w, b.reshape(1, n))


def _proj_conv_silu_kernel(h_ref, w_ref, b_ref, wc_ref, bc_ref, cs_ref, o_ref, *, seg):
    z = jnp.dot(h_ref[...], w_ref[...], preferred_element_type=F32) + b_ref[...]
    y = _conv3(z, wc_ref[...], bc_ref[...], seg)
    y = y * jax.nn.sigmoid(y)
    o_ref[...] = (y * cs_ref[...]).astype(o_ref.dtype)


def _proj_conv_silu(h, w, b, wc, bc, colscale, seg, tm, tn=SMALL_TN):
    bsz, L, d = h.shape
    n = w.shape[1]
    col = lambda b_, i, j: (0, j)
    return pl.pallas_call(
        functools.partial(_proj_conv_silu_kernel, seg=seg),
        grid=(bsz, L // tm, n // tn),
        in_specs=[pl.BlockSpec((None, tm, d), lambda b_, i, j: (b_, i, 0)),
                  pl.BlockSpec((d, tn), col),
                  pl.BlockSpec((1, tn), col),
                  pl.BlockSpec((3, tn), col),
                  pl.BlockSpec((1, tn), col),
                  pl.BlockSpec((1, tn), col)],
        out_specs=pl.BlockSpec((None, tm, tn), lambda b_, i, j: (b_, i, j)),
        out_shape=jax.ShapeDtypeStruct((bsz, L, n), BF16),
        compiler_params=_cparams("parallel", "parallel", "arbitrary"),
        name="proj_conv_silu",
    )(h, w, b.reshape(1, n), wc, bc.reshape(1, n), colscale.reshape(1, n))


PROJ_TN = 1024
PROJ_SUB = 512
PM_Q, PM_K, PM_V, PM_O, PM_GA, PM_GB = range(6)


def _proj_main_kernel(x_ref, g_ref, sh_ref, sc_ref, w_ref, b_ref, wc_ref, bc_ref, cs_ref,
                      o_ref, h_ref, hi_hbm, hp_sc, sem, *, seg):
    b, i, j = pl.program_id(0), pl.program_id(1), pl.program_id(2)
    n2, jt = hi_hbm.shape[2], hi_hbm.shape[3]

    def interleave_copy(jj):
        return pltpu.make_async_copy(hp_sc.at[pl.ds(jj * n2, n2)], hi_hbm.at[b, i, :, jj, :], sem)

    @pl.when(j == 0)
    def _():
        x = x_ref[...]
        y = x * lax.rsqrt(jnp.mean(x * x, axis=-1, keepdims=True) + EPS) * g_ref[...]
        y = y * (1.0 + sc_ref[...]) + sh_ref[...]
        h_ref[...] = y.astype(h_ref.dtype)
        hp_sc[...] = _pack_bf16_pairs(y)
        for jj in range(jt):
            interleave_copy(jj).start()

    @pl.when(j == pl.num_programs(2) - 1)
    def _():
        for jj in range(jt):
            interleave_copy(jj).wait()

    def run(epilogue):
        for c in range(PROJ_TN // PROJ_SUB):
            sl = slice(c * PROJ_SUB, (c + 1) * PROJ_SUB)
            z = jnp.dot(h_ref[...], w_ref[:, sl], preferred_element_type=F32) + b_ref[:, sl]
            o_ref[:, sl] = epilogue(z, sl).astype(o_ref.dtype)

    def conv_silu(z, sl):
        y = _conv3(z, wc_ref[:, sl], bc_ref[:, sl], seg)
        return (y * jax.nn.sigmoid(y)) * cs_ref[:, sl]

    @pl.when(j <= PM_K)
    def _():
        run(conv_silu)

    @pl.when(j == PM_V)
    def _():
        run(lambda z, sl: z)

    @pl.when(j >= PM_O)
    def _():
        run(lambda z, sl: jax.nn.sigmoid(z))


def _proj_main(x, g, shift, scale, w, b, wc, bc, colscale, seg, tm, n2):
    bsz, L, d = x.shape
    n = w.shape[1]
    jt = tm // n2
    qk = lambda b_, i, j: (0, jnp.minimum(j, PM_K))
    row = pl.BlockSpec((None, tm, d), lambda b_, i, j: (b_, i, 0))
    bvec = pl.BlockSpec((None, 1, d), lambda b_, i, j: (b_, 0, 0))
    return pl.pallas_call(
        functools.partial(_proj_main_kernel, seg=seg),
        grid=(bsz, L // tm, n // PROJ_TN),
        in_specs=[row, pl.BlockSpec((1, d), lambda b_, i, j: (0, 0)), bvec, bvec,
                  pl.BlockSpec((d, PROJ_TN), lambda b_, i, j: (0, j)),
                  pl.BlockSpec((1, PROJ_TN), lambda b_, i, j: (0, j)),
                  pl.BlockSpec((3, PROJ_TN), qk),
                  pl.BlockSpec((1, PROJ_TN), qk),
                  pl.BlockSpec((1, PROJ_TN), qk)],
        out_specs=[pl.BlockSpec((None, tm, PROJ_TN), lambda b_, i, j: (b_, i, j)), row,
                   pl.BlockSpec(memory_space=pl.ANY)],
        out_shape=[jax.ShapeDtypeStruct((bsz, L, n), BF16), jax.ShapeDtypeStruct((bsz, L, d), BF16),
                   jax.ShapeDtypeStruct((bsz, L // tm, n2, jt, d // 2), jnp.uint32)],
        scratch_shapes=[pltpu.VMEM((tm, d // 2), jnp.uint32), pltpu.SemaphoreType.DMA(())],
        compiler_params=_cparams("parallel", "parallel", "arbitrary"),
        name="proj_main",
    )(x, g.reshape(1, d), shift, scale, w, b.reshape(1, n), wc, bc.reshape(1, -1), colscale.reshape(1, -1))


def _conv3_interleaved(z, wc, bc, seg, jt):
    grp = seg * jt
    pad = jnp.zeros((jt, z.shape[1]), z.dtype)
    prev, nxt = [], []
    for g0 in range(0, z.shape[0], grp):
        zg = z[g0:g0 + grp]
        prev += [pad, zg[:grp - jt]]
        nxt += [zg[jt:], pad]
    zp = jnp.concatenate(prev, axis=0)
    zn = jnp.concatenate(nxt, axis=0)
    return zp * wc[0:1, :] + z * wc[1:2, :] + zn * wc[2:3, :] + bc


def _proj_hyena_kernel(h_ref, w0_ref, w1_ref, w2_ref, b_ref, wc_ref, bc_ref, x0_ref, s_ref, *, seg):
    n2, jt = s_ref.shape[0], s_ref.shape[1]
    h = _unpack_bf16_pairs(h_ref[...].reshape(n2 * jt, h_ref.shape[2]))
    us = []
    for g, w_ref in enumerate((w0_ref, w1_ref, w2_ref)):
        z = jnp.dot(h, w_ref[...], preferred_element_type=F32) + b_ref[g]
        us.append(_conv3_interleaved(z, wc_ref[g], bc_ref[g], seg, jt))
    x0_ref[...] = _pack_bf16_pairs(us[0]).reshape(x0_ref.shape)
    s_ref[...] = (us[1] * us[2]).reshape(s_ref.shape)


def _proj_hyena(hi, w, b, wc, bc, seg):
    bsz, nt, n2, jt, dp = hi.shape
    d, tm = 2 * dp, n2 * jt
    L = nt * tm
    tn = DFT_C_TILE
    nblk = H_WIDTH // tn
    assert n2 % seg == 0 and (jt % 8 == 0 or nt == 1)
    b3 = b.reshape(3, 1, H_WIDTH)
    wc3 = wc.reshape(3, 3, H_WIDTH).transpose(1, 0, 2)
    bc3 = bc.reshape(3, 1, H_WIDTH)
    return pl.pallas_call(
        functools.partial(_proj_hyena_kernel, seg=seg),
        grid=(bsz, nt, nblk),
        in_specs=[pl.BlockSpec((None, None, n2, jt, dp), lambda b_, i, j: (b_, i, 0, 0, 0)),
                  pl.BlockSpec((d, tn), lambda b_, i, j: (0, j)),
                  pl.BlockSpec((d, tn), lambda b_, i, j: (0, nblk + j)),
                  pl.BlockSpec((d, tn), lambda b_, i, j: (0, 2 * nblk + j)),
                  pl.BlockSpec((3, 1, tn), lambda b_, i, j: (0, 0, j)),
                  pl.BlockSpec((3, 3, tn), lambda b_, i, j: (0, 0, j)),
                  pl.BlockSpec((3, 1, tn), lambda b_, i, j: (0, 0, j))],
        out_specs=[pl.BlockSpec((None, n2, jt, tn // 2), lambda b_, i, j: (b_, 0, i, j)),
                   pl.BlockSpec((None, n2, jt, tn), lambda b_, i, j: (b_, 0, i, j))],
        out_shape=[jax.ShapeDtypeStruct((bsz, n2, L // n2, H_WIDTH // 2), jnp.uint32),
                   jax.ShapeDtypeStruct((bsz, n2, L // n2, H_WIDTH), F32)],
        compiler_params=_cparams("parallel", "parallel", "arbitrary"),
        name="proj_hyena",
    )(hi, w, w, w, b3, wc3, bc3)


N_GATES = 4 * M_HEADS


def _split3(x):
    hi = x.astype(BF16)
    r1 = x - hi.astype(F32)
    mid = r1.astype(BF16)
    lo = (r1 - mid.astype(F32)).astype(BF16)
    return hi, mid, lo


def _log_sigmoid(x):
    return jnp.minimum(x, 0.0) - jnp.log1p(jnp.exp(-jnp.abs(x)))


def _gates_kernel(h_ref, w_ref, wt_ref, b_ref, bt_ref, bc_ref, ac_ref, ar_ref):
    h = h_ref[...]
    t = h.shape[0]
    z = jnp.dot(h, w_ref[...], preferred_element_type=F32) + b_ref[...]
    zt = lax.dot_general(wt_ref[...], h, (((1,), (1,)), ((), ())),
                         preferred_element_type=F32) + bt_ref[...]
    r = lax.broadcasted_iota(jnp.int32, (t, t), 0)
    c = lax.broadcasted_iota(jnp.int32, (t, t), 1)
    lower = (r >= c).astype(BF16)
    upper = (r <= c).astype(BF16)
    g8 = FG_LANE0

    lf = _log_sigmoid(z)
    lane = lax.broadcasted_iota(jnp.int32, z.shape, 1)
    is_fg = (lane >= g8) & (lane < 2 * g8)
    terms = [jnp.where(is_fg, p.astype(F32), 0.0) for p in _split3(lf)]
    packed = terms[0] + pltpu.roll(terms[1], 2 * g8, 1) + pltpu.roll(terms[2], 4 * g8, 1)
    cfp = jnp.dot(lower, packed.astype(BF16), preferred_element_type=F32)
    cf = cfp + pltpu.roll(cfp, LANES - 2 * g8, 1) + pltpu.roll(cfp, LANES - 4 * g8, 1)
    cb = cf[t - 1:t, :] - cf + lf
    bc = jnp.where(lane < g8 + M_HEADS, cf, cb)
    bc = pltpu.roll(bc, LANES - g8, 1)
    bc_ref[...] = bc
    ac_ref[...] = z - bc

    lft = _log_sigmoid(zt[g8:, :])
    stacked = jnp.concatenate([p.astype(F32) for p in _split3(lft)] + [jnp.zeros_like(lft)], axis=0)
    cft3 = jnp.dot(stacked.astype(BF16), upper, preferred_element_type=F32)
    cft = cft3[0:g8] + cft3[g8:2 * g8] + cft3[2 * g8:3 * g8]
    cbt = cft[:, t - 1:t] - cft + lft
    row = lax.broadcasted_iota(jnp.int32, cft.shape, 0)
    ar_ref[...] = zt[:g8, :] - jnp.where(row < M_HEADS, cft, cbt)


FG_LANE0 = 2 * M_HEADS


def _gates(h, w_g, b_g, chunk):
    bsz, L, d = h.shape
    w_pad = jnp.zeros((d, LANES), F32).at[:, :N_GATES].set(w_g).astype(BF16)
    b_pad = jnp.zeros((1, LANES), F32).at[0, :N_GATES].set(b_g)
    wt = w_g.T.astype(BF16)
    bt = b_g.reshape(N_GATES, 1)
    tok = pl.BlockSpec((None, chunk, LANES), lambda b_, i: (b_, i, 0))
    return pl.pallas_call(
        _gates_kernel,
        grid=(bsz, L // chunk),
        in_specs=[pl.BlockSpec((None, chunk, d), lambda b_, i: (b_, i, 0)),
                  pl.BlockSpec((d, LANES), lambda b_, i: (0, 0)),
                  pl.BlockSpec((N_GATES, d), lambda b_, i: (0, 0)),
                  pl.BlockSpec((1, LANES), lambda b_, i: (0, 0)),
                  pl.BlockSpec((N_GATES, 1), lambda b_, i: (0, 0))],
        out_specs=[tok, tok, pl.BlockSpec((None, FG_LANE0, chunk), lambda b_, i: (b_, 0, i))],
        out_shape=[jax.ShapeDtypeStruct((bsz, L, LANES), F32),
                   jax.ShapeDtypeStruct((bsz, L, LANES), F32),
                   jax.ShapeDtypeStruct((bsz, FG_LANE0, L), F32)],
        compiler_params=_cparams("parallel", "parallel"),
        name="mlstm_gates",
    )(h, w_pad, wt, b_pad, bt)


def _mlstm_kernel(*refs, emit_h, n_chunks):
    if emit_h:
        (q_ref, k_ref, v_ref, bc_ref, ac_ref, ar_ref, c0_ref, n0_ref, m0_ref,
         h_ref, cf_ref, nf_ref, mf_ref, c_sc, n_sc, m_sc) = refs
    else:
        (k_ref, v_ref, bc_ref, ac_ref, ar_ref, c0_ref, n0_ref, m0_ref,
         cf_ref, nf_ref, mf_ref, c_sc, n_sc, m_sc) = refs
    d = pl.program_id(1)
    j = pl.program_id(2)
    fwd = d == 0
    t = k_ref.shape[0]
    dh = M_HEAD_DIM

    @pl.when(j == 0)
    def _():
        c_sc[...] = c0_ref[...]
        n_sc[...] = n0_ref[...]
        m_sc[...] = m0_ref[...]

    r = lax.broadcasted_iota(jnp.int32, (t, t), 0)
    c = lax.broadcasted_iota(jnp.int32, (t, t), 1)
    causal = jnp.where(fwd, r - c, c - r) >= 0
    bc_all = bc_ref[...]
    ac_all = ac_ref[...]
    ar_all = ar_ref[...]
    for hd in range(M_HEADS):
        sl = slice(hd * dh, (hd + 1) * dh)
        bc = jnp.where(fwd, bc_all[:, hd:hd + 1], bc_all[:, M_HEADS + hd:M_HEADS + hd + 1])
        ac = jnp.where(fwd, ac_all[:, hd:hd + 1], ac_all[:, M_HEADS + hd:M_HEADS + hd + 1])
        ar = jnp.where(fwd, ar_all[hd:hd + 1, :], ar_all[M_HEADS + hd:M_HEADS + hd + 1, :])
        b_tot = jnp.where(fwd, bc[t - 1:t, :], bc[0:1, :])
        m_prev = m_sc[hd][:, 0:1]
        k_h = k_ref[:, sl]
        v_h = v_ref[:, sl]
        if emit_h:
            q_h = q_ref[:, sl]
            dm = jnp.where(causal, bc + ar, NEG_BIG)
            inter = bc + m_prev
            m_t = jnp.maximum(inter, jnp.max(dm, axis=1, keepdims=True))
            qk = lax.dot_general(q_h, k_h, (((1,), (1,)), ((), ())), preferred_element_type=F32)
            s = qk * jnp.exp(dm - m_t)
            carry = jnp.exp(inter - m_t)
            num = (jnp.dot(s.astype(BF16), v_h, preferred_element_type=F32)
                   + carry * jnp.dot(q_h, c_sc[hd].astype(BF16), preferred_element_type=F32))
            den = (jnp.sum(s, axis=1, keepdims=True)
                   + carry * jnp.sum(q_h.astype(F32) * n_sc[hd], axis=1, keepdims=True))
            h_ref[:, sl] = (num / jnp.maximum(jnp.abs(den), jnp.exp(-m_t))).astype(h_ref.dtype)
        g = b_tot + ac
        m_new = jnp.maximum(b_tot + m_prev, jnp.max(g, axis=0, keepdims=True))
        wgt = jnp.exp(g - m_new)
        decay = jnp.exp(b_tot + m_prev - m_new)
        kw = k_h.astype(F32) * wgt
        c_sc[hd] = decay * c_sc[hd] + lax.dot_general(kw.astype(BF16), v_h, (((0,), (0,)), ((), ())),
                                                      preferred_element_type=F32)
        n_sc[hd] = decay * n_sc[hd] + jnp.sum(kw, axis=0, keepdims=True)
        m_sc[hd] = jnp.broadcast_to(m_new, (1, LANES))

    @pl.when(j == n_chunks - 1)
    def _():
        cf_ref[...] = c_sc[...]
        nf_ref[...] = n_sc[...]
        mf_ref[...] = m_sc[...]


def _mlstm(q, k, v, bc, ac, ar, state, emit_h, t):
    bsz, L, _ = k[0].shape
    nc = L // t
    seq = lambda b_, d, j: (b_, j + d * (nc - 1 - 2 * j), 0)
    st = lambda b_, d, j: (b_, d, 0, 0, 0)

    def tok(col):
        return pl.BlockSpec((None, t, M_WIDTH), lambda b_, d, j: (b_, j + d * (nc - 1 - 2 * j), col))

    gate_spec = pl.BlockSpec((None, t, LANES), seq)
    ar_spec = pl.BlockSpec((None, FG_LANE0, t), lambda b_, d, j: (b_, 0, j + d * (nc - 1 - 2 * j)))
    c_spec = pl.BlockSpec((None, None, M_HEADS, M_HEAD_DIM, M_HEAD_DIM), st)
    n_spec = pl.BlockSpec((None, None, M_HEADS, 1, M_HEAD_DIM), st)
    m_spec = pl.BlockSpec((None, None, M_HEADS, 1, LANES), st)
    state_shapes = [jax.ShapeDtypeStruct((bsz, 2, M_HEADS, M_HEAD_DIM, M_HEAD_DIM), F32),
                    jax.ShapeDtypeStruct((bsz, 2, M_HEADS, 1, M_HEAD_DIM), F32),
                    jax.ShapeDtypeStruct((bsz, 2, M_HEADS, 1, LANES), F32)]
    in_specs = [tok(k[1]), tok(v[1]), gate_spec, gate_spec, ar_spec, c_spec, n_spec, m_spec]
    args = [k[0], v[0], bc, ac, ar, *state]
    out_specs = [c_spec, n_spec, m_spec]
    out_shape = list(state_shapes)
    if emit_h:
        in_specs = [tok(q[1])] + in_specs
        args = [q[0]] + args
        out_specs = [pl.BlockSpec((None, None, t, M_WIDTH),
                                  lambda b_, d, j: (d, b_, j + d * (nc - 1 - 2 * j), 0))] + out_specs
        out_shape = [jax.ShapeDtypeStruct((2, bsz, L, M_WIDTH), BF16)] + out_shape
    outs = pl.pallas_call(
        functools.partial(_mlstm_kernel, emit_h=emit_h, n_chunks=nc),
        grid=(bsz, 2, nc),
        in_specs=in_specs,
        out_specs=out_specs,
        out_shape=out_shape,
        scratch_shapes=[pltpu.VMEM((M_HEADS, M_HEAD_DIM, M_HEAD_DIM), F32),
                        pltpu.VMEM((M_HEADS, 1, M_HEAD_DIM), F32),
                        pltpu.VMEM((M_HEADS, 1, LANES), F32)],
        compiler_params=_cparams("parallel", "parallel", "arbitrary"),
        name="mlstm" if emit_h else "mlstm_state",
    )(*args)
    if emit_h:
        return outs[0], tuple(outs[1:])
    return None, tuple(outs)


DFT_M_TILE = 8
DFT_C_TILE = 1024
DFT_INNER_C_TILE = 512
FEAT_ROWS = 16


def _filter_outer_kernel(bands_ref, w1t_ref, b1_ref, w2t_ref, b2_ref, w3p_ref, w3f_ref, fr_ref, dl_ref, l_ref,
                         a_ref, ss_ref, *, L, n1, n2):
    i = pl.program_id(0)
    h = n1 // 2
    cols = DFT_M_TILE * h

    def positions(shape, axis, side):
        q = lax.broadcasted_iota(jnp.int32, shape, axis)
        mm, jj = q // h, q % h
        n = n2 * (jj + side * h) + i * DFT_M_TILE + mm
        return n, jnp.where(n < L, n, 2 * L - n).astype(F32)

    taps = []
    sumsq = jnp.zeros((1, a_ref.shape[-1]), F32)
    for side, w3_ref in ((0, w3p_ref), (1, w3f_ref)):
        _, p_row = positions((1, cols), 1, side)
        t_row = p_row / float(max(L - 1, 1))
        ang = ((2 * math.pi / L) * p_row) * bands_ref[...]
        row = lax.broadcasted_iota(jnp.int32, (FEAT_ROWS, cols), 0)
        feats = jnp.concatenate([jnp.where(row == 0, t_row, 0.0), jnp.cos(ang), -jnp.sin(ang)], axis=0)
        fr = fr_ref[...]
        hid = jnp.sin(fr * (jnp.dot(w1t_ref[...], feats.astype(BF16), preferred_element_type=F32) + b1_ref[...]))
        hid = jnp.sin(fr * (jnp.dot(w2t_ref[...], hid.astype(BF16), preferred_element_type=F32) + b2_ref[...]))
        filt = lax.dot_general(hid.astype(BF16), w3_ref[...], (((0,), (0,)), ((), ())),
                               preferred_element_type=F32)
        n_col, p_col = positions((cols, 1), 0, side)
        t_col = p_col / float(max(L - 1, 1))
        kern = filt * jnp.exp(-t_col * jnp.abs(dl_ref[...]))
        kern = jnp.where(n_col == L, 0.0, kern)
        sumsq = sumsq + jnp.sum(kern * kern, axis=0, keepdims=True)
        taps.append(kern)

    for mm in range(DFT_M_TILE):
        x = jnp.concatenate([taps[0][mm * h:(mm + 1) * h], taps[1][mm * h:(mm + 1) * h]], axis=0)
        out = jnp.dot(l_ref[...], x.astype(BF16), preferred_element_type=F32)
        a_ref[0, :, mm, :] = out[:n1]
        a_ref[1, :, mm, :] = out[n1:]

    @pl.when(i == 0)
    def _():
        ss_ref[...] = jnp.zeros_like(ss_ref)

    ss_ref[...] += sumsq


def _filter_outer(L, n1, n2, fwd_r, w1, b1, w2, b2, w3, freq):
    hid = H_FILTER_HIDDEN
    bands = jnp.linspace(1e-4, H_POS_BANDS - 1, H_POS_BANDS, dtype=F32).reshape(H_POS_BANDS, 1)
    w1t = jnp.zeros((hid, 3 * FEAT_ROWS), F32)
    w1t = w1t.at[:, 0].set(w1[0]).at[:, FEAT_ROWS:2 * FEAT_ROWS].set(w1[1:1 + H_POS_BANDS].T)
    w1t = w1t.at[:, 2 * FEAT_ROWS:].set(w1[1 + H_POS_BANDS:].T).astype(BF16)
    w3h = w3.astype(BF16)
    max_decay = math.log(H_DECAY_TARGET) / H_FAST_DECAY_PCT
    min_decay = math.log(H_DECAY_TARGET) / H_SLOW_DECAY_PCT
    deltas = jnp.linspace(min_decay, max_decay, H_WIDTH, dtype=F32).reshape(1, H_WIDTH)
    col = lambda v: v.reshape(hid, 1)
    full = lambda a: pl.BlockSpec(a.shape, lambda i: (0,) * a.ndim)
    args = [bands, w1t, col(b1), w2.T.astype(BF16), col(b2)]
    return pl.pallas_call(
        functools.partial(_filter_outer_kernel, L=L, n1=n1, n2=n2),
        grid=(n2 // DFT_M_TILE,),
        in_specs=[full(a) for a in args]
        + [pl.BlockSpec((hid, H_WIDTH), lambda i: (0, 0)), pl.BlockSpec((hid, H_WIDTH), lambda i: (0, 1)),
           full(col(freq)), full(deltas), full(fwd_r)],
        out_specs=[pl.BlockSpec((2, n1, DFT_M_TILE, H_WIDTH), lambda i: (0, 0, i, 0)),
                   pl.BlockSpec((1, H_WIDTH), lambda i: (0, 0))],
        out_shape=[jax.ShapeDtypeStruct((2, n1, n2, H_WIDTH), F32),
                   jax.ShapeDtypeStruct((1, H_WIDTH), F32)],
        compiler_params=_cparams("arbitrary"),
        name="hyena_filter_outer",
    )(*args, w3h, w3h, col(freq), deltas, fwd_r)


def _dft_factors(n):
    lg = int(round(math.log2(n)))
    n1 = 1 << ((lg + 1) // 2)
    return n1, n // n1


def _dft_outer_matrices(n1):
    k = np.arange(n1)[:, None]
    n = np.arange(n1)[None, :]
    ang = 2.0 * np.pi * ((k * n) % n1) / n1
    cr, ci = np.cos(ang), -np.sin(ang)
    h = n1 // 2
    fwd_c = np.block([[cr[:, :h], -ci[:, :h]], [ci[:, :h], cr[:, :h]]])
    fwd_r = np.concatenate([cr, ci], axis=0)
    ir, ii = cr[:h, :], -ci[:h, :]
    inv = np.block([[ir, -ii], [ii, ir]])
    return (jnp.asarray(fwd_c, F32).astype(BF16), jnp.asarray(fwd_r, F32).astype(BF16),
            jnp.asarray(inv, F32).astype(BF16))


def _dft_inner_matrices(n1, n2):
    n = n1 * n2
    k2 = np.arange(n2)[:, None]
    m = np.arange(n2)[None, :]
    ang = 2.0 * np.pi * ((k2 * m) % n2) / n2
    fr, fi = np.cos(ang), -np.sin(ang)
    f = np.block([[fr, -fi], [fi, fr]])
    k1 = jnp.arange(n1, dtype=jnp.int32)[:, None]
    tw_ang = ((jnp.arange(n2, dtype=jnp.int32)[None, :] * k1) % n).astype(F32) * (2.0 * math.pi / n)
    rep = lambda t: jnp.broadcast_to(t[:, :, None], (n1, n2, LANES))
    return (jnp.asarray(f, F32).astype(BF16), jnp.asarray(f.T, F32).astype(BF16),
            rep(jnp.cos(tw_ang)), rep(-jnp.sin(tw_ang)))


def _outer_fwd_kernel(l_ref, s_ref, a_ref):
    n1 = a_ref.shape[1]
    for mm in range(s_ref.shape[1]):
        x = jnp.concatenate([s_ref[0, mm], s_ref[1, mm]], axis=0).astype(BF16)
        out = jnp.dot(l_ref[...], x, preferred_element_type=F32)
        a_ref[0, :, mm, :] = out[:n1]
        a_ref[1, :, mm, :] = out[n1:]


def _outer_fwd(lmat, s_t):
    _, n2, n1h, c = s_t.shape
    n1 = 2 * n1h
    tc = min(DFT_C_TILE, c)
    return pl.pallas_call(
        _outer_fwd_kernel,
        grid=(n2 // DFT_M_TILE, c // tc),
        in_specs=[pl.BlockSpec(lmat.shape, lambda m, j: (0, 0)),
                  pl.BlockSpec((2, DFT_M_TILE, n1h, tc), lambda m, j: (0, m, 0, j))],
        out_specs=pl.BlockSpec((2, n1, DFT_M_TILE, tc), lambda m, j: (0, 0, m, j)),
        out_shape=jax.ShapeDtypeStruct((2, n1, n2, c), F32),
        compiler_params=_cparams("parallel", "parallel"),
        name="dft_outer_fwd",
    )(lmat, s_t)


def _outer_inv_kernel(l_ref, b_ref, s_ref, x0_ref, ysc_ref, hb_ref, o_ref):
    n1h = s_ref.shape[2]
    for mm in range(b_ref.shape[1]):
        y = jnp.concatenate([b_ref[0, mm], b_ref[1, mm]], axis=0).astype(BF16)
        out = jnp.dot(l_ref[...], y, preferred_element_type=F32)
        for b in range(2):
            conv = out[b * n1h:(b + 1) * n1h]
            x0 = _unpack_bf16_pairs(x0_ref[b, mm]).astype(F32)
            hy = x0 * (conv * ysc_ref[...] + hb_ref[...] * s_ref[b, mm])
            o_ref[b, :, mm, :] = _pack_bf16_pairs(hy)


def _outer_inv(lmat, b_t, s_t, x0_t, yscale, h_bias):
    _, n2, n1, c = b_t.shape
    n1h = n1 // 2
    tc = min(DFT_C_TILE, c)
    vec = pl.BlockSpec((1, tc), lambda m, j: (0, j))
    hy = pl.pallas_call(
        _outer_inv_kernel,
        grid=(n2 // DFT_M_TILE, c // tc),
        in_specs=[pl.BlockSpec(lmat.shape, lambda m, j: (0, 0)),
                  pl.BlockSpec((2, DFT_M_TILE, n1, tc), lambda m, j: (0, m, 0, j)),
                  pl.BlockSpec((2, DFT_M_TILE, n1h, tc), lambda m, j: (0, m, 0, j)),
                  pl.BlockSpec((2, DFT_M_TILE, n1h, tc // 2), lambda m, j: (0, m, 0, j)),
                  vec, vec],
        out_specs=pl.BlockSpec((2, n1h, DFT_M_TILE, tc // 2), lambda m, j: (0, 0, m, j)),
        out_shape=jax.ShapeDtypeStruct((2, n1h, n2, c // 2), jnp.uint32),
        compiler_params=_cparams("parallel", "parallel"),
        name="dft_outer_inv",
    )(lmat, b_t, s_t, x0_t, yscale, h_bias.reshape(1, c))
    return hy.reshape(2, n1h * n2, c // 2)


DFT_K_TILE = 8


def _twiddled_inner_dft(f_ref, twr_ref, twi_ref, a_ref, kk):
    n2, c = a_ref.shape[2], a_ref.shape[3]
    twr = jnp.tile(twr_ref[kk], (1, c // LANES))
    twi = jnp.tile(twi_ref[kk], (1, c // LANES))
    ar, ai = a_ref[0, kk], a_ref[1, kk]
    a = jnp.concatenate([(ar * twr - ai * twi).astype(BF16), (ar * twi + ai * twr).astype(BF16)], axis=0)
    x = jnp.dot(f_ref[...], a, preferred_element_type=F32)
    return x[:n2], x[n2:], twr, twi


def _inner_fwd_kernel(f_ref, twr_ref, twi_ref, a_ref, o_ref):
    for kk in range(a_ref.shape[1]):
        xr, xi, _, _ = _twiddled_inner_dft(f_ref, twr_ref, twi_ref, a_ref, kk)
        o_ref[0, kk] = xr.astype(o_ref.dtype)
        o_ref[1, kk] = xi.astype(o_ref.dtype)


def _inner_specs(n1, n2, c):
    tc = min(DFT_INNER_C_TILE, c)
    kt = min(DFT_K_TILE, n1)
    blk = pl.BlockSpec((2, kt, n2, tc), lambda k, j: (0, k, 0, j))
    mat = pl.BlockSpec((2 * n2, 2 * n2), lambda k, j: (0, 0))
    tw = pl.BlockSpec((kt, n2, LANES), lambda k, j: (k, 0, 0))
    return blk, mat, tw, (n1 // kt, c // tc), kt, tc


def _inner_fwd(f, twr, twi, a):
    _, n1, n2, c = a.shape
    blk, mat, tw, grid, _, _ = _inner_specs(n1, n2, c)
    return pl.pallas_call(
        _inner_fwd_kernel,
        grid=grid,
        in_specs=[mat, tw, tw, blk],
        out_specs=blk,
        out_shape=jax.ShapeDtypeStruct((2, n1, n2, c), BF16),
        compiler_params=_cparams("parallel", "parallel"),
        name="dft_inner_filter",
    )(f, twr, twi, a)


def _inner_conv_kernel(f_ref, ft_ref, twr_ref, twi_ref, a_ref, k_ref, o_ref):
    n2 = a_ref.shape[2]
    for kk in range(a_ref.shape[1]):
        xr, xi, twr, twi = _twiddled_inner_dft(f_ref, twr_ref, twi_ref, a_ref, kk)
        kr, ki = k_ref[0, kk].astype(F32), k_ref[1, kk].astype(F32)
        yr = xr * kr - xi * ki
        yi = xr * ki + xi * kr
        y = jnp.concatenate([yr.astype(BF16), yi.astype(BF16)], axis=0)
        b = jnp.dot(ft_ref[...], y, preferred_element_type=F32)
        br, bi = b[:n2], b[n2:]
        o_ref[0, :, kk, :] = br * twr + bi * twi
        o_ref[1, :, kk, :] = bi * twr - br * twi


def _inner_conv(f, ft, twr, twi, a, kf):
    _, n1, n2, c = a.shape
    blk, mat, tw, grid, kt, tc = _inner_specs(n1, n2, c)
    return pl.pallas_call(
        _inner_conv_kernel,
        grid=grid,
        in_specs=[mat, mat, tw, tw, blk, blk],
        out_specs=pl.BlockSpec((2, n2, kt, tc), lambda k, j: (0, 0, k, j)),
        out_shape=jax.ShapeDtypeStruct((2, n2, n1, c), F32),
        compiler_params=_cparams("parallel", "parallel"),
        name="dft_inner_conv",
    )(f, ft, twr, twi, a, kf)


def _hyena_long_conv(s_t, x0_t, h_bias, w1, b1, w2, b2, w3, freq):
    bsz, n2, n1h, c = s_t.shape
    assert bsz == 2
    n1 = 2 * n1h
    L = n1h * n2
    fwd_c, fwd_r, inv = _dft_outer_matrices(n1)
    f, ft, twr, twi = _dft_inner_matrices(n1, n2)
    af, sumsq = _filter_outer(L, n1, n2, fwd_r, w1, b1, w2, b2, w3, freq)
    kf = _inner_fwd(f, twr, twi, af)
    a = _outer_fwd(fwd_c, s_t)
    b_t = _inner_conv(f, ft, twr, twi, a, kf)
    yscale = lax.rsqrt(sumsq + EPS) * (1.0 / (2 * L))
    return _outer_inv(inv, b_t, s_t, x0_t, yscale, h_bias)


def _pack_bf16_pairs(x):
    half = x.shape[1] // 2
    lo = pltpu.bitcast(x[:, :half].astype(BF16).astype(F32), jnp.uint32) >> 16
    hi = pltpu.bitcast(x[:, half:].astype(BF16).astype(F32), jnp.uint32) & jnp.uint32(0xFFFF0000)
    return lo | hi


def _unpack_bf16_pairs(p):
    lo = pltpu.bitcast(p << 16, F32).astype(BF16)
    hi = pltpu.bitcast(p & jnp.uint32(0xFFFF0000), F32).astype(BF16)
    return jnp.concatenate([lo, hi], axis=1)


def _merge_kernel(hf_ref, hb_ref, o_ref, hy_ref, ga_ref, gb_ref, x_ref,
                  gate_ref, g2_ref, sh_ref, sc_ref, wa_ref, wb_ref, wo_ref, x1_ref, h2_ref):
    a = o_ref[...].astype(F32) * (hf_ref[...].astype(F32) + hb_ref[...].astype(F32))
    half = DFT_C_TILE // 2
    hy = jnp.concatenate([_unpack_bf16_pairs(hy_ref[:, c * half:(c + 1) * half])
                          for c in range(hy_ref.shape[1] // half)], axis=1)
    pa = jnp.dot(a.astype(BF16), wa_ref[...], preferred_element_type=F32)
    pb = jnp.dot(hy, wb_ref[...], preferred_element_type=F32)
    mix = ga_ref[...].astype(F32) * pa + gb_ref[...].astype(F32) * pb
    out = jnp.dot(mix.astype(BF16), wo_ref[...], preferred_element_type=F32)
    x1 = x_ref[...] + gate_ref[...] * out
    x1_ref[...] = x1
    y = x1 * lax.rsqrt(jnp.mean(x1 * x1, axis=-1, keepdims=True) + EPS) * g2_ref[...]
    h2_ref[...] = _pack_bf16_pairs(y * (1.0 + sc_ref[...]) + sh_ref[...])


def _merge(hdirs, pm, hy, x, gate1, g2, shift2, scale2, w_a, w_b, w_out, tm=MERGE_TM):
    bsz, L, d = x.shape
    tok = pl.BlockSpec((None, tm, d), lambda b, i: (b, i, 0))

    def pm_tile(col):
        return pl.BlockSpec((None, tm, d), lambda b, i: (b, i, col))

    packed = pl.BlockSpec((None, tm, d // 2), lambda b, i: (b, i, 0))
    vec = pl.BlockSpec((1, d), lambda b, i: (0, 0))
    bvec = pl.BlockSpec((None, 1, d), lambda b, i: (b, 0, 0))
    wsp = pl.BlockSpec((d, d), lambda b, i: (0, 0), pipeline_mode=pl.Buffered(1))
    return pl.pallas_call(
        _merge_kernel,
        grid=(bsz, L // tm),
        in_specs=[pl.BlockSpec((None, None, tm, d), lambda b, i: (0, b, i, 0)),
                  pl.BlockSpec((None, None, tm, d), lambda b, i: (1, b, i, 0)),
                  pm_tile(PM_O), packed, pm_tile(PM_GA), pm_tile(PM_GB), tok,
                  bvec, vec, bvec, bvec, wsp, wsp, wsp],
        out_specs=[tok, packed],
        out_shape=[jax.ShapeDtypeStruct((bsz, L, d), F32), jax.ShapeDtypeStruct((bsz, L, d // 2), jnp.uint32)],
        compiler_params=_cparams("parallel", "parallel"),
        name="merge",
    )(hdirs, hdirs, pm, hy, pm, pm, x, gate1, g2.reshape(1, d), shift2, scale2, w_a, w_b, w_out)


MOE_BLOCK = 256
ROUTE_E1, ROUTE_E2, ROUTE_W1, ROUTE_W2 = 0, 1, 2, 3
EXP_LANE0 = N_GROUPS


def _first_lane_of_max(val, valid, lane):
    masked = jnp.where(valid, val, NEG_BIG)
    mx = jnp.max(masked, axis=1, keepdims=True)
    idx = jnp.min(jnp.where(valid & (masked == mx), lane, LANES), axis=1, keepdims=True)
    return mx, idx


MOE_TM = 1024


def _expert_onehots(rec):
    lane = lax.broadcasted_iota(jnp.int32, rec.shape, 1)
    oh1 = lane == rec[:, ROUTE_E1:ROUTE_E1 + 1].astype(jnp.int32)
    oh2 = lane == rec[:, ROUTE_E2:ROUTE_E2 + 1].astype(jnp.int32)
    return oh1, oh2


def _router_kernel(h_ref, w_ref, b_ref, r_ref, cnt_ref):
    logits = jnp.dot(_unpack_bf16_pairs(h_ref[...]), w_ref[...], preferred_element_type=F32) + b_ref[...]
    lane = lax.broadcasted_iota(jnp.int32, logits.shape, 1)
    is_g = lane < N_GROUPS
    gmax, gsel = _first_lane_of_max(logits, is_g, lane)
    gsum = jnp.sum(jnp.where(is_g, jnp.exp(logits - gmax), 0.0), axis=1, keepdims=True)
    gw = 1.0 / gsum
    lo = EXP_LANE0 + gsel * EXPERTS_PER_GROUP
    in_grp = (lane >= lo) & (lane < lo + EXPERTS_PER_GROUP)
    emax, l1 = _first_lane_of_max(logits, in_grp, lane)
    esum = jnp.sum(jnp.where(in_grp, jnp.exp(logits - emax), 0.0), axis=1, keepdims=True)
    e2max, l2 = _first_lane_of_max(logits, in_grp & (lane != l1), lane)
    v1 = 1.0 / esum
    v2 = jnp.exp(e2max - emax) / esum
    vs = v1 + v2
    w1 = gw * v1 / vs
    w2 = gw * v2 / vs
    e1 = (l1 - EXP_LANE0).astype(F32)
    e2 = (l2 - EXP_LANE0).astype(F32)
    rec = jnp.where(lane == ROUTE_E1, e1,
                    jnp.where(lane == ROUTE_E2, e2,
                              jnp.where(lane == ROUTE_W1, w1,
                                        jnp.where(lane == ROUTE_W2, w2, 0.0))))
    r_ref[...] = rec
    oh1, oh2 = _expert_onehots(rec)
    counts = jnp.sum((oh1 | oh2).astype(F32), axis=0, keepdims=True)
    cnt_ref[...] = jnp.broadcast_to(counts, cnt_ref.shape)


def _router(h2, w_group, b_group, w_router, b_router):
    n, dp = h2.shape
    d = 2 * dp
    tm = MOE_TM
    w = jnp.zeros((d, LANES), F32).at[:, :N_GROUPS].set(w_group).at[
        :, EXP_LANE0:EXP_LANE0 + N_EXPERTS].set(w_router).astype(BF16)
    b = jnp.zeros((1, LANES), F32).at[0, :N_GROUPS].set(b_group).at[
        0, EXP_LANE0:EXP_LANE0 + N_EXPERTS].set(b_router)
    return pl.pallas_call(
        _router_kernel,
        grid=(n // tm,),
        in_specs=[pl.BlockSpec((tm, dp), lambda i: (i, 0)),
                  pl.BlockSpec((d, LANES), lambda i: (0, 0)),
                  pl.BlockSpec((1, LANES), lambda i: (0, 0))],
        out_specs=[pl.BlockSpec((tm, LANES), lambda i: (i, 0)),
                   pl.BlockSpec((None, 8, LANES), lambda i: (i, 0, 0))],
        out_shape=[jax.ShapeDtypeStruct((n, LANES), F32), jax.ShapeDtypeStruct((n // tm, 8, LANES), F32)],
        compiler_params=_cparams("parallel"),
        name="moe_router",
    )(h2, w, b)


def _slots_kernel(r_ref, base_ref, dest_ref):
    rec = r_ref[...]
    tm = rec.shape[0]
    lane = lax.broadcasted_iota(jnp.int32, rec.shape, 1)
    oh1, oh2 = _expert_onehots(rec)
    r = lax.broadcasted_iota(jnp.int32, (tm, tm), 0)
    c = lax.broadcasted_iota(jnp.int32, (tm, tm), 1)
    earlier = (r > c).astype(BF16)
    rank = jnp.dot(earlier, (oh1 | oh2).astype(BF16), preferred_element_type=F32) + base_ref[0:1, :]
    d1 = jnp.sum(jnp.where(oh1, rank, 0.0), axis=1, keepdims=True)
    d2 = jnp.sum(jnp.where(oh2, rank, 0.0), axis=1, keepdims=True)
    dest_ref[...] = jnp.where(lane == 0, d1, jnp.where(lane == 1, d2, 0.0)).astype(jnp.int32)


def _slots(route, tile_counts):
    n = route.shape[0]
    tm = MOE_TM
    cnt = tile_counts[:, 0, :]
    totals = jnp.sum(cnt, axis=0)
    nblk = jnp.ceil(totals * (1.0 / MOE_BLOCK))
    first_slot = (jnp.cumsum(nblk) - nblk) * float(MOE_BLOCK)
    base = first_slot[None, :] + jnp.cumsum(cnt, axis=0) - cnt
    base = jnp.broadcast_to(base[:, None, :], tile_counts.shape)
    dest = pl.pallas_call(
        _slots_kernel,
        grid=(n // tm,),
        in_specs=[pl.BlockSpec((tm, LANES), lambda i: (i, 0)),
                  pl.BlockSpec((None, 8, LANES), lambda i: (i, 0, 0))],
        out_specs=pl.BlockSpec((tm, LANES), lambda i: (i, 0)),
        out_shape=jax.ShapeDtypeStruct((n, LANES), jnp.int32),
        compiler_params=_cparams("parallel"),
        name="moe_slots",
    )(route, base)
    return dest, totals


EXPERT_STEP_BLOCKS = 4


def _experts_kernel(be_ref, first_ref, nxt_ref, par_ref, nu_ref, x_ref, w1_hbm, w3_hbm, w2_hbm, o_ref,
                    w1f, w3f, w2f, w1b, w3b, w2b, sems):
    step = pl.program_id(0)

    def weight_copies(e, slot):
        return (pltpu.make_async_copy(w1_hbm.at[e], w1f.at[slot], sems.at[0, slot]),
                pltpu.make_async_copy(w3_hbm.at[e], w3f.at[slot], sems.at[1, slot]),
                pltpu.make_async_copy(w2_hbm.at[e], w2f.at[slot], sems.at[2, slot]))

    @pl.when(step == 0)
    def _():
        for cp in weight_copies(be_ref[0], 0):
            cp.start()

    for sub in range(EXPERT_STEP_BLOCKS):
        i = step * EXPERT_STEP_BLOCKS + sub
        rows = pl.ds(sub * MOE_BLOCK, MOE_BLOCK)

        @pl.when(first_ref[i] == 1)
        def _():
            slot = par_ref[i]

            @pl.when(nxt_ref[i] >= 0)
            def _():
                for cp in weight_copies(nxt_ref[i], 1 - slot):
                    cp.start()

            for cp in weight_copies(be_ref[i], slot):
                cp.wait()
            w1b[...] = w1f[slot].astype(BF16)
            w3b[...] = w3f[slot].astype(BF16)
            w2b[...] = w2f[slot].astype(BF16)

        @pl.when(i < nu_ref[0])
        def _():
            x = _unpack_bf16_pairs(x_ref[rows, :])
            a = jnp.dot(x, w1b[...], preferred_element_type=F32)
            b = jnp.dot(x, w3b[...], preferred_element_type=F32)
            hmid = (a * jax.nn.sigmoid(a)) * b
            o_ref[rows, :] = _pack_bf16_pairs(jnp.dot(hmid.astype(BF16), w2b[...], preferred_element_type=F32))

        @pl.when(i >= nu_ref[0])
        def _():
            o_ref[rows, :] = jnp.zeros((MOE_BLOCK, o_ref.shape[1]), o_ref.dtype)


def _experts(xs, nb, block_e, n_used, w1_e, w3_e, w2_e):
    dp = xs.shape[1]
    d, de = w1_e.shape[1], w1_e.shape[2]
    idx = jnp.arange(nb, dtype=jnp.int32)
    used = idx < n_used[0]
    first = used & ((idx == 0) | (block_e != jnp.roll(block_e, 1)))
    ordinal = jnp.cumsum(first.astype(jnp.int32)) - 1
    par = (ordinal % 2).astype(jnp.int32)
    first_pos = jnp.where(first, idx, nb)
    next_first = lax.cummin(jnp.concatenate([first_pos[1:], jnp.full((1,), nb, jnp.int32)]), reverse=True)
    nxt = jnp.where(next_first < nb, block_e[jnp.minimum(next_first, nb - 1)], -1).astype(jnp.int32)
    any_spec = pl.BlockSpec(memory_space=pl.ANY)
    assert nb % EXPERT_STEP_BLOCKS == 0
    step_rows = EXPERT_STEP_BLOCKS * MOE_BLOCK
    grid_spec = pltpu.PrefetchScalarGridSpec(
        num_scalar_prefetch=5,
        grid=(nb // EXPERT_STEP_BLOCKS,),
        in_specs=[pl.BlockSpec((step_rows, dp), lambda i, *_: (i, 0)), any_spec, any_spec, any_spec],
        out_specs=pl.BlockSpec((step_rows, dp), lambda i, *_: (i, 0)),
        scratch_shapes=[pltpu.VMEM((2, d, de), F32), pltpu.VMEM((2, d, de), F32), pltpu.VMEM((2, de, d), F32),
                        pltpu.VMEM((d, de), BF16), pltpu.VMEM((d, de), BF16), pltpu.VMEM((de, d), BF16),
                        pltpu.SemaphoreType.DMA((3, 2))],
    )
    return pl.pallas_call(
        _experts_kernel,
        grid_spec=grid_spec,
        out_shape=jax.ShapeDtypeStruct((nb * MOE_BLOCK, dp), xs.dtype),
        compiler_params=_cparams("arbitrary"),
        name="moe_experts",
    )(block_e, first.astype(jnp.int32), nxt, par, n_used, xs, w1_e, w3_e, w2_e)


SC_WINDOW = 128
SC_CORES, SC_SUBCORES = 2, 16
SC_WORKERS = SC_CORES * SC_SUBCORES


def _sc_worker_id():
    return lax.axis_index("c") * SC_SUBCORES + lax.axis_index("s")


def _sc_mesh():
    return plsc.VectorSubcoreMesh(core_axis_name="c", subcore_axis_name="s")


def _sc_dispatch(rows, dest0, dest1, pad_slots, n_rows):
    n, dv = rows.shape
    nwin, pwin = n // SC_WINDOW, pad_slots.shape[0] // SC_WINDOW
    assert n % (SC_WINDOW * SC_WORKERS) == 0 and pad_slots.shape[0] % (SC_WINDOW * SC_WORKERS) == 0
    zeros = jnp.zeros((SC_WINDOW, dv), rows.dtype)

    @pl.kernel(out_type=jax.ShapeDtypeStruct((n_rows, dv), rows.dtype), mesh=_sc_mesh(),
               scratch_types=[pltpu.VMEM((1, SC_WINDOW), jnp.int32), pltpu.VMEM((SC_WINDOW, dv), rows.dtype)],
               name="moe_dispatch_sc")
    def scatter(x_hbm, d0_hbm, d1_hbm, p_hbm, z_hbm, o_hbm, idx, buf):
        wid = _sc_worker_id()
        pltpu.sync_copy(z_hbm, buf)

        @pl.loop(0, pwin // SC_WORKERS)
        def _(t):
            w = t * SC_WORKERS + wid
            pltpu.sync_copy(p_hbm.at[pl.ds(w, 1)], idx)
            pltpu.sync_copy(buf, o_hbm.at[idx.at[0]])

        @pl.loop(0, nwin // SC_WORKERS)
        def _(t):
            w = t * SC_WORKERS + wid
            pltpu.sync_copy(x_hbm.at[pl.ds(w * SC_WINDOW, SC_WINDOW)], buf)
            for d_hbm in (d0_hbm, d1_hbm):
                pltpu.sync_copy(d_hbm.at[pl.ds(w, 1)], idx)
                pltpu.sync_copy(buf, o_hbm.at[idx.at[0]])

    return scatter(rows, dest0.reshape(nwin, SC_WINDOW), dest1.reshape(nwin, SC_WINDOW),
                   pad_slots.reshape(pwin, SC_WINDOW), zeros)


def _sc_gather(table, index):
    m = index.shape[0]
    dv = table.shape[1]
    nwin = m // SC_WINDOW
    assert m % (SC_WINDOW * SC_WORKERS) == 0

    @pl.kernel(out_type=jax.ShapeDtypeStruct((m, dv), table.dtype), mesh=_sc_mesh(),
               scratch_types=[pltpu.VMEM((1, SC_WINDOW), jnp.int32), pltpu.VMEM((SC_WINDOW, dv), table.dtype)],
               name="moe_gather_sc")
    def gather(x_hbm, i_hbm, o_hbm, idx, buf):
        wid = _sc_worker_id()

        @pl.loop(0, nwin // SC_WORKERS)
        def _(t):
            w = t * SC_WORKERS + wid
            pltpu.sync_copy(i_hbm.at[pl.ds(w, 1)], idx)
            pltpu.sync_copy(x_hbm.at[idx.at[0]], buf)
            pltpu.sync_copy(buf, o_hbm.at[pl.ds(w * SC_WINDOW, SC_WINDOW)])

    return gather(table, index.reshape(nwin, SC_WINDOW))


def _combine_planes_kernel(r_ref, ya_ref, yb_ref, x_ref, gate_ref, gf_ref, *rest):
    o_ref = rest[-1]
    rec = r_ref[...]
    y = (_unpack_bf16_pairs(ya_ref[...]).astype(F32) * rec[:, ROUTE_W1:ROUTE_W1 + 1]
         + _unpack_bf16_pairs(yb_ref[...]).astype(F32) * rec[:, ROUTE_W2:ROUTE_W2 + 1])
    x2 = x_ref[...] + gate_ref[...] * y
    o_ref[...] = x2 * lax.rsqrt(jnp.mean(x2 * x2, axis=-1, keepdims=True) + EPS) * gf_ref[...]


def _combine_planes(g, route, x1, gate2, g_final, batch, prev, tm=COMBINE_TM):
    bsz, L, d = x1.shape
    tpb = L // tm
    dp = g.shape[-1]
    in_specs = [pl.BlockSpec((tm, LANES), lambda i: (batch * tpb + i, 0)),
                pl.BlockSpec((None, tm, dp), lambda i: (0, i, 0)),
                pl.BlockSpec((None, tm, dp), lambda i: (1, i, 0)),
                pl.BlockSpec((None, tm, d), lambda i: (batch, i, 0)),
                pl.BlockSpec((None, 1, d), lambda i: (batch, 0, 0)),
                pl.BlockSpec((1, d), lambda i: (0, 0))]
    args = [route, g, g, x1, gate2, g_final.reshape(1, d)]
    aliases = {}
    if prev is not None:
        in_specs.append(pl.BlockSpec(memory_space=pl.ANY))
        aliases = {len(args): 0}
        args.append(prev)
    return pl.pallas_call(
        _combine_planes_kernel,
        grid=(tpb,),
        in_specs=in_specs,
        out_specs=pl.BlockSpec((None, tm, d), lambda i: (batch, i, 0)),
        out_shape=jax.ShapeDtypeStruct((bsz, L, d), F32),
        input_output_aliases=aliases,
        compiler_params=_cparams("parallel"),
        name="moe_combine",
    )(*args)


def _moe(h2, x1, gate2, g_final, w_group, b_group, w_router, b_router, w1_e, w3_e, w2_e):
    bsz, L, d = x1.shape
    n = bsz * L
    h2f = h2.reshape(n, h2.shape[-1])
    route, tile_counts = _router(h2f, w_group, b_group, w_router, b_router)
    dest_rec, counts = _slots(route, tile_counts)
    nb = (2 * n) // MOE_BLOCK + N_EXPERTS
    cnt = counts[:N_EXPERTS].astype(jnp.int32)
    blocks_per_e = (cnt + MOE_BLOCK - 1) // MOE_BLOCK
    ends = jnp.cumsum(blocks_per_e)
    block_e = jnp.minimum(jnp.sum(ends[None, :] <= jnp.arange(nb, dtype=jnp.int32)[:, None], axis=1),
                          N_EXPERTS - 1).astype(jnp.int32)
    n_used = ends[-1:].astype(jnp.int32)
    n_slots = nb * MOE_BLOCK
    pad_j = jnp.arange(MOE_BLOCK, dtype=jnp.int32)[None, :]
    spare = n_slots + jnp.arange(N_EXPERTS * MOE_BLOCK, dtype=jnp.int32).reshape(N_EXPERTS, MOE_BLOCK)
    first_slot = ((ends - blocks_per_e) * MOE_BLOCK)[:, None]
    is_pad = cnt[:, None] + pad_j < blocks_per_e[:, None] * MOE_BLOCK
    pad_slots = jnp.where(is_pad, first_slot + cnt[:, None] + pad_j, spare).reshape(-1)
    xs = _sc_dispatch(h2f, dest_rec[:, 0], dest_rec[:, 1], pad_slots, n_slots + N_EXPERTS * MOE_BLOCK)
    ys = _experts(xs, nb, block_e, n_used, w1_e, w3_e, w2_e)
    out = None
    for b in range(bsz):
        tok = slice(b * L, (b + 1) * L)
        g = _sc_gather(ys, jnp.concatenate([dest_rec[tok, 0], dest_rec[tok, 1]]))
        out = _combine_planes(g.reshape(2, L, g.shape[-1]), route, x1, gate2, g_final, b, out)
    return out


def kernel(x, c, ctx, c_ctx, w_mod, b_mod, g_norm1, g_norm2, w_in, b_in, w_qk_conv, b_qk_conv,
           w_h_conv, b_h_conv, hf_w1, hf_b1, hf_w2, hf_b2, hf_w3, hf_freq, h_bias, w_a, w_b, w_out,
           w_group, b_group, w_router, b_router, w1_e, w3_e, w2_e, g_final):
    assert w_mod.shape[0] == 1, "single-layer block"
    (w_mod, b_mod, g_norm1, g_norm2, w_in, b_in, w_qk_conv, b_qk_conv, w_h_conv, b_h_conv, hf_w1, hf_b1, hf_w2,
     hf_b2, hf_w3, hf_freq, h_bias, w_a, w_b, w_out, w_group, b_group, w_router, b_router, w1_e, w3_e, w2_e) = (
        t[0] for t in (w_mod, b_mod, g_norm1, g_norm2, w_in, b_in, w_qk_conv, b_qk_conv, w_h_conv, b_h_conv,
                       hf_w1, hf_b1, hf_w2, hf_b2, hf_w3, hf_freq, h_bias, w_a, w_b, w_out, w_group, b_group,
                       w_router, b_router, w1_e, w3_e, w2_e))
    bsz, L, d = x.shape
    lc = ctx.shape[1]
    seg = L // (L // GRID_W)
    chunk_c = min(lc, MLSTM_CHUNK)
    assert bsz + 1 <= 8 and lc % chunk_c == 0 and L % MLSTM_CHUNK == 0

    cond = jnp.zeros((8, d), F32).at[:bsz].set(c).at[bsz].set(c_ctx)
    mod = _adaln(cond, w_mod, b_mod).reshape(8, 6, d)
    modx = mod[:bsz]
    shift1, scale1, gate1, shift2, scale2, gate2 = (modx[:, i:i + 1] for i in range(6))
    shift1c = jnp.broadcast_to(mod[bsz, 0].reshape(1, 1, d), (bsz, 1, d))
    scale1c = jnp.broadcast_to(mod[bsz, 1].reshape(1, 1, d), (bsz, 1, d))

    w_in16 = w_in.astype(BF16)
    k_scale = jnp.full((M_WIDTH,), M_HEAD_DIM ** -0.5, F32)
    qk_scale = jnp.concatenate([jnp.ones((M_WIDTH,), F32), k_scale])
    w_gates, b_gates = w_in[:, IG0:M_COLS], b_in[IG0:M_COLS]

    hc = _norm_mod(ctx, g_norm1, shift1c, scale1c, lc)
    kc = _proj_conv_silu(hc, w_in16[:, K0:V0], b_in[K0:V0], w_qk_conv[:, M_WIDTH:], b_qk_conv[M_WIDTH:],
                         k_scale, lc, lc)
    vc = _proj_act(hc, w_in16[:, V0:O0], b_in[V0:O0], "none", BF16, lc)
    bcc, acc, arc = _gates(hc, w_gates, b_gates, chunk_c)
    zero_state = (jnp.zeros((bsz, 2, M_HEADS, M_HEAD_DIM, M_HEAD_DIM), F32),
                  jnp.zeros((bsz, 2, M_HEADS, 1, M_HEAD_DIM), F32),
                  jnp.zeros((bsz, 2, M_HEADS, 1, LANES), F32))
    _, ctx_state = _mlstm(None, (kc, 0), (vc, 0), bcc, acc, arc, zero_state, False, chunk_c)

    tm = ROW_TILE
    w_main = jnp.concatenate([w_in16[:, Q0:IG0], w_in16[:, GA0:IN_COLS]], axis=1)
    b_main = jnp.concatenate([b_in[Q0:IG0], b_in[GA0:IN_COLS]])
    _, dft_fast = _dft_factors(2 * L)
    pm, h, h_il = _proj_main(x, g_norm1, shift1, scale1, w_main, b_main, w_qk_conv, b_qk_conv, qk_scale,
                             seg, tm, dft_fast)
    bc, ac, ar = _gates(h, w_gates, b_gates, MLSTM_CHUNK)
    hdirs, _ = _mlstm((pm, PM_Q), (pm, PM_K), (pm, PM_V), bc, ac, ar, ctx_state, True, MLSTM_CHUNK)

    x0_t, s_t = _proj_hyena(h_il, w_in16[:, HY0:GA0], b_in[HY0:GA0], w_h_conv, b_h_conv, seg)
    hy = _hyena_long_conv(s_t, x0_t, h_bias, hf_w1, hf_b1, hf_w2, hf_b2, hf_w3, hf_freq)

    x1, h2 = _merge(hdirs, pm, hy, x, gate1, g_norm2, shift2, scale2,
                    w_a.astype(BF16), w_b.astype(BF16), w_out.astype(BF16))
    return _moe(h2, x1, gate2, g_final, w_group, b_group, w_router, b_router, w1_e, w3_e, w2_e)
```

```python
import functools
import math

import jax
import jax.numpy as jnp
import numpy as np
from jax import lax
from jax.experimental import pallas as pl
from jax.experimental.pallas import tpu as pltpu
from jax.experimental.pallas import tpu_sc as plsc

F32 = jnp.float32
BF16 = jnp.bfloat16

D_MODEL = 1024
GRID_W = 64
EPS = 1e-6
M_HEADS = 4
M_HEAD_DIM = 256
M_WIDTH = M_HEADS * M_HEAD_DIM
H_WIDTH = 1024
H_POS_BANDS = 16
H_FILTER_HIDDEN = 64
H_FAST_DECAY_PCT = 0.3
H_SLOW_DECAY_PCT = 1.5
H_DECAY_TARGET = 1e-2
N_GROUPS = 8
EXPERTS_PER_GROUP = 8
N_EXPERTS = N_GROUPS * EXPERTS_PER_GROUP
D_EXPERT = 512
Q0 = 0
K0 = Q0 + M_WIDTH
V0 = K0 + M_WIDTH
O0 = V0 + M_WIDTH
IG0 = O0 + M_WIDTH
FG0 = IG0 + 2 * M_HEADS
M_COLS = FG0 + 2 * M_HEADS
HY0 = M_COLS
GA0 = HY0 + 3 * H_WIDTH
GB0 = GA0 + D_MODEL
IN_COLS = GB0 + D_MODEL

LANES = 128
MLSTM_CHUNK = 512
NEG_BIG = -1e30
VMEM_LIMIT = 48 * 1024 * 1024
ROW_TILE = 2048
ILV_TILE = 1024
ADALN_TN = 1536
SMALL_TN = 512
MERGE_TM = 512
COMBINE_TM = 512


def _cparams(*sem):
    return pltpu.CompilerParams(dimension_semantics=sem, vmem_limit_bytes=VMEM_LIMIT)


def _adaln_kernel(c_ref, w_ref, b_ref, o_ref):
    s = c_ref[...]
    s = s * jax.nn.sigmoid(s)
    o_ref[...] = jnp.dot(s.astype(BF16), w_ref[...].astype(BF16), preferred_element_type=F32) + b_ref[...]


def _adaln(cond, w_mod, b_mod):
    n = w_mod.shape[1]
    tn = ADALN_TN
    return pl.pallas_call(
        _adaln_kernel,
        grid=(n // tn,),
        in_specs=[pl.BlockSpec((8, D_MODEL), lambda j: (0, 0)),
                  pl.BlockSpec((D_MODEL, tn), lambda j: (0, j)),
                  pl.BlockSpec((1, tn), lambda j: (0, j))],
        out_specs=pl.BlockSpec((8, tn), lambda j: (0, j)),
        out_shape=jax.ShapeDtypeStruct((8, n), F32),
        compiler_params=_cparams("arbitrary"),
        name="adaln",
    )(cond, w_mod, b_mod.reshape(1, n))


def _norm_mod_kernel(x_ref, g_ref, sh_ref, sc_ref, o_ref):
    x = x_ref[...]
    y = x * lax.rsqrt(jnp.mean(x * x, axis=-1, keepdims=True) + EPS)
    y = y * g_ref[...]
    o_ref[...] = (y * (1.0 + sc_ref[...]) + sh_ref[...]).astype(o_ref.dtype)


def _norm_mod(x, g, shift, scale, tm):
    bsz, L, d = x.shape
    return pl.pallas_call(
        _norm_mod_kernel,
        grid=(bsz, L // tm),
        in_specs=[pl.BlockSpec((None, tm, d), lambda b, i: (b, i, 0)),
                  pl.BlockSpec((1, d), lambda b, i: (0, 0)),
                  pl.BlockSpec((None, 1, d), lambda b, i: (b, 0, 0)),
                  pl.BlockSpec((None, 1, d), lambda b, i: (b, 0, 0))],
        out_specs=pl.BlockSpec((None, tm, d), lambda b, i: (b, i, 0)),
        out_shape=jax.ShapeDtypeStruct((bsz, L, d), BF16),
        compiler_params=_cparams("parallel", "parallel"),
        name="norm_mod",
    )(x, g.reshape(1, d), shift, scale)


def _conv3(z, wc, bc, seg):
    tm = z.shape[0]
    pos = lax.broadcasted_iota(jnp.int32, z.shape, 0) & (seg - 1)
    zp = jnp.where(pos == 0, 0.0, pltpu.roll(z, 1, 0))
    zn = jnp.where(pos == seg - 1, 0.0, pltpu.roll(z, tm - 1, 0))
    return zp * wc[0:1, :] + z * wc[1:2, :] + zn * wc[2:3, :] + bc


def _proj_act_kernel(h_ref, w_ref, b_ref, o_ref, *, act):
    z = jnp.dot(h_ref[...], w_ref[...], preferred_element_type=F32) + b_ref[...]
    if act == "sigmoid":
        z = jax.nn.sigmoid(z)
    o_ref[...] = z.astype(o_ref.dtype)


def _proj_act(h, w, b, act, out_dtype, tm, tn=SMALL_TN):
    bsz, L, d = h.shape
    n = w.shape[1]
    return pl.pallas_call(
        functools.partial(_proj_act_kernel, act=act),
        grid=(bsz, L // tm, n // tn),
        in_specs=[pl.BlockSpec((None, tm, d), lambda b_, i, j: (b_, i, 0)),
                  pl.BlockSpec((d, tn), lambda b_, i, j: (0, j)),
                  pl.BlockSpec((1, tn), lambda b_, i, j: (0, j))],
        out_specs=pl.BlockSpec((None, tm, tn), lambda b_, i, j: (b_, i, j)),
        out_shape=jax.ShapeDtypeStruct((bsz, L, n), out_dtype),
        compiler_params=_cparams("parallel", "parallel", "arbitrary"),
        name="proj_" + act,
    )(h, w, b.reshape(1, n))


def _proj_conv_silu_kernel(h_ref, w_ref, b_ref, wc_ref, bc_ref, cs_ref, o_ref, *, seg):
    z = jnp.dot(h_ref[...], w_ref[...], preferred_element_type=F32) + b_ref[...]
    y = _conv3(z, wc_ref[...], bc_ref[...], seg)
    y = y * jax.nn.sigmoid(y)
    o_ref[...] = (y * cs_ref[...]).astype(o_ref.dtype)


def _proj_conv_silu(h, w, b, wc, bc, colscale, seg, tm, tn=SMALL_TN):
    bsz, L, d = h.shape
    n = w.shape[1]
    col = lambda b_, i, j: (0, j)
    return pl.pallas_call(
        functools.partial(_proj_conv_silu_kernel, seg=seg),
        grid=(bsz, L // tm, n // tn),
        in_specs=[pl.BlockSpec((None, tm, d), lambda b_, i, j: (b_, i, 0)),
                  pl.BlockSpec((d, tn), col),
                  pl.BlockSpec((1, tn), col),
                  pl.BlockSpec((3, tn), col),
                  pl.BlockSpec((1, tn), col),
                  pl.BlockSpec((1, tn), col)],
        out_specs=pl.BlockSpec((None, tm, tn), lambda b_, i, j: (b_, i, j)),
        out_shape=jax.ShapeDtypeStruct((bsz, L, n), BF16),
        compiler_params=_cparams("parallel", "parallel", "arbitrary"),
        name="proj_conv_silu",
    )(h, w, b.reshape(1, n), wc, bc.reshape(1, n), colscale.reshape(1, n))


PROJ_TN = 1024
PROJ_SUB = 512
PM_Q, PM_K, PM_V, PM_O, PM_GA, PM_GB = range(6)


def _proj_main_kernel(x_ref, g_ref, sh_ref, sc_ref, w_ref, b_ref, wc_ref, bc_ref, cs_ref,
                      o_ref, h_ref, hi_hbm, hp_sc, sem, *, seg):
    b, i, j = pl.program_id(0), pl.program_id(1), pl.program_id(2)
    n2, jt = hi_hbm.shape[2], hi_hbm.shape[3]
    tiles = x_ref.shape[0] // (n2 * jt)

    def interleave_copies():
        return [pltpu.make_async_copy(hp_sc.at[pl.ds((t * jt + jj) * n2, n2)],
                                      hi_hbm.at[b, i * tiles + t, :, jj, :], sem)
                for t in range(tiles) for jj in range(jt)]

    @pl.when(j == 0)
    def _():
        x = x_ref[...]
        y = x * lax.rsqrt(jnp.mean(x * x, axis=-1, keepdims=True) + EPS) * g_ref[...]
        y = y * (1.0 + sc_ref[...]) + sh_ref[...]
        h_ref[...] = y.astype(h_ref.dtype)
        hp_sc[...] = _pack_bf16_pairs(y)
        for cp in interleave_copies():
            cp.start()

    @pl.when(j == pl.num_programs(2) - 1)
    def _():
        for cp in interleave_copies():
            cp.wait()

    def run(epilogue):
        for c in range(PROJ_TN // PROJ_SUB):
            sl = slice(c * PROJ_SUB, (c + 1) * PROJ_SUB)
            z = jnp.dot(h_ref[...], w_ref[:, sl], preferred_element_type=F32) + b_ref[:, sl]
            o_ref[:, sl] = epilogue(z, sl).astype(o_ref.dtype)

    def conv_silu(z, sl):
        y = _conv3(z, wc_ref[:, sl], bc_ref[:, sl], seg)
        return (y * jax.nn.sigmoid(y)) * cs_ref[:, sl]

    @pl.when(j <= PM_K)
    def _():
        run(conv_silu)

    @pl.when(j == PM_V)
    def _():
        run(lambda z, sl: z)

    @pl.when(j >= PM_O)
    def _():
        run(lambda z, sl: jax.nn.sigmoid(z))


def _proj_main(x, g, shift, scale, w, b, wc, bc, colscale, seg, tm, n2):
    bsz, L, d = x.shape
    n = w.shape[1]
    ilv = min(ILV_TILE, L)
    jt = ilv // n2
    assert tm % ilv == 0
    qk = lambda b_, i, j: (0, jnp.minimum(j, PM_K))
    row = pl.BlockSpec((None, tm, d), lambda b_, i, j: (b_, i, 0))
    bvec = pl.BlockSpec((None, 1, d), lambda b_, i, j: (b_, 0, 0))
    return pl.pallas_call(
        functools.partial(_proj_main_kernel, seg=seg),
        grid=(bsz, L // tm, n // PROJ_TN),
        in_specs=[row, pl.BlockSpec((1, d), lambda b_, i, j: (0, 0)), bvec, bvec,
                  pl.BlockSpec((d, PROJ_TN), lambda b_, i, j: (0, j)),
                  pl.BlockSpec((1, PROJ_TN), lambda b_, i, j: (0, j)),
                  pl.BlockSpec((3, PROJ_TN), qk),
                  pl.BlockSpec((1, PROJ_TN), qk),
                  pl.BlockSpec((1, PROJ_TN), qk)],
        out_specs=[pl.BlockSpec((None, tm, PROJ_TN), lambda b_, i, j: (b_, i, j)), row,
                   pl.BlockSpec(memory_space=pl.ANY)],
        out_shape=[jax.ShapeDtypeStruct((bsz, L, n), BF16), jax.ShapeDtypeStruct((bsz, L, d), BF16),
                   jax.ShapeDtypeStruct((bsz, L // ilv, n2, jt, d // 2), jnp.uint32)],
        scratch_shapes=[pltpu.VMEM((tm, d // 2), jnp.uint32), pltpu.SemaphoreType.DMA(())],
        compiler_params=_cparams("parallel", "parallel", "arbitrary"),
        name="proj_main",
    )(x, g.reshape(1, d), shift, scale, w, b.reshape(1, n), wc, bc.reshape(1, -1), colscale.reshape(1, -1))


def _conv3_interleaved(z, wc, bc, seg, jt):
    grp = seg * jt
    pad = jnp.zeros((jt, z.shape[1]), z.dtype)
    prev, nxt = [], []
    for g0 in range(0, z.shape[0], grp):
        zg = z[g0:g0 + grp]
        prev += [pad, zg[:grp - jt]]
        nxt += [zg[jt:], pad]
    zp = jnp.concatenate(prev, axis=0)
    zn = jnp.concatenate(nxt, axis=0)
    return zp * wc[0:1, :] + z * wc[1:2, :] + zn * wc[2:3, :] + bc


def _proj_hyena_kernel(h_ref, w0_ref, w1_ref, w2_ref, b_ref, wc_ref, bc_ref, x0_ref, s_ref, *, seg):
    n2, jt = s_ref.shape[0], s_ref.shape[1]
    h = _unpack_bf16_pairs(h_ref[...].reshape(n2 * jt, h_ref.shape[2]))
    us = []
    for g, w_ref in enumerate((w0_ref, w1_ref, w2_ref)):
        z = jnp.dot(h, w_ref[...], preferred_element_type=F32) + b_ref[g]
        us.append(_conv3_interleaved(z, wc_ref[g], bc_ref[g], seg, jt))
    x0_ref[...] = _pack_bf16_pairs(us[0]).reshape(x0_ref.shape)
    s_ref[...] = (us[1] * us[2]).reshape(s_ref.shape)


def _proj_hyena(hi, w, b, wc, bc, seg):
    bsz, nt, n2, jt, dp = hi.shape
    d, tm = 2 * dp, n2 * jt
    L = nt * tm
    tn = DFT_C_TILE
    nblk = H_WIDTH // tn
    assert n2 % seg == 0 and (jt % 8 == 0 or nt == 1)
    b3 = b.reshape(3, 1, H_WIDTH)
    wc3 = wc.reshape(3, 3, H_WIDTH).transpose(1, 0, 2)
    bc3 = bc.reshape(3, 1, H_WIDTH)
    return pl.pallas_call(
        functools.partial(_proj_hyena_kernel, seg=seg),
        grid=(bsz, nt, nblk),
        in_specs=[pl.BlockSpec((None, None, n2, jt, dp), lambda b_, i, j: (b_, i, 0, 0, 0)),
                  pl.BlockSpec((d, tn), lambda b_, i, j: (0, j)),
                  pl.BlockSpec((d, tn), lambda b_, i, j: (0, nblk + j)),
                  pl.BlockSpec((d, tn), lambda b_, i, j: (0, 2 * nblk + j)),
                  pl.BlockSpec((3, 1, tn), lambda b_, i, j: (0, 0, j)),
                  pl.BlockSpec((3, 3, tn), lambda b_, i, j: (0, 0, j)),
                  pl.BlockSpec((3, 1, tn), lambda b_, i, j: (0, 0, j))],
        out_specs=[pl.BlockSpec((None, n2, jt, tn // 2), lambda b_, i, j: (b_, 0, i, j)),
                   pl.BlockSpec((None, n2, jt, tn), lambda b_, i, j: (b_, 0, i, j))],
        out_shape=[jax.ShapeDtypeStruct((bsz, n2, L // n2, H_WIDTH // 2), jnp.uint32),
                   jax.ShapeDtypeStruct((bsz, n2, L // n2, H_WIDTH), F32)],
        compiler_params=_cparams("parallel", "parallel", "arbitrary"),
        name="proj_hyena",
    )(hi, w, w, w, b3, wc3, bc3)


N_GATES = 4 * M_HEADS


def _split3(x):
    hi = x.astype(BF16)
    r1 = x - hi.astype(F32)
    mid = r1.astype(BF16)
    lo = (r1 - mid.astype(F32)).astype(BF16)
    return hi, mid, lo


def _log_sigmoid(x):
    return jnp.minimum(x, 0.0) - jnp.log1p(jnp.exp(-jnp.abs(x)))


def _gates_kernel(h_ref, w_ref, wt_ref, b_ref, bt_ref, bc_ref, ac_ref, ar_ref):
    h = h_ref[...]
    t = h.shape[0]
    z = jnp.dot(h, w_ref[...], preferred_element_type=F32) + b_ref[...]
    zt = lax.dot_general(wt_ref[...], h, (((1,), (1,)), ((), ())),
                         preferred_element_type=F32) + bt_ref[...]
    r = lax.broadcasted_iota(jnp.int32, (t, t), 0)
    c = lax.broadcasted_iota(jnp.int32, (t, t), 1)
    lower = (r >= c).astype(BF16)
    upper = (r <= c).astype(BF16)
    g8 = FG_LANE0

    lf = _log_sigmoid(z)
    lane = lax.broadcasted_iota(jnp.int32, z.shape, 1)
    is_fg = (lane >= g8) & (lane < 2 * g8)
    terms = [jnp.where(is_fg, p.astype(F32), 0.0) for p in _split3(lf)]
    packed = terms[0] + pltpu.roll(terms[1], 2 * g8, 1) + pltpu.roll(terms[2], 4 * g8, 1)
    cfp = jnp.dot(lower, packed.astype(BF16), preferred_element_type=F32)
    cf = cfp + pltpu.roll(cfp, LANES - 2 * g8, 1) + pltpu.roll(cfp, LANES - 4 * g8, 1)
    cb = cf[t - 1:t, :] - cf + lf
    bc = jnp.where(lane < g8 + M_HEADS, cf, cb)
    bc = pltpu.roll(bc, LANES - g8, 1)
    bc_ref[...] = bc
    ac_ref[...] = z - bc

    lft = _log_sigmoid(zt[g8:, :])
    stacked = jnp.concatenate([p.astype(F32) for p in _split3(lft)] + [jnp.zeros_like(lft)], axis=0)
    cft3 = jnp.dot(stacked.astype(BF16), upper, preferred_element_type=F32)
    cft = cft3[0:g8] + cft3[g8:2 * g8] + cft3[2 * g8:3 * g8]
    cbt = cft[:, t - 1:t] - cft + lft
    row = lax.broadcasted_iota(jnp.int32, cft.shape, 0)
    ar_ref[...] = zt[:g8, :] - jnp.where(row < M_HEADS, cft, cbt)


FG_LANE0 = 2 * M_HEADS


def _gates(h, w_g, b_g, chunk):
    bsz, L, d = h.shape
    w_pad = jnp.zeros((d, LANES), F32).at[:, :N_GATES].set(w_g).astype(BF16)
    b_pad = jnp.zeros((1, LANES), F32).at[0, :N_GATES].set(b_g)
    wt = w_g.T.astype(BF16)
    bt = b_g.reshape(N_GATES, 1)
    tok = pl.BlockSpec((None, chunk, LANES), lambda b_, i: (b_, i, 0))
    return pl.pallas_call(
        _gates_kernel,
        grid=(bsz, L // chunk),
        in_specs=[pl.BlockSpec((None, chunk, d), lambda b_, i: (b_, i, 0)),
                  pl.BlockSpec((d, LANES), lambda b_, i: (0, 0)),
                  pl.BlockSpec((N_GATES, d), lambda b_, i: (0, 0)),
                  pl.BlockSpec((1, LANES), lambda b_, i: (0, 0)),
                  pl.BlockSpec((N_GATES, 1), lambda b_, i: (0, 0))],
        out_specs=[tok, tok, pl.BlockSpec((None, FG_LANE0, chunk), lambda b_, i: (b_, 0, i))],
        out_shape=[jax.ShapeDtypeStruct((bsz, L, LANES), F32),
                   jax.ShapeDtypeStruct((bsz, L, LANES), F32),
                   jax.ShapeDtypeStruct((bsz, FG_LANE0, L), F32)],
        compiler_params=_cparams("parallel", "parallel"),
        name="mlstm_gates",
    )(h, w_pad, wt, b_pad, bt)


def _mlstm_kernel(*refs, emit_h, n_chunks):
    if emit_h:
        (q_ref, k_ref, v_ref, bc_ref, ac_ref, ar_ref, c0_ref, n0_ref, m0_ref,
         h_ref, cf_ref, nf_ref, mf_ref, c_sc, n_sc, m_sc) = refs
    else:
        (k_ref, v_ref, bc_ref, ac_ref, ar_ref, c0_ref, n0_ref, m0_ref,
         cf_ref, nf_ref, mf_ref, c_sc, n_sc, m_sc) = refs
    d = pl.program_id(1)
    j = pl.program_id(2)
    fwd = d == 0
    t = k_ref.shape[0]
    dh = M_HEAD_DIM

    @pl.when(j == 0)
    def _():
        c_sc[...] = c0_ref[...]
        n_sc[...] = n0_ref[...]
        m_sc[...] = m0_ref[...]

    r = lax.broadcasted_iota(jnp.int32, (t, t), 0)
    c = lax.broadcasted_iota(jnp.int32, (t, t), 1)
    causal = jnp.where(fwd, r - c, c - r) >= 0
    bc_all = bc_ref[...]
    ac_all = ac_ref[...]
    ar_all = ar_ref[...]
    for hd in range(M_HEADS):
        sl = slice(hd * dh, (hd + 1) * dh)
        bc = jnp.where(fwd, bc_all[:, hd:hd + 1], bc_all[:, M_HEADS + hd:M_HEADS + hd + 1])
        ac = jnp.where(fwd, ac_all[:, hd:hd + 1], ac_all[:, M_HEADS + hd:M_HEADS + hd + 1])
        ar = jnp.where(fwd, ar_all[hd:hd + 1, :], ar_all[M_HEADS + hd:M_HEADS + hd + 1, :])
        b_tot = jnp.where(fwd, bc[t - 1:t, :], bc[0:1, :])
        m_prev = m_sc[hd][:, 0:1]
        k_h = k_ref[:, sl]
        v_h = v_ref[:, sl]
        if emit_h:
            q_h = q_ref[:, sl]
            dm = jnp.where(causal, bc + ar, NEG_BIG)
            inter = bc + m_prev
            m_t = jnp.maximum(inter, jnp.max(dm, axis=1, keepdims=True))
            qk = lax.dot_general(q_h, k_h, (((1,), (1,)), ((), ())), preferred_element_type=F32)
            s = qk * jnp.exp(dm - m_t)
            carry = jnp.exp(inter - m_t)
            num = (jnp.dot(s.astype(BF16), v_h, preferred_element_type=F32)
                   + carry * jnp.dot(q_h, c_sc[hd].astype(BF16), preferred_element_type=F32))
            den = (jnp.sum(s, axis=1, keepdims=True)
                   + carry * jnp.sum(q_h.astype(F32) * n_sc[hd], axis=1, keepdims=True))
            h_ref[:, sl] = (num / jnp.maximum(jnp.abs(den), jnp.exp(-m_t))).astype(h_ref.dtype)
        g = b_tot + ac
        m_new = jnp.maximum(b_tot + m_prev, jnp.max(g, axis=0, keepdims=True))
        wgt = jnp.exp(g - m_new)
        decay = jnp.exp(b_tot + m_prev - m_new)
        kw = k_h.astype(F32) * wgt
        c_sc[hd] = decay * c_sc[hd] + lax.dot_general(kw.astype(BF16), v_h, (((0,), (0,)), ((), ())),
                                                      preferred_element_type=F32)
        n_sc[hd] = decay * n_sc[hd] + jnp.sum(kw, axis=0, keepdims=True)
        m_sc[hd] = jnp.broadcast_to(m_new, (1, LANES))

    @pl.when(j == n_chunks - 1)
    def _():
        cf_ref[...] = c_sc[...]
        nf_ref[...] = n_sc[...]
        mf_ref[...] = m_sc[...]


def _mlstm(q, k, v, bc, ac, ar, state, emit_h, t):
    bsz, L, _ = k[0].shape
    nc = L // t
    seq = lambda b_, d, j: (b_, j + d * (nc - 1 - 2 * j), 0)
    st = lambda b_, d, j: (b_, d, 0, 0, 0)

    def tok(col):
        return pl.BlockSpec((None, t, M_WIDTH), lambda b_, d, j: (b_, j + d * (nc - 1 - 2 * j), col))

    gate_spec = pl.BlockSpec((None, t, LANES), seq)
    ar_spec = pl.BlockSpec((None, FG_LANE0, t), lambda b_, d, j: (b_, 0, j + d * (nc - 1 - 2 * j)))
    c_spec = pl.BlockSpec((None, None, M_HEADS, M_HEAD_DIM, M_HEAD_DIM), st)
    n_spec = pl.BlockSpec((None, None, M_HEADS, 1, M_HEAD_DIM), st)
    m_spec = pl.BlockSpec((None, None, M_HEADS, 1, LANES), st)
    state_shapes = [jax.ShapeDtypeStruct((bsz, 2, M_HEADS, M_HEAD_DIM, M_HEAD_DIM), F32),
                    jax.ShapeDtypeStruct((bsz, 2, M_HEADS, 1, M_HEAD_DIM), F32),
                    jax.ShapeDtypeStruct((bsz, 2, M_HEADS, 1, LANES), F32)]
    in_specs = [tok(k[1]), tok(v[1]), gate_spec, gate_spec, ar_spec, c_spec, n_spec, m_spec]
    args = [k[0], v[0], bc, ac, ar, *state]
    out_specs = [c_spec, n_spec, m_spec]
    out_shape = list(state_shapes)
    if emit_h:
        in_specs = [tok(q[1])] + in_specs
        args = [q[0]] + args
        out_specs = [pl.BlockSpec((None, None, t, M_WIDTH),
                                  lambda b_, d, j: (d, b_, j + d * (nc - 1 - 2 * j), 0))] + out_specs
        out_shape = [jax.ShapeDtypeStruct((2, bsz, L, M_WIDTH), BF16)] + out_shape
    outs = pl.pallas_call(
        functools.partial(_mlstm_kernel, emit_h=emit_h, n_chunks=nc),
        grid=(bsz, 2, nc),
        in_specs=in_specs,
        out_specs=out_specs,
        out_shape=out_shape,
        scratch_shapes=[pltpu.VMEM((M_HEADS, M_HEAD_DIM, M_HEAD_DIM), F32),
                        pltpu.VMEM((M_HEADS, 1, M_HEAD_DIM), F32),
                        pltpu.VMEM((M_HEADS, 1, LANES), F32)],
        compiler_params=_cparams("parallel", "parallel", "arbitrary"),
        name="mlstm" if emit_h else "mlstm_state",
    )(*args)
    if emit_h:
        return outs[0], tuple(outs[1:])
    return None, tuple(outs)


DFT_M_TILE = 8
DFT_C_TILE = 1024
DFT_INNER_C_TILE = 512
FEAT_ROWS = 16


def _filter_outer_kernel(bands_ref, w1t_ref, b1_ref, w2t_ref, b2_ref, w3p_ref, w3f_ref, fr_ref, dl_ref, l_ref,
                         a_ref, ss_ref, *, L, n1, n2):
    i = pl.program_id(0)
    h = n1 // 2
    cols = DFT_M_TILE * h

    def positions(shape, axis, side):
        q = lax.broadcasted_iota(jnp.int32, shape, axis)
        mm, jj = q // h, q % h
        n = n2 * (jj + side * h) + i * DFT_M_TILE + mm
        return n, jnp.where(n < L, n, 2 * L - n).astype(F32)

    taps = []
    sumsq = jnp.zeros((1, a_ref.shape[-1]), F32)
    for side, w3_ref in ((0, w3p_ref), (1, w3f_ref)):
        _, p_row = positions((1, cols), 1, side)
        t_row = p_row / float(max(L - 1, 1))
        ang = ((2 * math.pi / L) * p_row) * bands_ref[...]
        row = lax.broadcasted_iota(jnp.int32, (FEAT_ROWS, cols), 0)
        feats = jnp.concatenate([jnp.where(row == 0, t_row, 0.0), jnp.cos(ang), -jnp.sin(ang)], axis=0)
        fr = fr_ref[...]
        hid = jnp.sin(fr * (jnp.dot(w1t_ref[...], feats.astype(BF16), preferred_element_type=F32) + b1_ref[...]))
        hid = jnp.sin(fr * (jnp.dot(w2t_ref[...], hid.astype(BF16), preferred_element_type=F32) + b2_ref[...]))
        filt = lax.dot_general(hid.astype(BF16), w3_ref[...], (((0,), (0,)), ((), ())),
                               preferred_element_type=F32)
        n_col, p_col = positions((cols, 1), 0, side)
        t_col = p_col / float(max(L - 1, 1))
        kern = filt * jnp.exp(-t_col * jnp.abs(dl_ref[...]))
        kern = jnp.where(n_col == L, 0.0, kern)
        sumsq = sumsq + jnp.sum(kern * kern, axis=0, keepdims=True)
        taps.append(kern)

    for mm in range(DFT_M_TILE):
        x = jnp.concatenate([taps[0][mm * h:(mm + 1) * h], taps[1][mm * h:(mm + 1) * h]], axis=0)
        out = jnp.dot(l_ref[...], x.astype(BF16), preferred_element_type=F32)
        a_ref[0, :, mm, :] = out[:n1]
        a_ref[1, :, mm, :] = out[n1:]

    @pl.when(i == 0)
    def _():
        ss_ref[...] = jnp.zeros_like(ss_ref)

    ss_ref[...] += sumsq


def _filter_outer(L, n1, n2, fwd_r, w1, b1, w2, b2, w3, freq):
    hid = H_FILTER_HIDDEN
    bands = jnp.linspace(1e-4, H_POS_BANDS - 1, H_POS_BANDS, dtype=F32).reshape(H_POS_BANDS, 1)
    w1t = jnp.zeros((hid, 3 * FEAT_ROWS), F32)
    w1t = w1t.at[:, 0].set(w1[0]).at[:, FEAT_ROWS:2 * FEAT_ROWS].set(w1[1:1 + H_POS_BANDS].T)
    w1t = w1t.at[:, 2 * FEAT_ROWS:].set(w1[1 + H_POS_BANDS:].T).astype(BF16)
    w3h = w3.astype(BF16)
    max_decay = math.log(H_DECAY_TARGET) / H_FAST_DECAY_PCT
    min_decay = math.log(H_DECAY_TARGET) / H_SLOW_DECAY_PCT
    deltas = jnp.linspace(min_decay, max_decay, H_WIDTH, dtype=F32).reshape(1, H_WIDTH)
    col = lambda v: v.reshape(hid, 1)
    full = lambda a: pl.BlockSpec(a.shape, lambda i: (0,) * a.ndim)
    args = [bands, w1t, col(b1), w2.T.astype(BF16), col(b2)]
    return pl.pallas_call(
        functools.partial(_filter_outer_kernel, L=L, n1=n1, n2=n2),
        grid=(n2 // DFT_M_TILE,),
        in_specs=[full(a) for a in args]
        + [pl.BlockSpec((hid, H_WIDTH), lambda i: (0, 0)), pl.BlockSpec((hid, H_WIDTH), lambda i: (0, 1)),
           full(col(freq)), full(deltas), full(fwd_r)],
        out_specs=[pl.BlockSpec((2, n1, DFT_M_TILE, H_WIDTH), lambda i: (0, 0, i, 0)),
                   pl.BlockSpec((1, H_WIDTH), lambda i: (0, 0))],
        out_shape=[jax.ShapeDtypeStruct((2, n1, n2, H_WIDTH), F32),
                   jax.ShapeDtypeStruct((1, H_WIDTH), F32)],
        compiler_params=_cparams("arbitrary"),
        name="hyena_filter_outer",
    )(*args, w3h, w3h, col(freq), deltas, fwd_r)


def _dft_factors(n):
    lg = int(round(math.log2(n)))
    n1 = 1 << ((lg + 1) // 2)
    return n1, n // n1


def _dft_outer_matrices(n1):
    k = np.arange(n1)[:, None]
    n = np.arange(n1)[None, :]
    ang = 2.0 * np.pi * ((k * n) % n1) / n1
    cr, ci = np.cos(ang), -np.sin(ang)
    h = n1 // 2
    fwd_c = np.block([[cr[:, :h], -ci[:, :h]], [ci[:, :h], cr[:, :h]]])
    fwd_r = np.concatenate([cr, ci], axis=0)
    ir, ii = cr[:h, :], -ci[:h, :]
    inv = np.block([[ir, -ii], [ii, ir]])
    return (jnp.asarray(fwd_c, F32).astype(BF16), jnp.asarray(fwd_r, F32).astype(BF16),
            jnp.asarray(inv, F32).astype(BF16))


def _dft_inner_matrices(n1, n2):
    n = n1 * n2
    k2 = np.arange(n2)[:, None]
    m = np.arange(n2)[None, :]
    ang = 2.0 * np.pi * ((k2 * m) % n2) / n2
    fr, fi = np.cos(ang), -np.sin(ang)
    f = np.block([[fr, -fi], [fi, fr]])
    k1 = jnp.arange(n1, dtype=jnp.int32)[:, None]
    tw_ang = ((jnp.arange(n2, dtype=jnp.int32)[None, :] * k1) % n).astype(F32) * (2.0 * math.pi / n)
    rep = lambda t: jnp.broadcast_to(t[:, :, None], (n1, n2, LANES))
    return (jnp.asarray(f, F32).astype(BF16), jnp.asarray(f.T, F32).astype(BF16),
            rep(jnp.cos(tw_ang)), rep(-jnp.sin(tw_ang)))


def _outer_fwd_kernel(l_ref, s_ref, a_ref):
    n1 = a_ref.shape[1]
    for mm in range(s_ref.shape[1]):
        x = jnp.concatenate([s_ref[0, mm], s_ref[1, mm]], axis=0).astype(BF16)
        out = jnp.dot(l_ref[...], x, preferred_element_type=F32)
        a_ref[0, :, mm, :] = out[:n1]
        a_ref[1, :, mm, :] = out[n1:]


def _outer_fwd(lmat, s_t):
    _, n2, n1h, c = s_t.shape
    n1 = 2 * n1h
    tc = min(DFT_C_TILE, c)
    return pl.pallas_call(
        _outer_fwd_kernel,
        grid=(n2 // DFT_M_TILE, c // tc),
        in_specs=[pl.BlockSpec(lmat.shape, lambda m, j: (0, 0)),
                  pl.BlockSpec((2, DFT_M_TILE, n1h, tc), lambda m, j: (0, m, 0, j))],
        out_specs=pl.BlockSpec((2, n1, DFT_M_TILE, tc), lambda m, j: (0, 0, m, j)),
        out_shape=jax.ShapeDtypeStruct((2, n1, n2, c), F32),
        compiler_params=_cparams("parallel", "parallel"),
        name="dft_outer_fwd",
    )(lmat, s_t)


def _outer_inv_kernel(l_ref, b_ref, s_ref, x0_ref, ysc_ref, hb_ref, o_ref):
    n1h = s_ref.shape[2]
    for mm in range(b_ref.shape[1]):
        y = jnp.concatenate([b_ref[0, mm], b_ref[1, mm]], axis=0).astype(BF16)
        out = jnp.dot(l_ref[...], y, preferred_element_type=F32)
        for b in range(2):
            conv = out[b * n1h:(b + 1) * n1h]
            x0 = _unpack_bf16_pairs(x0_ref[b, mm]).astype(F32)
            hy = x0 * (conv * ysc_ref[...] + hb_ref[...] * s_ref[b, mm])
            o_ref[b, :, mm, :] = _pack_bf16_pairs(hy)


def _outer_inv(lmat, b_t, s_t, x0_t, yscale, h_bias):
    _, n2, n1, c = b_t.shape
    n1h = n1 // 2
    tc = min(DFT_C_TILE, c)
    vec = pl.BlockSpec((1, tc), lambda m, j: (0, j))
    hy = pl.pallas_call(
        _outer_inv_kernel,
        grid=(n2 // DFT_M_TILE, c // tc),
        in_specs=[pl.BlockSpec(lmat.shape, lambda m, j: (0, 0)),
                  pl.BlockSpec((2, DFT_M_TILE, n1, tc), lambda m, j: (0, m, 0, j)),
                  pl.BlockSpec((2, DFT_M_TILE, n1h, tc), lambda m, j: (0, m, 0, j)),
                  pl.BlockSpec((2, DFT_M_TILE, n1h, tc // 2), lambda m, j: (0, m, 0, j)),
                  vec, vec],
        out_specs=pl.BlockSpec((2, n1h, DFT_M_TILE, tc // 2), lambda m, j: (0, 0, m, j)),
        out_shape=jax.ShapeDtypeStruct((2, n1h, n2, c // 2), jnp.uint32),
        compiler_params=_cparams("parallel", "parallel"),
        name="dft_outer_inv",
    )(lmat, b_t, s_t, x0_t, yscale, h_bias.reshape(1, c))
    return hy.reshape(2, n1h * n2, c // 2)


DFT_K_TILE = 8


def _twiddled_inner_dft(f_ref, twr_ref, twi_ref, a_ref, kk):
    n2, c = a_ref.shape[2], a_ref.shape[3]
    twr = jnp.tile(twr_ref[kk], (1, c // LANES))
    twi = jnp.tile(twi_ref[kk], (1, c // LANES))
    ar, ai = a_ref[0, kk], a_ref[1, kk]
    a = jnp.concatenate([(ar * twr - ai * twi).astype(BF16), (ar * twi + ai * twr).astype(BF16)], axis=0)
    x = jnp.dot(f_ref[...], a, preferred_element_type=F32)
    return x[:n2], x[n2:], twr, twi


def _inner_fwd_kernel(f_ref, twr_ref, twi_ref, a_ref, o_ref):
    for kk in range(a_ref.shape[1]):
        xr, xi, _, _ = _twiddled_inner_dft(f_ref, twr_ref, twi_ref, a_ref, kk)
        o_ref[0, kk] = xr.astype(o_ref.dtype)
        o_ref[1, kk] = xi.astype(o_ref.dtype)


def _inner_specs(n1, n2, c):
    tc = min(DFT_INNER_C_TILE, c)
    kt = min(DFT_K_TILE, n1)
    blk = pl.BlockSpec((2, kt, n2, tc), lambda k, j: (0, k, 0, j))
    mat = pl.BlockSpec((2 * n2, 2 * n2), lambda k, j: (0, 0))
    tw = pl.BlockSpec((kt, n2, LANES), lambda k, j: (k, 0, 0))
    return blk, mat, tw, (n1 // kt, c // tc), kt, tc


def _inner_fwd(f, twr, twi, a):
    _, n1, n2, c = a.shape
    blk, mat, tw, grid, _, _ = _inner_specs(n1, n2, c)
    return pl.pallas_call(
        _inner_fwd_kernel,
        grid=grid,
        in_specs=[mat, tw, tw, blk],
        out_specs=blk,
        out_shape=jax.ShapeDtypeStruct((2, n1, n2, c), BF16),
        compiler_params=_cparams("parallel", "parallel"),
        name="dft_inner_filter",
    )(f, twr, twi, a)


def _inner_conv_kernel(f_ref, ft_ref, twr_ref, twi_ref, a_ref, k_ref, o_ref):
    n2 = a_ref.shape[2]
    for kk in range(a_ref.shape[1]):
        xr, xi, twr, twi = _twiddled_inner_dft(f_ref, twr_ref, twi_ref, a_ref, kk)
        kr, ki = k_ref[0, kk].astype(F32), k_ref[1, kk].astype(F32)
        yr = xr * kr - xi * ki
        yi = xr * ki + xi * kr
        y = jnp.concatenate([yr.astype(BF16), yi.astype(BF16)], axis=0)
        b = jnp.dot(ft_ref[...], y, preferred_element_type=F32)
        br, bi = b[:n2], b[n2:]
        o_ref[0, :, kk, :] = br * twr + bi * twi
        o_ref[1, :, kk, :] = bi * twr - br * twi


def _inner_conv(f, ft, twr, twi, a, kf):
    _, n1, n2, c = a.shape
    blk, mat, tw, grid, kt, tc = _inner_specs(n1, n2, c)
    return pl.pallas_call(
        _inner_conv_kernel,
        grid=grid,
        in_specs=[mat, mat, tw, tw, blk, blk],
        out_specs=pl.BlockSpec((2, n2, kt, tc), lambda k, j: (0, 0, k, j)),
        out_shape=jax.ShapeDtypeStruct((2, n2, n1, c), F32),
        compiler_params=_cparams("parallel", "parallel"),
        name="dft_inner_conv",
    )(f, ft, twr, twi, a, kf)


def _hyena_long_conv(s_t, x0_t, h_bias, w1, b1, w2, b2, w3, freq):
    bsz, n2, n1h, c = s_t.shape
    assert bsz == 2
    n1 = 2 * n1h
    L = n1h * n2
    fwd_c, fwd_r, inv = _dft_outer_matrices(n1)
    f, ft, twr, twi = _dft_inner_matrices(n1, n2)
    af, sumsq = _filter_outer(L, n1, n2, fwd_r, w1, b1, w2, b2, w3, freq)
    kf = _inner_fwd(f, twr, twi, af)
    a = _outer_fwd(fwd_c, s_t)
    b_t = _inner_conv(f, ft, twr, twi, a, kf)
    yscale = lax.rsqrt(sumsq + EPS) * (1.0 / (2 * L))
    return _outer_inv(inv, b_t, s_t, x0_t, yscale, h_bias)


def _pack_bf16_pairs(x):
    half = x.shape[1] // 2
    lo = pltpu.bitcast(x[:, :half].astype(BF16).astype(F32), jnp.uint32) >> 16
    hi = pltpu.bitcast(x[:, half:].astype(BF16).astype(F32), jnp.uint32) & jnp.uint32(0xFFFF0000)
    return lo | hi


def _unpack_bf16_pairs(p):
    lo = pltpu.bitcast(p << 16, F32).astype(BF16)
    hi = pltpu.bitcast(p & jnp.uint32(0xFFFF0000), F32).astype(BF16)
    return jnp.concatenate([lo, hi], axis=1)


def _merge_kernel(hf_ref, hb_ref, o_ref, hy_ref, ga_ref, gb_ref, x_ref,
                  gate_ref, g2_ref, sh_ref, sc_ref, wa_ref, wb_ref, wo_ref, x1_ref, h2_ref):
    a = o_ref[...].astype(F32) * (hf_ref[...].astype(F32) + hb_ref[...].astype(F32))
    half = DFT_C_TILE // 2
    hy = jnp.concatenate([_unpack_bf16_pairs(hy_ref[:, c * half:(c + 1) * half])
                          for c in range(hy_ref.shape[1] // half)], axis=1)
    pa = jnp.dot(a.astype(BF16), wa_ref[...], preferred_element_type=F32)
    pb = jnp.dot(hy, wb_ref[...], preferred_element_type=F32)
    mix = ga_ref[...].astype(F32) * pa + gb_ref[...].astype(F32) * pb
    out = jnp.dot(mix.astype(BF16), wo_ref[...], preferred_element_type=F32)
    x1 = x_ref[...] + gate_ref[...] * out
    x1_ref[...] = x1
    y = x1 * lax.rsqrt(jnp.mean(x1 * x1, axis=-1, keepdims=True) + EPS) * g2_ref[...]
    h2_ref[...] = _pack_bf16_pairs(y * (1.0 + sc_ref[...]) + sh_ref[...])


def _merge(hdirs, pm, hy, x, gate1, g2, shift2, scale2, w_a, w_b, w_out, tm=MERGE_TM):
    bsz, L, d = x.shape
    tok = pl.BlockSpec((None, tm, d), lambda b, i: (b, i, 0))

    def pm_tile(col):
        return pl.BlockSpec((None, tm, d), lambda b, i: (b, i, col))

    packed = pl.BlockSpec((None, tm, d // 2), lambda b, i: (b, i, 0))
    vec = pl.BlockSpec((1, d), lambda b, i: (0, 0))
    bvec = pl.BlockSpec((None, 1, d), lambda b, i: (b, 0, 0))
    wsp = pl.BlockSpec((d, d), lambda b, i: (0, 0), pipeline_mode=pl.Buffered(1))
    return pl.pallas_call(
        _merge_kernel,
        grid=(bsz, L // tm),
        in_specs=[pl.BlockSpec((None, None, tm, d), lambda b, i: (0, b, i, 0)),
                  pl.BlockSpec((None, None, tm, d), lambda b, i: (1, b, i, 0)),
                  pm_tile(PM_O), packed, pm_tile(PM_GA), pm_tile(PM_GB), tok,
                  bvec, vec, bvec, bvec, wsp, wsp, wsp],
        out_specs=[tok, packed],
        out_shape=[jax.ShapeDtypeStruct((bsz, L, d), F32), jax.ShapeDtypeStruct((bsz, L, d // 2), jnp.uint32)],
        compiler_params=_cparams("parallel", "parallel"),
        name="merge",
    )(hdirs, hdirs, pm, hy, pm, pm, x, gate1, g2.reshape(1, d), shift2, scale2, w_a, w_b, w_out)


MOE_BLOCK = 256
ROUTE_E1, ROUTE_E2, ROUTE_W1, ROUTE_W2 = 0, 1, 2, 3
EXP_LANE0 = N_GROUPS


def _first_lane_of_max(val, valid, lane):
    masked = jnp.where(valid, val, NEG_BIG)
    mx = jnp.max(masked, axis=1, keepdims=True)
    idx = jnp.min(jnp.where(valid & (masked == mx), lane, LANES), axis=1, keepdims=True)
    return mx, idx


MOE_TM = 1024


def _expert_onehots(rec):
    lane = lax.broadcasted_iota(jnp.int32, rec.shape, 1)
    oh1 = lane == rec[:, ROUTE_E1:ROUTE_E1 + 1].astype(jnp.int32)
    oh2 = lane == rec[:, ROUTE_E2:ROUTE_E2 + 1].astype(jnp.int32)
    return oh1, oh2


def _router_kernel(h_ref, w_ref, b_ref, r_ref, cnt_ref):
    logits = jnp.dot(_unpack_bf16_pairs(h_ref[...]), w_ref[...], preferred_element_type=F32) + b_ref[...]
    lane = lax.broadcasted_iota(jnp.int32, logits.shape, 1)
    is_g = lane < N_GROUPS
    gmax, gsel = _first_lane_of_max(logits, is_g, lane)
    gsum = jnp.sum(jnp.where(is_g, jnp.exp(logits - gmax), 0.0), axis=1, keepdims=True)
    gw = 1.0 / gsum
    lo = EXP_LANE0 + gsel * EXPERTS_PER_GROUP
    in_grp = (lane >= lo) & (lane < lo + EXPERTS_PER_GROUP)
    emax, l1 = _first_lane_of_max(logits, in_grp, lane)
    esum = jnp.sum(jnp.where(in_grp, jnp.exp(logits - emax), 0.0), axis=1, keepdims=True)
    e2max, l2 = _first_lane_of_max(logits, in_grp & (lane != l1), lane)
    v1 = 1.0 / esum
    v2 = jnp.exp(e2max - emax) / esum
    vs = v1 + v2
    w1 = gw * v1 / vs
    w2 = gw * v2 / vs
    e1 = (l1 - EXP_LANE0).astype(F32)
    e2 = (l2 - EXP_LANE0).astype(F32)
    rec = jnp.where(lane == ROUTE_E1, e1,
                    jnp.where(lane == ROUTE_E2, e2,
                              jnp.where(lane == ROUTE_W1, w1,
                                        jnp.where(lane == ROUTE_W2, w2, 0.0))))
    r_ref[...] = rec
    oh1, oh2 = _expert_onehots(rec)
    counts = jnp.sum((oh1 | oh2).astype(F32), axis=0, keepdims=True)
    cnt_ref[...] = jnp.broadcast_to(counts, cnt_ref.shape)


def _router(h2, w_group, b_group, w_router, b_router):
    n, dp = h2.shape
    d = 2 * dp
    tm = MOE_TM
    w = jnp.zeros((d, LANES), F32).at[:, :N_GROUPS].set(w_group).at[
        :, EXP_LANE0:EXP_LANE0 + N_EXPERTS].set(w_router).astype(BF16)
    b = jnp.zeros((1, LANES), F32).at[0, :N_GROUPS].set(b_group).at[
        0, EXP_LANE0:EXP_LANE0 + N_EXPERTS].set(b_router)
    return pl.pallas_call(
        _router_kernel,
        grid=(n // tm,),
        in_specs=[pl.BlockSpec((tm, dp), lambda i: (i, 0)),
                  pl.BlockSpec((d, LANES), lambda i: (0, 0)),
                  pl.BlockSpec((1, LANES), lambda i: (0, 0))],
        out_specs=[pl.BlockSpec((tm, LANES), lambda i: (i, 0)),
                   pl.BlockSpec((None, 8, LANES), lambda i: (i, 0, 0))],
        out_shape=[jax.ShapeDtypeStruct((n, LANES), F32), jax.ShapeDtypeStruct((n // tm, 8, LANES), F32)],
        compiler_params=_cparams("parallel"),
        name="moe_router",
    )(h2, w, b)


def _slots_kernel(r_ref, base_ref, dest_ref):
    rec = r_ref[...]
    tm = rec.shape[0]
    lane = lax.broadcasted_iota(jnp.int32, rec.shape, 1)
    oh1, oh2 = _expert_onehots(rec)
    r = lax.broadcasted_iota(jnp.int32, (tm, tm), 0)
    c = lax.broadcasted_iota(jnp.int32, (tm, tm), 1)
    earlier = (r > c).astype(BF16)
    rank = jnp.dot(earlier, (oh1 | oh2).astype(BF16), preferred_element_type=F32) + base_ref[0:1, :]
    d1 = jnp.sum(jnp.where(oh1, rank, 0.0), axis=1, keepdims=True)
    d2 = jnp.sum(jnp.where(oh2, rank, 0.0), axis=1, keepdims=True)
    dest_ref[...] = jnp.where(lane == 0, d1, jnp.where(lane == 1, d2, 0.0)).astype(jnp.int32)


def _slots(route, tile_counts):
    n = route.shape[0]
    tm = MOE_TM
    cnt = tile_counts[:, 0, :]
    totals = jnp.sum(cnt, axis=0)
    nblk = jnp.ceil(totals * (1.0 / MOE_BLOCK))
    first_slot = (jnp.cumsum(nblk) - nblk) * float(MOE_BLOCK)
    base = first_slot[None, :] + jnp.cumsum(cnt, axis=0) - cnt
    base = jnp.broadcast_to(base[:, None, :], tile_counts.shape)
    dest = pl.pallas_call(
        _slots_kernel,
        grid=(n // tm,),
        in_specs=[pl.BlockSpec((tm, LANES), lambda i: (i, 0)),
                  pl.BlockSpec((None, 8, LANES), lambda i: (i, 0, 0))],
        out_specs=pl.BlockSpec((tm, LANES), lambda i: (i, 0)),
        out_shape=jax.ShapeDtypeStruct((n, LANES), jnp.int32),
        compiler_params=_cparams("parallel"),
        name="moe_slots",
    )(route, base)
    return dest, totals


EXPERT_STEP_BLOCKS = 4


def _experts_kernel(be_ref, first_ref, nxt_ref, par_ref, nu_ref, x_ref, w1_hbm, w3_hbm, w2_hbm, o_ref,
                    w1f, w3f, w2f, w1b, w3b, w2b, sems):
    step = pl.program_id(0)

    def weight_copies(e, slot):
        return (pltpu.make_async_copy(w1_hbm.at[e], w1f.at[slot], sems.at[0, slot]),
                pltpu.make_async_copy(w3_hbm.at[e], w3f.at[slot], sems.at[1, slot]),
                pltpu.make_async_copy(w2_hbm.at[e], w2f.at[slot], sems.at[2, slot]))

    @pl.when(step == 0)
    def _():
        for cp in weight_copies(be_ref[0], 0):
            cp.start()

    for sub in range(EXPERT_STEP_BLOCKS):
        i = step * EXPERT_STEP_BLOCKS + sub
        rows = pl.ds(sub * MOE_BLOCK, MOE_BLOCK)

        @pl.when(first_ref[i] == 1)
        def _():
            slot = par_ref[i]

            @pl.when(nxt_ref[i] >= 0)
            def _():
                for cp in weight_copies(nxt_ref[i], 1 - slot):
                    cp.start()

            for cp in weight_copies(be_ref[i], slot):
                cp.wait()
            w1b[...] = w1f[slot].astype(BF16)
            w3b[...] = w3f[slot].astype(BF16)
            w2b[...] = w2f[slot].astype(BF16)

        @pl.when(i < nu_ref[0])
        def _():
            x = _unpack_bf16_pairs(x_ref[rows, :])
            a = jnp.dot(x, w1b[...], preferred_element_type=F32)
            b = jnp.dot(x, w3b[...], preferred_element_type=F32)
            hmid = (a * jax.nn.sigmoid(a)) * b
            o_ref[rows, :] = _pack_bf16_pairs(jnp.dot(hmid.astype(BF16), w2b[...], preferred_element_type=F32))

        @pl.when(i >= nu_ref[0])
        def _():
            o_ref[rows, :] = jnp.zeros((MOE_BLOCK, o_ref.shape[1]), o_ref.dtype)


def _experts(xs, nb, block_e, n_used, w1_e, w3_e, w2_e):
    dp = xs.shape[1]
    d, de = w1_e.shape[1], w1_e.shape[2]
    idx = jnp.arange(nb, dtype=jnp.int32)
    used = idx < n_used[0]
    first = used & ((idx == 0) | (block_e != jnp.roll(block_e, 1)))
    ordinal = jnp.cumsum(first.astype(jnp.int32)) - 1
    par = (ordinal % 2).astype(jnp.int32)
    first_pos = jnp.where(first, idx, nb)
    next_first = lax.cummin(jnp.concatenate([first_pos[1:], jnp.full((1,), nb, jnp.int32)]), reverse=True)
    nxt = jnp.where(next_first < nb, block_e[jnp.minimum(next_first, nb - 1)], -1).astype(jnp.int32)
    any_spec = pl.BlockSpec(memory_space=pl.ANY)
    assert nb % EXPERT_STEP_BLOCKS == 0
    step_rows = EXPERT_STEP_BLOCKS * MOE_BLOCK
    grid_spec = pltpu.PrefetchScalarGridSpec(
        num_scalar_prefetch=5,
        grid=(nb // EXPERT_STEP_BLOCKS,),
        in_specs=[pl.BlockSpec((step_rows, dp), lambda i, *_: (i, 0)), any_spec, any_spec, any_spec],
        out_specs=pl.BlockSpec((step_rows, dp), lambda i, *_: (i, 0)),
        scratch_shapes=[pltpu.VMEM((2, d, de), F32), pltpu.VMEM((2, d, de), F32), pltpu.VMEM((2, de, d), F32),
                        pltpu.VMEM((d, de), BF16), pltpu.VMEM((d, de), BF16), pltpu.VMEM((de, d), BF16),
                        pltpu.SemaphoreType.DMA((3, 2))],
    )
    return pl.pallas_call(
        _experts_kernel,
        grid_spec=grid_spec,
        out_shape=jax.ShapeDtypeStruct((nb * MOE_BLOCK, dp), xs.dtype),
        compiler_params=_cparams("arbitrary"),
        name="moe_experts",
    )(block_e, first.astype(jnp.int32), nxt, par, n_used, xs, w1_e, w3_e, w2_e)


SC_WINDOW = 128
SC_CORES, SC_SUBCORES = 2, 16
SC_WORKERS = SC_CORES * SC_SUBCORES


def _sc_worker_id():
    return lax.axis_index("c") * SC_SUBCORES + lax.axis_index("s")


def _sc_mesh():
    return plsc.VectorSubcoreMesh(core_axis_name="c", subcore_axis_name="s")


def _sc_dispatch(rows, dest0, dest1, pad_slots, n_rows):
    n, dv = rows.shape
    nwin, pwin = n // SC_WINDOW, pad_slots.shape[0] // SC_WINDOW
    assert n % (SC_WINDOW * SC_WORKERS) == 0 and pad_slots.shape[0] % (SC_WINDOW * SC_WORKERS) == 0
    zeros = jnp.zeros((SC_WINDOW, dv), rows.dtype)

    @pl.kernel(out_type=jax.ShapeDtypeStruct((n_rows, dv), rows.dtype), mesh=_sc_mesh(),
               scratch_types=[pltpu.VMEM((1, SC_WINDOW), jnp.int32), pltpu.VMEM((SC_WINDOW, dv), rows.dtype)],
               name="moe_dispatch_sc")
    def scatter(x_hbm, d0_hbm, d1_hbm, p_hbm, z_hbm, o_hbm, idx, buf):
        wid = _sc_worker_id()
        pltpu.sync_copy(z_hbm, buf)

        @pl.loop(0, pwin // SC_WORKERS)
        def _(t):
            w = t * SC_WORKERS + wid
            pltpu.sync_copy(p_hbm.at[pl.ds(w, 1)], idx)
            pltpu.sync_copy(buf, o_hbm.at[idx.at[0]])

        @pl.loop(0, nwin // SC_WORKERS)
        def _(t):
            w = t * SC_WORKERS + wid
            pltpu.sync_copy(x_hbm.at[pl.ds(w * SC_WINDOW, SC_WINDOW)], buf)
            for d_hbm in (d0_hbm, d1_hbm):
                pltpu.sync_copy(d_hbm.at[pl.ds(w, 1)], idx)
                pltpu.sync_copy(buf, o_hbm.at[idx.at[0]])

    return scatter(rows, dest0.reshape(nwin, SC_WINDOW), dest1.reshape(nwin, SC_WINDOW),
                   pad_slots.reshape(pwin, SC_WINDOW), zeros)


def _sc_gather(table, index):
    m = index.shape[0]
    dv = table.shape[1]
    nwin = m // SC_WINDOW
    assert m % (SC_WINDOW * SC_WORKERS) == 0

    @pl.kernel(out_type=jax.ShapeDtypeStruct((m, dv), table.dtype), mesh=_sc_mesh(),
               scratch_types=[pltpu.VMEM((1, SC_WINDOW), jnp.int32), pltpu.VMEM((SC_WINDOW, dv), table.dtype)],
               name="moe_gather_sc")
    def gather(x_hbm, i_hbm, o_hbm, idx, buf):
        wid = _sc_worker_id()

        @pl.loop(0, nwin // SC_WORKERS)
        def _(t):
            w = t * SC_WORKERS + wid
            pltpu.sync_copy(i_hbm.at[pl.ds(w, 1)], idx)
            pltpu.sync_copy(x_hbm.at[idx.at[0]], buf)
            pltpu.sync_copy(buf, o_hbm.at[pl.ds(w * SC_WINDOW, SC_WINDOW)])

    return gather(table, index.reshape(nwin, SC_WINDOW))


def _combine_planes_kernel(r_ref, ya_ref, yb_ref, x_ref, gate_ref, gf_ref, o_ref):
    rec = r_ref[...]
    y = (_unpack_bf16_pairs(ya_ref[...]).astype(F32) * rec[:, ROUTE_W1:ROUTE_W1 + 1]
         + _unpack_bf16_pairs(yb_ref[...]).astype(F32) * rec[:, ROUTE_W2:ROUTE_W2 + 1])
    x2 = x_ref[...] + gate_ref[...] * y
    o_ref[...] = x2 * lax.rsqrt(jnp.mean(x2 * x2, axis=-1, keepdims=True) + EPS) * gf_ref[...]


def _combine_planes(g, route, x1, gate2, g_final, tm=COMBINE_TM):
    bsz, L, d = x1.shape
    tpb = L // tm
    dp = g.shape[-1]
    return pl.pallas_call(
        _combine_planes_kernel,
        grid=(bsz, tpb),
        in_specs=[pl.BlockSpec((tm, LANES), lambda b, i: (b * tpb + i, 0)),
                  pl.BlockSpec((None, tm, dp), lambda b, i: (0, b * tpb + i, 0)),
                  pl.BlockSpec((None, tm, dp), lambda b, i: (1, b * tpb + i, 0)),
                  pl.BlockSpec((None, tm, d), lambda b, i: (b, i, 0)),
                  pl.BlockSpec((None, 1, d), lambda b, i: (b, 0, 0)),
                  pl.BlockSpec((1, d), lambda b, i: (0, 0))],
        out_specs=pl.BlockSpec((None, tm, d), lambda b, i: (b, i, 0)),
        out_shape=jax.ShapeDtypeStruct((bsz, L, d), F32),
        compiler_params=_cparams("parallel", "parallel"),
        name="moe_combine",
    )(route, g, g, x1, gate2, g_final.reshape(1, d))


def _moe(h2, x1, gate2, g_final, w_group, b_group, w_router, b_router, w1_e, w3_e, w2_e):
    bsz, L, d = x1.shape
    n = bsz * L
    h2f = h2.reshape(n, h2.shape[-1])
    route, tile_counts = _router(h2f, w_group, b_group, w_router, b_router)
    dest_rec, counts = _slots(route, tile_counts)
    nb = (2 * n) // MOE_BLOCK + N_EXPERTS
    cnt = counts[:N_EXPERTS].astype(jnp.int32)
    blocks_per_e = (cnt + MOE_BLOCK - 1) // MOE_BLOCK
    ends = jnp.cumsum(blocks_per_e)
    block_e = jnp.minimum(jnp.sum(ends[None, :] <= jnp.arange(nb, dtype=jnp.int32)[:, None], axis=1),
                          N_EXPERTS - 1).astype(jnp.int32)
    n_used = ends[-1:].astype(jnp.int32)
    n_slots = nb * MOE_BLOCK
    pad_j = jnp.arange(MOE_BLOCK, dtype=jnp.int32)[None, :]
    spare = n_slots + jnp.arange(N_EXPERTS * MOE_BLOCK, dtype=jnp.int32).reshape(N_EXPERTS, MOE_BLOCK)
    first_slot = ((ends - blocks_per_e) * MOE_BLOCK)[:, None]
    is_pad = cnt[:, None] + pad_j < blocks_per_e[:, None] * MOE_BLOCK
    pad_slots = jnp.where(is_pad, first_slot + cnt[:, None] + pad_j, spare).reshape(-1)
    xs = _sc_dispatch(h2f, dest_rec[:, 0], dest_rec[:, 1], pad_slots, n_slots + N_EXPERTS * MOE_BLOCK)
    ys = _experts(xs, nb, block_e, n_used, w1_e, w3_e, w2_e)
    g = _sc_gather(ys, jnp.concatenate([dest_rec[:, 0], dest_rec[:, 1]]))
    return _combine_planes(g.reshape(2, n, g.shape[-1]), route, x1, gate2, g_final)


def kernel(x, c, ctx, c_ctx, w_mod, b_mod, g_norm1, g_norm2, w_in, b_in, w_qk_conv, b_qk_conv,
           w_h_conv, b_h_conv, hf_w1, hf_b1, hf_w2, hf_b2, hf_w3, hf_freq, h_bias, w_a, w_b, w_out,
           w_group, b_group, w_router, b_router, w1_e, w3_e, w2_e, g_final):
    assert w_mod.shape[0] == 1, "single-layer block"
    (w_mod, b_mod, g_norm1, g_norm2, w_in, b_in, w_qk_conv, b_qk_conv, w_h_conv, b_h_conv, hf_w1, hf_b1, hf_w2,
     hf_b2, hf_w3, hf_freq, h_bias, w_a, w_b, w_out, w_group, b_group, w_router, b_router, w1_e, w3_e, w2_e) = (
        t[0] for t in (w_mod, b_mod, g_norm1, g_norm2, w_in, b_in, w_qk_conv, b_qk_conv, w_h_conv, b_h_conv,
                       hf_w1, hf_b1, hf_w2, hf_b2, hf_w3, hf_freq, h_bias, w_a, w_b, w_out, w_group, b_group,
                       w_router, b_router, w1_e, w3_e, w2_e))
    bsz, L, d = x.shape
    lc = ctx.shape[1]
    seg = L // (L // GRID_W)
    chunk_c = min(lc, MLSTM_CHUNK)
    assert bsz + 1 <= 8 and lc % chunk_c == 0 and L % MLSTM_CHUNK == 0

    cond = jnp.zeros((8, d), F32).at[:bsz].set(c).at[bsz].set(c_ctx)
    mod = _adaln(cond, w_mod, b_mod).reshape(8, 6, d)
    modx = mod[:bsz]
    shift1, scale1, gate1, shift2, scale2, gate2 = (modx[:, i:i + 1] for i in range(6))
    shift1c = jnp.broadcast_to(mod[bsz, 0].reshape(1, 1, d), (bsz, 1, d))
    scale1c = jnp.broadcast_to(mod[bsz, 1].reshape(1, 1, d), (bsz, 1, d))

    w_in16 = w_in.astype(BF16)
    k_scale = jnp.full((M_WIDTH,), M_HEAD_DIM ** -0.5, F32)
    qk_scale = jnp.concatenate([jnp.ones((M_WIDTH,), F32), k_scale])
    w_gates, b_gates = w_in[:, IG0:M_COLS], b_in[IG0:M_COLS]

    hc = _norm_mod(ctx, g_norm1, shift1c, scale1c, lc)
    kc = _proj_conv_silu(hc, w_in16[:, K0:V0], b_in[K0:V0], w_qk_conv[:, M_WIDTH:], b_qk_conv[M_WIDTH:],
                         k_scale, lc, lc)
    vc = _proj_act(hc, w_in16[:, V0:O0], b_in[V0:O0], "none", BF16, lc)
    bcc, acc, arc = _gates(hc, w_gates, b_gates, chunk_c)
    zero_state = (jnp.zeros((bsz, 2, M_HEADS, M_HEAD_DIM, M_HEAD_DIM), F32),
                  jnp.zeros((bsz, 2, M_HEADS, 1, M_HEAD_DIM), F32),
                  jnp.zeros((bsz, 2, M_HEADS, 1, LANES), F32))
    _, ctx_state = _mlstm(None, (kc, 0), (vc, 0), bcc, acc, arc, zero_state, False, chunk_c)

    tm = ROW_TILE
    w_main = jnp.concatenate([w_in16[:, Q0:IG0], w_in16[:, GA0:IN_COLS]], axis=1)
    b_main = jnp.concatenate([b_in[Q0:IG0], b_in[GA0:IN_COLS]])
    _, dft_fast = _dft_factors(2 * L)
    pm, h, h_il = _proj_main(x, g_norm1, shift1, scale1, w_main, b_main, w_qk_conv, b_qk_conv, qk_scale,
                             seg, tm, dft_fast)
    bc, ac, ar = _gates(h, w_gates, b_gates, MLSTM_CHUNK)
    hdirs, _ = _mlstm((pm, PM_Q), (pm, PM_K), (pm, PM_V), bc, ac, ar, ctx_state, True, MLSTM_CHUNK)

    x0_t, s_t = _proj_hyena(h_il, w_in16[:, HY0:GA0], b_in[HY0:GA0], w_h_conv, b_h_conv, seg)
    hy = _hyena_long_conv(s_t, x0_t, h_bias, hf_w1, hf_b1, hf_w2, hf_b2, hf_w3, hf_freq)

    x1, h2 = _merge(hdirs, pm, hy, x, gate1, g_norm2, shift2, scale2,
                    w_a.astype(BF16), w_b.astype(BF16), w_out.astype(BF16))
    return _moe(h2, x1, gate2, g_final, w_group, b_group, w_router, b_router, w1_e, w3_e, w2_e)
```

```python
import functools
import math

import jax
import jax.numpy as jnp
import numpy as np
from jax import lax
from jax.experimental import pallas as pl
from jax.experimental.pallas import tpu as pltpu
from jax.experimental.pallas import tpu_sc as plsc

F32 = jnp.float32
BF16 = jnp.bfloat16

D_MODEL = 1024
GRID_W = 64
EPS = 1e-6
M_HEADS = 4
M_HEAD_DIM = 256
M_WIDTH = M_HEADS * M_HEAD_DIM
H_WIDTH = 1024
H_POS_BANDS = 16
H_FILTER_HIDDEN = 64
H_FAST_DECAY_PCT = 0.3
H_SLOW_DECAY_PCT = 1.5
H_DECAY_TARGET = 1e-2
N_GROUPS = 8
EXPERTS_PER_GROUP = 8
N_EXPERTS = N_GROUPS * EXPERTS_PER_GROUP
D_EXPERT = 512
Q0 = 0
K0 = Q0 + M_WIDTH
V0 = K0 + M_WIDTH
O0 = V0 + M_WIDTH
IG0 = O0 + M_WIDTH
FG0 = IG0 + 2 * M_HEADS
M_COLS = FG0 + 2 * M_HEADS
HY0 = M_COLS
GA0 = HY0 + 3 * H_WIDTH
GB0 = GA0 + D_MODEL
IN_COLS = GB0 + D_MODEL

LANES = 128
MLSTM_CHUNK = 512
NEG_BIG = -1e30
VMEM_LIMIT = 48 * 1024 * 1024
ROW_TILE = 1024
ADALN_TN = 1536
SMALL_TN = 512
MERGE_TM = 512
COMBINE_TM = 512


def _cparams(*sem):
    return pltpu.CompilerParams(dimension_semantics=sem, vmem_limit_bytes=VMEM_LIMIT)


def _adaln_kernel(c_ref, w_ref, b_ref, o_ref):
    s = c_ref[...]
    s = s * jax.nn.sigmoid(s)
    o_ref[...] = jnp.dot(s.astype(BF16), w_ref[...].astype(BF16), preferred_element_type=F32) + b_ref[...]


def _adaln(cond, w_mod, b_mod):
    n = w_mod.shape[1]
    tn = ADALN_TN
    return pl.pallas_call(
        _adaln_kernel,
        grid=(n // tn,),
        in_specs=[pl.BlockSpec((8, D_MODEL), lambda j: (0, 0)),
                  pl.BlockSpec((D_MODEL, tn), lambda j: (0, j)),
                  pl.BlockSpec((1, tn), lambda j: (0, j))],
        out_specs=pl.BlockSpec((8, tn), lambda j: (0, j)),
        out_shape=jax.ShapeDtypeStruct((8, n), F32),
        compiler_params=_cparams("arbitrary"),
        name="adaln",
    )(cond, w_mod, b_mod.reshape(1, n))


def _norm_mod_kernel(x_ref, g_ref, sh_ref, sc_ref, o_ref):
    x = x_ref[...]
    y = x * lax.rsqrt(jnp.mean(x * x, axis=-1, keepdims=True) + EPS)
    y = y * g_ref[...]
    o_ref[...] = (y * (1.0 + sc_ref[...]) + sh_ref[...]).astype(o_ref.dtype)


def _norm_mod(x, g, shift, scale, tm):
    bsz, L, d = x.shape
    return pl.pallas_call(
        _norm_mod_kernel,
        grid=(bsz, L // tm),
        in_specs=[pl.BlockSpec((None, tm, d), lambda b, i: (b, i, 0)),
                  pl.BlockSpec((1, d), lambda b, i: (0, 0)),
                  pl.BlockSpec((None, 1, d), lambda b, i: (b, 0, 0)),
                  pl.BlockSpec((None, 1, d), lambda b, i: (b, 0, 0))],
        out_specs=pl.BlockSpec((None, tm, d), lambda b, i: (b, i, 0)),
        out_shape=jax.ShapeDtypeStruct((bsz, L, d), BF16),
        compiler_params=_cparams("parallel", "parallel"),
        name="norm_mod",
    )(x, g.reshape(1, d), shift, scale)


def _conv3(z, wc, bc, seg):
    tm = z.shape[0]
    pos = lax.broadcasted_iota(jnp.int32, z.shape, 0) & (seg - 1)
    zp = jnp.where(pos == 0, 0.0, pltpu.roll(z, 1, 0))
    zn = jnp.where(pos == seg - 1, 0.0, pltpu.roll(z, tm - 1, 0))
    return zp * wc[0:1, :] + z * wc[1:2, :] + zn * wc[2:3, :] + bc


def _proj_act_kernel(h_ref, w_ref, b_ref, o_ref, *, act):
    z = jnp.dot(h_ref[...], w_ref[...], preferred_element_type=F32) + b_ref[...]
    if act == "sigmoid":
        z = jax.nn.sigmoid(z)
    o_ref[...] = z.astype(o_ref.dtype)


def _proj_act(h, w, b, act, out_dtype, tm, tn=SMALL_TN):
    bsz, L, d = h.shape
    n = w.shape[1]
    return pl.pallas_call(
        functools.partial(_proj_act_kernel, act=act),
        grid=(bsz, L // tm, n // tn),
        in_specs=[pl.BlockSpec((None, tm, d), lambda b_, i, j: (b_, i, 0)),
                  pl.BlockSpec((d, tn), lambda b_, i, j: (0, j)),
                  pl.BlockSpec((1, tn), lambda b_, i, j: (0, j))],
        out_specs=pl.BlockSpec((None, tm, tn), lambda b_, i, j: (b_, i, j)),
        out_shape=jax.ShapeDtypeStruct((bsz, L, n), out_dtype),
        compiler_params=_cparams("parallel", "parallel", "arbitrary"),
        name="proj_" + act,
    )(h, w, b.reshape(1, n))


def _proj_conv_silu_kernel(h_ref, w_ref, b_ref, wc_ref, bc_ref, cs_ref, o_ref, *, seg):
    z = jnp.dot(h_ref[...], w_ref[...], preferred_element_type=F32) + b_ref[...]
    y = _conv3(z, wc_ref[...], bc_ref[...], seg)
    y = y * jax.nn.sigmoid(y)
    o_ref[...] = (y * cs_ref[...]).astype(o_ref.dtype)


def _proj_conv_silu(h, w, b, wc, bc, colscale, seg, tm, tn=SMALL_TN):
    bsz, L, d = h.shape
    n = w.shape[1]
    col = lambda b_, i, j: (0, j)
    return pl.pallas_call(
        functools.partial(_proj_conv_silu_kernel, seg=seg),
        grid=(bsz, L // tm, n // tn),
        in_specs=[pl.BlockSpec((None, tm, d), lambda b_, i, j: (b_, i, 0)),
                  pl.BlockSpec((d, tn), col),
                  pl.BlockSpec((1, tn), col),
                  pl.BlockSpec((3, tn), col),
                  pl.BlockSpec((1, tn), col),
                  pl.BlockSpec((1, tn), col)],
        out_specs=pl.BlockSpec((None, tm, tn), lambda b_, i, j: (b_, i, j)),
        out_shape=jax.ShapeDtypeStruct((bsz, L, n), BF16),
        compiler_params=_cparams("parallel", "parallel", "arbitrary"),
        name="proj_conv_silu",
    )(h, w, b.reshape(1, n), wc, bc.reshape(1, n), colscale.reshape(1, n))


PROJ_TN = 1024
PROJ_SUB = 512
PM_Q, PM_K, PM_V, PM_O, PM_GA, PM_GB = range(6)


def _proj_main_kernel(x_ref, g_ref, sh_ref, sc_ref, w_ref, b_ref, wc_ref, bc_ref, cs_ref,
                      o_ref, h_ref, hi_hbm, hp_sc, sem, *, seg):
    b, i, j = pl.program_id(0), pl.program_id(1), pl.program_id(2)
    n2, jt = hi_hbm.shape[2], hi_hbm.shape[3]

    def interleave_copy(jj):
        return pltpu.make_async_copy(hp_sc.at[pl.ds(jj * n2, n2)], hi_hbm.at[b, i, :, jj, :], sem)

    @pl.when(j == 0)
    def _():
        x = x_ref[...]
        y = x * lax.rsqrt(jnp.mean(x * x, axis=-1, keepdims=True) + EPS) * g_ref[...]
        y = y * (1.0 + sc_ref[...]) + sh_ref[...]
        h_ref[...] = y.astype(h_ref.dtype)
        hp_sc[...] = _pack_bf16_pairs(y)
        for jj in range(jt):
            interleave_copy(jj).start()

    @pl.when(j == pl.num_programs(2) - 1)
    def _():
        for jj in range(jt):
            interleave_copy(jj).wait()

    def run(epilogue):
        for c in range(PROJ_TN // PROJ_SUB):
            sl = slice(c * PROJ_SUB, (c + 1) * PROJ_SUB)
            z = jnp.dot(h_ref[...], w_ref[:, sl], preferred_element_type=F32) + b_ref[:, sl]
            o_ref[:, sl] = epilogue(z, sl).astype(o_ref.dtype)

    def conv_silu(z, sl):
        y = _conv3(z, wc_ref[:, sl], bc_ref[:, sl], seg)
        return (y * jax.nn.sigmoid(y)) * cs_ref[:, sl]

    @pl.when(j <= PM_K)
    def _():
        run(conv_silu)

    @pl.when(j == PM_V)
    def _():
        run(lambda z, sl: z)

    @pl.when(j >= PM_O)
    def _():
        run(lambda z, sl: jax.nn.sigmoid(z))


def _proj_main(x, g, shift, scale, w, b, wc, bc, colscale, seg, tm, n2):
    bsz, L, d = x.shape
    n = w.shape[1]
    jt = tm // n2
    qk = lambda b_, i, j: (0, jnp.minimum(j, PM_K))
    row = pl.BlockSpec((None, tm, d), lambda b_, i, j: (b_, i, 0))
    bvec = pl.BlockSpec((None, 1, d), lambda b_, i, j: (b_, 0, 0))
    return pl.pallas_call(
        functools.partial(_proj_main_kernel, seg=seg),
        grid=(bsz, L // tm, n // PROJ_TN),
        in_specs=[row, pl.BlockSpec((1, d), lambda b_, i, j: (0, 0)), bvec, bvec,
                  pl.BlockSpec((d, PROJ_TN), lambda b_, i, j: (0, j)),
                  pl.BlockSpec((1, PROJ_TN), lambda b_, i, j: (0, j)),
                  pl.BlockSpec((3, PROJ_TN), qk),
                  pl.BlockSpec((1, PROJ_TN), qk),
                  pl.BlockSpec((1, PROJ_TN), qk)],
        out_specs=[pl.BlockSpec((None, tm, PROJ_TN), lambda b_, i, j: (b_, i, j)), row,
                   pl.BlockSpec(memory_space=pl.ANY)],
        out_shape=[jax.ShapeDtypeStruct((bsz, L, n), BF16), jax.ShapeDtypeStruct((bsz, L, d), BF16),
                   jax.ShapeDtypeStruct((bsz, L // tm, n2, jt, d // 2), jnp.uint32)],
        scratch_shapes=[pltpu.VMEM((tm, d // 2), jnp.uint32), pltpu.SemaphoreType.DMA(())],
        compiler_params=_cparams("parallel", "parallel", "arbitrary"),
        name="proj_main",
    )(x, g.reshape(1, d), shift, scale, w, b.reshape(1, n), wc, bc.reshape(1, -1), colscale.reshape(1, -1))


def _conv3_interleaved(z, wc, bc, seg, jt):
    grp = seg * jt
    pad = jnp.zeros((jt, z.shape[1]), z.dtype)
    prev, nxt = [], []
    for g0 in range(0, z.shape[0], grp):
        zg = z[g0:g0 + grp]
        prev += [pad, zg[:grp - jt]]
        nxt += [zg[jt:], pad]
    zp = jnp.concatenate(prev, axis=0)
    zn = jnp.concatenate(nxt, axis=0)
    return zp * wc[0:1, :] + z * wc[1:2, :] + zn * wc[2:3, :] + bc


def _proj_hyena_kernel(h_ref, w0_ref, w1_ref, w2_ref, b_ref, wc_ref, bc_ref, x0_ref, s_ref, *, seg):
    n2, jt = s_ref.shape[0], s_ref.shape[1]
    h = _unpack_bf16_pairs(h_ref[...].reshape(n2 * jt, h_ref.shape[2]))
    us = []
    for g, w_ref in enumerate((w0_ref, w1_ref, w2_ref)):
        z = jnp.dot(h, w_ref[...], preferred_element_type=F32) + b_ref[g]
        us.append(_conv3_interleaved(z, wc_ref[g], bc_ref[g], seg, jt))
    x0_ref[...] = _pack_bf16_pairs(us[0]).reshape(x0_ref.shape)
    s_ref[...] = (us[1] * us[2]).reshape(s_ref.shape)


def _proj_hyena(hi, w, b, wc, bc, seg):
    bsz, nt, n2, jt, dp = hi.shape
    d, tm = 2 * dp, n2 * jt
    L = nt * tm
    tn = DFT_C_TILE
    nblk = H_WIDTH // tn
    assert n2 % seg == 0 and (jt % 8 == 0 or nt == 1)
    b3 = b.reshape(3, 1, H_WIDTH)
    wc3 = wc.reshape(3, 3, H_WIDTH).transpose(1, 0, 2)
    bc3 = bc.reshape(3, 1, H_WIDTH)
    return pl.pallas_call(
        functools.partial(_proj_hyena_kernel, seg=seg),
        grid=(bsz, nt, nblk),
        in_specs=[pl.BlockSpec((None, None, n2, jt, dp), lambda b_, i, j: (b_, i, 0, 0, 0)),
                  pl.BlockSpec((d, tn), lambda b_, i, j: (0, j)),
                  pl.BlockSpec((d, tn), lambda b_, i, j: (0, nblk + j)),
                  pl.BlockSpec((d, tn), lambda b_, i, j: (0, 2 * nblk + j)),
                  pl.BlockSpec((3, 1, tn), lambda b_, i, j: (0, 0, j)),
                  pl.BlockSpec((3, 3, tn), lambda b_, i, j: (0, 0, j)),
                  pl.BlockSpec((3, 1, tn), lambda b_, i, j: (0, 0, j))],
        out_specs=[pl.BlockSpec((None, n2, jt, tn // 2), lambda b_, i, j: (b_, 0, i, j)),
                   pl.BlockSpec((None, n2, jt, tn), lambda b_, i, j: (b_, 0, i, j))],
        out_shape=[jax.ShapeDtypeStruct((bsz, n2, L // n2, H_WIDTH // 2), jnp.uint32),
                   jax.ShapeDtypeStruct((bsz, n2, L // n2, H_WIDTH), F32)],
        compiler_params=_cparams("parallel", "parallel", "arbitrary"),
        name="proj_hyena",
    )(hi, w, w, w, b3, wc3, bc3)


N_GATES = 4 * M_HEADS


def _split3(x):
    hi = x.astype(BF16)
    r1 = x - hi.astype(F32)
    mid = r1.astype(BF16)
    lo = (r1 - mid.astype(F32)).astype(BF16)
    return hi, mid, lo


def _log_sigmoid(x):
    return jnp.minimum(x, 0.0) - jnp.log1p(jnp.exp(-jnp.abs(x)))


def _gates_kernel(h_ref, w_ref, wt_ref, b_ref, bt_ref, bc_ref, ac_ref, ar_ref):
    h = h_ref[...]
    t = h.shape[0]
    z = jnp.dot(h, w_ref[...], preferred_element_type=F32) + b_ref[...]
    zt = lax.dot_general(wt_ref[...], h, (((1,), (1,)), ((), ())),
                         preferred_element_type=F32) + bt_ref[...]
    r = lax.broadcasted_iota(jnp.int32, (t, t), 0)
    c = lax.broadcasted_iota(jnp.int32, (t, t), 1)
    lower = (r >= c).astype(BF16)
    upper = (r <= c).astype(BF16)
    g8 = FG_LANE0

    lf = _log_sigmoid(z)
    lane = lax.broadcasted_iota(jnp.int32, z.shape, 1)
    is_fg = (lane >= g8) & (lane < 2 * g8)
    terms = [jnp.where(is_fg, p.astype(F32), 0.0) for p in _split3(lf)]
    packed = terms[0] + pltpu.roll(terms[1], 2 * g8, 1) + pltpu.roll(terms[2], 4 * g8, 1)
    cfp = jnp.dot(lower, packed.astype(BF16), preferred_element_type=F32)
    cf = cfp + pltpu.roll(cfp, LANES - 2 * g8, 1) + pltpu.roll(cfp, LANES - 4 * g8, 1)
    cb = cf[t - 1:t, :] - cf + lf
    bc = jnp.where(lane < g8 + M_HEADS, cf, cb)
    bc = pltpu.roll(bc, LANES - g8, 1)
    bc_ref[...] = bc
    ac_ref[...] = z - bc

    lft = _log_sigmoid(zt[g8:, :])
    stacked = jnp.concatenate([p.astype(F32) for p in _split3(lft)] + [jnp.zeros_like(lft)], axis=0)
    cft3 = jnp.dot(stacked.astype(BF16), upper, preferred_element_type=F32)
    cft = cft3[0:g8] + cft3[g8:2 * g8] + cft3[2 * g8:3 * g8]
    cbt = cft[:, t - 1:t] - cft + lft
    row = lax.broadcasted_iota(jnp.int32, cft.shape, 0)
    ar_ref[...] = zt[:g8, :] - jnp.where(row < M_HEADS, cft, cbt)


FG_LANE0 = 2 * M_HEADS


def _gates(h, w_g, b_g, chunk):
    bsz, L, d = h.shape
    w_pad = jnp.zeros((d, LANES), F32).at[:, :N_GATES].set(w_g).astype(BF16)
    b_pad = jnp.zeros((1, LANES), F32).at[0, :N_GATES].set(b_g)
    wt = w_g.T.astype(BF16)
    bt = b_g.reshape(N_GATES, 1)
    tok = pl.BlockSpec((None, chunk, LANES), lambda b_, i: (b_, i, 0))
    return pl.pallas_call(
        _gates_kernel,
        grid=(bsz, L // chunk),
        in_specs=[pl.BlockSpec((None, chunk, d), lambda b_, i: (b_, i, 0)),
                  pl.BlockSpec((d, LANES), lambda b_, i: (0, 0)),
                  pl.BlockSpec((N_GATES, d), lambda b_, i: (0, 0)),
                  pl.BlockSpec((1, LANES), lambda b_, i: (0, 0)),
                  pl.BlockSpec((N_GATES, 1), lambda b_, i: (0, 0))],
        out_specs=[tok, tok, pl.BlockSpec((None, FG_LANE0, chunk), lambda b_, i: (b_, 0, i))],
        out_shape=[jax.ShapeDtypeStruct((bsz, L, LANES), F32),
                   jax.ShapeDtypeStruct((bsz, L, LANES), F32),
                   jax.ShapeDtypeStruct((bsz, FG_LANE0, L), F32)],
        compiler_params=_cparams("parallel", "parallel"),
        name="mlstm_gates",
    )(h, w_pad, wt, b_pad, bt)


def _mlstm_kernel(*refs, emit_h, n_chunks):
    if emit_h:
        (q_ref, k_ref, v_ref, bc_ref, ac_ref, ar_ref, c0_ref, n0_ref, m0_ref,
         h_ref, cf_ref, nf_ref, mf_ref, c_sc, n_sc, m_sc) = refs
    else:
        (k_ref, v_ref, bc_ref, ac_ref, ar_ref, c0_ref, n0_ref, m0_ref,
         cf_ref, nf_ref, mf_ref, c_sc, n_sc, m_sc) = refs
    d = pl.program_id(1)
    j = pl.program_id(2)
    fwd = d == 0
    t = k_ref.shape[0]
    dh = M_HEAD_DIM

    @pl.when(j == 0)
    def _():
        c_sc[...] = c0_ref[...]
        n_sc[...] = n0_ref[...]
        m_sc[...] = m0_ref[...]

    r = lax.broadcasted_iota(jnp.int32, (t, t), 0)
    c = lax.broadcasted_iota(jnp.int32, (t, t), 1)
    causal = jnp.where(fwd, r - c, c - r) >= 0
    bc_all = bc_ref[...]
    ac_all = ac_ref[...]
    ar_all = ar_ref[...]
    for hd in range(M_HEADS):
        sl = slice(hd * dh, (hd + 1) * dh)
        bc = jnp.where(fwd, bc_all[:, hd:hd + 1], bc_all[:, M_HEADS + hd:M_HEADS + hd + 1])
        ac = jnp.where(fwd, ac_all[:, hd:hd + 1], ac_all[:, M_HEADS + hd:M_HEADS + hd + 1])
        ar = jnp.where(fwd, ar_all[hd:hd + 1, :], ar_all[M_HEADS + hd:M_HEADS + hd + 1, :])
        b_tot = jnp.where(fwd, bc[t - 1:t, :], bc[0:1, :])
        m_prev = m_sc[hd][:, 0:1]
        k_h = k_ref[:, sl]
        v_h = v_ref[:, sl]
        if emit_h:
            q_h = q_ref[:, sl]
            dm = jnp.where(causal, bc + ar, NEG_BIG)
            inter = bc + m_prev
            m_t = jnp.maximum(inter, jnp.max(dm, axis=1, keepdims=True))
            qk = lax.dot_general(q_h, k_h, (((1,), (1,)), ((), ())), preferred_element_type=F32)
            s = qk * jnp.exp(dm - m_t)
            carry = jnp.exp(inter - m_t)
            num = (jnp.dot(s.astype(BF16), v_h, preferred_element_type=F32)
                   + carry * jnp.dot(q_h, c_sc[hd].astype(BF16), preferred_element_type=F32))
            den = (jnp.sum(s, axis=1, keepdims=True)
                   + carry * jnp.sum(q_h.astype(F32) * n_sc[hd], axis=1, keepdims=True))
            h_ref[:, sl] = (num / jnp.maximum(jnp.abs(den), jnp.exp(-m_t))).astype(h_ref.dtype)
        g = b_tot + ac
        m_new = jnp.maximum(b_tot + m_prev, jnp.max(g, axis=0, keepdims=True))
        wgt = jnp.exp(g - m_new)
        decay = jnp.exp(b_tot + m_prev - m_new)
        kw = k_h.astype(F32) * wgt
        c_sc[hd] = decay * c_sc[hd] + lax.dot_general(kw.astype(BF16), v_h, (((0,), (0,)), ((), ())),
                                                      preferred_element_type=F32)
        n_sc[hd] = decay * n_sc[hd] + jnp.sum(kw, axis=0, keepdims=True)
        m_sc[hd] = jnp.broadcast_to(m_new, (1, LANES))

    @pl.when(j == n_chunks - 1)
    def _():
        cf_ref[...] = c_sc[...]
        nf_ref[...] = n_sc[...]
        mf_ref[...] = m_sc[...]


def _mlstm(q, k, v, bc, ac, ar, state, emit_h, t):
    bsz, L, _ = k[0].shape
    nc = L // t
    seq = lambda b_, d, j: (b_, j + d * (nc - 1 - 2 * j), 0)
    st = lambda b_, d, j: (b_, d, 0, 0, 0)

    def tok(col):
        return pl.BlockSpec((None, t, M_WIDTH), lambda b_, d, j: (b_, j + d * (nc - 1 - 2 * j), col))

    gate_spec = pl.BlockSpec((None, t, LANES), seq)
    ar_spec = pl.BlockSpec((None, FG_LANE0, t), lambda b_, d, j: (b_, 0, j + d * (nc - 1 - 2 * j)))
    c_spec = pl.BlockSpec((None, None, M_HEADS, M_HEAD_DIM, M_HEAD_DIM), st)
    n_spec = pl.BlockSpec((None, None, M_HEADS, 1, M_HEAD_DIM), st)
    m_spec = pl.BlockSpec((None, None, M_HEADS, 1, LANES), st)
    state_shapes = [jax.ShapeDtypeStruct((bsz, 2, M_HEADS, M_HEAD_DIM, M_HEAD_DIM), F32),
                    jax.ShapeDtypeStruct((bsz, 2, M_HEADS, 1, M_HEAD_DIM), F32),
                    jax.ShapeDtypeStruct((bsz, 2, M_HEADS, 1, LANES), F32)]
    in_specs = [tok(k[1]), tok(v[1]), gate_spec, gate_spec, ar_spec, c_spec, n_spec, m_spec]
    args = [k[0], v[0], bc, ac, ar, *state]
    out_specs = [c_spec, n_spec, m_spec]
    out_shape = list(state_shapes)
    if emit_h:
        in_specs = [tok(q[1])] + in_specs
        args = [q[0]] + args
        out_specs = [pl.BlockSpec((None, None, t, M_WIDTH),
                                  lambda b_, d, j: (d, b_, j + d * (nc - 1 - 2 * j), 0))] + out_specs
        out_shape = [jax.ShapeDtypeStruct((2, bsz, L, M_WIDTH), BF16)] + out_shape
    outs = pl.pallas_call(
        functools.partial(_mlstm_kernel, emit_h=emit_h, n_chunks=nc),
        grid=(bsz, 2, nc),
        in_specs=in_specs,
        out_specs=out_specs,
        out_shape=out_shape,
        scratch_shapes=[pltpu.VMEM((M_HEADS, M_HEAD_DIM, M_HEAD_DIM), F32),
                        pltpu.VMEM((M_HEADS, 1, M_HEAD_DIM), F32),
                        pltpu.VMEM((M_HEADS, 1, LANES), F32)],
        compiler_params=_cparams("parallel", "parallel", "arbitrary"),
        name="mlstm" if emit_h else "mlstm_state",
    )(*args)
    if emit_h:
        return outs[0], tuple(outs[1:])
    return None, tuple(outs)


DFT_M_TILE = 8
DFT_C_TILE = 1024
DFT_INNER_C_TILE = 512
FEAT_ROWS = 16


def _filter_outer_kernel(bands_ref, w1t_ref, b1_ref, w2t_ref, b2_ref, w3p_ref, w3f_ref, fr_ref, dl_ref, l_ref,
                         a_ref, ss_ref, *, L, n1, n2):
    i = pl.program_id(0)
    h = n1 // 2
    cols = DFT_M_TILE * h

    def positions(shape, axis, side):
        q = lax.broadcasted_iota(jnp.int32, shape, axis)
        mm, jj = q // h, q % h
        n = n2 * (jj + side * h) + i * DFT_M_TILE + mm
        return n, jnp.where(n < L, n, 2 * L - n).astype(F32)

    taps = []
    sumsq = jnp.zeros((1, a_ref.shape[-1]), F32)
    for side, w3_ref in ((0, w3p_ref), (1, w3f_ref)):
        _, p_row = positions((1, cols), 1, side)
        t_row = p_row / float(max(L - 1, 1))
        ang = ((2 * math.pi / L) * p_row) * bands_ref[...]
        row = lax.broadcasted_iota(jnp.int32, (FEAT_ROWS, cols), 0)
        feats = jnp.concatenate([jnp.where(row == 0, t_row, 0.0), jnp.cos(ang), -jnp.sin(ang)], axis=0)
        fr = fr_ref[...]
        hid = jnp.sin(fr * (jnp.dot(w1t_ref[...], feats.astype(BF16), preferred_element_type=F32) + b1_ref[...]))
        hid = jnp.sin(fr * (jnp.dot(w2t_ref[...], hid.astype(BF16), preferred_element_type=F32) + b2_ref[...]))
        filt = lax.dot_general(hid.astype(BF16), w3_ref[...], (((0,), (0,)), ((), ())),
                               preferred_element_type=F32)
        n_col, p_col = positions((cols, 1), 0, side)
        t_col = p_col / float(max(L - 1, 1))
        kern = filt * jnp.exp(-t_col * jnp.abs(dl_ref[...]))
        kern = jnp.where(n_col == L, 0.0, kern)
        sumsq = sumsq + jnp.sum(kern * kern, axis=0, keepdims=True)
        taps.append(kern)

    for mm in range(DFT_M_TILE):
        x = jnp.concatenate([taps[0][mm * h:(mm + 1) * h], taps[1][mm * h:(mm + 1) * h]], axis=0)
        out = jnp.dot(l_ref[...], x.astype(BF16), preferred_element_type=F32)
        a_ref[0, :, mm, :] = out[:n1]
        a_ref[1, :, mm, :] = out[n1:]

    @pl.when(i == 0)
    def _():
        ss_ref[...] = jnp.zeros_like(ss_ref)

    ss_ref[...] += sumsq


def _filter_outer(L, n1, n2, fwd_r, w1, b1, w2, b2, w3, freq):
    hid = H_FILTER_HIDDEN
    bands = jnp.linspace(1e-4, H_POS_BANDS - 1, H_POS_BANDS, dtype=F32).reshape(H_POS_BANDS, 1)
    w1t = jnp.zeros((hid, 3 * FEAT_ROWS), F32)
    w1t = w1t.at[:, 0].set(w1[0]).at[:, FEAT_ROWS:2 * FEAT_ROWS].set(w1[1:1 + H_POS_BANDS].T)
    w1t = w1t.at[:, 2 * FEAT_ROWS:].set(w1[1 + H_POS_BANDS:].T).astype(BF16)
    w3h = w3.astype(BF16)
    max_decay = math.log(H_DECAY_TARGET) / H_FAST_DECAY_PCT
    min_decay = math.log(H_DECAY_TARGET) / H_SLOW_DECAY_PCT
    deltas = jnp.linspace(min_decay, max_decay, H_WIDTH, dtype=F32).reshape(1, H_WIDTH)
    col = lambda v: v.reshape(hid, 1)
    full = lambda a: pl.BlockSpec(a.shape, lambda i: (0,) * a.ndim)
    args = [bands, w1t, col(b1), w2.T.astype(BF16), col(b2)]
    return pl.pallas_call(
        functools.partial(_filter_outer_kernel, L=L, n1=n1, n2=n2),
        grid=(n2 // DFT_M_TILE,),
        in_specs=[full(a) for a in args]
        + [pl.BlockSpec((hid, H_WIDTH), lambda i: (0, 0)), pl.BlockSpec((hid, H_WIDTH), lambda i: (0, 1)),
           full(col(freq)), full(deltas), full(fwd_r)],
        out_specs=[pl.BlockSpec((2, n1, DFT_M_TILE, H_WIDTH), lambda i: (0, 0, i, 0)),
                   pl.BlockSpec((1, H_WIDTH), lambda i: (0, 0))],
        out_shape=[jax.ShapeDtypeStruct((2, n1, n2, H_WIDTH), F32),
                   jax.ShapeDtypeStruct((1, H_WIDTH), F32)],
        compiler_params=_cparams("arbitrary"),
        name="hyena_filter_outer",
    )(*args, w3h, w3h, col(freq), deltas, fwd_r)


def _dft_factors(n):
    lg = int(round(math.log2(n)))
    n1 = 1 << ((lg + 1) // 2)
    return n1, n // n1


def _dft_outer_matrices(n1):
    k = np.arange(n1)[:, None]
    n = np.arange(n1)[None, :]
    ang = 2.0 * np.pi * ((k * n) % n1) / n1
    cr, ci = np.cos(ang), -np.sin(ang)
    h = n1 // 2
    fwd_c = np.block([[cr[:, :h], -ci[:, :h]], [ci[:, :h], cr[:, :h]]])
    fwd_r = np.concatenate([cr, ci], axis=0)
    ir, ii = cr[:h, :], -ci[:h, :]
    inv = np.block([[ir, -ii], [ii, ir]])
    return (jnp.asarray(fwd_c, F32).astype(BF16), jnp.asarray(fwd_r, F32).astype(BF16),
            jnp.asarray(inv, F32).astype(BF16))


def _dft_inner_matrices(n1, n2):
    n = n1 * n2
    k2 = np.arange(n2)[:, None]
    m = np.arange(n2)[None, :]
    ang = 2.0 * np.pi * ((k2 * m) % n2) / n2
    fr, fi = np.cos(ang), -np.sin(ang)
    f = np.block([[fr, -fi], [fi, fr]])
    k1 = jnp.arange(n1, dtype=jnp.int32)[:, None]
    tw_ang = ((jnp.arange(n2, dtype=jnp.int32)[None, :] * k1) % n).astype(F32) * (2.0 * math.pi / n)
    rep = lambda t: jnp.broadcast_to(t[:, :, None], (n1, n2, LANES))
    return (jnp.asarray(f, F32).astype(BF16), jnp.asarray(f.T, F32).astype(BF16),
            rep(jnp.cos(tw_ang)), rep(-jnp.sin(tw_ang)))


def _outer_fwd_kernel(l_ref, s_ref, a_ref):
    n1 = a_ref.shape[1]
    for mm in range(s_ref.shape[1]):
        x = jnp.concatenate([s_ref[0, mm], s_ref[1, mm]], axis=0).astype(BF16)
        out = jnp.dot(l_ref[...], x, preferred_element_type=F32)
        a_ref[0, :, mm, :] = out[:n1]
        a_ref[1, :, mm, :] = out[n1:]


def _outer_fwd(lmat, s_t):
    _, n2, n1h, c = s_t.shape
    n1 = 2 * n1h
    tc = min(DFT_C_TILE, c)
    return pl.pallas_call(
        _outer_fwd_kernel,
        grid=(n2 // DFT_M_TILE, c // tc),
        in_specs=[pl.BlockSpec(lmat.shape, lambda m, j: (0, 0)),
                  pl.BlockSpec((2, DFT_M_TILE, n1h, tc), lambda m, j: (0, m, 0, j))],
        out_specs=pl.BlockSpec((2, n1, DFT_M_TILE, tc), lambda m, j: (0, 0, m, j)),
        out_shape=jax.ShapeDtypeStruct((2, n1, n2, c), F32),
        compiler_params=_cparams("parallel", "parallel"),
        name="dft_outer_fwd",
    )(lmat, s_t)


def _outer_inv_kernel(l_ref, b_ref, s_ref, x0_ref, ysc_ref, hb_ref, o_ref):
    n1h = s_ref.shape[2]
    for mm in range(b_ref.shape[1]):
        y = jnp.concatenate([b_ref[0, mm], b_ref[1, mm]], axis=0).astype(BF16)
        out = jnp.dot(l_ref[...], y, preferred_element_type=F32)
        for b in range(2):
            conv = out[b * n1h:(b + 1) * n1h]
            x0 = _unpack_bf16_pairs(x0_ref[b, mm]).astype(F32)
            hy = x0 * (conv * ysc_ref[...] + hb_ref[...] * s_ref[b, mm])
            o_ref[b, :, mm, :] = _pack_bf16_pairs(hy)


def _outer_inv(lmat, b_t, s_t, x0_t, yscale, h_bias):
    _, n2, n1, c = b_t.shape
    n1h = n1 // 2
    tc = min(DFT_C_TILE, c)
    vec = pl.BlockSpec((1, tc), lambda m, j: (0, j))
    hy = pl.pallas_call(
        _outer_inv_kernel,
        grid=(n2 // DFT_M_TILE, c // tc),
        in_specs=[pl.BlockSpec(lmat.shape, lambda m, j: (0, 0)),
                  pl.BlockSpec((2, DFT_M_TILE, n1, tc), lambda m, j: (0, m, 0, j)),
                  pl.BlockSpec((2, DFT_M_TILE, n1h, tc), lambda m, j: (0, m, 0, j)),
                  pl.BlockSpec((2, DFT_M_TILE, n1h, tc // 2), lambda m, j: (0, m, 0, j)),
                  vec, vec],
        out_specs=pl.BlockSpec((2, n1h, DFT_M_TILE, tc // 2), lambda m, j: (0, 0, m, j)),
        out_shape=jax.ShapeDtypeStruct((2, n1h, n2, c // 2), jnp.uint32),
        compiler_params=_cparams("parallel", "parallel"),
        name="dft_outer_inv",
    )(lmat, b_t, s_t, x0_t, yscale, h_bias.reshape(1, c))
    return hy.reshape(2, n1h * n2, c // 2)


DFT_K_TILE = 8


def _twiddled_inner_dft(f_ref, twr_ref, twi_ref, a_ref, kk):
    n2, c = a_ref.shape[2], a_ref.shape[3]
    twr = jnp.tile(twr_ref[kk], (1, c // LANES))
    twi = jnp.tile(twi_ref[kk], (1, c // LANES))
    ar, ai = a_ref[0, kk], a_ref[1, kk]
    a = jnp.concatenate([(ar * twr - ai * twi).astype(BF16), (ar * twi + ai * twr).astype(BF16)], axis=0)
    x = jnp.dot(f_ref[...], a, preferred_element_type=F32)
    return x[:n2], x[n2:], twr, twi


def _inner_fwd_kernel(f_ref, twr_ref, twi_ref, a_ref, o_ref):
    for kk in range(a_ref.shape[1]):
        xr, xi, _, _ = _twiddled_inner_dft(f_ref, twr_ref, twi_ref, a_ref, kk)
        o_ref[0, kk] = xr.astype(o_ref.dtype)
        o_ref[1, kk] = xi.astype(o_ref.dtype)


def _inner_specs(n1, n2, c):
    tc = min(DFT_INNER_C_TILE, c)
    kt = min(DFT_K_TILE, n1)
    blk = pl.BlockSpec((2, kt, n2, tc), lambda k, j: (0, k, 0, j))
    mat = pl.BlockSpec((2 * n2, 2 * n2), lambda k, j: (0, 0))
    tw = pl.BlockSpec((kt, n2, LANES), lambda k, j: (k, 0, 0))
    return blk, mat, tw, (n1 // kt, c // tc), kt, tc


def _inner_fwd(f, twr, twi, a):
    _, n1, n2, c = a.shape
    blk, mat, tw, grid, _, _ = _inner_specs(n1, n2, c)
    return pl.pallas_call(
        _inner_fwd_kernel,
        grid=grid,
        in_specs=[mat, tw, tw, blk],
        out_specs=blk,
        out_shape=jax.ShapeDtypeStruct((2, n1, n2, c), BF16),
        compiler_params=_cparams("parallel", "parallel"),
        name="dft_inner_filter",
    )(f, twr, twi, a)


def _inner_conv_kernel(f_ref, ft_ref, twr_ref, twi_ref, a_ref, k_ref, o_ref):
    n2 = a_ref.shape[2]
    for kk in range(a_ref.shape[1]):
        xr, xi, twr, twi = _twiddled_inner_dft(f_ref, twr_ref, twi_ref, a_ref, kk)
        kr, ki = k_ref[0, kk].astype(F32), k_ref[1, kk].astype(F32)
        yr = xr * kr - xi * ki
        yi = xr * ki + xi * kr
        y = jnp.concatenate([yr.astype(BF16), yi.astype(BF16)], axis=0)
        b = jnp.dot(ft_ref[...], y, preferred_element_type=F32)
        br, bi = b[:n2], b[n2:]
        o_ref[0, :, kk, :] = br * twr + bi * twi
        o_ref[1, :, kk, :] = bi * twr - br * twi


def _inner_conv(f, ft, twr, twi, a, kf):
    _, n1, n2, c = a.shape
    blk, mat, tw, grid, kt, tc = _inner_specs(n1, n2, c)
    return pl.pallas_call(
        _inner_conv_kernel,
        grid=grid,
        in_specs=[mat, mat, tw, tw, blk, blk],
        out_specs=pl.BlockSpec((2, n2, kt, tc), lambda k, j: (0, 0, k, j)),
        out_shape=jax.ShapeDtypeStruct((2, n2, n1, c), F32),
        compiler_params=_cparams("parallel", "parallel"),
        name="dft_inner_conv",
    )(f, ft, twr, twi, a, kf)


def _hyena_long_conv(s_t, x0_t, h_bias, w1, b1, w2, b2, w3, freq):
    bsz, n2, n1h, c = s_t.shape
    assert bsz == 2
    n1 = 2 * n1h
    L = n1h * n2
    fwd_c, fwd_r, inv = _dft_outer_matrices(n1)
    f, ft, twr, twi = _dft_inner_matrices(n1, n2)
    af, sumsq = _filter_outer(L, n1, n2, fwd_r, w1, b1, w2, b2, w3, freq)
    kf = _inner_fwd(f, twr, twi, af)
    a = _outer_fwd(fwd_c, s_t)
    b_t = _inner_conv(f, ft, twr, twi, a, kf)
    yscale = lax.rsqrt(sumsq + EPS) * (1.0 / (2 * L))
    return _outer_inv(inv, b_t, s_t, x0_t, yscale, h_bias)


def _pack_bf16_pairs(x):
    half = x.shape[1] // 2
    lo = pltpu.bitcast(x[:, :half].astype(BF16).astype(F32), jnp.uint32) >> 16
    hi = pltpu.bitcast(x[:, half:].astype(BF16).astype(F32), jnp.uint32) & jnp.uint32(0xFFFF0000)
    return lo | hi


def _unpack_bf16_pairs(p):
    lo = pltpu.bitcast(p << 16, F32).astype(BF16)
    hi = pltpu.bitcast(p & jnp.uint32(0xFFFF0000), F32).astype(BF16)
    return jnp.concatenate([lo, hi], axis=1)


def _merge_kernel(hf_ref, hb_ref, o_ref, hy_ref, ga_ref, gb_ref, x_ref,
                  gate_ref, g2_ref, sh_ref, sc_ref, wa_ref, wb_ref, wo_ref, x1_ref, h2_ref):
    a = o_ref[...].astype(F32) * (hf_ref[...].astype(F32) + hb_ref[...].astype(F32))
    half = DFT_C_TILE // 2
    hy = jnp.concatenate([_unpack_bf16_pairs(hy_ref[:, c * half:(c + 1) * half])
                          for c in range(hy_ref.shape[1] // half)], axis=1)
    pa = jnp.dot(a.astype(BF16), wa_ref[...], preferred_element_type=F32)
    pb = jnp.dot(hy, wb_ref[...], preferred_element_type=F32)
    mix = ga_ref[...].astype(F32) * pa + gb_ref[...].astype(F32) * pb
    out = jnp.dot(mix.astype(BF16), wo_ref[...], preferred_element_type=F32)
    x1 = x_ref[...] + gate_ref[...] * out
    x1_ref[...] = x1
    y = x1 * lax.rsqrt(jnp.mean(x1 * x1, axis=-1, keepdims=True) + EPS) * g2_ref[...]
    h2_ref[...] = _pack_bf16_pairs(y * (1.0 + sc_ref[...]) + sh_ref[...])


def _merge(hdirs, pm, hy, x, gate1, g2, shift2, scale2, w_a, w_b, w_out, tm=MERGE_TM):
    bsz, L, d = x.shape
    tok = pl.BlockSpec((None, tm, d), lambda b, i: (b, i, 0))

    def pm_tile(col):
        return pl.BlockSpec((None, tm, d), lambda b, i: (b, i, col))

    packed = pl.BlockSpec((None, tm, d // 2), lambda b, i: (b, i, 0))
    vec = pl.BlockSpec((1, d), lambda b, i: (0, 0))
    bvec = pl.BlockSpec((None, 1, d), lambda b, i: (b, 0, 0))
    wsp = pl.BlockSpec((d, d), lambda b, i: (0, 0), pipeline_mode=pl.Buffered(1))
    return pl.pallas_call(
        _merge_kernel,
        grid=(bsz, L // tm),
        in_specs=[pl.BlockSpec((None, None, tm, d), lambda b, i: (0, b, i, 0)),
                  pl.BlockSpec((None, None, tm, d), lambda b, i: (1, b, i, 0)),
                  pm_tile(PM_O), packed, pm_tile(PM_GA), pm_tile(PM_GB), tok,
                  bvec, vec, bvec, bvec, wsp, wsp, wsp],
        out_specs=[tok, packed],
        out_shape=[jax.ShapeDtypeStruct((bsz, L, d), F32), jax.ShapeDtypeStruct((bsz, L, d // 2), jnp.uint32)],
        compiler_params=_cparams("parallel", "parallel"),
        name="merge",
    )(hdirs, hdirs, pm, hy, pm, pm, x, gate1, g2.reshape(1, d), shift2, scale2, w_a, w_b, w_out)


MOE_BLOCK = 256
ROUTE_E1, ROUTE_E2, ROUTE_W1, ROUTE_W2 = 0, 1, 2, 3
EXP_LANE0 = N_GROUPS


def _first_lane_of_max(val, valid, lane):
    masked = jnp.where(valid, val, NEG_BIG)
    mx = jnp.max(masked, axis=1, keepdims=True)
    idx = jnp.min(jnp.where(valid & (masked == mx), lane, LANES), axis=1, keepdims=True)
    return mx, idx


MOE_TM = 1024


def _expert_onehots(rec):
    lane = lax.broadcasted_iota(jnp.int32, rec.shape, 1)
    oh1 = lane == rec[:, ROUTE_E1:ROUTE_E1 + 1].astype(jnp.int32)
    oh2 = lane == rec[:, ROUTE_E2:ROUTE_E2 + 1].astype(jnp.int32)
    return oh1, oh2


def _router_kernel(h_ref, w_ref, b_ref, r_ref, cnt_ref):
    logits = jnp.dot(_unpack_bf16_pairs(h_ref[...]), w_ref[...], preferred_element_type=F32) + b_ref[...]
    lane = lax.broadcasted_iota(jnp.int32, logits.shape, 1)
    is_g = lane < N_GROUPS
    gmax, gsel = _first_lane_of_max(logits, is_g, lane)
    gsum = jnp.sum(jnp.where(is_g, jnp.exp(logits - gmax), 0.0), axis=1, keepdims=True)
    gw = 1.0 / gsum
    lo = EXP_LANE0 + gsel * EXPERTS_PER_GROUP
    in_grp = (lane >= lo) & (lane < lo + EXPERTS_PER_GROUP)
    emax, l1 = _first_lane_of_max(logits, in_grp, lane)
    esum = jnp.sum(jnp.where(in_grp, jnp.exp(logits - emax), 0.0), axis=1, keepdims=True)
    e2max, l2 = _first_lane_of_max(logits, in_grp & (lane != l1), lane)
    v1 = 1.0 / esum
    v2 = jnp.exp(e2max - emax) / esum
    vs = v1 + v2
    w1 = gw * v1 / vs
    w2 = gw * v2 / vs
    e1 = (l1 - EXP_LANE0).astype(F32)
    e2 = (l2 - EXP_LANE0).astype(F32)
    rec = jnp.where(lane == ROUTE_E1, e1,
                    jnp.where(lane == ROUTE_E2, e2,
                              jnp.where(lane == ROUTE_W1, w1,
                                        jnp.where(lane == ROUTE_W2, w2, 0.0))))
    r_ref[...] = rec
    oh1, oh2 = _expert_onehots(rec)
    counts = jnp.sum((oh1 | oh2).astype(F32), axis=0, keepdims=True)
    cnt_ref[...] = jnp.broadcast_to(counts, cnt_ref.shape)


def _router(h2, w_group, b_group, w_router, b_router):
    n, dp = h2.shape
    d = 2 * dp
    tm = MOE_TM
    w = jnp.zeros((d, LANES), F32).at[:, :N_GROUPS].set(w_group).at[
        :, EXP_LANE0:EXP_LANE0 + N_EXPERTS].set(w_router).astype(BF16)
    b = jnp.zeros((1, LANES), F32).at[0, :N_GROUPS].set(b_group).at[
        0, EXP_LANE0:EXP_LANE0 + N_EXPERTS].set(b_router)
    return pl.pallas_call(
        _router_kernel,
        grid=(n // tm,),
        in_specs=[pl.BlockSpec((tm, dp), lambda i: (i, 0)),
                  pl.BlockSpec((d, LANES), lambda i: (0, 0)),
                  pl.BlockSpec((1, LANES), lambda i: (0, 0))],
        out_specs=[pl.BlockSpec((tm, LANES), lambda i: (i, 0)),
                   pl.BlockSpec((None, 8, LANES), lambda i: (i, 0, 0))],
        out_shape=[jax.ShapeDtypeStruct((n, LANES), F32), jax.ShapeDtypeStruct((n // tm, 8, LANES), F32)],
        compiler_params=_cparams("parallel"),
        name="moe_router",
    )(h2, w, b)


def _slots_kernel(r_ref, base_ref, dest_ref):
    rec = r_ref[...]
    tm = rec.shape[0]
    lane = lax.broadcasted_iota(jnp.int32, rec.shape, 1)
    oh1, oh2 = _expert_onehots(rec)
    r = lax.broadcasted_iota(jnp.int32, (tm, tm), 0)
    c = lax.broadcasted_iota(jnp.int32, (tm, tm), 1)
    earlier = (r > c).astype(BF16)
    rank = jnp.dot(earlier, (oh1 | oh2).astype(BF16), preferred_element_type=F32) + base_ref[0:1, :]
    d1 = jnp.sum(jnp.where(oh1, rank, 0.0), axis=1, keepdims=True)
    d2 = jnp.sum(jnp.where(oh2, rank, 0.0), axis=1, keepdims=True)
    dest_ref[...] = jnp.where(lane == 0, d1, jnp.where(lane == 1, d2, 0.0)).astype(jnp.int32)


def _slots(route, tile_counts):
    n = route.shape[0]
    tm = MOE_TM
    cnt = tile_counts[:, 0, :]
    totals = jnp.sum(cnt, axis=0)
    nblk = jnp.ceil(totals * (1.0 / MOE_BLOCK))
    first_slot = (jnp.cumsum(nblk) - nblk) * float(MOE_BLOCK)
    base = first_slot[None, :] + jnp.cumsum(cnt, axis=0) - cnt
    base = jnp.broadcast_to(base[:, None, :], tile_counts.shape)
    dest = pl.pallas_call(
        _slots_kernel,
        grid=(n // tm,),
        in_specs=[pl.BlockSpec((tm, LANES), lambda i: (i, 0)),
                  pl.BlockSpec((None, 8, LANES), lambda i: (i, 0, 0))],
        out_specs=pl.BlockSpec((tm, LANES), lambda i: (i, 0)),
        out_shape=jax.ShapeDtypeStruct((n, LANES), jnp.int32),
        compiler_params=_cparams("parallel"),
        name="moe_slots",
    )(route, base)
    return dest, totals


EXPERT_STEP_BLOCKS = 4


def _experts_kernel(be_ref, first_ref, nxt_ref, par_ref, nu_ref, x_ref, w1_hbm, w3_hbm, w2_hbm, o_ref,
                    w1f, w3f, w2f, w1b, w3b, w2b, sems):
    step = pl.program_id(0)

    def weight_copies(e, slot):
        return (pltpu.make_async_copy(w1_hbm.at[e], w1f.at[slot], sems.at[0, slot]),
                pltpu.make_async_copy(w3_hbm.at[e], w3f.at[slot], sems.at[1, slot]),
                pltpu.make_async_copy(w2_hbm.at[e], w2f.at[slot], sems.at[2, slot]))

    @pl.when(step == 0)
    def _():
        for cp in weight_copies(be_ref[0], 0):
            cp.start()

    for sub in range(EXPERT_STEP_BLOCKS):
        i = step * EXPERT_STEP_BLOCKS + sub
        rows = pl.ds(sub * MOE_BLOCK, MOE_BLOCK)

        @pl.when(first_ref[i] == 1)
        def _():
            slot = par_ref[i]

            @pl.when(nxt_ref[i] >= 0)
            def _():
                for cp in weight_copies(nxt_ref[i], 1 - slot):
                    cp.start()

            for cp in weight_copies(be_ref[i], slot):
                cp.wait()
            w1b[...] = w1f[slot].astype(BF16)
            w3b[...] = w3f[slot].astype(BF16)
            w2b[...] = w2f[slot].astype(BF16)

        @pl.when(i < nu_ref[0])
        def _():
            x = _unpack_bf16_pairs(x_ref[rows, :])
            a = jnp.dot(x, w1b[...], preferred_element_type=F32)
            b = jnp.dot(x, w3b[...], preferred_element_type=F32)
            hmid = (a * jax.nn.sigmoid(a)) * b
            o_ref[rows, :] = _pack_bf16_pairs(jnp.dot(hmid.astype(BF16), w2b[...], preferred_element_type=F32))

        @pl.when(i >= nu_ref[0])
        def _():
            o_ref[rows, :] = jnp.zeros((MOE_BLOCK, o_ref.shape[1]), o_ref.dtype)


def _experts(xs, nb, block_e, n_used, w1_e, w3_e, w2_e):
    dp = xs.shape[1]
    d, de = w1_e.shape[1], w1_e.shape[2]
    idx = jnp.arange(nb, dtype=jnp.int32)
    used = idx < n_used[0]
    first = used & ((idx == 0) | (block_e != jnp.roll(block_e, 1)))
    ordinal = jnp.cumsum(first.astype(jnp.int32)) - 1
    par = (ordinal % 2).astype(jnp.int32)
    first_pos = jnp.where(first, idx, nb)
    next_first = lax.cummin(jnp.concatenate([first_pos[1:], jnp.full((1,), nb, jnp.int32)]), reverse=True)
    nxt = jnp.where(next_first < nb, block_e[jnp.minimum(next_first, nb - 1)], -1).astype(jnp.int32)
    any_spec = pl.BlockSpec(memory_space=pl.ANY)
    assert nb % EXPERT_STEP_BLOCKS == 0
    step_rows = EXPERT_STEP_BLOCKS * MOE_BLOCK
    grid_spec = pltpu.PrefetchScalarGridSpec(
        num_scalar_prefetch=5,
        grid=(nb // EXPERT_STEP_BLOCKS,),
        in_specs=[pl.BlockSpec((step_rows, dp), lambda i, *_: (i, 0)), any_spec, any_spec, any_spec],
        out_specs=pl.BlockSpec((step_rows, dp), lambda i, *_: (i, 0)),
        scratch_shapes=[pltpu.VMEM((2, d, de), F32), pltpu.VMEM((2, d, de), F32), pltpu.VMEM((2, de, d), F32),
                        pltpu.VMEM((d, de), BF16), pltpu.VMEM((d, de), BF16), pltpu.VMEM((de, d), BF16),
                        pltpu.SemaphoreType.DMA((3, 2))],
    )
    return pl.pallas_call(
        _experts_kernel,
        grid_spec=grid_spec,
        out_shape=jax.ShapeDtypeStruct((nb * MOE_BLOCK, dp), xs.dtype),
        compiler_params=_cparams("arbitrary"),
        name="moe_experts",
    )(block_e, first.astype(jnp.int32), nxt, par, n_used, xs, w1_e, w3_e, w2_e)


SC_WINDOW = 128
SC_CORES, SC_SUBCORES = 2, 16
SC_WORKERS = SC_CORES * SC_SUBCORES


def _sc_worker_id():
    return lax.axis_index("c") * SC_SUBCORES + lax.axis_index("s")


def _sc_mesh():
    return plsc.VectorSubcoreMesh(core_axis_name="c", subcore_axis_name="s")


def _sc_dispatch(rows, dest0, dest1, pad_slots, n_rows):
    n, dv = rows.shape
    nwin, pwin = n // SC_WINDOW, pad_slots.shape[0] // SC_WINDOW
    assert n % (SC_WINDOW * SC_WORKERS) == 0 and pad_slots.shape[0] % (SC_WINDOW * SC_WORKERS) == 0
    zeros = jnp.zeros((SC_WINDOW, dv), rows.dtype)

    @pl.kernel(out_type=jax.ShapeDtypeStruct((n_rows, dv), rows.dtype), mesh=_sc_mesh(),
               scratch_types=[pltpu.VMEM((1, SC_WINDOW), jnp.int32), pltpu.VMEM((SC_WINDOW, dv), rows.dtype)],
               name="moe_dispatch_sc")
    def scatter(x_hbm, d0_hbm, d1_hbm, p_hbm, z_hbm, o_hbm, idx, buf):
        wid = _sc_worker_id()
        pltpu.sync_copy(z_hbm, buf)

        @pl.loop(0, pwin // SC_WORKERS)
        def _(t):
            w = t * SC_WORKERS + wid
            pltpu.sync_copy(p_hbm.at[pl.ds(w, 1)], idx)
            pltpu.sync_copy(buf, o_hbm.at[idx.at[0]])

        @pl.loop(0, nwin // SC_WORKERS)
        def _(t):
            w = t * SC_WORKERS + wid
            pltpu.sync_copy(x_hbm.at[pl.ds(w * SC_WINDOW, SC_WINDOW)], buf)
            for d_hbm in (d0_hbm, d1_hbm):
                pltpu.sync_copy(d_hbm.at[pl.ds(w, 1)], idx)
                pltpu.sync_copy(buf, o_hbm.at[idx.at[0]])

    return scatter(rows, dest0.reshape(nwin, SC_WINDOW), dest1.reshape(nwin, SC_WINDOW),
                   pad_slots.reshape(pwin, SC_WINDOW), zeros)


def _sc_gather(table, index):
    m = index.shape[0]
    dv = table.shape[1]
    nwin = m // SC_WINDOW
    assert m % (SC_WINDOW * SC_WORKERS) == 0

    @pl.kernel(out_type=jax.ShapeDtypeStruct((m, dv), table.dtype), mesh=_sc_mesh(),
               scratch_types=[pltpu.VMEM((1, SC_WINDOW), jnp.int32), pltpu.VMEM((SC_WINDOW, dv), table.dtype)],
               name="moe_gather_sc")
    def gather(x_hbm, i_hbm, o_hbm, idx, buf):
        wid = _sc_worker_id()

        @pl.loop(0, nwin // SC_WORKERS)
        def _(t):
            w = t * SC_WORKERS + wid
            pltpu.sync_copy(i_hbm.at[pl.ds(w, 1)], idx)
            pltpu.sync_copy(x_hbm.at[idx.at[0]], buf)
            pltpu.sync_copy(buf, o_hbm.at[pl.ds(w * SC_WINDOW, SC_WINDOW)])

    return gather(table, index.reshape(nwin, SC_WINDOW))


def _combine_planes_kernel(r_ref, ya_ref, yb_ref, x_ref, gate_ref, gf_ref, o_ref):
    rec = r_ref[...]
    y = (_unpack_bf16_pairs(ya_ref[...]).astype(F32) * rec[:, ROUTE_W1:ROUTE_W1 + 1]
         + _unpack_bf16_pairs(yb_ref[...]).astype(F32) * rec[:, ROUTE_W2:ROUTE_W2 + 1])
    x2 = x_ref[...] + gate_ref[...] * y
    o_ref[...] = x2 * lax.rsqrt(jnp.mean(x2 * x2, axis=-1, keepdims=True) + EPS) * gf_ref[...]


def _combine_planes(g, route, x1, gate2, g_final, tm=COMBINE_TM):
    bsz, L, d = x1.shape
    tpb = L // tm
    dp = g.shape[-1]
    return pl.pallas_call(
        _combine_planes_kernel,
        grid=(bsz, tpb),
        in_specs=[pl.BlockSpec((tm, LANES), lambda b, i: (b * tpb + i, 0)),
                  pl.BlockSpec((None, tm, dp), lambda b, i: (0, b * tpb + i, 0)),
                  pl.BlockSpec((None, tm, dp), lambda b, i: (1, b * tpb + i, 0)),
                  pl.BlockSpec((None, tm, d), lambda b, i: (b, i, 0)),
                  pl.BlockSpec((None, 1, d), lambda b, i: (b, 0, 0)),
                  pl.BlockSpec((1, d), lambda b, i: (0, 0))],
        out_specs=pl.BlockSpec((None, tm, d), lambda b, i: (b, i, 0)),
        out_shape=jax.ShapeDtypeStruct((bsz, L, d), F32),
        compiler_params=_cparams("parallel", "parallel"),
        name="moe_combine",
    )(route, g, g, x1, gate2, g_final.reshape(1, d))


def _moe(h2, x1, gate2, g_final, w_group, b_group, w_router, b_router, w1_e, w3_e, w2_e):
    bsz, L, d = x1.shape
    n = bsz * L
    h2f = h2.reshape(n, h2.shape[-1])
    route, tile_counts = _router(h2f, w_group, b_group, w_router, b_router)
    dest_rec, counts = _slots(route, tile_counts)
    nb = (2 * n) // MOE_BLOCK + N_EXPERTS
    cnt = counts[:N_EXPERTS].astype(jnp.int32)
    blocks_per_e = (cnt + MOE_BLOCK - 1) // MOE_BLOCK
    ends = jnp.cumsum(blocks_per_e)
    block_e = jnp.minimum(jnp.sum(ends[None, :] <= jnp.arange(nb, dtype=jnp.int32)[:, None], axis=1),
                          N_EXPERTS - 1).astype(jnp.int32)
    n_used = ends[-1:].astype(jnp.int32)
    n_slots = nb * MOE_BLOCK
    pad_j = jnp.arange(MOE_BLOCK, dtype=jnp.int32)[None, :]
    spare = n_slots + jnp.arange(N_EXPERTS * MOE_BLOCK, dtype=jnp.int32).reshape(N_EXPERTS, MOE_BLOCK)
    first_slot = ((ends - blocks_per_e) * MOE_BLOCK)[:, None]
    is_pad = cnt[:, None] + pad_j < blocks_per_e[:, None] * MOE_BLOCK
    pad_slots = jnp.where(is_pad, first_slot + cnt[:, None] + pad_j, spare).reshape(-1)
    xs = _sc_dispatch(h2f, dest_rec[:, 0], dest_rec[:, 1], pad_slots, n_slots + N_EXPERTS * MOE_BLOCK)
    ys = _experts(xs, nb, block_e, n_used, w1_e, w3_e, w2_e)
    g = _sc_gather(ys, jnp.concatenate([dest_rec[:, 0], dest_rec[:, 1]]))
    return _combine_planes(g.reshape(2, n, g.shape[-1]), route, x1, gate2, g_final)


def kernel(x, c, ctx, c_ctx, w_mod, b_mod, g_norm1, g_norm2, w_in, b_in, w_qk_conv, b_qk_conv,
           w_h_conv, b_h_conv, hf_w1, hf_b1, hf_w2, hf_b2, hf_w3, hf_freq, h_bias, w_a, w_b, w_out,
           w_group, b_group, w_router, b_router, w1_e, w3_e, w2_e, g_final):
    assert w_mod.shape[0] == 1, "single-layer block"
    (w_mod, b_mod, g_norm1, g_norm2, w_in, b_in, w_qk_conv, b_qk_conv, w_h_conv, b_h_conv, hf_w1, hf_b1, hf_w2,
     hf_b2, hf_w3, hf_freq, h_bias, w_a, w_b, w_out, w_group, b_group, w_router, b_router, w1_e, w3_e, w2_e) = (
        t[0] for t in (w_mod, b_mod, g_norm1, g_norm2, w_in, b_in, w_qk_conv, b_qk_conv, w_h_conv, b_h_conv,
                       hf_w1, hf_b1, hf_w2, hf_b2, hf_w3, hf_freq, h_bias, w_a, w_b, w_out, w_group, b_group,
                       w_router, b_router, w1_e, w3_e, w2_e))
    bsz, L, d = x.shape
    lc = ctx.shape[1]
    seg = L // (L // GRID_W)
    chunk_c = min(lc, MLSTM_CHUNK)
    assert bsz + 1 <= 8 and lc % chunk_c == 0 and L % MLSTM_CHUNK == 0

    cond = jnp.zeros((8, d), F32).at[:bsz].set(c).at[bsz].set(c_ctx)
    mod = _adaln(cond, w_mod, b_mod).reshape(8, 6, d)
    modx = mod[:bsz]
    shift1, scale1, gate1, shift2, scale2, gate2 = (modx[:, i:i + 1] for i in range(6))
    shift1c = jnp.broadcast_to(mod[bsz, 0].reshape(1, 1, d), (bsz, 1, d))
    scale1c = jnp.broadcast_to(mod[bsz, 1].reshape(1, 1, d), (bsz, 1, d))

    k_scale = jnp.full((M_WIDTH,), M_HEAD_DIM ** -0.5, F32)
    qk_scale = jnp.concatenate([jnp.ones((M_WIDTH,), F32), k_scale])
    w_gates, b_gates = w_in[:, IG0:M_COLS], b_in[IG0:M_COLS]

    hc = _norm_mod(ctx, g_norm1, shift1c, scale1c, lc)
    kc = _proj_conv_silu(hc, w_in[:, K0:V0].astype(BF16), b_in[K0:V0], w_qk_conv[:, M_WIDTH:],
                         b_qk_conv[M_WIDTH:], k_scale, lc, lc)
    vc = _proj_act(hc, w_in[:, V0:O0].astype(BF16), b_in[V0:O0], "none", BF16, lc)
    bcc, acc, arc = _gates(hc, w_gates, b_gates, chunk_c)
    zero_state = (jnp.zeros((bsz, 2, M_HEADS, M_HEAD_DIM, M_HEAD_DIM), F32),
                  jnp.zeros((bsz, 2, M_HEADS, 1, M_HEAD_DIM), F32),
                  jnp.zeros((bsz, 2, M_HEADS, 1, LANES), F32))
    _, ctx_state = _mlstm(None, (kc, 0), (vc, 0), bcc, acc, arc, zero_state, False, chunk_c)

    tm = ROW_TILE
    w_main = jnp.concatenate([w_in[:, Q0:IG0], w_in[:, GA0:IN_COLS]], axis=1).astype(BF16)
    b_main = jnp.concatenate([b_in[Q0:IG0], b_in[GA0:IN_COLS]])
    _, dft_fast = _dft_factors(2 * L)
    pm, h, h_il = _proj_main(x, g_norm1, shift1, scale1, w_main, b_main, w_qk_conv, b_qk_conv, qk_scale,
                             seg, tm, dft_fast)
    bc, ac, ar = _gates(h, w_gates, b_gates, MLSTM_CHUNK)
    hdirs, _ = _mlstm((pm, PM_Q), (pm, PM_K), (pm, PM_V), bc, ac, ar, ctx_state, True, MLSTM_CHUNK)

    x0_t, s_t = _proj_hyena(h_il, w_in[:, HY0:GA0].astype(BF16), b_in[HY0:GA0], w_h_conv, b_h_conv, seg)
    hy = _hyena_long_conv(s_t, x0_t, h_bias, hf_w1, hf_b1, hf_w2, hf_b2, hf_w3, hf_freq)

    x1, h2 = _merge(hdirs, pm, hy, x, gate1, g_norm2, shift2, scale2,
                    w_a.astype(BF16), w_b.astype(BF16), w_out.astype(BF16))
    return _moe(h2, x1, gate2, g_final, w_group, b_group, w_router, b_router, w1_e, w3_e, w2_e)
```

```python
import functools
import math

import jax
import jax.numpy as jnp
import numpy as np
from jax import lax
from jax.experimental import pallas as pl
from jax.experimental.pallas import tpu as pltpu
from jax.experimental.pallas import tpu_sc as plsc

F32 = jnp.float32
BF16 = jnp.bfloat16

D_MODEL = 1024
GRID_W = 64
EPS = 1e-6
M_HEADS = 4
M_HEAD_DIM = 256
M_WIDTH = M_HEADS * M_HEAD_DIM
H_WIDTH = 1024
H_POS_BANDS = 16
H_FILTER_HIDDEN = 64
H_FAST_DECAY_PCT = 0.3
H_SLOW_DECAY_PCT = 1.5
H_DECAY_TARGET = 1e-2
N_GROUPS = 8
EXPERTS_PER_GROUP = 8
N_EXPERTS = N_GROUPS * EXPERTS_PER_GROUP
D_EXPERT = 512
Q0 = 0
K0 = Q0 + M_WIDTH
V0 = K0 + M_WIDTH
O0 = V0 + M_WIDTH
IG0 = O0 + M_WIDTH
FG0 = IG0 + 2 * M_HEADS
M_COLS = FG0 + 2 * M_HEADS
HY0 = M_COLS
GA0 = HY0 + 3 * H_WIDTH
GB0 = GA0 + D_MODEL
IN_COLS = GB0 + D_MODEL

LANES = 128
MLSTM_CHUNK = 512
NEG_BIG = -1e30
VMEM_LIMIT = 48 * 1024 * 1024
ROW_TILE = 1024
ADALN_TN = 1536
SMALL_TN = 512
MERGE_TM = 512
COMBINE_TM = 512


def _cparams(*sem):
    return pltpu.CompilerParams(dimension_semantics=sem, vmem_limit_bytes=VMEM_LIMIT)


def _adaln_kernel(c_ref, w_ref, b_ref, o_ref):
    s = c_ref[...]
    s = s * jax.nn.sigmoid(s)
    o_ref[...] = jnp.dot(s.astype(BF16), w_ref[...].astype(BF16), preferred_element_type=F32) + b_ref[...]


def _adaln(cond, w_mod, b_mod):
    n = w_mod.shape[1]
    tn = ADALN_TN
    return pl.pallas_call(
        _adaln_kernel,
        grid=(n // tn,),
        in_specs=[pl.BlockSpec((8, D_MODEL), lambda j: (0, 0)),
                  pl.BlockSpec((D_MODEL, tn), lambda j: (0, j)),
                  pl.BlockSpec((1, tn), lambda j: (0, j))],
        out_specs=pl.BlockSpec((8, tn), lambda j: (0, j)),
        out_shape=jax.ShapeDtypeStruct((8, n), F32),
        compiler_params=_cparams("arbitrary"),
        name="adaln",
    )(cond, w_mod, b_mod.reshape(1, n))


def _norm_mod_kernel(x_ref, g_ref, sh_ref, sc_ref, o_ref):
    x = x_ref[...]
    y = x * lax.rsqrt(jnp.mean(x * x, axis=-1, keepdims=True) + EPS)
    y = y * g_ref[...]
    o_ref[...] = (y * (1.0 + sc_ref[...]) + sh_ref[...]).astype(o_ref.dtype)


def _norm_mod(x, g, shift, scale, tm):
    bsz, L, d = x.shape
    return pl.pallas_call(
        _norm_mod_kernel,
        grid=(bsz, L // tm),
        in_specs=[pl.BlockSpec((None, tm, d), lambda b, i: (b, i, 0)),
                  pl.BlockSpec((1, d), lambda b, i: (0, 0)),
                  pl.BlockSpec((None, 1, d), lambda b, i: (b, 0, 0)),
                  pl.BlockSpec((None, 1, d), lambda b, i: (b, 0, 0))],
        out_specs=pl.BlockSpec((None, tm, d), lambda b, i: (b, i, 0)),
        out_shape=jax.ShapeDtypeStruct((bsz, L, d), BF16),
        compiler_params=_cparams("parallel", "parallel"),
        name="norm_mod",
    )(x, g.reshape(1, d), shift, scale)


def _conv3(z, wc, bc, seg):
    tm = z.shape[0]
    pos = lax.broadcasted_iota(jnp.int32, z.shape, 0) & (seg - 1)
    zp = jnp.where(pos == 0, 0.0, pltpu.roll(z, 1, 0))
    zn = jnp.where(pos == seg - 1, 0.0, pltpu.roll(z, tm - 1, 0))
    return zp * wc[0:1, :] + z * wc[1:2, :] + zn * wc[2:3, :] + bc


def _proj_act_kernel(h_ref, w_ref, b_ref, o_ref, *, act):
    z = jnp.dot(h_ref[...], w_ref[...], preferred_element_type=F32) + b_ref[...]
    if act == "sigmoid":
        z = jax.nn.sigmoid(z)
    o_ref[...] = z.astype(o_ref.dtype)


def _proj_act(h, w, b, act, out_dtype, tm, tn=SMALL_TN):
    bsz, L, d = h.shape
    n = w.shape[1]
    return pl.pallas_call(
        functools.partial(_proj_act_kernel, act=act),
        grid=(bsz, L // tm, n // tn),
        in_specs=[pl.BlockSpec((None, tm, d), lambda b_, i, j: (b_, i, 0)),
                  pl.BlockSpec((d, tn), lambda b_, i, j: (0, j)),
                  pl.BlockSpec((1, tn), lambda b_, i, j: (0, j))],
        out_specs=pl.BlockSpec((None, tm, tn), lambda b_, i, j: (b_, i, j)),
        out_shape=jax.ShapeDtypeStruct((bsz, L, n), out_dtype),
        compiler_params=_cparams("parallel", "parallel", "arbitrary"),
        name="proj_" + act,
    )(h, w, b.reshape(1, n))


def _proj_conv_silu_kernel(h_ref, w_ref, b_ref, wc_ref, bc_ref, cs_ref, o_ref, *, seg):
    z = jnp.dot(h_ref[...], w_ref[...], preferred_element_type=F32) + b_ref[...]
    y = _conv3(z, wc_ref[...], bc_ref[...], seg)
    y = y * jax.nn.sigmoid(y)
    o_ref[...] = (y * cs_ref[...]).astype(o_ref.dtype)


def _proj_conv_silu(h, w, b, wc, bc, colscale, seg, tm, tn=SMALL_TN):
    bsz, L, d = h.shape
    n = w.shape[1]
    col = lambda b_, i, j: (0, j)
    return pl.pallas_call(
        functools.partial(_proj_conv_silu_kernel, seg=seg),
        grid=(bsz, L // tm, n // tn),
        in_specs=[pl.BlockSpec((None, tm, d), lambda b_, i, j: (b_, i, 0)),
                  pl.BlockSpec((d, tn), col),
                  pl.BlockSpec((1, tn), col),
                  pl.BlockSpec((3, tn), col),
                  pl.BlockSpec((1, tn), col),
                  pl.BlockSpec((1, tn), col)],
        out_specs=pl.BlockSpec((None, tm, tn), lambda b_, i, j: (b_, i, j)),
        out_shape=jax.ShapeDtypeStruct((bsz, L, n), BF16),
        compiler_params=_cparams("parallel", "parallel", "arbitrary"),
        name="proj_conv_silu",
    )(h, w, b.reshape(1, n), wc, bc.reshape(1, n), colscale.reshape(1, n))


WEIGHT_PREP_ROWS = 128


def _weight_prep_kernel(w_ref, main_ref, hy_ref):
    main_ref[:, :IG0 - Q0] = w_ref[:, Q0:IG0].astype(BF16)
    main_ref[:, IG0 - Q0:] = w_ref[:, GA0:IN_COLS].astype(BF16)
    hy_ref[...] = w_ref[:, HY0:GA0].astype(BF16)


def _weight_prep(w_in, tr=WEIGHT_PREP_ROWS):
    d, cols = w_in.shape
    n_main = IG0 - Q0 + IN_COLS - GA0
    return pl.pallas_call(
        _weight_prep_kernel,
        grid=(d // tr,),
        in_specs=[pl.BlockSpec((tr, cols), lambda i: (i, 0))],
        out_specs=[pl.BlockSpec((tr, n_main), lambda i: (i, 0)), pl.BlockSpec((tr, GA0 - HY0), lambda i: (i, 0))],
        out_shape=[jax.ShapeDtypeStruct((d, n_main), BF16), jax.ShapeDtypeStruct((d, GA0 - HY0), BF16)],
        compiler_params=_cparams("parallel"),
        name="weight_prep",
    )(w_in)


PROJ_TN = 1024
PROJ_SUB = 512
PM_Q, PM_K, PM_V, PM_O, PM_GA, PM_GB = range(6)


def _proj_main_kernel(x_ref, g_ref, sh_ref, sc_ref, w_ref, b_ref, wc_ref, bc_ref, cs_ref,
                      o_ref, h_ref, hi_hbm, hp_sc, sem, *, seg):
    b, i, j = pl.program_id(0), pl.program_id(1), pl.program_id(2)
    n2, jt = hi_hbm.shape[2], hi_hbm.shape[3]

    def interleave_copy(jj):
        return pltpu.make_async_copy(hp_sc.at[pl.ds(jj * n2, n2)], hi_hbm.at[b, i, :, jj, :], sem)

    @pl.when(j == 0)
    def _():
        x = x_ref[...]
        y = x * lax.rsqrt(jnp.mean(x * x, axis=-1, keepdims=True) + EPS) * g_ref[...]
        y = y * (1.0 + sc_ref[...]) + sh_ref[...]
        h_ref[...] = y.astype(h_ref.dtype)
        hp_sc[...] = _pack_bf16_pairs(y)
        for jj in range(jt):
            interleave_copy(jj).start()

    @pl.when(j == pl.num_programs(2) - 1)
    def _():
        for jj in range(jt):
            interleave_copy(jj).wait()

    def run(epilogue):
        for c in range(PROJ_TN // PROJ_SUB):
            sl = slice(c * PROJ_SUB, (c + 1) * PROJ_SUB)
            z = jnp.dot(h_ref[...], w_ref[:, sl], preferred_element_type=F32) + b_ref[:, sl]
            o_ref[:, sl] = epilogue(z, sl).astype(o_ref.dtype)

    def conv_silu(z, sl):
        y = _conv3(z, wc_ref[:, sl], bc_ref[:, sl], seg)
        return (y * jax.nn.sigmoid(y)) * cs_ref[:, sl]

    @pl.when(j <= PM_K)
    def _():
        run(conv_silu)

    @pl.when(j == PM_V)
    def _():
        run(lambda z, sl: z)

    @pl.when(j >= PM_O)
    def _():
        run(lambda z, sl: jax.nn.sigmoid(z))


def _proj_main(x, g, shift, scale, w, b, wc, bc, colscale, seg, tm, n2):
    bsz, L, d = x.shape
    n = w.shape[1]
    jt = tm // n2
    qk = lambda b_, i, j: (0, jnp.minimum(j, PM_K))
    row = pl.BlockSpec((None, tm, d), lambda b_, i, j: (b_, i, 0))
    bvec = pl.BlockSpec((None, 1, d), lambda b_, i, j: (b_, 0, 0))
    return pl.pallas_call(
        functools.partial(_proj_main_kernel, seg=seg),
        grid=(bsz, L // tm, n // PROJ_TN),
        in_specs=[row, pl.BlockSpec((1, d), lambda b_, i, j: (0, 0)), bvec, bvec,
                  pl.BlockSpec((d, PROJ_TN), lambda b_, i, j: (0, j)),
                  pl.BlockSpec((1, PROJ_TN), lambda b_, i, j: (0, j)),
                  pl.BlockSpec((3, PROJ_TN), qk),
                  pl.BlockSpec((1, PROJ_TN), qk),
                  pl.BlockSpec((1, PROJ_TN), qk)],
        out_specs=[pl.BlockSpec((None, tm, PROJ_TN), lambda b_, i, j: (b_, i, j)), row,
                   pl.BlockSpec(memory_space=pl.ANY)],
        out_shape=[jax.ShapeDtypeStruct((bsz, L, n), BF16), jax.ShapeDtypeStruct((bsz, L, d), BF16),
                   jax.ShapeDtypeStruct((bsz, L // tm, n2, jt, d // 2), jnp.uint32)],
        scratch_shapes=[pltpu.VMEM((tm, d // 2), jnp.uint32), pltpu.SemaphoreType.DMA(())],
        compiler_params=_cparams("parallel", "parallel", "arbitrary"),
        name="proj_main",
    )(x, g.reshape(1, d), shift, scale, w, b.reshape(1, n), wc, bc.reshape(1, -1), colscale.reshape(1, -1))


def _conv3_interleaved(z, wc, bc, seg, jt):
    grp = seg * jt
    pad = jnp.zeros((jt, z.shape[1]), z.dtype)
    prev, nxt = [], []
    for g0 in range(0, z.shape[0], grp):
        zg = z[g0:g0 + grp]
        prev += [pad, zg[:grp - jt]]
        nxt += [zg[jt:], pad]
    zp = jnp.concatenate(prev, axis=0)
    zn = jnp.concatenate(nxt, axis=0)
    return zp * wc[0:1, :] + z * wc[1:2, :] + zn * wc[2:3, :] + bc


def _proj_hyena_kernel(h_ref, w0_ref, w1_ref, w2_ref, b_ref, wc_ref, bc_ref, x0_ref, s_ref, *, seg):
    n2, jt = s_ref.shape[0], s_ref.shape[1]
    h = _unpack_bf16_pairs(h_ref[...].reshape(n2 * jt, h_ref.shape[2]))
    us = []
    for g, w_ref in enumerate((w0_ref, w1_ref, w2_ref)):
        z = jnp.dot(h, w_ref[...], preferred_element_type=F32) + b_ref[g]
        us.append(_conv3_interleaved(z, wc_ref[g], bc_ref[g], seg, jt))
    x0_ref[...] = _pack_bf16_pairs(us[0]).reshape(x0_ref.shape)
    s_ref[...] = (us[1] * us[2]).reshape(s_ref.shape)


def _proj_hyena(hi, w, b, wc, bc, seg):
    bsz, nt, n2, jt, dp = hi.shape
    d, tm = 2 * dp, n2 * jt
    L = nt * tm
    tn = DFT_C_TILE
    nblk = H_WIDTH // tn
    assert n2 % seg == 0 and (jt % 8 == 0 or nt == 1)
    b3 = b.reshape(3, 1, H_WIDTH)
    wc3 = wc.reshape(3, 3, H_WIDTH).transpose(1, 0, 2)
    bc3 = bc.reshape(3, 1, H_WIDTH)
    return pl.pallas_call(
        functools.partial(_proj_hyena_kernel, seg=seg),
        grid=(bsz, nt, nblk),
        in_specs=[pl.BlockSpec((None, None, n2, jt, dp), lambda b_, i, j: (b_, i, 0, 0, 0)),
                  pl.BlockSpec((d, tn), lambda b_, i, j: (0, j)),
                  pl.BlockSpec((d, tn), lambda b_, i, j: (0, nblk + j)),
                  pl.BlockSpec((d, tn), lambda b_, i, j: (0, 2 * nblk + j)),
                  pl.BlockSpec((3, 1, tn), lambda b_, i, j: (0, 0, j)),
                  pl.BlockSpec((3, 3, tn), lambda b_, i, j: (0, 0, j)),
                  pl.BlockSpec((3, 1, tn), lambda b_, i, j: (0, 0, j))],
        out_specs=[pl.BlockSpec((None, n2, jt, tn // 2), lambda b_, i, j: (b_, 0, i, j)),
                   pl.BlockSpec((None, n2, jt, tn), lambda b_, i, j: (b_, 0, i, j))],
        out_shape=[jax.ShapeDtypeStruct((bsz, n2, L // n2, H_WIDTH // 2), jnp.uint32),
                   jax.ShapeDtypeStruct((bsz, n2, L // n2, H_WIDTH), F32)],
        compiler_params=_cparams("parallel", "parallel", "arbitrary"),
        name="proj_hyena",
    )(hi, w, w, w, b3, wc3, bc3)


N_GATES = 4 * M_HEADS


def _split3(x):
    hi = x.astype(BF16)
    r1 = x - hi.astype(F32)
    mid = r1.astype(BF16)
    lo = (r1 - mid.astype(F32)).astype(BF16)
    return hi, mid, lo


def _log_sigmoid(x):
    return jnp.minimum(x, 0.0) - jnp.log1p(jnp.exp(-jnp.abs(x)))


def _gates_kernel(h_ref, w_ref, wt_ref, b_ref, bt_ref, bc_ref, ac_ref, ar_ref):
    h = h_ref[...]
    t = h.shape[0]
    z = jnp.dot(h, w_ref[...], preferred_element_type=F32) + b_ref[...]
    zt = lax.dot_general(wt_ref[...], h, (((1,), (1,)), ((), ())),
                         preferred_element_type=F32) + bt_ref[...]
    r = lax.broadcasted_iota(jnp.int32, (t, t), 0)
    c = lax.broadcasted_iota(jnp.int32, (t, t), 1)
    lower = (r >= c).astype(BF16)
    upper = (r <= c).astype(BF16)
    g8 = FG_LANE0

    lf = _log_sigmoid(z)
    lane = lax.broadcasted_iota(jnp.int32, z.shape, 1)
    is_fg = (lane >= g8) & (lane < 2 * g8)
    terms = [jnp.where(is_fg, p.astype(F32), 0.0) for p in _split3(lf)]
    packed = terms[0] + pltpu.roll(terms[1], 2 * g8, 1) + pltpu.roll(terms[2], 4 * g8, 1)
    cfp = jnp.dot(lower, packed.astype(BF16), preferred_element_type=F32)
    cf = cfp + pltpu.roll(cfp, LANES - 2 * g8, 1) + pltpu.roll(cfp, LANES - 4 * g8, 1)
    cb = cf[t - 1:t, :] - cf + lf
    bc = jnp.where(lane < g8 + M_HEADS, cf, cb)
    bc = pltpu.roll(bc, LANES - g8, 1)
    bc_ref[...] = bc
    ac_ref[...] = z - bc

    lft = _log_sigmoid(zt[g8:, :])
    stacked = jnp.concatenate([p.astype(F32) for p in _split3(lft)] + [jnp.zeros_like(lft)], axis=0)
    cft3 = jnp.dot(stacked.astype(BF16), upper, preferred_element_type=F32)
    cft = cft3[0:g8] + cft3[g8:2 * g8] + cft3[2 * g8:3 * g8]
    cbt = cft[:, t - 1:t] - cft + lft
    row = lax.broadcasted_iota(jnp.int32, cft.shape, 0)
    ar_ref[...] = zt[:g8, :] - jnp.where(row < M_HEADS, cft, cbt)


FG_LANE0 = 2 * M_HEADS


def _gates(h, w_g, b_g, chunk):
    bsz, L, d = h.shape
    w_pad = jnp.zeros((d, LANES), F32).at[:, :N_GATES].set(w_g).astype(BF16)
    b_pad = jnp.zeros((1, LANES), F32).at[0, :N_GATES].set(b_g)
    wt = w_g.T.astype(BF16)
    bt = b_g.reshape(N_GATES, 1)
    tok = pl.BlockSpec((None, chunk, LANES), lambda b_, i: (b_, i, 0))
    return pl.pallas_call(
        _gates_kernel,
        grid=(bsz, L // chunk),
        in_specs=[pl.BlockSpec((None, chunk, d), lambda b_, i: (b_, i, 0)),
                  pl.BlockSpec((d, LANES), lambda b_, i: (0, 0)),
                  pl.BlockSpec((N_GATES, d), lambda b_, i: (0, 0)),
                  pl.BlockSpec((1, LANES), lambda b_, i: (0, 0)),
                  pl.BlockSpec((N_GATES, 1), lambda b_, i: (0, 0))],
        out_specs=[tok, tok, pl.BlockSpec((None, FG_LANE0, chunk), lambda b_, i: (b_, 0, i))],
        out_shape=[jax.ShapeDtypeStruct((bsz, L, LANES), F32),
                   jax.ShapeDtypeStruct((bsz, L, LANES), F32),
                   jax.ShapeDtypeStruct((bsz, FG_LANE0, L), F32)],
        compiler_params=_cparams("parallel", "parallel"),
        name="mlstm_gates",
    )(h, w_pad, wt, b_pad, bt)


def _mlstm_kernel(*refs, emit_h, n_chunks):
    if emit_h:
        (q_ref, k_ref, v_ref, bc_ref, ac_ref, ar_ref, c0_ref, n0_ref, m0_ref,
         h_ref, cf_ref, nf_ref, mf_ref, c_sc, n_sc, m_sc) = refs
    else:
        (k_ref, v_ref, bc_ref, ac_ref, ar_ref, c0_ref, n0_ref, m0_ref,
         cf_ref, nf_ref, mf_ref, c_sc, n_sc, m_sc) = refs
    d = pl.program_id(1)
    j = pl.program_id(2)
    fwd = d == 0
    t = k_ref.shape[0]
    dh = M_HEAD_DIM

    @pl.when(j == 0)
    def _():
        c_sc[...] = c0_ref[...]
        n_sc[...] = n0_ref[...]
        m_sc[...] = m0_ref[...]

    r = lax.broadcasted_iota(jnp.int32, (t, t), 0)
    c = lax.broadcasted_iota(jnp.int32, (t, t), 1)
    causal = jnp.where(fwd, r - c, c - r) >= 0
    bc_all = bc_ref[...]
    ac_all = ac_ref[...]
    ar_all = ar_ref[...]
    for hd in range(M_HEADS):
        sl = slice(hd * dh, (hd + 1) * dh)
        bc = jnp.where(fwd, bc_all[:, hd:hd + 1], bc_all[:, M_HEADS + hd:M_HEADS + hd + 1])
        ac = jnp.where(fwd, ac_all[:, hd:hd + 1], ac_all[:, M_HEADS + hd:M_HEADS + hd + 1])
        ar = jnp.where(fwd, ar_all[hd:hd + 1, :], ar_all[M_HEADS + hd:M_HEADS + hd + 1, :])
        b_tot = jnp.where(fwd, bc[t - 1:t, :], bc[0:1, :])
        m_prev = m_sc[hd][:, 0:1]
        k_h = k_ref[:, sl]
        v_h = v_ref[:, sl]
        if emit_h:
            q_h = q_ref[:, sl]
            dm = jnp.where(causal, bc + ar, NEG_BIG)
            inter = bc + m_prev
            m_t = jnp.maximum(inter, jnp.max(dm, axis=1, keepdims=True))
            qk = lax.dot_general(q_h, k_h, (((1,), (1,)), ((), ())), preferred_element_type=F32)
            s = qk * jnp.exp(dm - m_t)
            carry = jnp.exp(inter - m_t)
            num = (jnp.dot(s.astype(BF16), v_h, preferred_element_type=F32)
                   + carry * jnp.dot(q_h, c_sc[hd].astype(BF16), preferred_element_type=F32))
            den = (jnp.sum(s, axis=1, keepdims=True)
                   + carry * jnp.sum(q_h.astype(F32) * n_sc[hd], axis=1, keepdims=True))
            h_ref[:, sl] = (num / jnp.maximum(jnp.abs(den), jnp.exp(-m_t))).astype(h_ref.dtype)
        g = b_tot + ac
        m_new = jnp.maximum(b_tot + m_prev, jnp.max(g, axis=0, keepdims=True))
        wgt = jnp.exp(g - m_new)
        decay = jnp.exp(b_tot + m_prev - m_new)
        kw = k_h.astype(F32) * wgt
        c_sc[hd] = decay * c_sc[hd] + lax.dot_general(kw.astype(BF16), v_h, (((0,), (0,)), ((), ())),
                                                      preferred_element_type=F32)
        n_sc[hd] = decay * n_sc[hd] + jnp.sum(kw, axis=0, keepdims=True)
        m_sc[hd] = jnp.broadcast_to(m_new, (1, LANES))

    @pl.when(j == n_chunks - 1)
    def _():
        cf_ref[...] = c_sc[...]
        nf_ref[...] = n_sc[...]
        mf_ref[...] = m_sc[...]


def _mlstm(q, k, v, bc, ac, ar, state, emit_h, t):
    bsz, L, _ = k[0].shape
    nc = L // t
    seq = lambda b_, d, j: (b_, j + d * (nc - 1 - 2 * j), 0)
    st = lambda b_, d, j: (b_, d, 0, 0, 0)

    def tok(col):
        return pl.BlockSpec((None, t, M_WIDTH), lambda b_, d, j: (b_, j + d * (nc - 1 - 2 * j), col))

    gate_spec = pl.BlockSpec((None, t, LANES), seq)
    ar_spec = pl.BlockSpec((None, FG_LANE0, t), lambda b_, d, j: (b_, 0, j + d * (nc - 1 - 2 * j)))
    c_spec = pl.BlockSpec((None, None, M_HEADS, M_HEAD_DIM, M_HEAD_DIM), st)
    n_spec = pl.BlockSpec((None, None, M_HEADS, 1, M_HEAD_DIM), st)
    m_spec = pl.BlockSpec((None, None, M_HEADS, 1, LANES), st)
    state_shapes = [jax.ShapeDtypeStruct((bsz, 2, M_HEADS, M_HEAD_DIM, M_HEAD_DIM), F32),
                    jax.ShapeDtypeStruct((bsz, 2, M_HEADS, 1, M_HEAD_DIM), F32),
                    jax.ShapeDtypeStruct((bsz, 2, M_HEADS, 1, LANES), F32)]
    in_specs = [tok(k[1]), tok(v[1]), gate_spec, gate_spec, ar_spec, c_spec, n_spec, m_spec]
    args = [k[0], v[0], bc, ac, ar, *state]
    out_specs = [c_spec, n_spec, m_spec]
    out_shape = list(state_shapes)
    if emit_h:
        in_specs = [tok(q[1])] + in_specs
        args = [q[0]] + args
        out_specs = [pl.BlockSpec((None, None, t, M_WIDTH),
                                  lambda b_, d, j: (d, b_, j + d * (nc - 1 - 2 * j), 0))] + out_specs
        out_shape = [jax.ShapeDtypeStruct((2, bsz, L, M_WIDTH), BF16)] + out_shape
    outs = pl.pallas_call(
        functools.partial(_mlstm_kernel, emit_h=emit_h, n_chunks=nc),
        grid=(bsz, 2, nc),
        in_specs=in_specs,
        out_specs=out_specs,
        out_shape=out_shape,
        scratch_shapes=[pltpu.VMEM((M_HEADS, M_HEAD_DIM, M_HEAD_DIM), F32),
                        pltpu.VMEM((M_HEADS, 1, M_HEAD_DIM), F32),
                        pltpu.VMEM((M_HEADS, 1, LANES), F32)],
        compiler_params=_cparams("parallel", "parallel", "arbitrary"),
        name="mlstm" if emit_h else "mlstm_state",
    )(*args)
    if emit_h:
        return outs[0], tuple(outs[1:])
    return None, tuple(outs)


DFT_M_TILE = 8
DFT_C_TILE = 1024
DFT_INNER_C_TILE = 512
FEAT_ROWS = 16


def _filter_outer_kernel(bands_ref, w1t_ref, b1_ref, w2t_ref, b2_ref, w3p_ref, w3f_ref, fr_ref, dl_ref, l_ref,
                         a_ref, ss_ref, *, L, n1, n2):
    i = pl.program_id(0)
    h = n1 // 2
    cols = DFT_M_TILE * h

    def positions(shape, axis, side):
        q = lax.broadcasted_iota(jnp.int32, shape, axis)
        mm, jj = q // h, q % h
        n = n2 * (jj + side * h) + i * DFT_M_TILE + mm
        return n, jnp.where(n < L, n, 2 * L - n).astype(F32)

    taps = []
    sumsq = jnp.zeros((1, a_ref.shape[-1]), F32)
    for side, w3_ref in ((0, w3p_ref), (1, w3f_ref)):
        _, p_row = positions((1, cols), 1, side)
        t_row = p_row / float(max(L - 1, 1))
        ang = ((2 * math.pi / L) * p_row) * bands_ref[...]
        row = lax.broadcasted_iota(jnp.int32, (FEAT_ROWS, cols), 0)
        feats = jnp.concatenate([jnp.where(row == 0, t_row, 0.0), jnp.cos(ang), -jnp.sin(ang)], axis=0)
        fr = fr_ref[...]
        hid = jnp.sin(fr * (jnp.dot(w1t_ref[...], feats.astype(BF16), preferred_element_type=F32) + b1_ref[...]))
        hid = jnp.sin(fr * (jnp.dot(w2t_ref[...], hid.astype(BF16), preferred_element_type=F32) + b2_ref[...]))
        filt = lax.dot_general(hid.astype(BF16), w3_ref[...], (((0,), (0,)), ((), ())),
                               preferred_element_type=F32)
        n_col, p_col = positions((cols, 1), 0, side)
        t_col = p_col / float(max(L - 1, 1))
        kern = filt * jnp.exp(-t_col * jnp.abs(dl_ref[...]))
        kern = jnp.where(n_col == L, 0.0, kern)
        sumsq = sumsq + jnp.sum(kern * kern, axis=0, keepdims=True)
        taps.append(kern)

    for mm in range(DFT_M_TILE):
        x = jnp.concatenate([taps[0][mm * h:(mm + 1) * h], taps[1][mm * h:(mm + 1) * h]], axis=0)
        out = jnp.dot(l_ref[...], x.astype(BF16), preferred_element_type=F32)
        a_ref[0, :, mm, :] = out[:n1]
        a_ref[1, :, mm, :] = out[n1:]

    @pl.when(i == 0)
    def _():
        ss_ref[...] = jnp.zeros_like(ss_ref)

    ss_ref[...] += sumsq


def _filter_outer(L, n1, n2, fwd_r, w1, b1, w2, b2, w3, freq):
    hid = H_FILTER_HIDDEN
    bands = jnp.linspace(1e-4, H_POS_BANDS - 1, H_POS_BANDS, dtype=F32).reshape(H_POS_BANDS, 1)
    w1t = jnp.zeros((hid, 3 * FEAT_ROWS), F32)
    w1t = w1t.at[:, 0].set(w1[0]).at[:, FEAT_ROWS:2 * FEAT_ROWS].set(w1[1:1 + H_POS_BANDS].T)
    w1t = w1t.at[:, 2 * FEAT_ROWS:].set(w1[1 + H_POS_BANDS:].T).astype(BF16)
    w3h = w3.astype(BF16)
    max_decay = math.log(H_DECAY_TARGET) / H_FAST_DECAY_PCT
    min_decay = math.log(H_DECAY_TARGET) / H_SLOW_DECAY_PCT
    deltas = jnp.linspace(min_decay, max_decay, H_WIDTH, dtype=F32).reshape(1, H_WIDTH)
    col = lambda v: v.reshape(hid, 1)
    full = lambda a: pl.BlockSpec(a.shape, lambda i: (0,) * a.ndim)
    args = [bands, w1t, col(b1), w2.T.astype(BF16), col(b2)]
    return pl.pallas_call(
        functools.partial(_filter_outer_kernel, L=L, n1=n1, n2=n2),
        grid=(n2 // DFT_M_TILE,),
        in_specs=[full(a) for a in args]
        + [pl.BlockSpec((hid, H_WIDTH), lambda i: (0, 0)), pl.BlockSpec((hid, H_WIDTH), lambda i: (0, 1)),
           full(col(freq)), full(deltas), full(fwd_r)],
        out_specs=[pl.BlockSpec((2, n1, DFT_M_TILE, H_WIDTH), lambda i: (0, 0, i, 0)),
                   pl.BlockSpec((1, H_WIDTH), lambda i: (0, 0))],
        out_shape=[jax.ShapeDtypeStruct((2, n1, n2, H_WIDTH), F32),
                   jax.ShapeDtypeStruct((1, H_WIDTH), F32)],
        compiler_params=_cparams("arbitrary"),
        name="hyena_filter_outer",
    )(*args, w3h, w3h, col(freq), deltas, fwd_r)


def _dft_factors(n):
    lg = int(round(math.log2(n)))
    n1 = 1 << ((lg + 1) // 2)
    return n1, n // n1


def _dft_outer_matrices(n1):
    k = np.arange(n1)[:, None]
    n = np.arange(n1)[None, :]
    ang = 2.0 * np.pi * ((k * n) % n1) / n1
    cr, ci = np.cos(ang), -np.sin(ang)
    h = n1 // 2
    fwd_c = np.block([[cr[:, :h], -ci[:, :h]], [ci[:, :h], cr[:, :h]]])
    fwd_r = np.concatenate([cr, ci], axis=0)
    ir, ii = cr[:h, :], -ci[:h, :]
    inv = np.block([[ir, -ii], [ii, ir]])
    return (jnp.asarray(fwd_c, F32).astype(BF16), jnp.asarray(fwd_r, F32).astype(BF16),
            jnp.asarray(inv, F32).astype(BF16))


def _dft_inner_matrices(n1, n2):
    n = n1 * n2
    k2 = np.arange(n2)[:, None]
    m = np.arange(n2)[None, :]
    ang = 2.0 * np.pi * ((k2 * m) % n2) / n2
    fr, fi = np.cos(ang), -np.sin(ang)
    f = np.block([[fr, -fi], [fi, fr]])
    k1 = jnp.arange(n1, dtype=jnp.int32)[:, None]
    tw_ang = ((jnp.arange(n2, dtype=jnp.int32)[None, :] * k1) % n).astype(F32) * (2.0 * math.pi / n)
    rep = lambda t: jnp.broadcast_to(t[:, :, None], (n1, n2, LANES))
    return (jnp.asarray(f, F32).astype(BF16), jnp.asarray(f.T, F32).astype(BF16),
            rep(jnp.cos(tw_ang)), rep(-jnp.sin(tw_ang)))


def _outer_fwd_kernel(l_ref, s_ref, a_ref):
    n1 = a_ref.shape[1]
    for mm in range(s_ref.shape[1]):
        x = jnp.concatenate([s_ref[0, mm], s_ref[1, mm]], axis=0).astype(BF16)
        out = jnp.dot(l_ref[...], x, preferred_element_type=F32)
        a_ref[0, :, mm, :] = out[:n1]
        a_ref[1, :, mm, :] = out[n1:]


def _outer_fwd(lmat, s_t):
    _, n2, n1h, c = s_t.shape
    n1 = 2 * n1h
    tc = min(DFT_C_TILE, c)
    return pl.pallas_call(
        _outer_fwd_kernel,
        grid=(n2 // DFT_M_TILE, c // tc),
        in_specs=[pl.BlockSpec(lmat.shape, lambda m, j: (0, 0)),
                  pl.BlockSpec((2, DFT_M_TILE, n1h, tc), lambda m, j: (0, m, 0, j))],
        out_specs=pl.BlockSpec((2, n1, DFT_M_TILE, tc), lambda m, j: (0, 0, m, j)),
        out_shape=jax.ShapeDtypeStruct((2, n1, n2, c), F32),
        compiler_params=_cparams("parallel", "parallel"),
        name="dft_outer_fwd",
    )(lmat, s_t)


def _outer_inv_kernel(l_ref, b_ref, s_ref, x0_ref, ysc_ref, hb_ref, o_ref):
    n1h = s_ref.shape[2]
    for mm in range(b_ref.shape[1]):
        y = jnp.concatenate([b_ref[0, mm], b_ref[1, mm]], axis=0).astype(BF16)
        out = jnp.dot(l_ref[...], y, preferred_element_type=F32)
        for b in range(2):
            conv = out[b * n1h:(b + 1) * n1h]
            x0 = _unpack_bf16_pairs(x0_ref[b, mm]).astype(F32)
            hy = x0 * (conv * ysc_ref[...] + hb_ref[...] * s_ref[b, mm])
            o_ref[b, :, mm, :] = _pack_bf16_pairs(hy)


def _outer_inv(lmat, b_t, s_t, x0_t, yscale, h_bias):
    _, n2, n1, c = b_t.shape
    n1h = n1 // 2
    tc = min(DFT_C_TILE, c)
    vec = pl.BlockSpec((1, tc), lambda m, j: (0, j))
    hy = pl.pallas_call(
        _outer_inv_kernel,
        grid=(n2 // DFT_M_TILE, c // tc),
        in_specs=[pl.BlockSpec(lmat.shape, lambda m, j: (0, 0)),
                  pl.BlockSpec((2, DFT_M_TILE, n1, tc), lambda m, j: (0, m, 0, j)),
                  pl.BlockSpec((2, DFT_M_TILE, n1h, tc), lambda m, j: (0, m, 0, j)),
                  pl.BlockSpec((2, DFT_M_TILE, n1h, tc // 2), lambda m, j: (0, m, 0, j)),
                  vec, vec],
        out_specs=pl.BlockSpec((2, n1h, DFT_M_TILE, tc // 2), lambda m, j: (0, 0, m, j)),
        out_shape=jax.ShapeDtypeStruct((2, n1h, n2, c // 2), jnp.uint32),
        compiler_params=_cparams("parallel", "parallel"),
        name="dft_outer_inv",
    )(lmat, b_t, s_t, x0_t, yscale, h_bias.reshape(1, c))
    return hy.reshape(2, n1h * n2, c // 2)


DFT_K_TILE = 8


def _twiddled_inner_dft(f_ref, twr_ref, twi_ref, a_ref, kk):
    n2, c = a_ref.shape[2], a_ref.shape[3]
    twr = jnp.tile(twr_ref[kk], (1, c // LANES))
    twi = jnp.tile(twi_ref[kk], (1, c // LANES))
    ar, ai = a_ref[0, kk], a_ref[1, kk]
    a = jnp.concatenate([(ar * twr - ai * twi).astype(BF16), (ar * twi + ai * twr).astype(BF16)], axis=0)
    x = jnp.dot(f_ref[...], a, preferred_element_type=F32)
    return x[:n2], x[n2:], twr, twi


def _inner_fwd_kernel(f_ref, twr_ref, twi_ref, a_ref, o_ref):
    for kk in range(a_ref.shape[1]):
        xr, xi, _, _ = _twiddled_inner_dft(f_ref, twr_ref, twi_ref, a_ref, kk)
        o_ref[0, kk] = xr.astype(o_ref.dtype)
        o_ref[1, kk] = xi.astype(o_ref.dtype)


def _inner_specs(n1, n2, c):
    tc = min(DFT_INNER_C_TILE, c)
    kt = min(DFT_K_TILE, n1)
    blk = pl.BlockSpec((2, kt, n2, tc), lambda k, j: (0, k, 0, j))
    mat = pl.BlockSpec((2 * n2, 2 * n2), lambda k, j: (0, 0))
    tw = pl.BlockSpec((kt, n2, LANES), lambda k, j: (k, 0, 0))
    return blk, mat, tw, (n1 // kt, c // tc), kt, tc


def _inner_fwd(f, twr, twi, a):
    _, n1, n2, c = a.shape
    blk, mat, tw, grid, _, _ = _inner_specs(n1, n2, c)
    return pl.pallas_call(
        _inner_fwd_kernel,
        grid=grid,
        in_specs=[mat, tw, tw, blk],
        out_specs=blk,
        out_shape=jax.ShapeDtypeStruct((2, n1, n2, c), BF16),
        compiler_params=_cparams("parallel", "parallel"),
        name="dft_inner_filter",
    )(f, twr, twi, a)


def _inner_conv_kernel(f_ref, ft_ref, twr_ref, twi_ref, a_ref, k_ref, o_ref):
    n2 = a_ref.shape[2]
    for kk in range(a_ref.shape[1]):
        xr, xi, twr, twi = _twiddled_inner_dft(f_ref, twr_ref, twi_ref, a_ref, kk)
        kr, ki = k_ref[0, kk].astype(F32), k_ref[1, kk].astype(F32)
        yr = xr * kr - xi * ki
        yi = xr * ki + xi * kr
        y = jnp.concatenate([yr.astype(BF16), yi.astype(BF16)], axis=0)
        b = jnp.dot(ft_ref[...], y, preferred_element_type=F32)
        br, bi = b[:n2], b[n2:]
        o_ref[0, :, kk, :] = br * twr + bi * twi
        o_ref[1, :, kk, :] = bi * twr - br * twi


def _inner_conv(f, ft, twr, twi, a, kf):
    _, n1, n2, c = a.shape
    blk, mat, tw, grid, kt, tc = _inner_specs(n1, n2, c)
    return pl.pallas_call(
        _inner_conv_kernel,
        grid=grid,
        in_specs=[mat, mat, tw, tw, blk, blk],
        out_specs=pl.BlockSpec((2, n2, kt, tc), lambda k, j: (0, 0, k, j)),
        out_shape=jax.ShapeDtypeStruct((2, n2, n1, c), F32),
        compiler_params=_cparams("parallel", "parallel"),
        name="dft_inner_conv",
    )(f, ft, twr, twi, a, kf)


def _hyena_long_conv(s_t, x0_t, h_bias, w1, b1, w2, b2, w3, freq):
    bsz, n2, n1h, c = s_t.shape
    assert bsz == 2
    n1 = 2 * n1h
    L = n1h * n2
    fwd_c, fwd_r, inv = _dft_outer_matrices(n1)
    f, ft, twr, twi = _dft_inner_matrices(n1, n2)
    af, sumsq = _filter_outer(L, n1, n2, fwd_r, w1, b1, w2, b2, w3, freq)
    kf = _inner_fwd(f, twr, twi, af)
    a = _outer_fwd(fwd_c, s_t)
    b_t = _inner_conv(f, ft, twr, twi, a, kf)
    yscale = lax.rsqrt(sumsq + EPS) * (1.0 / (2 * L))
    return _outer_inv(inv, b_t, s_t, x0_t, yscale, h_bias)


def _pack_bf16_pairs(x):
    half = x.shape[1] // 2
    lo = pltpu.bitcast(x[:, :half].astype(BF16).astype(F32), jnp.uint32) >> 16
    hi = pltpu.bitcast(x[:, half:].astype(BF16).astype(F32), jnp.uint32) & jnp.uint32(0xFFFF0000)
    return lo | hi


def _unpack_bf16_pairs(p):
    lo = pltpu.bitcast(p << 16, F32).astype(BF16)
    hi = pltpu.bitcast(p & jnp.uint32(0xFFFF0000), F32).astype(BF16)
    return jnp.concatenate([lo, hi], axis=1)


def _merge_kernel(hf_ref, hb_ref, o_ref, hy_ref, ga_ref, gb_ref, x_ref,
                  gate_ref, g2_ref, sh_ref, sc_ref, wa_ref, wb_ref, wo_ref, x1_ref, h2_ref):
    a = o_ref[...].astype(F32) * (hf_ref[...].astype(F32) + hb_ref[...].astype(F32))
    half = DFT_C_TILE // 2
    hy = jnp.concatenate([_unpack_bf16_pairs(hy_ref[:, c * half:(c + 1) * half])
                          for c in range(hy_ref.shape[1] // half)], axis=1)
    pa = jnp.dot(a.astype(BF16), wa_ref[...], preferred_element_type=F32)
    pb = jnp.dot(hy, wb_ref[...], preferred_element_type=F32)
    mix = ga_ref[...].astype(F32) * pa + gb_ref[...].astype(F32) * pb
    out = jnp.dot(mix.astype(BF16), wo_ref[...], preferred_element_type=F32)
    x1 = x_ref[...] + gate_ref[...] * out
    x1_ref[...] = x1
    y = x1 * lax.rsqrt(jnp.mean(x1 * x1, axis=-1, keepdims=True) + EPS) * g2_ref[...]
    h2_ref[...] = _pack_bf16_pairs(y * (1.0 + sc_ref[...]) + sh_ref[...])


def _merge(hdirs, pm, hy, x, gate1, g2, shift2, scale2, w_a, w_b, w_out, tm=MERGE_TM):
    bsz, L, d = x.shape
    tok = pl.BlockSpec((None, tm, d), lambda b, i: (b, i, 0))

    def pm_tile(col):
        return pl.BlockSpec((None, tm, d), lambda b, i: (b, i, col))

    packed = pl.BlockSpec((None, tm, d // 2), lambda b, i: (b, i, 0))
    vec = pl.BlockSpec((1, d), lambda b, i: (0, 0))
    bvec = pl.BlockSpec((None, 1, d), lambda b, i: (b, 0, 0))
    wsp = pl.BlockSpec((d, d), lambda b, i: (0, 0), pipeline_mode=pl.Buffered(1))
    return pl.pallas_call(
        _merge_kernel,
        grid=(bsz, L // tm),
        in_specs=[pl.BlockSpec((None, None, tm, d), lambda b, i: (0, b, i, 0)),
                  pl.BlockSpec((None, None, tm, d), lambda b, i: (1, b, i, 0)),
                  pm_tile(PM_O), packed, pm_tile(PM_GA), pm_tile(PM_GB), tok,
                  bvec, vec, bvec, bvec, wsp, wsp, wsp],
        out_specs=[tok, packed],
        out_shape=[jax.ShapeDtypeStruct((bsz, L, d), F32), jax.ShapeDtypeStruct((bsz, L, d // 2), jnp.uint32)],
        compiler_params=_cparams("parallel", "parallel"),
        name="merge",
    )(hdirs, hdirs, pm, hy, pm, pm, x, gate1, g2.reshape(1, d), shift2, scale2, w_a, w_b, w_out)


MOE_BLOCK = 256
ROUTE_E1, ROUTE_E2, ROUTE_W1, ROUTE_W2 = 0, 1, 2, 3
EXP_LANE0 = N_GROUPS


def _first_lane_of_max(val, valid, lane):
    masked = jnp.where(valid, val, NEG_BIG)
    mx = jnp.max(masked, axis=1, keepdims=True)
    idx = jnp.min(jnp.where(valid & (masked == mx), lane, LANES), axis=1, keepdims=True)
    return mx, idx


MOE_TM = 1024


def _expert_onehots(rec):
    lane = lax.broadcasted_iota(jnp.int32, rec.shape, 1)
    oh1 = lane == rec[:, ROUTE_E1:ROUTE_E1 + 1].astype(jnp.int32)
    oh2 = lane == rec[:, ROUTE_E2:ROUTE_E2 + 1].astype(jnp.int32)
    return oh1, oh2


def _router_kernel(h_ref, w_ref, b_ref, r_ref, cnt_ref):
    logits = jnp.dot(_unpack_bf16_pairs(h_ref[...]), w_ref[...], preferred_element_type=F32) + b_ref[...]
    lane = lax.broadcasted_iota(jnp.int32, logits.shape, 1)
    is_g = lane < N_GROUPS
    gmax, gsel = _first_lane_of_max(logits, is_g, lane)
    gsum = jnp.sum(jnp.where(is_g, jnp.exp(logits - gmax), 0.0), axis=1, keepdims=True)
    gw = 1.0 / gsum
    lo = EXP_LANE0 + gsel * EXPERTS_PER_GROUP
    in_grp = (lane >= lo) & (lane < lo + EXPERTS_PER_GROUP)
    emax, l1 = _first_lane_of_max(logits, in_grp, lane)
    esum = jnp.sum(jnp.where(in_grp, jnp.exp(logits - emax), 0.0), axis=1, keepdims=True)
    e2max, l2 = _first_lane_of_max(logits, in_grp & (lane != l1), lane)
    v1 = 1.0 / esum
    v2 = jnp.exp(e2max - emax) / esum
    vs = v1 + v2
    w1 = gw * v1 / vs
    w2 = gw * v2 / vs
    e1 = (l1 - EXP_LANE0).astype(F32)
    e2 = (l2 - EXP_LANE0).astype(F32)
    rec = jnp.where(lane == ROUTE_E1, e1,
                    jnp.where(lane == ROUTE_E2, e2,
                              jnp.where(lane == ROUTE_W1, w1,
                                        jnp.where(lane == ROUTE_W2, w2, 0.0))))
    r_ref[...] = rec
    oh1, oh2 = _expert_onehots(rec)
    counts = jnp.sum((oh1 | oh2).astype(F32), axis=0, keepdims=True)
    cnt_ref[...] = jnp.broadcast_to(counts, cnt_ref.shape)


def _router(h2, w_group, b_group, w_router, b_router):
    n, dp = h2.shape
    d = 2 * dp
    tm = MOE_TM
    w = jnp.zeros((d, LANES), F32).at[:, :N_GROUPS].set(w_group).at[
        :, EXP_LANE0:EXP_LANE0 + N_EXPERTS].set(w_router).astype(BF16)
    b = jnp.zeros((1, LANES), F32).at[0, :N_GROUPS].set(b_group).at[
        0, EXP_LANE0:EXP_LANE0 + N_EXPERTS].set(b_router)
    return pl.pallas_call(
        _router_kernel,
        grid=(n // tm,),
        in_specs=[pl.BlockSpec((tm, dp), lambda i: (i, 0)),
                  pl.BlockSpec((d, LANES), lambda i: (0, 0)),
                  pl.BlockSpec((1, LANES), lambda i: (0, 0))],
        out_specs=[pl.BlockSpec((tm, LANES), lambda i: (i, 0)),
                   pl.BlockSpec((None, 8, LANES), lambda i: (i, 0, 0))],
        out_shape=[jax.ShapeDtypeStruct((n, LANES), F32), jax.ShapeDtypeStruct((n // tm, 8, LANES), F32)],
        compiler_params=_cparams("parallel"),
        name="moe_router",
    )(h2, w, b)


def _slots_kernel(r_ref, base_ref, dest_ref):
    rec = r_ref[...]
    tm = rec.shape[0]
    lane = lax.broadcasted_iota(jnp.int32, rec.shape, 1)
    oh1, oh2 = _expert_onehots(rec)
    r = lax.broadcasted_iota(jnp.int32, (tm, tm), 0)
    c = lax.broadcasted_iota(jnp.int32, (tm, tm), 1)
    earlier = (r > c).astype(BF16)
    rank = jnp.dot(earlier, (oh1 | oh2).astype(BF16), preferred_element_type=F32) + base_ref[0:1, :]
    d1 = jnp.sum(jnp.where(oh1, rank, 0.0), axis=1, keepdims=True)
    d2 = jnp.sum(jnp.where(oh2, rank, 0.0), axis=1, keepdims=True)
    dest_ref[...] = jnp.where(lane == 0, d1, jnp.where(lane == 1, d2, 0.0)).astype(jnp.int32)


def _slots(route, tile_counts):
    n = route.shape[0]
    tm = MOE_TM
    cnt = tile_counts[:, 0, :]
    totals = jnp.sum(cnt, axis=0)
    nblk = jnp.ceil(totals * (1.0 / MOE_BLOCK))
    first_slot = (jnp.cumsum(nblk) - nblk) * float(MOE_BLOCK)
    base = first_slot[None, :] + jnp.cumsum(cnt, axis=0) - cnt
    base = jnp.broadcast_to(base[:, None, :], tile_counts.shape)
    dest = pl.pallas_call(
        _slots_kernel,
        grid=(n // tm,),
        in_specs=[pl.BlockSpec((tm, LANES), lambda i: (i, 0)),
                  pl.BlockSpec((None, 8, LANES), lambda i: (i, 0, 0))],
        out_specs=pl.BlockSpec((tm, LANES), lambda i: (i, 0)),
        out_shape=jax.ShapeDtypeStruct((n, LANES), jnp.int32),
        compiler_params=_cparams("parallel"),
        name="moe_slots",
    )(route, base)
    return dest, totals


EXPERT_STEP_BLOCKS = 4


def _experts_kernel(be_ref, first_ref, nxt_ref, par_ref, nu_ref, x_ref, w1_hbm, w3_hbm, w2_hbm, o_ref,
                    w1f, w3f, w2f, w1b, w3b, w2b, sems):
    step = pl.program_id(0)

    def weight_copies(e, slot):
        return (pltpu.make_async_copy(w1_hbm.at[e], w1f.at[slot], sems.at[0, slot]),
                pltpu.make_async_copy(w3_hbm.at[e], w3f.at[slot], sems.at[1, slot]),
                pltpu.make_async_copy(w2_hbm.at[e], w2f.at[slot], sems.at[2, slot]))

    @pl.when(step == 0)
    def _():
        for cp in weight_copies(be_ref[0], 0):
            cp.start()

    for sub in range(EXPERT_STEP_BLOCKS):
        i = step * EXPERT_STEP_BLOCKS + sub
        rows = pl.ds(sub * MOE_BLOCK, MOE_BLOCK)

        @pl.when(first_ref[i] == 1)
        def _():
            slot = par_ref[i]

            @pl.when(nxt_ref[i] >= 0)
            def _():
                for cp in weight_copies(nxt_ref[i], 1 - slot):
                    cp.start()

            for cp in weight_copies(be_ref[i], slot):
                cp.wait()
            w1b[...] = w1f[slot].astype(BF16)
            w3b[...] = w3f[slot].astype(BF16)
            w2b[...] = w2f[slot].astype(BF16)

        @pl.when(i < nu_ref[0])
        def _():
            x = _unpack_bf16_pairs(x_ref[rows, :])
            a = jnp.dot(x, w1b[...], preferred_element_type=F32)
            b = jnp.dot(x, w3b[...], preferred_element_type=F32)
            hmid = (a * jax.nn.sigmoid(a)) * b
            o_ref[rows, :] = _pack_bf16_pairs(jnp.dot(hmid.astype(BF16), w2b[...], preferred_element_type=F32))

        @pl.when(i >= nu_ref[0])
        def _():
            o_ref[rows, :] = jnp.zeros((MOE_BLOCK, o_ref.shape[1]), o_ref.dtype)


def _experts(xs, nb, block_e, n_used, w1_e, w3_e, w2_e):
    dp = xs.shape[1]
    d, de = w1_e.shape[1], w1_e.shape[2]
    idx = jnp.arange(nb, dtype=jnp.int32)
    used = idx < n_used[0]
    first = used & ((idx == 0) | (block_e != jnp.roll(block_e, 1)))
    ordinal = jnp.cumsum(first.astype(jnp.int32)) - 1
    par = (ordinal % 2).astype(jnp.int32)
    first_pos = jnp.where(first, idx, nb)
    next_first = lax.cummin(jnp.concatenate([first_pos[1:], jnp.full((1,), nb, jnp.int32)]), reverse=True)
    nxt = jnp.where(next_first < nb, block_e[jnp.minimum(next_first, nb - 1)], -1).astype(jnp.int32)
    any_spec = pl.BlockSpec(memory_space=pl.ANY)
    assert nb % EXPERT_STEP_BLOCKS == 0
    step_rows = EXPERT_STEP_BLOCKS * MOE_BLOCK
    grid_spec = pltpu.PrefetchScalarGridSpec(
        num_scalar_prefetch=5,
        grid=(nb // EXPERT_STEP_BLOCKS,),
        in_specs=[pl.BlockSpec((step_rows, dp), lambda i, *_: (i, 0)), any_spec, any_spec, any_spec],
        out_specs=pl.BlockSpec((step_rows, dp), lambda i, *_: (i, 0)),
        scratch_shapes=[pltpu.VMEM((2, d, de), F32), pltpu.VMEM((2, d, de), F32), pltpu.VMEM((2, de, d), F32),
                        pltpu.VMEM((d, de), BF16), pltpu.VMEM((d, de), BF16), pltpu.VMEM((de, d), BF16),
                        pltpu.SemaphoreType.DMA((3, 2))],
    )
    return pl.pallas_call(
        _experts_kernel,
        grid_spec=grid_spec,
        out_shape=jax.ShapeDtypeStruct((nb * MOE_BLOCK, dp), xs.dtype),
        compiler_params=_cparams("arbitrary"),
        name="moe_experts",
    )(block_e, first.astype(jnp.int32), nxt, par, n_used, xs, w1_e, w3_e, w2_e)


SC_WINDOW = 128
SC_CORES, SC_SUBCORES = 2, 16
SC_WORKERS = SC_CORES * SC_SUBCORES


def _sc_worker_id():
    return lax.axis_index("c") * SC_SUBCORES + lax.axis_index("s")


def _sc_mesh():
    return plsc.VectorSubcoreMesh(core_axis_name="c", subcore_axis_name="s")


def _sc_dispatch(rows, dest0, dest1, pad_slots, n_rows):
    n, dv = rows.shape
    nwin, pwin = n // SC_WINDOW, pad_slots.shape[0] // SC_WINDOW
    assert n % (SC_WINDOW * SC_WORKERS) == 0 and pad_slots.shape[0] % (SC_WINDOW * SC_WORKERS) == 0
    zeros = jnp.zeros((SC_WINDOW, dv), rows.dtype)

    @pl.kernel(out_type=jax.ShapeDtypeStruct((n_rows, dv), rows.dtype), mesh=_sc_mesh(),
               scratch_types=[pltpu.VMEM((1, SC_WINDOW), jnp.int32), pltpu.VMEM((SC_WINDOW, dv), rows.dtype)],
               name="moe_dispatch_sc")
    def scatter(x_hbm, d0_hbm, d1_hbm, p_hbm, z_hbm, o_hbm, idx, buf):
        wid = _sc_worker_id()
        pltpu.sync_copy(z_hbm, buf)

        @pl.loop(0, pwin // SC_WORKERS)
        def _(t):
            w = t * SC_WORKERS + wid
            pltpu.sync_copy(p_hbm.at[pl.ds(w, 1)], idx)
            pltpu.sync_copy(buf, o_hbm.at[idx.at[0]])

        @pl.loop(0, nwin // SC_WORKERS)
        def _(t):
            w = t * SC_WORKERS + wid
            pltpu.sync_copy(x_hbm.at[pl.ds(w * SC_WINDOW, SC_WINDOW)], buf)
            for d_hbm in (d0_hbm, d1_hbm):
                pltpu.sync_copy(d_hbm.at[pl.ds(w, 1)], idx)
                pltpu.sync_copy(buf, o_hbm.at[idx.at[0]])

    return scatter(rows, dest0.reshape(nwin, SC_WINDOW), dest1.reshape(nwin, SC_WINDOW),
                   pad_slots.reshape(pwin, SC_WINDOW), zeros)


def _sc_gather(table, index):
    m = index.shape[0]
    dv = table.shape[1]
    nwin = m // SC_WINDOW
    assert m % (SC_WINDOW * SC_WORKERS) == 0

    @pl.kernel(out_type=jax.ShapeDtypeStruct((m, dv), table.dtype), mesh=_sc_mesh(),
               scratch_types=[pltpu.VMEM((1, SC_WINDOW), jnp.int32), pltpu.VMEM((SC_WINDOW, dv), table.dtype)],
               name="moe_gather_sc")
    def gather(x_hbm, i_hbm, o_hbm, idx, buf):
        wid = _sc_worker_id()

        @pl.loop(0, nwin // SC_WORKERS)
        def _(t):
            w = t * SC_WORKERS + wid
            pltpu.sync_copy(i_hbm.at[pl.ds(w, 1)], idx)
            pltpu.sync_copy(x_hbm.at[idx.at[0]], buf)
            pltpu.sync_copy(buf, o_hbm.at[pl.ds(w * SC_WINDOW, SC_WINDOW)])

    return gather(table, index.reshape(nwin, SC_WINDOW))


def _combine_planes_kernel(r_ref, ya_ref, yb_ref, x_ref, gate_ref, gf_ref, o_ref):
    rec = r_ref[...]
    y = (_unpack_bf16_pairs(ya_ref[...]).astype(F32) * rec[:, ROUTE_W1:ROUTE_W1 + 1]
         + _unpack_bf16_pairs(yb_ref[...]).astype(F32) * rec[:, ROUTE_W2:ROUTE_W2 + 1])
    x2 = x_ref[...] + gate_ref[...] * y
    o_ref[...] = x2 * lax.rsqrt(jnp.mean(x2 * x2, axis=-1, keepdims=True) + EPS) * gf_ref[...]


def _combine_planes(g, route, x1, gate2, g_final, tm=COMBINE_TM):
    bsz, L, d = x1.shape
    tpb = L // tm
    dp = g.shape[-1]
    return pl.pallas_call(
        _combine_planes_kernel,
        grid=(bsz, tpb),
        in_specs=[pl.BlockSpec((tm, LANES), lambda b, i: (b * tpb + i, 0)),
                  pl.BlockSpec((None, tm, dp), lambda b, i: (0, b * tpb + i, 0)),
                  pl.BlockSpec((None, tm, dp), lambda b, i: (1, b * tpb + i, 0)),
                  pl.BlockSpec((None, tm, d), lambda b, i: (b, i, 0)),
                  pl.BlockSpec((None, 1, d), lambda b, i: (b, 0, 0)),
                  pl.BlockSpec((1, d), lambda b, i: (0, 0))],
        out_specs=pl.BlockSpec((None, tm, d), lambda b, i: (b, i, 0)),
        out_shape=jax.ShapeDtypeStruct((bsz, L, d), F32),
        compiler_params=_cparams("parallel", "parallel"),
        name="moe_combine",
    )(route, g, g, x1, gate2, g_final.reshape(1, d))


def _moe(h2, x1, gate2, g_final, w_group, b_group, w_router, b_router, w1_e, w3_e, w2_e):
    bsz, L, d = x1.shape
    n = bsz * L
    h2f = h2.reshape(n, h2.shape[-1])
    route, tile_counts = _router(h2f, w_group, b_group, w_router, b_router)
    dest_rec, counts = _slots(route, tile_counts)
    nb = (2 * n) // MOE_BLOCK + N_EXPERTS
    cnt = counts[:N_EXPERTS].astype(jnp.int32)
    blocks_per_e = (cnt + MOE_BLOCK - 1) // MOE_BLOCK
    ends = jnp.cumsum(blocks_per_e)
    block_e = jnp.minimum(jnp.sum(ends[None, :] <= jnp.arange(nb, dtype=jnp.int32)[:, None], axis=1),
                          N_EXPERTS - 1).astype(jnp.int32)
    n_used = ends[-1:].astype(jnp.int32)
    n_slots = nb * MOE_BLOCK
    pad_j = jnp.arange(MOE_BLOCK, dtype=jnp.int32)[None, :]
    spare = n_slots + jnp.arange(N_EXPERTS * MOE_BLOCK, dtype=jnp.int32).reshape(N_EXPERTS, MOE_BLOCK)
    first_slot = ((ends - blocks_per_e) * MOE_BLOCK)[:, None]
    is_pad = cnt[:, None] + pad_j < blocks_per_e[:, None] * MOE_BLOCK
    pad_slots = jnp.where(is_pad, first_slot + cnt[:, None] + pad_j, spare).reshape(-1)
    xs = _sc_dispatch(h2f, dest_rec[:, 0], dest_rec[:, 1], pad_slots, n_slots + N_EXPERTS * MOE_BLOCK)
    ys = _experts(xs, nb, block_e, n_used, w1_e, w3_e, w2_e)
    g = _sc_gather(ys, jnp.concatenate([dest_rec[:, 0], dest_rec[:, 1]]))
    return _combine_planes(g.reshape(2, n, g.shape[-1]), route, x1, gate2, g_final)


def kernel(x, c, ctx, c_ctx, w_mod, b_mod, g_norm1, g_norm2, w_in, b_in, w_qk_conv, b_qk_conv,
           w_h_conv, b_h_conv, hf_w1, hf_b1, hf_w2, hf_b2, hf_w3, hf_freq, h_bias, w_a, w_b, w_out,
           w_group, b_group, w_router, b_router, w1_e, w3_e, w2_e, g_final):
    assert w_mod.shape[0] == 1, "single-layer block"
    (w_mod, b_mod, g_norm1, g_norm2, w_in, b_in, w_qk_conv, b_qk_conv, w_h_conv, b_h_conv, hf_w1, hf_b1, hf_w2,
     hf_b2, hf_w3, hf_freq, h_bias, w_a, w_b, w_out, w_group, b_group, w_router, b_router, w1_e, w3_e, w2_e) = (
        t[0] for t in (w_mod, b_mod, g_norm1, g_norm2, w_in, b_in, w_qk_conv, b_qk_conv, w_h_conv, b_h_conv,
                       hf_w1, hf_b1, hf_w2, hf_b2, hf_w3, hf_freq, h_bias, w_a, w_b, w_out, w_group, b_group,
                       w_router, b_router, w1_e, w3_e, w2_e))
    bsz, L, d = x.shape
    lc = ctx.shape[1]
    seg = L // (L // GRID_W)
    chunk_c = min(lc, MLSTM_CHUNK)
    assert bsz + 1 <= 8 and lc % chunk_c == 0 and L % MLSTM_CHUNK == 0

    cond = jnp.zeros((8, d), F32).at[:bsz].set(c).at[bsz].set(c_ctx)
    mod = _adaln(cond, w_mod, b_mod).reshape(8, 6, d)
    modx = mod[:bsz]
    shift1, scale1, gate1, shift2, scale2, gate2 = (modx[:, i:i + 1] for i in range(6))
    shift1c = jnp.broadcast_to(mod[bsz, 0].reshape(1, 1, d), (bsz, 1, d))
    scale1c = jnp.broadcast_to(mod[bsz, 1].reshape(1, 1, d), (bsz, 1, d))

    k_scale = jnp.full((M_WIDTH,), M_HEAD_DIM ** -0.5, F32)
    qk_scale = jnp.concatenate([jnp.ones((M_WIDTH,), F32), k_scale])
    w_gates, b_gates = w_in[:, IG0:M_COLS], b_in[IG0:M_COLS]
    w_main, w_hyena = _weight_prep(w_in)
    b_main = jnp.concatenate([b_in[Q0:IG0], b_in[GA0:IN_COLS]])

    hc = _norm_mod(ctx, g_norm1, shift1c, scale1c, lc)
    kc = _proj_conv_silu(hc, w_main[:, K0 - Q0:V0 - Q0], b_in[K0:V0], w_qk_conv[:, M_WIDTH:],
                         b_qk_conv[M_WIDTH:], k_scale, lc, lc)
    vc = _proj_act(hc, w_main[:, V0 - Q0:O0 - Q0], b_in[V0:O0], "none", BF16, lc)
    bcc, acc, arc = _gates(hc, w_gates, b_gates, chunk_c)
    zero_state = (jnp.zeros((bsz, 2, M_HEADS, M_HEAD_DIM, M_HEAD_DIM), F32),
                  jnp.zeros((bsz, 2, M_HEADS, 1, M_HEAD_DIM), F32),
                  jnp.zeros((bsz, 2, M_HEADS, 1, LANES), F32))
    _, ctx_state = _mlstm(None, (kc, 0), (vc, 0), bcc, acc, arc, zero_state, False, chunk_c)

    tm = ROW_TILE
    _, dft_fast = _dft_factors(2 * L)
    pm, h, h_il = _proj_main(x, g_norm1, shift1, scale1, w_main, b_main, w_qk_conv, b_qk_conv, qk_scale,
                             seg, tm, dft_fast)
    bc, ac, ar = _gates(h, w_gates, b_gates, MLSTM_CHUNK)
    hdirs, _ = _mlstm((pm, PM_Q), (pm, PM_K), (pm, PM_V), bc, ac, ar, ctx_state, True, MLSTM_CHUNK)

    x0_t, s_t = _proj_hyena(h_il, w_hyena, b_in[HY0:GA0], w_h_conv, b_h_conv, seg)
    hy = _hyena_long_conv(s_t, x0_t, h_bias, hf_w1, hf_b1, hf_w2, hf_b2, hf_w3, hf_freq)

    x1, h2 = _merge(hdirs, pm, hy, x, gate1, g_norm2, shift2, scale2,
                    w_a.astype(BF16), w_b.astype(BF16), w_out.astype(BF16))
    return _moe(h2, x1, gate2, g_final, w_group, b_group, w_router, b_router, w1_e, w3_e, w2_e)
```

```python
import functools
import math

import jax
import jax.numpy as jnp
import numpy as np
from jax import lax
from jax.experimental import pallas as pl
from jax.experimental.pallas import tpu as pltpu
from jax.experimental.pallas import tpu_sc as plsc

F32 = jnp.float32
BF16 = jnp.bfloat16

D_MODEL = 1024
GRID_W = 64
EPS = 1e-6
M_HEADS = 4
M_HEAD_DIM = 256
M_WIDTH = M_HEADS * M_HEAD_DIM
H_WIDTH = 1024
H_POS_BANDS = 16
H_FILTER_HIDDEN = 64
H_FAST_DECAY_PCT = 0.3
H_SLOW_DECAY_PCT = 1.5
H_DECAY_TARGET = 1e-2
N_GROUPS = 8
EXPERTS_PER_GROUP = 8
N_EXPERTS = N_GROUPS * EXPERTS_PER_GROUP
D_EXPERT = 512
Q0 = 0
K0 = Q0 + M_WIDTH
V0 = K0 + M_WIDTH
O0 = V0 + M_WIDTH
IG0 = O0 + M_WIDTH
FG0 = IG0 + 2 * M_HEADS
M_COLS = FG0 + 2 * M_HEADS
HY0 = M_COLS
GA0 = HY0 + 3 * H_WIDTH
GB0 = GA0 + D_MODEL
IN_COLS = GB0 + D_MODEL

LANES = 128
MLSTM_CHUNK = 512
NEG_BIG = -1e30
VMEM_LIMIT = 48 * 1024 * 1024
ROW_TILE = 1024
ADALN_TN = 1536
SMALL_TN = 512
MERGE_TM = 512
COMBINE_TM = 512


def _cparams(*sem):
    return pltpu.CompilerParams(dimension_semantics=sem, vmem_limit_bytes=VMEM_LIMIT)


def _adaln_kernel(c_ref, w_ref, b_ref, o_ref):
    s = c_ref[...]
    s = s * jax.nn.sigmoid(s)
    o_ref[...] = jnp.dot(s.astype(BF16), w_ref[...].astype(BF16), preferred_element_type=F32) + b_ref[...]


def _adaln(cond, w_mod, b_mod):
    n = w_mod.shape[1]
    tn = ADALN_TN
    return pl.pallas_call(
        _adaln_kernel,
        grid=(n // tn,),
        in_specs=[pl.BlockSpec((8, D_MODEL), lambda j: (0, 0)),
                  pl.BlockSpec((D_MODEL, tn), lambda j: (0, j)),
                  pl.BlockSpec((1, tn), lambda j: (0, j))],
        out_specs=pl.BlockSpec((8, tn), lambda j: (0, j)),
        out_shape=jax.ShapeDtypeStruct((8, n), F32),
        compiler_params=_cparams("arbitrary"),
        name="adaln",
    )(cond, w_mod, b_mod.reshape(1, n))


def _norm_mod_kernel(x_ref, g_ref, sh_ref, sc_ref, o_ref):
    x = x_ref[...]
    y = x * lax.rsqrt(jnp.mean(x * x, axis=-1, keepdims=True) + EPS)
    y = y * g_ref[...]
    o_ref[...] = (y * (1.0 + sc_ref[...]) + sh_ref[...]).astype(o_ref.dtype)


def _norm_mod(x, g, shift, scale, tm):
    bsz, L, d = x.shape
    return pl.pallas_call(
        _norm_mod_kernel,
        grid=(bsz, L // tm),
        in_specs=[pl.BlockSpec((None, tm, d), lambda b, i: (b, i, 0)),
                  pl.BlockSpec((1, d), lambda b, i: (0, 0)),
                  pl.BlockSpec((None, 1, d), lambda b, i: (b, 0, 0)),
                  pl.BlockSpec((None, 1, d), lambda b, i: (b, 0, 0))],
        out_specs=pl.BlockSpec((None, tm, d), lambda b, i: (b, i, 0)),
        out_shape=jax.ShapeDtypeStruct((bsz, L, d), BF16),
        compiler_params=_cparams("parallel", "parallel"),
        name="norm_mod",
    )(x, g.reshape(1, d), shift, scale)


def _conv3(z, wc, bc, seg):
    tm = z.shape[0]
    pos = lax.broadcasted_iota(jnp.int32, z.shape, 0) & (seg - 1)
    zp = jnp.where(pos == 0, 0.0, pltpu.roll(z, 1, 0))
    zn = jnp.where(pos == seg - 1, 0.0, pltpu.roll(z, tm - 1, 0))
    return zp * wc[0:1, :] + z * wc[1:2, :] + zn * wc[2:3, :] + bc


def _proj_act_kernel(h_ref, w_ref, b_ref, o_ref, *, act):
    z = jnp.dot(h_ref[...], w_ref[...], preferred_element_type=F32) + b_ref[...]
    if act == "sigmoid":
        z = jax.nn.sigmoid(z)
    o_ref[...] = z.astype(o_ref.dtype)


def _proj_act(h, w, b, act, out_dtype, tm, tn=SMALL_TN):
    bsz, L, d = h.shape
    n = w.shape[1]
    return pl.pallas_call(
        functools.partial(_proj_act_kernel, act=act),
        grid=(bsz, L // tm, n // tn),
        in_specs=[pl.BlockSpec((None, tm, d), lambda b_, i, j: (b_, i, 0)),
                  pl.BlockSpec((d, tn), lambda b_, i, j: (0, j)),
                  pl.BlockSpec((1, tn), lambda b_, i, j: (0, j))],
        out_specs=pl.BlockSpec((None, tm, tn), lambda b_, i, j: (b_, i, j)),
        out_shape=jax.ShapeDtypeStruct((bsz, L, n), out_dtype),
        compiler_params=_cparams("parallel", "parallel", "arbitrary"),
        name="proj_" + act,
    )(h, w, b.reshape(1, n))


def _proj_conv_silu_kernel(h_ref, w_ref, b_ref, wc_ref, bc_ref, cs_ref, o_ref, *, seg):
    z = jnp.dot(h_ref[...], w_ref[...], preferred_element_type=F32) + b_ref[...]
    y = _conv3(z, wc_ref[...], bc_ref[...], seg)
    y = y * jax.nn.sigmoid(y)
    o_ref[...] = (y * cs_ref[...]).astype(o_ref.dtype)


def _proj_conv_silu(h, w, b, wc, bc, colscale, seg, tm, tn=SMALL_TN):
    bsz, L, d = h.shape
    n = w.shape[1]
    col = lambda b_, i, j: (0, j)
    return pl.pallas_call(
        functools.partial(_proj_conv_silu_kernel, seg=seg),
        grid=(bsz, L // tm, n // tn),
        in_specs=[pl.BlockSpec((None, tm, d), lambda b_, i, j: (b_, i, 0)),
                  pl.BlockSpec((d, tn), col),
                  pl.BlockSpec((1, tn), col),
                  pl.BlockSpec((3, tn), col),
                  pl.BlockSpec((1, tn), col),
                  pl.BlockSpec((1, tn), col)],
        out_specs=pl.BlockSpec((None, tm, tn), lambda b_, i, j: (b_, i, j)),
        out_shape=jax.ShapeDtypeStruct((bsz, L, n), BF16),
        compiler_params=_cparams("parallel", "parallel", "arbitrary"),
        name="proj_conv_silu",
    )(h, w, b.reshape(1, n), wc, bc.reshape(1, n), colscale.reshape(1, n))


PROJ_TN = 1024
MAIN_TILE_COLS = (Q0, K0, V0, O0, GA0, GB0)
HYENA_TILE_COLS = tuple(range(HY0, GA0, PROJ_TN))


def _weight_prep_kernel(start_ref, wt_hbm, main_ref, hy_ref, buf, sem):
    t = pl.program_id(0)
    n_main = len(MAIN_TILE_COLS)

    def tile_copy(step, slot):
        rows = pl.ds(pl.multiple_of(start_ref[step], 8), PROJ_TN)
        return pltpu.make_async_copy(wt_hbm.at[rows], buf.at[slot], sem.at[slot])

    @pl.when(t == 0)
    def _():
        tile_copy(0, 0).start()

    @pl.when(t + 1 < pl.num_programs(0))
    def _():
        tile_copy(t + 1, (t + 1) % 2).start()

    slot = t % 2
    tile_copy(t, slot).wait()
    tile = buf[slot].T.astype(BF16)

    @pl.when(t < n_main)
    def _():
        main_ref[...] = tile

    @pl.when(t >= n_main)
    def _():
        hy_ref[...] = tile


def _weight_prep(w_t):
    cols, d = w_t.shape
    starts = MAIN_TILE_COLS + HYENA_TILE_COLS
    n_main, n_hy = len(MAIN_TILE_COLS), len(HYENA_TILE_COLS)
    assert cols == IN_COLS and d % LANES == 0 and all(s % 8 == 0 and s + PROJ_TN <= cols for s in starts)
    grid_spec = pltpu.PrefetchScalarGridSpec(
        num_scalar_prefetch=1,
        grid=(len(starts),),
        in_specs=[pl.BlockSpec(memory_space=pl.ANY)],
        out_specs=[pl.BlockSpec((d, PROJ_TN), lambda t, *_: (0, jnp.minimum(t, n_main - 1))),
                   pl.BlockSpec((d, PROJ_TN), lambda t, *_: (0, jnp.maximum(t - n_main, 0)))],
        scratch_shapes=[pltpu.VMEM((2, PROJ_TN, d), F32), pltpu.SemaphoreType.DMA((2,))],
    )
    return pl.pallas_call(
        _weight_prep_kernel,
        grid_spec=grid_spec,
        out_shape=[jax.ShapeDtypeStruct((d, n_main * PROJ_TN), BF16), jax.ShapeDtypeStruct((d, n_hy * PROJ_TN), BF16)],
        compiler_params=_cparams("arbitrary"),
        name="weight_prep",
    )(jnp.asarray(starts, jnp.int32), w_t)


PROJ_SUB = 512
PM_Q, PM_K, PM_V, PM_O, PM_GA, PM_GB = range(6)


def _proj_main_kernel(x_ref, g_ref, sh_ref, sc_ref, w_ref, b_ref, wc_ref, bc_ref, cs_ref,
                      o_ref, h_ref, hi_hbm, hp_sc, sem, *, seg):
    b, i, j = pl.program_id(0), pl.program_id(1), pl.program_id(2)
    n2, jt = hi_hbm.shape[2], hi_hbm.shape[3]

    def interleave_copy(jj):
        return pltpu.make_async_copy(hp_sc.at[pl.ds(jj * n2, n2)], hi_hbm.at[b, i, :, jj, :], sem)

    @pl.when(j == 0)
    def _():
        x = x_ref[...]
        y = x * lax.rsqrt(jnp.mean(x * x, axis=-1, keepdims=True) + EPS) * g_ref[...]
        y = y * (1.0 + sc_ref[...]) + sh_ref[...]
        h_ref[...] = y.astype(h_ref.dtype)
        hp_sc[...] = _pack_bf16_pairs(y)
        for jj in range(jt):
            interleave_copy(jj).start()

    @pl.when(j == pl.num_programs(2) - 1)
    def _():
        for jj in range(jt):
            interleave_copy(jj).wait()

    def run(epilogue):
        for c in range(PROJ_TN // PROJ_SUB):
            sl = slice(c * PROJ_SUB, (c + 1) * PROJ_SUB)
            z = jnp.dot(h_ref[...], w_ref[:, sl], preferred_element_type=F32) + b_ref[:, sl]
            o_ref[:, sl] = epilogue(z, sl).astype(o_ref.dtype)

    def conv_silu(z, sl):
        y = _conv3(z, wc_ref[:, sl], bc_ref[:, sl], seg)
        return (y * jax.nn.sigmoid(y)) * cs_ref[:, sl]

    @pl.when(j <= PM_K)
    def _():
        run(conv_silu)

    @pl.when(j == PM_V)
    def _():
        run(lambda z, sl: z)

    @pl.when(j >= PM_O)
    def _():
        run(lambda z, sl: jax.nn.sigmoid(z))


def _proj_main(x, g, shift, scale, w, b, wc, bc, colscale, seg, tm, n2):
    bsz, L, d = x.shape
    n = w.shape[1]
    jt = tm // n2
    qk = lambda b_, i, j: (0, jnp.minimum(j, PM_K))
    row = pl.BlockSpec((None, tm, d), lambda b_, i, j: (b_, i, 0))
    bvec = pl.BlockSpec((None, 1, d), lambda b_, i, j: (b_, 0, 0))
    return pl.pallas_call(
        functools.partial(_proj_main_kernel, seg=seg),
        grid=(bsz, L // tm, n // PROJ_TN),
        in_specs=[row, pl.BlockSpec((1, d), lambda b_, i, j: (0, 0)), bvec, bvec,
                  pl.BlockSpec((d, PROJ_TN), lambda b_, i, j: (0, j)),
                  pl.BlockSpec((1, PROJ_TN), lambda b_, i, j: (0, j)),
                  pl.BlockSpec((3, PROJ_TN), qk),
                  pl.BlockSpec((1, PROJ_TN), qk),
                  pl.BlockSpec((1, PROJ_TN), qk)],
        out_specs=[pl.BlockSpec((None, tm, PROJ_TN), lambda b_, i, j: (b_, i, j)), row,
                   pl.BlockSpec(memory_space=pl.ANY)],
        out_shape=[jax.ShapeDtypeStruct((bsz, L, n), BF16), jax.ShapeDtypeStruct((bsz, L, d), BF16),
                   jax.ShapeDtypeStruct((bsz, L // tm, n2, jt, d // 2), jnp.uint32)],
        scratch_shapes=[pltpu.VMEM((tm, d // 2), jnp.uint32), pltpu.SemaphoreType.DMA(())],
        compiler_params=_cparams("parallel", "parallel", "arbitrary"),
        name="proj_main",
    )(x, g.reshape(1, d), shift, scale, w, b.reshape(1, n), wc, bc.reshape(1, -1), colscale.reshape(1, -1))


def _conv3_interleaved(z, wc, bc, seg, jt):
    grp = seg * jt
    pad = jnp.zeros((jt, z.shape[1]), z.dtype)
    prev, nxt = [], []
    for g0 in range(0, z.shape[0], grp):
        zg = z[g0:g0 + grp]
        prev += [pad, zg[:grp - jt]]
        nxt += [zg[jt:], pad]
    zp = jnp.concatenate(prev, axis=0)
    zn = jnp.concatenate(nxt, axis=0)
    return zp * wc[0:1, :] + z * wc[1:2, :] + zn * wc[2:3, :] + bc


def _proj_hyena_kernel(h_ref, w0_ref, w1_ref, w2_ref, b_ref, wc_ref, bc_ref, x0_ref, s_ref, *, seg):
    n2, jt = s_ref.shape[0], s_ref.shape[1]
    h = _unpack_bf16_pairs(h_ref[...].reshape(n2 * jt, h_ref.shape[2]))
    us = []
    for g, w_ref in enumerate((w0_ref, w1_ref, w2_ref)):
        z = jnp.dot(h, w_ref[...], preferred_element_type=F32) + b_ref[g]
        us.append(_conv3_interleaved(z, wc_ref[g], bc_ref[g], seg, jt))
    x0_ref[...] = _pack_bf16_pairs(us[0]).reshape(x0_ref.shape)
    s_ref[...] = (us[1] * us[2]).reshape(s_ref.shape)


def _proj_hyena(hi, w, b, wc, bc, seg):
    bsz, nt, n2, jt, dp = hi.shape
    d, tm = 2 * dp, n2 * jt
    L = nt * tm
    tn = DFT_C_TILE
    nblk = H_WIDTH // tn
    assert n2 % seg == 0 and (jt % 8 == 0 or nt == 1)
    b3 = b.reshape(3, 1, H_WIDTH)
    wc3 = wc.reshape(3, 3, H_WIDTH).transpose(1, 0, 2)
    bc3 = bc.reshape(3, 1, H_WIDTH)
    return pl.pallas_call(
        functools.partial(_proj_hyena_kernel, seg=seg),
        grid=(bsz, nt, nblk),
        in_specs=[pl.BlockSpec((None, None, n2, jt, dp), lambda b_, i, j: (b_, i, 0, 0, 0)),
                  pl.BlockSpec((d, tn), lambda b_, i, j: (0, j)),
                  pl.BlockSpec((d, tn), lambda b_, i, j: (0, nblk + j)),
                  pl.BlockSpec((d, tn), lambda b_, i, j: (0, 2 * nblk + j)),
                  pl.BlockSpec((3, 1, tn), lambda b_, i, j: (0, 0, j)),
                  pl.BlockSpec((3, 3, tn), lambda b_, i, j: (0, 0, j)),
                  pl.BlockSpec((3, 1, tn), lambda b_, i, j: (0, 0, j))],
        out_specs=[pl.BlockSpec((None, n2, jt, tn // 2), lambda b_, i, j: (b_, 0, i, j)),
                   pl.BlockSpec((None, n2, jt, tn), lambda b_, i, j: (b_, 0, i, j))],
        out_shape=[jax.ShapeDtypeStruct((bsz, n2, L // n2, H_WIDTH // 2), jnp.uint32),
                   jax.ShapeDtypeStruct((bsz, n2, L // n2, H_WIDTH), F32)],
        compiler_params=_cparams("parallel", "parallel", "arbitrary"),
        name="proj_hyena",
    )(hi, w, w, w, b3, wc3, bc3)


N_GATES = 4 * M_HEADS


def _split3(x):
    hi = x.astype(BF16)
    r1 = x - hi.astype(F32)
    mid = r1.astype(BF16)
    lo = (r1 - mid.astype(F32)).astype(BF16)
    return hi, mid, lo


def _log_sigmoid(x):
    return jnp.minimum(x, 0.0) - jnp.log1p(jnp.exp(-jnp.abs(x)))


def _gates_kernel(h_ref, w_ref, wt_ref, b_ref, bt_ref, bc_ref, ac_ref, ar_ref):
    h = h_ref[...]
    t = h.shape[0]
    z = jnp.dot(h, w_ref[...], preferred_element_type=F32) + b_ref[...]
    zt = lax.dot_general(wt_ref[...], h, (((1,), (1,)), ((), ())),
                         preferred_element_type=F32) + bt_ref[...]
    r = lax.broadcasted_iota(jnp.int32, (t, t), 0)
    c = lax.broadcasted_iota(jnp.int32, (t, t), 1)
    lower = (r >= c).astype(BF16)
    upper = (r <= c).astype(BF16)
    g8 = FG_LANE0

    lf = _log_sigmoid(z)
    lane = lax.broadcasted_iota(jnp.int32, z.shape, 1)
    is_fg = (lane >= g8) & (lane < 2 * g8)
    terms = [jnp.where(is_fg, p.astype(F32), 0.0) for p in _split3(lf)]
    packed = terms[0] + pltpu.roll(terms[1], 2 * g8, 1) + pltpu.roll(terms[2], 4 * g8, 1)
    cfp = jnp.dot(lower, packed.astype(BF16), preferred_element_type=F32)
    cf = cfp + pltpu.roll(cfp, LANES - 2 * g8, 1) + pltpu.roll(cfp, LANES - 4 * g8, 1)
    cb = cf[t - 1:t, :] - cf + lf
    bc = jnp.where(lane < g8 + M_HEADS, cf, cb)
    bc = pltpu.roll(bc, LANES - g8, 1)
    bc_ref[...] = bc
    ac_ref[...] = z - bc

    lft = _log_sigmoid(zt[g8:, :])
    stacked = jnp.concatenate([p.astype(F32) for p in _split3(lft)] + [jnp.zeros_like(lft)], axis=0)
    cft3 = jnp.dot(stacked.astype(BF16), upper, preferred_element_type=F32)
    cft = cft3[0:g8] + cft3[g8:2 * g8] + cft3[2 * g8:3 * g8]
    cbt = cft[:, t - 1:t] - cft + lft
    row = lax.broadcasted_iota(jnp.int32, cft.shape, 0)
    ar_ref[...] = zt[:g8, :] - jnp.where(row < M_HEADS, cft, cbt)


FG_LANE0 = 2 * M_HEADS


def _gates(h, w_g, b_g, chunk):
    bsz, L, d = h.shape
    w_pad = jnp.zeros((d, LANES), F32).at[:, :N_GATES].set(w_g).astype(BF16)
    b_pad = jnp.zeros((1, LANES), F32).at[0, :N_GATES].set(b_g)
    wt = w_g.T.astype(BF16)
    bt = b_g.reshape(N_GATES, 1)
    tok = pl.BlockSpec((None, chunk, LANES), lambda b_, i: (b_, i, 0))
    return pl.pallas_call(
        _gates_kernel,
        grid=(bsz, L // chunk),
        in_specs=[pl.BlockSpec((None, chunk, d), lambda b_, i: (b_, i, 0)),
                  pl.BlockSpec((d, LANES), lambda b_, i: (0, 0)),
                  pl.BlockSpec((N_GATES, d), lambda b_, i: (0, 0)),
                  pl.BlockSpec((1, LANES), lambda b_, i: (0, 0)),
                  pl.BlockSpec((N_GATES, 1), lambda b_, i: (0, 0))],
        out_specs=[tok, tok, pl.BlockSpec((None, FG_LANE0, chunk), lambda b_, i: (b_, 0, i))],
        out_shape=[jax.ShapeDtypeStruct((bsz, L, LANES), F32),
                   jax.ShapeDtypeStruct((bsz, L, LANES), F32),
                   jax.ShapeDtypeStruct((bsz, FG_LANE0, L), F32)],
        compiler_params=_cparams("parallel", "parallel"),
        name="mlstm_gates",
    )(h, w_pad, wt, b_pad, bt)


def _mlstm_kernel(*refs, emit_h, n_chunks):
    if emit_h:
        (q_ref, k_ref, v_ref, bc_ref, ac_ref, ar_ref, c0_ref, n0_ref, m0_ref,
         h_ref, cf_ref, nf_ref, mf_ref, c_sc, n_sc, m_sc) = refs
    else:
        (k_ref, v_ref, bc_ref, ac_ref, ar_ref, c0_ref, n0_ref, m0_ref,
         cf_ref, nf_ref, mf_ref, c_sc, n_sc, m_sc) = refs
    d = pl.program_id(1)
    j = pl.program_id(2)
    fwd = d == 0
    t = k_ref.shape[0]
    dh = M_HEAD_DIM

    @pl.when(j == 0)
    def _():
        c_sc[...] = c0_ref[...]
        n_sc[...] = n0_ref[...]
        m_sc[...] = m0_ref[...]

    r = lax.broadcasted_iota(jnp.int32, (t, t), 0)
    c = lax.broadcasted_iota(jnp.int32, (t, t), 1)
    causal = jnp.where(fwd, r - c, c - r) >= 0
    bc_all = bc_ref[...]
    ac_all = ac_ref[...]
    ar_all = ar_ref[...]
    for hd in range(M_HEADS):
        sl = slice(hd * dh, (hd + 1) * dh)
        bc = jnp.where(fwd, bc_all[:, hd:hd + 1], bc_all[:, M_HEADS + hd:M_HEADS + hd + 1])
        ac = jnp.where(fwd, ac_all[:, hd:hd + 1], ac_all[:, M_HEADS + hd:M_HEADS + hd + 1])
        ar = jnp.where(fwd, ar_all[hd:hd + 1, :], ar_all[M_HEADS + hd:M_HEADS + hd + 1, :])
        b_tot = jnp.where(fwd, bc[t - 1:t, :], bc[0:1, :])
        m_prev = m_sc[hd][:, 0:1]
        k_h = k_ref[:, sl]
        v_h = v_ref[:, sl]
        if emit_h:
            q_h = q_ref[:, sl]
            dm = jnp.where(causal, bc + ar, NEG_BIG)
            inter = bc + m_prev
            m_t = jnp.maximum(inter, jnp.max(dm, axis=1, keepdims=True))
            qk = lax.dot_general(q_h, k_h, (((1,), (1,)), ((), ())), preferred_element_type=F32)
            s = qk * jnp.exp(dm - m_t)
            carry = jnp.exp(inter - m_t)
            num = (jnp.dot(s.astype(BF16), v_h, preferred_element_type=F32)
                   + carry * jnp.dot(q_h, c_sc[hd].astype(BF16), preferred_element_type=F32))
            den = (jnp.sum(s, axis=1, keepdims=True)
                   + carry * jnp.sum(q_h.astype(F32) * n_sc[hd], axis=1, keepdims=True))
            h_ref[:, sl] = (num / jnp.maximum(jnp.abs(den), jnp.exp(-m_t))).astype(h_ref.dtype)
        g = b_tot + ac
        m_new = jnp.maximum(b_tot + m_prev, jnp.max(g, axis=0, keepdims=True))
        wgt = jnp.exp(g - m_new)
        decay = jnp.exp(b_tot + m_prev - m_new)
        kw = k_h.astype(F32) * wgt
        c_sc[hd] = decay * c_sc[hd] + lax.dot_general(kw.astype(BF16), v_h, (((0,), (0,)), ((), ())),
                                                      preferred_element_type=F32)
        n_sc[hd] = decay * n_sc[hd] + jnp.sum(kw, axis=0, keepdims=True)
        m_sc[hd] = jnp.broadcast_to(m_new, (1, LANES))

    @pl.when(j == n_chunks - 1)
    def _():
        cf_ref[...] = c_sc[...]
        nf_ref[...] = n_sc[...]
        mf_ref[...] = m_sc[...]


def _mlstm(q, k, v, bc, ac, ar, state, emit_h, t):
    bsz, L, _ = k[0].shape
    nc = L // t
    seq = lambda b_, d, j: (b_, j + d * (nc - 1 - 2 * j), 0)
    st = lambda b_, d, j: (b_, d, 0, 0, 0)

    def tok(col):
        return pl.BlockSpec((None, t, M_WIDTH), lambda b_, d, j: (b_, j + d * (nc - 1 - 2 * j), col))

    gate_spec = pl.BlockSpec((None, t, LANES), seq)
    ar_spec = pl.BlockSpec((None, FG_LANE0, t), lambda b_, d, j: (b_, 0, j + d * (nc - 1 - 2 * j)))
    c_spec = pl.BlockSpec((None, None, M_HEADS, M_HEAD_DIM, M_HEAD_DIM), st)
    n_spec = pl.BlockSpec((None, None, M_HEADS, 1, M_HEAD_DIM), st)
    m_spec = pl.BlockSpec((None, None, M_HEADS, 1, LANES), st)
    state_shapes = [jax.ShapeDtypeStruct((bsz, 2, M_HEADS, M_HEAD_DIM, M_HEAD_DIM), F32),
                    jax.ShapeDtypeStruct((bsz, 2, M_HEADS, 1, M_HEAD_DIM), F32),
                    jax.ShapeDtypeStruct((bsz, 2, M_HEADS, 1, LANES), F32)]
    in_specs = [tok(k[1]), tok(v[1]), gate_spec, gate_spec, ar_spec, c_spec, n_spec, m_spec]
    args = [k[0], v[0], bc, ac, ar, *state]
    out_specs = [c_spec, n_spec, m_spec]
    out_shape = list(state_shapes)
    if emit_h:
        in_specs = [tok(q[1])] + in_specs
        args = [q[0]] + args
        out_specs = [pl.BlockSpec((None, None, t, M_WIDTH),
                                  lambda b_, d, j: (d, b_, j + d * (nc - 1 - 2 * j), 0))] + out_specs
        out_shape = [jax.ShapeDtypeStruct((2, bsz, L, M_WIDTH), BF16)] + out_shape
    outs = pl.pallas_call(
        functools.partial(_mlstm_kernel, emit_h=emit_h, n_chunks=nc),
        grid=(bsz, 2, nc),
        in_specs=in_specs,
        out_specs=out_specs,
        out_shape=out_shape,
        scratch_shapes=[pltpu.VMEM((M_HEADS, M_HEAD_DIM, M_HEAD_DIM), F32),
                        pltpu.VMEM((M_HEADS, 1, M_HEAD_DIM), F32),
                        pltpu.VMEM((M_HEADS, 1, LANES), F32)],
        compiler_params=_cparams("parallel", "parallel", "arbitrary"),
        name="mlstm" if emit_h else "mlstm_state",
    )(*args)
    if emit_h:
        return outs[0], tuple(outs[1:])
    return None, tuple(outs)


DFT_M_TILE = 8
DFT_C_TILE = 1024
DFT_INNER_C_TILE = 512
FEAT_ROWS = 16


def _filter_outer_kernel(bands_ref, w1t_ref, b1_ref, w2t_ref, b2_ref, w3p_ref, w3f_ref, fr_ref, dl_ref, l_ref,
                         a_ref, ss_ref, *, L, n1, n2):
    i = pl.program_id(0)
    h = n1 // 2
    cols = DFT_M_TILE * h

    def positions(shape, axis, side):
        q = lax.broadcasted_iota(jnp.int32, shape, axis)
        mm, jj = q // h, q % h
        n = n2 * (jj + side * h) + i * DFT_M_TILE + mm
        return n, jnp.where(n < L, n, 2 * L - n).astype(F32)

    taps = []
    sumsq = jnp.zeros((1, a_ref.shape[-1]), F32)
    for side, w3_ref in ((0, w3p_ref), (1, w3f_ref)):
        _, p_row = positions((1, cols), 1, side)
        t_row = p_row / float(max(L - 1, 1))
        ang = ((2 * math.pi / L) * p_row) * bands_ref[...]
        row = lax.broadcasted_iota(jnp.int32, (FEAT_ROWS, cols), 0)
        feats = jnp.concatenate([jnp.where(row == 0, t_row, 0.0), jnp.cos(ang), -jnp.sin(ang)], axis=0)
        fr = fr_ref[...]
        hid = jnp.sin(fr * (jnp.dot(w1t_ref[...], feats.astype(BF16), preferred_element_type=F32) + b1_ref[...]))
        hid = jnp.sin(fr * (jnp.dot(w2t_ref[...], hid.astype(BF16), preferred_element_type=F32) + b2_ref[...]))
        filt = lax.dot_general(hid.astype(BF16), w3_ref[...], (((0,), (0,)), ((), ())),
                               preferred_element_type=F32)
        n_col, p_col = positions((cols, 1), 0, side)
        t_col = p_col / float(max(L - 1, 1))
        kern = filt * jnp.exp(-t_col * jnp.abs(dl_ref[...]))
        kern = jnp.where(n_col == L, 0.0, kern)
        sumsq = sumsq + jnp.sum(kern * kern, axis=0, keepdims=True)
        taps.append(kern)

    for mm in range(DFT_M_TILE):
        x = jnp.concatenate([taps[0][mm * h:(mm + 1) * h], taps[1][mm * h:(mm + 1) * h]], axis=0)
        out = jnp.dot(l_ref[...], x.astype(BF16), preferred_element_type=F32)
        a_ref[0, :, mm, :] = out[:n1]
        a_ref[1, :, mm, :] = out[n1:]

    @pl.when(i == 0)
    def _():
        ss_ref[...] = jnp.zeros_like(ss_ref)

    ss_ref[...] += sumsq


def _filter_outer(L, n1, n2, fwd_r, w1, b1, w2, b2, w3, freq):
    hid = H_FILTER_HIDDEN
    bands = jnp.linspace(1e-4, H_POS_BANDS - 1, H_POS_BANDS, dtype=F32).reshape(H_POS_BANDS, 1)
    w1t = jnp.zeros((hid, 3 * FEAT_ROWS), F32)
    w1t = w1t.at[:, 0].set(w1[0]).at[:, FEAT_ROWS:2 * FEAT_ROWS].set(w1[1:1 + H_POS_BANDS].T)
    w1t = w1t.at[:, 2 * FEAT_ROWS:].set(w1[1 + H_POS_BANDS:].T).astype(BF16)
    w3h = w3.astype(BF16)
    max_decay = math.log(H_DECAY_TARGET) / H_FAST_DECAY_PCT
    min_decay = math.log(H_DECAY_TARGET) / H_SLOW_DECAY_PCT
    deltas = jnp.linspace(min_decay, max_decay, H_WIDTH, dtype=F32).reshape(1, H_WIDTH)
    col = lambda v: v.reshape(hid, 1)
    full = lambda a: pl.BlockSpec(a.shape, lambda i: (0,) * a.ndim)
    args = [bands, w1t, col(b1), w2.T.astype(BF16), col(b2)]
    return pl.pallas_call(
        functools.partial(_filter_outer_kernel, L=L, n1=n1, n2=n2),
        grid=(n2 // DFT_M_TILE,),
        in_specs=[full(a) for a in args]
        + [pl.BlockSpec((hid, H_WIDTH), lambda i: (0, 0)), pl.BlockSpec((hid, H_WIDTH), lambda i: (0, 1)),
           full(col(freq)), full(deltas), full(fwd_r)],
        out_specs=[pl.BlockSpec((2, n1, DFT_M_TILE, H_WIDTH), lambda i: (0, 0, i, 0)),
                   pl.BlockSpec((1, H_WIDTH), lambda i: (0, 0))],
        out_shape=[jax.ShapeDtypeStruct((2, n1, n2, H_WIDTH), F32),
                   jax.ShapeDtypeStruct((1, H_WIDTH), F32)],
        compiler_params=_cparams("arbitrary"),
        name="hyena_filter_outer",
    )(*args, w3h, w3h, col(freq), deltas, fwd_r)


def _dft_factors(n):
    lg = int(round(math.log2(n)))
    n1 = 1 << ((lg + 1) // 2)
    return n1, n // n1


def _dft_outer_matrices(n1):
    k = np.arange(n1)[:, None]
    n = np.arange(n1)[None, :]
    ang = 2.0 * np.pi * ((k * n) % n1) / n1
    cr, ci = np.cos(ang), -np.sin(ang)
    h = n1 // 2
    fwd_c = np.block([[cr[:, :h], -ci[:, :h]], [ci[:, :h], cr[:, :h]]])
    fwd_r = np.concatenate([cr, ci], axis=0)
    ir, ii = cr[:h, :], -ci[:h, :]
    inv = np.block([[ir, -ii], [ii, ir]])
    return (jnp.asarray(fwd_c, F32).astype(BF16), jnp.asarray(fwd_r, F32).astype(BF16),
            jnp.asarray(inv, F32).astype(BF16))


def _dft_inner_matrices(n1, n2):
    n = n1 * n2
    k2 = np.arange(n2)[:, None]
    m = np.arange(n2)[None, :]
    ang = 2.0 * np.pi * ((k2 * m) % n2) / n2
    fr, fi = np.cos(ang), -np.sin(ang)
    f = np.block([[fr, -fi], [fi, fr]])
    k1 = jnp.arange(n1, dtype=jnp.int32)[:, None]
    tw_ang = ((jnp.arange(n2, dtype=jnp.int32)[None, :] * k1) % n).astype(F32) * (2.0 * math.pi / n)
    rep = lambda t: jnp.broadcast_to(t[:, :, None], (n1, n2, LANES))
    return (jnp.asarray(f, F32).astype(BF16), jnp.asarray(f.T, F32).astype(BF16),
            rep(jnp.cos(tw_ang)), rep(-jnp.sin(tw_ang)))


def _outer_fwd_kernel(l_ref, s_ref, a_ref):
    n1 = a_ref.shape[1]
    for mm in range(s_ref.shape[1]):
        x = jnp.concatenate([s_ref[0, mm], s_ref[1, mm]], axis=0).astype(BF16)
        out = jnp.dot(l_ref[...], x, preferred_element_type=F32)
        a_ref[0, :, mm, :] = out[:n1]
        a_ref[1, :, mm, :] = out[n1:]


def _outer_fwd(lmat, s_t):
    _, n2, n1h, c = s_t.shape
    n1 = 2 * n1h
    tc = min(DFT_C_TILE, c)
    return pl.pallas_call(
        _outer_fwd_kernel,
        grid=(n2 // DFT_M_TILE, c // tc),
        in_specs=[pl.BlockSpec(lmat.shape, lambda m, j: (0, 0)),
                  pl.BlockSpec((2, DFT_M_TILE, n1h, tc), lambda m, j: (0, m, 0, j))],
        out_specs=pl.BlockSpec((2, n1, DFT_M_TILE, tc), lambda m, j: (0, 0, m, j)),
        out_shape=jax.ShapeDtypeStruct((2, n1, n2, c), F32),
        compiler_params=_cparams("parallel", "parallel"),
        name="dft_outer_fwd",
    )(lmat, s_t)


def _outer_inv_kernel(l_ref, b_ref, s_ref, x0_ref, ysc_ref, hb_ref, o_ref):
    n1h = s_ref.shape[2]
    for mm in range(b_ref.shape[1]):
        y = jnp.concatenate([b_ref[0, mm], b_ref[1, mm]], axis=0).astype(BF16)
        out = jnp.dot(l_ref[...], y, preferred_element_type=F32)
        for b in range(2):
            conv = out[b * n1h:(b + 1) * n1h]
            x0 = _unpack_bf16_pairs(x0_ref[b, mm]).astype(F32)
            hy = x0 * (conv * ysc_ref[...] + hb_ref[...] * s_ref[b, mm])
            o_ref[b, :, mm, :] = _pack_bf16_pairs(hy)


def _outer_inv(lmat, b_t, s_t, x0_t, yscale, h_bias):
    _, n2, n1, c = b_t.shape
    n1h = n1 // 2
    tc = min(DFT_C_TILE, c)
    vec = pl.BlockSpec((1, tc), lambda m, j: (0, j))
    hy = pl.pallas_call(
        _outer_inv_kernel,
        grid=(n2 // DFT_M_TILE, c // tc),
        in_specs=[pl.BlockSpec(lmat.shape, lambda m, j: (0, 0)),
                  pl.BlockSpec((2, DFT_M_TILE, n1, tc), lambda m, j: (0, m, 0, j)),
                  pl.BlockSpec((2, DFT_M_TILE, n1h, tc), lambda m, j: (0, m, 0, j)),
                  pl.BlockSpec((2, DFT_M_TILE, n1h, tc // 2), lambda m, j: (0, m, 0, j)),
                  vec, vec],
        out_specs=pl.BlockSpec((2, n1h, DFT_M_TILE, tc // 2), lambda m, j: (0, 0, m, j)),
        out_shape=jax.ShapeDtypeStruct((2, n1h, n2, c // 2), jnp.uint32),
        compiler_params=_cparams("parallel", "parallel"),
        name="dft_outer_inv",
    )(lmat, b_t, s_t, x0_t, yscale, h_bias.reshape(1, c))
    return hy.reshape(2, n1h * n2, c // 2)


DFT_K_TILE = 8


def _twiddled_inner_dft(f_ref, twr_ref, twi_ref, a_ref, kk):
    n2, c = a_ref.shape[2], a_ref.shape[3]
    twr = jnp.tile(twr_ref[kk], (1, c // LANES))
    twi = jnp.tile(twi_ref[kk], (1, c // LANES))
    ar, ai = a_ref[0, kk], a_ref[1, kk]
    a = jnp.concatenate([(ar * twr - ai * twi).astype(BF16), (ar * twi + ai * twr).astype(BF16)], axis=0)
    x = jnp.dot(f_ref[...], a, preferred_element_type=F32)
    return x[:n2], x[n2:], twr, twi


def _inner_fwd_kernel(f_ref, twr_ref, twi_ref, a_ref, o_ref):
    for kk in range(a_ref.shape[1]):
        xr, xi, _, _ = _twiddled_inner_dft(f_ref, twr_ref, twi_ref, a_ref, kk)
        o_ref[0, kk] = xr.astype(o_ref.dtype)
        o_ref[1, kk] = xi.astype(o_ref.dtype)


def _inner_specs(n1, n2, c):
    tc = min(DFT_INNER_C_TILE, c)
    kt = min(DFT_K_TILE, n1)
    blk = pl.BlockSpec((2, kt, n2, tc), lambda k, j: (0, k, 0, j))
    mat = pl.BlockSpec((2 * n2, 2 * n2), lambda k, j: (0, 0))
    tw = pl.BlockSpec((kt, n2, LANES), lambda k, j: (k, 0, 0))
    return blk, mat, tw, (n1 // kt, c // tc), kt, tc


def _inner_fwd(f, twr, twi, a):
    _, n1, n2, c = a.shape
    blk, mat, tw, grid, _, _ = _inner_specs(n1, n2, c)
    return pl.pallas_call(
        _inner_fwd_kernel,
        grid=grid,
        in_specs=[mat, tw, tw, blk],
        out_specs=blk,
        out_shape=jax.ShapeDtypeStruct((2, n1, n2, c), BF16),
        compiler_params=_cparams("parallel", "parallel"),
        name="dft_inner_filter",
    )(f, twr, twi, a)


def _inner_conv_kernel(f_ref, ft_ref, twr_ref, twi_ref, a_ref, k_ref, o_ref):
    n2 = a_ref.shape[2]
    for kk in range(a_ref.shape[1]):
        xr, xi, twr, twi = _twiddled_inner_dft(f_ref, twr_ref, twi_ref, a_ref, kk)
        kr, ki = k_ref[0, kk].astype(F32), k_ref[1, kk].astype(F32)
        yr = xr * kr - xi * ki
        yi = xr * ki + xi * kr
        y = jnp.concatenate([yr.astype(BF16), yi.astype(BF16)], axis=0)
        b = jnp.dot(ft_ref[...], y, preferred_element_type=F32)
        br, bi = b[:n2], b[n2:]
        o_ref[0, :, kk, :] = br * twr + bi * twi
        o_ref[1, :, kk, :] = bi * twr - br * twi


def _inner_conv(f, ft, twr, twi, a, kf):
    _, n1, n2, c = a.shape
    blk, mat, tw, grid, kt, tc = _inner_specs(n1, n2, c)
    return pl.pallas_call(
        _inner_conv_kernel,
        grid=grid,
        in_specs=[mat, mat, tw, tw, blk, blk],
        out_specs=pl.BlockSpec((2, n2, kt, tc), lambda k, j: (0, 0, k, j)),
        out_shape=jax.ShapeDtypeStruct((2, n2, n1, c), F32),
        compiler_params=_cparams("parallel", "parallel"),
        name="dft_inner_conv",
    )(f, ft, twr, twi, a, kf)


def _hyena_long_conv(s_t, x0_t, h_bias, w1, b1, w2, b2, w3, freq):
    bsz, n2, n1h, c = s_t.shape
    assert bsz == 2
    n1 = 2 * n1h
    L = n1h * n2
    fwd_c, fwd_r, inv = _dft_outer_matrices(n1)
    f, ft, twr, twi = _dft_inner_matrices(n1, n2)
    af, sumsq = _filter_outer(L, n1, n2, fwd_r, w1, b1, w2, b2, w3, freq)
    kf = _inner_fwd(f, twr, twi, af)
    a = _outer_fwd(fwd_c, s_t)
    b_t = _inner_conv(f, ft, twr, twi, a, kf)
    yscale = lax.rsqrt(sumsq + EPS) * (1.0 / (2 * L))
    return _outer_inv(inv, b_t, s_t, x0_t, yscale, h_bias)


def _pack_bf16_pairs(x):
    half = x.shape[1] // 2
    lo = pltpu.bitcast(x[:, :half].astype(BF16).astype(F32), jnp.uint32) >> 16
    hi = pltpu.bitcast(x[:, half:].astype(BF16).astype(F32), jnp.uint32) & jnp.uint32(0xFFFF0000)
    return lo | hi


def _unpack_bf16_pairs(p):
    lo = pltpu.bitcast(p << 16, F32).astype(BF16)
    hi = pltpu.bitcast(p & jnp.uint32(0xFFFF0000), F32).astype(BF16)
    return jnp.concatenate([lo, hi], axis=1)


def _merge_kernel(hf_ref, hb_ref, o_ref, hy_ref, ga_ref, gb_ref, x_ref,
                  gate_ref, g2_ref, sh_ref, sc_ref, wa_ref, wb_ref, wo_ref, x1_ref, h2_ref):
    a = o_ref[...].astype(F32) * (hf_ref[...].astype(F32) + hb_ref[...].astype(F32))
    half = DFT_C_TILE // 2
    hy = jnp.concatenate([_unpack_bf16_pairs(hy_ref[:, c * half:(c + 1) * half])
                          for c in range(hy_ref.shape[1] // half)], axis=1)
    pa = jnp.dot(a.astype(BF16), wa_ref[...], preferred_element_type=F32)
    pb = jnp.dot(hy, wb_ref[...], preferred_element_type=F32)
    mix = ga_ref[...].astype(F32) * pa + gb_ref[...].astype(F32) * pb
    out = jnp.dot(mix.astype(BF16), wo_ref[...], preferred_element_type=F32)
    x1 = x_ref[...] + gate_ref[...] * out
    x1_ref[...] = x1
    y = x1 * lax.rsqrt(jnp.mean(x1 * x1, axis=-1, keepdims=True) + EPS) * g2_ref[...]
    h2_ref[...] = _pack_bf16_pairs(y * (1.0 + sc_ref[...]) + sh_ref[...])


def _merge(hdirs, pm, hy, x, gate1, g2, shift2, scale2, w_a, w_b, w_out, tm=MERGE_TM):
    bsz, L, d = x.shape
    tok = pl.BlockSpec((None, tm, d), lambda b, i: (b, i, 0))

    def pm_tile(col):
        return pl.BlockSpec((None, tm, d), lambda b, i: (b, i, col))

    packed = pl.BlockSpec((None, tm, d // 2), lambda b, i: (b, i, 0))
    vec = pl.BlockSpec((1, d), lambda b, i: (0, 0))
    bvec = pl.BlockSpec((None, 1, d), lambda b, i: (b, 0, 0))
    wsp = pl.BlockSpec((d, d), lambda b, i: (0, 0), pipeline_mode=pl.Buffered(1))
    return pl.pallas_call(
        _merge_kernel,
        grid=(bsz, L // tm),
        in_specs=[pl.BlockSpec((None, None, tm, d), lambda b, i: (0, b, i, 0)),
                  pl.BlockSpec((None, None, tm, d), lambda b, i: (1, b, i, 0)),
                  pm_tile(PM_O), packed, pm_tile(PM_GA), pm_tile(PM_GB), tok,
                  bvec, vec, bvec, bvec, wsp, wsp, wsp],
        out_specs=[tok, packed],
        out_shape=[jax.ShapeDtypeStruct((bsz, L, d), F32), jax.ShapeDtypeStruct((bsz, L, d // 2), jnp.uint32)],
        compiler_params=_cparams("parallel", "parallel"),
        name="merge",
    )(hdirs, hdirs, pm, hy, pm, pm, x, gate1, g2.reshape(1, d), shift2, scale2, w_a, w_b, w_out)


MOE_BLOCK = 256
ROUTE_E1, ROUTE_E2, ROUTE_W1, ROUTE_W2 = 0, 1, 2, 3
EXP_LANE0 = N_GROUPS


def _first_lane_of_max(val, valid, lane):
    masked = jnp.where(valid, val, NEG_BIG)
    mx = jnp.max(masked, axis=1, keepdims=True)
    idx = jnp.min(jnp.where(valid & (masked == mx), lane, LANES), axis=1, keepdims=True)
    return mx, idx


MOE_TM = 1024


def _expert_onehots(rec):
    lane = lax.broadcasted_iota(jnp.int32, rec.shape, 1)
    oh1 = lane == rec[:, ROUTE_E1:ROUTE_E1 + 1].astype(jnp.int32)
    oh2 = lane == rec[:, ROUTE_E2:ROUTE_E2 + 1].astype(jnp.int32)
    return oh1, oh2


def _router_kernel(h_ref, w_ref, b_ref, r_ref, cnt_ref):
    logits = jnp.dot(_unpack_bf16_pairs(h_ref[...]), w_ref[...], preferred_element_type=F32) + b_ref[...]
    lane = lax.broadcasted_iota(jnp.int32, logits.shape, 1)
    is_g = lane < N_GROUPS
    gmax, gsel = _first_lane_of_max(logits, is_g, lane)
    gsum = jnp.sum(jnp.where(is_g, jnp.exp(logits - gmax), 0.0), axis=1, keepdims=True)
    gw = 1.0 / gsum
    lo = EXP_LANE0 + gsel * EXPERTS_PER_GROUP
    in_grp = (lane >= lo) & (lane < lo + EXPERTS_PER_GROUP)
    emax, l1 = _first_lane_of_max(logits, in_grp, lane)
    esum = jnp.sum(jnp.where(in_grp, jnp.exp(logits - emax), 0.0), axis=1, keepdims=True)
    e2max, l2 = _first_lane_of_max(logits, in_grp & (lane != l1), lane)
    v1 = 1.0 / esum
    v2 = jnp.exp(e2max - emax) / esum
    vs = v1 + v2
    w1 = gw * v1 / vs
    w2 = gw * v2 / vs
    e1 = (l1 - EXP_LANE0).astype(F32)
    e2 = (l2 - EXP_LANE0).astype(F32)
    rec = jnp.where(lane == ROUTE_E1, e1,
                    jnp.where(lane == ROUTE_E2, e2,
                              jnp.where(lane == ROUTE_W1, w1,
                                        jnp.where(lane == ROUTE_W2, w2, 0.0))))
    r_ref[...] = rec
    oh1, oh2 = _expert_onehots(rec)
    counts = jnp.sum((oh1 | oh2).astype(F32), axis=0, keepdims=True)
    cnt_ref[...] = jnp.broadcast_to(counts, cnt_ref.shape)


def _router(h2, w_group, b_group, w_router, b_router):
    n, dp = h2.shape
    d = 2 * dp
    tm = MOE_TM
    w = jnp.zeros((d, LANES), F32).at[:, :N_GROUPS].set(w_group).at[
        :, EXP_LANE0:EXP_LANE0 + N_EXPERTS].set(w_router).astype(BF16)
    b = jnp.zeros((1, LANES), F32).at[0, :N_GROUPS].set(b_group).at[
        0, EXP_LANE0:EXP_LANE0 + N_EXPERTS].set(b_router)
    return pl.pallas_call(
        _router_kernel,
        grid=(n // tm,),
        in_specs=[pl.BlockSpec((tm, dp), lambda i: (i, 0)),
                  pl.BlockSpec((d, LANES), lambda i: (0, 0)),
                  pl.BlockSpec((1, LANES), lambda i: (0, 0))],
        out_specs=[pl.BlockSpec((tm, LANES), lambda i: (i, 0)),
                   pl.BlockSpec((None, 8, LANES), lambda i: (i, 0, 0))],
        out_shape=[jax.ShapeDtypeStruct((n, LANES), F32), jax.ShapeDtypeStruct((n // tm, 8, LANES), F32)],
        compiler_params=_cparams("parallel"),
        name="moe_router",
    )(h2, w, b)


def _slots_kernel(r_ref, base_ref, dest_ref):
    rec = r_ref[...]
    tm = rec.shape[0]
    lane = lax.broadcasted_iota(jnp.int32, rec.shape, 1)
    oh1, oh2 = _expert_onehots(rec)
    r = lax.broadcasted_iota(jnp.int32, (tm, tm), 0)
    c = lax.broadcasted_iota(jnp.int32, (tm, tm), 1)
    earlier = (r > c).astype(BF16)
    rank = jnp.dot(earlier, (oh1 | oh2).astype(BF16), preferred_element_type=F32) + base_ref[0:1, :]
    d1 = jnp.sum(jnp.where(oh1, rank, 0.0), axis=1, keepdims=True)
    d2 = jnp.sum(jnp.where(oh2, rank, 0.0), axis=1, keepdims=True)
    dest_ref[...] = jnp.where(lane == 0, d1, jnp.where(lane == 1, d2, 0.0)).astype(jnp.int32)


def _slots(route, tile_counts):
    n = route.shape[0]
    tm = MOE_TM
    cnt = tile_counts[:, 0, :]
    totals = jnp.sum(cnt, axis=0)
    nblk = jnp.ceil(totals * (1.0 / MOE_BLOCK))
    first_slot = (jnp.cumsum(nblk) - nblk) * float(MOE_BLOCK)
    base = first_slot[None, :] + jnp.cumsum(cnt, axis=0) - cnt
    base = jnp.broadcast_to(base[:, None, :], tile_counts.shape)
    dest = pl.pallas_call(
        _slots_kernel,
        grid=(n // tm,),
        in_specs=[pl.BlockSpec((tm, LANES), lambda i: (i, 0)),
                  pl.BlockSpec((None, 8, LANES), lambda i: (i, 0, 0))],
        out_specs=pl.BlockSpec((tm, LANES), lambda i: (i, 0)),
        out_shape=jax.ShapeDtypeStruct((n, LANES), jnp.int32),
        compiler_params=_cparams("parallel"),
        name="moe_slots",
    )(route, base)
    return dest, totals


EXPERT_STEP_BLOCKS = 4


def _experts_kernel(be_ref, first_ref, nxt_ref, par_ref, nu_ref, x_ref, w1_hbm, w3_hbm, w2_hbm, o_ref,
                    w1f, w3f, w2f, w1b, w3b, w2b, sems):
    step = pl.program_id(0)

    def weight_copies(e, slot):
        return (pltpu.make_async_copy(w1_hbm.at[e], w1f.at[slot], sems.at[0, slot]),
                pltpu.make_async_copy(w3_hbm.at[e], w3f.at[slot], sems.at[1, slot]),
                pltpu.make_async_copy(w2_hbm.at[e], w2f.at[slot], sems.at[2, slot]))

    @pl.when(step == 0)
    def _():
        for cp in weight_copies(be_ref[0], 0):
            cp.start()

    for sub in range(EXPERT_STEP_BLOCKS):
        i = step * EXPERT_STEP_BLOCKS + sub
        rows = pl.ds(sub * MOE_BLOCK, MOE_BLOCK)

        @pl.when(first_ref[i] == 1)
        def _():
            slot = par_ref[i]

            @pl.when(nxt_ref[i] >= 0)
            def _():
                for cp in weight_copies(nxt_ref[i], 1 - slot):
                    cp.start()

            for cp in weight_copies(be_ref[i], slot):
                cp.wait()
            w1b[...] = w1f[slot].astype(BF16)
            w3b[...] = w3f[slot].astype(BF16)
            w2b[...] = w2f[slot].astype(BF16)

        @pl.when(i < nu_ref[0])
        def _():
            x = _unpack_bf16_pairs(x_ref[rows, :])
            a = jnp.dot(x, w1b[...], preferred_element_type=F32)
            b = jnp.dot(x, w3b[...], preferred_element_type=F32)
            hmid = (a * jax.nn.sigmoid(a)) * b
            o_ref[rows, :] = _pack_bf16_pairs(jnp.dot(hmid.astype(BF16), w2b[...], preferred_element_type=F32))

        @pl.when(i >= nu_ref[0])
        def _():
            o_ref[rows, :] = jnp.zeros((MOE_BLOCK, o_ref.shape[1]), o_ref.dtype)


def _experts(xs, nb, block_e, n_used, w1_e, w3_e, w2_e):
    dp = xs.shape[1]
    d, de = w1_e.shape[1], w1_e.shape[2]
    idx = jnp.arange(nb, dtype=jnp.int32)
    used = idx < n_used[0]
    first = used & ((idx == 0) | (block_e != jnp.roll(block_e, 1)))
    ordinal = jnp.cumsum(first.astype(jnp.int32)) - 1
    par = (ordinal % 2).astype(jnp.int32)
    first_pos = jnp.where(first, idx, nb)
    next_first = lax.cummin(jnp.concatenate([first_pos[1:], jnp.full((1,), nb, jnp.int32)]), reverse=True)
    nxt = jnp.where(next_first < nb, block_e[jnp.minimum(next_first, nb - 1)], -1).astype(jnp.int32)
    any_spec = pl.BlockSpec(memory_space=pl.ANY)
    assert nb % EXPERT_STEP_BLOCKS == 0
    step_rows = EXPERT_STEP_BLOCKS * MOE_BLOCK
    grid_spec = pltpu.PrefetchScalarGridSpec(
        num_scalar_prefetch=5,
        grid=(nb // EXPERT_STEP_BLOCKS,),
        in_specs=[pl.BlockSpec((step_rows, dp), lambda i, *_: (i, 0)), any_spec, any_spec, any_spec],
        out_specs=pl.BlockSpec((step_rows, dp), lambda i, *_: (i, 0)),
        scratch_shapes=[pltpu.VMEM((2, d, de), F32), pltpu.VMEM((2, d, de), F32), pltpu.VMEM((2, de, d), F32),
                        pltpu.VMEM((d, de), BF16), pltpu.VMEM((d, de), BF16), pltpu.VMEM((de, d), BF16),
                        pltpu.SemaphoreType.DMA((3, 2))],
    )
    return pl.pallas_call(
        _experts_kernel,
        grid_spec=grid_spec,
        out_shape=jax.ShapeDtypeStruct((nb * MOE_BLOCK, dp), xs.dtype),
        compiler_params=_cparams("arbitrary"),
        name="moe_experts",
    )(block_e, first.astype(jnp.int32), nxt, par, n_used, xs, w1_e, w3_e, w2_e)


SC_WINDOW = 128
SC_CORES, SC_SUBCORES = 2, 16
SC_WORKERS = SC_CORES * SC_SUBCORES


def _sc_worker_id():
    return lax.axis_index("c") * SC_SUBCORES + lax.axis_index("s")


def _sc_mesh():
    return plsc.VectorSubcoreMesh(core_axis_name="c", subcore_axis_name="s")


def _sc_dispatch(rows, dest0, dest1, pad_slots, n_rows):
    n, dv = rows.shape
    nwin, pwin = n // SC_WINDOW, pad_slots.shape[0] // SC_WINDOW
    assert n % (SC_WINDOW * SC_WORKERS) == 0 and pad_slots.shape[0] % (SC_WINDOW * SC_WORKERS) == 0
    zeros = jnp.zeros((SC_WINDOW, dv), rows.dtype)

    @pl.kernel(out_type=jax.ShapeDtypeStruct((n_rows, dv), rows.dtype), mesh=_sc_mesh(),
               scratch_types=[pltpu.VMEM((1, SC_WINDOW), jnp.int32), pltpu.VMEM((SC_WINDOW, dv), rows.dtype)],
               name="moe_dispatch_sc")
    def scatter(x_hbm, d0_hbm, d1_hbm, p_hbm, z_hbm, o_hbm, idx, buf):
        wid = _sc_worker_id()
        pltpu.sync_copy(z_hbm, buf)

        @pl.loop(0, pwin // SC_WORKERS)
        def _(t):
            w = t * SC_WORKERS + wid
            pltpu.sync_copy(p_hbm.at[pl.ds(w, 1)], idx)
            pltpu.sync_copy(buf, o_hbm.at[idx.at[0]])

        @pl.loop(0, nwin // SC_WORKERS)
        def _(t):
            w = t * SC_WORKERS + wid
            pltpu.sync_copy(x_hbm.at[pl.ds(w * SC_WINDOW, SC_WINDOW)], buf)
            for d_hbm in (d0_hbm, d1_hbm):
                pltpu.sync_copy(d_hbm.at[pl.ds(w, 1)], idx)
                pltpu.sync_copy(buf, o_hbm.at[idx.at[0]])

    return scatter(rows, dest0.reshape(nwin, SC_WINDOW), dest1.reshape(nwin, SC_WINDOW),
                   pad_slots.reshape(pwin, SC_WINDOW), zeros)


def _sc_gather(table, index):
    m = index.shape[0]
    dv = table.shape[1]
    nwin = m // SC_WINDOW
    assert m % (SC_WINDOW * SC_WORKERS) == 0

    @pl.kernel(out_type=jax.ShapeDtypeStruct((m, dv), table.dtype), mesh=_sc_mesh(),
               scratch_types=[pltpu.VMEM((1, SC_WINDOW), jnp.int32), pltpu.VMEM((SC_WINDOW, dv), table.dtype)],
               name="moe_gather_sc")
    def gather(x_hbm, i_hbm, o_hbm, idx, buf):
        wid = _sc_worker_id()

        @pl.loop(0, nwin // SC_WORKERS)
        def _(t):
            w = t * SC_WORKERS + wid
            pltpu.sync_copy(i_hbm.at[pl.ds(w, 1)], idx)
            pltpu.sync_copy(x_hbm.at[idx.at[0]], buf)
            pltpu.sync_copy(buf, o_hbm.at[pl.ds(w * SC_WINDOW, SC_WINDOW)])

    return gather(table, index.reshape(nwin, SC_WINDOW))


def _combine_planes_kernel(r_ref, ya_ref, yb_ref, x_ref, gate_ref, gf_ref, o_ref):
    rec = r_ref[...]
    y = (_unpack_bf16_pairs(ya_ref[...]).astype(F32) * rec[:, ROUTE_W1:ROUTE_W1 + 1]
         + _unpack_bf16_pairs(yb_ref[...]).astype(F32) * rec[:, ROUTE_W2:ROUTE_W2 + 1])
    x2 = x_ref[...] + gate_ref[...] * y
    o_ref[...] = x2 * lax.rsqrt(jnp.mean(x2 * x2, axis=-1, keepdims=True) + EPS) * gf_ref[...]


def _combine_planes(g, route, x1, gate2, g_final, tm=COMBINE_TM):
    bsz, L, d = x1.shape
    tpb = L // tm
    dp = g.shape[-1]
    return pl.pallas_call(
        _combine_planes_kernel,
        grid=(bsz, tpb),
        in_specs=[pl.BlockSpec((tm, LANES), lambda b, i: (b * tpb + i, 0)),
                  pl.BlockSpec((None, tm, dp), lambda b, i: (0, b * tpb + i, 0)),
                  pl.BlockSpec((None, tm, dp), lambda b, i: (1, b * tpb + i, 0)),
                  pl.BlockSpec((None, tm, d), lambda b, i: (b, i, 0)),
                  pl.BlockSpec((None, 1, d), lambda b, i: (b, 0, 0)),
                  pl.BlockSpec((1, d), lambda b, i: (0, 0))],
        out_specs=pl.BlockSpec((None, tm, d), lambda b, i: (b, i, 0)),
        out_shape=jax.ShapeDtypeStruct((bsz, L, d), F32),
        compiler_params=_cparams("parallel", "parallel"),
        name="moe_combine",
    )(route, g, g, x1, gate2, g_final.reshape(1, d))


def _moe(h2, x1, gate2, g_final, w_group, b_group, w_router, b_router, w1_e, w3_e, w2_e):
    bsz, L, d = x1.shape
    n = bsz * L
    h2f = h2.reshape(n, h2.shape[-1])
    route, tile_counts = _router(h2f, w_group, b_group, w_router, b_router)
    dest_rec, counts = _slots(route, tile_counts)
    nb = (2 * n) // MOE_BLOCK + N_EXPERTS
    cnt = counts[:N_EXPERTS].astype(jnp.int32)
    blocks_per_e = (cnt + MOE_BLOCK - 1) // MOE_BLOCK
    ends = jnp.cumsum(blocks_per_e)
    block_e = jnp.minimum(jnp.sum(ends[None, :] <= jnp.arange(nb, dtype=jnp.int32)[:, None], axis=1),
                          N_EXPERTS - 1).astype(jnp.int32)
    n_used = ends[-1:].astype(jnp.int32)
    n_slots = nb * MOE_BLOCK
    pad_j = jnp.arange(MOE_BLOCK, dtype=jnp.int32)[None, :]
    spare = n_slots + jnp.arange(N_EXPERTS * MOE_BLOCK, dtype=jnp.int32).reshape(N_EXPERTS, MOE_BLOCK)
    first_slot = ((ends - blocks_per_e) * MOE_BLOCK)[:, None]
    is_pad = cnt[:, None] + pad_j < blocks_per_e[:, None] * MOE_BLOCK
    pad_slots = jnp.where(is_pad, first_slot + cnt[:, None] + pad_j, spare).reshape(-1)
    xs = _sc_dispatch(h2f, dest_rec[:, 0], dest_rec[:, 1], pad_slots, n_slots + N_EXPERTS * MOE_BLOCK)
    ys = _experts(xs, nb, block_e, n_used, w1_e, w3_e, w2_e)
    g = _sc_gather(ys, jnp.concatenate([dest_rec[:, 0], dest_rec[:, 1]]))
    return _combine_planes(g.reshape(2, n, g.shape[-1]), route, x1, gate2, g_final)


def kernel(x, c, ctx, c_ctx, w_mod, b_mod, g_norm1, g_norm2, w_in, b_in, w_qk_conv, b_qk_conv,
           w_h_conv, b_h_conv, hf_w1, hf_b1, hf_w2, hf_b2, hf_w3, hf_freq, h_bias, w_a, w_b, w_out,
           w_group, b_group, w_router, b_router, w1_e, w3_e, w2_e, g_final):
    assert w_mod.shape[0] == 1, "single-layer block"
    (w_mod, b_mod, g_norm1, g_norm2, w_in, b_in, w_qk_conv, b_qk_conv, w_h_conv, b_h_conv, hf_w1, hf_b1, hf_w2,
     hf_b2, hf_w3, hf_freq, h_bias, w_a, w_b, w_out, w_group, b_group, w_router, b_router, w1_e, w3_e, w2_e) = (
        t[0] for t in (w_mod, b_mod, g_norm1, g_norm2, w_in, b_in, w_qk_conv, b_qk_conv, w_h_conv, b_h_conv,
                       hf_w1, hf_b1, hf_w2, hf_b2, hf_w3, hf_freq, h_bias, w_a, w_b, w_out, w_group, b_group,
                       w_router, b_router, w1_e, w3_e, w2_e))
    bsz, L, d = x.shape
    lc = ctx.shape[1]
    seg = L // (L // GRID_W)
    chunk_c = min(lc, MLSTM_CHUNK)
    assert bsz + 1 <= 8 and lc % chunk_c == 0 and L % MLSTM_CHUNK == 0

    cond = jnp.zeros((8, d), F32).at[:bsz].set(c).at[bsz].set(c_ctx)
    mod = _adaln(cond, w_mod, b_mod).reshape(8, 6, d)
    modx = mod[:bsz]
    shift1, scale1, gate1, shift2, scale2, gate2 = (modx[:, i:i + 1] for i in range(6))
    shift1c = jnp.broadcast_to(mod[bsz, 0].reshape(1, 1, d), (bsz, 1, d))
    scale1c = jnp.broadcast_to(mod[bsz, 1].reshape(1, 1, d), (bsz, 1, d))

    k_scale = jnp.full((M_WIDTH,), M_HEAD_DIM ** -0.5, F32)
    qk_scale = jnp.concatenate([jnp.ones((M_WIDTH,), F32), k_scale])
    w_gates, b_gates = w_in[:, IG0:M_COLS], b_in[IG0:M_COLS]
    w_main, w_hyena = _weight_prep(jnp.swapaxes(w_in, 0, 1))
    b_main = jnp.concatenate([b_in[Q0:IG0], b_in[GA0:IN_COLS]])

    hc = _norm_mod(ctx, g_norm1, shift1c, scale1c, lc)
    kc = _proj_conv_silu(hc, w_main[:, K0 - Q0:V0 - Q0], b_in[K0:V0], w_qk_conv[:, M_WIDTH:],
                         b_qk_conv[M_WIDTH:], k_scale, lc, lc)
    vc = _proj_act(hc, w_main[:, V0 - Q0:O0 - Q0], b_in[V0:O0], "none", BF16, lc)
    bcc, acc, arc = _gates(hc, w_gates, b_gates, chunk_c)
    zero_state = (jnp.zeros((bsz, 2, M_HEADS, M_HEAD_DIM, M_HEAD_DIM), F32),
                  jnp.zeros((bsz, 2, M_HEADS, 1, M_HEAD_DIM), F32),
                  jnp.zeros((bsz, 2, M_HEADS, 1, LANES), F32))
    _, ctx_state = _mlstm(None, (kc, 0), (vc, 0), bcc, acc, arc, zero_state, False, chunk_c)

    tm = ROW_TILE
    _, dft_fast = _dft_factors(2 * L)
    pm, h, h_il = _proj_main(x, g_norm1, shift1, scale1, w_main, b_main, w_qk_conv, b_qk_conv, qk_scale,
                             seg, tm, dft_fast)
    bc, ac, ar = _gates(h, w_gates, b_gates, MLSTM_CHUNK)
    hdirs, _ = _mlstm((pm, PM_Q), (pm, PM_K), (pm, PM_V), bc, ac, ar, ctx_state, True, MLSTM_CHUNK)

    x0_t, s_t = _proj_hyena(h_il, w_hyena, b_in[HY0:GA0], w_h_conv, b_h_conv, seg)
    hy = _hyena_long_conv(s_t, x0_t, h_bias, hf_w1, hf_b1, hf_w2, hf_b2, hf_w3, hf_freq)

    x1, h2 = _merge(hdirs, pm, hy, x, gate1, g_norm2, shift2, scale2,
                    w_a.astype(BF16), w_b.astype(BF16), w_out.astype(BF16))
    return _moe(h2, x1, gate2, g_final, w_group, b_group, w_router, b_router, w1_e, w3_e, w2_e)
```

```python
import functools
import math

import jax
import jax.numpy as jnp
import numpy as np
from jax import lax
from jax.experimental import pallas as pl
from jax.experimental.pallas import tpu as pltpu
from jax.experimental.pallas import tpu_sc as plsc

F32 = jnp.float32
BF16 = jnp.bfloat16

D_MODEL = 1024
GRID_W = 64
EPS = 1e-6
M_HEADS = 4
M_HEAD_DIM = 256
M_WIDTH = M_HEADS * M_HEAD_DIM
H_WIDTH = 1024
H_POS_BANDS = 16
H_FILTER_HIDDEN = 64
H_FAST_DECAY_PCT = 0.3
H_SLOW_DECAY_PCT = 1.5
H_DECAY_TARGET = 1e-2
N_GROUPS = 8
EXPERTS_PER_GROUP = 8
N_EXPERTS = N_GROUPS * EXPERTS_PER_GROUP
D_EXPERT = 512
Q0 = 0
K0 = Q0 + M_WIDTH
V0 = K0 + M_WIDTH
O0 = V0 + M_WIDTH
IG0 = O0 + M_WIDTH
FG0 = IG0 + 2 * M_HEADS
M_COLS = FG0 + 2 * M_HEADS
HY0 = M_COLS
GA0 = HY0 + 3 * H_WIDTH
GB0 = GA0 + D_MODEL
IN_COLS = GB0 + D_MODEL

LANES = 128
MLSTM_CHUNK = 512
NEG_BIG = -1e30
VMEM_LIMIT = 48 * 1024 * 1024
ROW_TILE = 1024
ADALN_TN = 1536
SMALL_TN = 512
MERGE_TM = 512
COMBINE_TM = 512


def _cparams(*sem):
    return pltpu.CompilerParams(dimension_semantics=sem, vmem_limit_bytes=VMEM_LIMIT)


def _adaln_kernel(c_ref, w_ref, b_ref, o_ref):
    s = c_ref[...]
    s = s * jax.nn.sigmoid(s)
    o_ref[...] = jnp.dot(s.astype(BF16), w_ref[...].astype(BF16), preferred_element_type=F32) + b_ref[...]


def _adaln(cond, w_mod, b_mod):
    n = w_mod.shape[1]
    tn = ADALN_TN
    return pl.pallas_call(
        _adaln_kernel,
        grid=(n // tn,),
        in_specs=[pl.BlockSpec((8, D_MODEL), lambda j: (0, 0)),
                  pl.BlockSpec((D_MODEL, tn), lambda j: (0, j)),
                  pl.BlockSpec((1, tn), lambda j: (0, j))],
        out_specs=pl.BlockSpec((8, tn), lambda j: (0, j)),
        out_shape=jax.ShapeDtypeStruct((8, n), F32),
        compiler_params=_cparams("arbitrary"),
        name="adaln",
    )(cond, w_mod, b_mod.reshape(1, n))


def _norm_mod_kernel(x_ref, g_ref, sh_ref, sc_ref, o_ref):
    x = x_ref[...]
    y = x * lax.rsqrt(jnp.mean(x * x, axis=-1, keepdims=True) + EPS)
    y = y * g_ref[...]
    o_ref[...] = (y * (1.0 + sc_ref[...]) + sh_ref[...]).astype(o_ref.dtype)


def _norm_mod(x, g, shift, scale, tm):
    bsz, L, d = x.shape
    return pl.pallas_call(
        _norm_mod_kernel,
        grid=(bsz, L // tm),
        in_specs=[pl.BlockSpec((None, tm, d), lambda b, i: (b, i, 0)),
                  pl.BlockSpec((1, d), lambda b, i: (0, 0)),
                  pl.BlockSpec((None, 1, d), lambda b, i: (b, 0, 0)),
                  pl.BlockSpec((None, 1, d), lambda b, i: (b, 0, 0))],
        out_specs=pl.BlockSpec((None, tm, d), lambda b, i: (b, i, 0)),
        out_shape=jax.ShapeDtypeStruct((bsz, L, d), BF16),
        compiler_params=_cparams("parallel", "parallel"),
        name="norm_mod",
    )(x, g.reshape(1, d), shift, scale)


def _conv3(z, wc, bc, seg):
    tm = z.shape[0]
    pos = lax.broadcasted_iota(jnp.int32, z.shape, 0) & (seg - 1)
    zp = jnp.where(pos == 0, 0.0, pltpu.roll(z, 1, 0))
    zn = jnp.where(pos == seg - 1, 0.0, pltpu.roll(z, tm - 1, 0))
    return zp * wc[0:1, :] + z * wc[1:2, :] + zn * wc[2:3, :] + bc


def _proj_act_kernel(h_ref, w_ref, b_ref, o_ref, *, act):
    z = jnp.dot(h_ref[...], w_ref[...], preferred_element_type=F32) + b_ref[...]
    if act == "sigmoid":
        z = jax.nn.sigmoid(z)
    o_ref[...] = z.astype(o_ref.dtype)


def _proj_act(h, w, b, act, out_dtype, tm, tn=SMALL_TN):
    bsz, L, d = h.shape
    n = w.shape[1]
    return pl.pallas_call(
        functools.partial(_proj_act_kernel, act=act),
        grid=(bsz, L // tm, n // tn),
        in_specs=[pl.BlockSpec((None, tm, d), lambda b_, i, j: (b_, i, 0)),
                  pl.BlockSpec((d, tn), lambda b_, i, j: (0, j)),
                  pl.BlockSpec((1, tn), lambda b_, i, j: (0, j))],
        out_specs=pl.BlockSpec((None, tm, tn), lambda b_, i, j: (b_, i, j)),
        out_shape=jax.ShapeDtypeStruct((bsz, L, n), out_dtype),
        compiler_params=_cparams("parallel", "parallel", "arbitrary"),
        name="proj_" + act,
    )(h, w, b.reshape(1, n))


def _proj_conv_silu_kernel(h_ref, w_ref, b_ref, wc_ref, bc_ref, cs_ref, o_ref, *, seg):
    z = jnp.dot(h_ref[...], w_ref[...], preferred_element_type=F32) + b_ref[...]
    y = _conv3(z, wc_ref[...], bc_ref[...], seg)
    y = y * jax.nn.sigmoid(y)
    o_ref[...] = (y * cs_ref[...]).astype(o_ref.dtype)


def _proj_conv_silu(h, w, b, wc, bc, colscale, seg, tm, tn=SMALL_TN):
    bsz, L, d = h.shape
    n = w.shape[1]
    col = lambda b_, i, j: (0, j)
    return pl.pallas_call(
        functools.partial(_proj_conv_silu_kernel, seg=seg),
        grid=(bsz, L // tm, n // tn),
        in_specs=[pl.BlockSpec((None, tm, d), lambda b_, i, j: (b_, i, 0)),
                  pl.BlockSpec((d, tn), col),
                  pl.BlockSpec((1, tn), col),
                  pl.BlockSpec((3, tn), col),
                  pl.BlockSpec((1, tn), col),
                  pl.BlockSpec((1, tn), col)],
        out_specs=pl.BlockSpec((None, tm, tn), lambda b_, i, j: (b_, i, j)),
        out_shape=jax.ShapeDtypeStruct((bsz, L, n), BF16),
        compiler_params=_cparams("parallel", "parallel", "arbitrary"),
        name="proj_conv_silu",
    )(h, w, b.reshape(1, n), wc, bc.reshape(1, n), colscale.reshape(1, n))


PROJ_TN = 1024
MAIN_TILE_COLS = (Q0, K0, V0, O0, GA0, GB0)
HYENA_TILE_COLS = tuple(range(HY0, GA0, PROJ_TN))


def _weight_prep_kernel(start_ref, wt_hbm, main_ref, hy_ref, buf, sem):
    t = pl.program_id(0)
    n_main = len(MAIN_TILE_COLS)

    def tile_copy(step, slot):
        rows = pl.ds(pl.multiple_of(start_ref[step], 8), PROJ_TN)
        return pltpu.make_async_copy(wt_hbm.at[rows], buf.at[slot], sem.at[slot])

    @pl.when(t == 0)
    def _():
        tile_copy(0, 0).start()

    @pl.when(t + 1 < pl.num_programs(0))
    def _():
        tile_copy(t + 1, (t + 1) % 2).start()

    slot = t % 2
    tile_copy(t, slot).wait()
    tile = buf[slot].T.astype(BF16)

    @pl.when(t < n_main)
    def _():
        main_ref[...] = tile

    @pl.when(t >= n_main)
    def _():
        hy_ref[...] = tile


def _weight_prep(w_t):
    cols, d = w_t.shape
    starts = MAIN_TILE_COLS + HYENA_TILE_COLS
    n_main, n_hy = len(MAIN_TILE_COLS), len(HYENA_TILE_COLS)
    assert cols == IN_COLS and d % LANES == 0 and all(s % 8 == 0 and s + PROJ_TN <= cols for s in starts)
    grid_spec = pltpu.PrefetchScalarGridSpec(
        num_scalar_prefetch=1,
        grid=(len(starts),),
        in_specs=[pl.BlockSpec(memory_space=pl.ANY)],
        out_specs=[pl.BlockSpec((d, PROJ_TN), lambda t, *_: (0, jnp.minimum(t, n_main - 1))),
                   pl.BlockSpec((d, PROJ_TN), lambda t, *_: (0, jnp.maximum(t - n_main, 0)))],
        scratch_shapes=[pltpu.VMEM((2, PROJ_TN, d), F32), pltpu.SemaphoreType.DMA((2,))],
    )
    return pl.pallas_call(
        _weight_prep_kernel,
        grid_spec=grid_spec,
        out_shape=[jax.ShapeDtypeStruct((d, n_main * PROJ_TN), BF16), jax.ShapeDtypeStruct((d, n_hy * PROJ_TN), BF16)],
        compiler_params=_cparams("arbitrary"),
        name="weight_prep",
    )(jnp.asarray(starts, jnp.int32), w_t)


PROJ_SUB = 512
PM_Q, PM_K, PM_V, PM_O, PM_GA, PM_GB = range(6)


def _proj_main_kernel(x_ref, g_ref, sh_ref, sc_ref, w_ref, b_ref, wc_ref, bc_ref, cs_ref,
                      o_ref, h_ref, hi_hbm, hp_sc, sem, *, seg):
    b, i, j = pl.program_id(0), pl.program_id(1), pl.program_id(2)
    n2, jt = hi_hbm.shape[2], hi_hbm.shape[3]

    def interleave_copy(jj):
        return pltpu.make_async_copy(hp_sc.at[pl.ds(jj * n2, n2)], hi_hbm.at[b, i, :, jj, :], sem)

    @pl.when(j == 0)
    def _():
        x = x_ref[...]
        y = x * lax.rsqrt(jnp.mean(x * x, axis=-1, keepdims=True) + EPS) * g_ref[...]
        y = y * (1.0 + sc_ref[...]) + sh_ref[...]
        h_ref[...] = y.astype(h_ref.dtype)
        hp_sc[...] = _pack_bf16_pairs(y)
        for jj in range(jt):
            interleave_copy(jj).start()

    @pl.when(j == pl.num_programs(2) - 1)
    def _():
        for jj in range(jt):
            interleave_copy(jj).wait()

    def run(epilogue):
        for c in range(PROJ_TN // PROJ_SUB):
            sl = slice(c * PROJ_SUB, (c + 1) * PROJ_SUB)
            z = jnp.dot(h_ref[...], w_ref[:, sl], preferred_element_type=F32) + b_ref[:, sl]
            o_ref[:, sl] = epilogue(z, sl).astype(o_ref.dtype)

    def conv_silu(z, sl):
        y = _conv3(z, wc_ref[:, sl], bc_ref[:, sl], seg)
        return (y * jax.nn.sigmoid(y)) * cs_ref[:, sl]

    @pl.when(j <= PM_K)
    def _():
        run(conv_silu)

    @pl.when(j == PM_V)
    def _():
        run(lambda z, sl: z)

    @pl.when(j >= PM_O)
    def _():
        run(lambda z, sl: jax.nn.sigmoid(z))


def _proj_main(x, g, shift, scale, w, b, wc, bc, colscale, seg, tm, n2):
    bsz, L, d = x.shape
    n = w.shape[1]
    jt = tm // n2
    qk = lambda b_, i, j: (0, jnp.minimum(j, PM_K))
    row = pl.BlockSpec((None, tm, d), lambda b_, i, j: (b_, i, 0))
    bvec = pl.BlockSpec((None, 1, d), lambda b_, i, j: (b_, 0, 0))
    return pl.pallas_call(
        functools.partial(_proj_main_kernel, seg=seg),
        grid=(bsz, L // tm, n // PROJ_TN),
        in_specs=[row, pl.BlockSpec((1, d), lambda b_, i, j: (0, 0)), bvec, bvec,
                  pl.BlockSpec((d, PROJ_TN), lambda b_, i, j: (0, j)),
                  pl.BlockSpec((1, PROJ_TN), lambda b_, i, j: (0, j)),
                  pl.BlockSpec((3, PROJ_TN), qk),
                  pl.BlockSpec((1, PROJ_TN), qk),
                  pl.BlockSpec((1, PROJ_TN), qk)],
        out_specs=[pl.BlockSpec((None, tm, PROJ_TN), lambda b_, i, j: (b_, i, j)), row,
                   pl.BlockSpec(memory_space=pl.ANY)],
        out_shape=[jax.ShapeDtypeStruct((bsz, L, n), BF16), jax.ShapeDtypeStruct((bsz, L, d), BF16),
                   jax.ShapeDtypeStruct((bsz, L // tm, n2, jt, d // 2), jnp.uint32)],
        scratch_shapes=[pltpu.VMEM((tm, d // 2), jnp.uint32), pltpu.SemaphoreType.DMA(())],
        compiler_params=_cparams("parallel", "parallel", "arbitrary"),
        name="proj_main",
    )(x, g.reshape(1, d), shift, scale, w, b.reshape(1, n), wc, bc.reshape(1, -1), colscale.reshape(1, -1))


def _conv3_interleaved(z, wc, bc, seg, jt):
    grp = seg * jt
    pad = jnp.zeros((jt, z.shape[1]), z.dtype)
    prev, nxt = [], []
    for g0 in range(0, z.shape[0], grp):
        zg = z[g0:g0 + grp]
        prev += [pad, zg[:grp - jt]]
        nxt += [zg[jt:], pad]
    zp = jnp.concatenate(prev, axis=0)
    zn = jnp.concatenate(nxt, axis=0)
    return zp * wc[0:1, :] + z * wc[1:2, :] + zn * wc[2:3, :] + bc


def _proj_hyena_kernel(h_ref, w0_ref, w1_ref, w2_ref, b_ref, wc_ref, bc_ref, x0_ref, s_ref, *, seg):
    n2, jt = s_ref.shape[0], s_ref.shape[1]
    h = _unpack_bf16_pairs(h_ref[...].reshape(n2 * jt, h_ref.shape[2]))
    us = []
    for g, w_ref in enumerate((w0_ref, w1_ref, w2_ref)):
        z = jnp.dot(h, w_ref[...], preferred_element_type=F32) + b_ref[g]
        us.append(_conv3_interleaved(z, wc_ref[g], bc_ref[g], seg, jt))
    x0_ref[...] = _pack_bf16_pairs(us[0]).reshape(x0_ref.shape)
    s_ref[...] = (us[1] * us[2]).reshape(s_ref.shape)


def _proj_hyena(hi, w, b, wc, bc, seg):
    bsz, nt, n2, jt, dp = hi.shape
    d, tm = 2 * dp, n2 * jt
    L = nt * tm
    tn = DFT_C_TILE
    nblk = H_WIDTH // tn
    assert n2 % seg == 0 and (jt % 8 == 0 or nt == 1)
    b3 = b.reshape(3, 1, H_WIDTH)
    wc3 = wc.reshape(3, 3, H_WIDTH).transpose(1, 0, 2)
    bc3 = bc.reshape(3, 1, H_WIDTH)
    return pl.pallas_call(
        functools.partial(_proj_hyena_kernel, seg=seg),
        grid=(bsz, nt, nblk),
        in_specs=[pl.BlockSpec((None, None, n2, jt, dp), lambda b_, i, j: (b_, i, 0, 0, 0)),
                  pl.BlockSpec((d, tn), lambda b_, i, j: (0, j)),
                  pl.BlockSpec((d, tn), lambda b_, i, j: (0, nblk + j)),
                  pl.BlockSpec((d, tn), lambda b_, i, j: (0, 2 * nblk + j)),
                  pl.BlockSpec((3, 1, tn), lambda b_, i, j: (0, 0, j)),
                  pl.BlockSpec((3, 3, tn), lambda b_, i, j: (0, 0, j)),
                  pl.BlockSpec((3, 1, tn), lambda b_, i, j: (0, 0, j))],
        out_specs=[pl.BlockSpec((None, n2, jt, tn // 2), lambda b_, i, j: (b_, 0, i, j)),
                   pl.BlockSpec((None, n2, jt, tn), lambda b_, i, j: (b_, 0, i, j))],
        out_shape=[jax.ShapeDtypeStruct((bsz, n2, L // n2, H_WIDTH // 2), jnp.uint32),
                   jax.ShapeDtypeStruct((bsz, n2, L // n2, H_WIDTH), F32)],
        compiler_params=_cparams("parallel", "parallel", "arbitrary"),
        name="proj_hyena",
    )(hi, w, w, w, b3, wc3, bc3)


N_GATES = 4 * M_HEADS


def _split3(x):
    hi = x.astype(BF16)
    r1 = x - hi.astype(F32)
    mid = r1.astype(BF16)
    lo = (r1 - mid.astype(F32)).astype(BF16)
    return hi, mid, lo


def _log_sigmoid(x):
    return jnp.minimum(x, 0.0) - jnp.log1p(jnp.exp(-jnp.abs(x)))


def _gates_kernel(h_ref, w_ref, wt_ref, b_ref, bt_ref, bc_ref, ac_ref, ar_ref):
    h = h_ref[...]
    t = h.shape[0]
    z = jnp.dot(h, w_ref[...], preferred_element_type=F32) + b_ref[...]
    zt = lax.dot_general(wt_ref[...], h, (((1,), (1,)), ((), ())),
                         preferred_element_type=F32) + bt_ref[...]
    r = lax.broadcasted_iota(jnp.int32, (t, t), 0)
    c = lax.broadcasted_iota(jnp.int32, (t, t), 1)
    lower = (r >= c).astype(BF16)
    upper = (r <= c).astype(BF16)
    g8 = FG_LANE0

    lf = _log_sigmoid(z)
    lane = lax.broadcasted_iota(jnp.int32, z.shape, 1)
    is_fg = (lane >= g8) & (lane < 2 * g8)
    terms = [jnp.where(is_fg, p.astype(F32), 0.0) for p in _split3(lf)]
    packed = terms[0] + pltpu.roll(terms[1], 2 * g8, 1) + pltpu.roll(terms[2], 4 * g8, 1)
    cfp = jnp.dot(lower, packed.astype(BF16), preferred_element_type=F32)
    cf = cfp + pltpu.roll(cfp, LANES - 2 * g8, 1) + pltpu.roll(cfp, LANES - 4 * g8, 1)
    cb = cf[t - 1:t, :] - cf + lf
    bc = jnp.where(lane < g8 + M_HEADS, cf, cb)
    bc = pltpu.roll(bc, LANES - g8, 1)
    bc_ref[...] = bc
    ac_ref[...] = z - bc

    lft = _log_sigmoid(zt[g8:, :])
    stacked = jnp.concatenate([p.astype(F32) for p in _split3(lft)] + [jnp.zeros_like(lft)], axis=0)
    cft3 = jnp.dot(stacked.astype(BF16), upper, preferred_element_type=F32)
    cft = cft3[0:g8] + cft3[g8:2 * g8] + cft3[2 * g8:3 * g8]
    cbt = cft[:, t - 1:t] - cft + lft
    row = lax.broadcasted_iota(jnp.int32, cft.shape, 0)
    ar_ref[...] = zt[:g8, :] - jnp.where(row < M_HEADS, cft, cbt)


FG_LANE0 = 2 * M_HEADS


def _gates(h, w_g, b_g, chunk):
    bsz, L, d = h.shape
    w_pad = jnp.zeros((d, LANES), F32).at[:, :N_GATES].set(w_g).astype(BF16)
    b_pad = jnp.zeros((1, LANES), F32).at[0, :N_GATES].set(b_g)
    wt = w_g.T.astype(BF16)
    bt = b_g.reshape(N_GATES, 1)
    tok = pl.BlockSpec((None, chunk, LANES), lambda b_, i: (b_, i, 0))
    return pl.pallas_call(
        _gates_kernel,
        grid=(bsz, L // chunk),
        in_specs=[pl.BlockSpec((None, chunk, d), lambda b_, i: (b_, i, 0)),
                  pl.BlockSpec((d, LANES), lambda b_, i: (0, 0)),
                  pl.BlockSpec((N_GATES, d), lambda b_, i: (0, 0)),
                  pl.BlockSpec((1, LANES), lambda b_, i: (0, 0)),
                  pl.BlockSpec((N_GATES, 1), lambda b_, i: (0, 0))],
        out_specs=[tok, tok, pl.BlockSpec((None, FG_LANE0, chunk), lambda b_, i: (b_, 0, i))],
        out_shape=[jax.ShapeDtypeStruct((bsz, L, LANES), F32),
                   jax.ShapeDtypeStruct((bsz, L, LANES), F32),
                   jax.ShapeDtypeStruct((bsz, FG_LANE0, L), F32)],
        compiler_params=_cparams("parallel", "parallel"),
        name="mlstm_gates",
    )(h, w_pad, wt, b_pad, bt)


def _mlstm_kernel(*refs, emit_h, n_chunks):
    if emit_h:
        (q_ref, k_ref, v_ref, bc_ref, ac_ref, ar_ref, c0_ref, n0_ref, m0_ref,
         h_ref, cf_ref, nf_ref, mf_ref, c_sc, n_sc, m_sc) = refs
    else:
        (k_ref, v_ref, bc_ref, ac_ref, ar_ref, c0_ref, n0_ref, m0_ref,
         cf_ref, nf_ref, mf_ref, c_sc, n_sc, m_sc) = refs
    d = pl.program_id(1)
    j = pl.program_id(2)
    fwd = d == 0
    t = k_ref.shape[0]
    dh = M_HEAD_DIM

    @pl.when(j == 0)
    def _():
        c_sc[...] = c0_ref[...]
        n_sc[...] = n0_ref[...]
        m_sc[...] = m0_ref[...]

    r = lax.broadcasted_iota(jnp.int32, (t, t), 0)
    c = lax.broadcasted_iota(jnp.int32, (t, t), 1)
    causal = jnp.where(fwd, r - c, c - r) >= 0
    bc_all = bc_ref[...]
    ac_all = ac_ref[...]
    ar_all = ar_ref[...]
    for hd in range(M_HEADS):
        sl = slice(hd * dh, (hd + 1) * dh)
        bc = jnp.where(fwd, bc_all[:, hd:hd + 1], bc_all[:, M_HEADS + hd:M_HEADS + hd + 1])
        ac = jnp.where(fwd, ac_all[:, hd:hd + 1], ac_all[:, M_HEADS + hd:M_HEADS + hd + 1])
        ar = jnp.where(fwd, ar_all[hd:hd + 1, :], ar_all[M_HEADS + hd:M_HEADS + hd + 1, :])
        b_tot = jnp.where(fwd, bc[t - 1:t, :], bc[0:1, :])
        m_prev = m_sc[hd][:, 0:1]
        k_h = k_ref[:, sl]
        v_h = v_ref[:, sl]
        if emit_h:
            q_h = q_ref[:, sl]
            dm = jnp.where(causal, bc + ar, NEG_BIG)
            inter = bc + m_prev
            m_t = jnp.maximum(inter, jnp.max(dm, axis=1, keepdims=True))
            qk = lax.dot_general(q_h, k_h, (((1,), (1,)), ((), ())), preferred_element_type=F32)
            s = qk * jnp.exp(dm - m_t)
            carry = jnp.exp(inter - m_t)
            num = (jnp.dot(s.astype(BF16), v_h, preferred_element_type=F32)
                   + carry * jnp.dot(q_h, c_sc[hd].astype(BF16), preferred_element_type=F32))
            den = (jnp.sum(s, axis=1, keepdims=True)
                   + carry * jnp.sum(q_h.astype(F32) * n_sc[hd], axis=1, keepdims=True))
            h_ref[:, sl] = (num / jnp.maximum(jnp.abs(den), jnp.exp(-m_t))).astype(h_ref.dtype)
        g = b_tot + ac
        m_new = jnp.maximum(b_tot + m_prev, jnp.max(g, axis=0, keepdims=True))
        wgt = jnp.exp(g - m_new)
        decay = jnp.exp(b_tot + m_prev - m_new)
        kw = k_h.astype(F32) * wgt
        c_sc[hd] = decay * c_sc[hd] + lax.dot_general(kw.astype(BF16), v_h, (((0,), (0,)), ((), ())),
                                                      preferred_element_type=F32)
        n_sc[hd] = decay * n_sc[hd] + jnp.sum(kw, axis=0, keepdims=True)
        m_sc[hd] = jnp.broadcast_to(m_new, (1, LANES))

    @pl.when(j == n_chunks - 1)
    def _():
        cf_ref[...] = c_sc[...]
        nf_ref[...] = n_sc[...]
        mf_ref[...] = m_sc[...]


def _mlstm(q, k, v, bc, ac, ar, state, emit_h, t):
    bsz, L, _ = k[0].shape
    nc = L // t
    seq = lambda b_, d, j: (b_, j + d * (nc - 1 - 2 * j), 0)
    st = lambda b_, d, j: (b_, d, 0, 0, 0)

    def tok(col):
        return pl.BlockSpec((None, t, M_WIDTH), lambda b_, d, j: (b_, j + d * (nc - 1 - 2 * j), col))

    gate_spec = pl.BlockSpec((None, t, LANES), seq)
    ar_spec = pl.BlockSpec((None, FG_LANE0, t), lambda b_, d, j: (b_, 0, j + d * (nc - 1 - 2 * j)))
    c_spec = pl.BlockSpec((None, None, M_HEADS, M_HEAD_DIM, M_HEAD_DIM), st)
    n_spec = pl.BlockSpec((None, None, M_HEADS, 1, M_HEAD_DIM), st)
    m_spec = pl.BlockSpec((None, None, M_HEADS, 1, LANES), st)
    state_shapes = [jax.ShapeDtypeStruct((bsz, 2, M_HEADS, M_HEAD_DIM, M_HEAD_DIM), F32),
                    jax.ShapeDtypeStruct((bsz, 2, M_HEADS, 1, M_HEAD_DIM), F32),
                    jax.ShapeDtypeStruct((bsz, 2, M_HEADS, 1, LANES), F32)]
    in_specs = [tok(k[1]), tok(v[1]), gate_spec, gate_spec, ar_spec, c_spec, n_spec, m_spec]
    args = [k[0], v[0], bc, ac, ar, *state]
    out_specs = [c_spec, n_spec, m_spec]
    out_shape = list(state_shapes)
    if emit_h:
        in_specs = [tok(q[1])] + in_specs
        args = [q[0]] + args
        out_specs = [pl.BlockSpec((None, None, t, M_WIDTH),
                                  lambda b_, d, j: (d, b_, j + d * (nc - 1 - 2 * j), 0))] + out_specs
        out_shape = [jax.ShapeDtypeStruct((2, bsz, L, M_WIDTH), BF16)] + out_shape
    outs = pl.pallas_call(
        functools.partial(_mlstm_kernel, emit_h=emit_h, n_chunks=nc),
        grid=(bsz, 2, nc),
        in_specs=in_specs,
        out_specs=out_specs,
        out_shape=out_shape,
        scratch_shapes=[pltpu.VMEM((M_HEADS, M_HEAD_DIM, M_HEAD_DIM), F32),
                        pltpu.VMEM((M_HEADS, 1, M_HEAD_DIM), F32),
                        pltpu.VMEM((M_HEADS, 1, LANES), F32)],
        compiler_params=_cparams("parallel", "parallel", "arbitrary"),
        name="mlstm" if emit_h else "mlstm_state",
    )(*args)
    if emit_h:
        return outs[0], tuple(outs[1:])
    return None, tuple(outs)


DFT_M_TILE = 8
DFT_C_TILE = 1024
DFT_INNER_C_TILE = 512
FEAT_ROWS = 16


def _filter_outer_kernel(bands_ref, w1t_ref, b1_ref, w2t_ref, b2_ref, w3p_ref, w3f_ref, fr_ref, dl_ref, l_ref,
                         a_ref, ss_ref, *, L, n1, n2):
    i = pl.program_id(0)
    h = n1 // 2
    cols = DFT_M_TILE * h

    def positions(shape, axis, side):
        q = lax.broadcasted_iota(jnp.int32, shape, axis)
        mm, jj = q // h, q % h
        n = n2 * (jj + side * h) + i * DFT_M_TILE + mm
        return n, jnp.where(n < L, n, 2 * L - n).astype(F32)

    taps = []
    sumsq = jnp.zeros((1, a_ref.shape[-1]), F32)
    for side, w3_ref in ((0, w3p_ref), (1, w3f_ref)):
        _, p_row = positions((1, cols), 1, side)
        t_row = p_row / float(max(L - 1, 1))
        ang = ((2 * math.pi / L) * p_row) * bands_ref[...]
        row = lax.broadcasted_iota(jnp.int32, (FEAT_ROWS, cols), 0)
        feats = jnp.concatenate([jnp.where(row == 0, t_row, 0.0), jnp.cos(ang), -jnp.sin(ang)], axis=0)
        fr = fr_ref[...]
        hid = jnp.sin(fr * (jnp.dot(w1t_ref[...], feats.astype(BF16), preferred_element_type=F32) + b1_ref[...]))
        hid = jnp.sin(fr * (jnp.dot(w2t_ref[...], hid.astype(BF16), preferred_element_type=F32) + b2_ref[...]))
        filt = lax.dot_general(hid.astype(BF16), w3_ref[...], (((0,), (0,)), ((), ())),
                               preferred_element_type=F32)
        n_col, p_col = positions((cols, 1), 0, side)
        t_col = p_col / float(max(L - 1, 1))
        kern = filt * jnp.exp(-t_col * jnp.abs(dl_ref[...]))
        kern = jnp.where(n_col == L, 0.0, kern)
        sumsq = sumsq + jnp.sum(kern * kern, axis=0, keepdims=True)
        taps.append(kern)

    for mm in range(DFT_M_TILE):
        x = jnp.concatenate([taps[0][mm * h:(mm + 1) * h], taps[1][mm * h:(mm + 1) * h]], axis=0)
        out = jnp.dot(l_ref[...], x.astype(BF16), preferred_element_type=F32)
        a_ref[0, :, mm, :] = out[:n1]
        a_ref[1, :, mm, :] = out[n1:]

    @pl.when(i == 0)
    def _():
        ss_ref[...] = jnp.zeros_like(ss_ref)

    ss_ref[...] += sumsq


def _filter_outer(L, n1, n2, fwd_r, w1, b1, w2, b2, w3, freq):
    hid = H_FILTER_HIDDEN
    bands = jnp.linspace(1e-4, H_POS_BANDS - 1, H_POS_BANDS, dtype=F32).reshape(H_POS_BANDS, 1)
    w1t = jnp.zeros((hid, 3 * FEAT_ROWS), F32)
    w1t = w1t.at[:, 0].set(w1[0]).at[:, FEAT_ROWS:2 * FEAT_ROWS].set(w1[1:1 + H_POS_BANDS].T)
    w1t = w1t.at[:, 2 * FEAT_ROWS:].set(w1[1 + H_POS_BANDS:].T).astype(BF16)
    w3h = w3.astype(BF16)
    max_decay = math.log(H_DECAY_TARGET) / H_FAST_DECAY_PCT
    min_decay = math.log(H_DECAY_TARGET) / H_SLOW_DECAY_PCT
    deltas = jnp.linspace(min_decay, max_decay, H_WIDTH, dtype=F32).reshape(1, H_WIDTH)
    col = lambda v: v.reshape(hid, 1)
    full = lambda a: pl.BlockSpec(a.shape, lambda i: (0,) * a.ndim)
    args = [bands, w1t, col(b1), w2.T.astype(BF16), col(b2)]
    return pl.pallas_call(
        functools.partial(_filter_outer_kernel, L=L, n1=n1, n2=n2),
        grid=(n2 // DFT_M_TILE,),
        in_specs=[full(a) for a in args]
        + [pl.BlockSpec((hid, H_WIDTH), lambda i: (0, 0)), pl.BlockSpec((hid, H_WIDTH), lambda i: (0, 1)),
           full(col(freq)), full(deltas), full(fwd_r)],
        out_specs=[pl.BlockSpec((2, n1, DFT_M_TILE, H_WIDTH), lambda i: (0, 0, i, 0)),
                   pl.BlockSpec((1, H_WIDTH), lambda i: (0, 0))],
        out_shape=[jax.ShapeDtypeStruct((2, n1, n2, H_WIDTH), F32),
                   jax.ShapeDtypeStruct((1, H_WIDTH), F32)],
        compiler_params=_cparams("arbitrary"),
        name="hyena_filter_outer",
    )(*args, w3h, w3h, col(freq), deltas, fwd_r)


def _dft_factors(n):
    lg = int(round(math.log2(n)))
    n1 = 1 << ((lg + 1) // 2)
    return n1, n // n1


def _dft_outer_matrices(n1):
    k = np.arange(n1)[:, None]
    n = np.arange(n1)[None, :]
    ang = 2.0 * np.pi * ((k * n) % n1) / n1
    cr, ci = np.cos(ang), -np.sin(ang)
    h = n1 // 2
    fwd_c = np.block([[cr[:, :h], -ci[:, :h]], [ci[:, :h], cr[:, :h]]])
    fwd_r = np.concatenate([cr, ci], axis=0)
    ir, ii = cr[:h, :], -ci[:h, :]
    inv = np.block([[ir, -ii], [ii, ir]])
    return (jnp.asarray(fwd_c, F32).astype(BF16), jnp.asarray(fwd_r, F32).astype(BF16),
            jnp.asarray(inv, F32).astype(BF16))


def _dft_inner_matrices(n1, n2):
    n = n1 * n2
    k2 = np.arange(n2)[:, None]
    m = np.arange(n2)[None, :]
    ang = 2.0 * np.pi * ((k2 * m) % n2) / n2
    fr, fi = np.cos(ang), -np.sin(ang)
    f = np.block([[fr, -fi], [fi, fr]])
    k1 = jnp.arange(n1, dtype=jnp.int32)[:, None]
    tw_ang = ((jnp.arange(n2, dtype=jnp.int32)[None, :] * k1) % n).astype(F32) * (2.0 * math.pi / n)
    rep = lambda t: jnp.broadcast_to(t[:, :, None], (n1, n2, LANES))
    return (jnp.asarray(f, F32).astype(BF16), jnp.asarray(f.T, F32).astype(BF16),
            rep(jnp.cos(tw_ang)), rep(-jnp.sin(tw_ang)))


def _outer_fwd_kernel(l_ref, s_ref, a_ref):
    n1 = a_ref.shape[1]
    for mm in range(s_ref.shape[1]):
        x = jnp.concatenate([s_ref[0, mm], s_ref[1, mm]], axis=0).astype(BF16)
        out = jnp.dot(l_ref[...], x, preferred_element_type=F32)
        a_ref[0, :, mm, :] = out[:n1]
        a_ref[1, :, mm, :] = out[n1:]


def _outer_fwd(lmat, s_t):
    _, n2, n1h, c = s_t.shape
    n1 = 2 * n1h
    tc = min(DFT_C_TILE, c)
    return pl.pallas_call(
        _outer_fwd_kernel,
        grid=(n2 // DFT_M_TILE, c // tc),
        in_specs=[pl.BlockSpec(lmat.shape, lambda m, j: (0, 0)),
                  pl.BlockSpec((2, DFT_M_TILE, n1h, tc), lambda m, j: (0, m, 0, j))],
        out_specs=pl.BlockSpec((2, n1, DFT_M_TILE, tc), lambda m, j: (0, 0, m, j)),
        out_shape=jax.ShapeDtypeStruct((2, n1, n2, c), F32),
        compiler_params=_cparams("parallel", "parallel"),
        name="dft_outer_fwd",
    )(lmat, s_t)


def _outer_inv_kernel(l_ref, b_ref, s_ref, x0_ref, ysc_ref, hb_ref, o_ref):
    n1h = s_ref.shape[2]
    for mm in range(b_ref.shape[1]):
        y = jnp.concatenate([b_ref[0, mm], b_ref[1, mm]], axis=0).astype(BF16)
        out = jnp.dot(l_ref[...], y, preferred_element_type=F32)
        for b in range(2):
            conv = out[b * n1h:(b + 1) * n1h]
            x0 = _unpack_bf16_pairs(x0_ref[b, mm]).astype(F32)
            hy = x0 * (conv * ysc_ref[...] + hb_ref[...] * s_ref[b, mm])
            o_ref[b, :, mm, :] = _pack_bf16_pairs(hy)


def _outer_inv(lmat, b_t, s_t, x0_t, yscale, h_bias):
    _, n2, n1, c = b_t.shape
    n1h = n1 // 2
    tc = min(DFT_C_TILE, c)
    vec = pl.BlockSpec((1, tc), lambda m, j: (0, j))
    hy = pl.pallas_call(
        _outer_inv_kernel,
        grid=(n2 // DFT_M_TILE, c // tc),
        in_specs=[pl.BlockSpec(lmat.shape, lambda m, j: (0, 0)),
                  pl.BlockSpec((2, DFT_M_TILE, n1, tc), lambda m, j: (0, m, 0, j)),
                  pl.BlockSpec((2, DFT_M_TILE, n1h, tc), lambda m, j: (0, m, 0, j)),
                  pl.BlockSpec((2, DFT_M_TILE, n1h, tc // 2), lambda m, j: (0, m, 0, j)),
                  vec, vec],
        out_specs=pl.BlockSpec((2, n1h, DFT_M_TILE, tc // 2), lambda m, j: (0, 0, m, j)),
        out_shape=jax.ShapeDtypeStruct((2, n1h, n2, c // 2), jnp.uint32),
        compiler_params=_cparams("parallel", "parallel"),
        name="dft_outer_inv",
    )(lmat, b_t, s_t, x0_t, yscale, h_bias.reshape(1, c))
    return hy.reshape(2, n1h * n2, c // 2)


DFT_K_TILE = 8


def _twiddled_inner_dft(f_ref, twr_ref, twi_ref, a_ref, kk):
    n2, c = a_ref.shape[2], a_ref.shape[3]
    twr = jnp.tile(twr_ref[kk], (1, c // LANES))
    twi = jnp.tile(twi_ref[kk], (1, c // LANES))
    ar, ai = a_ref[0, kk], a_ref[1, kk]
    a = jnp.concatenate([(ar * twr - ai * twi).astype(BF16), (ar * twi + ai * twr).astype(BF16)], axis=0)
    x = jnp.dot(f_ref[...], a, preferred_element_type=F32)
    return x[:n2], x[n2:], twr, twi


def _inner_fwd_kernel(f_ref, twr_ref, twi_ref, a_ref, o_ref):
    for kk in range(a_ref.shape[1]):
        xr, xi, _, _ = _twiddled_inner_dft(f_ref, twr_ref, twi_ref, a_ref, kk)
        o_ref[0, kk] = xr.astype(o_ref.dtype)
        o_ref[1, kk] = xi.astype(o_ref.dtype)


def _inner_specs(n1, n2, c):
    tc = min(DFT_INNER_C_TILE, c)
    kt = min(DFT_K_TILE, n1)
    blk = pl.BlockSpec((2, kt, n2, tc), lambda k, j: (0, k, 0, j))
    mat = pl.BlockSpec((2 * n2, 2 * n2), lambda k, j: (0, 0))
    tw = pl.BlockSpec((kt, n2, LANES), lambda k, j: (k, 0, 0))
    return blk, mat, tw, (n1 // kt, c // tc), kt, tc


def _inner_fwd(f, twr, twi, a):
    _, n1, n2, c = a.shape
    blk, mat, tw, grid, _, _ = _inner_specs(n1, n2, c)
    return pl.pallas_call(
        _inner_fwd_kernel,
        grid=grid,
        in_specs=[mat, tw, tw, blk],
        out_specs=blk,
        out_shape=jax.ShapeDtypeStruct((2, n1, n2, c), BF16),
        compiler_params=_cparams("parallel", "parallel"),
        name="dft_inner_filter",
    )(f, twr, twi, a)


def _inner_conv_kernel(f_ref, ft_ref, twr_ref, twi_ref, a_ref, k_ref, o_ref):
    n2 = a_ref.shape[2]
    for kk in range(a_ref.shape[1]):
        xr, xi, twr, twi = _twiddled_inner_dft(f_ref, twr_ref, twi_ref, a_ref, kk)
        kr, ki = k_ref[0, kk].astype(F32), k_ref[1, kk].astype(F32)
        yr = xr * kr - xi * ki
        yi = xr * ki + xi * kr
        y = jnp.concatenate([yr.astype(BF16), yi.astype(BF16)], axis=0)
        b = jnp.dot(ft_ref[...], y, preferred_element_type=F32)
        br, bi = b[:n2], b[n2:]
        o_ref[0, :, kk, :] = br * twr + bi * twi
        o_ref[1, :, kk, :] = bi * twr - br * twi


def _inner_conv(f, ft, twr, twi, a, kf):
    _, n1, n2, c = a.shape
    blk, mat, tw, grid, kt, tc = _inner_specs(n1, n2, c)
    return pl.pallas_call(
        _inner_conv_kernel,
        grid=grid,
        in_specs=[mat, mat, tw, tw, blk, blk],
        out_specs=pl.BlockSpec((2, n2, kt, tc), lambda k, j: (0, 0, k, j)),
        out_shape=jax.ShapeDtypeStruct((2, n2, n1, c), F32),
        compiler_params=_cparams("parallel", "parallel"),
        name="dft_inner_conv",
    )(f, ft, twr, twi, a, kf)


def _hyena_long_conv(s_t, x0_t, h_bias, w1, b1, w2, b2, w3, freq):
    bsz, n2, n1h, c = s_t.shape
    assert bsz == 2
    n1 = 2 * n1h
    L = n1h * n2
    fwd_c, fwd_r, inv = _dft_outer_matrices(n1)
    f, ft, twr, twi = _dft_inner_matrices(n1, n2)
    af, sumsq = _filter_outer(L, n1, n2, fwd_r, w1, b1, w2, b2, w3, freq)
    kf = _inner_fwd(f, twr, twi, af)
    a = _outer_fwd(fwd_c, s_t)
    b_t = _inner_conv(f, ft, twr, twi, a, kf)
    yscale = lax.rsqrt(sumsq + EPS) * (1.0 / (2 * L))
    return _outer_inv(inv, b_t, s_t, x0_t, yscale, h_bias)


def _pack_bf16_pairs(x):
    half = x.shape[1] // 2
    lo = pltpu.bitcast(x[:, :half].astype(BF16).astype(F32), jnp.uint32) >> 16
    hi = pltpu.bitcast(x[:, half:].astype(BF16).astype(F32), jnp.uint32) & jnp.uint32(0xFFFF0000)
    return lo | hi


def _unpack_bf16_pairs(p):
    lo = pltpu.bitcast(p << 16, F32).astype(BF16)
    hi = pltpu.bitcast(p & jnp.uint32(0xFFFF0000), F32).astype(BF16)
    return jnp.concatenate([lo, hi], axis=1)


def _merge_kernel(hf_ref, hb_ref, o_ref, hy_ref, ga_ref, gb_ref, x_ref,
                  gate_ref, g2_ref, sh_ref, sc_ref, wa_ref, wb_ref, wo_ref, x1_ref, h2_ref):
    a = o_ref[...].astype(F32) * (hf_ref[...].astype(F32) + hb_ref[...].astype(F32))
    half = DFT_C_TILE // 2
    hy = jnp.concatenate([_unpack_bf16_pairs(hy_ref[:, c * half:(c + 1) * half])
                          for c in range(hy_ref.shape[1] // half)], axis=1)
    pa = jnp.dot(a.astype(BF16), wa_ref[...], preferred_element_type=F32)
    pb = jnp.dot(hy, wb_ref[...], preferred_element_type=F32)
    mix = ga_ref[...].astype(F32) * pa + gb_ref[...].astype(F32) * pb
    out = jnp.dot(mix.astype(BF16), wo_ref[...], preferred_element_type=F32)
    x1 = x_ref[...] + gate_ref[...] * out
    x1_ref[...] = x1
    y = x1 * lax.rsqrt(jnp.mean(x1 * x1, axis=-1, keepdims=True) + EPS) * g2_ref[...]
    h2_ref[...] = _pack_bf16_pairs(y * (1.0 + sc_ref[...]) + sh_ref[...])


def _merge(hdirs, pm, hy, x, gate1, g2, shift2, scale2, w_a, w_b, w_out, tm=MERGE_TM):
    bsz, L, d = x.shape
    tok = pl.BlockSpec((None, tm, d), lambda b, i: (b, i, 0))

    def pm_tile(col):
        return pl.BlockSpec((None, tm, d), lambda b, i: (b, i, col))

    packed = pl.BlockSpec((None, tm, d // 2), lambda b, i: (b, i, 0))
    vec = pl.BlockSpec((1, d), lambda b, i: (0, 0))
    bvec = pl.BlockSpec((None, 1, d), lambda b, i: (b, 0, 0))
    wsp = pl.BlockSpec((d, d), lambda b, i: (0, 0), pipeline_mode=pl.Buffered(1))
    return pl.pallas_call(
        _merge_kernel,
        grid=(bsz, L // tm),
        in_specs=[pl.BlockSpec((None, None, tm, d), lambda b, i: (0, b, i, 0)),
                  pl.BlockSpec((None, None, tm, d), lambda b, i: (1, b, i, 0)),
                  pm_tile(PM_O), packed, pm_tile(PM_GA), pm_tile(PM_GB), tok,
                  bvec, vec, bvec, bvec, wsp, wsp, wsp],
        out_specs=[tok, packed],
        out_shape=[jax.ShapeDtypeStruct((bsz, L, d), F32), jax.ShapeDtypeStruct((bsz, L, d // 2), jnp.uint32)],
        compiler_params=_cparams("parallel", "parallel"),
        name="merge",
    )(hdirs, hdirs, pm, hy, pm, pm, x, gate1, g2.reshape(1, d), shift2, scale2, w_a, w_b, w_out)


MOE_BLOCK = 256
ROUTE_E1, ROUTE_E2, ROUTE_W1, ROUTE_W2 = 0, 1, 2, 3
EXP_LANE0 = N_GROUPS


def _first_lane_of_max(val, valid, lane):
    masked = jnp.where(valid, val, NEG_BIG)
    mx = jnp.max(masked, axis=1, keepdims=True)
    idx = jnp.min(jnp.where(valid & (masked == mx), lane, LANES), axis=1, keepdims=True)
    return mx, idx


MOE_TM = 1024


def _expert_onehots(rec):
    lane = lax.broadcasted_iota(jnp.int32, rec.shape, 1)
    oh1 = lane == rec[:, ROUTE_E1:ROUTE_E1 + 1].astype(jnp.int32)
    oh2 = lane == rec[:, ROUTE_E2:ROUTE_E2 + 1].astype(jnp.int32)
    return oh1, oh2


def _router_kernel(h_ref, w_ref, b_ref, r_ref, cnt_ref):
    logits = jnp.dot(_unpack_bf16_pairs(h_ref[...]), w_ref[...], preferred_element_type=F32) + b_ref[...]
    lane = lax.broadcasted_iota(jnp.int32, logits.shape, 1)
    is_g = lane < N_GROUPS
    gmax, gsel = _first_lane_of_max(logits, is_g, lane)
    gsum = jnp.sum(jnp.where(is_g, jnp.exp(logits - gmax), 0.0), axis=1, keepdims=True)
    gw = 1.0 / gsum
    lo = EXP_LANE0 + gsel * EXPERTS_PER_GROUP
    in_grp = (lane >= lo) & (lane < lo + EXPERTS_PER_GROUP)
    emax, l1 = _first_lane_of_max(logits, in_grp, lane)
    esum = jnp.sum(jnp.where(in_grp, jnp.exp(logits - emax), 0.0), axis=1, keepdims=True)
    e2max, l2 = _first_lane_of_max(logits, in_grp & (lane != l1), lane)
    v1 = 1.0 / esum
    v2 = jnp.exp(e2max - emax) / esum
    vs = v1 + v2
    w1 = gw * v1 / vs
    w2 = gw * v2 / vs
    e1 = (l1 - EXP_LANE0).astype(F32)
    e2 = (l2 - EXP_LANE0).astype(F32)
    rec = jnp.where(lane == ROUTE_E1, e1,
                    jnp.where(lane == ROUTE_E2, e2,
                              jnp.where(lane == ROUTE_W1, w1,
                                        jnp.where(lane == ROUTE_W2, w2, 0.0))))
    r_ref[...] = rec
    oh1, oh2 = _expert_onehots(rec)
    counts = jnp.sum((oh1 | oh2).astype(F32), axis=0, keepdims=True)
    cnt_ref[...] = jnp.broadcast_to(counts, cnt_ref.shape)


def _router(h2, w_group, b_group, w_router, b_router):
    n, dp = h2.shape
    d = 2 * dp
    tm = MOE_TM
    w = jnp.zeros((d, LANES), F32).at[:, :N_GROUPS].set(w_group).at[
        :, EXP_LANE0:EXP_LANE0 + N_EXPERTS].set(w_router).astype(BF16)
    b = jnp.zeros((1, LANES), F32).at[0, :N_GROUPS].set(b_group).at[
        0, EXP_LANE0:EXP_LANE0 + N_EXPERTS].set(b_router)
    return pl.pallas_call(
        _router_kernel,
        grid=(n // tm,),
        in_specs=[pl.BlockSpec((tm, dp), lambda i: (i, 0)),
                  pl.BlockSpec((d, LANES), lambda i: (0, 0)),
                  pl.BlockSpec((1, LANES), lambda i: (0, 0))],
        out_specs=[pl.BlockSpec((tm, LANES), lambda i: (i, 0)),
                   pl.BlockSpec((None, 8, LANES), lambda i: (i, 0, 0))],
        out_shape=[jax.ShapeDtypeStruct((n, LANES), F32), jax.ShapeDtypeStruct((n // tm, 8, LANES), F32)],
        compiler_params=_cparams("parallel"),
        name="moe_router",
    )(h2, w, b)


def _slots_kernel(r_ref, base_ref, dest_ref):
    rec = r_ref[...]
    tm = rec.shape[0]
    lane = lax.broadcasted_iota(jnp.int32, rec.shape, 1)
    oh1, oh2 = _expert_onehots(rec)
    r = lax.broadcasted_iota(jnp.int32, (tm, tm), 0)
    c = lax.broadcasted_iota(jnp.int32, (tm, tm), 1)
    earlier = (r > c).astype(BF16)
    rank = jnp.dot(earlier, (oh1 | oh2).astype(BF16), preferred_element_type=F32) + base_ref[0:1, :]
    d1 = jnp.sum(jnp.where(oh1, rank, 0.0), axis=1, keepdims=True)
    d2 = jnp.sum(jnp.where(oh2, rank, 0.0), axis=1, keepdims=True)
    dest = jnp.where(lane == 0, d1, jnp.where(lane == 1, d2, 0.0)).astype(jnp.int32)
    dest_ref[...] = dest.T[:8, :]


def _slots(route, tile_counts):
    n = route.shape[0]
    tm = MOE_TM
    cnt = tile_counts[:, 0, :]
    totals = jnp.sum(cnt, axis=0)
    nblk = jnp.ceil(totals * (1.0 / MOE_BLOCK))
    first_slot = (jnp.cumsum(nblk) - nblk) * float(MOE_BLOCK)
    base = first_slot[None, :] + jnp.cumsum(cnt, axis=0) - cnt
    base = jnp.broadcast_to(base[:, None, :], tile_counts.shape)
    dest = pl.pallas_call(
        _slots_kernel,
        grid=(n // tm,),
        in_specs=[pl.BlockSpec((tm, LANES), lambda i: (i, 0)),
                  pl.BlockSpec((None, 8, LANES), lambda i: (i, 0, 0))],
        out_specs=pl.BlockSpec((None, 8, tm), lambda i: (i, 0, 0)),
        out_shape=jax.ShapeDtypeStruct((n // tm, 8, tm), jnp.int32),
        compiler_params=_cparams("parallel"),
        name="moe_slots",
    )(route, base)
    return (dest[:, 0, :].reshape(n), dest[:, 1, :].reshape(n)), totals


EXPERT_STEP_BLOCKS = 4


def _experts_kernel(be_ref, first_ref, nxt_ref, par_ref, nu_ref, x_ref, w1_hbm, w3_hbm, w2_hbm, o_ref,
                    w1f, w3f, w2f, w1b, w3b, w2b, sems):
    step = pl.program_id(0)

    def weight_copies(e, slot):
        return (pltpu.make_async_copy(w1_hbm.at[e], w1f.at[slot], sems.at[0, slot]),
                pltpu.make_async_copy(w3_hbm.at[e], w3f.at[slot], sems.at[1, slot]),
                pltpu.make_async_copy(w2_hbm.at[e], w2f.at[slot], sems.at[2, slot]))

    @pl.when(step == 0)
    def _():
        for cp in weight_copies(be_ref[0], 0):
            cp.start()

    for sub in range(EXPERT_STEP_BLOCKS):
        i = step * EXPERT_STEP_BLOCKS + sub
        rows = pl.ds(sub * MOE_BLOCK, MOE_BLOCK)

        @pl.when(first_ref[i] == 1)
        def _():
            slot = par_ref[i]

            @pl.when(nxt_ref[i] >= 0)
            def _():
                for cp in weight_copies(nxt_ref[i], 1 - slot):
                    cp.start()

            for cp in weight_copies(be_ref[i], slot):
                cp.wait()
            w1b[...] = w1f[slot].astype(BF16)
            w3b[...] = w3f[slot].astype(BF16)
            w2b[...] = w2f[slot].astype(BF16)

        @pl.when(i < nu_ref[0])
        def _():
            x = _unpack_bf16_pairs(x_ref[rows, :])
            a = jnp.dot(x, w1b[...], preferred_element_type=F32)
            b = jnp.dot(x, w3b[...], preferred_element_type=F32)
            hmid = (a * jax.nn.sigmoid(a)) * b
            o_ref[rows, :] = _pack_bf16_pairs(jnp.dot(hmid.astype(BF16), w2b[...], preferred_element_type=F32))

        @pl.when(i >= nu_ref[0])
        def _():
            o_ref[rows, :] = jnp.zeros((MOE_BLOCK, o_ref.shape[1]), o_ref.dtype)


def _experts(xs, nb, block_e, n_used, w1_e, w3_e, w2_e):
    dp = xs.shape[1]
    d, de = w1_e.shape[1], w1_e.shape[2]
    idx = jnp.arange(nb, dtype=jnp.int32)
    used = idx < n_used[0]
    first = used & ((idx == 0) | (block_e != jnp.roll(block_e, 1)))
    ordinal = jnp.cumsum(first.astype(jnp.int32)) - 1
    par = (ordinal % 2).astype(jnp.int32)
    first_pos = jnp.where(first, idx, nb)
    next_first = lax.cummin(jnp.concatenate([first_pos[1:], jnp.full((1,), nb, jnp.int32)]), reverse=True)
    nxt = jnp.where(next_first < nb, block_e[jnp.minimum(next_first, nb - 1)], -1).astype(jnp.int32)
    any_spec = pl.BlockSpec(memory_space=pl.ANY)
    assert nb % EXPERT_STEP_BLOCKS == 0
    step_rows = EXPERT_STEP_BLOCKS * MOE_BLOCK
    grid_spec = pltpu.PrefetchScalarGridSpec(
        num_scalar_prefetch=5,
        grid=(nb // EXPERT_STEP_BLOCKS,),
        in_specs=[pl.BlockSpec((step_rows, dp), lambda i, *_: (i, 0)), any_spec, any_spec, any_spec],
        out_specs=pl.BlockSpec((step_rows, dp), lambda i, *_: (i, 0)),
        scratch_shapes=[pltpu.VMEM((2, d, de), F32), pltpu.VMEM((2, d, de), F32), pltpu.VMEM((2, de, d), F32),
                        pltpu.VMEM((d, de), BF16), pltpu.VMEM((d, de), BF16), pltpu.VMEM((de, d), BF16),
                        pltpu.SemaphoreType.DMA((3, 2))],
    )
    return pl.pallas_call(
        _experts_kernel,
        grid_spec=grid_spec,
        out_shape=jax.ShapeDtypeStruct((nb * MOE_BLOCK, dp), xs.dtype),
        compiler_params=_cparams("arbitrary"),
        name="moe_experts",
    )(block_e, first.astype(jnp.int32), nxt, par, n_used, xs, w1_e, w3_e, w2_e)


SC_WINDOW = 128
SC_CORES, SC_SUBCORES = 2, 16
SC_WORKERS = SC_CORES * SC_SUBCORES


def _sc_worker_id():
    return lax.axis_index("c") * SC_SUBCORES + lax.axis_index("s")


def _sc_mesh():
    return plsc.VectorSubcoreMesh(core_axis_name="c", subcore_axis_name="s")


def _sc_dispatch(rows, dest0, dest1, pad_slots, n_rows):
    n, dv = rows.shape
    nwin, pwin = n // SC_WINDOW, pad_slots.shape[0] // SC_WINDOW
    assert n % (SC_WINDOW * SC_WORKERS) == 0 and pad_slots.shape[0] % (SC_WINDOW * SC_WORKERS) == 0
    zeros = jnp.zeros((SC_WINDOW, dv), rows.dtype)

    @pl.kernel(out_type=jax.ShapeDtypeStruct((n_rows, dv), rows.dtype), mesh=_sc_mesh(),
               scratch_types=[pltpu.VMEM((1, SC_WINDOW), jnp.int32), pltpu.VMEM((SC_WINDOW, dv), rows.dtype)],
               name="moe_dispatch_sc")
    def scatter(x_hbm, d0_hbm, d1_hbm, p_hbm, z_hbm, o_hbm, idx, buf):
        wid = _sc_worker_id()
        pltpu.sync_copy(z_hbm, buf)

        @pl.loop(0, pwin // SC_WORKERS)
        def _(t):
            w = t * SC_WORKERS + wid
            pltpu.sync_copy(p_hbm.at[pl.ds(w, 1)], idx)
            pltpu.sync_copy(buf, o_hbm.at[idx.at[0]])

        @pl.loop(0, nwin // SC_WORKERS)
        def _(t):
            w = t * SC_WORKERS + wid
            pltpu.sync_copy(x_hbm.at[pl.ds(w * SC_WINDOW, SC_WINDOW)], buf)
            for d_hbm in (d0_hbm, d1_hbm):
                pltpu.sync_copy(d_hbm.at[pl.ds(w, 1)], idx)
                pltpu.sync_copy(buf, o_hbm.at[idx.at[0]])

    return scatter(rows, dest0.reshape(nwin, SC_WINDOW), dest1.reshape(nwin, SC_WINDOW),
                   pad_slots.reshape(pwin, SC_WINDOW), zeros)


def _sc_gather(table, index):
    m = index.shape[0]
    dv = table.shape[1]
    nwin = m // SC_WINDOW
    assert m % (SC_WINDOW * SC_WORKERS) == 0

    @pl.kernel(out_type=jax.ShapeDtypeStruct((m, dv), table.dtype), mesh=_sc_mesh(),
               scratch_types=[pltpu.VMEM((1, SC_WINDOW), jnp.int32), pltpu.VMEM((SC_WINDOW, dv), table.dtype)],
               name="moe_gather_sc")
    def gather(x_hbm, i_hbm, o_hbm, idx, buf):
        wid = _sc_worker_id()

        @pl.loop(0, nwin // SC_WORKERS)
        def _(t):
            w = t * SC_WORKERS + wid
            pltpu.sync_copy(i_hbm.at[pl.ds(w, 1)], idx)
            pltpu.sync_copy(x_hbm.at[idx.at[0]], buf)
            pltpu.sync_copy(buf, o_hbm.at[pl.ds(w * SC_WINDOW, SC_WINDOW)])

    return gather(table, index.reshape(nwin, SC_WINDOW))


def _combine_planes_kernel(r_ref, ya_ref, yb_ref, x_ref, gate_ref, gf_ref, o_ref):
    rec = r_ref[...]
    y = (_unpack_bf16_pairs(ya_ref[...]).astype(F32) * rec[:, ROUTE_W1:ROUTE_W1 + 1]
         + _unpack_bf16_pairs(yb_ref[...]).astype(F32) * rec[:, ROUTE_W2:ROUTE_W2 + 1])
    x2 = x_ref[...] + gate_ref[...] * y
    o_ref[...] = x2 * lax.rsqrt(jnp.mean(x2 * x2, axis=-1, keepdims=True) + EPS) * gf_ref[...]


def _combine_planes(g, route, x1, gate2, g_final, tm=COMBINE_TM):
    bsz, L, d = x1.shape
    tpb = L // tm
    dp = g.shape[-1]
    return pl.pallas_call(
        _combine_planes_kernel,
        grid=(bsz, tpb),
        in_specs=[pl.BlockSpec((tm, LANES), lambda b, i: (b * tpb + i, 0)),
                  pl.BlockSpec((None, tm, dp), lambda b, i: (0, b * tpb + i, 0)),
                  pl.BlockSpec((None, tm, dp), lambda b, i: (1, b * tpb + i, 0)),
                  pl.BlockSpec((None, tm, d), lambda b, i: (b, i, 0)),
                  pl.BlockSpec((None, 1, d), lambda b, i: (b, 0, 0)),
                  pl.BlockSpec((1, d), lambda b, i: (0, 0))],
        out_specs=pl.BlockSpec((None, tm, d), lambda b, i: (b, i, 0)),
        out_shape=jax.ShapeDtypeStruct((bsz, L, d), F32),
        compiler_params=_cparams("parallel", "parallel"),
        name="moe_combine",
    )(route, g, g, x1, gate2, g_final.reshape(1, d))


def _moe(h2, x1, gate2, g_final, w_group, b_group, w_router, b_router, w1_e, w3_e, w2_e):
    bsz, L, d = x1.shape
    n = bsz * L
    h2f = h2.reshape(n, h2.shape[-1])
    route, tile_counts = _router(h2f, w_group, b_group, w_router, b_router)
    (dest0, dest1), counts = _slots(route, tile_counts)
    nb = (2 * n) // MOE_BLOCK + N_EXPERTS
    cnt = counts[:N_EXPERTS].astype(jnp.int32)
    blocks_per_e = (cnt + MOE_BLOCK - 1) // MOE_BLOCK
    ends = jnp.cumsum(blocks_per_e)
    block_e = jnp.minimum(jnp.sum(ends[None, :] <= jnp.arange(nb, dtype=jnp.int32)[:, None], axis=1),
                          N_EXPERTS - 1).astype(jnp.int32)
    n_used = ends[-1:].astype(jnp.int32)
    n_slots = nb * MOE_BLOCK
    pad_j = jnp.arange(MOE_BLOCK, dtype=jnp.int32)[None, :]
    spare = n_slots + jnp.arange(N_EXPERTS * MOE_BLOCK, dtype=jnp.int32).reshape(N_EXPERTS, MOE_BLOCK)
    first_slot = ((ends - blocks_per_e) * MOE_BLOCK)[:, None]
    is_pad = cnt[:, None] + pad_j < blocks_per_e[:, None] * MOE_BLOCK
    pad_slots = jnp.where(is_pad, first_slot + cnt[:, None] + pad_j, spare).reshape(-1)
    xs = _sc_dispatch(h2f, dest0, dest1, pad_slots, n_slots + N_EXPERTS * MOE_BLOCK)
    ys = _experts(xs, nb, block_e, n_used, w1_e, w3_e, w2_e)
    g = _sc_gather(ys, jnp.concatenate([dest0, dest1]))
    return _combine_planes(g.reshape(2, n, g.shape[-1]), route, x1, gate2, g_final)


def kernel(x, c, ctx, c_ctx, w_mod, b_mod, g_norm1, g_norm2, w_in, b_in, w_qk_conv, b_qk_conv,
           w_h_conv, b_h_conv, hf_w1, hf_b1, hf_w2, hf_b2, hf_w3, hf_freq, h_bias, w_a, w_b, w_out,
           w_group, b_group, w_router, b_router, w1_e, w3_e, w2_e, g_final):
    assert w_mod.shape[0] == 1, "single-layer block"
    (w_mod, b_mod, g_norm1, g_norm2, w_in, b_in, w_qk_conv, b_qk_conv, w_h_conv, b_h_conv, hf_w1, hf_b1, hf_w2,
     hf_b2, hf_w3, hf_freq, h_bias, w_a, w_b, w_out, w_group, b_group, w_router, b_router, w1_e, w3_e, w2_e) = (
        t[0] for t in (w_mod, b_mod, g_norm1, g_norm2, w_in, b_in, w_qk_conv, b_qk_conv, w_h_conv, b_h_conv,
                       hf_w1, hf_b1, hf_w2, hf_b2, hf_w3, hf_freq, h_bias, w_a, w_b, w_out, w_group, b_group,
                       w_router, b_router, w1_e, w3_e, w2_e))
    bsz, L, d = x.shape
    lc = ctx.shape[1]
    seg = L // (L // GRID_W)
    chunk_c = min(lc, MLSTM_CHUNK)
    assert bsz + 1 <= 8 and lc % chunk_c == 0 and L % MLSTM_CHUNK == 0

    cond = jnp.zeros((8, d), F32).at[:bsz].set(c).at[bsz].set(c_ctx)
    mod = _adaln(cond, w_mod, b_mod).reshape(8, 6, d)
    modx = mod[:bsz]
    shift1, scale1, gate1, shift2, scale2, gate2 = (modx[:, i:i + 1] for i in range(6))
    shift1c = jnp.broadcast_to(mod[bsz, 0].reshape(1, 1, d), (bsz, 1, d))
    scale1c = jnp.broadcast_to(mod[bsz, 1].reshape(1, 1, d), (bsz, 1, d))

    k_scale = jnp.full((M_WIDTH,), M_HEAD_DIM ** -0.5, F32)
    qk_scale = jnp.concatenate([jnp.ones((M_WIDTH,), F32), k_scale])
    w_gates, b_gates = w_in[:, IG0:M_COLS], b_in[IG0:M_COLS]
    w_main, w_hyena = _weight_prep(jnp.swapaxes(w_in, 0, 1))
    b_main = jnp.concatenate([b_in[Q0:IG0], b_in[GA0:IN_COLS]])

    hc = _norm_mod(ctx, g_norm1, shift1c, scale1c, lc)
    kc = _proj_conv_silu(hc, w_main[:, K0 - Q0:V0 - Q0], b_in[K0:V0], w_qk_conv[:, M_WIDTH:],
                         b_qk_conv[M_WIDTH:], k_scale, lc, lc)
    vc = _proj_act(hc, w_main[:, V0 - Q0:O0 - Q0], b_in[V0:O0], "none", BF16, lc)
    bcc, acc, arc = _gates(hc, w_gates, b_gates, chunk_c)
    zero_state = (jnp.zeros((bsz, 2, M_HEADS, M_HEAD_DIM, M_HEAD_DIM), F32),
                  jnp.zeros((bsz, 2, M_HEADS, 1, M_HEAD_DIM), F32),
                  jnp.zeros((bsz, 2, M_HEADS, 1, LANES), F32))
    _, ctx_state = _mlstm(None, (kc, 0), (vc, 0), bcc, acc, arc, zero_state, False, chunk_c)

    tm = ROW_TILE
    _, dft_fast = _dft_factors(2 * L)
    pm, h, h_il = _proj_main(x, g_norm1, shift1, scale1, w_main, b_main, w_qk_conv, b_qk_conv, qk_scale,
                             seg, tm, dft_fast)
    bc, ac, ar = _gates(h, w_gates, b_gates, MLSTM_CHUNK)
    hdirs, _ = _mlstm((pm, PM_Q), (pm, PM_K), (pm, PM_V), bc, ac, ar, ctx_state, True, MLSTM_CHUNK)

    x0_t, s_t = _proj_hyena(h_il, w_hyena, b_in[HY0:GA0], w_h_conv, b_h_conv, seg)
    hy = _hyena_long_conv(s_t, x0_t, h_bias, hf_w1, hf_b1, hf_w2, hf_b2, hf_w3, hf_freq)

    x1, h2 = _merge(hdirs, pm, hy, x, gate1, g_norm2, shift2, scale2,
                    w_a.astype(BF16), w_b.astype(BF16), w_out.astype(BF16))
    return _moe(h2, x1, gate2, g_final, w_group, b_group, w_router, b_router, w1_e, w3_e, w2_e)
```

```python
import functools
import math

import jax
import jax.numpy as jnp
import numpy as np
from jax import lax
from jax.experimental import pallas as pl
from jax.experimental.pallas import tpu as pltpu
from jax.experimental.pallas import tpu_sc as plsc

F32 = jnp.float32
BF16 = jnp.bfloat16

D_MODEL = 1024
GRID_W = 64
EPS = 1e-6
M_HEADS = 4
M_HEAD_DIM = 256
M_WIDTH = M_HEADS * M_HEAD_DIM
H_WIDTH = 1024
H_POS_BANDS = 16
H_FILTER_HIDDEN = 64
H_FAST_DECAY_PCT = 0.3
H_SLOW_DECAY_PCT = 1.5
H_DECAY_TARGET = 1e-2
N_GROUPS = 8
EXPERTS_PER_GROUP = 8
N_EXPERTS = N_GROUPS * EXPERTS_PER_GROUP
D_EXPERT = 512
Q0 = 0
K0 = Q0 + M_WIDTH
V0 = K0 + M_WIDTH
O0 = V0 + M_WIDTH
IG0 = O0 + M_WIDTH
FG0 = IG0 + 2 * M_HEADS
M_COLS = FG0 + 2 * M_HEADS
HY0 = M_COLS
GA0 = HY0 + 3 * H_WIDTH
GB0 = GA0 + D_MODEL
IN_COLS = GB0 + D_MODEL

LANES = 128
MLSTM_CHUNK = 512
NEG_BIG = -1e30
VMEM_LIMIT = 48 * 1024 * 1024
ROW_TILE = 1024
ADALN_TN = 1536
SMALL_TN = 512
MERGE_TM = 512
COMBINE_TM = 512


def _cparams(*sem):
    return pltpu.CompilerParams(dimension_semantics=sem, vmem_limit_bytes=VMEM_LIMIT)


def _adaln_kernel(c_ref, w_ref, b_ref, o_ref):
    s = c_ref[...]
    s = s * jax.nn.sigmoid(s)
    o_ref[...] = jnp.dot(s.astype(BF16), w_ref[...].astype(BF16), preferred_element_type=F32) + b_ref[...]


def _adaln(cond, w_mod, b_mod):
    n = w_mod.shape[1]
    tn = ADALN_TN
    return pl.pallas_call(
        _adaln_kernel,
        grid=(n // tn,),
        in_specs=[pl.BlockSpec((8, D_MODEL), lambda j: (0, 0)),
                  pl.BlockSpec((D_MODEL, tn), lambda j: (0, j)),
                  pl.BlockSpec((1, tn), lambda j: (0, j))],
        out_specs=pl.BlockSpec((8, tn), lambda j: (0, j)),
        out_shape=jax.ShapeDtypeStruct((8, n), F32),
        compiler_params=_cparams("arbitrary"),
        name="adaln",
    )(cond, w_mod, b_mod.reshape(1, n))


def _norm_mod_kernel(x_ref, g_ref, sh_ref, sc_ref, o_ref):
    x = x_ref[...]
    y = x * lax.rsqrt(jnp.mean(x * x, axis=-1, keepdims=True) + EPS)
    y = y * g_ref[...]
    o_ref[...] = (y * (1.0 + sc_ref[...]) + sh_ref[...]).astype(o_ref.dtype)


def _norm_mod(x, g, shift, scale, tm):
    bsz, L, d = x.shape
    return pl.pallas_call(
        _norm_mod_kernel,
        grid=(bsz, L // tm),
        in_specs=[pl.BlockSpec((None, tm, d), lambda b, i: (b, i, 0)),
                  pl.BlockSpec((1, d), lambda b, i: (0, 0)),
                  pl.BlockSpec((None, 1, d), lambda b, i: (b, 0, 0)),
                  pl.BlockSpec((None, 1, d), lambda b, i: (b, 0, 0))],
        out_specs=pl.BlockSpec((None, tm, d), lambda b, i: (b, i, 0)),
        out_shape=jax.ShapeDtypeStruct((bsz, L, d), BF16),
        compiler_params=_cparams("parallel", "parallel"),
        name="norm_mod",
    )(x, g.reshape(1, d), shift, scale)


def _conv3(z, wc, bc, seg):
    tm = z.shape[0]
    pos = lax.broadcasted_iota(jnp.int32, z.shape, 0) & (seg - 1)
    zp = jnp.where(pos == 0, 0.0, pltpu.roll(z, 1, 0))
    zn = jnp.where(pos == seg - 1, 0.0, pltpu.roll(z, tm - 1, 0))
    return zp * wc[0:1, :] + z * wc[1:2, :] + zn * wc[2:3, :] + bc


def _proj_act_kernel(h_ref, w_ref, b_ref, o_ref, *, act):
    z = jnp.dot(h_ref[...], w_ref[...], preferred_element_type=F32) + b_ref[...]
    if act == "sigmoid":
        z = jax.nn.sigmoid(z)
    o_ref[...] = z.astype(o_ref.dtype)


def _proj_act(h, w, b, act, out_dtype, tm, tn=SMALL_TN):
    bsz, L, d = h.shape
    n = w.shape[1]
    return pl.pallas_call(
        functools.partial(_proj_act_kernel, act=act),
        grid=(bsz, L // tm, n // tn),
        in_specs=[pl.BlockSpec((None, tm, d), lambda b_, i, j: (b_, i, 0)),
                  pl.BlockSpec((d, tn), lambda b_, i, j: (0, j)),
                  pl.BlockSpec((1, tn), lambda b_, i, j: (0, j))],
        out_specs=pl.BlockSpec((None, tm, tn), lambda b_, i, j: (b_, i, j)),
        out_shape=jax.ShapeDtypeStruct((bsz, L, n), out_dtype),
        compiler_params=_cparams("parallel", "parallel", "arbitrary"),
        name="proj_" + act,
    )(h, w, b.reshape(1, n))


def _proj_conv_silu_kernel(h_ref, w_ref, b_ref, wc_ref, bc_ref, cs_ref, o_ref, *, seg):
    z = jnp.dot(h_ref[...], w_ref[...], preferred_element_type=F32) + b_ref[...]
    y = _conv3(z, wc_ref[...], bc_ref[...], seg)
    y = y * jax.nn.sigmoid(y)
    o_ref[...] = (y * cs_ref[...]).astype(o_ref.dtype)


def _proj_conv_silu(h, w, b, wc, bc, colscale, seg, tm, tn=SMALL_TN):
    bsz, L, d = h.shape
    n = w.shape[1]
    col = lambda b_, i, j: (0, j)
    return pl.pallas_call(
        functools.partial(_proj_conv_silu_kernel, seg=seg),
        grid=(bsz, L // tm, n // tn),
        in_specs=[pl.BlockSpec((None, tm, d), lambda b_, i, j: (b_, i, 0)),
                  pl.BlockSpec((d, tn), col),
                  pl.BlockSpec((1, tn), col),
                  pl.BlockSpec((3, tn), col),
                  pl.BlockSpec((1, tn), col),
                  pl.BlockSpec((1, tn), col)],
        out_specs=pl.BlockSpec((None, tm, tn), lambda b_, i, j: (b_, i, j)),
        out_shape=jax.ShapeDtypeStruct((bsz, L, n), BF16),
        compiler_params=_cparams("parallel", "parallel", "arbitrary"),
        name="proj_conv_silu",
    )(h, w, b.reshape(1, n), wc, bc.reshape(1, n), colscale.reshape(1, n))


PROJ_TN = 1024
MAIN_TILE_COLS = (Q0, K0, V0, O0, GA0, GB0)
HYENA_TILE_COLS = tuple(range(HY0, GA0, PROJ_TN))


def _weight_prep_kernel(start_ref, wt_hbm, main_ref, hy_ref, buf, sem):
    t = pl.program_id(0)
    n_main = len(MAIN_TILE_COLS)

    def tile_copy(step, slot):
        rows = pl.ds(pl.multiple_of(start_ref[step], 8), PROJ_TN)
        return pltpu.make_async_copy(wt_hbm.at[rows], buf.at[slot], sem.at[slot])

    @pl.when(t == 0)
    def _():
        tile_copy(0, 0).start()

    @pl.when(t + 1 < pl.num_programs(0))
    def _():
        tile_copy(t + 1, (t + 1) % 2).start()

    slot = t % 2
    tile_copy(t, slot).wait()
    tile = buf[slot].T.astype(BF16)

    @pl.when(t < n_main)
    def _():
        main_ref[...] = tile

    @pl.when(t >= n_main)
    def _():
        hy_ref[...] = tile


def _weight_prep(w_t):
    cols, d = w_t.shape
    starts = MAIN_TILE_COLS + HYENA_TILE_COLS
    n_main, n_hy = len(MAIN_TILE_COLS), len(HYENA_TILE_COLS)
    assert cols == IN_COLS and d % LANES == 0 and all(s % 8 == 0 and s + PROJ_TN <= cols for s in starts)
    grid_spec = pltpu.PrefetchScalarGridSpec(
        num_scalar_prefetch=1,
        grid=(len(starts),),
        in_specs=[pl.BlockSpec(memory_space=pl.ANY)],
        out_specs=[pl.BlockSpec((d, PROJ_TN), lambda t, *_: (0, jnp.minimum(t, n_main - 1))),
                   pl.BlockSpec((d, PROJ_TN), lambda t, *_: (0, jnp.maximum(t - n_main, 0)))],
        scratch_shapes=[pltpu.VMEM((2, PROJ_TN, d), F32), pltpu.SemaphoreType.DMA((2,))],
    )
    return pl.pallas_call(
        _weight_prep_kernel,
        grid_spec=grid_spec,
        out_shape=[jax.ShapeDtypeStruct((d, n_main * PROJ_TN), BF16), jax.ShapeDtypeStruct((d, n_hy * PROJ_TN), BF16)],
        compiler_params=_cparams("arbitrary"),
        name="weight_prep",
    )(jnp.asarray(starts, jnp.int32), w_t)


PROJ_SUB = 512
PM_Q, PM_K, PM_V, PM_O, PM_GA, PM_GB = range(6)


def _proj_main_kernel(x_ref, g_ref, sh_ref, sc_ref, w_ref, b_ref, wc_ref, bc_ref, cs_ref,
                      o_ref, h_ref, hi_hbm, hp_sc, sem, *, seg):
    b, i, j = pl.program_id(0), pl.program_id(1), pl.program_id(2)
    n2, jt = hi_hbm.shape[2], hi_hbm.shape[3]

    def interleave_copy(jj):
        return pltpu.make_async_copy(hp_sc.at[pl.ds(jj * n2, n2)], hi_hbm.at[b, i, :, jj, :], sem)

    @pl.when(j == 0)
    def _():
        x = x_ref[...]
        y = x * lax.rsqrt(jnp.mean(x * x, axis=-1, keepdims=True) + EPS) * g_ref[...]
        y = y * (1.0 + sc_ref[...]) + sh_ref[...]
        h_ref[...] = y.astype(h_ref.dtype)
        hp_sc[...] = _pack_bf16_pairs(y)
        for jj in range(jt):
            interleave_copy(jj).start()

    @pl.when(j == pl.num_programs(2) - 1)
    def _():
        for jj in range(jt):
            interleave_copy(jj).wait()

    def run(epilogue):
        for c in range(PROJ_TN // PROJ_SUB):
            sl = slice(c * PROJ_SUB, (c + 1) * PROJ_SUB)
            z = jnp.dot(h_ref[...], w_ref[:, sl], preferred_element_type=F32) + b_ref[:, sl]
            o_ref[:, sl] = epilogue(z, sl).astype(o_ref.dtype)

    def conv_silu(z, sl):
        y = _conv3(z, wc_ref[:, sl], bc_ref[:, sl], seg)
        return (y * jax.nn.sigmoid(y)) * cs_ref[:, sl]

    @pl.when(j <= PM_K)
    def _():
        run(conv_silu)

    @pl.when(j == PM_V)
    def _():
        run(lambda z, sl: z)

    @pl.when(j >= PM_O)
    def _():
        run(lambda z, sl: jax.nn.sigmoid(z))


def _proj_main(x, g, shift, scale, w, b, wc, bc, colscale, seg, tm, n2):
    bsz, L, d = x.shape
    n = w.shape[1]
    jt = tm // n2
    qk = lambda b_, i, j: (0, jnp.minimum(j, PM_K))
    row = pl.BlockSpec((None, tm, d), lambda b_, i, j: (b_, i, 0))
    bvec = pl.BlockSpec((None, 1, d), lambda b_, i, j: (b_, 0, 0))
    return pl.pallas_call(
        functools.partial(_proj_main_kernel, seg=seg),
        grid=(bsz, L // tm, n // PROJ_TN),
        in_specs=[row, pl.BlockSpec((1, d), lambda b_, i, j: (0, 0)), bvec, bvec,
                  pl.BlockSpec((d, PROJ_TN), lambda b_, i, j: (0, j)),
                  pl.BlockSpec((1, PROJ_TN), lambda b_, i, j: (0, j)),
                  pl.BlockSpec((3, PROJ_TN), qk),
                  pl.BlockSpec((1, PROJ_TN), qk),
                  pl.BlockSpec((1, PROJ_TN), qk)],
        out_specs=[pl.BlockSpec((None, tm, PROJ_TN), lambda b_, i, j: (b_, i, j)), row,
                   pl.BlockSpec(memory_space=pl.ANY)],
        out_shape=[jax.ShapeDtypeStruct((bsz, L, n), BF16), jax.ShapeDtypeStruct((bsz, L, d), BF16),
                   jax.ShapeDtypeStruct((bsz, L // tm, n2, jt, d // 2), jnp.uint32)],
        scratch_shapes=[pltpu.VMEM((tm, d // 2), jnp.uint32), pltpu.SemaphoreType.DMA(())],
        compiler_params=_cparams("parallel", "parallel", "arbitrary"),
        name="proj_main",
    )(x, g.reshape(1, d), shift, scale, w, b.reshape(1, n), wc, bc.reshape(1, -1), colscale.reshape(1, -1))


def _conv3_interleaved(z, wc, bc, seg, jt):
    grp = seg * jt
    pad = jnp.zeros((jt, z.shape[1]), z.dtype)
    prev, nxt = [], []
    for g0 in range(0, z.shape[0], grp):
        zg = z[g0:g0 + grp]
        prev += [pad, zg[:grp - jt]]
        nxt += [zg[jt:], pad]
    zp = jnp.concatenate(prev, axis=0)
    zn = jnp.concatenate(nxt, axis=0)
    return zp * wc[0:1, :] + z * wc[1:2, :] + zn * wc[2:3, :] + bc


def _proj_hyena_kernel(h_ref, w0_ref, w1_ref, w2_ref, b_ref, wc_ref, bc_ref, x0_ref, s_ref, *, seg):
    n2, jt = s_ref.shape[0], s_ref.shape[1]
    h = _unpack_bf16_pairs(h_ref[...].reshape(n2 * jt, h_ref.shape[2]))
    us = []
    for g, w_ref in enumerate((w0_ref, w1_ref, w2_ref)):
        z = jnp.dot(h, w_ref[...], preferred_element_type=F32) + b_ref[g]
        us.append(_conv3_interleaved(z, wc_ref[g], bc_ref[g], seg, jt))
    x0_ref[...] = _pack_bf16_pairs(us[0]).reshape(x0_ref.shape)
    s_ref[...] = (us[1] * us[2]).reshape(s_ref.shape)


def _proj_hyena(hi, w, b, wc, bc, seg):
    bsz, nt, n2, jt, dp = hi.shape
    d, tm = 2 * dp, n2 * jt
    L = nt * tm
    tn = DFT_C_TILE
    nblk = H_WIDTH // tn
    assert n2 % seg == 0 and (jt % 8 == 0 or nt == 1)
    b3 = b.reshape(3, 1, H_WIDTH)
    wc3 = wc.reshape(3, 3, H_WIDTH).transpose(1, 0, 2)
    bc3 = bc.reshape(3, 1, H_WIDTH)
    return pl.pallas_call(
        functools.partial(_proj_hyena_kernel, seg=seg),
        grid=(bsz, nt, nblk),
        in_specs=[pl.BlockSpec((None, None, n2, jt, dp), lambda b_, i, j: (b_, i, 0, 0, 0)),
                  pl.BlockSpec((d, tn), lambda b_, i, j: (0, j)),
                  pl.BlockSpec((d, tn), lambda b_, i, j: (0, nblk + j)),
                  pl.BlockSpec((d, tn), lambda b_, i, j: (0, 2 * nblk + j)),
                  pl.BlockSpec((3, 1, tn), lambda b_, i, j: (0, 0, j)),
                  pl.BlockSpec((3, 3, tn), lambda b_, i, j: (0, 0, j)),
                  pl.BlockSpec((3, 1, tn), lambda b_, i, j: (0, 0, j))],
        out_specs=[pl.BlockSpec((None, n2, jt, tn // 2), lambda b_, i, j: (b_, 0, i, j)),
                   pl.BlockSpec((None, n2, jt, tn), lambda b_, i, j: (b_, 0, i, j))],
        out_shape=[jax.ShapeDtypeStruct((bsz, n2, L // n2, H_WIDTH // 2), jnp.uint32),
                   jax.ShapeDtypeStruct((bsz, n2, L // n2, H_WIDTH), F32)],
        compiler_params=_cparams("parallel", "parallel", "arbitrary"),
        name="proj_hyena",
    )(hi, w, w, w, b3, wc3, bc3)


N_GATES = 4 * M_HEADS


def _split3(x):
    hi = x.astype(BF16)
    r1 = x - hi.astype(F32)
    mid = r1.astype(BF16)
    lo = (r1 - mid.astype(F32)).astype(BF16)
    return hi, mid, lo


def _log_sigmoid(x):
    return jnp.minimum(x, 0.0) - jnp.log1p(jnp.exp(-jnp.abs(x)))


def _gates_kernel(h_ref, w_ref, wt_ref, b_ref, bt_ref, bc_ref, ac_ref, ar_ref):
    h = h_ref[...]
    t = h.shape[0]
    z = jnp.dot(h, w_ref[...], preferred_element_type=F32) + b_ref[...]
    zt = lax.dot_general(wt_ref[...], h, (((1,), (1,)), ((), ())),
                         preferred_element_type=F32) + bt_ref[...]
    r = lax.broadcasted_iota(jnp.int32, (t, t), 0)
    c = lax.broadcasted_iota(jnp.int32, (t, t), 1)
    lower = (r >= c).astype(BF16)
    upper = (r <= c).astype(BF16)
    g8 = FG_LANE0

    lf = _log_sigmoid(z)
    lane = lax.broadcasted_iota(jnp.int32, z.shape, 1)
    is_fg = (lane >= g8) & (lane < 2 * g8)
    terms = [jnp.where(is_fg, p.astype(F32), 0.0) for p in _split3(lf)]
    packed = terms[0] + pltpu.roll(terms[1], 2 * g8, 1) + pltpu.roll(terms[2], 4 * g8, 1)
    cfp = jnp.dot(lower, packed.astype(BF16), preferred_element_type=F32)
    cf = cfp + pltpu.roll(cfp, LANES - 2 * g8, 1) + pltpu.roll(cfp, LANES - 4 * g8, 1)
    cb = cf[t - 1:t, :] - cf + lf
    bc = jnp.where(lane < g8 + M_HEADS, cf, cb)
    bc = pltpu.roll(bc, LANES - g8, 1)
    bc_ref[...] = bc
    ac_ref[...] = z - bc

    lft = _log_sigmoid(zt[g8:, :])
    stacked = jnp.concatenate([p.astype(F32) for p in _split3(lft)] + [jnp.zeros_like(lft)], axis=0)
    cft3 = jnp.dot(stacked.astype(BF16), upper, preferred_element_type=F32)
    cft = cft3[0:g8] + cft3[g8:2 * g8] + cft3[2 * g8:3 * g8]
    cbt = cft[:, t - 1:t] - cft + lft
    row = lax.broadcasted_iota(jnp.int32, cft.shape, 0)
    ar_ref[...] = zt[:g8, :] - jnp.where(row < M_HEADS, cft, cbt)


FG_LANE0 = 2 * M_HEADS


def _gates(h, w_g, b_g, chunk):
    bsz, L, d = h.shape
    w_pad = jnp.zeros((d, LANES), F32).at[:, :N_GATES].set(w_g).astype(BF16)
    b_pad = jnp.zeros((1, LANES), F32).at[0, :N_GATES].set(b_g)
    wt = w_g.T.astype(BF16)
    bt = b_g.reshape(N_GATES, 1)
    tok = pl.BlockSpec((None, chunk, LANES), lambda b_, i: (b_, i, 0))
    return pl.pallas_call(
        _gates_kernel,
        grid=(bsz, L // chunk),
        in_specs=[pl.BlockSpec((None, chunk, d), lambda b_, i: (b_, i, 0)),
                  pl.BlockSpec((d, LANES), lambda b_, i: (0, 0)),
                  pl.BlockSpec((N_GATES, d), lambda b_, i: (0, 0)),
                  pl.BlockSpec((1, LANES), lambda b_, i: (0, 0)),
                  pl.BlockSpec((N_GATES, 1), lambda b_, i: (0, 0))],
        out_specs=[tok, tok, pl.BlockSpec((None, FG_LANE0, chunk), lambda b_, i: (b_, 0, i))],
        out_shape=[jax.ShapeDtypeStruct((bsz, L, LANES), F32),
                   jax.ShapeDtypeStruct((bsz, L, LANES), F32),
                   jax.ShapeDtypeStruct((bsz, FG_LANE0, L), F32)],
        compiler_params=_cparams("parallel", "parallel"),
        name="mlstm_gates",
    )(h, w_pad, wt, b_pad, bt)


def _mlstm_kernel(*refs, emit_h, n_chunks):
    if emit_h:
        (q_ref, k_ref, v_ref, bc_ref, ac_ref, ar_ref, c0_ref, n0_ref, m0_ref,
         h_ref, cf_ref, nf_ref, mf_ref, c_sc, n_sc, m_sc) = refs
    else:
        (k_ref, v_ref, bc_ref, ac_ref, ar_ref, c0_ref, n0_ref, m0_ref,
         cf_ref, nf_ref, mf_ref, c_sc, n_sc, m_sc) = refs
    d = pl.program_id(1)
    j = pl.program_id(2)
    fwd = d == 0
    t = k_ref.shape[0]
    dh = M_HEAD_DIM

    @pl.when(j == 0)
    def _():
        c_sc[...] = c0_ref[...]
        n_sc[...] = n0_ref[...]
        m_sc[...] = m0_ref[...]

    r = lax.broadcasted_iota(jnp.int32, (t, t), 0)
    c = lax.broadcasted_iota(jnp.int32, (t, t), 1)
    causal = jnp.where(fwd, r - c, c - r) >= 0
    bc_all = bc_ref[...]
    ac_all = ac_ref[...]
    ar_all = ar_ref[...]
    for hd in range(M_HEADS):
        sl = slice(hd * dh, (hd + 1) * dh)
        bc = jnp.where(fwd, bc_all[:, hd:hd + 1], bc_all[:, M_HEADS + hd:M_HEADS + hd + 1])
        ac = jnp.where(fwd, ac_all[:, hd:hd + 1], ac_all[:, M_HEADS + hd:M_HEADS + hd + 1])
        ar = jnp.where(fwd, ar_all[hd:hd + 1, :], ar_all[M_HEADS + hd:M_HEADS + hd + 1, :])
        b_tot = jnp.where(fwd, bc[t - 1:t, :], bc[0:1, :])
        m_prev = m_sc[hd][:, 0:1]
        k_h = k_ref[:, sl]
        v_h = v_ref[:, sl]
        if emit_h:
            q_h = q_ref[:, sl]
            dm = jnp.where(causal, bc + ar, NEG_BIG)
            inter = bc + m_prev
            m_t = jnp.maximum(inter, jnp.max(dm, axis=1, keepdims=True))
            qk = lax.dot_general(q_h, k_h, (((1,), (1,)), ((), ())), preferred_element_type=F32)
            s = qk * jnp.exp(dm - m_t)
            carry = jnp.exp(inter - m_t)
            num = (jnp.dot(s.astype(BF16), v_h, preferred_element_type=F32)
                   + carry * jnp.dot(q_h, c_sc[hd].astype(BF16), preferred_element_type=F32))
            den = (jnp.sum(s, axis=1, keepdims=True)
                   + carry * jnp.sum(q_h.astype(F32) * n_sc[hd], axis=1, keepdims=True))
            h_ref[:, sl] = (num / jnp.maximum(jnp.abs(den), jnp.exp(-m_t))).astype(h_ref.dtype)
        g = b_tot + ac
        m_new = jnp.maximum(b_tot + m_prev, jnp.max(g, axis=0, keepdims=True))
        wgt = jnp.exp(g - m_new)
        decay = jnp.exp(b_tot + m_prev - m_new)
        kw = k_h.astype(F32) * wgt
        c_sc[hd] = decay * c_sc[hd] + lax.dot_general(kw.astype(BF16), v_h, (((0,), (0,)), ((), ())),
                                                      preferred_element_type=F32)
        n_sc[hd] = decay * n_sc[hd] + jnp.sum(kw, axis=0, keepdims=True)
        m_sc[hd] = jnp.broadcast_to(m_new, (1, LANES))

    @pl.when(j == n_chunks - 1)
    def _():
        cf_ref[...] = c_sc[...]
        nf_ref[...] = n_sc[...]
        mf_ref[...] = m_sc[...]


def _mlstm(q, k, v, bc, ac, ar, state, emit_h, t):
    bsz, L, _ = k[0].shape
    nc = L // t
    seq = lambda b_, d, j: (b_, j + d * (nc - 1 - 2 * j), 0)
    st = lambda b_, d, j: (b_, d, 0, 0, 0)

    def tok(col):
        return pl.BlockSpec((None, t, M_WIDTH), lambda b_, d, j: (b_, j + d * (nc - 1 - 2 * j), col))

    gate_spec = pl.BlockSpec((None, t, LANES), seq)
    ar_spec = pl.BlockSpec((None, FG_LANE0, t), lambda b_, d, j: (b_, 0, j + d * (nc - 1 - 2 * j)))
    c_spec = pl.BlockSpec((None, None, M_HEADS, M_HEAD_DIM, M_HEAD_DIM), st)
    n_spec = pl.BlockSpec((None, None, M_HEADS, 1, M_HEAD_DIM), st)
    m_spec = pl.BlockSpec((None, None, M_HEADS, 1, LANES), st)
    state_shapes = [jax.ShapeDtypeStruct((bsz, 2, M_HEADS, M_HEAD_DIM, M_HEAD_DIM), F32),
                    jax.ShapeDtypeStruct((bsz, 2, M_HEADS, 1, M_HEAD_DIM), F32),
                    jax.ShapeDtypeStruct((bsz, 2, M_HEADS, 1, LANES), F32)]
    in_specs = [tok(k[1]), tok(v[1]), gate_spec, gate_spec, ar_spec, c_spec, n_spec, m_spec]
    args = [k[0], v[0], bc, ac, ar, *state]
    out_specs = [c_spec, n_spec, m_spec]
    out_shape = list(state_shapes)
    if emit_h:
        in_specs = [tok(q[1])] + in_specs
        args = [q[0]] + args
        out_specs = [pl.BlockSpec((None, None, t, M_WIDTH),
                                  lambda b_, d, j: (d, b_, j + d * (nc - 1 - 2 * j), 0))] + out_specs
        out_shape = [jax.ShapeDtypeStruct((2, bsz, L, M_WIDTH), BF16)] + out_shape
    outs = pl.pallas_call(
        functools.partial(_mlstm_kernel, emit_h=emit_h, n_chunks=nc),
        grid=(bsz, 2, nc),
        in_specs=in_specs,
        out_specs=out_specs,
        out_shape=out_shape,
        scratch_shapes=[pltpu.VMEM((M_HEADS, M_HEAD_DIM, M_HEAD_DIM), F32),
                        pltpu.VMEM((M_HEADS, 1, M_HEAD_DIM), F32),
                        pltpu.VMEM((M_HEADS, 1, LANES), F32)],
        compiler_params=_cparams("parallel", "parallel", "arbitrary"),
        name="mlstm" if emit_h else "mlstm_state",
    )(*args)
    if emit_h:
        return outs[0], tuple(outs[1:])
    return None, tuple(outs)


DFT_M_TILE = 8
DFT_C_TILE = 1024
DFT_INNER_C_TILE = 512
FEAT_ROWS = 16


def _filter_outer_kernel(bands_ref, w1t_ref, b1_ref, w2t_ref, b2_ref, w3p_ref, w3f_ref, fr_ref, dl_ref, l_ref,
                         a_ref, ss_ref, *, L, n1, n2):
    i = pl.program_id(0)
    h = n1 // 2
    cols = DFT_M_TILE * h

    def positions(shape, axis, side):
        q = lax.broadcasted_iota(jnp.int32, shape, axis)
        mm, jj = q // h, q % h
        n = n2 * (jj + side * h) + i * DFT_M_TILE + mm
        return n, jnp.where(n < L, n, 2 * L - n).astype(F32)

    taps = []
    sumsq = jnp.zeros((1, a_ref.shape[-1]), F32)
    for side, w3_ref in ((0, w3p_ref), (1, w3f_ref)):
        _, p_row = positions((1, cols), 1, side)
        t_row = p_row / float(max(L - 1, 1))
        ang = ((2 * math.pi / L) * p_row) * bands_ref[...]
        row = lax.broadcasted_iota(jnp.int32, (FEAT_ROWS, cols), 0)
        feats = jnp.concatenate([jnp.where(row == 0, t_row, 0.0), jnp.cos(ang), -jnp.sin(ang)], axis=0)
        fr = fr_ref[...]
        hid = jnp.sin(fr * (jnp.dot(w1t_ref[...], feats.astype(BF16), preferred_element_type=F32) + b1_ref[...]))
        hid = jnp.sin(fr * (jnp.dot(w2t_ref[...], hid.astype(BF16), preferred_element_type=F32) + b2_ref[...]))
        filt = lax.dot_general(hid.astype(BF16), w3_ref[...], (((0,), (0,)), ((), ())),
                               preferred_element_type=F32)
        n_col, p_col = positions((cols, 1), 0, side)
        t_col = p_col / float(max(L - 1, 1))
        kern = filt * jnp.exp(-t_col * jnp.abs(dl_ref[...]))
        kern = jnp.where(n_col == L, 0.0, kern)
        sumsq = sumsq + jnp.sum(kern * kern, axis=0, keepdims=True)
        taps.append(kern)

    for mm in range(DFT_M_TILE):
        x = jnp.concatenate([taps[0][mm * h:(mm + 1) * h], taps[1][mm * h:(mm + 1) * h]], axis=0)
        out = jnp.dot(l_ref[...], x.astype(BF16), preferred_element_type=F32)
        a_ref[0, :, mm, :] = out[:n1]
        a_ref[1, :, mm, :] = out[n1:]

    @pl.when(i == 0)
    def _():
        ss_ref[...] = jnp.zeros_like(ss_ref)

    ss_ref[...] += sumsq


def _filter_outer(L, n1, n2, fwd_r, w1, b1, w2, b2, w3, freq):
    hid = H_FILTER_HIDDEN
    bands = jnp.linspace(1e-4, H_POS_BANDS - 1, H_POS_BANDS, dtype=F32).reshape(H_POS_BANDS, 1)
    w1t = jnp.zeros((hid, 3 * FEAT_ROWS), F32)
    w1t = w1t.at[:, 0].set(w1[0]).at[:, FEAT_ROWS:2 * FEAT_ROWS].set(w1[1:1 + H_POS_BANDS].T)
    w1t = w1t.at[:, 2 * FEAT_ROWS:].set(w1[1 + H_POS_BANDS:].T).astype(BF16)
    w3h = w3.astype(BF16)
    max_decay = math.log(H_DECAY_TARGET) / H_FAST_DECAY_PCT
    min_decay = math.log(H_DECAY_TARGET) / H_SLOW_DECAY_PCT
    deltas = jnp.linspace(min_decay, max_decay, H_WIDTH, dtype=F32).reshape(1, H_WIDTH)
    col = lambda v: v.reshape(hid, 1)
    full = lambda a: pl.BlockSpec(a.shape, lambda i: (0,) * a.ndim)
    args = [bands, w1t, col(b1), w2.T.astype(BF16), col(b2)]
    return pl.pallas_call(
        functools.partial(_filter_outer_kernel, L=L, n1=n1, n2=n2),
        grid=(n2 // DFT_M_TILE,),
        in_specs=[full(a) for a in args]
        + [pl.BlockSpec((hid, H_WIDTH), lambda i: (0, 0)), pl.BlockSpec((hid, H_WIDTH), lambda i: (0, 1)),
           full(col(freq)), full(deltas), full(fwd_r)],
        out_specs=[pl.BlockSpec((2, n1, DFT_M_TILE, H_WIDTH), lambda i: (0, 0, i, 0)),
                   pl.BlockSpec((1, H_WIDTH), lambda i: (0, 0))],
        out_shape=[jax.ShapeDtypeStruct((2, n1, n2, H_WIDTH), F32),
                   jax.ShapeDtypeStruct((1, H_WIDTH), F32)],
        compiler_params=_cparams("arbitrary"),
        name="hyena_filter_outer",
    )(*args, w3h, w3h, col(freq), deltas, fwd_r)


def _dft_factors(n):
    lg = int(round(math.log2(n)))
    n1 = 1 << ((lg + 1) // 2)
    return n1, n // n1


def _dft_outer_matrices(n1):
    k = np.arange(n1)[:, None]
    n = np.arange(n1)[None, :]
    ang = 2.0 * np.pi * ((k * n) % n1) / n1
    cr, ci = np.cos(ang), -np.sin(ang)
    h = n1 // 2
    fwd_c = np.block([[cr[:, :h], -ci[:, :h]], [ci[:, :h], cr[:, :h]]])
    fwd_r = np.concatenate([cr, ci], axis=0)
    ir, ii = cr[:h, :], -ci[:h, :]
    inv = np.block([[ir, -ii], [ii, ir]])
    return (jnp.asarray(fwd_c, F32).astype(BF16), jnp.asarray(fwd_r, F32).astype(BF16),
            jnp.asarray(inv, F32).astype(BF16))


def _dft_inner_matrices(n1, n2):
    n = n1 * n2
    k2 = np.arange(n2)[:, None]
    m = np.arange(n2)[None, :]
    ang = 2.0 * np.pi * ((k2 * m) % n2) / n2
    fr, fi = np.cos(ang), -np.sin(ang)
    f = np.block([[fr, -fi], [fi, fr]])
    tw_ang = 2.0 * np.pi * ((np.arange(n1)[:, None] * np.arange(n2)[None, :]) % n) / n
    rep = lambda t: jnp.asarray(np.ascontiguousarray(np.broadcast_to(t[:, :, None], (n1, n2, LANES))), F32)
    return (jnp.asarray(f, F32).astype(BF16), jnp.asarray(f.T, F32).astype(BF16),
            rep(np.cos(tw_ang)), rep(-np.sin(tw_ang)))


def _outer_fwd_kernel(l_ref, s_ref, a_ref):
    n1 = a_ref.shape[1]
    for mm in range(s_ref.shape[1]):
        x = jnp.concatenate([s_ref[0, mm], s_ref[1, mm]], axis=0).astype(BF16)
        out = jnp.dot(l_ref[...], x, preferred_element_type=F32)
        a_ref[0, :, mm, :] = out[:n1]
        a_ref[1, :, mm, :] = out[n1:]


def _outer_fwd(lmat, s_t):
    _, n2, n1h, c = s_t.shape
    n1 = 2 * n1h
    tc = min(DFT_C_TILE, c)
    return pl.pallas_call(
        _outer_fwd_kernel,
        grid=(n2 // DFT_M_TILE, c // tc),
        in_specs=[pl.BlockSpec(lmat.shape, lambda m, j: (0, 0)),
                  pl.BlockSpec((2, DFT_M_TILE, n1h, tc), lambda m, j: (0, m, 0, j))],
        out_specs=pl.BlockSpec((2, n1, DFT_M_TILE, tc), lambda m, j: (0, 0, m, j)),
        out_shape=jax.ShapeDtypeStruct((2, n1, n2, c), F32),
        compiler_params=_cparams("parallel", "parallel"),
        name="dft_outer_fwd",
    )(lmat, s_t)


def _outer_inv_kernel(l_ref, b_ref, s_ref, x0_ref, ysc_ref, hb_ref, o_ref):
    n1h = s_ref.shape[2]
    for mm in range(b_ref.shape[1]):
        y = jnp.concatenate([b_ref[0, mm], b_ref[1, mm]], axis=0).astype(BF16)
        out = jnp.dot(l_ref[...], y, preferred_element_type=F32)
        for b in range(2):
            conv = out[b * n1h:(b + 1) * n1h]
            x0 = _unpack_bf16_pairs(x0_ref[b, mm]).astype(F32)
            hy = x0 * (conv * ysc_ref[...] + hb_ref[...] * s_ref[b, mm])
            o_ref[b, :, mm, :] = _pack_bf16_pairs(hy)


def _outer_inv(lmat, b_t, s_t, x0_t, yscale, h_bias):
    _, n2, n1, c = b_t.shape
    n1h = n1 // 2
    tc = min(DFT_C_TILE, c)
    vec = pl.BlockSpec((1, tc), lambda m, j: (0, j))
    hy = pl.pallas_call(
        _outer_inv_kernel,
        grid=(n2 // DFT_M_TILE, c // tc),
        in_specs=[pl.BlockSpec(lmat.shape, lambda m, j: (0, 0)),
                  pl.BlockSpec((2, DFT_M_TILE, n1, tc), lambda m, j: (0, m, 0, j)),
                  pl.BlockSpec((2, DFT_M_TILE, n1h, tc), lambda m, j: (0, m, 0, j)),
                  pl.BlockSpec((2, DFT_M_TILE, n1h, tc // 2), lambda m, j: (0, m, 0, j)),
                  vec, vec],
        out_specs=pl.BlockSpec((2, n1h, DFT_M_TILE, tc // 2), lambda m, j: (0, 0, m, j)),
        out_shape=jax.ShapeDtypeStruct((2, n1h, n2, c // 2), jnp.uint32),
        compiler_params=_cparams("parallel", "parallel"),
        name="dft_outer_inv",
    )(lmat, b_t, s_t, x0_t, yscale, h_bias.reshape(1, c))
    return hy.reshape(2, n1h * n2, c // 2)


DFT_K_TILE = 8


def _twiddled_inner_dft(f_ref, twr_ref, twi_ref, a_ref, kk):
    n2, c = a_ref.shape[2], a_ref.shape[3]
    twr = jnp.tile(twr_ref[kk], (1, c // LANES))
    twi = jnp.tile(twi_ref[kk], (1, c // LANES))
    ar, ai = a_ref[0, kk], a_ref[1, kk]
    a = jnp.concatenate([(ar * twr - ai * twi).astype(BF16), (ar * twi + ai * twr).astype(BF16)], axis=0)
    x = jnp.dot(f_ref[...], a, preferred_element_type=F32)
    return x[:n2], x[n2:], twr, twi


def _inner_fwd_kernel(f_ref, twr_ref, twi_ref, a_ref, o_ref):
    for kk in range(a_ref.shape[1]):
        xr, xi, _, _ = _twiddled_inner_dft(f_ref, twr_ref, twi_ref, a_ref, kk)
        o_ref[0, kk] = xr.astype(o_ref.dtype)
        o_ref[1, kk] = xi.astype(o_ref.dtype)


def _inner_specs(n1, n2, c):
    tc = min(DFT_INNER_C_TILE, c)
    kt = min(DFT_K_TILE, n1)
    blk = pl.BlockSpec((2, kt, n2, tc), lambda k, j: (0, k, 0, j))
    mat = pl.BlockSpec((2 * n2, 2 * n2), lambda k, j: (0, 0))
    tw = pl.BlockSpec((kt, n2, LANES), lambda k, j: (k, 0, 0))
    return blk, mat, tw, (n1 // kt, c // tc), kt, tc


def _inner_fwd(f, twr, twi, a):
    _, n1, n2, c = a.shape
    blk, mat, tw, grid, _, _ = _inner_specs(n1, n2, c)
    return pl.pallas_call(
        _inner_fwd_kernel,
        grid=grid,
        in_specs=[mat, tw, tw, blk],
        out_specs=blk,
        out_shape=jax.ShapeDtypeStruct((2, n1, n2, c), BF16),
        compiler_params=_cparams("parallel", "parallel"),
        name="dft_inner_filter",
    )(f, twr, twi, a)


def _inner_conv_kernel(f_ref, ft_ref, twr_ref, twi_ref, a_ref, k_ref, o_ref):
    n2 = a_ref.shape[2]
    for kk in range(a_ref.shape[1]):
        xr, xi, twr, twi = _twiddled_inner_dft(f_ref, twr_ref, twi_ref, a_ref, kk)
        kr, ki = k_ref[0, kk].astype(F32), k_ref[1, kk].astype(F32)
        yr = xr * kr - xi * ki
        yi = xr * ki + xi * kr
        y = jnp.concatenate([yr.astype(BF16), yi.astype(BF16)], axis=0)
        b = jnp.dot(ft_ref[...], y, preferred_element_type=F32)
        br, bi = b[:n2], b[n2:]
        o_ref[0, :, kk, :] = br * twr + bi * twi
        o_ref[1, :, kk, :] = bi * twr - br * twi


def _inner_conv(f, ft, twr, twi, a, kf):
    _, n1, n2, c = a.shape
    blk, mat, tw, grid, kt, tc = _inner_specs(n1, n2, c)
    return pl.pallas_call(
        _inner_conv_kernel,
        grid=grid,
        in_specs=[mat, mat, tw, tw, blk, blk],
        out_specs=pl.BlockSpec((2, n2, kt, tc), lambda k, j: (0, 0, k, j)),
        out_shape=jax.ShapeDtypeStruct((2, n2, n1, c), F32),
        compiler_params=_cparams("parallel", "parallel"),
        name="dft_inner_conv",
    )(f, ft, twr, twi, a, kf)


def _hyena_long_conv(s_t, x0_t, h_bias, w1, b1, w2, b2, w3, freq):
    bsz, n2, n1h, c = s_t.shape
    assert bsz == 2
    n1 = 2 * n1h
    L = n1h * n2
    fwd_c, fwd_r, inv = _dft_outer_matrices(n1)
    f, ft, twr, twi = _dft_inner_matrices(n1, n2)
    af, sumsq = _filter_outer(L, n1, n2, fwd_r, w1, b1, w2, b2, w3, freq)
    kf = _inner_fwd(f, twr, twi, af)
    a = _outer_fwd(fwd_c, s_t)
    b_t = _inner_conv(f, ft, twr, twi, a, kf)
    yscale = lax.rsqrt(sumsq + EPS) * (1.0 / (2 * L))
    return _outer_inv(inv, b_t, s_t, x0_t, yscale, h_bias)


def _pack_bf16_pairs(x):
    half = x.shape[1] // 2
    lo = pltpu.bitcast(x[:, :half].astype(BF16).astype(F32), jnp.uint32) >> 16
    hi = pltpu.bitcast(x[:, half:].astype(BF16).astype(F32), jnp.uint32) & jnp.uint32(0xFFFF0000)
    return lo | hi


def _unpack_bf16_pairs(p):
    lo = pltpu.bitcast(p << 16, F32).astype(BF16)
    hi = pltpu.bitcast(p & jnp.uint32(0xFFFF0000), F32).astype(BF16)
    return jnp.concatenate([lo, hi], axis=1)


def _merge_kernel(hf_ref, hb_ref, o_ref, hy_ref, ga_ref, gb_ref, x_ref,
                  gate_ref, g2_ref, sh_ref, sc_ref, wa_ref, wb_ref, wo_ref, x1_ref, h2_ref):
    a = o_ref[...].astype(F32) * (hf_ref[...].astype(F32) + hb_ref[...].astype(F32))
    half = DFT_C_TILE // 2
    hy = jnp.concatenate([_unpack_bf16_pairs(hy_ref[:, c * half:(c + 1) * half])
                          for c in range(hy_ref.shape[1] // half)], axis=1)
    pa = jnp.dot(a.astype(BF16), wa_ref[...], preferred_element_type=F32)
    pb = jnp.dot(hy, wb_ref[...], preferred_element_type=F32)
    mix = ga_ref[...].astype(F32) * pa + gb_ref[...].astype(F32) * pb
    out = jnp.dot(mix.astype(BF16), wo_ref[...], preferred_element_type=F32)
    x1 = x_ref[...] + gate_ref[...] * out
    x1_ref[...] = x1
    y = x1 * lax.rsqrt(jnp.mean(x1 * x1, axis=-1, keepdims=True) + EPS) * g2_ref[...]
    h2_ref[...] = _pack_bf16_pairs(y * (1.0 + sc_ref[...]) + sh_ref[...])


def _merge(hdirs, pm, hy, x, gate1, g2, shift2, scale2, w_a, w_b, w_out, tm=MERGE_TM):
    bsz, L, d = x.shape
    tok = pl.BlockSpec((None, tm, d), lambda b, i: (b, i, 0))

    def pm_tile(col):
        return pl.BlockSpec((None, tm, d), lambda b, i: (b, i, col))

    packed = pl.BlockSpec((None, tm, d // 2), lambda b, i: (b, i, 0))
    vec = pl.BlockSpec((1, d), lambda b, i: (0, 0))
    bvec = pl.BlockSpec((None, 1, d), lambda b, i: (b, 0, 0))
    wsp = pl.BlockSpec((d, d), lambda b, i: (0, 0), pipeline_mode=pl.Buffered(1))
    return pl.pallas_call(
        _merge_kernel,
        grid=(bsz, L // tm),
        in_specs=[pl.BlockSpec((None, None, tm, d), lambda b, i: (0, b, i, 0)),
                  pl.BlockSpec((None, None, tm, d), lambda b, i: (1, b, i, 0)),
                  pm_tile(PM_O), packed, pm_tile(PM_GA), pm_tile(PM_GB), tok,
                  bvec, vec, bvec, bvec, wsp, wsp, wsp],
        out_specs=[tok, packed],
        out_shape=[jax.ShapeDtypeStruct((bsz, L, d), F32), jax.ShapeDtypeStruct((bsz, L, d // 2), jnp.uint32)],
        compiler_params=_cparams("parallel", "parallel"),
        name="merge",
    )(hdirs, hdirs, pm, hy, pm, pm, x, gate1, g2.reshape(1, d), shift2, scale2, w_a, w_b, w_out)


MOE_BLOCK = 256
ROUTE_E1, ROUTE_E2, ROUTE_W1, ROUTE_W2 = 0, 1, 2, 3
EXP_LANE0 = N_GROUPS


def _first_lane_of_max(val, valid, lane):
    masked = jnp.where(valid, val, NEG_BIG)
    mx = jnp.max(masked, axis=1, keepdims=True)
    idx = jnp.min(jnp.where(valid & (masked == mx), lane, LANES), axis=1, keepdims=True)
    return mx, idx


MOE_TM = 1024


def _expert_onehots(rec):
    lane = lax.broadcasted_iota(jnp.int32, rec.shape, 1)
    oh1 = lane == rec[:, ROUTE_E1:ROUTE_E1 + 1].astype(jnp.int32)
    oh2 = lane == rec[:, ROUTE_E2:ROUTE_E2 + 1].astype(jnp.int32)
    return oh1, oh2


def _router_kernel(h_ref, w_ref, b_ref, r_ref, cnt_ref):
    logits = jnp.dot(_unpack_bf16_pairs(h_ref[...]), w_ref[...], preferred_element_type=F32) + b_ref[...]
    lane = lax.broadcasted_iota(jnp.int32, logits.shape, 1)
    is_g = lane < N_GROUPS
    gmax, gsel = _first_lane_of_max(logits, is_g, lane)
    gsum = jnp.sum(jnp.where(is_g, jnp.exp(logits - gmax), 0.0), axis=1, keepdims=True)
    gw = 1.0 / gsum
    lo = EXP_LANE0 + gsel * EXPERTS_PER_GROUP
    in_grp = (lane >= lo) & (lane < lo + EXPERTS_PER_GROUP)
    emax, l1 = _first_lane_of_max(logits, in_grp, lane)
    esum = jnp.sum(jnp.where(in_grp, jnp.exp(logits - emax), 0.0), axis=1, keepdims=True)
    e2max, l2 = _first_lane_of_max(logits, in_grp & (lane != l1), lane)
    v1 = 1.0 / esum
    v2 = jnp.exp(e2max - emax) / esum
    vs = v1 + v2
    w1 = gw * v1 / vs
    w2 = gw * v2 / vs
    e1 = (l1 - EXP_LANE0).astype(F32)
    e2 = (l2 - EXP_LANE0).astype(F32)
    rec = jnp.where(lane == ROUTE_E1, e1,
                    jnp.where(lane == ROUTE_E2, e2,
                              jnp.where(lane == ROUTE_W1, w1,
                                        jnp.where(lane == ROUTE_W2, w2, 0.0))))
    r_ref[...] = rec
    oh1, oh2 = _expert_onehots(rec)
    counts = jnp.sum((oh1 | oh2).astype(F32), axis=0, keepdims=True)
    cnt_ref[...] = jnp.broadcast_to(counts, cnt_ref.shape)


def _router(h2, w_group, b_group, w_router, b_router):
    n, dp = h2.shape
    d = 2 * dp
    tm = MOE_TM
    w = jnp.zeros((d, LANES), F32).at[:, :N_GROUPS].set(w_group).at[
        :, EXP_LANE0:EXP_LANE0 + N_EXPERTS].set(w_router).astype(BF16)
    b = jnp.zeros((1, LANES), F32).at[0, :N_GROUPS].set(b_group).at[
        0, EXP_LANE0:EXP_LANE0 + N_EXPERTS].set(b_router)
    return pl.pallas_call(
        _router_kernel,
        grid=(n // tm,),
        in_specs=[pl.BlockSpec((tm, dp), lambda i: (i, 0)),
                  pl.BlockSpec((d, LANES), lambda i: (0, 0)),
                  pl.BlockSpec((1, LANES), lambda i: (0, 0))],
        out_specs=[pl.BlockSpec((tm, LANES), lambda i: (i, 0)),
                   pl.BlockSpec((None, 8, LANES), lambda i: (i, 0, 0))],
        out_shape=[jax.ShapeDtypeStruct((n, LANES), F32), jax.ShapeDtypeStruct((n // tm, 8, LANES), F32)],
        compiler_params=_cparams("parallel"),
        name="moe_router",
    )(h2, w, b)


def _slots_kernel(r_ref, base_ref, dest_ref):
    rec = r_ref[...]
    tm = rec.shape[0]
    lane = lax.broadcasted_iota(jnp.int32, rec.shape, 1)
    oh1, oh2 = _expert_onehots(rec)
    r = lax.broadcasted_iota(jnp.int32, (tm, tm), 0)
    c = lax.broadcasted_iota(jnp.int32, (tm, tm), 1)
    earlier = (r > c).astype(BF16)
    rank = jnp.dot(earlier, (oh1 | oh2).astype(BF16), preferred_element_type=F32) + base_ref[0:1, :]
    d1 = jnp.sum(jnp.where(oh1, rank, 0.0), axis=1, keepdims=True)
    d2 = jnp.sum(jnp.where(oh2, rank, 0.0), axis=1, keepdims=True)
    dest = jnp.where(lane == 0, d1, jnp.where(lane == 1, d2, 0.0)).astype(jnp.int32)
    dest_ref[...] = dest.T[:8, :]


def _slots(route, tile_counts):
    n = route.shape[0]
    tm = MOE_TM
    cnt = tile_counts[:, 0, :]
    totals = jnp.sum(cnt, axis=0)
    nblk = jnp.ceil(totals * (1.0 / MOE_BLOCK))
    first_slot = (jnp.cumsum(nblk) - nblk) * float(MOE_BLOCK)
    base = first_slot[None, :] + jnp.cumsum(cnt, axis=0) - cnt
    base = jnp.broadcast_to(base[:, None, :], tile_counts.shape)
    dest = pl.pallas_call(
        _slots_kernel,
        grid=(n // tm,),
        in_specs=[pl.BlockSpec((tm, LANES), lambda i: (i, 0)),
                  pl.BlockSpec((None, 8, LANES), lambda i: (i, 0, 0))],
        out_specs=pl.BlockSpec((None, 8, tm), lambda i: (i, 0, 0)),
        out_shape=jax.ShapeDtypeStruct((n // tm, 8, tm), jnp.int32),
        compiler_params=_cparams("parallel"),
        name="moe_slots",
    )(route, base)
    return (dest[:, 0, :].reshape(n), dest[:, 1, :].reshape(n)), totals


EXPERT_STEP_BLOCKS = 4


def _experts_kernel(be_ref, first_ref, nxt_ref, par_ref, nu_ref, x_ref, w1_hbm, w3_hbm, w2_hbm, o_ref,
                    w1f, w3f, w2f, w1b, w3b, w2b, sems):
    step = pl.program_id(0)

    def weight_copies(e, slot):
        return (pltpu.make_async_copy(w1_hbm.at[e], w1f.at[slot], sems.at[0, slot]),
                pltpu.make_async_copy(w3_hbm.at[e], w3f.at[slot], sems.at[1, slot]),
                pltpu.make_async_copy(w2_hbm.at[e], w2f.at[slot], sems.at[2, slot]))

    @pl.when(step == 0)
    def _():
        for cp in weight_copies(be_ref[0], 0):
            cp.start()

    for sub in range(EXPERT_STEP_BLOCKS):
        i = step * EXPERT_STEP_BLOCKS + sub
        rows = pl.ds(sub * MOE_BLOCK, MOE_BLOCK)

        @pl.when(first_ref[i] == 1)
        def _():
            slot = par_ref[i]

            @pl.when(nxt_ref[i] >= 0)
            def _():
                for cp in weight_copies(nxt_ref[i], 1 - slot):
                    cp.start()

            for cp in weight_copies(be_ref[i], slot):
                cp.wait()
            w1b[...] = w1f[slot].astype(BF16)
            w3b[...] = w3f[slot].astype(BF16)
            w2b[...] = w2f[slot].astype(BF16)

        @pl.when(i < nu_ref[0])
        def _():
            x = _unpack_bf16_pairs(x_ref[rows, :])
            a = jnp.dot(x, w1b[...], preferred_element_type=F32)
            b = jnp.dot(x, w3b[...], preferred_element_type=F32)
            hmid = (a * jax.nn.sigmoid(a)) * b
            o_ref[rows, :] = _pack_bf16_pairs(jnp.dot(hmid.astype(BF16), w2b[...], preferred_element_type=F32))

        @pl.when(i >= nu_ref[0])
        def _():
            o_ref[rows, :] = jnp.zeros((MOE_BLOCK, o_ref.shape[1]), o_ref.dtype)


def _experts(xs, nb, block_e, n_used, w1_e, w3_e, w2_e):
    dp = xs.shape[1]
    d, de = w1_e.shape[1], w1_e.shape[2]
    idx = jnp.arange(nb, dtype=jnp.int32)
    used = idx < n_used[0]
    first = used & ((idx == 0) | (block_e != jnp.roll(block_e, 1)))
    ordinal = jnp.cumsum(first.astype(jnp.int32)) - 1
    par = (ordinal % 2).astype(jnp.int32)
    first_pos = jnp.where(first, idx, nb)
    next_first = lax.cummin(jnp.concatenate([first_pos[1:], jnp.full((1,), nb, jnp.int32)]), reverse=True)
    nxt = jnp.where(next_first < nb, block_e[jnp.minimum(next_first, nb - 1)], -1).astype(jnp.int32)
    any_spec = pl.BlockSpec(memory_space=pl.ANY)
    assert nb % EXPERT_STEP_BLOCKS == 0
    step_rows = EXPERT_STEP_BLOCKS * MOE_BLOCK
    grid_spec = pltpu.PrefetchScalarGridSpec(
        num_scalar_prefetch=5,
        grid=(nb // EXPERT_STEP_BLOCKS,),
        in_specs=[pl.BlockSpec((step_rows, dp), lambda i, *_: (i, 0)), any_spec, any_spec, any_spec],
        out_specs=pl.BlockSpec((step_rows, dp), lambda i, *_: (i, 0)),
        scratch_shapes=[pltpu.VMEM((2, d, de), F32), pltpu.VMEM((2, d, de), F32), pltpu.VMEM((2, de, d), F32),
                        pltpu.VMEM((d, de), BF16), pltpu.VMEM((d, de), BF16), pltpu.VMEM((de, d), BF16),
                        pltpu.SemaphoreType.DMA((3, 2))],
    )
    return pl.pallas_call(
        _experts_kernel,
        grid_spec=grid_spec,
        out_shape=jax.ShapeDtypeStruct((nb * MOE_BLOCK, dp), xs.dtype),
        compiler_params=_cparams("arbitrary"),
        name="moe_experts",
    )(block_e, first.astype(jnp.int32), nxt, par, n_used, xs, w1_e, w3_e, w2_e)


SC_WINDOW = 128
SC_CORES, SC_SUBCORES = 2, 16
SC_WORKERS = SC_CORES * SC_SUBCORES


def _sc_worker_id():
    return lax.axis_index("c") * SC_SUBCORES + lax.axis_index("s")


def _sc_mesh():
    return plsc.VectorSubcoreMesh(core_axis_name="c", subcore_axis_name="s")


def _sc_dispatch(rows, dest0, dest1, pad_slots, n_rows):
    n, dv = rows.shape
    nwin, pwin = n // SC_WINDOW, pad_slots.shape[0] // SC_WINDOW
    assert n % (SC_WINDOW * SC_WORKERS) == 0 and pad_slots.shape[0] % (SC_WINDOW * SC_WORKERS) == 0
    zeros = jnp.zeros((SC_WINDOW, dv), rows.dtype)

    @pl.kernel(out_type=jax.ShapeDtypeStruct((n_rows, dv), rows.dtype), mesh=_sc_mesh(),
               scratch_types=[pltpu.VMEM((1, SC_WINDOW), jnp.int32), pltpu.VMEM((SC_WINDOW, dv), rows.dtype)],
               name="moe_dispatch_sc")
    def scatter(x_hbm, d0_hbm, d1_hbm, p_hbm, z_hbm, o_hbm, idx, buf):
        wid = _sc_worker_id()
        pltpu.sync_copy(z_hbm, buf)

        @pl.loop(0, pwin // SC_WORKERS)
        def _(t):
            w = t * SC_WORKERS + wid
            pltpu.sync_copy(p_hbm.at[pl.ds(w, 1)], idx)
            pltpu.sync_copy(buf, o_hbm.at[idx.at[0]])

        @pl.loop(0, nwin // SC_WORKERS)
        def _(t):
            w = t * SC_WORKERS + wid
            pltpu.sync_copy(x_hbm.at[pl.ds(w * SC_WINDOW, SC_WINDOW)], buf)
            for d_hbm in (d0_hbm, d1_hbm):
                pltpu.sync_copy(d_hbm.at[pl.ds(w, 1)], idx)
                pltpu.sync_copy(buf, o_hbm.at[idx.at[0]])

    return scatter(rows, dest0.reshape(nwin, SC_WINDOW), dest1.reshape(nwin, SC_WINDOW),
                   pad_slots.reshape(pwin, SC_WINDOW), zeros)


def _sc_gather(table, index):
    m = index.shape[0]
    dv = table.shape[1]
    nwin = m // SC_WINDOW
    assert m % (SC_WINDOW * SC_WORKERS) == 0

    @pl.kernel(out_type=jax.ShapeDtypeStruct((m, dv), table.dtype), mesh=_sc_mesh(),
               scratch_types=[pltpu.VMEM((1, SC_WINDOW), jnp.int32), pltpu.VMEM((SC_WINDOW, dv), table.dtype)],
               name="moe_gather_sc")
    def gather(x_hbm, i_hbm, o_hbm, idx, buf):
        wid = _sc_worker_id()

        @pl.loop(0, nwin // SC_WORKERS)
        def _(t):
            w = t * SC_WORKERS + wid
            pltpu.sync_copy(i_hbm.at[pl.ds(w, 1)], idx)
            pltpu.sync_copy(x_hbm.at[idx.at[0]], buf)
            pltpu.sync_copy(buf, o_hbm.at[pl.ds(w * SC_WINDOW, SC_WINDOW)])

    return gather(table, index.reshape(nwin, SC_WINDOW))


def _combine_planes_kernel(r_ref, ya_ref, yb_ref, x_ref, gate_ref, gf_ref, o_ref):
    rec = r_ref[...]
    y = (_unpack_bf16_pairs(ya_ref[...]).astype(F32) * rec[:, ROUTE_W1:ROUTE_W1 + 1]
         + _unpack_bf16_pairs(yb_ref[...]).astype(F32) * rec[:, ROUTE_W2:ROUTE_W2 + 1])
    x2 = x_ref[...] + gate_ref[...] * y
    o_ref[...] = x2 * lax.rsqrt(jnp.mean(x2 * x2, axis=-1, keepdims=True) + EPS) * gf_ref[...]


def _combine_planes(g, route, x1, gate2, g_final, tm=COMBINE_TM):
    bsz, L, d = x1.shape
    tpb = L // tm
    dp = g.shape[-1]
    return pl.pallas_call(
        _combine_planes_kernel,
        grid=(bsz, tpb),
        in_specs=[pl.BlockSpec((tm, LANES), lambda b, i: (b * tpb + i, 0)),
                  pl.BlockSpec((None, tm, dp), lambda b, i: (0, b * tpb + i, 0)),
                  pl.BlockSpec((None, tm, dp), lambda b, i: (1, b * tpb + i, 0)),
                  pl.BlockSpec((None, tm, d), lambda b, i: (b, i, 0)),
                  pl.BlockSpec((None, 1, d), lambda b, i: (b, 0, 0)),
                  pl.BlockSpec((1, d), lambda b, i: (0, 0))],
        out_specs=pl.BlockSpec((None, tm, d), lambda b, i: (b, i, 0)),
        out_shape=jax.ShapeDtypeStruct((bsz, L, d), F32),
        compiler_params=_cparams("parallel", "parallel"),
        name="moe_combine",
    )(route, g, g, x1, gate2, g_final.reshape(1, d))


def _moe(h2, x1, gate2, g_final, w_group, b_group, w_router, b_router, w1_e, w3_e, w2_e):
    bsz, L, d = x1.shape
    n = bsz * L
    h2f = h2.reshape(n, h2.shape[-1])
    route, tile_counts = _router(h2f, w_group, b_group, w_router, b_router)
    (dest0, dest1), counts = _slots(route, tile_counts)
    nb = (2 * n) // MOE_BLOCK + N_EXPERTS
    cnt = counts[:N_EXPERTS].astype(jnp.int32)
    blocks_per_e = (cnt + MOE_BLOCK - 1) // MOE_BLOCK
    ends = jnp.cumsum(blocks_per_e)
    block_e = jnp.minimum(jnp.sum(ends[None, :] <= jnp.arange(nb, dtype=jnp.int32)[:, None], axis=1),
                          N_EXPERTS - 1).astype(jnp.int32)
    n_used = ends[-1:].astype(jnp.int32)
    n_slots = nb * MOE_BLOCK
    pad_j = jnp.arange(MOE_BLOCK, dtype=jnp.int32)[None, :]
    spare = n_slots + jnp.arange(N_EXPERTS * MOE_BLOCK, dtype=jnp.int32).reshape(N_EXPERTS, MOE_BLOCK)
    first_slot = ((ends - blocks_per_e) * MOE_BLOCK)[:, None]
    is_pad = cnt[:, None] + pad_j < blocks_per_e[:, None] * MOE_BLOCK
    pad_slots = jnp.where(is_pad, first_slot + cnt[:, None] + pad_j, spare).reshape(-1)
    xs = _sc_dispatch(h2f, dest0, dest1, pad_slots, n_slots + N_EXPERTS * MOE_BLOCK)
    ys = _experts(xs, nb, block_e, n_used, w1_e, w3_e, w2_e)
    g = _sc_gather(ys, jnp.concatenate([dest0, dest1]))
    return _combine_planes(g.reshape(2, n, g.shape[-1]), route, x1, gate2, g_final)


def kernel(x, c, ctx, c_ctx, w_mod, b_mod, g_norm1, g_norm2, w_in, b_in, w_qk_conv, b_qk_conv,
           w_h_conv, b_h_conv, hf_w1, hf_b1, hf_w2, hf_b2, hf_w3, hf_freq, h_bias, w_a, w_b, w_out,
           w_group, b_group, w_router, b_router, w1_e, w3_e, w2_e, g_final):
    assert w_mod.shape[0] == 1, "single-layer block"
    (w_mod, b_mod, g_norm1, g_norm2, w_in, b_in, w_qk_conv, b_qk_conv, w_h_conv, b_h_conv, hf_w1, hf_b1, hf_w2,
     hf_b2, hf_w3, hf_freq, h_bias, w_a, w_b, w_out, w_group, b_group, w_router, b_router, w1_e, w3_e, w2_e) = (
        t[0] for t in (w_mod, b_mod, g_norm1, g_norm2, w_in, b_in, w_qk_conv, b_qk_conv, w_h_conv, b_h_conv,
                       hf_w1, hf_b1, hf_w2, hf_b2, hf_w3, hf_freq, h_bias, w_a, w_b, w_out, w_group, b_group,
                       w_router, b_router, w1_e, w3_e, w2_e))
    bsz, L, d = x.shape
    lc = ctx.shape[1]
    seg = L // (L // GRID_W)
    chunk_c = min(lc, MLSTM_CHUNK)
    assert bsz + 1 <= 8 and lc % chunk_c == 0 and L % MLSTM_CHUNK == 0

    cond = jnp.zeros((8, d), F32).at[:bsz].set(c).at[bsz].set(c_ctx)
    mod = _adaln(cond, w_mod, b_mod).reshape(8, 6, d)
    modx = mod[:bsz]
    shift1, scale1, gate1, shift2, scale2, gate2 = (modx[:, i:i + 1] for i in range(6))
    shift1c = jnp.broadcast_to(mod[bsz, 0].reshape(1, 1, d), (bsz, 1, d))
    scale1c = jnp.broadcast_to(mod[bsz, 1].reshape(1, 1, d), (bsz, 1, d))

    k_scale = jnp.full((M_WIDTH,), M_HEAD_DIM ** -0.5, F32)
    qk_scale = jnp.concatenate([jnp.ones((M_WIDTH,), F32), k_scale])
    w_gates, b_gates = w_in[:, IG0:M_COLS], b_in[IG0:M_COLS]
    w_main, w_hyena = _weight_prep(jnp.swapaxes(w_in, 0, 1))
    b_main = jnp.concatenate([b_in[Q0:IG0], b_in[GA0:IN_COLS]])

    hc = _norm_mod(ctx, g_norm1, shift1c, scale1c, lc)
    kc = _proj_conv_silu(hc, w_main[:, K0 - Q0:V0 - Q0], b_in[K0:V0], w_qk_conv[:, M_WIDTH:],
                         b_qk_conv[M_WIDTH:], k_scale, lc, lc)
    vc = _proj_act(hc, w_main[:, V0 - Q0:O0 - Q0], b_in[V0:O0], "none", BF16, lc)
    bcc, acc, arc = _gates(hc, w_gates, b_gates, chunk_c)
    zero_state = (jnp.zeros((bsz, 2, M_HEADS, M_HEAD_DIM, M_HEAD_DIM), F32),
                  jnp.zeros((bsz, 2, M_HEADS, 1, M_HEAD_DIM), F32),
                  jnp.zeros((bsz, 2, M_HEADS, 1, LANES), F32))
    _, ctx_state = _mlstm(None, (kc, 0), (vc, 0), bcc, acc, arc, zero_state, False, chunk_c)

    tm = ROW_TILE
    _, dft_fast = _dft_factors(2 * L)
    pm, h, h_il = _proj_main(x, g_norm1, shift1, scale1, w_main, b_main, w_qk_conv, b_qk_conv, qk_scale,
                             seg, tm, dft_fast)
    bc, ac, ar = _gates(h, w_gates, b_gates, MLSTM_CHUNK)
    hdirs, _ = _mlstm((pm, PM_Q), (pm, PM_K), (pm, PM_V), bc, ac, ar, ctx_state, True, MLSTM_CHUNK)

    x0_t, s_t = _proj_hyena(h_il, w_hyena, b_in[HY0:GA0], w_h_conv, b_h_conv, seg)
    hy = _hyena_long_conv(s_t, x0_t, h_bias, hf_w1, hf_b1, hf_w2, hf_b2, hf_w3, hf_freq)

    x1, h2 = _merge(hdirs, pm, hy, x, gate1, g_norm2, shift2, scale2,
                    w_a.astype(BF16), w_b.astype(BF16), w_out.astype(BF16))
    return _moe(h2, x1, gate2, g_final, w_group, b_group, w_router, b_router, w1_e, w3_e, w2_e)
```
